```python
import math
import jax, jax.numpy as jnp
from jax import lax
import numpy as np

D_MODEL = 1024
BATCH = 8
SEQ = 4096
DEPTH = 1

N_META = 16
SSD_HEAD_DIM = 64
SSD_INNER = D_MODEL
SSD_HEADS = SSD_INNER // SSD_HEAD_DIM
SSD_GROUPS = 2
SSD_STATE = 128
SSD_CONV = 4
CHUNK = 128
SB_HEAD_DIM = 64
SB_WIDTH = D_MODEL
SB_HEADS = SB_WIDTH // SB_HEAD_DIM
Q_BLOCK = 128
MIX_WIDTH = SSD_INNER + SB_WIDTH
D_FF = 256 * ((8 * D_MODEL // 3 + 255) // 256)
FFN_CONV = 3
EPS = 1e-6

XBC_WIDTH = SSD_INNER + 2 * SSD_GROUPS * SSD_STATE
OFF_Z = 0
OFF_XBC = OFF_Z + SSD_INNER
OFF_DT = OFF_XBC + XBC_WIDTH
OFF_Q = OFF_DT + SSD_HEADS
OFF_K = OFF_Q + SB_WIDTH
OFF_V = OFF_K + SB_WIDTH
IN_COLS = OFF_V + SB_WIDTH

kernel_name = "hymba_ssd_stickbreaking_convffn_layer"


def rms_norm(x, g):
    x32 = x.astype(jnp.float32)
    y = x32 * lax.rsqrt(jnp.mean(x32 * x32, axis=-1, keepdims=True) + EPS)
    return (y * g.astype(jnp.float32)).astype(x.dtype)


def causal_dwconv(x, w, b):
    K = w.shape[0]
    L = x.shape[1]
    xp = jnp.pad(x, ((0, 0), (K - 1, 0), (0, 0)))
    y = b
    for k in range(K):
        y = y + xp[:, k:k + L] * w[k]
    return y


def ssd_mixer(z, xbc, dt_raw, conv_w, conv_b, dt_bias, a_log, d_skip, norm_g):
    out_dtype = z.dtype
    Bsz, L, _ = xbc.shape
    H, P, G, N = SSD_HEADS, SSD_HEAD_DIM, SSD_GROUPS, SSD_STATE
    J = H // G
    f32 = jnp.float32
    xbc = jax.nn.silu(causal_dwconv(xbc, conv_w, conv_b)).astype(f32)
    xs = xbc[..., :SSD_INNER].reshape(Bsz, L, H, P)
    Bm = xbc[..., SSD_INNER:SSD_INNER + G * N].reshape(Bsz, L, G, N)
    Cm = xbc[..., SSD_INNER + G * N:].reshape(Bsz, L, G, N)
    dt = jax.nn.softplus(dt_raw.astype(f32) + dt_bias.astype(f32))
    A = -jnp.exp(a_log.astype(f32))

    pad = CHUNK - N_META
    Lp = L + pad
    nc = Lp // CHUNK

    def front_pad(t):
        return jnp.pad(t, ((0, 0), (pad, 0)) + ((0, 0),) * (t.ndim - 2))

    Xdt = front_pad(xs * dt[..., None]).reshape(Bsz, nc, CHUNK, G, J, P)
    Adt = front_pad(dt * A).reshape(Bsz, nc, CHUNK, G, J).transpose(0, 3, 4, 1, 2)
    Bc = front_pad(Bm).reshape(Bsz, nc, CHUNK, G, N)
    Cc = front_pad(Cm).reshape(Bsz, nc, CHUNK, G, N)

    Acs = jnp.cumsum(Adt, axis=-1)
    causal = jnp.tril(jnp.ones((CHUNK, CHUNK), dtype=bool))
    seg = Acs[..., :, None] - Acs[..., None, :]
    Ldec = jnp.where(causal, jnp.exp(jnp.where(causal, seg, 0.0)), 0.0)

    CB = jnp.einsum('bclgn,bcsgn->bgcls', Cc, Bc)
    y_diag = jnp.einsum('bgcls,bgjcls,bcsgjp->bclgjp', CB, Ldec, Xdt)

    decay_states = jnp.exp(Acs[..., -1:] - Acs)
    states = jnp.einsum('bclgn,bgjcl,bclgjp->bcgjpn', Bc, decay_states, Xdt)
    chunk_decay = jnp.exp(Acs[..., -1])

    def step(carry, inp):
        st, dec = inp
        return carry * dec[..., None, None] + st, carry

    init = jnp.zeros((Bsz, G, J, P, N), f32)
    _, prev = lax.scan(step, init, (states.transpose(1, 0, 2, 3, 4, 5),
                                    chunk_decay.transpose(3, 0, 1, 2)))
    prev = prev.transpose(1, 0, 2, 3, 4, 5)

    y_off = jnp.einsum('bclgn,bcgjpn,bgjcl->bclgjp', Cc, prev, jnp.exp(Acs))

    y = (y_diag + y_off).reshape(Bsz, Lp, H, P)[:, pad:]
    y = y + xs * d_skip.astype(f32)[:, None]
    y = y.reshape(Bsz, L, SSD_INNER) * jax.nn.silu(z.astype(f32))
    return rms_norm(y, norm_g).astype(out_dtype)


def stick_breaking_attention(q, k, v):
    out_dtype = q.dtype
    Bsz, L, H, D = q.shape
    f32 = jnp.float32
    scale = 1.0 / math.sqrt(D)
    pad = Q_BLOCK - N_META
    Lp = L + pad
    nb = Lp // Q_BLOCK
    padw = ((0, 0), (pad, 0), (0, 0), (0, 0))
    qp = jnp.pad(q.astype(f32), padw)
    kp = jnp.pad(k.astype(f32), padw)
    vp = jnp.pad(v.astype(f32), padw)
    qb = qp.reshape(Bsz, nb, Q_BLOCK, H, D).transpose(1, 0, 2, 3, 4)
    key_pos = jnp.arange(Lp)

    def block(args):
        i, qi = args
        q_pos = i * Q_BLOCK + jnp.arange(Q_BLOCK)
        valid = (key_pos[None, :] < q_pos[:, None]) & (key_pos[None, :] >= pad)
        zlog = jnp.einsum('bqhd,bkhd->bhqk', qi, kp) * scale
        log_beta = jax.nn.log_sigmoid(zlog)
        log_keep = jnp.where(valid, log_beta - zlog, 0.0)
        after = lax.cumsum(log_keep, axis=3, reverse=True) - log_keep
        w = jnp.where(valid, jnp.exp(log_beta + after), 0.0)
        return jnp.einsum('bhqk,bkhd->bqhd', w, vp)

    o = lax.map(block, (jnp.arange(nb), qb))
    o = o.transpose(1, 0, 2, 3, 4).reshape(Bsz, Lp, H, D)[:, pad:]
    return o.astype(out_dtype)


def _fwd_setup_inputs(seed: int = 0) -> dict:
    key = jax.random.key(seed)
    ks = jax.random.split(key, 20)
    f32 = jnp.float32

    def gain(k, n):
        return 1.0 + 0.05 * jax.random.normal(k, (DEPTH, n), f32)

    dt0 = jnp.exp(jax.random.uniform(ks[5], (DEPTH, SSD_HEADS), f32,
                                     math.log(1e-3), math.log(1e-1)))
    dt_bias = dt0 + jnp.log(-jnp.expm1(-dt0))
    return {
        "x": jax.random.normal(ks[0], (BATCH, SEQ, D_MODEL), f32),
        "meta_tokens": jax.random.normal(ks[1], (N_META, D_MODEL), f32),
        "mix_pre_g": gain(ks[2], D_MODEL),
        "w_in": jax.random.normal(ks[3], (DEPTH, D_MODEL, IN_COLS), f32) * D_MODEL ** -0.5,
        "ssd_conv_w": jax.random.normal(ks[4], (DEPTH, SSD_CONV, XBC_WIDTH), f32) * SSD_CONV ** -0.5,
        "ssd_conv_b": 0.01 * jax.random.normal(ks[6], (DEPTH, XBC_WIDTH), f32),
        "ssd_dt_bias": dt_bias,
        "ssd_a_log": jnp.log(jax.random.uniform(ks[7], (DEPTH, SSD_HEADS), f32, 1.0, 16.0)),
        "ssd_d": 1.0 + 0.1 * jax.random.normal(ks[8], (DEPTH, SSD_HEADS), f32),
        "ssd_norm_g": gain(ks[9], SSD_INNER),
        "sb_norm_g": gain(ks[10], SB_WIDTH),
        "w_out": jax.random.normal(ks[11], (DEPTH, MIX_WIDTH, D_MODEL), f32) * MIX_WIDTH ** -0.5,
        "mix_post_g": gain(ks[12], D_MODEL),
        "ffn_pre_g": gain(ks[13], D_MODEL),
        "w_up": jax.random.normal(ks[14], (DEPTH, D_MODEL, 2 * D_FF), f32) * D_MODEL ** -0.5,
        "ffn_conv_w": jax.random.normal(ks[15], (DEPTH, FFN_CONV, D_FF), f32) * FFN_CONV ** -0.5,
        "ffn_conv_b": 0.01 * jax.random.normal(ks[16], (DEPTH, D_FF), f32),
        "w_down": jax.random.normal(ks[17], (DEPTH, D_FF, D_MODEL), f32) * D_FF ** -0.5,
        "ffn_post_g": gain(ks[18], D_MODEL),
    }


def _fwd_reference(x, meta_tokens, mix_pre_g, w_in, ssd_conv_w, ssd_conv_b, ssd_dt_bias,
              ssd_a_log, ssd_d, ssd_norm_g, sb_norm_g, w_out, mix_post_g, ffn_pre_g,
              w_up, ffn_conv_w, ffn_conv_b, w_down, ffn_post_g):
    Bsz = x.shape[0]
    meta = jnp.broadcast_to(meta_tokens.astype(x.dtype)[None], (Bsz, N_META, x.shape[-1]))
    h = jnp.concatenate([meta, x], axis=1)
    L = h.shape[1]
    for l in range(DEPTH):
        xn = rms_norm(h, mix_pre_g[l])
        proj = xn @ w_in[l]
        z = proj[..., OFF_Z:OFF_XBC]
        xbc = proj[..., OFF_XBC:OFF_DT]
        dt_raw = proj[..., OFF_DT:OFF_Q]
        q = proj[..., OFF_Q:OFF_K].reshape(Bsz, L, SB_HEADS, SB_HEAD_DIM)
        k = proj[..., OFF_K:OFF_V].reshape(Bsz, L, SB_HEADS, SB_HEAD_DIM)
        v = proj[..., OFF_V:IN_COLS].reshape(Bsz, L, SB_HEADS, SB_HEAD_DIM)
        y_ssd = ssd_mixer(z, xbc, dt_raw, ssd_conv_w[l], ssd_conv_b[l], ssd_dt_bias[l],
                          ssd_a_log[l], ssd_d[l], ssd_norm_g[l])
        y_sb = rms_norm(stick_breaking_attention(q, k, v).reshape(Bsz, L, SB_WIDTH), sb_norm_g[l])
        mix = jnp.concatenate([y_ssd, y_sb], axis=-1) @ w_out[l]
        h = h + rms_norm(mix, mix_post_g[l])
        xn = rms_norm(h, ffn_pre_g[l])
        gu = xn @ w_up[l]
        g = causal_dwconv(gu[..., :D_FF], ffn_conv_w[l], ffn_conv_b[l])
        f = (jax.nn.gelu(g, approximate=True) * gu[..., D_FF:]) @ w_down[l]
        h = h + rms_norm(f, ffn_post_g[l])
    return h[:, N_META:]


import jax as _jax
import jax.numpy as _jnp

TWIN_FORMAT = 'train_step'
FWD_PARAMS = ['x', 'meta_tokens', 'mix_pre_g', 'w_in', 'ssd_conv_w', 'ssd_conv_b', 'ssd_dt_bias', 'ssd_a_log', 'ssd_d', 'ssd_norm_g', 'sb_norm_g', 'w_out', 'mix_post_g', 'ffn_pre_g', 'w_up', 'ffn_conv_w', 'ffn_conv_b', 'w_down', 'ffn_post_g']
TWIN_WEIGHTS = ['meta_tokens', 'mix_pre_g', 'w_in', 'ssd_conv_w', 'ssd_conv_b', 'ssd_dt_bias', 'ssd_a_log', 'ssd_d', 'ssd_norm_g', 'sb_norm_g', 'w_out', 'mix_post_g', 'ffn_pre_g', 'w_up', 'ffn_conv_w', 'ffn_conv_b', 'w_down', 'ffn_post_g']
TWIN_DIFF_INPUT = 'x'
TWIN_INPUTS = ['x', 'meta_tokens', 'mix_pre_g', 'w_in', 'ssd_conv_w', 'ssd_conv_b', 'ssd_dt_bias', 'ssd_a_log', 'ssd_d', 'ssd_norm_g', 'sb_norm_g', 'w_out', 'mix_post_g', 'ffn_pre_g', 'w_up', 'ffn_conv_w', 'ffn_conv_b', 'w_down', 'ffn_post_g', 'loss_target', 'm_meta_tokens', 'm_mix_pre_g', 'm_w_in', 'm_ssd_conv_w', 'm_ssd_conv_b', 'm_ssd_dt_bias', 'm_ssd_a_log', 'm_ssd_d', 'm_ssd_norm_g', 'm_sb_norm_g', 'm_w_out', 'm_mix_post_g', 'm_ffn_pre_g', 'm_w_up', 'm_ffn_conv_w', 'm_ffn_conv_b', 'm_w_down', 'm_ffn_post_g', 'v_meta_tokens', 'v_mix_pre_g', 'v_w_in', 'v_ssd_conv_w', 'v_ssd_conv_b', 'v_ssd_dt_bias', 'v_ssd_a_log', 'v_ssd_d', 'v_ssd_norm_g', 'v_sb_norm_g', 'v_w_out', 'v_mix_post_g', 'v_ffn_pre_g', 'v_w_up', 'v_ffn_conv_w', 'v_ffn_conv_b', 'v_w_down', 'v_ffn_post_g']
TWIN_OUTPUTS = ['loss', 'grad_x', 'grad_meta_tokens', 'grad_mix_pre_g', 'grad_w_in', 'grad_ssd_conv_w', 'grad_ssd_conv_b', 'grad_ssd_dt_bias', 'grad_ssd_a_log', 'grad_ssd_d', 'grad_ssd_norm_g', 'grad_sb_norm_g', 'grad_w_out', 'grad_mix_post_g', 'grad_ffn_pre_g', 'grad_w_up', 'grad_ffn_conv_w', 'grad_ffn_conv_b', 'grad_w_down', 'grad_ffn_post_g', 'delta_meta_tokens', 'delta_mix_pre_g', 'delta_w_in', 'delta_ssd_conv_w', 'delta_ssd_conv_b', 'delta_ssd_dt_bias', 'delta_ssd_a_log', 'delta_ssd_d', 'delta_ssd_norm_g', 'delta_sb_norm_g', 'delta_w_out', 'delta_mix_post_g', 'delta_ffn_pre_g', 'delta_w_up', 'delta_ffn_conv_w', 'delta_ffn_conv_b', 'delta_w_down', 'delta_ffn_post_g', 'new_m_meta_tokens', 'new_m_mix_pre_g', 'new_m_w_in', 'new_m_ssd_conv_w', 'new_m_ssd_conv_b', 'new_m_ssd_dt_bias', 'new_m_ssd_a_log', 'new_m_ssd_d', 'new_m_ssd_norm_g', 'new_m_sb_norm_g', 'new_m_w_out', 'new_m_mix_post_g', 'new_m_ffn_pre_g', 'new_m_w_up', 'new_m_ffn_conv_w', 'new_m_ffn_conv_b', 'new_m_w_down', 'new_m_ffn_post_g', 'new_v_meta_tokens', 'new_v_mix_pre_g', 'new_v_w_in', 'new_v_ssd_conv_w', 'new_v_ssd_conv_b', 'new_v_ssd_dt_bias', 'new_v_ssd_a_log', 'new_v_ssd_d', 'new_v_ssd_norm_g', 'new_v_sb_norm_g', 'new_v_w_out', 'new_v_mix_post_g', 'new_v_ffn_pre_g', 'new_v_w_up', 'new_v_ffn_conv_w', 'new_v_ffn_conv_b', 'new_v_w_down', 'new_v_ffn_post_g']
TWIN_LEAF_KINDS = {'loss': 'loss', 'grad_x': 'grad_x', 'grad_meta_tokens': 'grad_w', 'grad_mix_pre_g': 'grad_w', 'grad_w_in': 'grad_w', 'grad_ssd_conv_w': 'grad_w', 'grad_ssd_conv_b': 'grad_w', 'grad_ssd_dt_bias': 'grad_w', 'grad_ssd_a_log': 'grad_w', 'grad_ssd_d': 'grad_w', 'grad_ssd_norm_g': 'grad_w', 'grad_sb_norm_g': 'grad_w', 'grad_w_out': 'grad_w', 'grad_mix_post_g': 'grad_w', 'grad_ffn_pre_g': 'grad_w', 'grad_w_up': 'grad_w', 'grad_ffn_conv_w': 'grad_w', 'grad_ffn_conv_b': 'grad_w', 'grad_w_down': 'grad_w', 'grad_ffn_post_g': 'grad_w', 'delta_meta_tokens': 'delta_w', 'delta_mix_pre_g': 'delta_w', 'delta_w_in': 'delta_w', 'delta_ssd_conv_w': 'delta_w', 'delta_ssd_conv_b': 'delta_w', 'delta_ssd_dt_bias': 'delta_w', 'delta_ssd_a_log': 'delta_w', 'delta_ssd_d': 'delta_w', 'delta_ssd_norm_g': 'delta_w', 'delta_sb_norm_g': 'delta_w', 'delta_w_out': 'delta_w', 'delta_mix_post_g': 'delta_w', 'delta_ffn_pre_g': 'delta_w', 'delta_w_up': 'delta_w', 'delta_ffn_conv_w': 'delta_w', 'delta_ffn_conv_b': 'delta_w', 'delta_w_down': 'delta_w', 'delta_ffn_post_g': 'delta_w', 'new_m_meta_tokens': 'new_m', 'new_m_mix_pre_g': 'new_m', 'new_m_w_in': 'new_m', 'new_m_ssd_conv_w': 'new_m', 'new_m_ssd_conv_b': 'new_m', 'new_m_ssd_dt_bias': 'new_m', 'new_m_ssd_a_log': 'new_m', 'new_m_ssd_d': 'new_m', 'new_m_ssd_norm_g': 'new_m', 'new_m_sb_norm_g': 'new_m', 'new_m_w_out': 'new_m', 'new_m_mix_post_g': 'new_m', 'new_m_ffn_pre_g': 'new_m', 'new_m_w_up': 'new_m', 'new_m_ffn_conv_w': 'new_m', 'new_m_ffn_conv_b': 'new_m', 'new_m_w_down': 'new_m', 'new_m_ffn_post_g': 'new_m', 'new_v_meta_tokens': 'new_v', 'new_v_mix_pre_g': 'new_v', 'new_v_w_in': 'new_v', 'new_v_ssd_conv_w': 'new_v', 'new_v_ssd_conv_b': 'new_v', 'new_v_ssd_dt_bias': 'new_v', 'new_v_ssd_a_log': 'new_v', 'new_v_ssd_d': 'new_v', 'new_v_ssd_norm_g': 'new_v', 'new_v_sb_norm_g': 'new_v', 'new_v_w_out': 'new_v', 'new_v_mix_post_g': 'new_v', 'new_v_ffn_pre_g': 'new_v', 'new_v_w_up': 'new_v', 'new_v_ffn_conv_w': 'new_v', 'new_v_ffn_conv_b': 'new_v', 'new_v_w_down': 'new_v', 'new_v_ffn_post_g': 'new_v'}


def _forward(args):
    return _fwd_reference(*[args[k] for k in FWD_PARAMS])


def _output_shape():
    out = _jax.eval_shape(lambda: _forward(_fwd_setup_inputs(0)))
    return out.shape, out.dtype

N_MICROBATCH = 1
ADAM_LR = 0.001
ADAM_B1 = 0.9
ADAM_B2 = 0.999
ADAM_EPS = 1e-08
ADAM_WD = 0.01
ADAM_STEP = 10
PER_EXAMPLE_BATCH_AXIS = {'x': 0, 'loss_target': 0}
SHARED_INPUTS = []
_WEIGHT_DTYPES = {'meta_tokens': _jnp.float32, 'mix_pre_g': _jnp.float32, 'w_in': _jnp.float32, 'ssd_conv_w': _jnp.float32, 'ssd_conv_b': _jnp.float32, 'ssd_dt_bias': _jnp.float32, 'ssd_a_log': _jnp.float32, 'ssd_d': _jnp.float32, 'ssd_norm_g': _jnp.float32, 'sb_norm_g': _jnp.float32, 'w_out': _jnp.float32, 'mix_post_g': _jnp.float32, 'ffn_pre_g': _jnp.float32, 'w_up': _jnp.float32, 'ffn_conv_w': _jnp.float32, 'ffn_conv_b': _jnp.float32, 'w_down': _jnp.float32, 'ffn_post_g': _jnp.float32}
MOMENT_SCALE = {'meta_tokens': 1.347042e-02, 'mix_pre_g': 6.182420e-01, 'w_in': 2.765453e-01, 'ssd_conv_w': 3.404641e-01, 'ssd_conv_b': 7.526498e-01, 'ssd_dt_bias': 9.143069e-01, 'ssd_a_log': 1.973229e+00, 'ssd_d': 2.221781e+00, 'ssd_norm_g': 5.645338e-01, 'sb_norm_g': 4.005132e-01, 'w_out': 5.997283e-01, 'mix_post_g': 3.220777e+01, 'ffn_pre_g': 6.241228e-01, 'w_up': 2.757242e-01, 'ffn_conv_w': 2.871262e-01, 'ffn_conv_b': 6.123065e-01, 'w_down': 5.790915e-01, 'ffn_post_g': 3.201562e+01}


def _to_microbatches(a, axis):
    t = _jnp.moveaxis(a, axis, 0)
    t = t.reshape((N_MICROBATCH, t.shape[0] // N_MICROBATCH) + t.shape[1:])
    return _jnp.moveaxis(t, 1, axis + 1)


def setup_inputs(seed: int = 0) -> dict:
    inp = _fwd_setup_inputs(seed)
    key = _jax.random.fold_in(_jax.random.key(seed), 7919)
    shape, _ = _output_shape()
    out = dict(inp)
    out["loss_target"] = _jax.random.normal(_jax.random.fold_in(key, 0), shape, _jnp.float32)
    for i, name in enumerate(TWIN_WEIGHTS):
        w = inp[name].astype(_jnp.float32)
        if MOMENT_SCALE is None:
            s = _jnp.sqrt(_jnp.mean(_jnp.square(w)) + 1e-30)
        else:
            s = MOMENT_SCALE[name]
        km, kv = _jax.random.split(_jax.random.fold_in(key, i + 1))
        out[name] = w
        out["m_" + name] = s * _jax.random.normal(km, w.shape, _jnp.float32)
        out["v_" + name] = (s * s) * _jax.random.uniform(kv, w.shape, _jnp.float32, 0.5, 1.5)
    if N_MICROBATCH > 1:
        for name, axis in PER_EXAMPLE_BATCH_AXIS.items():
            out[name] = _to_microbatches(out[name], axis)
    return {'x': out['x'], 'meta_tokens': out['meta_tokens'], 'mix_pre_g': out['mix_pre_g'], 'w_in': out['w_in'], 'ssd_conv_w': out['ssd_conv_w'], 'ssd_conv_b': out['ssd_conv_b'], 'ssd_dt_bias': out['ssd_dt_bias'], 'ssd_a_log': out['ssd_a_log'], 'ssd_d': out['ssd_d'], 'ssd_norm_g': out['ssd_norm_g'], 'sb_norm_g': out['sb_norm_g'], 'w_out': out['w_out'], 'mix_post_g': out['mix_post_g'], 'ffn_pre_g': out['ffn_pre_g'], 'w_up': out['w_up'], 'ffn_conv_w': out['ffn_conv_w'], 'ffn_conv_b': out['ffn_conv_b'], 'w_down': out['w_down'], 'ffn_post_g': out['ffn_post_g'], 'loss_target': out['loss_target'], 'm_meta_tokens': out['m_meta_tokens'], 'm_mix_pre_g': out['m_mix_pre_g'], 'm_w_in': out['m_w_in'], 'm_ssd_conv_w': out['m_ssd_conv_w'], 'm_ssd_conv_b': out['m_ssd_conv_b'], 'm_ssd_dt_bias': out['m_ssd_dt_bias'], 'm_ssd_a_log': out['m_ssd_a_log'], 'm_ssd_d': out['m_ssd_d'], 'm_ssd_norm_g': out['m_ssd_norm_g'], 'm_sb_norm_g': out['m_sb_norm_g'], 'm_w_out': out['m_w_out'], 'm_mix_post_g': out['m_mix_post_g'], 'm_ffn_pre_g': out['m_ffn_pre_g'], 'm_w_up': out['m_w_up'], 'm_ffn_conv_w': out['m_ffn_conv_w'], 'm_ffn_conv_b': out['m_ffn_conv_b'], 'm_w_down': out['m_w_down'], 'm_ffn_post_g': out['m_ffn_post_g'], 'v_meta_tokens': out['v_meta_tokens'], 'v_mix_pre_g': out['v_mix_pre_g'], 'v_w_in': out['v_w_in'], 'v_ssd_conv_w': out['v_ssd_conv_w'], 'v_ssd_conv_b': out['v_ssd_conv_b'], 'v_ssd_dt_bias': out['v_ssd_dt_bias'], 'v_ssd_a_log': out['v_ssd_a_log'], 'v_ssd_d': out['v_ssd_d'], 'v_ssd_norm_g': out['v_ssd_norm_g'], 'v_sb_norm_g': out['v_sb_norm_g'], 'v_w_out': out['v_w_out'], 'v_mix_post_g': out['v_mix_post_g'], 'v_ffn_pre_g': out['v_ffn_pre_g'], 'v_w_up': out['v_w_up'], 'v_ffn_conv_w': out['v_ffn_conv_w'], 'v_ffn_conv_b': out['v_ffn_conv_b'], 'v_w_down': out['v_w_down'], 'v_ffn_post_g': out['v_ffn_post_g']}


def _loss(weights, diff, rest, loss_target):
    with _jax.named_scope("forward"):
        args = {**rest, TWIN_DIFF_INPUT: diff, **{k: w.astype(_WEIGHT_DTYPES[k]) for k, w in weights.items()}}
        y = _forward(args)
    with _jax.named_scope("loss_head"):
        err = _jnp.square(y.astype(_jnp.float32) - loss_target)
        return 0.5 * _jnp.sum(_jnp.mean(err, axis=-1)) if err.ndim else 0.5 * err


def _adamw(w, g, m, v):
    m = ADAM_B1 * m + (1.0 - ADAM_B1) * g
    v = ADAM_B2 * v + (1.0 - ADAM_B2) * _jnp.square(g)
    m_hat = m / (1.0 - ADAM_B1 ** ADAM_STEP)
    v_hat = v / (1.0 - ADAM_B2 ** ADAM_STEP)
    delta = -ADAM_LR * (m_hat / (_jnp.sqrt(v_hat) + ADAM_EPS) + ADAM_WD * w)
    return delta, m, v


def reference(x, meta_tokens, mix_pre_g, w_in, ssd_conv_w, ssd_conv_b, ssd_dt_bias, ssd_a_log, ssd_d, ssd_norm_g, sb_norm_g, w_out, mix_post_g, ffn_pre_g, w_up, ffn_conv_w, ffn_conv_b, w_down, ffn_post_g, loss_target, m_meta_tokens, m_mix_pre_g, m_w_in, m_ssd_conv_w, m_ssd_conv_b, m_ssd_dt_bias, m_ssd_a_log, m_ssd_d, m_ssd_norm_g, m_sb_norm_g, m_w_out, m_mix_post_g, m_ffn_pre_g, m_w_up, m_ffn_conv_w, m_ffn_conv_b, m_w_down, m_ffn_post_g, v_meta_tokens, v_mix_pre_g, v_w_in, v_ssd_conv_w, v_ssd_conv_b, v_ssd_dt_bias, v_ssd_a_log, v_ssd_d, v_ssd_norm_g, v_sb_norm_g, v_w_out, v_mix_post_g, v_ffn_pre_g, v_w_up, v_ffn_conv_w, v_ffn_conv_b, v_w_down, v_ffn_post_g):
    given = dict(x=x, meta_tokens=meta_tokens, mix_pre_g=mix_pre_g, w_in=w_in, ssd_conv_w=ssd_conv_w, ssd_conv_b=ssd_conv_b, ssd_dt_bias=ssd_dt_bias, ssd_a_log=ssd_a_log, ssd_d=ssd_d, ssd_norm_g=ssd_norm_g, sb_norm_g=sb_norm_g, w_out=w_out, mix_post_g=mix_post_g, ffn_pre_g=ffn_pre_g, w_up=w_up, ffn_conv_w=ffn_conv_w, ffn_conv_b=ffn_conv_b, w_down=w_down, ffn_post_g=ffn_post_g, loss_target=loss_target, m_meta_tokens=m_meta_tokens, m_mix_pre_g=m_mix_pre_g, m_w_in=m_w_in, m_ssd_conv_w=m_ssd_conv_w, m_ssd_conv_b=m_ssd_conv_b, m_ssd_dt_bias=m_ssd_dt_bias, m_ssd_a_log=m_ssd_a_log, m_ssd_d=m_ssd_d, m_ssd_norm_g=m_ssd_norm_g, m_sb_norm_g=m_sb_norm_g, m_w_out=m_w_out, m_mix_post_g=m_mix_post_g, m_ffn_pre_g=m_ffn_pre_g, m_w_up=m_w_up, m_ffn_conv_w=m_ffn_conv_w, m_ffn_conv_b=m_ffn_conv_b, m_w_down=m_w_down, m_ffn_post_g=m_ffn_post_g, v_meta_tokens=v_meta_tokens, v_mix_pre_g=v_mix_pre_g, v_w_in=v_w_in, v_ssd_conv_w=v_ssd_conv_w, v_ssd_conv_b=v_ssd_conv_b, v_ssd_dt_bias=v_ssd_dt_bias, v_ssd_a_log=v_ssd_a_log, v_ssd_d=v_ssd_d, v_ssd_norm_g=v_ssd_norm_g, v_sb_norm_g=v_sb_norm_g, v_w_out=v_w_out, v_mix_post_g=v_mix_post_g, v_ffn_pre_g=v_ffn_pre_g, v_w_up=v_w_up, v_ffn_conv_w=v_ffn_conv_w, v_ffn_conv_b=v_ffn_conv_b, v_w_down=v_w_down, v_ffn_post_g=v_ffn_post_g)
    weights = {n: given[n] for n in TWIN_WEIGHTS}
    shared = {n: given[n] for n in SHARED_INPUTS}
    per_example = {n: given[n] for n in ['x']}
    grad_fn = _jax.value_and_grad(_loss, argnums=(0, 1))

    def one_microbatch(ex, loss_target):
        ex = dict(ex)
        diff = ex.pop(TWIN_DIFF_INPUT)
        return grad_fn(weights, diff, {**shared, **ex}, loss_target)

    if N_MICROBATCH == 1:
        loss, (grad_w, grad_x) = one_microbatch(per_example, given["loss_target"])
    else:
        def body(carry, xs):
            loss_sum, grad_sum = carry
            l_k, (gw_k, gx_k) = one_microbatch(xs[0], xs[1])
            with _jax.named_scope("update"):
                return (loss_sum + l_k, _jax.tree.map(_jnp.add, grad_sum, gw_k)), gx_k

        init = (_jnp.zeros((), _jnp.float32), _jax.tree.map(_jnp.zeros_like, weights))
        (loss, grad_w), grad_x = _jax.lax.scan(body, init, (per_example, given["loss_target"]))
    with _jax.named_scope("update"):
        delta_w, new_m, new_v = {}, {}, {}
        for n in TWIN_WEIGHTS:
            delta_w[n], new_m[n], new_v[n] = _adamw(weights[n], grad_w[n], given["m_" + n], given["v_" + n])
    return (loss, grad_x, *[grad_w[n] for n in TWIN_WEIGHTS], *[delta_w[n] for n in TWIN_WEIGHTS],
            *[new_m[n] for n in TWIN_WEIGHTS], *[new_v[n] for n in TWIN_WEIGHTS])
```

```python
import functools
import math

import jax
import jax.numpy as jnp
from jax import lax
from jax.experimental import pallas as pl
from jax.experimental.pallas import tpu as pltpu

F32 = jnp.float32
BF16 = jnp.bfloat16

D_MODEL = 1024
N_META = 16
BLK = 128
PAD = BLK - N_META
HEAD_DIM = 64
N_HEADS = 16
SSD_GROUPS = 2
SSD_STATE = 128
SSD_INNER = 1024
XBC = SSD_INNER + 2 * SSD_GROUPS * SSD_STATE
D_FF = 2816
EPS = 1e-6
IN_COLS = 5648
C_Z, C_XBC, C_DT, C_Q, C_K, C_V, C_END = 0, 1024, 2560, 2688, 3712, 4736, 5760
DT_REAL_OFF = 2560
N_CHIPS = 4
ADAM_LR, ADAM_B1, ADAM_B2, ADAM_EPS, ADAM_WD, ADAM_STEP = 0.001, 0.9, 0.999, 1e-08, 0.01, 10
VMEM_LIMIT = 56 * 1024 * 1024
MESH = pl.DeviceIdType.MESH


def _cparams(sem=None, **kw):
    if sem is not None:
        kw["dimension_semantics"] = sem
    return pltpu.CompilerParams(vmem_limit_bytes=VMEM_LIMIT, **kw)


def _pick(n, cands):
    for c in cands:
        if n % c == 0:
            return c
    raise ValueError((n, cands))


def _iota(shape, dim):
    return lax.broadcasted_iota(jnp.int32, shape, dim)


def _sigmoid(x):
    return 1.0 / (1.0 + jnp.exp(-x))


def _split3(v):
    h1 = v.astype(BF16)
    r1 = v - h1.astype(F32)
    h2 = r1.astype(BF16)
    h3 = (r1 - h2.astype(F32)).astype(BF16)
    return h1, h2, h3


def _dot(a, b, ca=1, cb=0):
    return lax.dot_general(a, b, (((ca,), (cb,)), ((), ())), preferred_element_type=F32)


def _dot_sel_r(v, sel, cb=0):
    h1, h2, h3 = _split3(v)
    return _dot(h1, sel, 1, cb) + _dot(h2, sel, 1, cb) + _dot(h3, sel, 1, cb)


def _dot_sel_l(sel, v, ca=1):
    h1, h2, h3 = _split3(v)
    return _dot(sel, h1, ca, 0) + _dot(sel, h2, ca, 0) + _dot(sel, h3, ca, 0)


def _mm(a, b, *, ta=False, tb=False, tm, tn, tk, out_dtype=F32, nsplit=1, extra_bf16=False, name):
    K, M = (a.shape if ta else a.shape[::-1])
    N = b.shape[0] if tb else b.shape[1]
    assert M % tm == 0 and N % tn == 0 and K % tk == 0, (name, M, N, K, tm, tn, tk)
    nm, nn, nk = M // tm, N // tn, K // tk
    assert nn % nsplit == 0
    per = nn // nsplit
    a_spec = (pl.BlockSpec((tk, tm), lambda i, j, k: (k, i)) if ta
              else pl.BlockSpec((tm, tk), lambda i, j, k: (i, k)))
    b_spec = (pl.BlockSpec((tn, tk), lambda i, j, k: (j, k)) if tb
              else pl.BlockSpec((tk, tn), lambda i, j, k: (k, j)))
    o_spec = pl.BlockSpec((None, tm, tn), lambda i, j, k: (j // per, i, j % per))
    n_out = 2 if extra_bf16 else 1
    ca, cb = (0 if ta else 1), (1 if tb else 0)

    def body(a_ref, b_ref, *rest):
        outs = rest[:n_out]
        p = _dot(a_ref[...].astype(BF16), b_ref[...].astype(BF16), ca, cb)

        def emit(val):
            outs[0][...] = val.astype(out_dtype)
            if extra_bf16:
                outs[1][...] = val.astype(BF16)

        if nk == 1:
            emit(p)
        else:
            acc = rest[n_out]
            k = pl.program_id(2)

            @pl.when(k == 0)
            def _():
                acc[...] = p

            @pl.when(k > 0)
            def _():
                acc[...] += p

            @pl.when(k == nk - 1)
            def _():
                emit(acc[...])

    shp = (nsplit, M, N // nsplit)
    out_shape = [jax.ShapeDtypeStruct(shp, out_dtype)]
    out_specs = [o_spec]
    if extra_bf16:
        out_shape.append(jax.ShapeDtypeStruct(shp, BF16))
        out_specs.append(o_spec)
    res = pl.pallas_call(
        body, name=name, grid=(nm, nn, nk), in_specs=[a_spec, b_spec], out_specs=out_specs,
        out_shape=out_shape, scratch_shapes=([pltpu.VMEM((tm, tn), F32)] if nk > 1 else []),
        compiler_params=_cparams(("parallel", "parallel", "arbitrary")),
    )(a, b)
    return res if extra_bf16 else res[0]


def _rms_stats(x):
    r = lax.rsqrt(jnp.mean(x * x, axis=-1, keepdims=True) + EPS)
    return r, x * r


def _rms_bwd(x, g, dy):
    r, xh = _rms_stats(x)
    dxh = dy * g
    dx = r * (dxh - xh * jnp.mean(dxh * xh, axis=-1, keepdims=True))
    return dx, jnp.sum(dy * xh, axis=0, keepdims=True)


def _row_spec(tr, w, col=0):
    return pl.BlockSpec((tr, w), lambda i: (i, col))


def _vec_spec(w):
    return pl.BlockSpec((1, w), lambda i: (0, 0))


def _acc_rows(ref, val, i):
    @pl.when(i == 0)
    def _():
        ref[...] = val

    @pl.when(i > 0)
    def _():
        ref[...] += val


def _rms_fwd_call(x, g, name):
    lp, w = x.shape
    tr = _pick(lp, [384, 128])

    def body(x_ref, g_ref, o_ref):
        _, xh = _rms_stats(x_ref[...])
        o_ref[...] = (xh * g_ref[...]).astype(BF16)

    return pl.pallas_call(
        body, name=name, grid=(lp // tr,), in_specs=[_row_spec(tr, w), _vec_spec(w)],
        out_specs=_row_spec(tr, w), out_shape=jax.ShapeDtypeStruct((lp, w), BF16),
        compiler_params=_cparams(("parallel",)))(x, g)


def _mid_fwd_call(h0, mix, g_post, g_pre2):
    lp, w = h0.shape
    tr = _pick(lp, [384, 128])

    def body(h0_ref, mix_ref, gp_ref, g2_ref, h1_ref, xn_ref):
        _, mh = _rms_stats(mix_ref[...])
        h1 = h0_ref[...] + mh * gp_ref[...]
        h1_ref[...] = h1
        _, hh = _rms_stats(h1)
        xn_ref[...] = (hh * g2_ref[...]).astype(BF16)

    return pl.pallas_call(
        body, name="mid_fwd", grid=(lp // tr,),
        in_specs=[_row_spec(tr, w), _row_spec(tr, w), _vec_spec(w), _vec_spec(w)],
        out_specs=[_row_spec(tr, w), _row_spec(tr, w)],
        out_shape=[jax.ShapeDtypeStruct((lp, w), F32), jax.ShapeDtypeStruct((lp, w), BF16)],
        compiler_params=_cparams(("parallel",)))(h0, mix, g_post, g_pre2)


def _final_call(h1, f, g_post, target):
    lp, w = h1.shape
    tr = BLK
    nb = lp // tr

    def body(h1_ref, f_ref, g_ref, t_ref, loss_ref, df_ref, dh_ref, dg_ref):
        i = pl.program_id(0)
        fv = f_ref[...]
        g = g_ref[...]
        _, fh = _rms_stats(fv)
        h2 = h1_ref[...] + fh * g
        diff = jnp.where(i > 0, h2 - t_ref[...], 0.0)
        part = 0.5 * jnp.sum(diff * diff, axis=0, keepdims=True) * (1.0 / w)
        _acc_rows(loss_ref, part, i)
        dh = diff * (1.0 / w)
        dh_ref[...] = dh
        df, dg = _rms_bwd(fv, g, dh)
        df_ref[...] = df.astype(BF16)
        _acc_rows(dg_ref, dg, i)

    t_spec = pl.BlockSpec((tr, w), lambda i: (jnp.maximum(i - 1, 0), 0))
    return pl.pallas_call(
        body, name="final_fwd_bwd", grid=(nb,),
        in_specs=[_row_spec(tr, w), _row_spec(tr, w), _vec_spec(w), t_spec],
        out_specs=[_vec_spec(w), _row_spec(tr, w), _row_spec(tr, w), _vec_spec(w)],
        out_shape=[jax.ShapeDtypeStruct((1, w), F32), jax.ShapeDtypeStruct((lp, w), BF16),
                   jax.ShapeDtypeStruct((lp, w), F32), jax.ShapeDtypeStruct((1, w), F32)],
        compiler_params=_cparams(("arbitrary",)))(h1, f, g_post, target)


def _mid_bwd_call(dh2, h1, dxn2, mix, g_pre2, g_post):
    lp, w = h1.shape
    tr = _pick(lp, [384, 128])

    def body(dh2_ref, h1_ref, dxn_ref, mix_ref, g2_ref, gp_ref, dh1_ref, dmix_ref, dg2_ref, dgp_ref):
        i = pl.program_id(0)
        live = (i * tr + _iota((tr, 1), 0)) >= PAD
        dx, dg2 = _rms_bwd(h1_ref[...], g2_ref[...], dxn_ref[...])
        dh1 = jnp.where(live, dh2_ref[...] + dx, 0.0)
        dh1_ref[...] = dh1
        dmix, dgp = _rms_bwd(mix_ref[...], gp_ref[...], dh1)
        dmix_ref[...] = jnp.where(live, dmix, 0.0).astype(BF16)
        _acc_rows(dg2_ref, dg2, i)
        _acc_rows(dgp_ref, dgp, i)

    rs = _row_spec(tr, w)
    return pl.pallas_call(
        body, name="mid_bwd", grid=(lp // tr,),
        in_specs=[rs, rs, rs, rs, _vec_spec(w), _vec_spec(w)],
        out_specs=[rs, rs, _vec_spec(w), _vec_spec(w)],
        out_shape=[jax.ShapeDtypeStruct((lp, w), F32), jax.ShapeDtypeStruct((lp, w), BF16),
                   jax.ShapeDtypeStruct((1, w), F32), jax.ShapeDtypeStruct((1, w), F32)],
        compiler_params=_cparams(("arbitrary",)))(dh2, h1, dxn2, mix, g_pre2, g_post)


def _norm_bwd_call(x, g, dy_arr, dy_col, name, res=None):
    lp, w = x.shape
    tr = _pick(lp, [384, 128])
    has_res = res is not None

    def body(x_ref, g_ref, dy_ref, *rest):
        i = pl.program_id(0)
        live = (i * tr + _iota((tr, 1), 0)) >= PAD
        dx, dg = _rms_bwd(x_ref[...], g_ref[...], dy_ref[...])
        if has_res:
            dx = dx + rest[0][...]
        out_ref, dg_ref = rest[-2], rest[-1]
        out_ref[...] = jnp.where(live, dx, 0.0)
        _acc_rows(dg_ref, dg, i)

    rs = _row_spec(tr, w)
    ins = [rs, _vec_spec(w), _row_spec(tr, w, dy_col)] + ([rs] if has_res else [])
    args = [x, g, dy_arr] + ([res] if has_res else [])
    return pl.pallas_call(
        body, name=name, grid=(lp // tr,), in_specs=ins, out_specs=[rs, _vec_spec(w)],
        out_shape=[jax.ShapeDtypeStruct((lp, w), F32), jax.ShapeDtypeStruct((1, w), F32)],
        compiler_params=_cparams(("arbitrary",)))(*args)


def _shift_down(cur, prev_tail, s, rows):
    if s == 0:
        return cur
    prev = jnp.tile(prev_tail, (BLK // 8, 1))
    return jnp.where(rows >= s, pltpu.roll(cur, s, 0), pltpu.roll(prev, s, 0))


def _shift_up(cur, next_head, s, rows):
    if s == 0:
        return cur
    nxt = jnp.tile(next_head, (BLK // 8, 1))
    return jnp.where(rows < BLK - s, pltpu.roll(cur, BLK - s, 0), pltpu.roll(nxt, BLK - s, 0))


def _gelu_tanh(x):
    c = math.sqrt(2.0 / math.pi)
    t = jnp.tanh(c * (x + 0.044715 * x * x * x))
    return 0.5 * x * (1.0 + t), t


def _conv_fwd_call(src, col0, width, cw, w8, b, taps, *, gate_src=None, gate_col0=0, name):
    lp = src.shape[0]
    nb, nc = lp // BLK, width // cw
    cb0 = col0 // cw
    ffn = gate_src is not None

    def body(x_ref, w_ref, b_ref, *rest):
        if ffn:
            u_ref, y_ref, a_ref, tail = rest
        else:
            y_ref, a_ref, tail = rest
        i = pl.program_id(1)

        @pl.when(i == 0)
        def _():
            tail[...] = jnp.zeros_like(tail)

        cur = x_ref[...]
        rows = _iota((BLK, cw), 0)
        y = b_ref[...] + w_ref[taps - 1:taps, :] * cur
        pt = tail[...]
        for s in range(1, taps):
            y = y + w_ref[taps - 1 - s:taps - s, :] * _shift_down(cur, pt, s, rows)
        tail[...] = cur[BLK - 8:, :]
        y_ref[...] = y
        if ffn:
            ge, _ = _gelu_tanh(y)
            a_ref[...] = (ge * u_ref[...]).astype(BF16)
        else:
            live = (i * BLK + rows) >= PAD
            a_ref[...] = jnp.where(live, y * _sigmoid(y), 0.0)

    blk = lambda c0: pl.BlockSpec((BLK, cw), lambda j, i: (i, c0 + j))
    ins = [blk(cb0), pl.BlockSpec((8, cw), lambda j, i: (0, j)), pl.BlockSpec((1, cw), lambda j, i: (0, j))]
    args = [src, w8, b]
    if ffn:
        ins.append(blk(gate_col0 // cw))
        args.append(gate_src)
    return pl.pallas_call(
        body, name=name, grid=(nc, nb), in_specs=ins, out_specs=[blk(0), blk(0)],
        out_shape=[jax.ShapeDtypeStruct((lp, width), F32),
                   jax.ShapeDtypeStruct((lp, width), BF16 if ffn else F32)],
        scratch_shapes=[pltpu.VMEM((8, cw), F32)],
        compiler_params=_cparams(("parallel", "arbitrary")))(*args)


def _conv_bwd_call(src, col0, width, cw, w8, taps, ypre, dact, *, gate_src=None, gate_col0=0, name):
    lp = src.shape[0]
    nb, nc = lp // BLK, width // cw
    cb0 = col0 // cw
    ffn = gate_src is not None

    def body(x_ref, w_ref, y_ref, d_ref, *rest):
        if ffn:
            u_ref, dx_ref, du_ref, dw_ref, db_ref, head = rest
        else:
            dx_ref, dw_ref, db_ref, head = rest
        step = pl.program_id(1)
        i = nb - 1 - step

        @pl.when(step == 0)
        def _():
            head[...] = jnp.zeros_like(head)

        rows = _iota((BLK, cw), 0)
        live = (i * BLK + rows) >= PAD
        y = y_ref[...]
        d = d_ref[...]
        if ffn:
            ge, t = _gelu_tanh(y)
            c = math.sqrt(2.0 / math.pi)
            dge = 0.5 * (1.0 + t) + 0.5 * y * (1.0 - t * t) * c * (1.0 + 3.0 * 0.044715 * y * y)
            u = u_ref[...]
            du_ref[...] = jnp.where(live, d * ge, 0.0).astype(BF16)
            dy = jnp.where(live, d * u * dge, 0.0)
        else:
            sg = _sigmoid(y)
            dy = jnp.where(live, d * sg * (1.0 + y * (1.0 - sg)), 0.0)
        x = x_ref[...]
        nh = head[...]
        dx = jnp.zeros_like(dy)
        dws = []
        for s in range(taps):
            sh = _shift_up(dy, nh, s, rows)
            dx = dx + w_ref[taps - 1 - s:taps - s, :] * sh
            dws.append(jnp.sum(x * sh, axis=0, keepdims=True))
        head[...] = dy[:8, :]
        dx_ref[...] = jnp.where(live, dx, 0.0).astype(BF16)
        dw = jnp.concatenate([dws[taps - 1 - k] for k in range(taps)]
                             + [jnp.zeros((8 - taps, cw), F32)], axis=0)
        _acc_rows(dw_ref, dw, step)
        _acc_rows(db_ref, jnp.sum(dy, axis=0, keepdims=True), step)

    blk = lambda c0: pl.BlockSpec((BLK, cw), lambda j, s: (nb - 1 - s, c0 + j))
    ins = [blk(cb0), pl.BlockSpec((8, cw), lambda j, s: (0, j)), blk(0), blk(0)]
    args = [src, w8, ypre, dact]
    outs = [blk(0)]
    oshape = [jax.ShapeDtypeStruct((lp, width), BF16)]
    if ffn:
        ins.append(blk(gate_col0 // cw))
        args.append(gate_src)
        outs.append(blk(0))
        oshape.append(jax.ShapeDtypeStruct((lp, width), BF16))
    outs += [pl.BlockSpec((8, cw), lambda j, s: (0, j)), pl.BlockSpec((1, cw), lambda j, s: (0, j))]
    oshape += [jax.ShapeDtypeStruct((8, width), F32), jax.ShapeDtypeStruct((1, width), F32)]
    return pl.pallas_call(
        body, name=name, grid=(nc, nb), in_specs=ins, out_specs=outs, out_shape=oshape,
        scratch_shapes=[pltpu.VMEM((8, cw), F32)],
        compiler_params=_cparams(("parallel", "arbitrary")))(*args)


def _sb_scores(qm_h, kb, valid):
    z = _dot(qm_h, kb, 1, 1)
    sp = jnp.maximum(z, 0.0) + jnp.log(1.0 + jnp.exp(-jnp.abs(z)))
    lk = jnp.where(valid, -sp, 0.0)
    return z - sp, lk


def _dot_tri(v, tri):
    hi = v.astype(BF16)
    lo = (v - hi.astype(F32)).astype(BF16)
    return _dot(hi, tri) + _dot(lo, tri)


def _sb_fwd_call(proj):
    lp = proj.shape[0]
    nb = lp // BLK
    scale = 1.0 / math.sqrt(HEAD_DIM)

    def body(q_ref, k_ref, v_ref, o_ref, tl_ref):
        i = pl.program_id(1)
        lane = _iota((BLK, BLK), 1)
        row = _iota((BLK, BLK), 0)
        low = lane < HEAD_DIM
        q = q_ref[...] * scale
        qm = (jnp.where(low, q, 0.0).astype(BF16), jnp.where(low, 0.0, q).astype(BF16))
        tri = (row > lane).astype(BF16)
        qpos = i * BLK + row

        def step(jj, carry):
            j = i - jj
            off = pl.multiple_of(j * BLK, BLK)
            kb = k_ref[pl.ds(off, BLK), :].astype(BF16)
            vb = v_ref[pl.ds(off, BLK), :].astype(BF16)
            kpos = j * BLK + lane
            valid = (kpos < qpos) & (kpos >= PAD)
            new = []
            for h in range(2):
                run, acc = carry[2 * h], carry[2 * h + 1]
                lb, lk = _sb_scores(qm[h], kb, valid)
                after = _dot_tri(lk, tri) + run
                w = jnp.where(valid, jnp.exp(lb + after), 0.0)
                new += [run + jnp.sum(lk, axis=1, keepdims=True), acc + _dot(w.astype(BF16), vb)]
            return tuple(new)

        zc = jnp.zeros((BLK, 1), F32)
        za = jnp.zeros((BLK, BLK), F32)
        r0, o0, r1, o1 = lax.fori_loop(0, i + 1, step, (zc, za, zc, za))
        o_ref[...] = jnp.where(low, o0, o1)
        tl_ref[...] = jnp.where(low, r0, r1)

    qc, kc, vc = C_Q // BLK, C_K // BLK, C_V // BLK
    blk = pl.BlockSpec((BLK, BLK), lambda p, i: (i, p))
    return pl.pallas_call(
        body, name="sb_fwd", grid=(N_HEADS // 2, nb),
        in_specs=[pl.BlockSpec((BLK, BLK), lambda p, i: (i, qc + p)),
                  pl.BlockSpec((lp, BLK), lambda p, i: (0, kc + p)),
                  pl.BlockSpec((lp, BLK), lambda p, i: (0, vc + p))],
        out_specs=[blk, blk],
        out_shape=[jax.ShapeDtypeStruct((lp, N_HEADS * HEAD_DIM), F32)] * 2,
        compiler_params=_cparams(("parallel", "arbitrary")))(proj, proj, proj)


def _sb_bwd_call(proj, o, tl, do):
    lp = proj.shape[0]
    nb = lp // BLK
    scale = 1.0 / math.sqrt(HEAD_DIM)

    def body(q_ref, k_ref, v_ref, o_ref, tl_ref, do_ref, dq_ref, dk_ref, dv_ref, dk_acc, dv_acc):
        i = pl.program_id(1)

        @pl.when(i == 0)
        def _():
            dk_acc[...] = jnp.zeros_like(dk_acc)
            dv_acc[...] = jnp.zeros_like(dv_acc)

        lane = _iota((BLK, BLK), 1)
        row = _iota((BLK, BLK), 0)
        low = lane < HEAD_DIM
        q = q_ref[...] * scale
        dov = do_ref[...]
        qm = (jnp.where(low, q, 0.0).astype(BF16), jnp.where(low, 0.0, q).astype(BF16))
        dom = (jnp.where(low, dov, 0.0).astype(BF16), jnp.where(low, 0.0, dov).astype(BF16))
        tlv = tl_ref[...]
        tot = (tlv[:, 0:1], tlv[:, HEAD_DIM:HEAD_DIM + 1])
        tri_in = (row <= lane).astype(BF16)
        tri_ex = (row < lane).astype(BF16)
        qpos = i * BLK + row

        def step(j, carry):
            off = pl.multiple_of(j * BLK, BLK)
            kb = k_ref[pl.ds(off, BLK), :].astype(BF16)
            vb = v_ref[pl.ds(off, BLK), :].astype(BF16)
            kpos = j * BLK + lane
            valid = (kpos < qpos) & (kpos >= PAD)
            new = []
            dk_blk = jnp.zeros((BLK, BLK), F32)
            dv_blk = jnp.zeros((BLK, BLK), F32)
            for h in range(2):
                run, gsum, dq = carry[3 * h], carry[3 * h + 1], carry[3 * h + 2]
                lb, lk = _sb_scores(qm[h], kb, valid)
                after = tot[h] - run - _dot_tri(lk, tri_in)
                w = jnp.where(valid, jnp.exp(lb + after), 0.0)
                g = w * _dot(dom[h], vb, 1, 1)
                gpre = gsum + _dot_tri(g, tri_ex)
                beta = jnp.exp(lb)
                dz = jnp.where(valid, g * (1.0 - beta) - gpre * beta, 0.0).astype(BF16)
                dk_blk = dk_blk + _dot(dz, qm[h], 0, 0)
                dv_blk = dv_blk + _dot(w.astype(BF16), dom[h], 0, 0)
                new += [run + jnp.sum(lk, axis=1, keepdims=True),
                        gsum + jnp.sum(g, axis=1, keepdims=True),
                        dq + _dot(dz, kb)]
            dk_acc[pl.ds(off, BLK), :] += dk_blk
            dv_acc[pl.ds(off, BLK), :] += dv_blk
            return tuple(new)

        zc = jnp.zeros((BLK, 1), F32)
        za = jnp.zeros((BLK, BLK), F32)
        res = lax.fori_loop(0, i + 1, step, (zc, zc, za, zc, zc, za))
        dq_ref[...] = (jnp.where(low, res[2], res[5]) * scale).astype(BF16)

        @pl.when(i == nb - 1)
        def _():
            dk_ref[...] = dk_acc[...].astype(BF16)
            dv_ref[...] = dv_acc[...].astype(BF16)

    qc, kc, vc = C_Q // BLK, C_K // BLK, C_V // BLK
    blk = pl.BlockSpec((BLK, BLK), lambda p, i: (i, p))
    full = pl.BlockSpec((lp, BLK), lambda p, i: (0, p))
    w = N_HEADS * HEAD_DIM
    return pl.pallas_call(
        body, name="sb_bwd", grid=(N_HEADS // 2, nb),
        in_specs=[pl.BlockSpec((BLK, BLK), lambda p, i: (i, qc + p)),
                  pl.BlockSpec((lp, BLK), lambda p, i: (0, kc + p)),
                  pl.BlockSpec((lp, BLK), lambda p, i: (0, vc + p)),
                  blk, blk, blk],
        out_specs=[blk, full, full],
        out_shape=[jax.ShapeDtypeStruct((lp, w), BF16)] * 3,
        scratch_shapes=[pltpu.VMEM((lp, BLK), F32), pltpu.VMEM((lp, BLK), F32)],
        compiler_params=_cparams(("parallel", "arbitrary")))(proj, proj, proj, o, tl, do)


def _log1p(e):
    u = 1.0 + e
    return jnp.where(u == 1.0, e, jnp.log(u) * e / jnp.where(u == 1.0, 1.0, u - 1.0))


def _ssd_common(c, dtr, bias, alog):
    row = _iota((BLK, BLK), 0)
    lane = _iota((BLK, BLK), 1)
    live = ((c * BLK + row) >= PAD) & (lane < N_HEADS)
    pre = dtr + bias
    dt = jnp.where(live, jnp.maximum(pre, 0.0) + _log1p(jnp.exp(-jnp.abs(pre))), 0.0)
    a_neg = -jnp.exp(alog)
    a = dt * a_neg
    t_in = (lane <= row).astype(BF16)
    cs = _dot_sel_l(t_in, a)
    cs_t = cs.T
    cs_end = cs[BLK - 1:BLK, :]
    e = jnp.exp(cs)
    f = jnp.exp(cs_end - cs)
    xp = ((_iota((BLK, SSD_INNER), 1) // HEAD_DIM) == _iota((BLK, SSD_INNER), 0)).astype(BF16)
    xp_t = ((_iota((SSD_INNER, BLK), 0) // HEAD_DIM) == _iota((SSD_INNER, BLK), 1)).astype(BF16)
    decay_col = _dot_sel_l(xp_t, jnp.exp(cs_t))[:, BLK - 1:BLK]
    return dict(live=live, pre=pre, dt=dt, a_neg=a_neg, cs=cs, cs_t=cs_t, e=e, f=f, xp=xp, xp_t=xp_t,
                decay_col=decay_col, row=row, lane=lane,
                dt_x=_dot_sel_r(dt, xp), e_x=_dot_sel_r(e, xp), f_x=_dot_sel_r(f, xp))


def _ssd_ldec(q, h):
    diff = q["cs"][:, h:h + 1] - q["cs_t"][h:h + 1, :]
    causal = q["row"] >= q["lane"]
    return jnp.where(causal, jnp.exp(jnp.where(causal, diff, 0.0)), 0.0)


def _ssd_fwd_call(xbc, proj, bias, alog, d_x, norm_g):
    lp = xbc.shape[0]
    nb = lp // BLK
    gw = SSD_INNER // SSD_GROUPS
    ppg = gw // BLK

    def body(xbc_ref, dtr_ref, z_ref, bias_ref, alog_ref, dx_ref, ng_ref, yb_ref, ypre_ref, sprev_ref, s_ref):
        c = pl.program_id(0)

        @pl.when(c == 0)
        def _():
            s_ref[...] = jnp.zeros_like(s_ref)

        q = _ssd_common(c, dtr_ref[...], bias_ref[...], alog_ref[...])
        x = xbc_ref[:, 0:SSD_INNER]
        xd = x * q["dt_x"]
        low = q["lane"] < HEAD_DIM
        s_old = s_ref[...]
        sprev_ref[...] = s_old
        xdf = (xd * q["f_x"]).astype(BF16)
        for g in range(SSD_GROUPS):
            bg = xbc_ref[:, SSD_INNER + g * SSD_STATE:SSD_INNER + (g + 1) * SSD_STATE].astype(BF16)
            cg = xbc_ref[:, SSD_INNER + (SSD_GROUPS + g) * SSD_STATE:
                         SSD_INNER + (SSD_GROUPS + g + 1) * SSD_STATE].astype(BF16)
            cb = _dot(cg, bg, 1, 1)
            gs = slice(g * gw, (g + 1) * gw)
            y_off = _dot(cg, s_old[gs, :].astype(BF16), 1, 1) * q["e_x"][:, gs]
            s_ref[gs, :] = s_old[gs, :] * q["decay_col"][gs, :] + _dot(xdf[:, gs], bg, 0, 0)
            for pr in range(ppg):
                cols = slice(g * gw + pr * BLK, g * gw + (pr + 1) * BLK)
                xd_p = xd[:, cols]
                acc = y_off[:, pr * BLK:(pr + 1) * BLK]
                for hh in range(2):
                    h = (g * gw + pr * BLK) // HEAD_DIM + hh
                    m = (cb * _ssd_ldec(q, h)).astype(BF16)
                    xm = jnp.where(low, xd_p, 0.0) if hh == 0 else jnp.where(low, 0.0, xd_p)
                    acc = acc + _dot(m, xm.astype(BF16))
                ypre_ref[:, cols] = acc
        ypre = ypre_ref[...] + x * dx_ref[...]
        ypre_ref[...] = ypre
        z = z_ref[...]
        yg = ypre * (z * _sigmoid(z))
        _, yh = _rms_stats(yg)
        yb_ref[...] = (yh * ng_ref[...]).astype(BF16)

    row = lambda w, col: pl.BlockSpec((BLK, w), lambda c: (c, col))
    vec = lambda w: pl.BlockSpec((1, w), lambda c: (0, 0))
    return pl.pallas_call(
        body, name="ssd_fwd", grid=(nb,),
        in_specs=[row(XBC, 0), row(BLK, C_DT // BLK), row(SSD_INNER, 0), vec(BLK), vec(BLK),
                  vec(SSD_INNER), vec(SSD_INNER)],
        out_specs=[row(SSD_INNER, 0), row(SSD_INNER, 0),
                   pl.BlockSpec((None, SSD_INNER, SSD_STATE), lambda c: (c, 0, 0))],
        out_shape=[jax.ShapeDtypeStruct((lp, SSD_INNER), BF16), jax.ShapeDtypeStruct((lp, SSD_INNER), F32),
                   jax.ShapeDtypeStruct((nb, SSD_INNER, SSD_STATE), F32)],
        scratch_shapes=[pltpu.VMEM((SSD_INNER, SSD_STATE), F32)],
        compiler_params=_cparams(("arbitrary",)))(xbc, proj, proj, bias, alog, d_x, norm_g)


def _ssd_bwd_call(dycat, ypre, xbc, proj, sprev, bias, alog, d_x, norm_g):
    lp = xbc.shape[0]
    nb = lp // BLK
    gw = SSD_INNER // SSD_GROUPS
    ppg = gw // BLK

    def body(dy_ref, ypre_ref, xbc_ref, dtr_ref, z_ref, sp_ref, bias_ref, alog_ref, dxp_ref, ng_ref,
             dz_ref, dxbc_ref, ddt_ref, dng_ref, dd_ref, dal_ref, dbi_ref, ds_ref, dxd_ref):
        step = pl.program_id(0)
        c = nb - 1 - step

        @pl.when(step == 0)
        def _():
            ds_ref[...] = jnp.zeros_like(ds_ref)

        q = _ssd_common(c, dtr_ref[...], bias_ref[...], alog_ref[...])
        row, lane = q["row"], q["lane"]
        low = lane < HEAD_DIM
        rowlive = ((c * BLK + _iota((BLK, 1), 0)) >= PAD)
        x = xbc_ref[:, 0:SSD_INNER]
        xd = x * q["dt_x"]
        z = z_ref[...]
        sz = _sigmoid(z)
        silu = z * sz
        ypre = ypre_ref[...]
        dyg, dng = _rms_bwd(ypre * silu, ng_ref[...], dy_ref[...])
        _acc_rows(dng_ref, dng, step)
        dyp = dyg * silu
        dz_ref[...] = jnp.where(rowlive, dyg * ypre * (sz * (1.0 + z * (1.0 - sz))), 0.0).astype(BF16)
        _acc_rows(dd_ref, jnp.sum(dyp * x, axis=0, keepdims=True), step)
        dye = dyp * q["e_x"]
        xdf = xd * q["f_x"]
        s_prev = sp_ref[...]
        ds_old = ds_ref[...]
        qrow = jnp.zeros((BLK, BLK), F32)
        qcol_t = jnp.zeros((BLK, BLK), F32)
        red_e = []
        red_f = []
        for g in range(SSD_GROUPS):
            gs = slice(g * gw, (g + 1) * gw)
            bsl = slice(SSD_INNER + g * SSD_STATE, SSD_INNER + (g + 1) * SSD_STATE)
            csl = slice(SSD_INNER + (SSD_GROUPS + g) * SSD_STATE, SSD_INNER + (SSD_GROUPS + g + 1) * SSD_STATE)
            bg = xbc_ref[:, bsl].astype(BF16)
            cg = xbc_ref[:, csl].astype(BF16)
            sg = s_prev[gs, :].astype(BF16)
            dsg = ds_old[gs, :].astype(BF16)
            cb = _dot(cg, bg, 1, 1)
            bds = _dot(bg, dsg, 1, 1)
            y_off = _dot(cg, sg, 1, 1) * q["e_x"][:, gs]
            red_e.append(dyp[:, gs] * y_off)
            red_f.append(xd[:, gs] * bds * q["f_x"][:, gs])
            dc = _dot(dye[:, gs].astype(BF16), sg)
            db = _dot(xdf[:, gs].astype(BF16), dsg)
            ds_ref[gs, :] = ds_old[gs, :] * q["decay_col"][gs, :] + _dot(dye[:, gs].astype(BF16), cg, 0, 0)
            dcb = jnp.zeros((BLK, BLK), F32)
            for pr in range(ppg):
                cols = slice(g * gw + pr * BLK, g * gw + (pr + 1) * BLK)
                xd_p = xd[:, cols].astype(BF16)
                dy_p = dyp[:, cols]
                acc = q["f_x"][:, cols] * bds[:, pr * BLK:(pr + 1) * BLK]
                for hh in range(2):
                    h = (g * gw + pr * BLK) // HEAD_DIM + hh
                    ld = _ssd_ldec(q, h)
                    m = cb * ld
                    dym = (jnp.where(low, dy_p, 0.0) if hh == 0 else jnp.where(low, 0.0, dy_p)).astype(BF16)
                    dm = jnp.where(row >= lane, _dot(dym, xd_p, 1, 1), 0.0)
                    acc = acc + _dot(m.astype(BF16), dym, 0, 0)
                    dcb = dcb + dm * ld
                    qq = dm * m
                    qrow = qrow + jnp.where(lane == h, jnp.sum(qq, axis=1, keepdims=True), 0.0)
                    qcol_t = qcol_t + jnp.where(row == h, jnp.sum(qq, axis=0, keepdims=True), 0.0)
                dxd_ref[:, cols] = acc
            dcbb = dcb.astype(BF16)
            dxbc_ref[:, bsl] = jnp.where(rowlive, db + _dot(dcbb, cg, 0, 0), 0.0)
            dxbc_ref[:, csl] = jnp.where(rowlive, dc + _dot(dcbb, bg), 0.0)
        dxd = dxd_ref[...]
        dxbc_ref[:, 0:SSD_INNER] = jnp.where(rowlive, dxd * q["dt_x"] + dyp * dxp_ref[...], 0.0)
        xp_t = q["xp_t"]
        fw = _dot_sel_r(jnp.concatenate(red_f, axis=1), xp_t)
        dcs = qrow - qcol_t.T + _dot_sel_r(jnp.concatenate(red_e, axis=1), xp_t) - fw
        end_f = jnp.sum(fw, axis=0, keepdims=True)
        sds = jnp.sum(ds_old * s_prev, axis=1, keepdims=True)
        per_head = _dot_sel_l(q["xp"], jnp.broadcast_to(sds, (SSD_INNER, BLK)))
        end_e = per_head.T[0:1, :] * jnp.exp(q["cs"][BLK - 1:BLK, :])
        dcs = dcs + jnp.where(row == BLK - 1, end_f + end_e, 0.0)
        t_up = (lane >= row).astype(BF16)
        da = _dot_sel_l(t_up, dcs)
        ddt = da * q["a_neg"] + _dot_sel_r(dxd * x, xp_t)
        _acc_rows(dal_ref, jnp.sum(da * q["dt"] * q["a_neg"], axis=0, keepdims=True), step)
        ddtr = jnp.where(q["live"], ddt * _sigmoid(q["pre"]), 0.0)
        ddt_ref[...] = ddtr.astype(BF16)
        _acc_rows(dbi_ref, jnp.sum(ddtr, axis=0, keepdims=True), step)

    row_s = lambda w, col: pl.BlockSpec((BLK, w), lambda s: (nb - 1 - s, col))
    vec = lambda w: pl.BlockSpec((1, w), lambda s: (0, 0))
    return pl.pallas_call(
        body, name="ssd_bwd", grid=(nb,),
        in_specs=[row_s(SSD_INNER, 0), row_s(SSD_INNER, 0), row_s(XBC, 0), row_s(BLK, C_DT // BLK),
                  row_s(SSD_INNER, 0), pl.BlockSpec((None, SSD_INNER, SSD_STATE), lambda s: (nb - 1 - s, 0, 0)),
                  vec(BLK), vec(BLK), vec(SSD_INNER), vec(SSD_INNER)],
        out_specs=[row_s(SSD_INNER, 0), row_s(XBC, 0), row_s(BLK, 0),
                   vec(SSD_INNER), vec(SSD_INNER), vec(BLK), vec(BLK)],
        out_shape=[jax.ShapeDtypeStruct((lp, SSD_INNER), BF16), jax.ShapeDtypeStruct((lp, XBC), F32),
                   jax.ShapeDtypeStruct((lp, BLK), BF16),
                   jax.ShapeDtypeStruct((1, SSD_INNER), F32), jax.ShapeDtypeStruct((1, SSD_INNER), F32),
                   jax.ShapeDtypeStruct((1, BLK), F32), jax.ShapeDtypeStruct((1, BLK), F32)],
        scratch_shapes=[pltpu.VMEM((SSD_INNER, SSD_STATE), F32), pltpu.VMEM((BLK, SSD_INNER), F32)],
        compiler_params=_cparams(("arbitrary",)))(dycat, ypre, xbc, proj, proj, sprev, bias, alog, d_x, norm_g)


def _pad_rows8(w):
    return jnp.pad(w, ((0, 8 - w.shape[0]), (0, 0)))


def _pad_lanes(v, n=BLK):
    return jnp.pad(v, ((0, 0), (0, n - v.shape[1])))


def _local_step(x, target, wt):
    seq = x.shape[0]
    lp = seq + BLK
    tm = _pick(lp, [1408, 768, 384, 128])
    tkr = _pick(lp, [384, 128])
    h0 = jnp.concatenate([jnp.zeros((PAD, D_MODEL), F32), wt["meta"], x], axis=0)
    bias = _pad_lanes(wt["ssd_dt_bias"])
    alog = _pad_lanes(wt["ssd_a_log"])
    d_x = jnp.repeat(wt["ssd_d"], HEAD_DIM, axis=1)
    cw8 = _pad_rows8(wt["ssd_conv_w"])
    fw8 = _pad_rows8(wt["ffn_conv_w"])
    fcw = D_FF // 2

    xn1 = _rms_fwd_call(h0, wt["mix_pre_g"], "norm1")
    proj = _mm(xn1, wt["w_in"], tm=tm, tn=1152, tk=D_MODEL, name="mm_proj")[0]
    conv_pre, xbc = _conv_fwd_call(proj, C_XBC, XBC, 512, cw8, wt["ssd_conv_b"], 4, name="ssd_conv_fwd")
    y_ssd, ypre, sprev = _ssd_fwd_call(xbc, proj, bias, alog, d_x, wt["ssd_norm_g"])
    o, tl = _sb_fwd_call(proj)
    y_sb = _rms_fwd_call(o, wt["sb_norm_g"], "sb_norm")
    ycat = jnp.concatenate([y_ssd, y_sb], axis=1)
    mix = _mm(ycat, wt["w_out"], tm=tm, tn=1024, tk=2048, name="mm_mix")[0]
    h1, xn2 = _mid_fwd_call(h0, mix, wt["mix_post_g"], wt["ffn_pre_g"])
    gu = _mm(xn2, wt["w_up"], tm=tm, tn=1408, tk=D_MODEL, name="mm_up")[0]
    gpre, act = _conv_fwd_call(gu, 0, D_FF, fcw, fw8, wt["ffn_conv_b"], 3, gate_src=gu, gate_col0=D_FF,
                               name="ffn_conv_fwd")
    f = _mm(act, wt["w_down"], tm=tm, tn=1024, tk=1408, name="mm_down")[0]
    loss_row, df, dh2, dg_ffn_post = _final_call(h1, f, wt["ffn_post_g"], target)

    dact = _mm(df, wt["w_down"], tb=True, tm=tm, tn=1408, tk=D_MODEL, name="mm_dact")[0]
    dw_down, dw_down_b = _mm(act, df, ta=True, tm=1408, tn=1024, tk=tkr, extra_bf16=True, name="mm_dw_down")
    dgate, dup, dfcw, dfcb = _conv_bwd_call(gu, 0, D_FF, fcw, fw8, 3, gpre, dact, gate_src=gu, gate_col0=D_FF,
                                            name="ffn_conv_bwd")
    dgu = jnp.concatenate([dgate, dup], axis=1)
    dxn2 = _mm(dgu, wt["w_up"], tb=True, tm=tm, tn=1024, tk=1408, name="mm_dxn2")[0]
    dw_up, dw_up_b = _mm(xn2, dgu, ta=True, tm=1024, tn=1408, tk=tkr, nsplit=N_CHIPS, extra_bf16=True,
                         name="mm_dw_up")
    dh1, dmix, dg_ffn_pre, dg_mix_post = _mid_bwd_call(dh2, h1, dxn2, mix, wt["ffn_pre_g"], wt["mix_post_g"])
    dycat = _mm(dmix, wt["w_out"], tb=True, tm=tm, tn=1024, tk=D_MODEL, name="mm_dycat")[0]
    dw_out, dw_out_b = _mm(ycat, dmix, ta=True, tm=1024, tn=1024, tk=tkr, extra_bf16=True, name="mm_dw_out")
    do, dg_sb = _norm_bwd_call(o, wt["sb_norm_g"], dycat, 1, "sb_norm_bwd")
    dq, dk, dv = _sb_bwd_call(proj, o, tl, do)
    dz, dxbc_act, ddt, dg_ssd, dd_x, dalog, dbias = _ssd_bwd_call(
        dycat, ypre, xbc, proj, sprev, bias, alog, d_x, wt["ssd_norm_g"])
    dxbc, dcw, dcb = _conv_bwd_call(proj, C_XBC, XBC, 512, cw8, 4, conv_pre, dxbc_act, name="ssd_conv_bwd")
    dproj = jnp.concatenate([dz, dxbc, ddt, dq, dk, dv], axis=1)
    dxn1 = _mm(dproj, wt["w_in"], tb=True, tm=tm, tn=1024, tk=1152, name="mm_dxn1")[0]
    dw_in = _mm(xn1, dproj, ta=True, tm=1024, tn=1152, tk=tkr, name="mm_dw_in")[0]
    dh0, dg_pre = _norm_bwd_call(h0, wt["mix_pre_g"], dxn1, 0, "norm1_bwd", res=dh1)

    small = {
        "meta_tokens": dh0[PAD:BLK], "mix_pre_g": dg_pre, "ssd_conv_w": dcw[:4], "ssd_conv_b": dcb,
        "ssd_dt_bias": dbias[:, :N_HEADS], "ssd_a_log": dalog[:, :N_HEADS],
        "ssd_d": jnp.sum(dd_x.reshape(N_HEADS, HEAD_DIM), axis=1)[None],
        "ssd_norm_g": dg_ssd, "sb_norm_g": dg_sb, "mix_post_g": dg_mix_post, "ffn_pre_g": dg_ffn_pre,
        "ffn_conv_w": dfcw[:3], "ffn_conv_b": dfcb, "ffn_post_g": dg_ffn_post,
    }
    big = {"w_in": dw_in, "w_out": (dw_out, dw_out_b), "w_up": (dw_up, dw_up_b), "w_down": (dw_down, dw_down_b)}
    return loss_row, dh0[BLK:], small, big


def _adamw_call(w, g, m, v, name):
    rows, cols = w.shape
    tr = 256 if rows % 256 == 0 else (352 if rows % 352 == 0 else rows)
    c1 = 1.0 - ADAM_B1 ** ADAM_STEP
    c2 = 1.0 - ADAM_B2 ** ADAM_STEP

    def body(w_ref, g_ref, m_ref, v_ref, d_ref, mo_ref, vo_ref):
        gv = g_ref[...]
        m2 = ADAM_B1 * m_ref[...] + (1.0 - ADAM_B1) * gv
        v2 = ADAM_B2 * v_ref[...] + (1.0 - ADAM_B2) * (gv * gv)
        d_ref[...] = -ADAM_LR * ((m2 / c1) / (jnp.sqrt(v2 / c2) + ADAM_EPS) + ADAM_WD * w_ref[...])
        mo_ref[...] = m2
        vo_ref[...] = v2

    spec = pl.BlockSpec((tr, cols), lambda i: (i, 0))
    return pl.pallas_call(
        body, name=name, grid=(rows // tr,), in_specs=[spec] * 4, out_specs=[spec] * 3,
        out_shape=[jax.ShapeDtypeStruct((rows, cols), F32)] * 3,
        compiler_params=_cparams(("parallel",)))(w, g, m, v)


ANY = pl.BlockSpec(memory_space=pl.ANY)


def _place():
    x, y, c = lax.axis_index("x"), lax.axis_index("y"), lax.axis_index("c")
    chips = [(1 - x, y), (x, 1 - y), (1 - x, 1 - y)]
    return x, y, c, chips


def _half(c, h):
    return pl.ds(pl.multiple_of(c * h, 8), h)


def _allgather_call(shards):
    n = len(shards)

    def body(*refs):
        ins, outs = refs[:n], refs[n:2 * n]
        send_i, recv_i, send_d, recv_d, loc = refs[2 * n:]
        x, y, c, chips = _place()
        me = 2 * x + y
        local = [pltpu.make_async_copy(ins[a], outs[a].at[me], loc.at[a]) for a in range(n)]
        for cp in local:
            cp.start()
        sends = []
        for a in range(n):
            h = shards[a].shape[0] // 2
            for j, chip in enumerate(chips):
                cp = pltpu.make_async_remote_copy(
                    src_ref=ins[a].at[_half(c, h)], dst_ref=outs[a].at[me, _half(c, h)],
                    send_sem=send_i.at[3 * a + j], recv_sem=recv_i.at[3 * a + j],
                    device_id=(*chip, c), device_id_type=MESH)
                cp.start()
                sends.append(cp)
        for a in range(n):
            h = shards[a].shape[0] // 2
            for j, chip in enumerate(chips):
                src = 2 * chip[0] + chip[1]
                landed = outs[a].at[src, _half(c, h)]
                pltpu.make_async_remote_copy(
                    src_ref=landed, dst_ref=landed, send_sem=send_i.at[3 * a + j], recv_sem=recv_i.at[3 * a + j],
                    device_id=(*chip, c), device_id_type=MESH).wait_recv()
                cp = pltpu.make_async_remote_copy(
                    src_ref=landed, dst_ref=landed, send_sem=send_d.at[3 * a + j], recv_sem=recv_d.at[3 * a + j],
                    device_id=(x, y, 1 - c), device_id_type=MESH)
                cp.start()
                sends.append(cp)
        for a in range(n):
            h = shards[a].shape[0] // 2
            for j, chip in enumerate(chips):
                src = 2 * chip[0] + chip[1]
                other = outs[a].at[src, _half(1 - c, h)]
                pltpu.make_async_remote_copy(
                    src_ref=other, dst_ref=other, send_sem=send_d.at[3 * a + j], recv_sem=recv_d.at[3 * a + j],
                    device_id=(x, y, 1 - c), device_id_type=MESH).wait_recv()
        for cp in sends:
            cp.wait_send()
        for cp in local:
            cp.wait()

    return pl.pallas_call(
        body, name="allgather_weights", in_specs=[ANY] * n, out_specs=[ANY] * n,
        out_shape=[jax.ShapeDtypeStruct((N_CHIPS,) + s.shape, s.dtype) for s in shards],
        scratch_shapes=[pltpu.SemaphoreType.DMA((3 * n,))] * 4 + [pltpu.SemaphoreType.DMA((n,))],
    )(*shards)


def _pair_exchange_call(grads):
    n = len(grads)

    def body(*refs):
        ins, outs = refs[:n], refs[n:2 * n]
        send_d, recv_d = refs[2 * n:]
        x, y, c, _ = _place()
        cps = []
        for a in range(n):
            h = grads[a].shape[1] // 2
            cp = pltpu.make_async_remote_copy(
                src_ref=ins[a].at[:, _half(1 - c, h)], dst_ref=outs[a], send_sem=send_d.at[a], recv_sem=recv_d.at[a],
                device_id=(x, y, 1 - c), device_id_type=MESH)
            cp.start()
            cps.append(cp)
        for cp in cps:
            cp.wait()

    return pl.pallas_call(
        body, name="grad_pair_exchange", in_specs=[ANY] * n, out_specs=[ANY] * n,
        out_shape=[jax.ShapeDtypeStruct((N_CHIPS, g.shape[1] // 2, g.shape[2]), BF16) for g in grads],
        scratch_shapes=[pltpu.SemaphoreType.DMA((n,))] * 2,
    )(*grads)


def _pair_sum_call(own, got, c_idx, name):
    _, rows, cols = own.shape
    h = rows // 2
    th = _pick(h, [256, 176, 8])
    nt = h // th

    def body(c_ref, own_ref, got_ref, o_ref):
        o_ref[...] = (own_ref[...] + got_ref[...].astype(F32)).astype(BF16)

    return pl.pallas_call(
        body, name=name,
        grid_spec=pltpu.PrefetchScalarGridSpec(
            num_scalar_prefetch=1, grid=(N_CHIPS, nt),
            in_specs=[pl.BlockSpec((None, th, cols), lambda s, i, c_ref: (s, c_ref[0] * nt + i, 0)),
                      pl.BlockSpec((None, th, cols), lambda s, i, c_ref: (s, i, 0))],
            out_specs=pl.BlockSpec((None, th, cols), lambda s, i, c_ref: (s, i, 0))),
        out_shape=jax.ShapeDtypeStruct((N_CHIPS, h, cols), BF16),
        compiler_params=_cparams(("parallel", "parallel")))(c_idx, own, got)


def _chip_exchange_call(pairs):
    n = len(pairs)

    def body(*refs):
        ins, outs = refs[:n], refs[n:2 * n]
        send_i, recv_i, loc = refs[2 * n:]
        x, y, c, chips = _place()
        me = 2 * x + y
        local, sends = [], []
        for a in range(n):
            cp = pltpu.make_async_copy(ins[a].at[me], outs[a].at[me], loc.at[a])
            cp.start()
            local.append(cp)
            for j, chip in enumerate(chips):
                dst = 2 * chip[0] + chip[1]
                cp = pltpu.make_async_remote_copy(
                    src_ref=ins[a].at[dst], dst_ref=outs[a].at[me], send_sem=send_i.at[3 * a + j],
                    recv_sem=recv_i.at[3 * a + j], device_id=(*chip, c), device_id_type=MESH)
                cp.start()
                sends.append(cp)
        for a in range(n):
            for j, chip in enumerate(chips):
                src = 2 * chip[0] + chip[1]
                pltpu.make_async_remote_copy(
                    src_ref=ins[a].at[src], dst_ref=outs[a].at[src], send_sem=send_i.at[3 * a + j],
                    recv_sem=recv_i.at[3 * a + j], device_id=(*chip, c), device_id_type=MESH).wait_recv()
        for cp in sends:
            cp.wait_send()
        for cp in local:
            cp.wait()

    return pl.pallas_call(
        body, name="grad_chip_exchange", in_specs=[ANY] * n, out_specs=[ANY] * n,
        out_shape=[jax.ShapeDtypeStruct(p.shape, BF16) for p in pairs],
        scratch_shapes=[pltpu.SemaphoreType.DMA((3 * n,))] * 2 + [pltpu.SemaphoreType.DMA((n,))],
    )(*pairs)


def _chip_sum_call(parts, name):
    _, h, cols = parts.shape
    th = _pick(h, [256, 176, 8])

    def body(p_ref, o_ref):
        acc = p_ref[0].astype(F32)
        for s in range(1, N_CHIPS):
            acc = acc + p_ref[s].astype(F32)
        o_ref[...] = acc

    return pl.pallas_call(
        body, name=name, grid=(h // th,),
        in_specs=[pl.BlockSpec((N_CHIPS, th, cols), lambda i: (0, i, 0))],
        out_specs=pl.BlockSpec((th, cols), lambda i: (i, 0)),
        out_shape=jax.ShapeDtypeStruct((h, cols), F32),
        compiler_params=_cparams(("parallel",)))(parts)


def _half_exchange_call(halves):
    n = len(halves)

    def body(*refs):
        ins, outs = refs[:n], refs[n:2 * n]
        send_d, recv_d, loc = refs[2 * n:]
        x, y, c, _ = _place()
        cps = []
        for a in range(n):
            h = halves[a].shape[0]
            mine = outs[a].at[_half(c, h)]
            lc = pltpu.make_async_copy(ins[a], mine, loc.at[a])
            lc.start()
            rc = pltpu.make_async_remote_copy(
                src_ref=ins[a], dst_ref=mine, send_sem=send_d.at[a], recv_sem=recv_d.at[a],
                device_id=(x, y, 1 - c), device_id_type=MESH)
            rc.start()
            cps.append((lc, rc))
        for a, (lc, rc) in enumerate(cps):
            h = halves[a].shape[0]
            theirs = outs[a].at[_half(1 - c, h)]
            pltpu.make_async_remote_copy(
                src_ref=ins[a], dst_ref=theirs, send_sem=send_d.at[a], recv_sem=recv_d.at[a],
                device_id=(x, y, 1 - c), device_id_type=MESH).wait_recv()
            rc.wait_send()
            lc.wait()

    return pl.pallas_call(
        body, name="grad_half_exchange", in_specs=[ANY] * n, out_specs=[ANY] * n,
        out_shape=[jax.ShapeDtypeStruct((2 * hv.shape[0], hv.shape[1]), F32) for hv in halves],
        scratch_shapes=[pltpu.SemaphoreType.DMA((n,))] * 3,
    )(*halves)


def _allreduce_small_call(v):
    rows = v.shape[0]

    def body(v_ref, o_ref, gath, send_sems, recv_sems):
        x, y, c, chips = _place()
        me, sibling = (x, y, c), (x, y, 1 - c)

        def slot(px, py, pc):
            return gath.at[4 * px + 2 * py + pc]

        def copy(k, block, to, src=None):
            return pltpu.make_async_remote_copy(
                src_ref=slot(*block) if src is None else src, dst_ref=slot(*block),
                send_sem=send_sems.at[k], recv_sem=recv_sems.at[k], device_id=to, device_id_type=MESH)

        gath[4 * x + 2 * y + c] = v_ref[...]
        first = [copy(0, me, sibling, src=v_ref)]
        first += [copy(1 + j, me, (*chip, c), src=v_ref) for j, chip in enumerate(chips)]
        for cp in first:
            cp.start()
        passed = [copy(4 + j, (*chip, c), sibling) for j, chip in enumerate(chips)]
        for j, chip in enumerate(chips):
            copy(1 + j, (*chip, c), me).wait_recv()
            passed[j].start()
        copy(0, sibling, me).wait_recv()
        for j, chip in enumerate(chips):
            copy(4 + j, (*chip, 1 - c), me).wait_recv()
        for cp in first + passed:
            cp.wait_send()
        acc = gath[0]
        for d in range(1, 8):
            acc = acc + gath[d]
        o_ref[...] = acc

    vm = pl.BlockSpec(memory_space=pltpu.VMEM)
    return pl.pallas_call(
        body, name="allreduce_small", in_specs=[vm], out_specs=vm,
        out_shape=jax.ShapeDtypeStruct((rows, BLK), F32),
        scratch_shapes=[pltpu.VMEM((8, rows, BLK), F32), pltpu.SemaphoreType.DMA((7,)),
                        pltpu.SemaphoreType.DMA((7,))],
        compiler_params=pltpu.CompilerParams(vmem_limit_bytes=VMEM_LIMIT),
    )(v)


def _pack(arrs, min_rows=8):
    parts = []
    for a in arrs:
        flat = a.reshape(-1).astype(F32)
        parts.append(jnp.pad(flat, (0, (-flat.shape[0]) % BLK)))
    buf = jnp.concatenate(parts).reshape(-1, BLK)
    return jnp.pad(buf, ((0, (-buf.shape[0]) % min_rows), (0, 0)))


def _unpack(buf, shapes):
    out, r = [], 0
    for shp in shapes:
        n = math.prod(shp)
        nr = -(-n // BLK)
        out.append(buf[r:r + nr].reshape(-1)[:n].reshape(shp))
        r += nr
    return out


SMALL = ["meta_tokens", "mix_pre_g", "ssd_conv_w", "ssd_conv_b", "ssd_dt_bias", "ssd_a_log", "ssd_d", "ssd_norm_g",
         "sb_norm_g", "mix_post_g", "ffn_pre_g", "ffn_conv_w", "ffn_conv_b", "ffn_post_g"]
BIG = ["w_in", "w_out", "w_up", "w_down"]
WEIGHTS = ["meta_tokens", "mix_pre_g", "w_in", "ssd_conv_w", "ssd_conv_b", "ssd_dt_bias", "ssd_a_log", "ssd_d",
           "ssd_norm_g", "sb_norm_g", "w_out", "mix_post_g", "ffn_pre_g", "w_up", "ffn_conv_w", "ffn_conv_b",
           "w_down", "ffn_post_g"]
W_IN_SHARD = IN_COLS // N_CHIPS
W_IN_PAD = 1536


def kernel(x, meta_tokens, mix_pre_g, w_in, ssd_conv_w, ssd_conv_b, ssd_dt_bias, ssd_a_log, ssd_d, ssd_norm_g, sb_norm_g, w_out, mix_post_g, ffn_pre_g, w_up, ffn_conv_w, ffn_conv_b, w_down, ffn_post_g, loss_target, m_meta_tokens, m_mix_pre_g, m_w_in, m_ssd_conv_w, m_ssd_conv_b, m_ssd_dt_bias, m_ssd_a_log, m_ssd_d, m_ssd_norm_g, m_sb_norm_g, m_w_out, m_mix_post_g, m_ffn_pre_g, m_w_up, m_ffn_conv_w, m_ffn_conv_b, m_w_down, m_ffn_post_g, v_meta_tokens, v_mix_pre_g, v_w_in, v_ssd_conv_w, v_ssd_conv_b, v_ssd_dt_bias, v_ssd_a_log, v_ssd_d, v_ssd_norm_g, v_sb_norm_g, v_w_out, v_mix_post_g, v_ffn_pre_g, v_w_up, v_ffn_conv_w, v_ffn_conv_b, v_w_down, v_ffn_post_g):
    w = dict(meta_tokens=meta_tokens, mix_pre_g=mix_pre_g, w_in=w_in, ssd_conv_w=ssd_conv_w, ssd_conv_b=ssd_conv_b, ssd_dt_bias=ssd_dt_bias, ssd_a_log=ssd_a_log, ssd_d=ssd_d, ssd_norm_g=ssd_norm_g, sb_norm_g=sb_norm_g, w_out=w_out, mix_post_g=mix_post_g, ffn_pre_g=ffn_pre_g, w_up=w_up, ffn_conv_w=ffn_conv_w, ffn_conv_b=ffn_conv_b, w_down=w_down, ffn_post_g=ffn_post_g)
    m = dict(meta_tokens=m_meta_tokens, mix_pre_g=m_mix_pre_g, w_in=m_w_in, ssd_conv_w=m_ssd_conv_w, ssd_conv_b=m_ssd_conv_b, ssd_dt_bias=m_ssd_dt_bias, ssd_a_log=m_ssd_a_log, ssd_d=m_ssd_d, ssd_norm_g=m_ssd_norm_g, sb_norm_g=m_sb_norm_g, w_out=m_w_out, mix_post_g=m_mix_post_g, ffn_pre_g=m_ffn_pre_g, w_up=m_w_up, ffn_conv_w=m_ffn_conv_w, ffn_conv_b=m_ffn_conv_b, w_down=m_w_down, ffn_post_g=m_ffn_post_g)
    v = dict(meta_tokens=v_meta_tokens, mix_pre_g=v_mix_pre_g, w_in=v_w_in, ssd_conv_w=v_ssd_conv_w, ssd_conv_b=v_ssd_conv_b, ssd_dt_bias=v_ssd_dt_bias, ssd_a_log=v_ssd_a_log, ssd_d=v_ssd_d, ssd_norm_g=v_ssd_norm_g, sb_norm_g=v_sb_norm_g, w_out=v_w_out, mix_post_g=v_mix_post_g, ffn_pre_g=v_ffn_pre_g, w_up=v_w_up, ffn_conv_w=v_ffn_conv_w, ffn_conv_b=v_ffn_conv_b, w_down=v_w_down, ffn_post_g=v_ffn_post_g)
    chip = 2 * lax.axis_index("x") + lax.axis_index("y")
    c_idx = lax.axis_index("c").astype(jnp.int32).reshape(1)

    shard_small = [w["meta_tokens"], w["ssd_conv_w"][0], w["ffn_conv_w"][0]]
    shards = [
        jnp.pad(w["w_in"][0], ((0, 0), (0, W_IN_PAD - W_IN_SHARD))).astype(BF16),
        w["w_out"][0].astype(BF16), w["w_up"][0].astype(BF16), w["w_down"][0].astype(BF16),
        _pack(shard_small, 16),
    ]
    g_in, g_out, g_up, g_down, g_small = _allgather_call(shards)
    w_in_ref = jnp.concatenate([g_in[i, :, :W_IN_SHARD] for i in range(N_CHIPS)], axis=1)
    w_in_c = jnp.concatenate([w_in_ref[:, :DT_REAL_OFF + N_HEADS], jnp.zeros((D_MODEL, BLK - N_HEADS), BF16),
                              w_in_ref[:, DT_REAL_OFF + N_HEADS:]], axis=1)
    parts = [_unpack(g_small[i], [s.shape for s in shard_small]) for i in range(N_CHIPS)]
    wt = {k: w[k][0][None] if w[k].ndim == 3 else w[k] for k in
          ["mix_pre_g", "ssd_conv_b", "ssd_dt_bias", "ssd_a_log", "ssd_d", "ssd_norm_g", "sb_norm_g", "mix_post_g",
           "ffn_pre_g", "ffn_conv_b", "ffn_post_g"]}
    wt.update(
        meta=jnp.concatenate([p[0] for p in parts], axis=1),
        ssd_conv_w=jnp.concatenate([p[1] for p in parts], axis=1),
        ffn_conv_w=jnp.concatenate([p[2] for p in parts], axis=1),
        w_in=w_in_c, w_out=g_out.reshape(2 * D_MODEL, D_MODEL),
        w_up=jnp.concatenate([g_up[i] for i in range(N_CHIPS)], axis=1),
        w_down=g_down.reshape(D_FF, D_MODEL))

    loss_row, dx, small, big = _local_step(x[0], loss_target[0], wt)

    dw_in = big["w_in"]
    dw_in_ref = jnp.concatenate([dw_in[:, :DT_REAL_OFF + N_HEADS], dw_in[:, C_Q:]], axis=1)
    dw_in_s = jnp.stack([jnp.pad(dw_in_ref[:, i * W_IN_SHARD:(i + 1) * W_IN_SHARD],
                                 ((0, 0), (0, W_IN_PAD - W_IN_SHARD))) for i in range(N_CHIPS)])
    own = [dw_in_s, big["w_out"][0].reshape(N_CHIPS, -1, D_MODEL), big["w_up"][0],
           big["w_down"][0].reshape(N_CHIPS, -1, D_MODEL)]
    own_b = [dw_in_s.astype(BF16), big["w_out"][1].reshape(N_CHIPS, -1, D_MODEL), big["w_up"][1],
             big["w_down"][1].reshape(N_CHIPS, -1, D_MODEL)]
    got = _pair_exchange_call(own_b)
    pairs = [_pair_sum_call(own[a], got[a], c_idx, "pair_sum_" + BIG[a]) for a in range(4)]
    by_chip = _chip_exchange_call(pairs)
    halves = [_chip_sum_call(by_chip[a], "chip_sum_" + BIG[a]) for a in range(4)]
    full = _half_exchange_call(halves)
    grads = {"w_in": full[0][:, :W_IN_SHARD], "w_out": full[1], "w_up": full[2], "w_down": full[3]}

    small_list = [small[k] for k in SMALL] + [jnp.sum(loss_row).reshape(1, 1)]
    red = _allreduce_small_call(_pack(small_list, 8))
    red_list = _unpack(red, [a.shape for a in small_list])
    loss = red_list[-1].reshape(())
    for k, g in zip(SMALL, red_list[:-1]):
        grads[k] = g
    for k in ["meta_tokens", "ssd_conv_w", "ffn_conv_w"]:
        wk = w[k].shape[-1]
        grads[k] = lax.dynamic_slice_in_dim(grads[k], chip * wk, wk, axis=1)

    delta, new_m, new_v = {}, {}, {}
    for k in BIG:
        delta[k], new_m[k], new_v[k] = _adamw_call(w[k][0], grads[k], m[k][0], v[k][0], "adamw_" + k)
    sm_shapes = [grads[k].shape for k in SMALL]
    res = _adamw_call(_pack([w[k] for k in SMALL]), _pack([grads[k] for k in SMALL]),
                      _pack([m[k] for k in SMALL]), _pack([v[k] for k in SMALL]), "adamw_small")
    for out, buf in zip((delta, new_m, new_v), res):
        for k, a in zip(SMALL, _unpack(buf, sm_shapes)):
            out[k] = a

    def shaped(d, k):
        return d[k].reshape(w[k].shape)

    return (loss, dx[None], *[shaped(grads, k) for k in WEIGHTS], *[shaped(delta, k) for k in WEIGHTS],
            *[shaped(new_m, k) for k in WEIGHTS], *[shaped(new_v, k) for k in WEIGHTS])
```

```python
import functools
import math

import jax
import jax.numpy as jnp
from jax import lax
from jax.experimental import pallas as pl
from jax.experimental.pallas import tpu as pltpu

F32 = jnp.float32
BF16 = jnp.bfloat16

D_MODEL = 1024
N_META = 16
BLK = 128
PAD = BLK - N_META
HEAD_DIM = 64
N_HEADS = 16
SSD_GROUPS = 2
SSD_STATE = 128
SSD_INNER = 1024
XBC = SSD_INNER + 2 * SSD_GROUPS * SSD_STATE
D_FF = 2816
EPS = 1e-6
IN_COLS = 5648
C_Z, C_XBC, C_DT, C_Q, C_K, C_V, C_END = 0, 1024, 2560, 2688, 3712, 4736, 5760
DT_REAL_OFF = 2560
N_CHIPS = 4
ADAM_LR, ADAM_B1, ADAM_B2, ADAM_EPS, ADAM_WD, ADAM_STEP = 0.001, 0.9, 0.999, 1e-08, 0.01, 10
VMEM_LIMIT = 56 * 1024 * 1024
MESH = pl.DeviceIdType.MESH


def _cparams(sem=None, **kw):
    if sem is not None:
        kw["dimension_semantics"] = sem
    return pltpu.CompilerParams(vmem_limit_bytes=VMEM_LIMIT, **kw)


def _pick(n, cands):
    for c in cands:
        if n % c == 0:
            return c
    raise ValueError((n, cands))


def _iota(shape, dim):
    return lax.broadcasted_iota(jnp.int32, shape, dim)


def _sigmoid(x):
    return 1.0 / (1.0 + jnp.exp(-x))


def _split3(v):
    h1 = v.astype(BF16)
    r1 = v - h1.astype(F32)
    h2 = r1.astype(BF16)
    h3 = (r1 - h2.astype(F32)).astype(BF16)
    return h1, h2, h3


def _dot(a, b, ca=1, cb=0):
    return lax.dot_general(a, b, (((ca,), (cb,)), ((), ())), preferred_element_type=F32)


def _dot_sel_r(v, sel, cb=0):
    h1, h2, h3 = _split3(v)
    return _dot(h1, sel, 1, cb) + _dot(h2, sel, 1, cb) + _dot(h3, sel, 1, cb)


def _dot_sel_l(sel, v, ca=1):
    h1, h2, h3 = _split3(v)
    return _dot(sel, h1, ca, 0) + _dot(sel, h2, ca, 0) + _dot(sel, h3, ca, 0)


def _mm(a, b, *, ta=False, tb=False, tm, tn, tk, out_dtype=F32, nsplit=1, extra_bf16=False, name):
    K, M = (a.shape if ta else a.shape[::-1])
    N = b.shape[0] if tb else b.shape[1]
    assert M % tm == 0 and N % tn == 0 and K % tk == 0, (name, M, N, K, tm, tn, tk)
    nm, nn, nk = M // tm, N // tn, K // tk
    assert nn % nsplit == 0
    per = nn // nsplit
    a_spec = (pl.BlockSpec((tk, tm), lambda i, j, k: (k, i)) if ta
              else pl.BlockSpec((tm, tk), lambda i, j, k: (i, k)))
    b_spec = (pl.BlockSpec((tn, tk), lambda i, j, k: (j, k)) if tb
              else pl.BlockSpec((tk, tn), lambda i, j, k: (k, j)))
    o_spec = pl.BlockSpec((None, tm, tn), lambda i, j, k: (j // per, i, j % per))
    n_out = 2 if extra_bf16 else 1
    ca, cb = (0 if ta else 1), (1 if tb else 0)

    def body(a_ref, b_ref, *rest):
        outs = rest[:n_out]
        p = _dot(a_ref[...].astype(BF16), b_ref[...].astype(BF16), ca, cb)

        def emit(val):
            outs[0][...] = val.astype(out_dtype)
            if extra_bf16:
                outs[1][...] = val.astype(BF16)

        if nk == 1:
            emit(p)
        else:
            acc = rest[n_out]
            k = pl.program_id(2)

            @pl.when(k == 0)
            def _():
                acc[...] = p

            @pl.when(k > 0)
            def _():
                acc[...] += p

            @pl.when(k == nk - 1)
            def _():
                emit(acc[...])

    shp = (nsplit, M, N // nsplit)
    out_shape = [jax.ShapeDtypeStruct(shp, out_dtype)]
    out_specs = [o_spec]
    if extra_bf16:
        out_shape.append(jax.ShapeDtypeStruct(shp, BF16))
        out_specs.append(o_spec)
    res = pl.pallas_call(
        body, name=name, grid=(nm, nn, nk), in_specs=[a_spec, b_spec], out_specs=out_specs,
        out_shape=out_shape, scratch_shapes=([pltpu.VMEM((tm, tn), F32)] if nk > 1 else []),
        compiler_params=_cparams(("parallel", "parallel", "arbitrary")),
    )(a, b)
    return res if extra_bf16 else res[0]


def _rms_stats(x):
    r = lax.rsqrt(jnp.mean(x * x, axis=-1, keepdims=True) + EPS)
    return r, x * r


def _rms_bwd(x, g, dy):
    r, xh = _rms_stats(x)
    dxh = dy * g
    dx = r * (dxh - xh * jnp.mean(dxh * xh, axis=-1, keepdims=True))
    return dx, jnp.sum(dy * xh, axis=0, keepdims=True)


def _row_spec(tr, w, col=0):
    return pl.BlockSpec((tr, w), lambda i: (i, col))


def _vec_spec(w):
    return pl.BlockSpec((1, w), lambda i: (0, 0))


def _acc_rows(ref, val, i):
    @pl.when(i == 0)
    def _():
        ref[...] = val

    @pl.when(i > 0)
    def _():
        ref[...] += val


def _rms_fwd_call(x, g, name):
    lp, w = x.shape
    tr = _pick(lp, [384, 128])

    def body(x_ref, g_ref, o_ref):
        _, xh = _rms_stats(x_ref[...])
        o_ref[...] = (xh * g_ref[...]).astype(BF16)

    return pl.pallas_call(
        body, name=name, grid=(lp // tr,), in_specs=[_row_spec(tr, w), _vec_spec(w)],
        out_specs=_row_spec(tr, w), out_shape=jax.ShapeDtypeStruct((lp, w), BF16),
        compiler_params=_cparams(("parallel",)))(x, g)


def _mid_fwd_call(h0, mix, g_post, g_pre2):
    lp, w = h0.shape
    tr = _pick(lp, [384, 128])

    def body(h0_ref, mix_ref, gp_ref, g2_ref, h1_ref, xn_ref):
        _, mh = _rms_stats(mix_ref[...])
        h1 = h0_ref[...] + mh * gp_ref[...]
        h1_ref[...] = h1
        _, hh = _rms_stats(h1)
        xn_ref[...] = (hh * g2_ref[...]).astype(BF16)

    return pl.pallas_call(
        body, name="mid_fwd", grid=(lp // tr,),
        in_specs=[_row_spec(tr, w), _row_spec(tr, w), _vec_spec(w), _vec_spec(w)],
        out_specs=[_row_spec(tr, w), _row_spec(tr, w)],
        out_shape=[jax.ShapeDtypeStruct((lp, w), F32), jax.ShapeDtypeStruct((lp, w), BF16)],
        compiler_params=_cparams(("parallel",)))(h0, mix, g_post, g_pre2)


def _final_call(h1, f, g_post, target):
    lp, w = h1.shape
    tr = BLK
    nb = lp // tr

    def body(h1_ref, f_ref, g_ref, t_ref, loss_ref, df_ref, dh_ref, dg_ref):
        i = pl.program_id(0)
        fv = f_ref[...]
        g = g_ref[...]
        _, fh = _rms_stats(fv)
        h2 = h1_ref[...] + fh * g
        diff = jnp.where(i > 0, h2 - t_ref[...], 0.0)
        part = 0.5 * jnp.sum(diff * diff, axis=0, keepdims=True) * (1.0 / w)
        _acc_rows(loss_ref, part, i)
        dh = diff * (1.0 / w)
        dh_ref[...] = dh
        df, dg = _rms_bwd(fv, g, dh)
        df_ref[...] = df.astype(BF16)
        _acc_rows(dg_ref, dg, i)

    t_spec = pl.BlockSpec((tr, w), lambda i: (jnp.maximum(i - 1, 0), 0))
    return pl.pallas_call(
        body, name="final_fwd_bwd", grid=(nb,),
        in_specs=[_row_spec(tr, w), _row_spec(tr, w), _vec_spec(w), t_spec],
        out_specs=[_vec_spec(w), _row_spec(tr, w), _row_spec(tr, w), _vec_spec(w)],
        out_shape=[jax.ShapeDtypeStruct((1, w), F32), jax.ShapeDtypeStruct((lp, w), BF16),
                   jax.ShapeDtypeStruct((lp, w), F32), jax.ShapeDtypeStruct((1, w), F32)],
        compiler_params=_cparams(("arbitrary",)))(h1, f, g_post, target)


def _mid_bwd_call(dh2, h1, dxn2, mix, g_pre2, g_post):
    lp, w = h1.shape
    tr = _pick(lp, [384, 128])

    def body(dh2_ref, h1_ref, dxn_ref, mix_ref, g2_ref, gp_ref, dh1_ref, dmix_ref, dg2_ref, dgp_ref):
        i = pl.program_id(0)
        live = (i * tr + _iota((tr, 1), 0)) >= PAD
        dx, dg2 = _rms_bwd(h1_ref[...], g2_ref[...], dxn_ref[...])
        dh1 = jnp.where(live, dh2_ref[...] + dx, 0.0)
        dh1_ref[...] = dh1
        dmix, dgp = _rms_bwd(mix_ref[...], gp_ref[...], dh1)
        dmix_ref[...] = jnp.where(live, dmix, 0.0).astype(BF16)
        _acc_rows(dg2_ref, dg2, i)
        _acc_rows(dgp_ref, dgp, i)

    rs = _row_spec(tr, w)
    return pl.pallas_call(
        body, name="mid_bwd", grid=(lp // tr,),
        in_specs=[rs, rs, rs, rs, _vec_spec(w), _vec_spec(w)],
        out_specs=[rs, rs, _vec_spec(w), _vec_spec(w)],
        out_shape=[jax.ShapeDtypeStruct((lp, w), F32), jax.ShapeDtypeStruct((lp, w), BF16),
                   jax.ShapeDtypeStruct((1, w), F32), jax.ShapeDtypeStruct((1, w), F32)],
        compiler_params=_cparams(("arbitrary",)))(dh2, h1, dxn2, mix, g_pre2, g_post)


def _norm_bwd_call(x, g, dy_arr, dy_col, name, res=None):
    lp, w = x.shape
    tr = _pick(lp, [384, 128])
    has_res = res is not None

    def body(x_ref, g_ref, dy_ref, *rest):
        i = pl.program_id(0)
        live = (i * tr + _iota((tr, 1), 0)) >= PAD
        dx, dg = _rms_bwd(x_ref[...], g_ref[...], dy_ref[...])
        if has_res:
            dx = dx + rest[0][...]
        out_ref, dg_ref = rest[-2], rest[-1]
        out_ref[...] = jnp.where(live, dx, 0.0)
        _acc_rows(dg_ref, dg, i)

    rs = _row_spec(tr, w)
    ins = [rs, _vec_spec(w), _row_spec(tr, w, dy_col)] + ([rs] if has_res else [])
    args = [x, g, dy_arr] + ([res] if has_res else [])
    return pl.pallas_call(
        body, name=name, grid=(lp // tr,), in_specs=ins, out_specs=[rs, _vec_spec(w)],
        out_shape=[jax.ShapeDtypeStruct((lp, w), F32), jax.ShapeDtypeStruct((1, w), F32)],
        compiler_params=_cparams(("arbitrary",)))(*args)


def _shift_down(cur, prev_tail, s, rows):
    if s == 0:
        return cur
    prev = jnp.tile(prev_tail, (BLK // 8, 1))
    return jnp.where(rows >= s, pltpu.roll(cur, s, 0), pltpu.roll(prev, s, 0))


def _shift_up(cur, next_head, s, rows):
    if s == 0:
        return cur
    nxt = jnp.tile(next_head, (BLK // 8, 1))
    return jnp.where(rows < BLK - s, pltpu.roll(cur, BLK - s, 0), pltpu.roll(nxt, BLK - s, 0))


def _gelu_tanh(x):
    c = math.sqrt(2.0 / math.pi)
    t = jnp.tanh(c * (x + 0.044715 * x * x * x))
    return 0.5 * x * (1.0 + t), t


def _conv_fwd_call(src, col0, width, cw, w8, b, taps, *, gate_src=None, gate_col0=0, name):
    lp = src.shape[0]
    nb, nc = lp // BLK, width // cw
    cb0 = col0 // cw
    ffn = gate_src is not None

    def body(x_ref, w_ref, b_ref, *rest):
        if ffn:
            u_ref, y_ref, a_ref, tail = rest
        else:
            y_ref, a_ref, tail = rest
        i = pl.program_id(1)

        @pl.when(i == 0)
        def _():
            tail[...] = jnp.zeros_like(tail)

        cur = x_ref[...]
        rows = _iota((BLK, cw), 0)
        y = b_ref[...] + w_ref[taps - 1:taps, :] * cur
        pt = tail[...]
        for s in range(1, taps):
            y = y + w_ref[taps - 1 - s:taps - s, :] * _shift_down(cur, pt, s, rows)
        tail[...] = cur[BLK - 8:, :]
        y_ref[...] = y
        if ffn:
            ge, _ = _gelu_tanh(y)
            a_ref[...] = (ge * u_ref[...]).astype(BF16)
        else:
            live = (i * BLK + rows) >= PAD
            a_ref[...] = jnp.where(live, y * _sigmoid(y), 0.0)

    blk = lambda c0: pl.BlockSpec((BLK, cw), lambda j, i: (i, c0 + j))
    ins = [blk(cb0), pl.BlockSpec((8, cw), lambda j, i: (0, j)), pl.BlockSpec((1, cw), lambda j, i: (0, j))]
    args = [src, w8, b]
    if ffn:
        ins.append(blk(gate_col0 // cw))
        args.append(gate_src)
    return pl.pallas_call(
        body, name=name, grid=(nc, nb), in_specs=ins, out_specs=[blk(0), blk(0)],
        out_shape=[jax.ShapeDtypeStruct((lp, width), F32),
                   jax.ShapeDtypeStruct((lp, width), BF16 if ffn else F32)],
        scratch_shapes=[pltpu.VMEM((8, cw), F32)],
        compiler_params=_cparams(("parallel", "arbitrary")))(*args)


def _conv_bwd_call(src, col0, width, cw, w8, taps, ypre, dact, *, gate_src=None, gate_col0=0, name):
    lp = src.shape[0]
    nb, nc = lp // BLK, width // cw
    cb0 = col0 // cw
    ffn = gate_src is not None

    def body(x_ref, w_ref, y_ref, d_ref, *rest):
        if ffn:
            u_ref, dx_ref, du_ref, dw_ref, db_ref, head = rest
        else:
            dx_ref, dw_ref, db_ref, head = rest
        step = pl.program_id(1)
        i = nb - 1 - step

        @pl.when(step == 0)
        def _():
            head[...] = jnp.zeros_like(head)

        rows = _iota((BLK, cw), 0)
        live = (i * BLK + rows) >= PAD
        y = y_ref[...]
        d = d_ref[...]
        if ffn:
            ge, t = _gelu_tanh(y)
            c = math.sqrt(2.0 / math.pi)
            dge = 0.5 * (1.0 + t) + 0.5 * y * (1.0 - t * t) * c * (1.0 + 3.0 * 0.044715 * y * y)
            u = u_ref[...]
            du_ref[...] = jnp.where(live, d * ge, 0.0).astype(BF16)
            dy = jnp.where(live, d * u * dge, 0.0)
        else:
            sg = _sigmoid(y)
            dy = jnp.where(live, d * sg * (1.0 + y * (1.0 - sg)), 0.0)
        x = x_ref[...]
        nh = head[...]
        dx = jnp.zeros_like(dy)
        dws = []
        for s in range(taps):
            sh = _shift_up(dy, nh, s, rows)
            dx = dx + w_ref[taps - 1 - s:taps - s, :] * sh
            dws.append(jnp.sum(x * sh, axis=0, keepdims=True))
        head[...] = dy[:8, :]
        dx_ref[...] = jnp.where(live, dx, 0.0).astype(BF16)
        dw = jnp.concatenate([dws[taps - 1 - k] for k in range(taps)]
                             + [jnp.zeros((8 - taps, cw), F32)], axis=0)
        _acc_rows(dw_ref, dw, step)
        _acc_rows(db_ref, jnp.sum(dy, axis=0, keepdims=True), step)

    blk = lambda c0: pl.BlockSpec((BLK, cw), lambda j, s: (nb - 1 - s, c0 + j))
    ins = [blk(cb0), pl.BlockSpec((8, cw), lambda j, s: (0, j)), blk(0), blk(0)]
    args = [src, w8, ypre, dact]
    outs = [blk(0)]
    oshape = [jax.ShapeDtypeStruct((lp, width), BF16)]
    if ffn:
        ins.append(blk(gate_col0 // cw))
        args.append(gate_src)
        outs.append(blk(0))
        oshape.append(jax.ShapeDtypeStruct((lp, width), BF16))
    outs += [pl.BlockSpec((8, cw), lambda j, s: (0, j)), pl.BlockSpec((1, cw), lambda j, s: (0, j))]
    oshape += [jax.ShapeDtypeStruct((8, width), F32), jax.ShapeDtypeStruct((1, width), F32)]
    return pl.pallas_call(
        body, name=name, grid=(nc, nb), in_specs=ins, out_specs=outs, out_shape=oshape,
        scratch_shapes=[pltpu.VMEM((8, cw), F32)],
        compiler_params=_cparams(("parallel", "arbitrary")))(*args)


SB_GROUP = 4


def _sb_scores(qm_h, kb):
    z = _dot(qm_h, kb, 1, 1)
    sp = jnp.maximum(z, 0.0) + jnp.log(1.0 + jnp.exp(-jnp.abs(z)))
    return z - sp, -sp


def _sb_valid(i, off, width):
    kpos = off + _iota((BLK, width), 1)
    qpos = i * BLK + _iota((BLK, width), 0)
    return (kpos < qpos) & (kpos >= PAD)


def _sb_groups(i):
    edge = jnp.maximum((i + SB_GROUP - 1) // SB_GROUP - 1, 0)
    return edge, pl.multiple_of(BLK + edge * (SB_GROUP * BLK), BLK)


def _tri2(cond):
    t = cond.astype(BF16)
    return jnp.concatenate([t, t], axis=0)


def _dot_tri(v, tri2):
    hi = v.astype(BF16)
    lo = (v - hi.astype(F32)).astype(BF16)
    return _dot(jnp.concatenate([hi, lo], axis=1), tri2)


def _sb_fwd_call(proj):
    lp = proj.shape[0]
    nb = lp // BLK
    assert (nb - 1) % SB_GROUP == 0
    scale = 1.0 / math.sqrt(HEAD_DIM)

    def body(q_ref, k_ref, v_ref, o_ref, tl_ref):
        i = pl.program_id(1)
        lane = _iota((BLK, BLK), 1)
        row = _iota((BLK, BLK), 0)
        low = lane < HEAD_DIM
        q = q_ref[...] * scale
        qm = (jnp.where(low, q, 0.0).astype(BF16), jnp.where(low, 0.0, q).astype(BF16))
        tri = _tri2(row > lane)

        def chunk(off, nsub, masked, carry):
            width = nsub * BLK
            sls = [slice(b * BLK, (b + 1) * BLK) for b in range(nsub)]
            kb = k_ref[pl.ds(off, width), :].astype(BF16)
            vb = v_ref[pl.ds(off, width), :].astype(BF16)
            valid = _sb_valid(i, off, width) if masked else None
            sc = [_sb_scores(qm[h], kb) for h in range(2)]
            lbs = [s[0] for s in sc]
            lks = [jnp.where(valid, s[1], 0.0) if masked else s[1] for s in sc]
            afters = [[_dot_tri(lks[h][:, sl], tri) for sl in sls] for h in range(2)]
            runs, wss = [], []
            for h in range(2):
                run = carry[2 * h]
                ws = [None] * nsub
                for b in reversed(range(nsub)):
                    wb = jnp.exp(lbs[h][:, sls[b]] + afters[h][b] + run)
                    if masked:
                        wb = jnp.where(valid[:, sls[b]], wb, 0.0)
                    ws[b] = wb.astype(BF16)
                    run = run + jnp.sum(lks[h][:, sls[b]], axis=1, keepdims=True)
                runs.append(run)
                wss.append(ws[0] if nsub == 1 else jnp.concatenate(ws, axis=1))
            return (runs[0], carry[1] + _dot(wss[0], vb), runs[1], carry[3] + _dot(wss[1], vb))

        edge, edge_off = _sb_groups(i)
        zc = jnp.zeros((BLK, 1), F32)
        za = jnp.zeros((BLK, BLK), F32)
        carry = chunk(edge_off, SB_GROUP, True, (zc, za, zc, za))

        def interior(t, carry):
            off = pl.multiple_of(BLK + (edge - 1 - t) * (SB_GROUP * BLK), BLK)
            return chunk(off, SB_GROUP, False, carry)

        carry = lax.fori_loop(0, edge, interior, carry)
        r0, o0, r1, o1 = chunk(0, 1, True, carry)
        o_ref[...] = jnp.where(low, o0, o1)
        tl_ref[...] = jnp.where(low, r0, r1)

    qc, kc, vc = C_Q // BLK, C_K // BLK, C_V // BLK
    blk = pl.BlockSpec((BLK, BLK), lambda p, i: (i, p))
    return pl.pallas_call(
        body, name="sb_fwd", grid=(N_HEADS // 2, nb),
        in_specs=[pl.BlockSpec((BLK, BLK), lambda p, i: (i, qc + p)),
                  pl.BlockSpec((lp, BLK), lambda p, i: (0, kc + p)),
                  pl.BlockSpec((lp, BLK), lambda p, i: (0, vc + p))],
        out_specs=[blk, blk],
        out_shape=[jax.ShapeDtypeStruct((lp, N_HEADS * HEAD_DIM), F32)] * 2,
        compiler_params=_cparams(("parallel", "arbitrary")))(proj, proj, proj)


def _sb_bwd_call(proj, tl, do):
    lp = proj.shape[0]
    nb = lp // BLK
    assert (nb - 1) % SB_GROUP == 0
    scale = 1.0 / math.sqrt(HEAD_DIM)

    def body(q_ref, k_ref, v_ref, tl_ref, do_ref, dq_ref, dk_ref, dv_ref, dk_acc, dv_acc):
        i = pl.program_id(1)

        @pl.when(i == 0)
        def _():
            dk_acc[...] = jnp.zeros_like(dk_acc)
            dv_acc[...] = jnp.zeros_like(dv_acc)

        lane = _iota((BLK, BLK), 1)
        row = _iota((BLK, BLK), 0)
        low = lane < HEAD_DIM
        q = q_ref[...] * scale
        dov = do_ref[...]
        qm = (jnp.where(low, q, 0.0).astype(BF16), jnp.where(low, 0.0, q).astype(BF16))
        dom = (jnp.where(low, dov, 0.0).astype(BF16), jnp.where(low, 0.0, dov).astype(BF16))
        tlv = tl_ref[...]
        tot = (tlv[:, 0:1], tlv[:, HEAD_DIM:HEAD_DIM + 1])
        tri_in = _tri2(row <= lane)
        tri_ex = _tri2(row < lane)

        def chunk(off, nsub, masked, carry):
            width = nsub * BLK
            sls = [slice(b * BLK, (b + 1) * BLK) for b in range(nsub)]
            cat = lambda parts: parts[0] if nsub == 1 else jnp.concatenate(parts, axis=1)
            kb = k_ref[pl.ds(off, width), :].astype(BF16)
            vb = v_ref[pl.ds(off, width), :].astype(BF16)
            valid = _sb_valid(i, off, width) if masked else None
            sc = [_sb_scores(qm[h], kb) for h in range(2)]
            dws = [_dot(dom[h], vb, 1, 1) for h in range(2)]
            lbs = [s[0] for s in sc]
            lks = [jnp.where(valid, s[1], 0.0) if masked else s[1] for s in sc]
            pins = [[_dot_tri(lks[h][:, sl], tri_in) for sl in sls] for h in range(2)]
            runs, wss, gss = [], [], []
            for h in range(2):
                run = carry[3 * h]
                ws, gs = [], []
                for b in range(nsub):
                    wb = jnp.exp(lbs[h][:, sls[b]] + (tot[h] - run - pins[h][b]))
                    if masked:
                        wb = jnp.where(valid[:, sls[b]], wb, 0.0)
                    ws.append(wb.astype(BF16))
                    gs.append(wb * dws[h][:, sls[b]])
                    run = run + jnp.sum(lks[h][:, sls[b]], axis=1, keepdims=True)
                runs.append(run)
                wss.append(cat(ws))
                gss.append(gs)
            gexs = [[_dot_tri(gss[h][b], tri_ex) for b in range(nsub)] for h in range(2)]
            gsums, dzs = [], []
            for h in range(2):
                gsum = carry[3 * h + 1]
                beta = jnp.exp(lbs[h])
                parts = []
                for b in range(nsub):
                    g, bt = gss[h][b], beta[:, sls[b]]
                    dzb = g * (1.0 - bt) - (gsum + gexs[h][b]) * bt
                    if masked:
                        dzb = jnp.where(valid[:, sls[b]], dzb, 0.0)
                    parts.append(dzb.astype(BF16))
                    gsum = gsum + jnp.sum(g, axis=1, keepdims=True)
                gsums.append(gsum)
                dzs.append(cat(parts))
            dk_acc[pl.ds(off, width), :] += _dot(dzs[0], qm[0], 0, 0) + _dot(dzs[1], qm[1], 0, 0)
            dv_acc[pl.ds(off, width), :] += _dot(wss[0], dom[0], 0, 0) + _dot(wss[1], dom[1], 0, 0)
            return (runs[0], gsums[0], carry[2] + _dot(dzs[0], kb),
                    runs[1], gsums[1], carry[5] + _dot(dzs[1], kb))

        edge, edge_off = _sb_groups(i)
        zc = jnp.zeros((BLK, 1), F32)
        za = jnp.zeros((BLK, BLK), F32)
        carry = chunk(0, 1, True, (zc, zc, za, zc, zc, za))

        def interior(t, carry):
            return chunk(pl.multiple_of(BLK + t * (SB_GROUP * BLK), BLK), SB_GROUP, False, carry)

        carry = lax.fori_loop(0, edge, interior, carry)
        res = chunk(edge_off, SB_GROUP, True, carry)
        dq_ref[...] = (jnp.where(low, res[2], res[5]) * scale).astype(BF16)

        @pl.when(i == nb - 1)
        def _():
            dk_ref[...] = dk_acc[...].astype(BF16)
            dv_ref[...] = dv_acc[...].astype(BF16)

    qc, kc, vc = C_Q // BLK, C_K // BLK, C_V // BLK
    blk = pl.BlockSpec((BLK, BLK), lambda p, i: (i, p))
    full = pl.BlockSpec((lp, BLK), lambda p, i: (0, p))
    w = N_HEADS * HEAD_DIM
    return pl.pallas_call(
        body, name="sb_bwd", grid=(N_HEADS // 2, nb),
        in_specs=[pl.BlockSpec((BLK, BLK), lambda p, i: (i, qc + p)),
                  pl.BlockSpec((lp, BLK), lambda p, i: (0, kc + p)),
                  pl.BlockSpec((lp, BLK), lambda p, i: (0, vc + p)),
                  blk, blk],
        out_specs=[blk, full, full],
        out_shape=[jax.ShapeDtypeStruct((lp, w), BF16)] * 3,
        scratch_shapes=[pltpu.VMEM((lp, BLK), F32), pltpu.VMEM((lp, BLK), F32)],
        compiler_params=_cparams(("parallel", "arbitrary")))(proj, proj, proj, tl, do)


def _log1p(e):
    u = 1.0 + e
    return jnp.where(u == 1.0, e, jnp.log(u) * e / jnp.where(u == 1.0, 1.0, u - 1.0))


def _ssd_common(c, dtr, bias, alog):
    row = _iota((BLK, BLK), 0)
    lane = _iota((BLK, BLK), 1)
    live = ((c * BLK + row) >= PAD) & (lane < N_HEADS)
    pre = dtr + bias
    dt = jnp.where(live, jnp.maximum(pre, 0.0) + _log1p(jnp.exp(-jnp.abs(pre))), 0.0)
    a_neg = -jnp.exp(alog)
    a = dt * a_neg
    t_in = (lane <= row).astype(BF16)
    cs = _dot_sel_l(t_in, a)
    cs_t = cs.T
    cs_end = cs[BLK - 1:BLK, :]
    e = jnp.exp(cs)
    f = jnp.exp(cs_end - cs)
    xp = ((_iota((BLK, SSD_INNER), 1) // HEAD_DIM) == _iota((BLK, SSD_INNER), 0)).astype(BF16)
    xp_t = ((_iota((SSD_INNER, BLK), 0) // HEAD_DIM) == _iota((SSD_INNER, BLK), 1)).astype(BF16)
    decay_col = _dot_sel_l(xp_t, jnp.exp(cs_t))[:, BLK - 1:BLK]
    return dict(live=live, pre=pre, dt=dt, a_neg=a_neg, cs=cs, cs_t=cs_t, e=e, f=f, xp=xp, xp_t=xp_t,
                decay_col=decay_col, row=row, lane=lane,
                dt_x=_dot_sel_r(dt, xp), e_x=_dot_sel_r(e, xp), f_x=_dot_sel_r(f, xp))


def _ssd_ldec(q, h):
    diff = q["cs"][:, h:h + 1] - q["cs_t"][h:h + 1, :]
    causal = q["row"] >= q["lane"]
    return jnp.where(causal, jnp.exp(jnp.where(causal, diff, 0.0)), 0.0)


def _ssd_fwd_call(xbc, proj, bias, alog, d_x, norm_g):
    lp = xbc.shape[0]
    nb = lp // BLK
    gw = SSD_INNER // SSD_GROUPS
    ppg = gw // BLK

    def body(xbc_ref, dtr_ref, z_ref, bias_ref, alog_ref, dx_ref, ng_ref, yb_ref, ypre_ref, sprev_ref, s_ref):
        c = pl.program_id(0)

        @pl.when(c == 0)
        def _():
            s_ref[...] = jnp.zeros_like(s_ref)

        q = _ssd_common(c, dtr_ref[...], bias_ref[...], alog_ref[...])
        x = xbc_ref[:, 0:SSD_INNER]
        xd = x * q["dt_x"]
        low = q["lane"] < HEAD_DIM
        s_old = s_ref[...]
        sprev_ref[...] = s_old
        xdf = (xd * q["f_x"]).astype(BF16)
        for g in range(SSD_GROUPS):
            bg = xbc_ref[:, SSD_INNER + g * SSD_STATE:SSD_INNER + (g + 1) * SSD_STATE].astype(BF16)
            cg = xbc_ref[:, SSD_INNER + (SSD_GROUPS + g) * SSD_STATE:
                         SSD_INNER + (SSD_GROUPS + g + 1) * SSD_STATE].astype(BF16)
            cb = _dot(cg, bg, 1, 1)
            gs = slice(g * gw, (g + 1) * gw)
            y_off = _dot(cg, s_old[gs, :].astype(BF16), 1, 1) * q["e_x"][:, gs]
            s_ref[gs, :] = s_old[gs, :] * q["decay_col"][gs, :] + _dot(xdf[:, gs], bg, 0, 0)
            for pr in range(ppg):
                cols = slice(g * gw + pr * BLK, g * gw + (pr + 1) * BLK)
                xd_p = xd[:, cols]
                acc = y_off[:, pr * BLK:(pr + 1) * BLK]
                for hh in range(2):
                    h = (g * gw + pr * BLK) // HEAD_DIM + hh
                    m = (cb * _ssd_ldec(q, h)).astype(BF16)
                    xm = jnp.where(low, xd_p, 0.0) if hh == 0 else jnp.where(low, 0.0, xd_p)
                    acc = acc + _dot(m, xm.astype(BF16))
                ypre_ref[:, cols] = acc
        ypre = ypre_ref[...] + x * dx_ref[...]
        ypre_ref[...] = ypre
        z = z_ref[...]
        yg = ypre * (z * _sigmoid(z))
        _, yh = _rms_stats(yg)
        yb_ref[...] = (yh * ng_ref[...]).astype(BF16)

    row = lambda w, col: pl.BlockSpec((BLK, w), lambda c: (c, col))
    vec = lambda w: pl.BlockSpec((1, w), lambda c: (0, 0))
    return pl.pallas_call(
        body, name="ssd_fwd", grid=(nb,),
        in_specs=[row(XBC, 0), row(BLK, C_DT // BLK), row(SSD_INNER, 0), vec(BLK), vec(BLK),
                  vec(SSD_INNER), vec(SSD_INNER)],
        out_specs=[row(SSD_INNER, 0), row(SSD_INNER, 0),
                   pl.BlockSpec((None, SSD_INNER, SSD_STATE), lambda c: (c, 0, 0))],
        out_shape=[jax.ShapeDtypeStruct((lp, SSD_INNER), BF16), jax.ShapeDtypeStruct((lp, SSD_INNER), F32),
                   jax.ShapeDtypeStruct((nb, SSD_INNER, SSD_STATE), F32)],
        scratch_shapes=[pltpu.VMEM((SSD_INNER, SSD_STATE), F32)],
        compiler_params=_cparams(("arbitrary",)))(xbc, proj, proj, bias, alog, d_x, norm_g)


def _ssd_bwd_call(dycat, ypre, xbc, proj, sprev, bias, alog, d_x, norm_g):
    lp = xbc.shape[0]
    nb = lp // BLK
    gw = SSD_INNER // SSD_GROUPS
    ppg = gw // BLK

    def body(dy_ref, ypre_ref, xbc_ref, dtr_ref, z_ref, sp_ref, bias_ref, alog_ref, dxp_ref, ng_ref,
             dz_ref, dxbc_ref, ddt_ref, dng_ref, dd_ref, dal_ref, dbi_ref, ds_ref, dxd_ref):
        step = pl.program_id(0)
        c = nb - 1 - step

        @pl.when(step == 0)
        def _():
            ds_ref[...] = jnp.zeros_like(ds_ref)

        q = _ssd_common(c, dtr_ref[...], bias_ref[...], alog_ref[...])
        row, lane = q["row"], q["lane"]
        low = lane < HEAD_DIM
        rowlive = ((c * BLK + _iota((BLK, 1), 0)) >= PAD)
        x = xbc_ref[:, 0:SSD_INNER]
        xd = x * q["dt_x"]
        z = z_ref[...]
        sz = _sigmoid(z)
        silu = z * sz
        ypre = ypre_ref[...]
        dyg, dng = _rms_bwd(ypre * silu, ng_ref[...], dy_ref[...])
        _acc_rows(dng_ref, dng, step)
        dyp = dyg * silu
        dz_ref[...] = jnp.where(rowlive, dyg * ypre * (sz * (1.0 + z * (1.0 - sz))), 0.0).astype(BF16)
        _acc_rows(dd_ref, jnp.sum(dyp * x, axis=0, keepdims=True), step)
        dye = dyp * q["e_x"]
        xdf = xd * q["f_x"]
        s_prev = sp_ref[...]
        ds_old = ds_ref[...]
        qrow = jnp.zeros((BLK, BLK), F32)
        qcol_t = jnp.zeros((BLK, BLK), F32)
        red_e = []
        red_f = []
        for g in range(SSD_GROUPS):
            gs = slice(g * gw, (g + 1) * gw)
            bsl = slice(SSD_INNER + g * SSD_STATE, SSD_INNER + (g + 1) * SSD_STATE)
            csl = slice(SSD_INNER + (SSD_GROUPS + g) * SSD_STATE, SSD_INNER + (SSD_GROUPS + g + 1) * SSD_STATE)
            bg = xbc_ref[:, bsl].astype(BF16)
            cg = xbc_ref[:, csl].astype(BF16)
            sg = s_prev[gs, :].astype(BF16)
            dsg = ds_old[gs, :].astype(BF16)
            cb = _dot(cg, bg, 1, 1)
            bds = _dot(bg, dsg, 1, 1)
            y_off = _dot(cg, sg, 1, 1) * q["e_x"][:, gs]
            red_e.append(dyp[:, gs] * y_off)
            red_f.append(xd[:, gs] * bds * q["f_x"][:, gs])
            dc = _dot(dye[:, gs].astype(BF16), sg)
            db = _dot(xdf[:, gs].astype(BF16), dsg)
            ds_ref[gs, :] = ds_old[gs, :] * q["decay_col"][gs, :] + _dot(dye[:, gs].astype(BF16), cg, 0, 0)
            dcb = jnp.zeros((BLK, BLK), F32)
            for pr in range(ppg):
                cols = slice(g * gw + pr * BLK, g * gw + (pr + 1) * BLK)
                xd_p = xd[:, cols].astype(BF16)
                dy_p = dyp[:, cols]
                acc = q["f_x"][:, cols] * bds[:, pr * BLK:(pr + 1) * BLK]
                for hh in range(2):
                    h = (g * gw + pr * BLK) // HEAD_DIM + hh
                    ld = _ssd_ldec(q, h)
                    m = cb * ld
                    dym = (jnp.where(low, dy_p, 0.0) if hh == 0 else jnp.where(low, 0.0, dy_p)).astype(BF16)
                    dm = jnp.where(row >= lane, _dot(dym, xd_p, 1, 1), 0.0)
                    acc = acc + _dot(m.astype(BF16), dym, 0, 0)
                    dcb = dcb + dm * ld
                    qq = dm * m
                    qrow = qrow + jnp.where(lane == h, jnp.sum(qq, axis=1, keepdims=True), 0.0)
                    qcol_t = qcol_t + jnp.where(row == h, jnp.sum(qq, axis=0, keepdims=True), 0.0)
                dxd_ref[:, cols] = acc
            dcbb = dcb.astype(BF16)
            dxbc_ref[:, bsl] = jnp.where(rowlive, db + _dot(dcbb, cg, 0, 0), 0.0)
            dxbc_ref[:, csl] = jnp.where(rowlive, dc + _dot(dcbb, bg), 0.0)
        dxd = dxd_ref[...]
        dxbc_ref[:, 0:SSD_INNER] = jnp.where(rowlive, dxd * q["dt_x"] + dyp * dxp_ref[...], 0.0)
        xp_t = q["xp_t"]
        fw = _dot_sel_r(jnp.concatenate(red_f, axis=1), xp_t)
        dcs = qrow - qcol_t.T + _dot_sel_r(jnp.concatenate(red_e, axis=1), xp_t) - fw
        end_f = jnp.sum(fw, axis=0, keepdims=True)
        sds = jnp.sum(ds_old * s_prev, axis=1, keepdims=True)
        per_head = _dot_sel_l(q["xp"], jnp.broadcast_to(sds, (SSD_INNER, BLK)))
        end_e = per_head.T[0:1, :] * jnp.exp(q["cs"][BLK - 1:BLK, :])
        dcs = dcs + jnp.where(row == BLK - 1, end_f + end_e, 0.0)
        t_up = (lane >= row).astype(BF16)
        da = _dot_sel_l(t_up, dcs)
        ddt = da * q["a_neg"] + _dot_sel_r(dxd * x, xp_t)
        _acc_rows(dal_ref, jnp.sum(da * q["dt"] * q["a_neg"], axis=0, keepdims=True), step)
        ddtr = jnp.where(q["live"], ddt * _sigmoid(q["pre"]), 0.0)
        ddt_ref[...] = ddtr.astype(BF16)
        _acc_rows(dbi_ref, jnp.sum(ddtr, axis=0, keepdims=True), step)

    row_s = lambda w, col: pl.BlockSpec((BLK, w), lambda s: (nb - 1 - s, col))
    vec = lambda w: pl.BlockSpec((1, w), lambda s: (0, 0))
    return pl.pallas_call(
        body, name="ssd_bwd", grid=(nb,),
        in_specs=[row_s(SSD_INNER, 0), row_s(SSD_INNER, 0), row_s(XBC, 0), row_s(BLK, C_DT // BLK),
                  row_s(SSD_INNER, 0), pl.BlockSpec((None, SSD_INNER, SSD_STATE), lambda s: (nb - 1 - s, 0, 0)),
                  vec(BLK), vec(BLK), vec(SSD_INNER), vec(SSD_INNER)],
        out_specs=[row_s(SSD_INNER, 0), row_s(XBC, 0), row_s(BLK, 0),
                   vec(SSD_INNER), vec(SSD_INNER), vec(BLK), vec(BLK)],
        out_shape=[jax.ShapeDtypeStruct((lp, SSD_INNER), BF16), jax.ShapeDtypeStruct((lp, XBC), F32),
                   jax.ShapeDtypeStruct((lp, BLK), BF16),
                   jax.ShapeDtypeStruct((1, SSD_INNER), F32), jax.ShapeDtypeStruct((1, SSD_INNER), F32),
                   jax.ShapeDtypeStruct((1, BLK), F32), jax.ShapeDtypeStruct((1, BLK), F32)],
        scratch_shapes=[pltpu.VMEM((SSD_INNER, SSD_STATE), F32), pltpu.VMEM((BLK, SSD_INNER), F32)],
        compiler_params=_cparams(("arbitrary",)))(dycat, ypre, xbc, proj, proj, sprev, bias, alog, d_x, norm_g)


def _pad_rows8(w):
    return jnp.pad(w, ((0, 8 - w.shape[0]), (0, 0)))


def _pad_lanes(v, n=BLK):
    return jnp.pad(v, ((0, 0), (0, n - v.shape[1])))


def _local_step(x, target, wt):
    seq = x.shape[0]
    lp = seq + BLK
    tm = _pick(lp, [1408, 768, 384, 128])
    tkr = _pick(lp, [384, 128])
    h0 = jnp.concatenate([jnp.zeros((PAD, D_MODEL), F32), wt["meta"], x], axis=0)
    bias = _pad_lanes(wt["ssd_dt_bias"])
    alog = _pad_lanes(wt["ssd_a_log"])
    d_x = jnp.repeat(wt["ssd_d"], HEAD_DIM, axis=1)
    cw8 = _pad_rows8(wt["ssd_conv_w"])
    fw8 = _pad_rows8(wt["ffn_conv_w"])
    fcw = D_FF // 2

    xn1 = _rms_fwd_call(h0, wt["mix_pre_g"], "norm1")
    proj = _mm(xn1, wt["w_in"], tm=tm, tn=1152, tk=D_MODEL, name="mm_proj")[0]
    conv_pre, xbc = _conv_fwd_call(proj, C_XBC, XBC, 512, cw8, wt["ssd_conv_b"], 4, name="ssd_conv_fwd")
    y_ssd, ypre, sprev = _ssd_fwd_call(xbc, proj, bias, alog, d_x, wt["ssd_norm_g"])
    o, tl = _sb_fwd_call(proj)
    y_sb = _rms_fwd_call(o, wt["sb_norm_g"], "sb_norm")
    ycat = jnp.concatenate([y_ssd, y_sb], axis=1)
    mix = _mm(ycat, wt["w_out"], tm=tm, tn=1024, tk=2048, name="mm_mix")[0]
    h1, xn2 = _mid_fwd_call(h0, mix, wt["mix_post_g"], wt["ffn_pre_g"])
    gu = _mm(xn2, wt["w_up"], tm=tm, tn=1408, tk=D_MODEL, name="mm_up")[0]
    gpre, act = _conv_fwd_call(gu, 0, D_FF, fcw, fw8, wt["ffn_conv_b"], 3, gate_src=gu, gate_col0=D_FF,
                               name="ffn_conv_fwd")
    f = _mm(act, wt["w_down"], tm=tm, tn=1024, tk=1408, name="mm_down")[0]
    loss_row, df, dh2, dg_ffn_post = _final_call(h1, f, wt["ffn_post_g"], target)

    dact = _mm(df, wt["w_down"], tb=True, tm=tm, tn=1408, tk=D_MODEL, name="mm_dact")[0]
    dw_down, dw_down_b = _mm(act, df, ta=True, tm=1408, tn=1024, tk=tkr, extra_bf16=True, name="mm_dw_down")
    dgate, dup, dfcw, dfcb = _conv_bwd_call(gu, 0, D_FF, fcw, fw8, 3, gpre, dact, gate_src=gu, gate_col0=D_FF,
                                            name="ffn_conv_bwd")
    dgu = jnp.concatenate([dgate, dup], axis=1)
    dxn2 = _mm(dgu, wt["w_up"], tb=True, tm=tm, tn=1024, tk=1408, name="mm_dxn2")[0]
    dw_up, dw_up_b = _mm(xn2, dgu, ta=True, tm=1024, tn=1408, tk=tkr, nsplit=N_CHIPS, extra_bf16=True,
                         name="mm_dw_up")
    dh1, dmix, dg_ffn_pre, dg_mix_post = _mid_bwd_call(dh2, h1, dxn2, mix, wt["ffn_pre_g"], wt["mix_post_g"])
    dycat = _mm(dmix, wt["w_out"], tb=True, tm=tm, tn=1024, tk=D_MODEL, name="mm_dycat")[0]
    dw_out, dw_out_b = _mm(ycat, dmix, ta=True, tm=1024, tn=1024, tk=tkr, extra_bf16=True, name="mm_dw_out")
    do, dg_sb = _norm_bwd_call(o, wt["sb_norm_g"], dycat, 1, "sb_norm_bwd")
    dq, dk, dv = _sb_bwd_call(proj, tl, do)
    dz, dxbc_act, ddt, dg_ssd, dd_x, dalog, dbias = _ssd_bwd_call(
        dycat, ypre, xbc, proj, sprev, bias, alog, d_x, wt["ssd_norm_g"])
    dxbc, dcw, dcb = _conv_bwd_call(proj, C_XBC, XBC, 512, cw8, 4, conv_pre, dxbc_act, name="ssd_conv_bwd")
    dproj = jnp.concatenate([dz, dxbc, ddt, dq, dk, dv], axis=1)
    dxn1 = _mm(dproj, wt["w_in"], tb=True, tm=tm, tn=1024, tk=1152, name="mm_dxn1")[0]
    dw_in = _mm(xn1, dproj, ta=True, tm=1024, tn=1152, tk=tkr, name="mm_dw_in")[0]
    dh0, dg_pre = _norm_bwd_call(h0, wt["mix_pre_g"], dxn1, 0, "norm1_bwd", res=dh1)

    small = {
        "meta_tokens": dh0[PAD:BLK], "mix_pre_g": dg_pre, "ssd_conv_w": dcw[:4], "ssd_conv_b": dcb,
        "ssd_dt_bias": dbias[:, :N_HEADS], "ssd_a_log": dalog[:, :N_HEADS],
        "ssd_d": jnp.sum(dd_x.reshape(N_HEADS, HEAD_DIM), axis=1)[None],
        "ssd_norm_g": dg_ssd, "sb_norm_g": dg_sb, "mix_post_g": dg_mix_post, "ffn_pre_g": dg_ffn_pre,
        "ffn_conv_w": dfcw[:3], "ffn_conv_b": dfcb, "ffn_post_g": dg_ffn_post,
    }
    big = {"w_in": dw_in, "w_out": (dw_out, dw_out_b), "w_up": (dw_up, dw_up_b), "w_down": (dw_down, dw_down_b)}
    return loss_row, dh0[BLK:], small, big


def _adamw_call(w, g, m, v, name):
    rows, cols = w.shape
    tr = 256 if rows % 256 == 0 else (352 if rows % 352 == 0 else rows)
    c1 = 1.0 - ADAM_B1 ** ADAM_STEP
    c2 = 1.0 - ADAM_B2 ** ADAM_STEP

    def body(w_ref, g_ref, m_ref, v_ref, d_ref, mo_ref, vo_ref):
        gv = g_ref[...]
        m2 = ADAM_B1 * m_ref[...] + (1.0 - ADAM_B1) * gv
        v2 = ADAM_B2 * v_ref[...] + (1.0 - ADAM_B2) * (gv * gv)
        d_ref[...] = -ADAM_LR * ((m2 / c1) / (jnp.sqrt(v2 / c2) + ADAM_EPS) + ADAM_WD * w_ref[...])
        mo_ref[...] = m2
        vo_ref[...] = v2

    spec = pl.BlockSpec((tr, cols), lambda i: (i, 0))
    return pl.pallas_call(
        body, name=name, grid=(rows // tr,), in_specs=[spec] * 4, out_specs=[spec] * 3,
        out_shape=[jax.ShapeDtypeStruct((rows, cols), F32)] * 3,
        compiler_params=_cparams(("parallel",)))(w, g, m, v)


ANY = pl.BlockSpec(memory_space=pl.ANY)


def _place():
    x, y, c = lax.axis_index("x"), lax.axis_index("y"), lax.axis_index("c")
    chips = [(1 - x, y), (x, 1 - y), (1 - x, 1 - y)]
    return x, y, c, chips


def _half(c, h):
    return pl.ds(pl.multiple_of(c * h, 8), h)


def _allgather_call(shards):
    n = len(shards)

    def body(*refs):
        ins, outs = refs[:n], refs[n:2 * n]
        send_i, recv_i, send_d, recv_d, loc = refs[2 * n:]
        x, y, c, chips = _place()
        me = 2 * x + y
        local = [pltpu.make_async_copy(ins[a], outs[a].at[me], loc.at[a]) for a in range(n)]
        for cp in local:
            cp.start()
        sends = []
        for a in range(n):
            h = shards[a].shape[0] // 2
            for j, chip in enumerate(chips):
                cp = pltpu.make_async_remote_copy(
                    src_ref=ins[a].at[_half(c, h)], dst_ref=outs[a].at[me, _half(c, h)],
                    send_sem=send_i.at[3 * a + j], recv_sem=recv_i.at[3 * a + j],
                    device_id=(*chip, c), device_id_type=MESH)
                cp.start()
                sends.append(cp)
        for a in range(n):
            h = shards[a].shape[0] // 2
            for j, chip in enumerate(chips):
                src = 2 * chip[0] + chip[1]
                landed = outs[a].at[src, _half(c, h)]
                pltpu.make_async_remote_copy(
                    src_ref=landed, dst_ref=landed, send_sem=send_i.at[3 * a + j], recv_sem=recv_i.at[3 * a + j],
                    device_id=(*chip, c), device_id_type=MESH).wait_recv()
                cp = pltpu.make_async_remote_copy(
                    src_ref=landed, dst_ref=landed, send_sem=send_d.at[3 * a + j], recv_sem=recv_d.at[3 * a + j],
                    device_id=(x, y, 1 - c), device_id_type=MESH)
                cp.start()
                sends.append(cp)
        for a in range(n):
            h = shards[a].shape[0] // 2
            for j, chip in enumerate(chips):
                src = 2 * chip[0] + chip[1]
                other = outs[a].at[src, _half(1 - c, h)]
                pltpu.make_async_remote_copy(
                    src_ref=other, dst_ref=other, send_sem=send_d.at[3 * a + j], recv_sem=recv_d.at[3 * a + j],
                    device_id=(x, y, 1 - c), device_id_type=MESH).wait_recv()
        for cp in sends:
            cp.wait_send()
        for cp in local:
            cp.wait()

    return pl.pallas_call(
        body, name="allgather_weights", in_specs=[ANY] * n, out_specs=[ANY] * n,
        out_shape=[jax.ShapeDtypeStruct((N_CHIPS,) + s.shape, s.dtype) for s in shards],
        scratch_shapes=[pltpu.SemaphoreType.DMA((3 * n,))] * 4 + [pltpu.SemaphoreType.DMA((n,))],
    )(*shards)


def _pair_exchange_call(grads):
    n = len(grads)

    def body(*refs):
        ins, outs = refs[:n], refs[n:2 * n]
        send_d, recv_d = refs[2 * n:]
        x, y, c, _ = _place()
        cps = []
        for a in range(n):
            h = grads[a].shape[1] // 2
            cp = pltpu.make_async_remote_copy(
                src_ref=ins[a].at[:, _half(1 - c, h)], dst_ref=outs[a], send_sem=send_d.at[a], recv_sem=recv_d.at[a],
                device_id=(x, y, 1 - c), device_id_type=MESH)
            cp.start()
            cps.append(cp)
        for cp in cps:
            cp.wait()

    return pl.pallas_call(
        body, name="grad_pair_exchange", in_specs=[ANY] * n, out_specs=[ANY] * n,
        out_shape=[jax.ShapeDtypeStruct((N_CHIPS, g.shape[1] // 2, g.shape[2]), BF16) for g in grads],
        scratch_shapes=[pltpu.SemaphoreType.DMA((n,))] * 2,
    )(*grads)


def _pair_sum_call(own, got, c_idx, name):
    _, rows, cols = own.shape
    h = rows // 2
    th = _pick(h, [256, 176, 8])
    nt = h // th

    def body(c_ref, own_ref, got_ref, o_ref):
        o_ref[...] = (own_ref[...] + got_ref[...].astype(F32)).astype(BF16)

    return pl.pallas_call(
        body, name=name,
        grid_spec=pltpu.PrefetchScalarGridSpec(
            num_scalar_prefetch=1, grid=(N_CHIPS, nt),
            in_specs=[pl.BlockSpec((None, th, cols), lambda s, i, c_ref: (s, c_ref[0] * nt + i, 0)),
                      pl.BlockSpec((None, th, cols), lambda s, i, c_ref: (s, i, 0))],
            out_specs=pl.BlockSpec((None, th, cols), lambda s, i, c_ref: (s, i, 0))),
        out_shape=jax.ShapeDtypeStruct((N_CHIPS, h, cols), BF16),
        compiler_params=_cparams(("parallel", "parallel")))(c_idx, own, got)


def _chip_exchange_call(pairs):
    n = len(pairs)

    def body(*refs):
        ins, outs = refs[:n], refs[n:2 * n]
        send_i, recv_i, loc = refs[2 * n:]
        x, y, c, chips = _place()
        me = 2 * x + y
        local, sends = [], []
        for a in range(n):
            cp = pltpu.make_async_copy(ins[a].at[me], outs[a].at[me], loc.at[a])
            cp.start()
            local.append(cp)
            for j, chip in enumerate(chips):
                dst = 2 * chip[0] + chip[1]
                cp = pltpu.make_async_remote_copy(
                    src_ref=ins[a].at[dst], dst_ref=outs[a].at[me], send_sem=send_i.at[3 * a + j],
                    recv_sem=recv_i.at[3 * a + j], device_id=(*chip, c), device_id_type=MESH)
                cp.start()
                sends.append(cp)
        for a in range(n):
            for j, chip in enumerate(chips):
                src = 2 * chip[0] + chip[1]
                pltpu.make_async_remote_copy(
                    src_ref=ins[a].at[src], dst_ref=outs[a].at[src], send_sem=send_i.at[3 * a + j],
                    recv_sem=recv_i.at[3 * a + j], device_id=(*chip, c), device_id_type=MESH).wait_recv()
        for cp in sends:
            cp.wait_send()
        for cp in local:
            cp.wait()

    return pl.pallas_call(
        body, name="grad_chip_exchange", in_specs=[ANY] * n, out_specs=[ANY] * n,
        out_shape=[jax.ShapeDtypeStruct(p.shape, BF16) for p in pairs],
        scratch_shapes=[pltpu.SemaphoreType.DMA((3 * n,))] * 2 + [pltpu.SemaphoreType.DMA((n,))],
    )(*pairs)


def _chip_sum_call(parts, name):
    _, h, cols = parts.shape
    th = _pick(h, [256, 176, 8])

    def body(p_ref, o_ref):
        acc = p_ref[0].astype(F32)
        for s in range(1, N_CHIPS):
            acc = acc + p_ref[s].astype(F32)
        o_ref[...] = acc

    return pl.pallas_call(
        body, name=name, grid=(h // th,),
        in_specs=[pl.BlockSpec((N_CHIPS, th, cols), lambda i: (0, i, 0))],
        out_specs=pl.BlockSpec((th, cols), lambda i: (i, 0)),
        out_shape=jax.ShapeDtypeStruct((h, cols), F32),
        compiler_params=_cparams(("parallel",)))(parts)


def _half_exchange_call(halves):
    n = len(halves)

    def body(*refs):
        ins, outs = refs[:n], refs[n:2 * n]
        send_d, recv_d, loc = refs[2 * n:]
        x, y, c, _ = _place()
        cps = []
        for a in range(n):
            h = halves[a].shape[0]
            mine = outs[a].at[_half(c, h)]
            lc = pltpu.make_async_copy(ins[a], mine, loc.at[a])
            lc.start()
            rc = pltpu.make_async_remote_copy(
                src_ref=ins[a], dst_ref=mine, send_sem=send_d.at[a], recv_sem=recv_d.at[a],
                device_id=(x, y, 1 - c), device_id_type=MESH)
            rc.start()
            cps.append((lc, rc))
        for a, (lc, rc) in enumerate(cps):
            h = halves[a].shape[0]
            theirs = outs[a].at[_half(1 - c, h)]
            pltpu.make_async_remote_copy(
                src_ref=ins[a], dst_ref=theirs, send_sem=send_d.at[a], recv_sem=recv_d.at[a],
                device_id=(x, y, 1 - c), device_id_type=MESH).wait_recv()
            rc.wait_send()
            lc.wait()

    return pl.pallas_call(
        body, name="grad_half_exchange", in_specs=[ANY] * n, out_specs=[ANY] * n,
        out_shape=[jax.ShapeDtypeStruct((2 * hv.shape[0], hv.shape[1]), F32) for hv in halves],
        scratch_shapes=[pltpu.SemaphoreType.DMA((n,))] * 3,
    )(*halves)


def _allreduce_small_call(v):
    rows = v.shape[0]

    def body(v_ref, o_ref, gath, send_sems, recv_sems):
        x, y, c, chips = _place()
        me, sibling = (x, y, c), (x, y, 1 - c)

        def slot(px, py, pc):
            return gath.at[4 * px + 2 * py + pc]

        def copy(k, block, to, src=None):
            return pltpu.make_async_remote_copy(
                src_ref=slot(*block) if src is None else src, dst_ref=slot(*block),
                send_sem=send_sems.at[k], recv_sem=recv_sems.at[k], device_id=to, device_id_type=MESH)

        gath[4 * x + 2 * y + c] = v_ref[...]
        first = [copy(0, me, sibling, src=v_ref)]
        first += [copy(1 + j, me, (*chip, c), src=v_ref) for j, chip in enumerate(chips)]
        for cp in first:
            cp.start()
        passed = [copy(4 + j, (*chip, c), sibling) for j, chip in enumerate(chips)]
        for j, chip in enumerate(chips):
            copy(1 + j, (*chip, c), me).wait_recv()
            passed[j].start()
        copy(0, sibling, me).wait_recv()
        for j, chip in enumerate(chips):
            copy(4 + j, (*chip, 1 - c), me).wait_recv()
        for cp in first + passed:
            cp.wait_send()
        acc = gath[0]
        for d in range(1, 8):
            acc = acc + gath[d]
        o_ref[...] = acc

    vm = pl.BlockSpec(memory_space=pltpu.VMEM)
    return pl.pallas_call(
        body, name="allreduce_small", in_specs=[vm], out_specs=vm,
        out_shape=jax.ShapeDtypeStruct((rows, BLK), F32),
        scratch_shapes=[pltpu.VMEM((8, rows, BLK), F32), pltpu.SemaphoreType.DMA((7,)),
                        pltpu.SemaphoreType.DMA((7,))],
        compiler_params=pltpu.CompilerParams(vmem_limit_bytes=VMEM_LIMIT),
    )(v)


def _pack(arrs, min_rows=8):
    parts = []
    for a in arrs:
        flat = a.reshape(-1).astype(F32)
        parts.append(jnp.pad(flat, (0, (-flat.shape[0]) % BLK)))
    buf = jnp.concatenate(parts).reshape(-1, BLK)
    return jnp.pad(buf, ((0, (-buf.shape[0]) % min_rows), (0, 0)))


def _unpack(buf, shapes):
    out, r = [], 0
    for shp in shapes:
        n = math.prod(shp)
        nr = -(-n // BLK)
        out.append(buf[r:r + nr].reshape(-1)[:n].reshape(shp))
        r += nr
    return out


SMALL = ["meta_tokens", "mix_pre_g", "ssd_conv_w", "ssd_conv_b", "ssd_dt_bias", "ssd_a_log", "ssd_d", "ssd_norm_g",
         "sb_norm_g", "mix_post_g", "ffn_pre_g", "ffn_conv_w", "ffn_conv_b", "ffn_post_g"]
BIG = ["w_in", "w_out", "w_up", "w_down"]
WEIGHTS = ["meta_tokens", "mix_pre_g", "w_in", "ssd_conv_w", "ssd_conv_b", "ssd_dt_bias", "ssd_a_log", "ssd_d",
           "ssd_norm_g", "sb_norm_g", "w_out", "mix_post_g", "ffn_pre_g", "w_up", "ffn_conv_w", "ffn_conv_b",
           "w_down", "ffn_post_g"]
W_IN_SHARD = IN_COLS // N_CHIPS
W_IN_PAD = 1536


def kernel(x, meta_tokens, mix_pre_g, w_in, ssd_conv_w, ssd_conv_b, ssd_dt_bias, ssd_a_log, ssd_d, ssd_norm_g, sb_norm_g, w_out, mix_post_g, ffn_pre_g, w_up, ffn_conv_w, ffn_conv_b, w_down, ffn_post_g, loss_target, m_meta_tokens, m_mix_pre_g, m_w_in, m_ssd_conv_w, m_ssd_conv_b, m_ssd_dt_bias, m_ssd_a_log, m_ssd_d, m_ssd_norm_g, m_sb_norm_g, m_w_out, m_mix_post_g, m_ffn_pre_g, m_w_up, m_ffn_conv_w, m_ffn_conv_b, m_w_down, m_ffn_post_g, v_meta_tokens, v_mix_pre_g, v_w_in, v_ssd_conv_w, v_ssd_conv_b, v_ssd_dt_bias, v_ssd_a_log, v_ssd_d, v_ssd_norm_g, v_sb_norm_g, v_w_out, v_mix_post_g, v_ffn_pre_g, v_w_up, v_ffn_conv_w, v_ffn_conv_b, v_w_down, v_ffn_post_g):
    w = dict(meta_tokens=meta_tokens, mix_pre_g=mix_pre_g, w_in=w_in, ssd_conv_w=ssd_conv_w, ssd_conv_b=ssd_conv_b, ssd_dt_bias=ssd_dt_bias, ssd_a_log=ssd_a_log, ssd_d=ssd_d, ssd_norm_g=ssd_norm_g, sb_norm_g=sb_norm_g, w_out=w_out, mix_post_g=mix_post_g, ffn_pre_g=ffn_pre_g, w_up=w_up, ffn_conv_w=ffn_conv_w, ffn_conv_b=ffn_conv_b, w_down=w_down, ffn_post_g=ffn_post_g)
    m = dict(meta_tokens=m_meta_tokens, mix_pre_g=m_mix_pre_g, w_in=m_w_in, ssd_conv_w=m_ssd_conv_w, ssd_conv_b=m_ssd_conv_b, ssd_dt_bias=m_ssd_dt_bias, ssd_a_log=m_ssd_a_log, ssd_d=m_ssd_d, ssd_norm_g=m_ssd_norm_g, sb_norm_g=m_sb_norm_g, w_out=m_w_out, mix_post_g=m_mix_post_g, ffn_pre_g=m_ffn_pre_g, w_up=m_w_up, ffn_conv_w=m_ffn_conv_w, ffn_conv_b=m_ffn_conv_b, w_down=m_w_down, ffn_post_g=m_ffn_post_g)
    v = dict(meta_tokens=v_meta_tokens, mix_pre_g=v_mix_pre_g, w_in=v_w_in, ssd_conv_w=v_ssd_conv_w, ssd_conv_b=v_ssd_conv_b, ssd_dt_bias=v_ssd_dt_bias, ssd_a_log=v_ssd_a_log, ssd_d=v_ssd_d, ssd_norm_g=v_ssd_norm_g, sb_norm_g=v_sb_norm_g, w_out=v_w_out, mix_post_g=v_mix_post_g, ffn_pre_g=v_ffn_pre_g, w_up=v_w_up, ffn_conv_w=v_ffn_conv_w, ffn_conv_b=v_ffn_conv_b, w_down=v_w_down, ffn_post_g=v_ffn_post_g)
    chip = 2 * lax.axis_index("x") + lax.axis_index("y")
    c_idx = lax.axis_index("c").astype(jnp.int32).reshape(1)

    shard_small = [w["meta_tokens"], w["ssd_conv_w"][0], w["ffn_conv_w"][0]]
    shards = [
        jnp.pad(w["w_in"][0], ((0, 0), (0, W_IN_PAD - W_IN_SHARD))).astype(BF16),
        w["w_out"][0].astype(BF16), w["w_up"][0].astype(BF16), w["w_down"][0].astype(BF16),
        _pack(shard_small, 16),
    ]
    g_in, g_out, g_up, g_down, g_small = _allgather_call(shards)
    w_in_ref = jnp.concatenate([g_in[i, :, :W_IN_SHARD] for i in range(N_CHIPS)], axis=1)
    w_in_c = jnp.concatenate([w_in_ref[:, :DT_REAL_OFF + N_HEADS], jnp.zeros((D_MODEL, BLK - N_HEADS), BF16),
                              w_in_ref[:, DT_REAL_OFF + N_HEADS:]], axis=1)
    parts = [_unpack(g_small[i], [s.shape for s in shard_small]) for i in range(N_CHIPS)]
    wt = {k: w[k][0][None] if w[k].ndim == 3 else w[k] for k in
          ["mix_pre_g", "ssd_conv_b", "ssd_dt_bias", "ssd_a_log", "ssd_d", "ssd_norm_g", "sb_norm_g", "mix_post_g",
           "ffn_pre_g", "ffn_conv_b", "ffn_post_g"]}
    wt.update(
        meta=jnp.concatenate([p[0] for p in parts], axis=1),
        ssd_conv_w=jnp.concatenate([p[1] for p in parts], axis=1),
        ffn_conv_w=jnp.concatenate([p[2] for p in parts], axis=1),
        w_in=w_in_c, w_out=g_out.reshape(2 * D_MODEL, D_MODEL),
        w_up=jnp.concatenate([g_up[i] for i in range(N_CHIPS)], axis=1),
        w_down=g_down.reshape(D_FF, D_MODEL))

    loss_row, dx, small, big = _local_step(x[0], loss_target[0], wt)

    dw_in = big["w_in"]
    dw_in_ref = jnp.concatenate([dw_in[:, :DT_REAL_OFF + N_HEADS], dw_in[:, C_Q:]], axis=1)
    dw_in_s = jnp.stack([jnp.pad(dw_in_ref[:, i * W_IN_SHARD:(i + 1) * W_IN_SHARD],
                                 ((0, 0), (0, W_IN_PAD - W_IN_SHARD))) for i in range(N_CHIPS)])
    own = [dw_in_s, big["w_out"][0].reshape(N_CHIPS, -1, D_MODEL), big["w_up"][0],
           big["w_down"][0].reshape(N_CHIPS, -1, D_MODEL)]
    own_b = [dw_in_s.astype(BF16), big["w_out"][1].reshape(N_CHIPS, -1, D_MODEL), big["w_up"][1],
             big["w_down"][1].reshape(N_CHIPS, -1, D_MODEL)]
    got = _pair_exchange_call(own_b)
    pairs = [_pair_sum_call(own[a], got[a], c_idx, "pair_sum_" + BIG[a]) for a in range(4)]
    by_chip = _chip_exchange_call(pairs)
    halves = [_chip_sum_call(by_chip[a], "chip_sum_" + BIG[a]) for a in range(4)]
    full = _half_exchange_call(halves)
    grads = {"w_in": full[0][:, :W_IN_SHARD], "w_out": full[1], "w_up": full[2], "w_down": full[3]}

    small_list = [small[k] for k in SMALL] + [jnp.sum(loss_row).reshape(1, 1)]
    red = _allreduce_small_call(_pack(small_list, 8))
    red_list = _unpack(red, [a.shape for a in small_list])
    loss = red_list[-1].reshape(())
    for k, g in zip(SMALL, red_list[:-1]):
        grads[k] = g
    for k in ["meta_tokens", "ssd_conv_w", "ffn_conv_w"]:
        wk = w[k].shape[-1]
        grads[k] = lax.dynamic_slice_in_dim(grads[k], chip * wk, wk, axis=1)

    delta, new_m, new_v = {}, {}, {}
    for k in BIG:
        delta[k], new_m[k], new_v[k] = _adamw_call(w[k][0], grads[k], m[k][0], v[k][0], "adamw_" + k)
    sm_shapes = [grads[k].shape for k in SMALL]
    res = _adamw_call(_pack([w[k] for k in SMALL]), _pack([grads[k] for k in SMALL]),
                      _pack([m[k] for k in SMALL]), _pack([v[k] for k in SMALL]), "adamw_small")
    for out, buf in zip((delta, new_m, new_v), res):
        for k, a in zip(SMALL, _unpack(buf, sm_shapes)):
            out[k] = a

    def shaped(d, k):
        return d[k].reshape(w[k].shape)

    return (loss, dx[None], *[shaped(grads, k) for k in WEIGHTS], *[shaped(delta, k) for k in WEIGHTS],
            *[shaped(new_m, k) for k in WEIGHTS], *[shaped(new_v, k) for k in WEIGHTS])
```

```python
import functools
import math

import jax
import jax.numpy as jnp
from jax import lax
from jax.experimental import pallas as pl
from jax.experimental.pallas import tpu as pltpu

F32 = jnp.float32
BF16 = jnp.bfloat16

D_MODEL = 1024
N_META = 16
BLK = 128
PAD = BLK - N_META
HEAD_DIM = 64
N_HEADS = 16
SSD_GROUPS = 2
SSD_STATE = 128
SSD_INNER = 1024
XBC = SSD_INNER + 2 * SSD_GROUPS * SSD_STATE
D_FF = 2816
EPS = 1e-6
IN_COLS = 5648
C_Z, C_XBC, C_DT, C_Q, C_K, C_V, C_END = 0, 1024, 2560, 2688, 3712, 4736, 5760
DT_REAL_OFF = 2560
N_CHIPS = 4
ADAM_LR, ADAM_B1, ADAM_B2, ADAM_EPS, ADAM_WD, ADAM_STEP = 0.001, 0.9, 0.999, 1e-08, 0.01, 10
VMEM_LIMIT = 56 * 1024 * 1024
MESH = pl.DeviceIdType.MESH


def _cparams(sem=None, **kw):
    if sem is not None:
        kw["dimension_semantics"] = sem
    return pltpu.CompilerParams(vmem_limit_bytes=VMEM_LIMIT, **kw)


def _pick(n, cands):
    for c in cands:
        if n % c == 0:
            return c
    raise ValueError((n, cands))


def _iota(shape, dim):
    return lax.broadcasted_iota(jnp.int32, shape, dim)


def _sigmoid(x):
    return 1.0 / (1.0 + jnp.exp(-x))


def _split3(v):
    h1 = v.astype(BF16)
    r1 = v - h1.astype(F32)
    h2 = r1.astype(BF16)
    h3 = (r1 - h2.astype(F32)).astype(BF16)
    return h1, h2, h3


def _dot(a, b, ca=1, cb=0):
    return lax.dot_general(a, b, (((ca,), (cb,)), ((), ())), preferred_element_type=F32)


def _dot_sel_r(v, sel, cb=0):
    h1, h2, h3 = _split3(v)
    return _dot(h1, sel, 1, cb) + _dot(h2, sel, 1, cb) + _dot(h3, sel, 1, cb)


def _dot_sel_l(sel, v, ca=1):
    h1, h2, h3 = _split3(v)
    return _dot(sel, h1, ca, 0) + _dot(sel, h2, ca, 0) + _dot(sel, h3, ca, 0)


def _mm(a, b, *, ta=False, tb=False, tm, tn, tk, out_dtype=F32, nsplit=1, extra_bf16=False, name):
    K, M = (a.shape if ta else a.shape[::-1])
    N = b.shape[0] if tb else b.shape[1]
    assert M % tm == 0 and N % tn == 0 and K % tk == 0, (name, M, N, K, tm, tn, tk)
    nm, nn, nk = M // tm, N // tn, K // tk
    assert nn % nsplit == 0
    per = nn // nsplit
    a_spec = (pl.BlockSpec((tk, tm), lambda i, j, k: (k, i)) if ta
              else pl.BlockSpec((tm, tk), lambda i, j, k: (i, k)))
    b_spec = (pl.BlockSpec((tn, tk), lambda i, j, k: (j, k)) if tb
              else pl.BlockSpec((tk, tn), lambda i, j, k: (k, j)))
    o_spec = pl.BlockSpec((None, tm, tn), lambda i, j, k: (j // per, i, j % per))
    n_out = 2 if extra_bf16 else 1
    ca, cb = (0 if ta else 1), (1 if tb else 0)

    def body(a_ref, b_ref, *rest):
        outs = rest[:n_out]
        p = _dot(a_ref[...].astype(BF16), b_ref[...].astype(BF16), ca, cb)

        def emit(val):
            outs[0][...] = val.astype(out_dtype)
            if extra_bf16:
                outs[1][...] = val.astype(BF16)

        if nk == 1:
            emit(p)
        else:
            acc = rest[n_out]
            k = pl.program_id(2)

            @pl.when(k == 0)
            def _():
                acc[...] = p

            @pl.when(k > 0)
            def _():
                acc[...] += p

            @pl.when(k == nk - 1)
            def _():
                emit(acc[...])

    shp = (nsplit, M, N // nsplit)
    out_shape = [jax.ShapeDtypeStruct(shp, out_dtype)]
    out_specs = [o_spec]
    if extra_bf16:
        out_shape.append(jax.ShapeDtypeStruct(shp, BF16))
        out_specs.append(o_spec)
    res = pl.pallas_call(
        body, name=name, grid=(nm, nn, nk), in_specs=[a_spec, b_spec], out_specs=out_specs,
        out_shape=out_shape, scratch_shapes=([pltpu.VMEM((tm, tn), F32)] if nk > 1 else []),
        compiler_params=_cparams(("parallel", "parallel", "arbitrary")),
    )(a, b)
    return res if extra_bf16 else res[0]


def _rms_stats(x):
    r = lax.rsqrt(jnp.mean(x * x, axis=-1, keepdims=True) + EPS)
    return r, x * r


def _rms_bwd(x, g, dy):
    r, xh = _rms_stats(x)
    dxh = dy * g
    dx = r * (dxh - xh * jnp.mean(dxh * xh, axis=-1, keepdims=True))
    return dx, jnp.sum(dy * xh, axis=0, keepdims=True)


def _row_spec(tr, w, col=0):
    return pl.BlockSpec((tr, w), lambda i: (i, col))


def _vec_spec(w):
    return pl.BlockSpec((1, w), lambda i: (0, 0))


def _acc_rows(ref, val, i):
    @pl.when(i == 0)
    def _():
        ref[...] = val

    @pl.when(i > 0)
    def _():
        ref[...] += val


def _rms_fwd_call(x, g, name):
    lp, w = x.shape
    tr = _pick(lp, [384, 128])

    def body(x_ref, g_ref, o_ref):
        _, xh = _rms_stats(x_ref[...])
        o_ref[...] = (xh * g_ref[...]).astype(BF16)

    return pl.pallas_call(
        body, name=name, grid=(lp // tr,), in_specs=[_row_spec(tr, w), _vec_spec(w)],
        out_specs=_row_spec(tr, w), out_shape=jax.ShapeDtypeStruct((lp, w), BF16),
        compiler_params=_cparams(("parallel",)))(x, g)


def _mid_fwd_call(h0, mix, g_post, g_pre2):
    lp, w = h0.shape
    tr = _pick(lp, [384, 128])

    def body(h0_ref, mix_ref, gp_ref, g2_ref, h1_ref, xn_ref):
        _, mh = _rms_stats(mix_ref[...])
        h1 = h0_ref[...] + mh * gp_ref[...]
        h1_ref[...] = h1
        _, hh = _rms_stats(h1)
        xn_ref[...] = (hh * g2_ref[...]).astype(BF16)

    return pl.pallas_call(
        body, name="mid_fwd", grid=(lp // tr,),
        in_specs=[_row_spec(tr, w), _row_spec(tr, w), _vec_spec(w), _vec_spec(w)],
        out_specs=[_row_spec(tr, w), _row_spec(tr, w)],
        out_shape=[jax.ShapeDtypeStruct((lp, w), F32), jax.ShapeDtypeStruct((lp, w), BF16)],
        compiler_params=_cparams(("parallel",)))(h0, mix, g_post, g_pre2)


def _final_call(h1, f, g_post, target):
    lp, w = h1.shape
    tr = BLK
    nb = lp // tr

    def body(h1_ref, f_ref, g_ref, t_ref, loss_ref, df_ref, dh_ref, dg_ref):
        i = pl.program_id(0)
        fv = f_ref[...]
        g = g_ref[...]
        _, fh = _rms_stats(fv)
        h2 = h1_ref[...] + fh * g
        diff = jnp.where(i > 0, h2 - t_ref[...], 0.0)
        part = 0.5 * jnp.sum(diff * diff, axis=0, keepdims=True) * (1.0 / w)
        _acc_rows(loss_ref, part, i)
        dh = diff * (1.0 / w)
        dh_ref[...] = dh
        df, dg = _rms_bwd(fv, g, dh)
        df_ref[...] = df.astype(BF16)
        _acc_rows(dg_ref, dg, i)

    t_spec = pl.BlockSpec((tr, w), lambda i: (jnp.maximum(i - 1, 0), 0))
    return pl.pallas_call(
        body, name="final_fwd_bwd", grid=(nb,),
        in_specs=[_row_spec(tr, w), _row_spec(tr, w), _vec_spec(w), t_spec],
        out_specs=[_vec_spec(w), _row_spec(tr, w), _row_spec(tr, w), _vec_spec(w)],
        out_shape=[jax.ShapeDtypeStruct((1, w), F32), jax.ShapeDtypeStruct((lp, w), BF16),
                   jax.ShapeDtypeStruct((lp, w), F32), jax.ShapeDtypeStruct((1, w), F32)],
        compiler_params=_cparams(("arbitrary",)))(h1, f, g_post, target)


def _mid_bwd_call(dh2, h1, dxn2, mix, g_pre2, g_post):
    lp, w = h1.shape
    tr = _pick(lp, [384, 128])

    def body(dh2_ref, h1_ref, dxn_ref, mix_ref, g2_ref, gp_ref, dh1_ref, dmix_ref, dg2_ref, dgp_ref):
        i = pl.program_id(0)
        live = (i * tr + _iota((tr, 1), 0)) >= PAD
        dx, dg2 = _rms_bwd(h1_ref[...], g2_ref[...], dxn_ref[...])
        dh1 = jnp.where(live, dh2_ref[...] + dx, 0.0)
        dh1_ref[...] = dh1
        dmix, dgp = _rms_bwd(mix_ref[...], gp_ref[...], dh1)
        dmix_ref[...] = jnp.where(live, dmix, 0.0).astype(BF16)
        _acc_rows(dg2_ref, dg2, i)
        _acc_rows(dgp_ref, dgp, i)

    rs = _row_spec(tr, w)
    return pl.pallas_call(
        body, name="mid_bwd", grid=(lp // tr,),
        in_specs=[rs, rs, rs, rs, _vec_spec(w), _vec_spec(w)],
        out_specs=[rs, rs, _vec_spec(w), _vec_spec(w)],
        out_shape=[jax.ShapeDtypeStruct((lp, w), F32), jax.ShapeDtypeStruct((lp, w), BF16),
                   jax.ShapeDtypeStruct((1, w), F32), jax.ShapeDtypeStruct((1, w), F32)],
        compiler_params=_cparams(("arbitrary",)))(dh2, h1, dxn2, mix, g_pre2, g_post)


def _norm_bwd_call(x, g, dy_arr, dy_col, name, res=None):
    lp, w = x.shape
    tr = _pick(lp, [384, 128])
    has_res = res is not None

    def body(x_ref, g_ref, dy_ref, *rest):
        i = pl.program_id(0)
        live = (i * tr + _iota((tr, 1), 0)) >= PAD
        dx, dg = _rms_bwd(x_ref[...], g_ref[...], dy_ref[...])
        if has_res:
            dx = dx + rest[0][...]
        out_ref, dg_ref = rest[-2], rest[-1]
        out_ref[...] = jnp.where(live, dx, 0.0)
        _acc_rows(dg_ref, dg, i)

    rs = _row_spec(tr, w)
    ins = [rs, _vec_spec(w), _row_spec(tr, w, dy_col)] + ([rs] if has_res else [])
    args = [x, g, dy_arr] + ([res] if has_res else [])
    return pl.pallas_call(
        body, name=name, grid=(lp // tr,), in_specs=ins, out_specs=[rs, _vec_spec(w)],
        out_shape=[jax.ShapeDtypeStruct((lp, w), F32), jax.ShapeDtypeStruct((1, w), F32)],
        compiler_params=_cparams(("arbitrary",)))(*args)


def _shift_down(cur, prev_tail, s, rows):
    if s == 0:
        return cur
    prev = jnp.tile(prev_tail, (BLK // 8, 1))
    return jnp.where(rows >= s, pltpu.roll(cur, s, 0), pltpu.roll(prev, s, 0))


def _shift_up(cur, next_head, s, rows):
    if s == 0:
        return cur
    nxt = jnp.tile(next_head, (BLK // 8, 1))
    return jnp.where(rows < BLK - s, pltpu.roll(cur, BLK - s, 0), pltpu.roll(nxt, BLK - s, 0))


def _gelu_tanh(x):
    c = math.sqrt(2.0 / math.pi)
    t = jnp.tanh(c * (x + 0.044715 * x * x * x))
    return 0.5 * x * (1.0 + t), t


def _conv_fwd_call(src, col0, width, cw, w8, b, taps, *, gate_src=None, gate_col0=0, name):
    lp = src.shape[0]
    nb, nc = lp // BLK, width // cw
    cb0 = col0 // cw
    ffn = gate_src is not None

    def body(x_ref, w_ref, b_ref, *rest):
        if ffn:
            u_ref, y_ref, a_ref, tail = rest
        else:
            y_ref, a_ref, tail = rest
        i = pl.program_id(1)

        @pl.when(i == 0)
        def _():
            tail[...] = jnp.zeros_like(tail)

        cur = x_ref[...]
        rows = _iota((BLK, cw), 0)
        y = b_ref[...] + w_ref[taps - 1:taps, :] * cur
        pt = tail[...]
        for s in range(1, taps):
            y = y + w_ref[taps - 1 - s:taps - s, :] * _shift_down(cur, pt, s, rows)
        tail[...] = cur[BLK - 8:, :]
        y_ref[...] = y
        if ffn:
            ge, _ = _gelu_tanh(y)
            a_ref[...] = (ge * u_ref[...]).astype(BF16)
        else:
            live = (i * BLK + rows) >= PAD
            a_ref[...] = jnp.where(live, y * _sigmoid(y), 0.0)

    blk = lambda c0: pl.BlockSpec((BLK, cw), lambda j, i: (i, c0 + j))
    ins = [blk(cb0), pl.BlockSpec((8, cw), lambda j, i: (0, j)), pl.BlockSpec((1, cw), lambda j, i: (0, j))]
    args = [src, w8, b]
    if ffn:
        ins.append(blk(gate_col0 // cw))
        args.append(gate_src)
    return pl.pallas_call(
        body, name=name, grid=(nc, nb), in_specs=ins, out_specs=[blk(0), blk(0)],
        out_shape=[jax.ShapeDtypeStruct((lp, width), F32),
                   jax.ShapeDtypeStruct((lp, width), BF16 if ffn else F32)],
        scratch_shapes=[pltpu.VMEM((8, cw), F32)],
        compiler_params=_cparams(("parallel", "arbitrary")))(*args)


def _conv_bwd_call(src, col0, width, cw, w8, taps, ypre, dact, *, gate_src=None, gate_col0=0, name):
    lp = src.shape[0]
    nb, nc = lp // BLK, width // cw
    cb0 = col0 // cw
    ffn = gate_src is not None

    def body(x_ref, w_ref, y_ref, d_ref, *rest):
        if ffn:
            u_ref, dx_ref, du_ref, dw_ref, db_ref, head = rest
        else:
            dx_ref, dw_ref, db_ref, head = rest
        step = pl.program_id(1)
        i = nb - 1 - step

        @pl.when(step == 0)
        def _():
            head[...] = jnp.zeros_like(head)

        rows = _iota((BLK, cw), 0)
        live = (i * BLK + rows) >= PAD
        y = y_ref[...]
        d = d_ref[...]
        if ffn:
            ge, t = _gelu_tanh(y)
            c = math.sqrt(2.0 / math.pi)
            dge = 0.5 * (1.0 + t) + 0.5 * y * (1.0 - t * t) * c * (1.0 + 3.0 * 0.044715 * y * y)
            u = u_ref[...]
            du_ref[...] = jnp.where(live, d * ge, 0.0).astype(BF16)
            dy = jnp.where(live, d * u * dge, 0.0)
        else:
            sg = _sigmoid(y)
            dy = jnp.where(live, d * sg * (1.0 + y * (1.0 - sg)), 0.0)
        x = x_ref[...]
        nh = head[...]
        dx = jnp.zeros_like(dy)
        dws = []
        for s in range(taps):
            sh = _shift_up(dy, nh, s, rows)
            dx = dx + w_ref[taps - 1 - s:taps - s, :] * sh
            dws.append(jnp.sum(x * sh, axis=0, keepdims=True))
        head[...] = dy[:8, :]
        dx_ref[...] = jnp.where(live, dx, 0.0).astype(BF16)
        dw = jnp.concatenate([dws[taps - 1 - k] for k in range(taps)]
                             + [jnp.zeros((8 - taps, cw), F32)], axis=0)
        _acc_rows(dw_ref, dw, step)
        _acc_rows(db_ref, jnp.sum(dy, axis=0, keepdims=True), step)

    blk = lambda c0: pl.BlockSpec((BLK, cw), lambda j, s: (nb - 1 - s, c0 + j))
    ins = [blk(cb0), pl.BlockSpec((8, cw), lambda j, s: (0, j)), blk(0), blk(0)]
    args = [src, w8, ypre, dact]
    outs = [blk(0)]
    oshape = [jax.ShapeDtypeStruct((lp, width), BF16)]
    if ffn:
        ins.append(blk(gate_col0 // cw))
        args.append(gate_src)
        outs.append(blk(0))
        oshape.append(jax.ShapeDtypeStruct((lp, width), BF16))
    outs += [pl.BlockSpec((8, cw), lambda j, s: (0, j)), pl.BlockSpec((1, cw), lambda j, s: (0, j))]
    oshape += [jax.ShapeDtypeStruct((8, width), F32), jax.ShapeDtypeStruct((1, width), F32)]
    return pl.pallas_call(
        body, name=name, grid=(nc, nb), in_specs=ins, out_specs=outs, out_shape=oshape,
        scratch_shapes=[pltpu.VMEM((8, cw), F32)],
        compiler_params=_cparams(("parallel", "arbitrary")))(*args)


SB_GROUP = 4


def _sb_scores(qm_h, kb):
    z = _dot(qm_h, kb, 1, 1)
    sp = jnp.maximum(z, 0.0) + jnp.log(1.0 + jnp.exp(-jnp.abs(z)))
    return z - sp, -sp


def _sb_valid(i, off, width):
    kpos = off + _iota((BLK, width), 1)
    qpos = i * BLK + _iota((BLK, width), 0)
    return (kpos < qpos) & (kpos >= PAD)


def _sb_groups(i):
    edge = i // SB_GROUP
    return edge, pl.multiple_of(edge * (SB_GROUP * BLK), BLK)


def _tri2(cond):
    t = jnp.concatenate([cond.astype(BF16), jnp.ones((BLK, BLK), BF16)], axis=1)
    return jnp.concatenate([t, t], axis=0)


def _dot_tri(v, tri2):
    hi = v.astype(BF16)
    lo = (v - hi.astype(F32)).astype(BF16)
    r = _dot(jnp.concatenate([hi, lo], axis=1), tri2)
    return r[:, :BLK], r[:, BLK:]


def _sb_fwd_call(proj):
    lp = proj.shape[0]
    nb = lp // BLK
    assert (nb - 1) % SB_GROUP == 0
    scale = 1.0 / math.sqrt(HEAD_DIM)

    def body(q_ref, k_ref, v_ref, o_ref, tl_ref):
        i = pl.program_id(1)
        lane = _iota((BLK, BLK), 1)
        row = _iota((BLK, BLK), 0)
        low = lane < HEAD_DIM
        q = q_ref[...] * scale
        qm = (jnp.where(low, q, 0.0).astype(BF16), jnp.where(low, 0.0, q).astype(BF16))
        tri = _tri2(row > lane)

        def chunk(off, nsub, last_valid, carry):
            width = nsub * BLK
            sls = [slice(b * BLK, (b + 1) * BLK) for b in range(nsub)]
            kb = k_ref[pl.ds(off, width), :].astype(BF16)
            vb = v_ref[pl.ds(off, width), :].astype(BF16)
            sc = [_sb_scores(qm[h], kb) for h in range(2)]
            lbs = [s[0] for s in sc]
            lks = [[s[1][:, sl] for sl in sls] for s in sc]
            first_valid = (off + lane) >= PAD
            for h in range(2):
                lks[h][0] = jnp.where(first_valid, lks[h][0], 0.0)
                if last_valid is not None:
                    lks[h][-1] = jnp.where(last_valid, lks[h][-1], 0.0)
            afters = [[_dot_tri(lks[h][b], tri) for b in range(nsub)] for h in range(2)]
            runs, wss = [], []
            for h in range(2):
                run = carry[2 * h]
                ws = [None] * nsub
                for b in reversed(range(nsub)):
                    wb = jnp.exp(lbs[h][:, sls[b]] + afters[h][b][0] + run)
                    if b == 0:
                        wb = jnp.where(first_valid, wb, 0.0)
                    if last_valid is not None and b == nsub - 1:
                        wb = jnp.where(last_valid, wb, 0.0)
                    ws[b] = wb.astype(BF16)
                    run = run + afters[h][b][1]
                runs.append(run)
                wss.append(ws[0] if nsub == 1 else jnp.concatenate(ws, axis=1))
            return (runs[0], carry[1] + _dot(wss[0], vb), runs[1], carry[3] + _dot(wss[1], vb))

        edge, edge_off = _sb_groups(i)
        diag = lane < row
        zc = jnp.zeros((BLK, BLK), F32)
        za = jnp.zeros((BLK, BLK), F32)
        upto = [functools.partial(chunk, edge_off, r, diag) for r in range(1, SB_GROUP + 1)]
        carry = lax.switch(i - edge * SB_GROUP, upto, (zc, za, zc, za))

        def interior(t, carry):
            off = pl.multiple_of((edge - 2 - 2 * t) * (SB_GROUP * BLK), BLK)
            return chunk(off, 2 * SB_GROUP, None, carry)

        carry = lax.fori_loop(0, edge // 2, interior, carry)
        r0, o0, r1, o1 = lax.cond(edge % 2 == 1, lambda cr: chunk(0, SB_GROUP, None, cr), lambda cr: cr, carry)
        o_ref[...] = jnp.where(low, o0, o1)
        tl_ref[...] = jnp.where(low, r0, r1)

    qc, kc, vc = C_Q // BLK, C_K // BLK, C_V // BLK
    blk = pl.BlockSpec((BLK, BLK), lambda p, i: (i, p))
    return pl.pallas_call(
        body, name="sb_fwd", grid=(N_HEADS // 2, nb),
        in_specs=[pl.BlockSpec((BLK, BLK), lambda p, i: (i, qc + p)),
                  pl.BlockSpec((lp, BLK), lambda p, i: (0, kc + p)),
                  pl.BlockSpec((lp, BLK), lambda p, i: (0, vc + p))],
        out_specs=[blk, blk],
        out_shape=[jax.ShapeDtypeStruct((lp, N_HEADS * HEAD_DIM), F32)] * 2,
        compiler_params=_cparams(("parallel", "arbitrary")))(proj, proj, proj)


def _sb_bwd_call(proj, tl, do):
    lp = proj.shape[0]
    nb = lp // BLK
    assert (nb - 1) % SB_GROUP == 0
    scale = 1.0 / math.sqrt(HEAD_DIM)

    def body(q_ref, k_ref, v_ref, tl_ref, do_ref, dq_ref, dk_ref, dv_ref, dk_acc, dv_acc):
        i = pl.program_id(1)

        @pl.when(i == 0)
        def _():
            dk_acc[...] = jnp.zeros_like(dk_acc)
            dv_acc[...] = jnp.zeros_like(dv_acc)

        lane = _iota((BLK, BLK), 1)
        row = _iota((BLK, BLK), 0)
        low = lane < HEAD_DIM
        q = q_ref[...] * scale
        dov = do_ref[...]
        qm = (jnp.where(low, q, 0.0).astype(BF16), jnp.where(low, 0.0, q).astype(BF16))
        dom = (jnp.where(low, dov, 0.0).astype(BF16), jnp.where(low, 0.0, dov).astype(BF16))
        tlv = tl_ref[...]
        tot = (jnp.broadcast_to(tlv[:, 0:1], (BLK, BLK)),
               jnp.broadcast_to(tlv[:, HEAD_DIM:HEAD_DIM + 1], (BLK, BLK)))
        tri_in = _tri2(row <= lane)
        tri_ex = _tri2(row < lane)

        def chunk(off, nsub, last_valid, carry):
            width = nsub * BLK
            sls = [slice(b * BLK, (b + 1) * BLK) for b in range(nsub)]
            cat = lambda parts: parts[0] if nsub == 1 else jnp.concatenate(parts, axis=1)
            mask_last = lambda b: last_valid is not None and b == nsub - 1
            kb = k_ref[pl.ds(off, width), :].astype(BF16)
            vb = v_ref[pl.ds(off, width), :].astype(BF16)
            sc = [_sb_scores(qm[h], kb) for h in range(2)]
            dws = [_dot(dom[h], vb, 1, 1) for h in range(2)]
            lbs = [s[0] for s in sc]
            lks = [[s[1][:, sl] for sl in sls] for s in sc]
            first_valid = (off + lane) >= PAD
            for h in range(2):
                lks[h][0] = jnp.where(first_valid, lks[h][0], 0.0)
                if last_valid is not None:
                    lks[h][-1] = jnp.where(last_valid, lks[h][-1], 0.0)
            pins = [[_dot_tri(lks[h][b], tri_in) for b in range(nsub)] for h in range(2)]
            runs, wss, gss = [], [], []
            for h in range(2):
                run = carry[3 * h]
                ws, gs = [], []
                for b in range(nsub):
                    wb = jnp.exp(lbs[h][:, sls[b]] + (tot[h] - run - pins[h][b][0]))
                    if b == 0:
                        wb = jnp.where(first_valid, wb, 0.0)
                    if mask_last(b):
                        wb = jnp.where(last_valid, wb, 0.0)
                    ws.append(wb.astype(BF16))
                    gs.append(wb * dws[h][:, sls[b]])
                    run = run + pins[h][b][1]
                runs.append(run)
                wss.append(cat(ws))
                gss.append(gs)
            gexs = [[_dot_tri(gss[h][b], tri_ex) for b in range(nsub)] for h in range(2)]
            gsums, dzs = [], []
            for h in range(2):
                gsum = carry[3 * h + 1]
                beta = jnp.exp(lbs[h])
                parts = []
                for b in range(nsub):
                    g, bt = gss[h][b], beta[:, sls[b]]
                    dzb = g * (1.0 - bt) - (gsum + gexs[h][b][0]) * bt
                    if b == 0:
                        dzb = jnp.where(first_valid, dzb, 0.0)
                    if mask_last(b):
                        dzb = jnp.where(last_valid, dzb, 0.0)
                    parts.append(dzb.astype(BF16))
                    gsum = gsum + gexs[h][b][1]
                gsums.append(gsum)
                dzs.append(cat(parts))
            dk_acc[pl.ds(off, width), :] += _dot(dzs[0], qm[0], 0, 0) + _dot(dzs[1], qm[1], 0, 0)
            dv_acc[pl.ds(off, width), :] += _dot(wss[0], dom[0], 0, 0) + _dot(wss[1], dom[1], 0, 0)
            return (runs[0], gsums[0], carry[2] + _dot(dzs[0], kb),
                    runs[1], gsums[1], carry[5] + _dot(dzs[1], kb))

        edge, edge_off = _sb_groups(i)
        diag = lane < row
        zc = jnp.zeros((BLK, BLK), F32)
        za = jnp.zeros((BLK, BLK), F32)
        odd = edge % 2
        carry = lax.cond(odd == 1, lambda cr: chunk(0, SB_GROUP, None, cr), lambda cr: cr,
                         (zc, zc, za, zc, zc, za))

        def interior(t, carry):
            off = pl.multiple_of((odd + 2 * t) * (SB_GROUP * BLK), BLK)
            return chunk(off, 2 * SB_GROUP, None, carry)

        carry = lax.fori_loop(0, edge // 2, interior, carry)
        upto = [functools.partial(chunk, edge_off, r, diag) for r in range(1, SB_GROUP + 1)]
        res = lax.switch(i - edge * SB_GROUP, upto, carry)
        dq_ref[...] = (jnp.where(low, res[2], res[5]) * scale).astype(BF16)

        @pl.when(i == nb - 1)
        def _():
            dk_ref[...] = dk_acc[...].astype(BF16)
            dv_ref[...] = dv_acc[...].astype(BF16)

    qc, kc, vc = C_Q // BLK, C_K // BLK, C_V // BLK
    blk = pl.BlockSpec((BLK, BLK), lambda p, i: (i, p))
    full = pl.BlockSpec((lp, BLK), lambda p, i: (0, p))
    w = N_HEADS * HEAD_DIM
    return pl.pallas_call(
        body, name="sb_bwd", grid=(N_HEADS // 2, nb),
        in_specs=[pl.BlockSpec((BLK, BLK), lambda p, i: (i, qc + p)),
                  pl.BlockSpec((lp, BLK), lambda p, i: (0, kc + p)),
                  pl.BlockSpec((lp, BLK), lambda p, i: (0, vc + p)),
                  blk, blk],
        out_specs=[blk, full, full],
        out_shape=[jax.ShapeDtypeStruct((lp, w), BF16)] * 3,
        scratch_shapes=[pltpu.VMEM((lp, BLK), F32), pltpu.VMEM((lp, BLK), F32)],
        compiler_params=_cparams(("parallel", "arbitrary")))(proj, proj, proj, tl, do)


def _log1p(e):
    u = 1.0 + e
    return jnp.where(u == 1.0, e, jnp.log(u) * e / jnp.where(u == 1.0, 1.0, u - 1.0))


def _ssd_common(c, dtr, bias, alog):
    row = _iota((BLK, BLK), 0)
    lane = _iota((BLK, BLK), 1)
    live = ((c * BLK + row) >= PAD) & (lane < N_HEADS)
    pre = dtr + bias
    dt = jnp.where(live, jnp.maximum(pre, 0.0) + _log1p(jnp.exp(-jnp.abs(pre))), 0.0)
    a_neg = -jnp.exp(alog)
    a = dt * a_neg
    t_in = (lane <= row).astype(BF16)
    cs = _dot_sel_l(t_in, a)
    cs_t = cs.T
    cs_end = cs[BLK - 1:BLK, :]
    e = jnp.exp(cs)
    f = jnp.exp(cs_end - cs)
    xp = ((_iota((BLK, SSD_INNER), 1) // HEAD_DIM) == _iota((BLK, SSD_INNER), 0)).astype(BF16)
    xp_t = ((_iota((SSD_INNER, BLK), 0) // HEAD_DIM) == _iota((SSD_INNER, BLK), 1)).astype(BF16)
    decay_col = _dot_sel_l(xp_t, jnp.exp(cs_t))[:, BLK - 1:BLK]
    return dict(live=live, pre=pre, dt=dt, a_neg=a_neg, cs=cs, cs_t=cs_t, e=e, f=f, xp=xp, xp_t=xp_t,
                decay_col=decay_col, row=row, lane=lane,
                dt_x=_dot_sel_r(dt, xp), e_x=_dot_sel_r(e, xp), f_x=_dot_sel_r(f, xp))


def _ssd_ldec(q, h):
    diff = q["cs"][:, h:h + 1] - q["cs_t"][h:h + 1, :]
    causal = q["row"] >= q["lane"]
    return jnp.where(causal, jnp.exp(jnp.where(causal, diff, 0.0)), 0.0)


def _ssd_fwd_call(xbc, proj, bias, alog, d_x, norm_g):
    lp = xbc.shape[0]
    nb = lp // BLK
    gw = SSD_INNER // SSD_GROUPS
    ppg = gw // BLK

    def body(xbc_ref, dtr_ref, z_ref, bias_ref, alog_ref, dx_ref, ng_ref, yb_ref, ypre_ref, sprev_ref, s_ref):
        c = pl.program_id(0)

        @pl.when(c == 0)
        def _():
            s_ref[...] = jnp.zeros_like(s_ref)

        q = _ssd_common(c, dtr_ref[...], bias_ref[...], alog_ref[...])
        x = xbc_ref[:, 0:SSD_INNER]
        xd = x * q["dt_x"]
        low = q["lane"] < HEAD_DIM
        s_old = s_ref[...]
        sprev_ref[...] = s_old
        xdf = (xd * q["f_x"]).astype(BF16)
        for g in range(SSD_GROUPS):
            bg = xbc_ref[:, SSD_INNER + g * SSD_STATE:SSD_INNER + (g + 1) * SSD_STATE].astype(BF16)
            cg = xbc_ref[:, SSD_INNER + (SSD_GROUPS + g) * SSD_STATE:
                         SSD_INNER + (SSD_GROUPS + g + 1) * SSD_STATE].astype(BF16)
            cb = _dot(cg, bg, 1, 1)
            gs = slice(g * gw, (g + 1) * gw)
            y_off = _dot(cg, s_old[gs, :].astype(BF16), 1, 1) * q["e_x"][:, gs]
            s_ref[gs, :] = s_old[gs, :] * q["decay_col"][gs, :] + _dot(xdf[:, gs], bg, 0, 0)
            for pr in range(ppg):
                cols = slice(g * gw + pr * BLK, g * gw + (pr + 1) * BLK)
                xd_p = xd[:, cols]
                acc = y_off[:, pr * BLK:(pr + 1) * BLK]
                for hh in range(2):
                    h = (g * gw + pr * BLK) // HEAD_DIM + hh
                    m = (cb * _ssd_ldec(q, h)).astype(BF16)
                    xm = jnp.where(low, xd_p, 0.0) if hh == 0 else jnp.where(low, 0.0, xd_p)
                    acc = acc + _dot(m, xm.astype(BF16))
                ypre_ref[:, cols] = acc
        ypre = ypre_ref[...] + x * dx_ref[...]
        ypre_ref[...] = ypre
        z = z_ref[...]
        yg = ypre * (z * _sigmoid(z))
        _, yh = _rms_stats(yg)
        yb_ref[...] = (yh * ng_ref[...]).astype(BF16)

    row = lambda w, col: pl.BlockSpec((BLK, w), lambda c: (c, col))
    vec = lambda w: pl.BlockSpec((1, w), lambda c: (0, 0))
    return pl.pallas_call(
        body, name="ssd_fwd", grid=(nb,),
        in_specs=[row(XBC, 0), row(BLK, C_DT // BLK), row(SSD_INNER, 0), vec(BLK), vec(BLK),
                  vec(SSD_INNER), vec(SSD_INNER)],
        out_specs=[row(SSD_INNER, 0), row(SSD_INNER, 0),
                   pl.BlockSpec((None, SSD_INNER, SSD_STATE), lambda c: (c, 0, 0))],
        out_shape=[jax.ShapeDtypeStruct((lp, SSD_INNER), BF16), jax.ShapeDtypeStruct((lp, SSD_INNER), F32),
                   jax.ShapeDtypeStruct((nb, SSD_INNER, SSD_STATE), F32)],
        scratch_shapes=[pltpu.VMEM((SSD_INNER, SSD_STATE), F32)],
        compiler_params=_cparams(("arbitrary",)))(xbc, proj, proj, bias, alog, d_x, norm_g)


def _ssd_bwd_call(dycat, ypre, xbc, proj, sprev, bias, alog, d_x, norm_g):
    lp = xbc.shape[0]
    nb = lp // BLK
    gw = SSD_INNER // SSD_GROUPS
    ppg = gw // BLK

    def body(dy_ref, ypre_ref, xbc_ref, dtr_ref, z_ref, sp_ref, bias_ref, alog_ref, dxp_ref, ng_ref,
             dz_ref, dxbc_ref, ddt_ref, dng_ref, dd_ref, dal_ref, dbi_ref, ds_ref, dxd_ref):
        step = pl.program_id(0)
        c = nb - 1 - step

        @pl.when(step == 0)
        def _():
            ds_ref[...] = jnp.zeros_like(ds_ref)

        q = _ssd_common(c, dtr_ref[...], bias_ref[...], alog_ref[...])
        row, lane = q["row"], q["lane"]
        low = lane < HEAD_DIM
        rowlive = ((c * BLK + _iota((BLK, 1), 0)) >= PAD)
        x = xbc_ref[:, 0:SSD_INNER]
        xd = x * q["dt_x"]
        z = z_ref[...]
        sz = _sigmoid(z)
        silu = z * sz
        ypre = ypre_ref[...]
        dyg, dng = _rms_bwd(ypre * silu, ng_ref[...], dy_ref[...])
        _acc_rows(dng_ref, dng, step)
        dyp = dyg * silu
        dz_ref[...] = jnp.where(rowlive, dyg * ypre * (sz * (1.0 + z * (1.0 - sz))), 0.0).astype(BF16)
        _acc_rows(dd_ref, jnp.sum(dyp * x, axis=0, keepdims=True), step)
        dye = dyp * q["e_x"]
        xdf = xd * q["f_x"]
        s_prev = sp_ref[...]
        ds_old = ds_ref[...]
        qrow = jnp.zeros((BLK, BLK), F32)
        qcol_t = jnp.zeros((BLK, BLK), F32)
        red_e = []
        red_f = []
        for g in range(SSD_GROUPS):
            gs = slice(g * gw, (g + 1) * gw)
            bsl = slice(SSD_INNER + g * SSD_STATE, SSD_INNER + (g + 1) * SSD_STATE)
            csl = slice(SSD_INNER + (SSD_GROUPS + g) * SSD_STATE, SSD_INNER + (SSD_GROUPS + g + 1) * SSD_STATE)
            bg = xbc_ref[:, bsl].astype(BF16)
            cg = xbc_ref[:, csl].astype(BF16)
            sg = s_prev[gs, :].astype(BF16)
            dsg = ds_old[gs, :].astype(BF16)
            cb = _dot(cg, bg, 1, 1)
            bds = _dot(bg, dsg, 1, 1)
            y_off = _dot(cg, sg, 1, 1) * q["e_x"][:, gs]
            red_e.append(dyp[:, gs] * y_off)
            red_f.append(xd[:, gs] * bds * q["f_x"][:, gs])
            dc = _dot(dye[:, gs].astype(BF16), sg)
            db = _dot(xdf[:, gs].astype(BF16), dsg)
            ds_ref[gs, :] = ds_old[gs, :] * q["decay_col"][gs, :] + _dot(dye[:, gs].astype(BF16), cg, 0, 0)
            dcb = jnp.zeros((BLK, BLK), F32)
            for pr in range(ppg):
                cols = slice(g * gw + pr * BLK, g * gw + (pr + 1) * BLK)
                xd_p = xd[:, cols].astype(BF16)
                dy_p = dyp[:, cols]
                acc = q["f_x"][:, cols] * bds[:, pr * BLK:(pr + 1) * BLK]
                for hh in range(2):
                    h = (g * gw + pr * BLK) // HEAD_DIM + hh
                    ld = _ssd_ldec(q, h)
                    m = cb * ld
                    dym = (jnp.where(low, dy_p, 0.0) if hh == 0 else jnp.where(low, 0.0, dy_p)).astype(BF16)
                    dm = jnp.where(row >= lane, _dot(dym, xd_p, 1, 1), 0.0)
                    acc = acc + _dot(m.astype(BF16), dym, 0, 0)
                    dcb = dcb + dm * ld
                    qq = dm * m
                    qrow = qrow + jnp.where(lane == h, jnp.sum(qq, axis=1, keepdims=True), 0.0)
                    qcol_t = qcol_t + jnp.where(row == h, jnp.sum(qq, axis=0, keepdims=True), 0.0)
                dxd_ref[:, cols] = acc
            dcbb = dcb.astype(BF16)
            dxbc_ref[:, bsl] = jnp.where(rowlive, db + _dot(dcbb, cg, 0, 0), 0.0)
            dxbc_ref[:, csl] = jnp.where(rowlive, dc + _dot(dcbb, bg), 0.0)
        dxd = dxd_ref[...]
        dxbc_ref[:, 0:SSD_INNER] = jnp.where(rowlive, dxd * q["dt_x"] + dyp * dxp_ref[...], 0.0)
        xp_t = q["xp_t"]
        fw = _dot_sel_r(jnp.concatenate(red_f, axis=1), xp_t)
        dcs = qrow - qcol_t.T + _dot_sel_r(jnp.concatenate(red_e, axis=1), xp_t) - fw
        end_f = jnp.sum(fw, axis=0, keepdims=True)
        sds = jnp.sum(ds_old * s_prev, axis=1, keepdims=True)
        per_head = _dot_sel_l(q["xp"], jnp.broadcast_to(sds, (SSD_INNER, BLK)))
        end_e = per_head.T[0:1, :] * jnp.exp(q["cs"][BLK - 1:BLK, :])
        dcs = dcs + jnp.where(row == BLK - 1, end_f + end_e, 0.0)
        t_up = (lane >= row).astype(BF16)
        da = _dot_sel_l(t_up, dcs)
        ddt = da * q["a_neg"] + _dot_sel_r(dxd * x, xp_t)
        _acc_rows(dal_ref, jnp.sum(da * q["dt"] * q["a_neg"], axis=0, keepdims=True), step)
        ddtr = jnp.where(q["live"], ddt * _sigmoid(q["pre"]), 0.0)
        ddt_ref[...] = ddtr.astype(BF16)
        _acc_rows(dbi_ref, jnp.sum(ddtr, axis=0, keepdims=True), step)

    row_s = lambda w, col: pl.BlockSpec((BLK, w), lambda s: (nb - 1 - s, col))
    vec = lambda w: pl.BlockSpec((1, w), lambda s: (0, 0))
    return pl.pallas_call(
        body, name="ssd_bwd", grid=(nb,),
        in_specs=[row_s(SSD_INNER, 0), row_s(SSD_INNER, 0), row_s(XBC, 0), row_s(BLK, C_DT // BLK),
                  row_s(SSD_INNER, 0), pl.BlockSpec((None, SSD_INNER, SSD_STATE), lambda s: (nb - 1 - s, 0, 0)),
                  vec(BLK), vec(BLK), vec(SSD_INNER), vec(SSD_INNER)],
        out_specs=[row_s(SSD_INNER, 0), row_s(XBC, 0), row_s(BLK, 0),
                   vec(SSD_INNER), vec(SSD_INNER), vec(BLK), vec(BLK)],
        out_shape=[jax.ShapeDtypeStruct((lp, SSD_INNER), BF16), jax.ShapeDtypeStruct((lp, XBC), F32),
                   jax.ShapeDtypeStruct((lp, BLK), BF16),
                   jax.ShapeDtypeStruct((1, SSD_INNER), F32), jax.ShapeDtypeStruct((1, SSD_INNER), F32),
                   jax.ShapeDtypeStruct((1, BLK), F32), jax.ShapeDtypeStruct((1, BLK), F32)],
        scratch_shapes=[pltpu.VMEM((SSD_INNER, SSD_STATE), F32), pltpu.VMEM((BLK, SSD_INNER), F32)],
        compiler_params=_cparams(("arbitrary",)))(dycat, ypre, xbc, proj, proj, sprev, bias, alog, d_x, norm_g)


def _pad_rows8(w):
    return jnp.pad(w, ((0, 8 - w.shape[0]), (0, 0)))


def _pad_lanes(v, n=BLK):
    return jnp.pad(v, ((0, 0), (0, n - v.shape[1])))


def _local_step(x, target, wt):
    seq = x.shape[0]
    lp = seq + BLK
    tm = _pick(lp, [1408, 768, 384, 128])
    tkr = _pick(lp, [384, 128])
    h0 = jnp.concatenate([jnp.zeros((PAD, D_MODEL), F32), wt["meta"], x], axis=0)
    bias = _pad_lanes(wt["ssd_dt_bias"])
    alog = _pad_lanes(wt["ssd_a_log"])
    d_x = jnp.repeat(wt["ssd_d"], HEAD_DIM, axis=1)
    cw8 = _pad_rows8(wt["ssd_conv_w"])
    fw8 = _pad_rows8(wt["ffn_conv_w"])
    fcw = D_FF // 2

    xn1 = _rms_fwd_call(h0, wt["mix_pre_g"], "norm1")
    proj = _mm(xn1, wt["w_in"], tm=tm, tn=1152, tk=D_MODEL, name="mm_proj")[0]
    conv_pre, xbc = _conv_fwd_call(proj, C_XBC, XBC, 512, cw8, wt["ssd_conv_b"], 4, name="ssd_conv_fwd")
    y_ssd, ypre, sprev = _ssd_fwd_call(xbc, proj, bias, alog, d_x, wt["ssd_norm_g"])
    o, tl = _sb_fwd_call(proj)
    y_sb = _rms_fwd_call(o, wt["sb_norm_g"], "sb_norm")
    ycat = jnp.concatenate([y_ssd, y_sb], axis=1)
    mix = _mm(ycat, wt["w_out"], tm=tm, tn=1024, tk=2048, name="mm_mix")[0]
    h1, xn2 = _mid_fwd_call(h0, mix, wt["mix_post_g"], wt["ffn_pre_g"])
    gu = _mm(xn2, wt["w_up"], tm=tm, tn=1408, tk=D_MODEL, name="mm_up")[0]
    gpre, act = _conv_fwd_call(gu, 0, D_FF, fcw, fw8, wt["ffn_conv_b"], 3, gate_src=gu, gate_col0=D_FF,
                               name="ffn_conv_fwd")
    f = _mm(act, wt["w_down"], tm=tm, tn=1024, tk=1408, name="mm_down")[0]
    loss_row, df, dh2, dg_ffn_post = _final_call(h1, f, wt["ffn_post_g"], target)

    dact = _mm(df, wt["w_down"], tb=True, tm=tm, tn=1408, tk=D_MODEL, name="mm_dact")[0]
    dw_down, dw_down_b = _mm(act, df, ta=True, tm=1408, tn=1024, tk=tkr, extra_bf16=True, name="mm_dw_down")
    dgate, dup, dfcw, dfcb = _conv_bwd_call(gu, 0, D_FF, fcw, fw8, 3, gpre, dact, gate_src=gu, gate_col0=D_FF,
                                            name="ffn_conv_bwd")
    dgu = jnp.concatenate([dgate, dup], axis=1)
    dxn2 = _mm(dgu, wt["w_up"], tb=True, tm=tm, tn=1024, tk=1408, name="mm_dxn2")[0]
    dw_up, dw_up_b = _mm(xn2, dgu, ta=True, tm=1024, tn=1408, tk=tkr, nsplit=N_CHIPS, extra_bf16=True,
                         name="mm_dw_up")
    dh1, dmix, dg_ffn_pre, dg_mix_post = _mid_bwd_call(dh2, h1, dxn2, mix, wt["ffn_pre_g"], wt["mix_post_g"])
    dycat = _mm(dmix, wt["w_out"], tb=True, tm=tm, tn=1024, tk=D_MODEL, name="mm_dycat")[0]
    dw_out, dw_out_b = _mm(ycat, dmix, ta=True, tm=1024, tn=1024, tk=tkr, extra_bf16=True, name="mm_dw_out")
    do, dg_sb = _norm_bwd_call(o, wt["sb_norm_g"], dycat, 1, "sb_norm_bwd")
    dq, dk, dv = _sb_bwd_call(proj, tl, do)
    dz, dxbc_act, ddt, dg_ssd, dd_x, dalog, dbias = _ssd_bwd_call(
        dycat, ypre, xbc, proj, sprev, bias, alog, d_x, wt["ssd_norm_g"])
    dxbc, dcw, dcb = _conv_bwd_call(proj, C_XBC, XBC, 512, cw8, 4, conv_pre, dxbc_act, name="ssd_conv_bwd")
    dproj = jnp.concatenate([dz, dxbc, ddt, dq, dk, dv], axis=1)
    dxn1 = _mm(dproj, wt["w_in"], tb=True, tm=tm, tn=1024, tk=1152, name="mm_dxn1")[0]
    dw_in = _mm(xn1, dproj, ta=True, tm=1024, tn=1152, tk=tkr, name="mm_dw_in")[0]
    dh0, dg_pre = _norm_bwd_call(h0, wt["mix_pre_g"], dxn1, 0, "norm1_bwd", res=dh1)

    small = {
        "meta_tokens": dh0[PAD:BLK], "mix_pre_g": dg_pre, "ssd_conv_w": dcw[:4], "ssd_conv_b": dcb,
        "ssd_dt_bias": dbias[:, :N_HEADS], "ssd_a_log": dalog[:, :N_HEADS],
        "ssd_d": jnp.sum(dd_x.reshape(N_HEADS, HEAD_DIM), axis=1)[None],
        "ssd_norm_g": dg_ssd, "sb_norm_g": dg_sb, "mix_post_g": dg_mix_post, "ffn_pre_g": dg_ffn_pre,
        "ffn_conv_w": dfcw[:3], "ffn_conv_b": dfcb, "ffn_post_g": dg_ffn_post,
    }
    big = {"w_in": dw_in, "w_out": (dw_out, dw_out_b), "w_up": (dw_up, dw_up_b), "w_down": (dw_down, dw_down_b)}
    return loss_row, dh0[BLK:], small, big


def _adamw_call(w, g, m, v, name):
    rows, cols = w.shape
    tr = 256 if rows % 256 == 0 else (352 if rows % 352 == 0 else rows)
    c1 = 1.0 - ADAM_B1 ** ADAM_STEP
    c2 = 1.0 - ADAM_B2 ** ADAM_STEP

    def body(w_ref, g_ref, m_ref, v_ref, d_ref, mo_ref, vo_ref):
        gv = g_ref[...]
        m2 = ADAM_B1 * m_ref[...] + (1.0 - ADAM_B1) * gv
        v2 = ADAM_B2 * v_ref[...] + (1.0 - ADAM_B2) * (gv * gv)
        d_ref[...] = -ADAM_LR * ((m2 / c1) / (jnp.sqrt(v2 / c2) + ADAM_EPS) + ADAM_WD * w_ref[...])
        mo_ref[...] = m2
        vo_ref[...] = v2

    spec = pl.BlockSpec((tr, cols), lambda i: (i, 0))
    return pl.pallas_call(
        body, name=name, grid=(rows // tr,), in_specs=[spec] * 4, out_specs=[spec] * 3,
        out_shape=[jax.ShapeDtypeStruct((rows, cols), F32)] * 3,
        compiler_params=_cparams(("parallel",)))(w, g, m, v)


ANY = pl.BlockSpec(memory_space=pl.ANY)


def _place():
    x, y, c = lax.axis_index("x"), lax.axis_index("y"), lax.axis_index("c")
    chips = [(1 - x, y), (x, 1 - y), (1 - x, 1 - y)]
    return x, y, c, chips


def _half(c, h):
    return pl.ds(pl.multiple_of(c * h, 8), h)


def _allgather_call(shards):
    n = len(shards)

    def body(*refs):
        ins, outs = refs[:n], refs[n:2 * n]
        send_i, recv_i, send_d, recv_d = refs[2 * n:]
        x, y, c, chips = _place()
        me = 2 * x + y
        sends = []
        for a in range(n):
            h = shards[a].shape[0] // 2
            for j, chip in enumerate(chips):
                cp = pltpu.make_async_remote_copy(
                    src_ref=ins[a].at[_half(c, h)], dst_ref=outs[a].at[me, _half(c, h)],
                    send_sem=send_i.at[3 * a + j], recv_sem=recv_i.at[3 * a + j],
                    device_id=(*chip, c), device_id_type=MESH)
                cp.start()
                sends.append(cp)
        for a in range(n):
            h = shards[a].shape[0] // 2
            for j, chip in enumerate(chips):
                src = 2 * chip[0] + chip[1]
                landed = outs[a].at[src, _half(c, h)]
                pltpu.make_async_remote_copy(
                    src_ref=landed, dst_ref=landed, send_sem=send_i.at[3 * a + j], recv_sem=recv_i.at[3 * a + j],
                    device_id=(*chip, c), device_id_type=MESH).wait_recv()
                cp = pltpu.make_async_remote_copy(
                    src_ref=landed, dst_ref=landed, send_sem=send_d.at[3 * a + j], recv_sem=recv_d.at[3 * a + j],
                    device_id=(x, y, 1 - c), device_id_type=MESH)
                cp.start()
                sends.append(cp)
        for a in range(n):
            h = shards[a].shape[0] // 2
            for j, chip in enumerate(chips):
                src = 2 * chip[0] + chip[1]
                other = outs[a].at[src, _half(1 - c, h)]
                pltpu.make_async_remote_copy(
                    src_ref=other, dst_ref=other, send_sem=send_d.at[3 * a + j], recv_sem=recv_d.at[3 * a + j],
                    device_id=(x, y, 1 - c), device_id_type=MESH).wait_recv()
        for cp in sends:
            cp.wait_send()

    return pl.pallas_call(
        body, name="allgather_weights", in_specs=[ANY] * n, out_specs=[ANY] * n,
        out_shape=[jax.ShapeDtypeStruct((N_CHIPS,) + s.shape, s.dtype) for s in shards],
        scratch_shapes=[pltpu.SemaphoreType.DMA((3 * n,))] * 4,
    )(*shards)


def _pair_exchange_call(grads):
    n = len(grads)

    def body(*refs):
        ins, outs = refs[:n], refs[n:2 * n]
        send_d, recv_d = refs[2 * n:]
        x, y, c, _ = _place()
        cps = []
        for a in range(n):
            h = grads[a].shape[1] // 2
            cp = pltpu.make_async_remote_copy(
                src_ref=ins[a].at[:, _half(1 - c, h)], dst_ref=outs[a], send_sem=send_d.at[a], recv_sem=recv_d.at[a],
                device_id=(x, y, 1 - c), device_id_type=MESH)
            cp.start()
            cps.append(cp)
        for cp in cps:
            cp.wait()

    return pl.pallas_call(
        body, name="grad_pair_exchange", in_specs=[ANY] * n, out_specs=[ANY] * n,
        out_shape=[jax.ShapeDtypeStruct((N_CHIPS, g.shape[1] // 2, g.shape[2]), BF16) for g in grads],
        scratch_shapes=[pltpu.SemaphoreType.DMA((n,))] * 2,
    )(*grads)


def _pair_sum_call(own, got, c_idx, name):
    _, rows, cols = own.shape
    h = rows // 2
    th = _pick(h, [256, 176, 8])
    nt = h // th

    def body(c_ref, own_ref, got_ref, o_ref):
        o_ref[...] = (own_ref[...] + got_ref[...].astype(F32)).astype(BF16)

    return pl.pallas_call(
        body, name=name,
        grid_spec=pltpu.PrefetchScalarGridSpec(
            num_scalar_prefetch=1, grid=(N_CHIPS, nt),
            in_specs=[pl.BlockSpec((None, th, cols), lambda s, i, c_ref: (s, c_ref[0] * nt + i, 0)),
                      pl.BlockSpec((None, th, cols), lambda s, i, c_ref: (s, i, 0))],
            out_specs=pl.BlockSpec((None, th, cols), lambda s, i, c_ref: (s, i, 0))),
        out_shape=jax.ShapeDtypeStruct((N_CHIPS, h, cols), BF16),
        compiler_params=_cparams(("parallel", "parallel")))(c_idx, own, got)


def _chip_exchange_call(pairs):
    n = len(pairs)

    def body(*refs):
        ins, outs = refs[:n], refs[n:2 * n]
        send_i, recv_i = refs[2 * n:]
        x, y, c, chips = _place()
        me = 2 * x + y
        sends = []
        for a in range(n):
            for j, chip in enumerate(chips):
                dst = 2 * chip[0] + chip[1]
                cp = pltpu.make_async_remote_copy(
                    src_ref=ins[a].at[dst], dst_ref=outs[a].at[me], send_sem=send_i.at[3 * a + j],
                    recv_sem=recv_i.at[3 * a + j], device_id=(*chip, c), device_id_type=MESH)
                cp.start()
                sends.append(cp)
        for a in range(n):
            for j, chip in enumerate(chips):
                src = 2 * chip[0] + chip[1]
                pltpu.make_async_remote_copy(
                    src_ref=ins[a].at[src], dst_ref=outs[a].at[src], send_sem=send_i.at[3 * a + j],
                    recv_sem=recv_i.at[3 * a + j], device_id=(*chip, c), device_id_type=MESH).wait_recv()
        for cp in sends:
            cp.wait_send()

    return pl.pallas_call(
        body, name="grad_chip_exchange", in_specs=[ANY] * n, out_specs=[ANY] * n,
        out_shape=[jax.ShapeDtypeStruct(p.shape, BF16) for p in pairs],
        scratch_shapes=[pltpu.SemaphoreType.DMA((3 * n,))] * 2,
    )(*pairs)


def _chip_sum_call(pair, by_chip, place, name):
    _, h, cols = pair.shape
    th = _pick(h, [256, 176, 8])
    nt = h // th

    def body(p_ref, own_ref, a_ref, b_ref, c_ref, o_ref):
        s = own_ref[...].astype(F32) + a_ref[...].astype(F32)
        o_ref[...] = (s + b_ref[...].astype(F32)) + c_ref[...].astype(F32)

    def src(k):
        return pl.BlockSpec((None, th, cols), lambda i, p_ref: (p_ref[k], i, 0))

    return pl.pallas_call(
        body, name=name,
        grid_spec=pltpu.PrefetchScalarGridSpec(
            num_scalar_prefetch=1, grid=(nt,), in_specs=[src(1), src(2), src(3), src(4)],
            out_specs=pl.BlockSpec((th, cols), lambda i, p_ref: (p_ref[0] * nt + i, 0))),
        out_shape=jax.ShapeDtypeStruct((2 * h, cols), F32),
        compiler_params=_cparams(("parallel",)))(place, pair, by_chip, by_chip, by_chip)


def _half_exchange_call(shards):
    n = len(shards)

    def body(*refs):
        outs = refs[n:2 * n]
        send_d, recv_d = refs[2 * n:]
        x, y, c, _ = _place()
        cps = []
        for a in range(n):
            h = shards[a].shape[0] // 2
            mine = outs[a].at[_half(c, h)]
            cp = pltpu.make_async_remote_copy(
                src_ref=mine, dst_ref=mine, send_sem=send_d.at[a], recv_sem=recv_d.at[a],
                device_id=(x, y, 1 - c), device_id_type=MESH)
            cp.start()
            cps.append(cp)
        for a, cp in enumerate(cps):
            h = shards[a].shape[0] // 2
            theirs = outs[a].at[_half(1 - c, h)]
            pltpu.make_async_remote_copy(
                src_ref=theirs, dst_ref=theirs, send_sem=send_d.at[a], recv_sem=recv_d.at[a],
                device_id=(x, y, 1 - c), device_id_type=MESH).wait_recv()
            cp.wait_send()

    return pl.pallas_call(
        body, name="grad_half_exchange", in_specs=[ANY] * n, out_specs=[ANY] * n,
        out_shape=[jax.ShapeDtypeStruct(sv.shape, F32) for sv in shards],
        input_output_aliases={a: a for a in range(n)},
        scratch_shapes=[pltpu.SemaphoreType.DMA((n,))] * 2,
    )(*shards)


def _allreduce_small_call(v):
    rows = v.shape[0]

    def body(v_ref, o_ref, gath, send_sems, recv_sems):
        x, y, c, chips = _place()
        me, sibling = (x, y, c), (x, y, 1 - c)

        def slot(px, py, pc):
            return gath.at[4 * px + 2 * py + pc]

        def copy(k, block, to, src=None):
            return pltpu.make_async_remote_copy(
                src_ref=slot(*block) if src is None else src, dst_ref=slot(*block),
                send_sem=send_sems.at[k], recv_sem=recv_sems.at[k], device_id=to, device_id_type=MESH)

        gath[4 * x + 2 * y + c] = v_ref[...]
        first = [copy(0, me, sibling, src=v_ref)]
        first += [copy(1 + j, me, (*chip, c), src=v_ref) for j, chip in enumerate(chips)]
        for cp in first:
            cp.start()
        passed = [copy(4 + j, (*chip, c), sibling) for j, chip in enumerate(chips)]
        for j, chip in enumerate(chips):
            copy(1 + j, (*chip, c), me).wait_recv()
            passed[j].start()
        copy(0, sibling, me).wait_recv()
        for j, chip in enumerate(chips):
            copy(4 + j, (*chip, 1 - c), me).wait_recv()
        for cp in first + passed:
            cp.wait_send()
        acc = gath[0]
        for d in range(1, 8):
            acc = acc + gath[d]
        o_ref[...] = acc

    vm = pl.BlockSpec(memory_space=pltpu.VMEM)
    return pl.pallas_call(
        body, name="allreduce_small", in_specs=[vm], out_specs=vm,
        out_shape=jax.ShapeDtypeStruct((rows, BLK), F32),
        scratch_shapes=[pltpu.VMEM((8, rows, BLK), F32), pltpu.SemaphoreType.DMA((7,)),
                        pltpu.SemaphoreType.DMA((7,))],
        compiler_params=pltpu.CompilerParams(vmem_limit_bytes=VMEM_LIMIT),
    )(v)


def _pack(arrs, min_rows=8):
    parts = []
    for a in arrs:
        flat = a.reshape(-1).astype(F32)
        parts.append(jnp.pad(flat, (0, (-flat.shape[0]) % BLK)))
    buf = jnp.concatenate(parts).reshape(-1, BLK)
    return jnp.pad(buf, ((0, (-buf.shape[0]) % min_rows), (0, 0)))


def _unpack(buf, shapes):
    out, r = [], 0
    for shp in shapes:
        n = math.prod(shp)
        nr = -(-n // BLK)
        out.append(buf[r:r + nr].reshape(-1)[:n].reshape(shp))
        r += nr
    return out


SMALL = ["meta_tokens", "mix_pre_g", "ssd_conv_w", "ssd_conv_b", "ssd_dt_bias", "ssd_a_log", "ssd_d", "ssd_norm_g",
         "sb_norm_g", "mix_post_g", "ffn_pre_g", "ffn_conv_w", "ffn_conv_b", "ffn_post_g"]
BIG = ["w_in", "w_out", "w_up", "w_down"]
WEIGHTS = ["meta_tokens", "mix_pre_g", "w_in", "ssd_conv_w", "ssd_conv_b", "ssd_dt_bias", "ssd_a_log", "ssd_d",
           "ssd_norm_g", "sb_norm_g", "w_out", "mix_post_g", "ffn_pre_g", "w_up", "ffn_conv_w", "ffn_conv_b",
           "w_down", "ffn_post_g"]
W_IN_SHARD = IN_COLS // N_CHIPS
W_IN_PAD = 1536


def kernel(x, meta_tokens, mix_pre_g, w_in, ssd_conv_w, ssd_conv_b, ssd_dt_bias, ssd_a_log, ssd_d, ssd_norm_g, sb_norm_g, w_out, mix_post_g, ffn_pre_g, w_up, ffn_conv_w, ffn_conv_b, w_down, ffn_post_g, loss_target, m_meta_tokens, m_mix_pre_g, m_w_in, m_ssd_conv_w, m_ssd_conv_b, m_ssd_dt_bias, m_ssd_a_log, m_ssd_d, m_ssd_norm_g, m_sb_norm_g, m_w_out, m_mix_post_g, m_ffn_pre_g, m_w_up, m_ffn_conv_w, m_ffn_conv_b, m_w_down, m_ffn_post_g, v_meta_tokens, v_mix_pre_g, v_w_in, v_ssd_conv_w, v_ssd_conv_b, v_ssd_dt_bias, v_ssd_a_log, v_ssd_d, v_ssd_norm_g, v_sb_norm_g, v_w_out, v_mix_post_g, v_ffn_pre_g, v_w_up, v_ffn_conv_w, v_ffn_conv_b, v_w_down, v_ffn_post_g):
    w = dict(meta_tokens=meta_tokens, mix_pre_g=mix_pre_g, w_in=w_in, ssd_conv_w=ssd_conv_w, ssd_conv_b=ssd_conv_b, ssd_dt_bias=ssd_dt_bias, ssd_a_log=ssd_a_log, ssd_d=ssd_d, ssd_norm_g=ssd_norm_g, sb_norm_g=sb_norm_g, w_out=w_out, mix_post_g=mix_post_g, ffn_pre_g=ffn_pre_g, w_up=w_up, ffn_conv_w=ffn_conv_w, ffn_conv_b=ffn_conv_b, w_down=w_down, ffn_post_g=ffn_post_g)
    m = dict(meta_tokens=m_meta_tokens, mix_pre_g=m_mix_pre_g, w_in=m_w_in, ssd_conv_w=m_ssd_conv_w, ssd_conv_b=m_ssd_conv_b, ssd_dt_bias=m_ssd_dt_bias, ssd_a_log=m_ssd_a_log, ssd_d=m_ssd_d, ssd_norm_g=m_ssd_norm_g, sb_norm_g=m_sb_norm_g, w_out=m_w_out, mix_post_g=m_mix_post_g, ffn_pre_g=m_ffn_pre_g, w_up=m_w_up, ffn_conv_w=m_ffn_conv_w, ffn_conv_b=m_ffn_conv_b, w_down=m_w_down, ffn_post_g=m_ffn_post_g)
    v = dict(meta_tokens=v_meta_tokens, mix_pre_g=v_mix_pre_g, w_in=v_w_in, ssd_conv_w=v_ssd_conv_w, ssd_conv_b=v_ssd_conv_b, ssd_dt_bias=v_ssd_dt_bias, ssd_a_log=v_ssd_a_log, ssd_d=v_ssd_d, ssd_norm_g=v_ssd_norm_g, sb_norm_g=v_sb_norm_g, w_out=v_w_out, mix_post_g=v_mix_post_g, ffn_pre_g=v_ffn_pre_g, w_up=v_w_up, ffn_conv_w=v_ffn_conv_w, ffn_conv_b=v_ffn_conv_b, w_down=v_w_down, ffn_post_g=v_ffn_post_g)
    chip = 2 * lax.axis_index("x") + lax.axis_index("y")
    place = jnp.stack([lax.axis_index("c"), chip, chip ^ 2, chip ^ 1, chip ^ 3]).astype(jnp.int32)

    shard_small = [w["meta_tokens"], w["ssd_conv_w"][0], w["ffn_conv_w"][0]]
    shards = [
        jnp.pad(w["w_in"][0], ((0, 0), (0, W_IN_PAD - W_IN_SHARD))).astype(BF16),
        w["w_out"][0].astype(BF16), w["w_up"][0].astype(BF16), w["w_down"][0].astype(BF16),
        _pack(shard_small, 16),
    ]
    g_in, g_out, g_up, g_down, g_small = [
        lax.dynamic_update_slice(g, s[None], (chip, 0, 0)) for g, s in zip(_allgather_call(shards), shards)]
    w_in_ref = jnp.concatenate([g_in[i, :, :W_IN_SHARD] for i in range(N_CHIPS)], axis=1)
    w_in_c = jnp.concatenate([w_in_ref[:, :DT_REAL_OFF + N_HEADS], jnp.zeros((D_MODEL, BLK - N_HEADS), BF16),
                              w_in_ref[:, DT_REAL_OFF + N_HEADS:]], axis=1)
    parts = [_unpack(g_small[i], [s.shape for s in shard_small]) for i in range(N_CHIPS)]
    wt = {k: w[k][0][None] if w[k].ndim == 3 else w[k] for k in
          ["mix_pre_g", "ssd_conv_b", "ssd_dt_bias", "ssd_a_log", "ssd_d", "ssd_norm_g", "sb_norm_g", "mix_post_g",
           "ffn_pre_g", "ffn_conv_b", "ffn_post_g"]}
    wt.update(
        meta=jnp.concatenate([p[0] for p in parts], axis=1),
        ssd_conv_w=jnp.concatenate([p[1] for p in parts], axis=1),
        ffn_conv_w=jnp.concatenate([p[2] for p in parts], axis=1),
        w_in=w_in_c, w_out=g_out.reshape(2 * D_MODEL, D_MODEL),
        w_up=jnp.concatenate([g_up[i] for i in range(N_CHIPS)], axis=1),
        w_down=g_down.reshape(D_FF, D_MODEL))

    loss_row, dx, small, big = _local_step(x[0], loss_target[0], wt)

    dw_in = big["w_in"]
    dw_in_ref = jnp.concatenate([dw_in[:, :DT_REAL_OFF + N_HEADS], dw_in[:, C_Q:]], axis=1)
    dw_in_s = jnp.stack([jnp.pad(dw_in_ref[:, i * W_IN_SHARD:(i + 1) * W_IN_SHARD],
                                 ((0, 0), (0, W_IN_PAD - W_IN_SHARD))) for i in range(N_CHIPS)])
    own = [dw_in_s, big["w_out"][0].reshape(N_CHIPS, -1, D_MODEL), big["w_up"][0],
           big["w_down"][0].reshape(N_CHIPS, -1, D_MODEL)]
    own_b = [dw_in_s.astype(BF16), big["w_out"][1].reshape(N_CHIPS, -1, D_MODEL), big["w_up"][1],
             big["w_down"][1].reshape(N_CHIPS, -1, D_MODEL)]
    got = _pair_exchange_call(own_b)
    pairs = [_pair_sum_call(own[a], got[a], place, "pair_sum_" + BIG[a]) for a in range(4)]
    by_chip = _chip_exchange_call(pairs)
    full = _half_exchange_call(
        [_chip_sum_call(pairs[a], by_chip[a], place, "chip_sum_" + BIG[a]) for a in range(4)])
    grads = {"w_in": full[0][:, :W_IN_SHARD], "w_out": full[1], "w_up": full[2], "w_down": full[3]}

    small_list = [small[k] for k in SMALL] + [jnp.sum(loss_row).reshape(1, 1)]
    red = _allreduce_small_call(_pack(small_list, 8))
    red_list = _unpack(red, [a.shape for a in small_list])
    loss = red_list[-1].reshape(())
    for k, g in zip(SMALL, red_list[:-1]):
        grads[k] = g
    for k in ["meta_tokens", "ssd_conv_w", "ffn_conv_w"]:
        wk = w[k].shape[-1]
        grads[k] = lax.dynamic_slice_in_dim(grads[k], chip * wk, wk, axis=1)

    delta, new_m, new_v = {}, {}, {}
    for k in BIG:
        delta[k], new_m[k], new_v[k] = _adamw_call(w[k][0], grads[k], m[k][0], v[k][0], "adamw_" + k)
    sm_shapes = [grads[k].shape for k in SMALL]
    res = _adamw_call(_pack([w[k] for k in SMALL]), _pack([grads[k] for k in SMALL]),
                      _pack([m[k] for k in SMALL]), _pack([v[k] for k in SMALL]), "adamw_small")
    for out, buf in zip((delta, new_m, new_v), res):
        for k, a in zip(SMALL, _unpack(buf, sm_shapes)):
            out[k] = a

    def shaped(d, k):
        return d[k].reshape(w[k].shape)

    return (loss, dx[None], *[shaped(grads, k) for k in WEIGHTS], *[shaped(delta, k) for k in WEIGHTS],
            *[shaped(new_m, k) for k in WEIGHTS], *[shaped(new_v, k) for k in WEIGHTS])
```

```python
import functools
import math

import jax
import jax.numpy as jnp
from jax import lax
from jax.experimental import pallas as pl
from jax.experimental.pallas import tpu as pltpu

F32 = jnp.float32
BF16 = jnp.bfloat16

D_MODEL = 1024
N_META = 16
BLK = 128
PAD = BLK - N_META
HEAD_DIM = 64
N_HEADS = 16
SSD_GROUPS = 2
SSD_STATE = 128
SSD_INNER = 1024
XBC = SSD_INNER + 2 * SSD_GROUPS * SSD_STATE
D_FF = 2816
EPS = 1e-6
IN_COLS = 5648
C_Z, C_XBC, C_DT, C_Q, C_K, C_V, C_END = 0, 1024, 2560, 2688, 3712, 4736, 5760
DT_REAL_OFF = 2560
N_CHIPS = 4
ADAM_LR, ADAM_B1, ADAM_B2, ADAM_EPS, ADAM_WD, ADAM_STEP = 0.001, 0.9, 0.999, 1e-08, 0.01, 10
VMEM_LIMIT = 56 * 1024 * 1024
MESH = pl.DeviceIdType.MESH


def _cparams(sem=None, **kw):
    if sem is not None:
        kw["dimension_semantics"] = sem
    return pltpu.CompilerParams(vmem_limit_bytes=VMEM_LIMIT, **kw)


def _pick(n, cands):
    for c in cands:
        if n % c == 0:
            return c
    raise ValueError((n, cands))


def _iota(shape, dim):
    return lax.broadcasted_iota(jnp.int32, shape, dim)


def _sigmoid(x):
    return 1.0 / (1.0 + jnp.exp(-x))


def _split3(v):
    h1 = v.astype(BF16)
    r1 = v - h1.astype(F32)
    h2 = r1.astype(BF16)
    h3 = (r1 - h2.astype(F32)).astype(BF16)
    return h1, h2, h3


def _dot(a, b, ca=1, cb=0):
    return lax.dot_general(a, b, (((ca,), (cb,)), ((), ())), preferred_element_type=F32)


def _dot_sel_r(v, sel, cb=0):
    h1, h2, h3 = _split3(v)
    return _dot(h1, sel, 1, cb) + _dot(h2, sel, 1, cb) + _dot(h3, sel, 1, cb)


def _dot_sel_l(sel, v, ca=1):
    h1, h2, h3 = _split3(v)
    return _dot(sel, h1, ca, 0) + _dot(sel, h2, ca, 0) + _dot(sel, h3, ca, 0)


def _mm(a, b, *, ta=False, tb=False, tm, tn, tk, out_dtype=F32, nsplit=1, extra_bf16=False, name):
    K, M = (a.shape if ta else a.shape[::-1])
    N = b.shape[0] if tb else b.shape[1]
    assert M % tm == 0 and N % tn == 0 and K % tk == 0, (name, M, N, K, tm, tn, tk)
    nm, nn, nk = M // tm, N // tn, K // tk
    assert nn % nsplit == 0
    per = nn // nsplit
    a_spec = (pl.BlockSpec((tk, tm), lambda i, j, k: (k, i)) if ta
              else pl.BlockSpec((tm, tk), lambda i, j, k: (i, k)))
    b_spec = (pl.BlockSpec((tn, tk), lambda i, j, k: (j, k)) if tb
              else pl.BlockSpec((tk, tn), lambda i, j, k: (k, j)))
    o_spec = pl.BlockSpec((None, tm, tn), lambda i, j, k: (j // per, i, j % per))
    n_out = 2 if extra_bf16 else 1
    ca, cb = (0 if ta else 1), (1 if tb else 0)

    def body(a_ref, b_ref, *rest):
        outs = rest[:n_out]
        p = _dot(a_ref[...].astype(BF16), b_ref[...].astype(BF16), ca, cb)

        def emit(val):
            outs[0][...] = val.astype(out_dtype)
            if extra_bf16:
                outs[1][...] = val.astype(BF16)

        if nk == 1:
            emit(p)
        else:
            acc = rest[n_out]
            k = pl.program_id(2)

            @pl.when(k == 0)
            def _():
                acc[...] = p

            @pl.when(k > 0)
            def _():
                acc[...] += p

            @pl.when(k == nk - 1)
            def _():
                emit(acc[...])

    shp = (nsplit, M, N // nsplit)
    out_shape = [jax.ShapeDtypeStruct(shp, out_dtype)]
    out_specs = [o_spec]
    if extra_bf16:
        out_shape.append(jax.ShapeDtypeStruct(shp, BF16))
        out_specs.append(o_spec)
    res = pl.pallas_call(
        body, name=name, grid=(nm, nn, nk), in_specs=[a_spec, b_spec], out_specs=out_specs,
        out_shape=out_shape, scratch_shapes=([pltpu.VMEM((tm, tn), F32)] if nk > 1 else []),
        compiler_params=_cparams(("parallel", "parallel", "arbitrary")),
    )(a, b)
    return res if extra_bf16 else res[0]


def _rms_stats(x):
    r = lax.rsqrt(jnp.mean(x * x, axis=-1, keepdims=True) + EPS)
    return r, x * r


def _rms_bwd(x, g, dy):
    r, xh = _rms_stats(x)
    dxh = dy * g
    dx = r * (dxh - xh * jnp.mean(dxh * xh, axis=-1, keepdims=True))
    return dx, jnp.sum(dy * xh, axis=0, keepdims=True)


def _row_spec(tr, w, col=0):
    return pl.BlockSpec((tr, w), lambda i: (i, col))


def _vec_spec(w):
    return pl.BlockSpec((1, w), lambda i: (0, 0))


def _acc_rows(ref, val, i):
    @pl.when(i == 0)
    def _():
        ref[...] = val

    @pl.when(i > 0)
    def _():
        ref[...] += val


def _rms_fwd_call(x, g, name):
    lp, w = x.shape
    tr = _pick(lp, [384, 128])

    def body(x_ref, g_ref, o_ref):
        _, xh = _rms_stats(x_ref[...])
        o_ref[...] = (xh * g_ref[...]).astype(BF16)

    return pl.pallas_call(
        body, name=name, grid=(lp // tr,), in_specs=[_row_spec(tr, w), _vec_spec(w)],
        out_specs=_row_spec(tr, w), out_shape=jax.ShapeDtypeStruct((lp, w), BF16),
        compiler_params=_cparams(("parallel",)))(x, g)


def _mid_fwd_call(h0, mix, g_post, g_pre2):
    lp, w = h0.shape
    tr = _pick(lp, [384, 128])

    def body(h0_ref, mix_ref, gp_ref, g2_ref, h1_ref, xn_ref):
        _, mh = _rms_stats(mix_ref[...])
        h1 = h0_ref[...] + mh * gp_ref[...]
        h1_ref[...] = h1
        _, hh = _rms_stats(h1)
        xn_ref[...] = (hh * g2_ref[...]).astype(BF16)

    return pl.pallas_call(
        body, name="mid_fwd", grid=(lp // tr,),
        in_specs=[_row_spec(tr, w), _row_spec(tr, w), _vec_spec(w), _vec_spec(w)],
        out_specs=[_row_spec(tr, w), _row_spec(tr, w)],
        out_shape=[jax.ShapeDtypeStruct((lp, w), F32), jax.ShapeDtypeStruct((lp, w), BF16)],
        compiler_params=_cparams(("parallel",)))(h0, mix, g_post, g_pre2)


def _final_call(h1, f, g_post, target):
    lp, w = h1.shape
    tr = BLK
    nb = lp // tr

    def body(h1_ref, f_ref, g_ref, t_ref, loss_ref, df_ref, dh_ref, dg_ref):
        i = pl.program_id(0)
        fv = f_ref[...]
        g = g_ref[...]
        _, fh = _rms_stats(fv)
        h2 = h1_ref[...] + fh * g
        diff = jnp.where(i > 0, h2 - t_ref[...], 0.0)
        part = 0.5 * jnp.sum(diff * diff, axis=0, keepdims=True) * (1.0 / w)
        _acc_rows(loss_ref, part, i)
        dh = diff * (1.0 / w)
        dh_ref[...] = dh
        df, dg = _rms_bwd(fv, g, dh)
        df_ref[...] = df.astype(BF16)
        _acc_rows(dg_ref, dg, i)

    t_spec = pl.BlockSpec((tr, w), lambda i: (jnp.maximum(i - 1, 0), 0))
    return pl.pallas_call(
        body, name="final_fwd_bwd", grid=(nb,),
        in_specs=[_row_spec(tr, w), _row_spec(tr, w), _vec_spec(w), t_spec],
        out_specs=[_vec_spec(w), _row_spec(tr, w), _row_spec(tr, w), _vec_spec(w)],
        out_shape=[jax.ShapeDtypeStruct((1, w), F32), jax.ShapeDtypeStruct((lp, w), BF16),
                   jax.ShapeDtypeStruct((lp, w), F32), jax.ShapeDtypeStruct((1, w), F32)],
        compiler_params=_cparams(("arbitrary",)))(h1, f, g_post, target)


def _mid_bwd_call(dh2, h1, dxn2, mix, g_pre2, g_post):
    lp, w = h1.shape
    tr = _pick(lp, [384, 128])

    def body(dh2_ref, h1_ref, dxn_ref, mix_ref, g2_ref, gp_ref, dh1_ref, dmix_ref, dg2_ref, dgp_ref):
        i = pl.program_id(0)
        live = (i * tr + _iota((tr, 1), 0)) >= PAD
        dx, dg2 = _rms_bwd(h1_ref[...], g2_ref[...], dxn_ref[...])
        dh1 = jnp.where(live, dh2_ref[...] + dx, 0.0)
        dh1_ref[...] = dh1
        dmix, dgp = _rms_bwd(mix_ref[...], gp_ref[...], dh1)
        dmix_ref[...] = jnp.where(live, dmix, 0.0).astype(BF16)
        _acc_rows(dg2_ref, dg2, i)
        _acc_rows(dgp_ref, dgp, i)

    rs = _row_spec(tr, w)
    return pl.pallas_call(
        body, name="mid_bwd", grid=(lp // tr,),
        in_specs=[rs, rs, rs, rs, _vec_spec(w), _vec_spec(w)],
        out_specs=[rs, rs, _vec_spec(w), _vec_spec(w)],
        out_shape=[jax.ShapeDtypeStruct((lp, w), F32), jax.ShapeDtypeStruct((lp, w), BF16),
                   jax.ShapeDtypeStruct((1, w), F32), jax.ShapeDtypeStruct((1, w), F32)],
        compiler_params=_cparams(("arbitrary",)))(dh2, h1, dxn2, mix, g_pre2, g_post)


def _norm_bwd_call(x, g, dy_arr, dy_col, name, res=None):
    lp, w = x.shape
    tr = _pick(lp, [384, 128])
    has_res = res is not None

    def body(x_ref, g_ref, dy_ref, *rest):
        i = pl.program_id(0)
        live = (i * tr + _iota((tr, 1), 0)) >= PAD
        dx, dg = _rms_bwd(x_ref[...], g_ref[...], dy_ref[...])
        if has_res:
            dx = dx + rest[0][...]
        out_ref, dg_ref = rest[-2], rest[-1]
        out_ref[...] = jnp.where(live, dx, 0.0)
        _acc_rows(dg_ref, dg, i)

    rs = _row_spec(tr, w)
    ins = [rs, _vec_spec(w), _row_spec(tr, w, dy_col)] + ([rs] if has_res else [])
    args = [x, g, dy_arr] + ([res] if has_res else [])
    return pl.pallas_call(
        body, name=name, grid=(lp // tr,), in_specs=ins, out_specs=[rs, _vec_spec(w)],
        out_shape=[jax.ShapeDtypeStruct((lp, w), F32), jax.ShapeDtypeStruct((1, w), F32)],
        compiler_params=_cparams(("arbitrary",)))(*args)


def _shift_down(cur, prev_tail, s, rows):
    if s == 0:
        return cur
    prev = jnp.tile(prev_tail, (BLK // 8, 1))
    return jnp.where(rows >= s, pltpu.roll(cur, s, 0), pltpu.roll(prev, s, 0))


def _shift_up(cur, next_head, s, rows):
    if s == 0:
        return cur
    nxt = jnp.tile(next_head, (BLK // 8, 1))
    return jnp.where(rows < BLK - s, pltpu.roll(cur, BLK - s, 0), pltpu.roll(nxt, BLK - s, 0))


def _gelu_tanh(x):
    c = math.sqrt(2.0 / math.pi)
    t = jnp.tanh(c * (x + 0.044715 * x * x * x))
    return 0.5 * x * (1.0 + t), t


def _conv_fwd_call(src, col0, width, cw, w8, b, taps, *, gate_src=None, gate_col0=0, name):
    lp = src.shape[0]
    nb, nc = lp // BLK, width // cw
    cb0 = col0 // cw
    ffn = gate_src is not None

    def body(x_ref, w_ref, b_ref, *rest):
        if ffn:
            u_ref, y_ref, a_ref, tail = rest
        else:
            y_ref, a_ref, tail = rest
        i = pl.program_id(1)

        @pl.when(i == 0)
        def _():
            tail[...] = jnp.zeros_like(tail)

        cur = x_ref[...]
        rows = _iota((BLK, cw), 0)
        y = b_ref[...] + w_ref[taps - 1:taps, :] * cur
        pt = tail[...]
        for s in range(1, taps):
            y = y + w_ref[taps - 1 - s:taps - s, :] * _shift_down(cur, pt, s, rows)
        tail[...] = cur[BLK - 8:, :]
        y_ref[...] = y
        if ffn:
            ge, _ = _gelu_tanh(y)
            a_ref[...] = (ge * u_ref[...]).astype(BF16)
        else:
            live = (i * BLK + rows) >= PAD
            a_ref[...] = jnp.where(live, y * _sigmoid(y), 0.0)

    blk = lambda c0: pl.BlockSpec((BLK, cw), lambda j, i: (i, c0 + j))
    ins = [blk(cb0), pl.BlockSpec((8, cw), lambda j, i: (0, j)), pl.BlockSpec((1, cw), lambda j, i: (0, j))]
    args = [src, w8, b]
    if ffn:
        ins.append(blk(gate_col0 // cw))
        args.append(gate_src)
    return pl.pallas_call(
        body, name=name, grid=(nc, nb), in_specs=ins, out_specs=[blk(0), blk(0)],
        out_shape=[jax.ShapeDtypeStruct((lp, width), F32),
                   jax.ShapeDtypeStruct((lp, width), BF16 if ffn else F32)],
        scratch_shapes=[pltpu.VMEM((8, cw), F32)],
        compiler_params=_cparams(("parallel", "arbitrary")))(*args)


def _conv_bwd_call(src, col0, width, cw, w8, taps, ypre, dact, *, gate_src=None, gate_col0=0, name):
    lp = src.shape[0]
    nb, nc = lp // BLK, width // cw
    cb0 = col0 // cw
    ffn = gate_src is not None

    def body(x_ref, w_ref, y_ref, d_ref, *rest):
        if ffn:
            u_ref, dx_ref, du_ref, dw_ref, db_ref, head = rest
        else:
            dx_ref, dw_ref, db_ref, head = rest
        step = pl.program_id(1)
        i = nb - 1 - step

        @pl.when(step == 0)
        def _():
            head[...] = jnp.zeros_like(head)

        rows = _iota((BLK, cw), 0)
        live = (i * BLK + rows) >= PAD
        y = y_ref[...]
        d = d_ref[...]
        if ffn:
            ge, t = _gelu_tanh(y)
            c = math.sqrt(2.0 / math.pi)
            dge = 0.5 * (1.0 + t) + 0.5 * y * (1.0 - t * t) * c * (1.0 + 3.0 * 0.044715 * y * y)
            u = u_ref[...]
            du_ref[...] = jnp.where(live, d * ge, 0.0).astype(BF16)
            dy = jnp.where(live, d * u * dge, 0.0)
        else:
            sg = _sigmoid(y)
            dy = jnp.where(live, d * sg * (1.0 + y * (1.0 - sg)), 0.0)
        x = x_ref[...]
        nh = head[...]
        dx = jnp.zeros_like(dy)
        dws = []
        for s in range(taps):
            sh = _shift_up(dy, nh, s, rows)
            dx = dx + w_ref[taps - 1 - s:taps - s, :] * sh
            dws.append(jnp.sum(x * sh, axis=0, keepdims=True))
        head[...] = dy[:8, :]
        dx_ref[...] = jnp.where(live, dx, 0.0).astype(BF16)
        dw = jnp.concatenate([dws[taps - 1 - k] for k in range(taps)]
                             + [jnp.zeros((8 - taps, cw), F32)], axis=0)
        _acc_rows(dw_ref, dw, step)
        _acc_rows(db_ref, jnp.sum(dy, axis=0, keepdims=True), step)

    blk = lambda c0: pl.BlockSpec((BLK, cw), lambda j, s: (nb - 1 - s, c0 + j))
    ins = [blk(cb0), pl.BlockSpec((8, cw), lambda j, s: (0, j)), blk(0), blk(0)]
    args = [src, w8, ypre, dact]
    outs = [blk(0)]
    oshape = [jax.ShapeDtypeStruct((lp, width), BF16)]
    if ffn:
        ins.append(blk(gate_col0 // cw))
        args.append(gate_src)
        outs.append(blk(0))
        oshape.append(jax.ShapeDtypeStruct((lp, width), BF16))
    outs += [pl.BlockSpec((8, cw), lambda j, s: (0, j)), pl.BlockSpec((1, cw), lambda j, s: (0, j))]
    oshape += [jax.ShapeDtypeStruct((8, width), F32), jax.ShapeDtypeStruct((1, width), F32)]
    return pl.pallas_call(
        body, name=name, grid=(nc, nb), in_specs=ins, out_specs=outs, out_shape=oshape,
        scratch_shapes=[pltpu.VMEM((8, cw), F32)],
        compiler_params=_cparams(("parallel", "arbitrary")))(*args)


SB_GROUP = 4


def _sb_scores(qm_h, kb):
    z = _dot(qm_h, kb, 1, 1)
    sp = jnp.maximum(z, 0.0) + jnp.log(1.0 + jnp.exp(-jnp.abs(z)))
    return z - sp, -sp


def _sb_valid(i, off, width):
    kpos = off + _iota((BLK, width), 1)
    qpos = i * BLK + _iota((BLK, width), 0)
    return (kpos < qpos) & (kpos >= PAD)


def _dot_tri1(v, tri2):
    r = _dot(v.astype(BF16), tri2[:BLK])
    return r[:, :BLK], r[:, BLK:]


def _sb_groups(i):
    edge = i // SB_GROUP
    return edge, pl.multiple_of(edge * (SB_GROUP * BLK), BLK)


def _tri2(cond):
    t = jnp.concatenate([cond.astype(BF16), jnp.ones((BLK, BLK), BF16)], axis=1)
    return jnp.concatenate([t, t], axis=0)


def _dot_tri(v, tri2):
    hi = v.astype(BF16)
    lo = (v - hi.astype(F32)).astype(BF16)
    r = _dot(jnp.concatenate([hi, lo], axis=1), tri2)
    return r[:, :BLK], r[:, BLK:]


def _sb_fwd_call(proj):
    lp = proj.shape[0]
    nb = lp // BLK
    assert (nb - 1) % SB_GROUP == 0
    scale = 1.0 / math.sqrt(HEAD_DIM)

    def body(q_ref, k_ref, v_ref, o_ref, tl_ref):
        i = pl.program_id(1)
        lane = _iota((2 * BLK, BLK), 1)
        row = _iota((2 * BLK, BLK), 0)
        first = row < BLK
        qrow = row & (BLK - 1)
        q = q_ref[...] * scale
        q2 = jnp.concatenate([q, q], axis=0)
        qm = jnp.where(first == (lane < HEAD_DIM), q2, 0.0).astype(BF16)
        tri = _tri2(_iota((BLK, BLK), 0) > _iota((BLK, BLK), 1))

        def chunk(off, nsub, last_valid, carry):
            width = nsub * BLK
            sls = [slice(b * BLK, (b + 1) * BLK) for b in range(nsub)]
            kb = k_ref[pl.ds(off, width), :].astype(BF16)
            vb = v_ref[pl.ds(off, width), :].astype(BF16)
            lb, lk = _sb_scores(qm, kb)
            lks = [lk[:, sl] for sl in sls]
            first_valid = (off + lane) >= PAD
            lks[0] = jnp.where(first_valid, lks[0], 0.0)
            if last_valid is not None:
                lks[-1] = jnp.where(last_valid, lks[-1], 0.0)
            afters = [_dot_tri(lks[b], tri) for b in range(nsub)]
            run, acc = carry
            ws = [None] * nsub
            for b in reversed(range(nsub)):
                wb = jnp.exp(lb[:, sls[b]] + afters[b][0] + run)
                if b == 0:
                    wb = jnp.where(first_valid, wb, 0.0)
                if last_valid is not None and b == nsub - 1:
                    wb = jnp.where(last_valid, wb, 0.0)
                ws[b] = wb.astype(BF16)
                run = run + afters[b][1]
            w = ws[0] if nsub == 1 else jnp.concatenate(ws, axis=1)
            return run, acc + _dot(w, vb)

        edge, edge_off = _sb_groups(i)
        diag = lane < qrow
        zero = jnp.zeros((2 * BLK, BLK), F32)
        upto = [functools.partial(chunk, edge_off, r, diag) for r in range(1, SB_GROUP + 1)]
        carry = lax.switch(i - edge * SB_GROUP, upto, (zero, zero))

        def interior(t, carry):
            off = pl.multiple_of((edge - 2 - 2 * t) * (SB_GROUP * BLK), BLK)
            return chunk(off, 2 * SB_GROUP, None, carry)

        carry = lax.fori_loop(0, edge // 2, interior, carry)
        run, acc = lax.cond(edge % 2 == 1, lambda cr: chunk(0, SB_GROUP, None, cr), lambda cr: cr, carry)
        low = lane[:BLK] < HEAD_DIM
        o_ref[...] = jnp.where(low, acc[:BLK], acc[BLK:])
        tl_ref[...] = jnp.where(low, run[:BLK], run[BLK:])

    qc, kc, vc = C_Q // BLK, C_K // BLK, C_V // BLK
    blk = pl.BlockSpec((BLK, BLK), lambda p, i: (i, p))
    return pl.pallas_call(
        body, name="sb_fwd", grid=(N_HEADS // 2, nb),
        in_specs=[pl.BlockSpec((BLK, BLK), lambda p, i: (i, qc + p)),
                  pl.BlockSpec((lp, BLK), lambda p, i: (0, kc + p)),
                  pl.BlockSpec((lp, BLK), lambda p, i: (0, vc + p))],
        out_specs=[blk, blk],
        out_shape=[jax.ShapeDtypeStruct((lp, N_HEADS * HEAD_DIM), F32)] * 2,
        compiler_params=_cparams(("parallel", "arbitrary")))(proj, proj, proj)


def _sb_bwd_call(proj, tl, do):
    lp = proj.shape[0]
    nb = lp // BLK
    assert (nb - 1) % SB_GROUP == 0
    scale = 1.0 / math.sqrt(HEAD_DIM)

    def body(q_ref, k_ref, v_ref, tl_ref, do_ref, dq_ref, dk_ref, dv_ref, dk_acc, dv_acc):
        i = pl.program_id(1)

        @pl.when(i == 0)
        def _():
            dk_acc[...] = jnp.zeros_like(dk_acc)
            dv_acc[...] = jnp.zeros_like(dv_acc)

        lane = _iota((2 * BLK, BLK), 1)
        row = _iota((2 * BLK, BLK), 0)
        qrow = row & (BLK - 1)
        mine = (row < BLK) == (lane < HEAD_DIM)
        q = q_ref[...] * scale
        dov = do_ref[...]
        qm = jnp.where(mine, jnp.concatenate([q, q], axis=0), 0.0).astype(BF16)
        dom = jnp.where(mine, jnp.concatenate([dov, dov], axis=0), 0.0).astype(BF16)
        tlv = tl_ref[...]
        tot = jnp.concatenate([jnp.broadcast_to(tlv[:, 0:1], (BLK, BLK)),
                               jnp.broadcast_to(tlv[:, HEAD_DIM:HEAD_DIM + 1], (BLK, BLK))], axis=0)
        r1, l1 = _iota((BLK, BLK), 0), _iota((BLK, BLK), 1)
        tri_in = _tri2(r1 <= l1)
        tri_ex = _tri2(r1 < l1)

        def chunk(off, nsub, last_valid, carry):
            width = nsub * BLK
            sls = [slice(b * BLK, (b + 1) * BLK) for b in range(nsub)]
            cat = lambda parts: parts[0] if nsub == 1 else jnp.concatenate(parts, axis=1)
            mask_last = lambda b: last_valid is not None and b == nsub - 1
            kb = k_ref[pl.ds(off, width), :].astype(BF16)
            vb = v_ref[pl.ds(off, width), :].astype(BF16)
            lb, lk = _sb_scores(qm, kb)
            dw = _dot(dom, vb, 1, 1)
            lks = [lk[:, sl] for sl in sls]
            first_valid = (off + lane) >= PAD
            lks[0] = jnp.where(first_valid, lks[0], 0.0)
            if last_valid is not None:
                lks[-1] = jnp.where(last_valid, lks[-1], 0.0)
            pins = [_dot_tri(lks[b], tri_in) for b in range(nsub)]
            run, gsum, dq = carry
            ws, gs = [], []
            for b in range(nsub):
                wb = jnp.exp(lb[:, sls[b]] + (tot - run - pins[b][0]))
                if b == 0:
                    wb = jnp.where(first_valid, wb, 0.0)
                if mask_last(b):
                    wb = jnp.where(last_valid, wb, 0.0)
                ws.append(wb.astype(BF16))
                gs.append(wb * dw[:, sls[b]])
                run = run + pins[b][1]
            gexs = [_dot_tri1(gs[b], tri_ex) for b in range(nsub)]
            beta = jnp.exp(lb)
            parts = []
            for b in range(nsub):
                bt = beta[:, sls[b]]
                dzb = gs[b] * (1.0 - bt) - (gsum + gexs[b][0]) * bt
                if b == 0:
                    dzb = jnp.where(first_valid, dzb, 0.0)
                if mask_last(b):
                    dzb = jnp.where(last_valid, dzb, 0.0)
                parts.append(dzb.astype(BF16))
                gsum = gsum + gexs[b][1]
            dz, w = cat(parts), cat(ws)
            dk_acc[pl.ds(off, width), :] += _dot(dz, qm, 0, 0)
            dv_acc[pl.ds(off, width), :] += _dot(w, dom, 0, 0)
            return run, gsum, dq + _dot(dz, kb)

        edge, edge_off = _sb_groups(i)
        diag = lane < qrow
        zero = jnp.zeros((2 * BLK, BLK), F32)
        odd = edge % 2
        carry = lax.cond(odd == 1, lambda cr: chunk(0, SB_GROUP, None, cr), lambda cr: cr, (zero, zero, zero))

        def interior(t, carry):
            off = pl.multiple_of((odd + 2 * t) * (SB_GROUP * BLK), BLK)
            return chunk(off, 2 * SB_GROUP, None, carry)

        carry = lax.fori_loop(0, edge // 2, interior, carry)
        upto = [functools.partial(chunk, edge_off, r, diag) for r in range(1, SB_GROUP + 1)]
        dq = lax.switch(i - edge * SB_GROUP, upto, carry)[2]
        dq_ref[...] = (jnp.where(lane[:BLK] < HEAD_DIM, dq[:BLK], dq[BLK:]) * scale).astype(BF16)

        @pl.when(i == nb - 1)
        def _():
            dk_ref[...] = dk_acc[...].astype(BF16)
            dv_ref[...] = dv_acc[...].astype(BF16)

    qc, kc, vc = C_Q // BLK, C_K // BLK, C_V // BLK
    blk = pl.BlockSpec((BLK, BLK), lambda p, i: (i, p))
    full = pl.BlockSpec((lp, BLK), lambda p, i: (0, p))
    w = N_HEADS * HEAD_DIM
    return pl.pallas_call(
        body, name="sb_bwd", grid=(N_HEADS // 2, nb),
        in_specs=[pl.BlockSpec((BLK, BLK), lambda p, i: (i, qc + p)),
                  pl.BlockSpec((lp, BLK), lambda p, i: (0, kc + p)),
                  pl.BlockSpec((lp, BLK), lambda p, i: (0, vc + p)),
                  blk, blk],
        out_specs=[blk, full, full],
        out_shape=[jax.ShapeDtypeStruct((lp, w), BF16)] * 3,
        scratch_shapes=[pltpu.VMEM((lp, BLK), F32), pltpu.VMEM((lp, BLK), F32)],
        compiler_params=_cparams(("parallel", "arbitrary")))(proj, proj, proj, tl, do)


def _log1p(e):
    u = 1.0 + e
    return jnp.where(u == 1.0, e, jnp.log(u) * e / jnp.where(u == 1.0, 1.0, u - 1.0))


def _ssd_common(c, dtr, bias, alog):
    row = _iota((BLK, BLK), 0)
    lane = _iota((BLK, BLK), 1)
    live = ((c * BLK + row) >= PAD) & (lane < N_HEADS)
    pre = dtr + bias
    dt = jnp.where(live, jnp.maximum(pre, 0.0) + _log1p(jnp.exp(-jnp.abs(pre))), 0.0)
    a_neg = -jnp.exp(alog)
    a = dt * a_neg
    t_in = (lane <= row).astype(BF16)
    cs = _dot_sel_l(t_in, a)
    cs_t = cs.T
    cs_end = cs[BLK - 1:BLK, :]
    e = jnp.exp(cs)
    f = jnp.exp(cs_end - cs)
    xp = ((_iota((BLK, SSD_INNER), 1) // HEAD_DIM) == _iota((BLK, SSD_INNER), 0)).astype(BF16)
    xp_t = ((_iota((SSD_INNER, BLK), 0) // HEAD_DIM) == _iota((SSD_INNER, BLK), 1)).astype(BF16)
    decay_col = _dot_sel_l(xp_t, jnp.exp(cs_t))[:, BLK - 1:BLK]
    return dict(live=live, pre=pre, dt=dt, a_neg=a_neg, cs=cs, cs_t=cs_t, e=e, f=f, xp=xp, xp_t=xp_t,
                decay_col=decay_col, row=row, lane=lane,
                dt_x=_dot_sel_r(dt, xp), e_x=_dot_sel_r(e, xp), f_x=_dot_sel_r(f, xp))


def _ssd_ldec(q, h):
    diff = q["cs"][:, h:h + 1] - q["cs_t"][h:h + 1, :]
    causal = q["row"] >= q["lane"]
    return jnp.where(causal, jnp.exp(jnp.where(causal, diff, 0.0)), 0.0)


def _ssd_fwd_call(xbc, proj, bias, alog, d_x, norm_g):
    lp = xbc.shape[0]
    nb = lp // BLK
    gw = SSD_INNER // SSD_GROUPS
    ppg = gw // BLK

    def body(xbc_ref, dtr_ref, z_ref, bias_ref, alog_ref, dx_ref, ng_ref, yb_ref, ypre_ref, sprev_ref, s_ref):
        c = pl.program_id(0)

        @pl.when(c == 0)
        def _():
            s_ref[...] = jnp.zeros_like(s_ref)

        q = _ssd_common(c, dtr_ref[...], bias_ref[...], alog_ref[...])
        x = xbc_ref[:, 0:SSD_INNER]
        xd = x * q["dt_x"]
        low = q["lane"] < HEAD_DIM
        s_old = s_ref[...]
        sprev_ref[...] = s_old
        xdf = (xd * q["f_x"]).astype(BF16)
        for g in range(SSD_GROUPS):
            bg = xbc_ref[:, SSD_INNER + g * SSD_STATE:SSD_INNER + (g + 1) * SSD_STATE].astype(BF16)
            cg = xbc_ref[:, SSD_INNER + (SSD_GROUPS + g) * SSD_STATE:
                         SSD_INNER + (SSD_GROUPS + g + 1) * SSD_STATE].astype(BF16)
            cb = _dot(cg, bg, 1, 1)
            gs = slice(g * gw, (g + 1) * gw)
            y_off = _dot(cg, s_old[gs, :].astype(BF16), 1, 1) * q["e_x"][:, gs]
            s_ref[gs, :] = s_old[gs, :] * q["decay_col"][gs, :] + _dot(xdf[:, gs], bg, 0, 0)
            for pr in range(ppg):
                cols = slice(g * gw + pr * BLK, g * gw + (pr + 1) * BLK)
                xd_p = xd[:, cols]
                acc = y_off[:, pr * BLK:(pr + 1) * BLK]
                for hh in range(2):
                    h = (g * gw + pr * BLK) // HEAD_DIM + hh
                    m = (cb * _ssd_ldec(q, h)).astype(BF16)
                    xm = jnp.where(low, xd_p, 0.0) if hh == 0 else jnp.where(low, 0.0, xd_p)
                    acc = acc + _dot(m, xm.astype(BF16))
                ypre_ref[:, cols] = acc
        ypre = ypre_ref[...] + x * dx_ref[...]
        ypre_ref[...] = ypre
        z = z_ref[...]
        yg = ypre * (z * _sigmoid(z))
        _, yh = _rms_stats(yg)
        yb_ref[...] = (yh * ng_ref[...]).astype(BF16)

    row = lambda w, col: pl.BlockSpec((BLK, w), lambda c: (c, col))
    vec = lambda w: pl.BlockSpec((1, w), lambda c: (0, 0))
    return pl.pallas_call(
        body, name="ssd_fwd", grid=(nb,),
        in_specs=[row(XBC, 0), row(BLK, C_DT // BLK), row(SSD_INNER, 0), vec(BLK), vec(BLK),
                  vec(SSD_INNER), vec(SSD_INNER)],
        out_specs=[row(SSD_INNER, 0), row(SSD_INNER, 0),
                   pl.BlockSpec((None, SSD_INNER, SSD_STATE), lambda c: (c, 0, 0))],
        out_shape=[jax.ShapeDtypeStruct((lp, SSD_INNER), BF16), jax.ShapeDtypeStruct((lp, SSD_INNER), F32),
                   jax.ShapeDtypeStruct((nb, SSD_INNER, SSD_STATE), F32)],
        scratch_shapes=[pltpu.VMEM((SSD_INNER, SSD_STATE), F32)],
        compiler_params=_cparams(("arbitrary",)))(xbc, proj, proj, bias, alog, d_x, norm_g)


def _ssd_bwd_call(dycat, ypre, xbc, proj, sprev, bias, alog, d_x, norm_g):
    lp = xbc.shape[0]
    nb = lp // BLK
    gw = SSD_INNER // SSD_GROUPS
    ppg = gw // BLK

    def body(dy_ref, ypre_ref, xbc_ref, dtr_ref, z_ref, sp_ref, bias_ref, alog_ref, dxp_ref, ng_ref,
             dz_ref, dxbc_ref, ddt_ref, dng_ref, dd_ref, dal_ref, dbi_ref, ds_ref, dxd_ref):
        step = pl.program_id(0)
        c = nb - 1 - step

        @pl.when(step == 0)
        def _():
            ds_ref[...] = jnp.zeros_like(ds_ref)

        q = _ssd_common(c, dtr_ref[...], bias_ref[...], alog_ref[...])
        row, lane = q["row"], q["lane"]
        low = lane < HEAD_DIM
        rowlive = ((c * BLK + _iota((BLK, 1), 0)) >= PAD)
        x = xbc_ref[:, 0:SSD_INNER]
        xd = x * q["dt_x"]
        z = z_ref[...]
        sz = _sigmoid(z)
        silu = z * sz
        ypre = ypre_ref[...]
        dyg, dng = _rms_bwd(ypre * silu, ng_ref[...], dy_ref[...])
        _acc_rows(dng_ref, dng, step)
        dyp = dyg * silu
        dz_ref[...] = jnp.where(rowlive, dyg * ypre * (sz * (1.0 + z * (1.0 - sz))), 0.0).astype(BF16)
        _acc_rows(dd_ref, jnp.sum(dyp * x, axis=0, keepdims=True), step)
        dye = dyp * q["e_x"]
        xdf = xd * q["f_x"]
        s_prev = sp_ref[...]
        ds_old = ds_ref[...]
        qrow = jnp.zeros((BLK, BLK), F32)
        qcol_t = jnp.zeros((BLK, BLK), F32)
        red_e = []
        red_f = []
        for g in range(SSD_GROUPS):
            gs = slice(g * gw, (g + 1) * gw)
            bsl = slice(SSD_INNER + g * SSD_STATE, SSD_INNER + (g + 1) * SSD_STATE)
            csl = slice(SSD_INNER + (SSD_GROUPS + g) * SSD_STATE, SSD_INNER + (SSD_GROUPS + g + 1) * SSD_STATE)
            bg = xbc_ref[:, bsl].astype(BF16)
            cg = xbc_ref[:, csl].astype(BF16)
            sg = s_prev[gs, :].astype(BF16)
            dsg = ds_old[gs, :].astype(BF16)
            cb = _dot(cg, bg, 1, 1)
            bds = _dot(bg, dsg, 1, 1)
            y_off = _dot(cg, sg, 1, 1) * q["e_x"][:, gs]
            red_e.append(dyp[:, gs] * y_off)
            red_f.append(xd[:, gs] * bds * q["f_x"][:, gs])
            dc = _dot(dye[:, gs].astype(BF16), sg)
            db = _dot(xdf[:, gs].astype(BF16), dsg)
            ds_ref[gs, :] = ds_old[gs, :] * q["decay_col"][gs, :] + _dot(dye[:, gs].astype(BF16), cg, 0, 0)
            dcb = jnp.zeros((BLK, BLK), F32)
            for pr in range(ppg):
                cols = slice(g * gw + pr * BLK, g * gw + (pr + 1) * BLK)
                xd_p = xd[:, cols].astype(BF16)
                dy_p = dyp[:, cols]
                acc = q["f_x"][:, cols] * bds[:, pr * BLK:(pr + 1) * BLK]
                for hh in range(2):
                    h = (g * gw + pr * BLK) // HEAD_DIM + hh
                    ld = _ssd_ldec(q, h)
                    m = cb * ld
                    dym = (jnp.where(low, dy_p, 0.0) if hh == 0 else jnp.where(low, 0.0, dy_p)).astype(BF16)
                    dm = jnp.where(row >= lane, _dot(dym, xd_p, 1, 1), 0.0)
                    acc = acc + _dot(m.astype(BF16), dym, 0, 0)
                    dcb = dcb + dm * ld
                    qq = dm * m
                    qrow = qrow + jnp.where(lane == h, jnp.sum(qq, axis=1, keepdims=True), 0.0)
                    qcol_t = qcol_t + jnp.where(row == h, jnp.sum(qq, axis=0, keepdims=True), 0.0)
                dxd_ref[:, cols] = acc
            dcbb = dcb.astype(BF16)
            dxbc_ref[:, bsl] = jnp.where(rowlive, db + _dot(dcbb, cg, 0, 0), 0.0)
            dxbc_ref[:, csl] = jnp.where(rowlive, dc + _dot(dcbb, bg), 0.0)
        dxd = dxd_ref[...]
        dxbc_ref[:, 0:SSD_INNER] = jnp.where(rowlive, dxd * q["dt_x"] + dyp * dxp_ref[...], 0.0)
        xp_t = q["xp_t"]
        fw = _dot_sel_r(jnp.concatenate(red_f, axis=1), xp_t)
        dcs = qrow - qcol_t.T + _dot_sel_r(jnp.concatenate(red_e, axis=1), xp_t) - fw
        end_f = jnp.sum(fw, axis=0, keepdims=True)
        sds = jnp.sum(ds_old * s_prev, axis=1, keepdims=True)
        per_head = _dot_sel_l(q["xp"], jnp.broadcast_to(sds, (SSD_INNER, BLK)))
        end_e = per_head.T[0:1, :] * jnp.exp(q["cs"][BLK - 1:BLK, :])
        dcs = dcs + jnp.where(row == BLK - 1, end_f + end_e, 0.0)
        t_up = (lane >= row).astype(BF16)
        da = _dot_sel_l(t_up, dcs)
        ddt = da * q["a_neg"] + _dot_sel_r(dxd * x, xp_t)
        _acc_rows(dal_ref, jnp.sum(da * q["dt"] * q["a_neg"], axis=0, keepdims=True), step)
        ddtr = jnp.where(q["live"], ddt * _sigmoid(q["pre"]), 0.0)
        ddt_ref[...] = ddtr.astype(BF16)
        _acc_rows(dbi_ref, jnp.sum(ddtr, axis=0, keepdims=True), step)

    row_s = lambda w, col: pl.BlockSpec((BLK, w), lambda s: (nb - 1 - s, col))
    vec = lambda w: pl.BlockSpec((1, w), lambda s: (0, 0))
    return pl.pallas_call(
        body, name="ssd_bwd", grid=(nb,),
        in_specs=[row_s(SSD_INNER, 0), row_s(SSD_INNER, 0), row_s(XBC, 0), row_s(BLK, C_DT // BLK),
                  row_s(SSD_INNER, 0), pl.BlockSpec((None, SSD_INNER, SSD_STATE), lambda s: (nb - 1 - s, 0, 0)),
                  vec(BLK), vec(BLK), vec(SSD_INNER), vec(SSD_INNER)],
        out_specs=[row_s(SSD_INNER, 0), row_s(XBC, 0), row_s(BLK, 0),
                   vec(SSD_INNER), vec(SSD_INNER), vec(BLK), vec(BLK)],
        out_shape=[jax.ShapeDtypeStruct((lp, SSD_INNER), BF16), jax.ShapeDtypeStruct((lp, XBC), F32),
                   jax.ShapeDtypeStruct((lp, BLK), BF16),
                   jax.ShapeDtypeStruct((1, SSD_INNER), F32), jax.ShapeDtypeStruct((1, SSD_INNER), F32),
                   jax.ShapeDtypeStruct((1, BLK), F32), jax.ShapeDtypeStruct((1, BLK), F32)],
        scratch_shapes=[pltpu.VMEM((SSD_INNER, SSD_STATE), F32), pltpu.VMEM((BLK, SSD_INNER), F32)],
        compiler_params=_cparams(("arbitrary",)))(dycat, ypre, xbc, proj, proj, sprev, bias, alog, d_x, norm_g)


def _pad_rows8(w):
    return jnp.pad(w, ((0, 8 - w.shape[0]), (0, 0)))


def _pad_lanes(v, n=BLK):
    return jnp.pad(v, ((0, 0), (0, n - v.shape[1])))


def _local_step(x, target, wt):
    seq = x.shape[0]
    lp = seq + BLK
    tm = _pick(lp, [1408, 768, 384, 128])
    tkr = _pick(lp, [384, 128])
    h0 = jnp.concatenate([jnp.zeros((PAD, D_MODEL), F32), wt["meta"], x], axis=0)
    bias = _pad_lanes(wt["ssd_dt_bias"])
    alog = _pad_lanes(wt["ssd_a_log"])
    d_x = jnp.repeat(wt["ssd_d"], HEAD_DIM, axis=1)
    cw8 = _pad_rows8(wt["ssd_conv_w"])
    fw8 = _pad_rows8(wt["ffn_conv_w"])
    fcw = D_FF // 2

    xn1 = _rms_fwd_call(h0, wt["mix_pre_g"], "norm1")
    proj = _mm(xn1, wt["w_in"], tm=tm, tn=1152, tk=D_MODEL, name="mm_proj")[0]
    conv_pre, xbc = _conv_fwd_call(proj, C_XBC, XBC, 512, cw8, wt["ssd_conv_b"], 4, name="ssd_conv_fwd")
    y_ssd, ypre, sprev = _ssd_fwd_call(xbc, proj, bias, alog, d_x, wt["ssd_norm_g"])
    o, tl = _sb_fwd_call(proj)
    y_sb = _rms_fwd_call(o, wt["sb_norm_g"], "sb_norm")
    ycat = jnp.concatenate([y_ssd, y_sb], axis=1)
    mix = _mm(ycat, wt["w_out"], tm=tm, tn=1024, tk=2048, name="mm_mix")[0]
    h1, xn2 = _mid_fwd_call(h0, mix, wt["mix_post_g"], wt["ffn_pre_g"])
    gu = _mm(xn2, wt["w_up"], tm=tm, tn=1408, tk=D_MODEL, name="mm_up")[0]
    gpre, act = _conv_fwd_call(gu, 0, D_FF, fcw, fw8, wt["ffn_conv_b"], 3, gate_src=gu, gate_col0=D_FF,
                               name="ffn_conv_fwd")
    f = _mm(act, wt["w_down"], tm=tm, tn=1024, tk=1408, name="mm_down")[0]
    loss_row, df, dh2, dg_ffn_post = _final_call(h1, f, wt["ffn_post_g"], target)

    dact = _mm(df, wt["w_down"], tb=True, tm=tm, tn=1408, tk=D_MODEL, name="mm_dact")[0]
    dw_down, dw_down_b = _mm(act, df, ta=True, tm=1408, tn=1024, tk=tkr, extra_bf16=True, name="mm_dw_down")
    dgate, dup, dfcw, dfcb = _conv_bwd_call(gu, 0, D_FF, fcw, fw8, 3, gpre, dact, gate_src=gu, gate_col0=D_FF,
                                            name="ffn_conv_bwd")
    dgu = jnp.concatenate([dgate, dup], axis=1)
    dxn2 = _mm(dgu, wt["w_up"], tb=True, tm=tm, tn=1024, tk=1408, name="mm_dxn2")[0]
    dw_up, dw_up_b = _mm(xn2, dgu, ta=True, tm=1024, tn=1408, tk=tkr, nsplit=N_CHIPS, extra_bf16=True,
                         name="mm_dw_up")
    dh1, dmix, dg_ffn_pre, dg_mix_post = _mid_bwd_call(dh2, h1, dxn2, mix, wt["ffn_pre_g"], wt["mix_post_g"])
    dycat = _mm(dmix, wt["w_out"], tb=True, tm=tm, tn=1024, tk=D_MODEL, name="mm_dycat")[0]
    dw_out, dw_out_b = _mm(ycat, dmix, ta=True, tm=1024, tn=1024, tk=tkr, extra_bf16=True, name="mm_dw_out")
    do, dg_sb = _norm_bwd_call(o, wt["sb_norm_g"], dycat, 1, "sb_norm_bwd")
    dq, dk, dv = _sb_bwd_call(proj, tl, do)
    dz, dxbc_act, ddt, dg_ssd, dd_x, dalog, dbias = _ssd_bwd_call(
        dycat, ypre, xbc, proj, sprev, bias, alog, d_x, wt["ssd_norm_g"])
    dxbc, dcw, dcb = _conv_bwd_call(proj, C_XBC, XBC, 512, cw8, 4, conv_pre, dxbc_act, name="ssd_conv_bwd")
    dproj = jnp.concatenate([dz, dxbc, ddt, dq, dk, dv], axis=1)
    dxn1 = _mm(dproj, wt["w_in"], tb=True, tm=tm, tn=1024, tk=1152, name="mm_dxn1")[0]
    dw_in = _mm(xn1, dproj, ta=True, tm=1024, tn=1152, tk=tkr, name="mm_dw_in")[0]
    dh0, dg_pre = _norm_bwd_call(h0, wt["mix_pre_g"], dxn1, 0, "norm1_bwd", res=dh1)

    small = {
        "meta_tokens": dh0[PAD:BLK], "mix_pre_g": dg_pre, "ssd_conv_w": dcw[:4], "ssd_conv_b": dcb,
        "ssd_dt_bias": dbias[:, :N_HEADS], "ssd_a_log": dalog[:, :N_HEADS],
        "ssd_d": jnp.sum(dd_x.reshape(N_HEADS, HEAD_DIM), axis=1)[None],
        "ssd_norm_g": dg_ssd, "sb_norm_g": dg_sb, "mix_post_g": dg_mix_post, "ffn_pre_g": dg_ffn_pre,
        "ffn_conv_w": dfcw[:3], "ffn_conv_b": dfcb, "ffn_post_g": dg_ffn_post,
    }
    big = {"w_in": dw_in, "w_out": (dw_out, dw_out_b), "w_up": (dw_up, dw_up_b), "w_down": (dw_down, dw_down_b)}
    return loss_row, dh0[BLK:], small, big


def _adamw_call(w, g, m, v, name):
    rows, cols = w.shape
    tr = 256 if rows % 256 == 0 else (352 if rows % 352 == 0 else rows)
    c1 = 1.0 - ADAM_B1 ** ADAM_STEP
    c2 = 1.0 - ADAM_B2 ** ADAM_STEP

    def body(w_ref, g_ref, m_ref, v_ref, d_ref, mo_ref, vo_ref):
        gv = g_ref[...]
        m2 = ADAM_B1 * m_ref[...] + (1.0 - ADAM_B1) * gv
        v2 = ADAM_B2 * v_ref[...] + (1.0 - ADAM_B2) * (gv * gv)
        d_ref[...] = -ADAM_LR * ((m2 / c1) / (jnp.sqrt(v2 / c2) + ADAM_EPS) + ADAM_WD * w_ref[...])
        mo_ref[...] = m2
        vo_ref[...] = v2

    spec = pl.BlockSpec((tr, cols), lambda i: (i, 0))
    return pl.pallas_call(
        body, name=name, grid=(rows // tr,), in_specs=[spec] * 4, out_specs=[spec] * 3,
        out_shape=[jax.ShapeDtypeStruct((rows, cols), F32)] * 3,
        compiler_params=_cparams(("parallel",)))(w, g, m, v)


ANY = pl.BlockSpec(memory_space=pl.ANY)


def _place():
    x, y, c = lax.axis_index("x"), lax.axis_index("y"), lax.axis_index("c")
    chips = [(1 - x, y), (x, 1 - y), (1 - x, 1 - y)]
    return x, y, c, chips


def _half(c, h):
    return pl.ds(pl.multiple_of(c * h, 8), h)


def _allgather_call(shards):
    n = len(shards)

    def body(*refs):
        ins, outs = refs[:n], refs[n:2 * n]
        send_i, recv_i, send_d, recv_d = refs[2 * n:]
        x, y, c, chips = _place()
        me = 2 * x + y
        sends = []
        for a in range(n):
            h = shards[a].shape[0] // 2
            for j, chip in enumerate(chips):
                cp = pltpu.make_async_remote_copy(
                    src_ref=ins[a].at[_half(c, h)], dst_ref=outs[a].at[me, _half(c, h)],
                    send_sem=send_i.at[3 * a + j], recv_sem=recv_i.at[3 * a + j],
                    device_id=(*chip, c), device_id_type=MESH)
                cp.start()
                sends.append(cp)
        for a in range(n):
            h = shards[a].shape[0] // 2
            for j, chip in enumerate(chips):
                src = 2 * chip[0] + chip[1]
                landed = outs[a].at[src, _half(c, h)]
                pltpu.make_async_remote_copy(
                    src_ref=landed, dst_ref=landed, send_sem=send_i.at[3 * a + j], recv_sem=recv_i.at[3 * a + j],
                    device_id=(*chip, c), device_id_type=MESH).wait_recv()
                cp = pltpu.make_async_remote_copy(
                    src_ref=landed, dst_ref=landed, send_sem=send_d.at[3 * a + j], recv_sem=recv_d.at[3 * a + j],
                    device_id=(x, y, 1 - c), device_id_type=MESH)
                cp.start()
                sends.append(cp)
        for a in range(n):
            h = shards[a].shape[0] // 2
            for j, chip in enumerate(chips):
                src = 2 * chip[0] + chip[1]
                other = outs[a].at[src, _half(1 - c, h)]
                pltpu.make_async_remote_copy(
                    src_ref=other, dst_ref=other, send_sem=send_d.at[3 * a + j], recv_sem=recv_d.at[3 * a + j],
                    device_id=(x, y, 1 - c), device_id_type=MESH).wait_recv()
        for cp in sends:
            cp.wait_send()

    return pl.pallas_call(
        body, name="allgather_weights", in_specs=[ANY] * n, out_specs=[ANY] * n,
        out_shape=[jax.ShapeDtypeStruct((N_CHIPS,) + s.shape, s.dtype) for s in shards],
        scratch_shapes=[pltpu.SemaphoreType.DMA((3 * n,))] * 4,
    )(*shards)


def _pair_exchange_call(grads):
    n = len(grads)

    def body(*refs):
        ins, outs = refs[:n], refs[n:2 * n]
        send_d, recv_d = refs[2 * n:]
        x, y, c, _ = _place()
        cps = []
        for a in range(n):
            h = grads[a].shape[1] // 2
            cp = pltpu.make_async_remote_copy(
                src_ref=ins[a].at[:, _half(1 - c, h)], dst_ref=outs[a], send_sem=send_d.at[a], recv_sem=recv_d.at[a],
                device_id=(x, y, 1 - c), device_id_type=MESH)
            cp.start()
            cps.append(cp)
        for cp in cps:
            cp.wait()

    return pl.pallas_call(
        body, name="grad_pair_exchange", in_specs=[ANY] * n, out_specs=[ANY] * n,
        out_shape=[jax.ShapeDtypeStruct((N_CHIPS, g.shape[1] // 2, g.shape[2]), BF16) for g in grads],
        scratch_shapes=[pltpu.SemaphoreType.DMA((n,))] * 2,
    )(*grads)


def _pair_sum_call(own, got, c_idx, name):
    _, rows, cols = own.shape
    h = rows // 2
    th = _pick(h, [256, 176, 8])
    nt = h // th

    def body(c_ref, own_ref, got_ref, o_ref):
        o_ref[...] = (own_ref[...] + got_ref[...].astype(F32)).astype(BF16)

    return pl.pallas_call(
        body, name=name,
        grid_spec=pltpu.PrefetchScalarGridSpec(
            num_scalar_prefetch=1, grid=(N_CHIPS, nt),
            in_specs=[pl.BlockSpec((None, th, cols), lambda s, i, c_ref: (s, c_ref[0] * nt + i, 0)),
                      pl.BlockSpec((None, th, cols), lambda s, i, c_ref: (s, i, 0))],
            out_specs=pl.BlockSpec((None, th, cols), lambda s, i, c_ref: (s, i, 0))),
        out_shape=jax.ShapeDtypeStruct((N_CHIPS, h, cols), BF16),
        compiler_params=_cparams(("parallel", "parallel")))(c_idx, own, got)


def _chip_exchange_call(pairs):
    n = len(pairs)

    def body(*refs):
        ins, outs = refs[:n], refs[n:2 * n]
        send_i, recv_i = refs[2 * n:]
        x, y, c, chips = _place()
        me = 2 * x + y
        sends = []
        for a in range(n):
            for j, chip in enumerate(chips):
                dst = 2 * chip[0] + chip[1]
                cp = pltpu.make_async_remote_copy(
                    src_ref=ins[a].at[dst], dst_ref=outs[a].at[me], send_sem=send_i.at[3 * a + j],
                    recv_sem=recv_i.at[3 * a + j], device_id=(*chip, c), device_id_type=MESH)
                cp.start()
                sends.append(cp)
        for a in range(n):
            for j, chip in enumerate(chips):
                src = 2 * chip[0] + chip[1]
                pltpu.make_async_remote_copy(
                    src_ref=ins[a].at[src], dst_ref=outs[a].at[src], send_sem=send_i.at[3 * a + j],
                    recv_sem=recv_i.at[3 * a + j], device_id=(*chip, c), device_id_type=MESH).wait_recv()
        for cp in sends:
            cp.wait_send()

    return pl.pallas_call(
        body, name="grad_chip_exchange", in_specs=[ANY] * n, out_specs=[ANY] * n,
        out_shape=[jax.ShapeDtypeStruct(p.shape, BF16) for p in pairs],
        scratch_shapes=[pltpu.SemaphoreType.DMA((3 * n,))] * 2,
    )(*pairs)


def _chip_sum_call(pair, by_chip, place, name):
    _, h, cols = pair.shape
    th = _pick(h, [256, 176, 8])
    nt = h // th

    def body(p_ref, own_ref, a_ref, b_ref, c_ref, o_ref):
        s = own_ref[...].astype(F32) + a_ref[...].astype(F32)
        o_ref[...] = (s + b_ref[...].astype(F32)) + c_ref[...].astype(F32)

    def src(k):
        return pl.BlockSpec((None, th, cols), lambda i, p_ref: (p_ref[k], i, 0))

    return pl.pallas_call(
        body, name=name,
        grid_spec=pltpu.PrefetchScalarGridSpec(
            num_scalar_prefetch=1, grid=(nt,), in_specs=[src(1), src(2), src(3), src(4)],
            out_specs=pl.BlockSpec((th, cols), lambda i, p_ref: (p_ref[0] * nt + i, 0))),
        out_shape=jax.ShapeDtypeStruct((2 * h, cols), F32),
        compiler_params=_cparams(("parallel",)))(place, pair, by_chip, by_chip, by_chip)


def _half_exchange_call(shards):
    n = len(shards)

    def body(*refs):
        outs = refs[n:2 * n]
        send_d, recv_d = refs[2 * n:]
        x, y, c, _ = _place()
        cps = []
        for a in range(n):
            h = shards[a].shape[0] // 2
            mine = outs[a].at[_half(c, h)]
            cp = pltpu.make_async_remote_copy(
                src_ref=mine, dst_ref=mine, send_sem=send_d.at[a], recv_sem=recv_d.at[a],
                device_id=(x, y, 1 - c), device_id_type=MESH)
            cp.start()
            cps.append(cp)
        for a, cp in enumerate(cps):
            h = shards[a].shape[0] // 2
            theirs = outs[a].at[_half(1 - c, h)]
            pltpu.make_async_remote_copy(
                src_ref=theirs, dst_ref=theirs, send_sem=send_d.at[a], recv_sem=recv_d.at[a],
                device_id=(x, y, 1 - c), device_id_type=MESH).wait_recv()
            cp.wait_send()

    return pl.pallas_call(
        body, name="grad_half_exchange", in_specs=[ANY] * n, out_specs=[ANY] * n,
        out_shape=[jax.ShapeDtypeStruct(sv.shape, F32) for sv in shards],
        input_output_aliases={a: a for a in range(n)},
        scratch_shapes=[pltpu.SemaphoreType.DMA((n,))] * 2,
    )(*shards)


def _allreduce_small_call(v):
    rows = v.shape[0]

    def body(v_ref, o_ref, gath, send_sems, recv_sems):
        x, y, c, chips = _place()
        me, sibling = (x, y, c), (x, y, 1 - c)

        def slot(px, py, pc):
            return gath.at[4 * px + 2 * py + pc]

        def copy(k, block, to, src=None):
            return pltpu.make_async_remote_copy(
                src_ref=slot(*block) if src is None else src, dst_ref=slot(*block),
                send_sem=send_sems.at[k], recv_sem=recv_sems.at[k], device_id=to, device_id_type=MESH)

        gath[4 * x + 2 * y + c] = v_ref[...]
        first = [copy(0, me, sibling, src=v_ref)]
        first += [copy(1 + j, me, (*chip, c), src=v_ref) for j, chip in enumerate(chips)]
        for cp in first:
            cp.start()
        passed = [copy(4 + j, (*chip, c), sibling) for j, chip in enumerate(chips)]
        for j, chip in enumerate(chips):
            copy(1 + j, (*chip, c), me).wait_recv()
            passed[j].start()
        copy(0, sibling, me).wait_recv()
        for j, chip in enumerate(chips):
            copy(4 + j, (*chip, 1 - c), me).wait_recv()
        for cp in first + passed:
            cp.wait_send()
        acc = gath[0]
        for d in range(1, 8):
            acc = acc + gath[d]
        o_ref[...] = acc

    vm = pl.BlockSpec(memory_space=pltpu.VMEM)
    return pl.pallas_call(
        body, name="allreduce_small", in_specs=[vm], out_specs=vm,
        out_shape=jax.ShapeDtypeStruct((rows, BLK), F32),
        scratch_shapes=[pltpu.VMEM((8, rows, BLK), F32), pltpu.SemaphoreType.DMA((7,)),
                        pltpu.SemaphoreType.DMA((7,))],
        compiler_params=pltpu.CompilerParams(vmem_limit_bytes=VMEM_LIMIT),
    )(v)


def _pack(arrs, min_rows=8):
    parts = []
    for a in arrs:
        flat = a.reshape(-1).astype(F32)
        parts.append(jnp.pad(flat, (0, (-flat.shape[0]) % BLK)))
    buf = jnp.concatenate(parts).reshape(-1, BLK)
    return jnp.pad(buf, ((0, (-buf.shape[0]) % min_rows), (0, 0)))


def _unpack(buf, shapes):
    out, r = [], 0
    for shp in shapes:
        n = math.prod(shp)
        nr = -(-n // BLK)
        out.append(buf[r:r + nr].reshape(-1)[:n].reshape(shp))
        r += nr
    return out


SMALL = ["meta_tokens", "mix_pre_g", "ssd_conv_w", "ssd_conv_b", "ssd_dt_bias", "ssd_a_log", "ssd_d", "ssd_norm_g",
         "sb_norm_g", "mix_post_g", "ffn_pre_g", "ffn_conv_w", "ffn_conv_b", "ffn_post_g"]
BIG = ["w_in", "w_out", "w_up", "w_down"]
WEIGHTS = ["meta_tokens", "mix_pre_g", "w_in", "ssd_conv_w", "ssd_conv_b", "ssd_dt_bias", "ssd_a_log", "ssd_d",
           "ssd_norm_g", "sb_norm_g", "w_out", "mix_post_g", "ffn_pre_g", "w_up", "ffn_conv_w", "ffn_conv_b",
           "w_down", "ffn_post_g"]
W_IN_SHARD = IN_COLS // N_CHIPS
W_IN_PAD = 1536


def kernel(x, meta_tokens, mix_pre_g, w_in, ssd_conv_w, ssd_conv_b, ssd_dt_bias, ssd_a_log, ssd_d, ssd_norm_g, sb_norm_g, w_out, mix_post_g, ffn_pre_g, w_up, ffn_conv_w, ffn_conv_b, w_down, ffn_post_g, loss_target, m_meta_tokens, m_mix_pre_g, m_w_in, m_ssd_conv_w, m_ssd_conv_b, m_ssd_dt_bias, m_ssd_a_log, m_ssd_d, m_ssd_norm_g, m_sb_norm_g, m_w_out, m_mix_post_g, m_ffn_pre_g, m_w_up, m_ffn_conv_w, m_ffn_conv_b, m_w_down, m_ffn_post_g, v_meta_tokens, v_mix_pre_g, v_w_in, v_ssd_conv_w, v_ssd_conv_b, v_ssd_dt_bias, v_ssd_a_log, v_ssd_d, v_ssd_norm_g, v_sb_norm_g, v_w_out, v_mix_post_g, v_ffn_pre_g, v_w_up, v_ffn_conv_w, v_ffn_conv_b, v_w_down, v_ffn_post_g):
    w = dict(meta_tokens=meta_tokens, mix_pre_g=mix_pre_g, w_in=w_in, ssd_conv_w=ssd_conv_w, ssd_conv_b=ssd_conv_b, ssd_dt_bias=ssd_dt_bias, ssd_a_log=ssd_a_log, ssd_d=ssd_d, ssd_norm_g=ssd_norm_g, sb_norm_g=sb_norm_g, w_out=w_out, mix_post_g=mix_post_g, ffn_pre_g=ffn_pre_g, w_up=w_up, ffn_conv_w=ffn_conv_w, ffn_conv_b=ffn_conv_b, w_down=w_down, ffn_post_g=ffn_post_g)
    m = dict(meta_tokens=m_meta_tokens, mix_pre_g=m_mix_pre_g, w_in=m_w_in, ssd_conv_w=m_ssd_conv_w, ssd_conv_b=m_ssd_conv_b, ssd_dt_bias=m_ssd_dt_bias, ssd_a_log=m_ssd_a_log, ssd_d=m_ssd_d, ssd_norm_g=m_ssd_norm_g, sb_norm_g=m_sb_norm_g, w_out=m_w_out, mix_post_g=m_mix_post_g, ffn_pre_g=m_ffn_pre_g, w_up=m_w_up, ffn_conv_w=m_ffn_conv_w, ffn_conv_b=m_ffn_conv_b, w_down=m_w_down, ffn_post_g=m_ffn_post_g)
    v = dict(meta_tokens=v_meta_tokens, mix_pre_g=v_mix_pre_g, w_in=v_w_in, ssd_conv_w=v_ssd_conv_w, ssd_conv_b=v_ssd_conv_b, ssd_dt_bias=v_ssd_dt_bias, ssd_a_log=v_ssd_a_log, ssd_d=v_ssd_d, ssd_norm_g=v_ssd_norm_g, sb_norm_g=v_sb_norm_g, w_out=v_w_out, mix_post_g=v_mix_post_g, ffn_pre_g=v_ffn_pre_g, w_up=v_w_up, ffn_conv_w=v_ffn_conv_w, ffn_conv_b=v_ffn_conv_b, w_down=v_w_down, ffn_post_g=v_ffn_post_g)
    chip = 2 * lax.axis_index("x") + lax.axis_index("y")
    place = jnp.stack([lax.axis_index("c"), chip, chip ^ 2, chip ^ 1, chip ^ 3]).astype(jnp.int32)

    shard_small = [w["meta_tokens"], w["ssd_conv_w"][0], w["ffn_conv_w"][0]]
    shards = [
        jnp.pad(w["w_in"][0], ((0, 0), (0, W_IN_PAD - W_IN_SHARD))).astype(BF16),
        w["w_out"][0].astype(BF16), w["w_up"][0].astype(BF16), w["w_down"][0].astype(BF16),
        _pack(shard_small, 16),
    ]
    gathered = _allgather_call(shards)

    def blocks(a):
        return [jnp.where(chip == i, shards[a], gathered[a][i]) for i in range(N_CHIPS)]

    cut = DT_REAL_OFF + N_HEADS - W_IN_SHARD
    s_in = blocks(0)
    w_in_c = jnp.concatenate(
        [s_in[0][:, :W_IN_SHARD], s_in[1][:, :cut], jnp.zeros((D_MODEL, BLK - N_HEADS), BF16),
         s_in[1][:, cut:W_IN_SHARD], s_in[2][:, :W_IN_SHARD], s_in[3][:, :W_IN_SHARD]], axis=1)
    parts = [_unpack(b, [s.shape for s in shard_small]) for b in blocks(4)]
    wt = {k: w[k][0][None] if w[k].ndim == 3 else w[k] for k in
          ["mix_pre_g", "ssd_conv_b", "ssd_dt_bias", "ssd_a_log", "ssd_d", "ssd_norm_g", "sb_norm_g", "mix_post_g",
           "ffn_pre_g", "ffn_conv_b", "ffn_post_g"]}
    wt.update(
        meta=jnp.concatenate([p[0] for p in parts], axis=1),
        ssd_conv_w=jnp.concatenate([p[1] for p in parts], axis=1),
        ffn_conv_w=jnp.concatenate([p[2] for p in parts], axis=1),
        w_in=w_in_c, w_out=jnp.concatenate(blocks(1), axis=0), w_up=jnp.concatenate(blocks(2), axis=1),
        w_down=jnp.concatenate(blocks(3), axis=0))

    loss_row, dx, small, big = _local_step(x[0], loss_target[0], wt)

    dw_in = big["w_in"]
    skip = BLK - N_HEADS
    by_chip_cols = [dw_in[:, :W_IN_SHARD],
                    jnp.concatenate([dw_in[:, W_IN_SHARD:W_IN_SHARD + cut], dw_in[:, C_Q:2 * W_IN_SHARD + skip]], axis=1),
                    dw_in[:, 2 * W_IN_SHARD + skip:3 * W_IN_SHARD + skip], dw_in[:, 3 * W_IN_SHARD + skip:]]
    dw_in_s = jnp.stack([jnp.pad(b, ((0, 0), (0, W_IN_PAD - W_IN_SHARD))) for b in by_chip_cols])
    own = [dw_in_s, big["w_out"][0].reshape(N_CHIPS, -1, D_MODEL), big["w_up"][0],
           big["w_down"][0].reshape(N_CHIPS, -1, D_MODEL)]
    own_b = [dw_in_s.astype(BF16), big["w_out"][1].reshape(N_CHIPS, -1, D_MODEL), big["w_up"][1],
             big["w_down"][1].reshape(N_CHIPS, -1, D_MODEL)]
    got = _pair_exchange_call(own_b)
    pairs = [_pair_sum_call(own[a], got[a], place, "pair_sum_" + BIG[a]) for a in range(4)]
    by_chip = _chip_exchange_call(pairs)
    full = _half_exchange_call(
        [_chip_sum_call(pairs[a], by_chip[a], place, "chip_sum_" + BIG[a]) for a in range(4)])
    grads = {"w_in": full[0][:, :W_IN_SHARD], "w_out": full[1], "w_up": full[2], "w_down": full[3]}

    small_list = [small[k] for k in SMALL] + [jnp.sum(loss_row).reshape(1, 1)]
    red = _allreduce_small_call(_pack(small_list, 8))
    red_list = _unpack(red, [a.shape for a in small_list])
    loss = red_list[-1].reshape(())
    for k, g in zip(SMALL, red_list[:-1]):
        grads[k] = g
    for k in ["meta_tokens", "ssd_conv_w", "ffn_conv_w"]:
        wk = w[k].shape[-1]
        grads[k] = lax.dynamic_slice_in_dim(grads[k], chip * wk, wk, axis=1)

    delta, new_m, new_v = {}, {}, {}
    for k in BIG:
        delta[k], new_m[k], new_v[k] = _adamw_call(w[k][0], grads[k], m[k][0], v[k][0], "adamw_" + k)
    sm_shapes = [grads[k].shape for k in SMALL]
    res = _adamw_call(_pack([w[k] for k in SMALL]), _pack([grads[k] for k in SMALL]),
                      _pack([m[k] for k in SMALL]), _pack([v[k] for k in SMALL]), "adamw_small")
    for out, buf in zip((delta, new_m, new_v), res):
        for k, a in zip(SMALL, _unpack(buf, sm_shapes)):
            out[k] = a

    def shaped(d, k):
        return d[k].reshape(w[k].shape)

    return (loss, dx[None], *[shaped(grads, k) for k in WEIGHTS], *[shaped(delta, k) for k in WEIGHTS],
            *[shaped(new_m, k) for k in WEIGHTS], *[shaped(new_v, k) for k in WEIGHTS])
```

```python
import functools
import math

import jax
import jax.numpy as jnp
from jax import lax
from jax.experimental import pallas as pl
from jax.experimental.pallas import tpu as pltpu

F32 = jnp.float32
BF16 = jnp.bfloat16

D_MODEL = 1024
N_META = 16
BLK = 128
PAD = BLK - N_META
HEAD_DIM = 64
N_HEADS = 16
SSD_GROUPS = 2
SSD_STATE = 128
SSD_INNER = 1024
XBC = SSD_INNER + 2 * SSD_GROUPS * SSD_STATE
D_FF = 2816
EPS = 1e-6
IN_COLS = 5648
C_Z, C_XBC, C_DT, C_Q, C_K, C_V, C_END = 0, 1024, 2560, 2688, 3712, 4736, 5760
DT_REAL_OFF = 2560
N_CHIPS = 4
ADAM_LR, ADAM_B1, ADAM_B2, ADAM_EPS, ADAM_WD, ADAM_STEP = 0.001, 0.9, 0.999, 1e-08, 0.01, 10
VMEM_LIMIT = 56 * 1024 * 1024
MESH = pl.DeviceIdType.MESH


def _cparams(sem=None, **kw):
    if sem is not None:
        kw["dimension_semantics"] = sem
    return pltpu.CompilerParams(vmem_limit_bytes=VMEM_LIMIT, **kw)


def _pick(n, cands):
    for c in cands:
        if n % c == 0:
            return c
    raise ValueError((n, cands))


def _iota(shape, dim):
    return lax.broadcasted_iota(jnp.int32, shape, dim)


def _sigmoid(x):
    return 1.0 / (1.0 + jnp.exp(-x))


def _split3(v):
    h1 = v.astype(BF16)
    r1 = v - h1.astype(F32)
    h2 = r1.astype(BF16)
    h3 = (r1 - h2.astype(F32)).astype(BF16)
    return h1, h2, h3


def _dot(a, b, ca=1, cb=0):
    return lax.dot_general(a, b, (((ca,), (cb,)), ((), ())), preferred_element_type=F32)


def _dot_sel_r(v, sel, cb=0):
    h1, h2, h3 = _split3(v)
    return _dot(h1, sel, 1, cb) + _dot(h2, sel, 1, cb) + _dot(h3, sel, 1, cb)


def _dot_sel_l(sel, v, ca=1):
    h1, h2, h3 = _split3(v)
    return _dot(sel, h1, ca, 0) + _dot(sel, h2, ca, 0) + _dot(sel, h3, ca, 0)


def _mm(a, b, *, ta=False, tb=False, tm, tn, tk, out_dtype=F32, nsplit=1, extra_bf16=False, name):
    K, M = (a.shape if ta else a.shape[::-1])
    N = b.shape[0] if tb else b.shape[1]
    assert M % tm == 0 and N % tn == 0 and K % tk == 0, (name, M, N, K, tm, tn, tk)
    nm, nn, nk = M // tm, N // tn, K // tk
    assert nn % nsplit == 0
    per = nn // nsplit
    a_spec = (pl.BlockSpec((tk, tm), lambda i, j, k: (k, i)) if ta
              else pl.BlockSpec((tm, tk), lambda i, j, k: (i, k)))
    b_spec = (pl.BlockSpec((tn, tk), lambda i, j, k: (j, k)) if tb
              else pl.BlockSpec((tk, tn), lambda i, j, k: (k, j)))
    o_spec = pl.BlockSpec((None, tm, tn), lambda i, j, k: (j // per, i, j % per))
    n_out = 2 if extra_bf16 else 1
    ca, cb = (0 if ta else 1), (1 if tb else 0)

    def body(a_ref, b_ref, *rest):
        outs = rest[:n_out]
        p = _dot(a_ref[...].astype(BF16), b_ref[...].astype(BF16), ca, cb)

        def emit(val):
            outs[0][...] = val.astype(out_dtype)
            if extra_bf16:
                outs[1][...] = val.astype(BF16)

        if nk == 1:
            emit(p)
        else:
            acc = rest[n_out]
            k = pl.program_id(2)

            @pl.when(k == 0)
            def _():
                acc[...] = p

            @pl.when(k > 0)
            def _():
                acc[...] += p

            @pl.when(k == nk - 1)
            def _():
                emit(acc[...])

    shp = (nsplit, M, N // nsplit)
    out_shape = [jax.ShapeDtypeStruct(shp, out_dtype)]
    out_specs = [o_spec]
    if extra_bf16:
        out_shape.append(jax.ShapeDtypeStruct(shp, BF16))
        out_specs.append(o_spec)
    res = pl.pallas_call(
        body, name=name, grid=(nm, nn, nk), in_specs=[a_spec, b_spec], out_specs=out_specs,
        out_shape=out_shape, scratch_shapes=([pltpu.VMEM((tm, tn), F32)] if nk > 1 else []),
        compiler_params=_cparams(("parallel", "parallel", "arbitrary")),
    )(a, b)
    return res if extra_bf16 else res[0]


def _rms_stats(x):
    r = lax.rsqrt(jnp.mean(x * x, axis=-1, keepdims=True) + EPS)
    return r, x * r


def _rms_bwd(x, g, dy):
    r, xh = _rms_stats(x)
    dxh = dy * g
    dx = r * (dxh - xh * jnp.mean(dxh * xh, axis=-1, keepdims=True))
    return dx, jnp.sum(dy * xh, axis=0, keepdims=True)


def _row_spec(tr, w, col=0):
    return pl.BlockSpec((tr, w), lambda i: (i, col))


def _vec_spec(w):
    return pl.BlockSpec((1, w), lambda i: (0, 0))


def _acc_rows(ref, val, i):
    @pl.when(i == 0)
    def _():
        ref[...] = val

    @pl.when(i > 0)
    def _():
        ref[...] += val


def _rms_fwd_call(x, g, name):
    lp, w = x.shape
    tr = _pick(lp, [384, 128])

    def body(x_ref, g_ref, o_ref):
        _, xh = _rms_stats(x_ref[...])
        o_ref[...] = (xh * g_ref[...]).astype(BF16)

    return pl.pallas_call(
        body, name=name, grid=(lp // tr,), in_specs=[_row_spec(tr, w), _vec_spec(w)],
        out_specs=_row_spec(tr, w), out_shape=jax.ShapeDtypeStruct((lp, w), BF16),
        compiler_params=_cparams(("parallel",)))(x, g)


def _mid_fwd_call(h0, mix, g_post, g_pre2):
    lp, w = h0.shape
    tr = _pick(lp, [384, 128])

    def body(h0_ref, mix_ref, gp_ref, g2_ref, h1_ref, xn_ref):
        _, mh = _rms_stats(mix_ref[...])
        h1 = h0_ref[...] + mh * gp_ref[...]
        h1_ref[...] = h1
        _, hh = _rms_stats(h1)
        xn_ref[...] = (hh * g2_ref[...]).astype(BF16)

    return pl.pallas_call(
        body, name="mid_fwd", grid=(lp // tr,),
        in_specs=[_row_spec(tr, w), _row_spec(tr, w), _vec_spec(w), _vec_spec(w)],
        out_specs=[_row_spec(tr, w), _row_spec(tr, w)],
        out_shape=[jax.ShapeDtypeStruct((lp, w), F32), jax.ShapeDtypeStruct((lp, w), BF16)],
        compiler_params=_cparams(("parallel",)))(h0, mix, g_post, g_pre2)


def _final_call(h1, f, g_post, target):
    lp, w = h1.shape
    tr = BLK
    nb = lp // tr

    def body(h1_ref, f_ref, g_ref, t_ref, loss_ref, df_ref, dh_ref, dg_ref):
        i = pl.program_id(0)
        fv = f_ref[...]
        g = g_ref[...]
        _, fh = _rms_stats(fv)
        h2 = h1_ref[...] + fh * g
        diff = jnp.where(i > 0, h2 - t_ref[...], 0.0)
        part = 0.5 * jnp.sum(diff * diff, axis=0, keepdims=True) * (1.0 / w)
        _acc_rows(loss_ref, part, i)
        dh = diff * (1.0 / w)
        dh_ref[...] = dh
        df, dg = _rms_bwd(fv, g, dh)
        df_ref[...] = df.astype(BF16)
        _acc_rows(dg_ref, dg, i)

    t_spec = pl.BlockSpec((tr, w), lambda i: (jnp.maximum(i - 1, 0), 0))
    return pl.pallas_call(
        body, name="final_fwd_bwd", grid=(nb,),
        in_specs=[_row_spec(tr, w), _row_spec(tr, w), _vec_spec(w), t_spec],
        out_specs=[_vec_spec(w), _row_spec(tr, w), _row_spec(tr, w), _vec_spec(w)],
        out_shape=[jax.ShapeDtypeStruct((1, w), F32), jax.ShapeDtypeStruct((lp, w), BF16),
                   jax.ShapeDtypeStruct((lp, w), F32), jax.ShapeDtypeStruct((1, w), F32)],
        compiler_params=_cparams(("arbitrary",)))(h1, f, g_post, target)


def _mid_bwd_call(dh2, h1, dxn2, mix, g_pre2, g_post):
    lp, w = h1.shape
    tr = _pick(lp, [384, 128])

    def body(dh2_ref, h1_ref, dxn_ref, mix_ref, g2_ref, gp_ref, dh1_ref, dmix_ref, dg2_ref, dgp_ref):
        i = pl.program_id(0)
        live = (i * tr + _iota((tr, 1), 0)) >= PAD
        dx, dg2 = _rms_bwd(h1_ref[...], g2_ref[...], dxn_ref[...])
        dh1 = jnp.where(live, dh2_ref[...] + dx, 0.0)
        dh1_ref[...] = dh1
        dmix, dgp = _rms_bwd(mix_ref[...], gp_ref[...], dh1)
        dmix_ref[...] = jnp.where(live, dmix, 0.0).astype(BF16)
        _acc_rows(dg2_ref, dg2, i)
        _acc_rows(dgp_ref, dgp, i)

    rs = _row_spec(tr, w)
    return pl.pallas_call(
        body, name="mid_bwd", grid=(lp // tr,),
        in_specs=[rs, rs, rs, rs, _vec_spec(w), _vec_spec(w)],
        out_specs=[rs, rs, _vec_spec(w), _vec_spec(w)],
        out_shape=[jax.ShapeDtypeStruct((lp, w), F32), jax.ShapeDtypeStruct((lp, w), BF16),
                   jax.ShapeDtypeStruct((1, w), F32), jax.ShapeDtypeStruct((1, w), F32)],
        compiler_params=_cparams(("arbitrary",)))(dh2, h1, dxn2, mix, g_pre2, g_post)


def _norm_bwd_call(x, g, dy_arr, dy_col, name, res=None):
    lp, w = x.shape
    tr = _pick(lp, [384, 128])
    has_res = res is not None

    def body(x_ref, g_ref, dy_ref, *rest):
        i = pl.program_id(0)
        live = (i * tr + _iota((tr, 1), 0)) >= PAD
        dx, dg = _rms_bwd(x_ref[...], g_ref[...], dy_ref[...])
        if has_res:
            dx = dx + rest[0][...]
        out_ref, dg_ref = rest[-2], rest[-1]
        out_ref[...] = jnp.where(live, dx, 0.0)
        _acc_rows(dg_ref, dg, i)

    rs = _row_spec(tr, w)
    ins = [rs, _vec_spec(w), _row_spec(tr, w, dy_col)] + ([rs] if has_res else [])
    args = [x, g, dy_arr] + ([res] if has_res else [])
    return pl.pallas_call(
        body, name=name, grid=(lp // tr,), in_specs=ins, out_specs=[rs, _vec_spec(w)],
        out_shape=[jax.ShapeDtypeStruct((lp, w), F32), jax.ShapeDtypeStruct((1, w), F32)],
        compiler_params=_cparams(("arbitrary",)))(*args)


def _shift_down(cur, prev_tail, s, rows):
    if s == 0:
        return cur
    prev = jnp.tile(prev_tail, (BLK // 8, 1))
    return jnp.where(rows >= s, pltpu.roll(cur, s, 0), pltpu.roll(prev, s, 0))


def _shift_up(cur, next_head, s, rows):
    if s == 0:
        return cur
    nxt = jnp.tile(next_head, (BLK // 8, 1))
    return jnp.where(rows < BLK - s, pltpu.roll(cur, BLK - s, 0), pltpu.roll(nxt, BLK - s, 0))


def _gelu_tanh(x):
    c = math.sqrt(2.0 / math.pi)
    t = jnp.tanh(c * (x + 0.044715 * x * x * x))
    return 0.5 * x * (1.0 + t), t


def _conv_fwd_call(src, col0, width, cw, w8, b, taps, *, gate_src=None, gate_col0=0, name):
    lp = src.shape[0]
    nb, nc = lp // BLK, width // cw
    cb0 = col0 // cw
    ffn = gate_src is not None

    def body(x_ref, w_ref, b_ref, *rest):
        if ffn:
            u_ref, y_ref, a_ref, tail = rest
        else:
            y_ref, a_ref, tail = rest
        i = pl.program_id(1)

        @pl.when(i == 0)
        def _():
            tail[...] = jnp.zeros_like(tail)

        cur = x_ref[...]
        rows = _iota((BLK, cw), 0)
        y = b_ref[...] + w_ref[taps - 1:taps, :] * cur
        pt = tail[...]
        for s in range(1, taps):
            y = y + w_ref[taps - 1 - s:taps - s, :] * _shift_down(cur, pt, s, rows)
        tail[...] = cur[BLK - 8:, :]
        y_ref[...] = y
        if ffn:
            ge, _ = _gelu_tanh(y)
            a_ref[...] = (ge * u_ref[...]).astype(BF16)
        else:
            live = (i * BLK + rows) >= PAD
            a_ref[...] = jnp.where(live, y * _sigmoid(y), 0.0)

    blk = lambda c0: pl.BlockSpec((BLK, cw), lambda j, i: (i, c0 + j))
    ins = [blk(cb0), pl.BlockSpec((8, cw), lambda j, i: (0, j)), pl.BlockSpec((1, cw), lambda j, i: (0, j))]
    args = [src, w8, b]
    if ffn:
        ins.append(blk(gate_col0 // cw))
        args.append(gate_src)
    return pl.pallas_call(
        body, name=name, grid=(nc, nb), in_specs=ins, out_specs=[blk(0), blk(0)],
        out_shape=[jax.ShapeDtypeStruct((lp, width), F32),
                   jax.ShapeDtypeStruct((lp, width), BF16 if ffn else F32)],
        scratch_shapes=[pltpu.VMEM((8, cw), F32)],
        compiler_params=_cparams(("parallel", "arbitrary")))(*args)


def _conv_bwd_call(src, col0, width, cw, w8, taps, ypre, dact, *, gate_src=None, gate_col0=0, name):
    lp = src.shape[0]
    nb, nc = lp // BLK, width // cw
    cb0 = col0 // cw
    ffn = gate_src is not None

    def body(x_ref, w_ref, y_ref, d_ref, *rest):
        if ffn:
            u_ref, dx_ref, du_ref, dw_ref, db_ref, head = rest
        else:
            dx_ref, dw_ref, db_ref, head = rest
        step = pl.program_id(1)
        i = nb - 1 - step

        @pl.when(step == 0)
        def _():
            head[...] = jnp.zeros_like(head)

        rows = _iota((BLK, cw), 0)
        live = (i * BLK + rows) >= PAD
        y = y_ref[...]
        d = d_ref[...]
        if ffn:
            ge, t = _gelu_tanh(y)
            c = math.sqrt(2.0 / math.pi)
            dge = 0.5 * (1.0 + t) + 0.5 * y * (1.0 - t * t) * c * (1.0 + 3.0 * 0.044715 * y * y)
            u = u_ref[...]
            du_ref[...] = jnp.where(live, d * ge, 0.0).astype(BF16)
            dy = jnp.where(live, d * u * dge, 0.0)
        else:
            sg = _sigmoid(y)
            dy = jnp.where(live, d * sg * (1.0 + y * (1.0 - sg)), 0.0)
        x = x_ref[...]
        nh = head[...]
        dx = jnp.zeros_like(dy)
        dws = []
        for s in range(taps):
            sh = _shift_up(dy, nh, s, rows)
            dx = dx + w_ref[taps - 1 - s:taps - s, :] * sh
            dws.append(jnp.sum(x * sh, axis=0, keepdims=True))
        head[...] = dy[:8, :]
        dx_ref[...] = jnp.where(live, dx, 0.0).astype(BF16)
        dw = jnp.concatenate([dws[taps - 1 - k] for k in range(taps)]
                             + [jnp.zeros((8 - taps, cw), F32)], axis=0)
        _acc_rows(dw_ref, dw, step)
        _acc_rows(db_ref, jnp.sum(dy, axis=0, keepdims=True), step)

    blk = lambda c0: pl.BlockSpec((BLK, cw), lambda j, s: (nb - 1 - s, c0 + j))
    ins = [blk(cb0), pl.BlockSpec((8, cw), lambda j, s: (0, j)), blk(0), blk(0)]
    args = [src, w8, ypre, dact]
    outs = [blk(0)]
    oshape = [jax.ShapeDtypeStruct((lp, width), BF16)]
    if ffn:
        ins.append(blk(gate_col0 // cw))
        args.append(gate_src)
        outs.append(blk(0))
        oshape.append(jax.ShapeDtypeStruct((lp, width), BF16))
    outs += [pl.BlockSpec((8, cw), lambda j, s: (0, j)), pl.BlockSpec((1, cw), lambda j, s: (0, j))]
    oshape += [jax.ShapeDtypeStruct((8, width), F32), jax.ShapeDtypeStruct((1, width), F32)]
    return pl.pallas_call(
        body, name=name, grid=(nc, nb), in_specs=ins, out_specs=outs, out_shape=oshape,
        scratch_shapes=[pltpu.VMEM((8, cw), F32)],
        compiler_params=_cparams(("parallel", "arbitrary")))(*args)


SB_GROUP = 4


def _sb_scores(qm_h, kb):
    z = _dot(qm_h, kb, 1, 1)
    sp = jnp.maximum(z, 0.0) + jnp.log(1.0 + jnp.exp(-jnp.abs(z)))
    return z - sp, -sp


def _sb_valid(i, off, width):
    kpos = off + _iota((BLK, width), 1)
    qpos = i * BLK + _iota((BLK, width), 0)
    return (kpos < qpos) & (kpos >= PAD)


def _dot_tri1(v, tri2):
    r = _dot(v.astype(BF16), tri2[:BLK])
    return r[:, :BLK], r[:, BLK:]


def _sb_groups(i):
    edge = i // SB_GROUP
    return edge, pl.multiple_of(edge * (SB_GROUP * BLK), BLK)


def _tri2(cond):
    t = jnp.concatenate([cond.astype(BF16), jnp.ones((BLK, BLK), BF16)], axis=1)
    return jnp.concatenate([t, t], axis=0)


def _dot_tri(v, tri2):
    hi = v.astype(BF16)
    lo = (v - hi.astype(F32)).astype(BF16)
    r = _dot(jnp.concatenate([hi, lo], axis=1), tri2)
    return r[:, :BLK], r[:, BLK:]


def _sb_fwd_call(proj):
    lp = proj.shape[0]
    nb = lp // BLK
    assert (nb - 1) % SB_GROUP == 0
    scale = 1.0 / math.sqrt(HEAD_DIM)

    def body(q_ref, k_ref, v_ref, o_ref, tl_ref):
        i = pl.program_id(1)
        lane = _iota((2 * BLK, BLK), 1)
        row = _iota((2 * BLK, BLK), 0)
        first = row < BLK
        qrow = row & (BLK - 1)
        q = q_ref[...] * scale
        q2 = jnp.concatenate([q, q], axis=0)
        qm = jnp.where(first == (lane < HEAD_DIM), q2, 0.0).astype(BF16)
        tri = _tri2(_iota((BLK, BLK), 0) > _iota((BLK, BLK), 1))

        def chunk(off, nsub, last_valid, carry):
            width = nsub * BLK
            sls = [slice(b * BLK, (b + 1) * BLK) for b in range(nsub)]
            kb = k_ref[pl.ds(off, width), :].astype(BF16)
            vb = v_ref[pl.ds(off, width), :].astype(BF16)
            lb, lk = _sb_scores(qm, kb)
            lks = [lk[:, sl] for sl in sls]
            first_valid = (off + lane) >= PAD
            lks[0] = jnp.where(first_valid, lks[0], 0.0)
            if last_valid is not None:
                lks[-1] = jnp.where(last_valid, lks[-1], 0.0)
            afters = [_dot_tri(lks[b], tri) for b in range(nsub)]
            run, acc = carry
            ws = [None] * nsub
            for b in reversed(range(nsub)):
                wb = jnp.exp(lb[:, sls[b]] + afters[b][0] + run)
                if b == 0:
                    wb = jnp.where(first_valid, wb, 0.0)
                if last_valid is not None and b == nsub - 1:
                    wb = jnp.where(last_valid, wb, 0.0)
                ws[b] = wb.astype(BF16)
                run = run + afters[b][1]
            w = ws[0] if nsub == 1 else jnp.concatenate(ws, axis=1)
            return run, acc + _dot(w, vb)

        edge, edge_off = _sb_groups(i)
        diag = lane < qrow
        zero = jnp.zeros((2 * BLK, BLK), F32)
        upto = [functools.partial(chunk, edge_off, r, diag) for r in range(1, SB_GROUP + 1)]
        carry = lax.switch(i - edge * SB_GROUP, upto, (zero, zero))

        def interior(t, carry):
            off = pl.multiple_of((edge - 2 - 2 * t) * (SB_GROUP * BLK), BLK)
            return chunk(off, 2 * SB_GROUP, None, carry)

        carry = lax.fori_loop(0, edge // 2, interior, carry)
        run, acc = lax.cond(edge % 2 == 1, lambda cr: chunk(0, SB_GROUP, None, cr), lambda cr: cr, carry)
        low = lane[:BLK] < HEAD_DIM
        o_ref[...] = jnp.where(low, acc[:BLK], acc[BLK:])
        tl_ref[...] = jnp.where(low, run[:BLK], run[BLK:])

    qc, kc, vc = C_Q // BLK, C_K // BLK, C_V // BLK
    blk = pl.BlockSpec((BLK, BLK), lambda p, i: (i, p))
    return pl.pallas_call(
        body, name="sb_fwd", grid=(N_HEADS // 2, nb),
        in_specs=[pl.BlockSpec((BLK, BLK), lambda p, i: (i, qc + p)),
                  pl.BlockSpec((lp, BLK), lambda p, i: (0, kc + p)),
                  pl.BlockSpec((lp, BLK), lambda p, i: (0, vc + p))],
        out_specs=[blk, blk],
        out_shape=[jax.ShapeDtypeStruct((lp, N_HEADS * HEAD_DIM), F32)] * 2,
        compiler_params=_cparams(("parallel", "arbitrary")))(proj, proj, proj)


def _sb_bwd_call(proj, tl, do):
    lp = proj.shape[0]
    nb = lp // BLK
    assert (nb - 1) % SB_GROUP == 0
    scale = 1.0 / math.sqrt(HEAD_DIM)

    def body(q_ref, k_ref, v_ref, tl_ref, do_ref, dq_ref, dk_ref, dv_ref, dk_acc, dv_acc):
        i = pl.program_id(1)

        @pl.when(i == 0)
        def _():
            dk_acc[...] = jnp.zeros_like(dk_acc)
            dv_acc[...] = jnp.zeros_like(dv_acc)

        lane = _iota((2 * BLK, BLK), 1)
        row = _iota((2 * BLK, BLK), 0)
        qrow = row & (BLK - 1)
        mine = (row < BLK) == (lane < HEAD_DIM)
        q = q_ref[...] * scale
        dov = do_ref[...]
        qm = jnp.where(mine, jnp.concatenate([q, q], axis=0), 0.0).astype(BF16)
        dom = jnp.where(mine, jnp.concatenate([dov, dov], axis=0), 0.0).astype(BF16)
        tlv = tl_ref[...]
        tot = jnp.concatenate([jnp.broadcast_to(tlv[:, 0:1], (BLK, BLK)),
                               jnp.broadcast_to(tlv[:, HEAD_DIM:HEAD_DIM + 1], (BLK, BLK))], axis=0)
        r1, l1 = _iota((BLK, BLK), 0), _iota((BLK, BLK), 1)
        tri_in = _tri2(r1 <= l1)
        tri_ex = _tri2(r1 < l1)

        def chunk(off, nsub, last_valid, carry):
            width = nsub * BLK
            sls = [slice(b * BLK, (b + 1) * BLK) for b in range(nsub)]
            cat = lambda parts: parts[0] if nsub == 1 else jnp.concatenate(parts, axis=1)
            mask_last = lambda b: last_valid is not None and b == nsub - 1
            kb = k_ref[pl.ds(off, width), :].astype(BF16)
            vb = v_ref[pl.ds(off, width), :].astype(BF16)
            lb, lk = _sb_scores(qm, kb)
            dw = _dot(dom, vb, 1, 1)
            lks = [lk[:, sl] for sl in sls]
            first_valid = (off + lane) >= PAD
            lks[0] = jnp.where(first_valid, lks[0], 0.0)
            if last_valid is not None:
                lks[-1] = jnp.where(last_valid, lks[-1], 0.0)
            pins = [_dot_tri(lks[b], tri_in) for b in range(nsub)]
            run, gsum, dq = carry
            ws, gs = [], []
            for b in range(nsub):
                wb = jnp.exp(lb[:, sls[b]] + (tot - run - pins[b][0]))
                if b == 0:
                    wb = jnp.where(first_valid, wb, 0.0)
                if mask_last(b):
                    wb = jnp.where(last_valid, wb, 0.0)
                ws.append(wb.astype(BF16))
                gs.append(wb * dw[:, sls[b]])
                run = run + pins[b][1]
            gexs = [_dot_tri1(gs[b], tri_ex) for b in range(nsub)]
            beta = jnp.exp(lb)
            parts = []
            for b in range(nsub):
                bt = beta[:, sls[b]]
                dzb = gs[b] * (1.0 - bt) - (gsum + gexs[b][0]) * bt
                if b == 0:
                    dzb = jnp.where(first_valid, dzb, 0.0)
                if mask_last(b):
                    dzb = jnp.where(last_valid, dzb, 0.0)
                parts.append(dzb.astype(BF16))
                gsum = gsum + gexs[b][1]
            dz, w = cat(parts), cat(ws)
            dk_acc[pl.ds(off, width), :] += _dot(dz, qm, 0, 0)
            dv_acc[pl.ds(off, width), :] += _dot(w, dom, 0, 0)
            return run, gsum, dq + _dot(dz, kb)

        edge, edge_off = _sb_groups(i)
        diag = lane < qrow
        zero = jnp.zeros((2 * BLK, BLK), F32)
        odd = edge % 2
        carry = lax.cond(odd == 1, lambda cr: chunk(0, SB_GROUP, None, cr), lambda cr: cr, (zero, zero, zero))

        def interior(t, carry):
            off = pl.multiple_of((odd + 2 * t) * (SB_GROUP * BLK), BLK)
            return chunk(off, 2 * SB_GROUP, None, carry)

        carry = lax.fori_loop(0, edge // 2, interior, carry)
        upto = [functools.partial(chunk, edge_off, r, diag) for r in range(1, SB_GROUP + 1)]
        dq = lax.switch(i - edge * SB_GROUP, upto, carry)[2]
        dq_ref[...] = (jnp.where(lane[:BLK] < HEAD_DIM, dq[:BLK], dq[BLK:]) * scale).astype(BF16)

        @pl.when(i == nb - 1)
        def _():
            dk_ref[...] = dk_acc[...].astype(BF16)
            dv_ref[...] = dv_acc[...].astype(BF16)

    qc, kc, vc = C_Q // BLK, C_K // BLK, C_V // BLK
    blk = pl.BlockSpec((BLK, BLK), lambda p, i: (i, p))
    full = pl.BlockSpec((lp, BLK), lambda p, i: (0, p))
    w = N_HEADS * HEAD_DIM
    return pl.pallas_call(
        body, name="sb_bwd", grid=(N_HEADS // 2, nb),
        in_specs=[pl.BlockSpec((BLK, BLK), lambda p, i: (i, qc + p)),
                  pl.BlockSpec((lp, BLK), lambda p, i: (0, kc + p)),
                  pl.BlockSpec((lp, BLK), lambda p, i: (0, vc + p)),
                  blk, blk],
        out_specs=[blk, full, full],
        out_shape=[jax.ShapeDtypeStruct((lp, w), BF16)] * 3,
        scratch_shapes=[pltpu.VMEM((lp, BLK), F32), pltpu.VMEM((lp, BLK), F32)],
        compiler_params=_cparams(("parallel", "arbitrary")))(proj, proj, proj, tl, do)


def _log1p(e):
    u = 1.0 + e
    return jnp.where(u == 1.0, e, jnp.log(u) * e / jnp.where(u == 1.0, 1.0, u - 1.0))


def _ssd_common(c, dtr, bias, alog):
    row = _iota((BLK, BLK), 0)
    lane = _iota((BLK, BLK), 1)
    live = ((c * BLK + row) >= PAD) & (lane < N_HEADS)
    pre = dtr + bias
    dt = jnp.where(live, jnp.maximum(pre, 0.0) + _log1p(jnp.exp(-jnp.abs(pre))), 0.0)
    a_neg = -jnp.exp(alog)
    a = dt * a_neg
    t_in = (lane <= row).astype(BF16)
    cs = _dot_sel_l(t_in, a)
    cs_t = cs.T
    cs_end = cs[BLK - 1:BLK, :]
    e = jnp.exp(cs)
    f = jnp.exp(cs_end - cs)
    xp = ((_iota((BLK, SSD_INNER), 1) // HEAD_DIM) == _iota((BLK, SSD_INNER), 0)).astype(BF16)
    xp_t = ((_iota((SSD_INNER, BLK), 0) // HEAD_DIM) == _iota((SSD_INNER, BLK), 1)).astype(BF16)
    decay_col = _dot_sel_l(xp_t, jnp.exp(cs_t))[:, BLK - 1:BLK]
    return dict(live=live, pre=pre, dt=dt, a_neg=a_neg, cs=cs, cs_t=cs_t, e=e, f=f, xp=xp, xp_t=xp_t,
                decay_col=decay_col, row=row, lane=lane,
                dt_x=_dot_sel_r(dt, xp), e_x=_dot_sel_r(e, xp), f_x=_dot_sel_r(f, xp))


def _ssd_ldec(q, h):
    diff = q["cs"][:, h:h + 1] - q["cs_t"][h:h + 1, :]
    causal = q["row"] >= q["lane"]
    return jnp.where(causal, jnp.exp(jnp.where(causal, diff, 0.0)), 0.0)


def _ssd_fwd_call(xbc, proj, bias, alog, d_x, norm_g):
    lp = xbc.shape[0]
    nb = lp // BLK
    gw = SSD_INNER // SSD_GROUPS
    ppg = gw // BLK

    def body(xbc_ref, dtr_ref, z_ref, bias_ref, alog_ref, dx_ref, ng_ref, yb_ref, ypre_ref, sprev_ref, s_ref):
        c = pl.program_id(0)

        @pl.when(c == 0)
        def _():
            s_ref[...] = jnp.zeros_like(s_ref)

        q = _ssd_common(c, dtr_ref[...], bias_ref[...], alog_ref[...])
        x = xbc_ref[:, 0:SSD_INNER]
        xd = x * q["dt_x"]
        low = q["lane"] < HEAD_DIM
        s_old = s_ref[...]
        sprev_ref[...] = s_old
        xdf = (xd * q["f_x"]).astype(BF16)
        for g in range(SSD_GROUPS):
            bg = xbc_ref[:, SSD_INNER + g * SSD_STATE:SSD_INNER + (g + 1) * SSD_STATE].astype(BF16)
            cg = xbc_ref[:, SSD_INNER + (SSD_GROUPS + g) * SSD_STATE:
                         SSD_INNER + (SSD_GROUPS + g + 1) * SSD_STATE].astype(BF16)
            cb = _dot(cg, bg, 1, 1)
            gs = slice(g * gw, (g + 1) * gw)
            y_off = _dot(cg, s_old[gs, :].astype(BF16), 1, 1) * q["e_x"][:, gs]
            s_ref[gs, :] = s_old[gs, :] * q["decay_col"][gs, :] + _dot(xdf[:, gs], bg, 0, 0)
            for pr in range(ppg):
                cols = slice(g * gw + pr * BLK, g * gw + (pr + 1) * BLK)
                xd_p = xd[:, cols]
                acc = y_off[:, pr * BLK:(pr + 1) * BLK]
                for hh in range(2):
                    h = (g * gw + pr * BLK) // HEAD_DIM + hh
                    m = (cb * _ssd_ldec(q, h)).astype(BF16)
                    xm = jnp.where(low, xd_p, 0.0) if hh == 0 else jnp.where(low, 0.0, xd_p)
                    acc = acc + _dot(m, xm.astype(BF16))
                ypre_ref[:, cols] = acc
        ypre = ypre_ref[...] + x * dx_ref[...]
        ypre_ref[...] = ypre
        z = z_ref[...]
        yg = ypre * (z * _sigmoid(z))
        _, yh = _rms_stats(yg)
        yb_ref[...] = (yh * ng_ref[...]).astype(BF16)

    row = lambda w, col: pl.BlockSpec((BLK, w), lambda c: (c, col))
    vec = lambda w: pl.BlockSpec((1, w), lambda c: (0, 0))
    return pl.pallas_call(
        body, name="ssd_fwd", grid=(nb,),
        in_specs=[row(XBC, 0), row(BLK, C_DT // BLK), row(SSD_INNER, 0), vec(BLK), vec(BLK),
                  vec(SSD_INNER), vec(SSD_INNER)],
        out_specs=[row(SSD_INNER, 0), row(SSD_INNER, 0),
                   pl.BlockSpec((None, SSD_INNER, SSD_STATE), lambda c: (c, 0, 0))],
        out_shape=[jax.ShapeDtypeStruct((lp, SSD_INNER), BF16), jax.ShapeDtypeStruct((lp, SSD_INNER), F32),
                   jax.ShapeDtypeStruct((nb, SSD_INNER, SSD_STATE), F32)],
        scratch_shapes=[pltpu.VMEM((SSD_INNER, SSD_STATE), F32)],
        compiler_params=_cparams(("arbitrary",)))(xbc, proj, proj, bias, alog, d_x, norm_g)


def _ssd_bwd_call(dycat, ypre, xbc, proj, sprev, bias, alog, d_x, norm_g):
    lp = xbc.shape[0]
    nb = lp // BLK
    gw = SSD_INNER // SSD_GROUPS
    ppg = gw // BLK

    def body(dy_ref, ypre_ref, xbc_ref, dtr_ref, z_ref, sp_ref, bias_ref, alog_ref, dxp_ref, ng_ref,
             dz_ref, dxbc_ref, ddt_ref, dng_ref, dd_ref, dal_ref, dbi_ref, ds_ref, dxd_ref):
        step = pl.program_id(0)
        c = nb - 1 - step

        @pl.when(step == 0)
        def _():
            ds_ref[...] = jnp.zeros_like(ds_ref)

        q = _ssd_common(c, dtr_ref[...], bias_ref[...], alog_ref[...])
        row, lane = q["row"], q["lane"]
        low = lane < HEAD_DIM
        rowlive = ((c * BLK + _iota((BLK, 1), 0)) >= PAD)
        x = xbc_ref[:, 0:SSD_INNER]
        xd = x * q["dt_x"]
        z = z_ref[...]
        sz = _sigmoid(z)
        silu = z * sz
        ypre = ypre_ref[...]
        dyg, dng = _rms_bwd(ypre * silu, ng_ref[...], dy_ref[...])
        _acc_rows(dng_ref, dng, step)
        dyp = dyg * silu
        dz_ref[...] = jnp.where(rowlive, dyg * ypre * (sz * (1.0 + z * (1.0 - sz))), 0.0).astype(BF16)
        _acc_rows(dd_ref, jnp.sum(dyp * x, axis=0, keepdims=True), step)
        dye = dyp * q["e_x"]
        xdf = xd * q["f_x"]
        s_prev = sp_ref[...]
        ds_old = ds_ref[...]
        qrow = jnp.zeros((BLK, BLK), F32)
        qcol_t = jnp.zeros((BLK, BLK), F32)
        red_e = []
        red_f = []
        for g in range(SSD_GROUPS):
            gs = slice(g * gw, (g + 1) * gw)
            bsl = slice(SSD_INNER + g * SSD_STATE, SSD_INNER + (g + 1) * SSD_STATE)
            csl = slice(SSD_INNER + (SSD_GROUPS + g) * SSD_STATE, SSD_INNER + (SSD_GROUPS + g + 1) * SSD_STATE)
            bg = xbc_ref[:, bsl].astype(BF16)
            cg = xbc_ref[:, csl].astype(BF16)
            sg = s_prev[gs, :].astype(BF16)
            dsg = ds_old[gs, :].astype(BF16)
            cb = _dot(cg, bg, 1, 1)
            bds = _dot(bg, dsg, 1, 1)
            y_off = _dot(cg, sg, 1, 1) * q["e_x"][:, gs]
            red_e.append(dyp[:, gs] * y_off)
            red_f.append(xd[:, gs] * bds * q["f_x"][:, gs])
            dc = _dot(dye[:, gs].astype(BF16), sg)
            db = _dot(xdf[:, gs].astype(BF16), dsg)
            ds_ref[gs, :] = ds_old[gs, :] * q["decay_col"][gs, :] + _dot(dye[:, gs].astype(BF16), cg, 0, 0)
            dcb = jnp.zeros((BLK, BLK), F32)
            for pr in range(ppg):
                cols = slice(g * gw + pr * BLK, g * gw + (pr + 1) * BLK)
                xd_p = xd[:, cols].astype(BF16)
                dy_p = dyp[:, cols]
                acc = q["f_x"][:, cols] * bds[:, pr * BLK:(pr + 1) * BLK]
                for hh in range(2):
                    h = (g * gw + pr * BLK) // HEAD_DIM + hh
                    ld = _ssd_ldec(q, h)
                    m = cb * ld
                    dym = (jnp.where(low, dy_p, 0.0) if hh == 0 else jnp.where(low, 0.0, dy_p)).astype(BF16)
                    dm = jnp.where(row >= lane, _dot(dym, xd_p, 1, 1), 0.0)
                    acc = acc + _dot(m.astype(BF16), dym, 0, 0)
                    dcb = dcb + dm * ld
                    qq = dm * m
                    qrow = qrow + jnp.where(lane == h, jnp.sum(qq, axis=1, keepdims=True), 0.0)
                    qcol_t = qcol_t + jnp.where(row == h, jnp.sum(qq, axis=0, keepdims=True), 0.0)
                dxd_ref[:, cols] = acc
            dcbb = dcb.astype(BF16)
            dxbc_ref[:, bsl] = jnp.where(rowlive, db + _dot(dcbb, cg, 0, 0), 0.0)
            dxbc_ref[:, csl] = jnp.where(rowlive, dc + _dot(dcbb, bg), 0.0)
        dxd = dxd_ref[...]
        dxbc_ref[:, 0:SSD_INNER] = jnp.where(rowlive, dxd * q["dt_x"] + dyp * dxp_ref[...], 0.0)
        xp_t = q["xp_t"]
        fw = _dot_sel_r(jnp.concatenate(red_f, axis=1), xp_t)
        dcs = qrow - qcol_t.T + _dot_sel_r(jnp.concatenate(red_e, axis=1), xp_t) - fw
        end_f = jnp.sum(fw, axis=0, keepdims=True)
        sds = jnp.sum(ds_old * s_prev, axis=1, keepdims=True)
        per_head = _dot_sel_l(q["xp"], jnp.broadcast_to(sds, (SSD_INNER, BLK)))
        end_e = per_head.T[0:1, :] * jnp.exp(q["cs"][BLK - 1:BLK, :])
        dcs = dcs + jnp.where(row == BLK - 1, end_f + end_e, 0.0)
        t_up = (lane >= row).astype(BF16)
        da = _dot_sel_l(t_up, dcs)
        ddt = da * q["a_neg"] + _dot_sel_r(dxd * x, xp_t)
        _acc_rows(dal_ref, jnp.sum(da * q["dt"] * q["a_neg"], axis=0, keepdims=True), step)
        ddtr = jnp.where(q["live"], ddt * _sigmoid(q["pre"]), 0.0)
        ddt_ref[...] = ddtr.astype(BF16)
        _acc_rows(dbi_ref, jnp.sum(ddtr, axis=0, keepdims=True), step)

    row_s = lambda w, col: pl.BlockSpec((BLK, w), lambda s: (nb - 1 - s, col))
    vec = lambda w: pl.BlockSpec((1, w), lambda s: (0, 0))
    return pl.pallas_call(
        body, name="ssd_bwd", grid=(nb,),
        in_specs=[row_s(SSD_INNER, 0), row_s(SSD_INNER, 0), row_s(XBC, 0), row_s(BLK, C_DT // BLK),
                  row_s(SSD_INNER, 0), pl.BlockSpec((None, SSD_INNER, SSD_STATE), lambda s: (nb - 1 - s, 0, 0)),
                  vec(BLK), vec(BLK), vec(SSD_INNER), vec(SSD_INNER)],
        out_specs=[row_s(SSD_INNER, 0), row_s(XBC, 0), row_s(BLK, 0),
                   vec(SSD_INNER), vec(SSD_INNER), vec(BLK), vec(BLK)],
        out_shape=[jax.ShapeDtypeStruct((lp, SSD_INNER), BF16), jax.ShapeDtypeStruct((lp, XBC), F32),
                   jax.ShapeDtypeStruct((lp, BLK), BF16),
                   jax.ShapeDtypeStruct((1, SSD_INNER), F32), jax.ShapeDtypeStruct((1, SSD_INNER), F32),
                   jax.ShapeDtypeStruct((1, BLK), F32), jax.ShapeDtypeStruct((1, BLK), F32)],
        scratch_shapes=[pltpu.VMEM((SSD_INNER, SSD_STATE), F32), pltpu.VMEM((BLK, SSD_INNER), F32)],
        compiler_params=_cparams(("arbitrary",)))(dycat, ypre, xbc, proj, proj, sprev, bias, alog, d_x, norm_g)


def _pad_rows8(w):
    return jnp.pad(w, ((0, 8 - w.shape[0]), (0, 0)))


def _pad_lanes(v, n=BLK):
    return jnp.pad(v, ((0, 0), (0, n - v.shape[1])))


def _local_step(x, target, wt):
    seq = x.shape[0]
    lp = seq + BLK
    tm = _pick(lp, [1408, 768, 384, 128])
    tkr = _pick(lp, [1408, 384, 128])
    h0 = jnp.concatenate([jnp.zeros((PAD, D_MODEL), F32), wt["meta"], x], axis=0)
    bias = _pad_lanes(wt["ssd_dt_bias"])
    alog = _pad_lanes(wt["ssd_a_log"])
    d_x = jnp.repeat(wt["ssd_d"], HEAD_DIM, axis=1)
    cw8 = _pad_rows8(wt["ssd_conv_w"])
    fw8 = _pad_rows8(wt["ffn_conv_w"])
    fcw = D_FF // 2

    xn1 = _rms_fwd_call(h0, wt["mix_pre_g"], "norm1")
    proj = _mm(xn1, wt["w_in"], tm=tm, tn=1152, tk=D_MODEL, name="mm_proj")[0]
    conv_pre, xbc = _conv_fwd_call(proj, C_XBC, XBC, 512, cw8, wt["ssd_conv_b"], 4, name="ssd_conv_fwd")
    y_ssd, ypre, sprev = _ssd_fwd_call(xbc, proj, bias, alog, d_x, wt["ssd_norm_g"])
    o, tl = _sb_fwd_call(proj)
    y_sb = _rms_fwd_call(o, wt["sb_norm_g"], "sb_norm")
    ycat = jnp.concatenate([y_ssd, y_sb], axis=1)
    mix = _mm(ycat, wt["w_out"], tm=tm, tn=1024, tk=2048, name="mm_mix")[0]
    h1, xn2 = _mid_fwd_call(h0, mix, wt["mix_post_g"], wt["ffn_pre_g"])
    gu = _mm(xn2, wt["w_up"], tm=tm, tn=1408, tk=D_MODEL, name="mm_up")[0]
    gpre, act = _conv_fwd_call(gu, 0, D_FF, fcw, fw8, wt["ffn_conv_b"], 3, gate_src=gu, gate_col0=D_FF,
                               name="ffn_conv_fwd")
    f = _mm(act, wt["w_down"], tm=tm, tn=1024, tk=1408, name="mm_down")[0]
    loss_row, df, dh2, dg_ffn_post = _final_call(h1, f, wt["ffn_post_g"], target)

    dact = _mm(df, wt["w_down"], tb=True, tm=tm, tn=1408, tk=D_MODEL, name="mm_dact")[0]
    dw_down, dw_down_b = _mm(act, df, ta=True, tm=1408, tn=1024, tk=tkr, extra_bf16=True, name="mm_dw_down")
    dgate, dup, dfcw, dfcb = _conv_bwd_call(gu, 0, D_FF, fcw, fw8, 3, gpre, dact, gate_src=gu, gate_col0=D_FF,
                                            name="ffn_conv_bwd")
    dgu = jnp.concatenate([dgate, dup], axis=1)
    dxn2 = _mm(dgu, wt["w_up"], tb=True, tm=tm, tn=1024, tk=1408, name="mm_dxn2")[0]
    dw_up, dw_up_b = _mm(xn2, dgu, ta=True, tm=1024, tn=1408, tk=tkr, nsplit=N_CHIPS, extra_bf16=True,
                         name="mm_dw_up")
    dh1, dmix, dg_ffn_pre, dg_mix_post = _mid_bwd_call(dh2, h1, dxn2, mix, wt["ffn_pre_g"], wt["mix_post_g"])
    dycat = _mm(dmix, wt["w_out"], tb=True, tm=tm, tn=1024, tk=D_MODEL, name="mm_dycat")[0]
    dw_out, dw_out_b = _mm(ycat, dmix, ta=True, tm=1024, tn=1024, tk=tkr, extra_bf16=True, name="mm_dw_out")
    do, dg_sb = _norm_bwd_call(o, wt["sb_norm_g"], dycat, 1, "sb_norm_bwd")
    dq, dk, dv = _sb_bwd_call(proj, tl, do)
    dz, dxbc_act, ddt, dg_ssd, dd_x, dalog, dbias = _ssd_bwd_call(
        dycat, ypre, xbc, proj, sprev, bias, alog, d_x, wt["ssd_norm_g"])
    dxbc, dcw, dcb = _conv_bwd_call(proj, C_XBC, XBC, 512, cw8, 4, conv_pre, dxbc_act, name="ssd_conv_bwd")
    dproj = jnp.concatenate([dz, dxbc, ddt, dq, dk, dv], axis=1)
    dxn1 = _mm(dproj, wt["w_in"], tb=True, tm=tm, tn=1024, tk=1152, name="mm_dxn1")[0]
    dw_in = _mm(xn1, dproj, ta=True, tm=1024, tn=1152, tk=tkr, name="mm_dw_in")[0]
    dh0, dg_pre = _norm_bwd_call(h0, wt["mix_pre_g"], dxn1, 0, "norm1_bwd", res=dh1)

    small = {
        "meta_tokens": dh0[PAD:BLK], "mix_pre_g": dg_pre, "ssd_conv_w": dcw[:4], "ssd_conv_b": dcb,
        "ssd_dt_bias": dbias[:, :N_HEADS], "ssd_a_log": dalog[:, :N_HEADS],
        "ssd_d": jnp.sum(dd_x.reshape(N_HEADS, HEAD_DIM), axis=1)[None],
        "ssd_norm_g": dg_ssd, "sb_norm_g": dg_sb, "mix_post_g": dg_mix_post, "ffn_pre_g": dg_ffn_pre,
        "ffn_conv_w": dfcw[:3], "ffn_conv_b": dfcb, "ffn_post_g": dg_ffn_post,
    }
    big = {"w_in": dw_in, "w_out": (dw_out, dw_out_b), "w_up": (dw_up, dw_up_b), "w_down": (dw_down, dw_down_b)}
    return loss_row, dh0[BLK:], small, big


def _adamw_call(w, g, m, v, name):
    rows, cols = w.shape
    tr = 256 if rows % 256 == 0 else (352 if rows % 352 == 0 else rows)
    c1 = 1.0 - ADAM_B1 ** ADAM_STEP
    c2 = 1.0 - ADAM_B2 ** ADAM_STEP

    def body(w_ref, g_ref, m_ref, v_ref, d_ref, mo_ref, vo_ref):
        gv = g_ref[...]
        m2 = ADAM_B1 * m_ref[...] + (1.0 - ADAM_B1) * gv
        v2 = ADAM_B2 * v_ref[...] + (1.0 - ADAM_B2) * (gv * gv)
        d_ref[...] = -ADAM_LR * ((m2 / c1) / (jnp.sqrt(v2 / c2) + ADAM_EPS) + ADAM_WD * w_ref[...])
        mo_ref[...] = m2
        vo_ref[...] = v2

    spec = pl.BlockSpec((tr, cols), lambda i: (i, 0))
    return pl.pallas_call(
        body, name=name, grid=(rows // tr,), in_specs=[spec] * 4, out_specs=[spec] * 3,
        out_shape=[jax.ShapeDtypeStruct((rows, cols), F32)] * 3,
        compiler_params=_cparams(("parallel",)))(w, g, m, v)


ANY = pl.BlockSpec(memory_space=pl.ANY)


def _place():
    x, y, c = lax.axis_index("x"), lax.axis_index("y"), lax.axis_index("c")
    chips = [(1 - x, y), (x, 1 - y), (1 - x, 1 - y)]
    return x, y, c, chips


def _half(c, h):
    return pl.ds(pl.multiple_of(c * h, 8), h)


def _allgather_call(shards):
    n = len(shards)

    def body(*refs):
        ins, outs = refs[:n], refs[n:2 * n]
        send_i, recv_i, send_d, recv_d = refs[2 * n:]
        x, y, c, chips = _place()
        me = 2 * x + y
        sends = []
        for a in range(n):
            h = shards[a].shape[0] // 2
            for j, chip in enumerate(chips):
                cp = pltpu.make_async_remote_copy(
                    src_ref=ins[a].at[_half(c, h)], dst_ref=outs[a].at[me, _half(c, h)],
                    send_sem=send_i.at[3 * a + j], recv_sem=recv_i.at[3 * a + j],
                    device_id=(*chip, c), device_id_type=MESH)
                cp.start()
                sends.append(cp)
        for a in range(n):
            h = shards[a].shape[0] // 2
            for j, chip in enumerate(chips):
                src = 2 * chip[0] + chip[1]
                landed = outs[a].at[src, _half(c, h)]
                pltpu.make_async_remote_copy(
                    src_ref=landed, dst_ref=landed, send_sem=send_i.at[3 * a + j], recv_sem=recv_i.at[3 * a + j],
                    device_id=(*chip, c), device_id_type=MESH).wait_recv()
                cp = pltpu.make_async_remote_copy(
                    src_ref=landed, dst_ref=landed, send_sem=send_d.at[3 * a + j], recv_sem=recv_d.at[3 * a + j],
                    device_id=(x, y, 1 - c), device_id_type=MESH)
                cp.start()
                sends.append(cp)
        for a in range(n):
            h = shards[a].shape[0] // 2
            for j, chip in enumerate(chips):
                src = 2 * chip[0] + chip[1]
                other = outs[a].at[src, _half(1 - c, h)]
                pltpu.make_async_remote_copy(
                    src_ref=other, dst_ref=other, send_sem=send_d.at[3 * a + j], recv_sem=recv_d.at[3 * a + j],
                    device_id=(x, y, 1 - c), device_id_type=MESH).wait_recv()
        for cp in sends:
            cp.wait_send()

    return pl.pallas_call(
        body, name="allgather_weights", in_specs=[ANY] * n, out_specs=[ANY] * n,
        out_shape=[jax.ShapeDtypeStruct((N_CHIPS,) + s.shape, s.dtype) for s in shards],
        scratch_shapes=[pltpu.SemaphoreType.DMA((3 * n,))] * 4,
    )(*shards)


def _pair_exchange_call(grads):
    n = len(grads)

    def body(*refs):
        ins, outs = refs[:n], refs[n:2 * n]
        send_d, recv_d = refs[2 * n:]
        x, y, c, _ = _place()
        cps = []
        for a in range(n):
            h = grads[a].shape[1] // 2
            cp = pltpu.make_async_remote_copy(
                src_ref=ins[a].at[:, _half(1 - c, h)], dst_ref=outs[a], send_sem=send_d.at[a], recv_sem=recv_d.at[a],
                device_id=(x, y, 1 - c), device_id_type=MESH)
            cp.start()
            cps.append(cp)
        for cp in cps:
            cp.wait()

    return pl.pallas_call(
        body, name="grad_pair_exchange", in_specs=[ANY] * n, out_specs=[ANY] * n,
        out_shape=[jax.ShapeDtypeStruct((N_CHIPS, g.shape[1] // 2, g.shape[2]), BF16) for g in grads],
        scratch_shapes=[pltpu.SemaphoreType.DMA((n,))] * 2,
    )(*grads)


def _pair_sum_call(own, got, c_idx, name):
    _, rows, cols = own.shape
    h = rows // 2
    th = _pick(h, [256, 176, 8])
    nt = h // th

    def body(c_ref, own_ref, got_ref, o_ref):
        o_ref[...] = (own_ref[...] + got_ref[...].astype(F32)).astype(BF16)

    return pl.pallas_call(
        body, name=name,
        grid_spec=pltpu.PrefetchScalarGridSpec(
            num_scalar_prefetch=1, grid=(N_CHIPS, nt),
            in_specs=[pl.BlockSpec((None, th, cols), lambda s, i, c_ref: (s, c_ref[0] * nt + i, 0)),
                      pl.BlockSpec((None, th, cols), lambda s, i, c_ref: (s, i, 0))],
            out_specs=pl.BlockSpec((None, th, cols), lambda s, i, c_ref: (s, i, 0))),
        out_shape=jax.ShapeDtypeStruct((N_CHIPS, h, cols), BF16),
        compiler_params=_cparams(("parallel", "parallel")))(c_idx, own, got)


def _chip_exchange_call(pairs):
    n = len(pairs)

    def body(*refs):
        ins, outs = refs[:n], refs[n:2 * n]
        send_i, recv_i = refs[2 * n:]
        x, y, c, chips = _place()
        me = 2 * x + y
        sends = []
        for a in range(n):
            for j, chip in enumerate(chips):
                dst = 2 * chip[0] + chip[1]
                cp = pltpu.make_async_remote_copy(
                    src_ref=ins[a].at[dst], dst_ref=outs[a].at[me], send_sem=send_i.at[3 * a + j],
                    recv_sem=recv_i.at[3 * a + j], device_id=(*chip, c), device_id_type=MESH)
                cp.start()
                sends.append(cp)
        for a in range(n):
            for j, chip in enumerate(chips):
                src = 2 * chip[0] + chip[1]
                pltpu.make_async_remote_copy(
                    src_ref=ins[a].at[src], dst_ref=outs[a].at[src], send_sem=send_i.at[3 * a + j],
                    recv_sem=recv_i.at[3 * a + j], device_id=(*chip, c), device_id_type=MESH).wait_recv()
        for cp in sends:
            cp.wait_send()

    return pl.pallas_call(
        body, name="grad_chip_exchange", in_specs=[ANY] * n, out_specs=[ANY] * n,
        out_shape=[jax.ShapeDtypeStruct(p.shape, BF16) for p in pairs],
        scratch_shapes=[pltpu.SemaphoreType.DMA((3 * n,))] * 2,
    )(*pairs)


def _chip_sum_call(pair, by_chip, place, name):
    _, h, cols = pair.shape
    th = _pick(h, [256, 176, 8])
    nt = h // th

    def body(p_ref, own_ref, a_ref, b_ref, c_ref, o_ref):
        s = own_ref[...].astype(F32) + a_ref[...].astype(F32)
        o_ref[...] = (s + b_ref[...].astype(F32)) + c_ref[...].astype(F32)

    def src(k):
        return pl.BlockSpec((None, th, cols), lambda i, p_ref: (p_ref[k], i, 0))

    return pl.pallas_call(
        body, name=name,
        grid_spec=pltpu.PrefetchScalarGridSpec(
            num_scalar_prefetch=1, grid=(nt,), in_specs=[src(1), src(2), src(3), src(4)],
            out_specs=pl.BlockSpec((th, cols), lambda i, p_ref: (p_ref[0] * nt + i, 0))),
        out_shape=jax.ShapeDtypeStruct((2 * h, cols), F32),
        compiler_params=_cparams(("parallel",)))(place, pair, by_chip, by_chip, by_chip)


def _half_exchange_call(shards):
    n = len(shards)

    def body(*refs):
        outs = refs[n:2 * n]
        send_d, recv_d = refs[2 * n:]
        x, y, c, _ = _place()
        cps = []
        for a in range(n):
            h = shards[a].shape[0] // 2
            mine = outs[a].at[_half(c, h)]
            cp = pltpu.make_async_remote_copy(
                src_ref=mine, dst_ref=mine, send_sem=send_d.at[a], recv_sem=recv_d.at[a],
                device_id=(x, y, 1 - c), device_id_type=MESH)
            cp.start()
            cps.append(cp)
        for a, cp in enumerate(cps):
            h = shards[a].shape[0] // 2
            theirs = outs[a].at[_half(1 - c, h)]
            pltpu.make_async_remote_copy(
                src_ref=theirs, dst_ref=theirs, send_sem=send_d.at[a], recv_sem=recv_d.at[a],
                device_id=(x, y, 1 - c), device_id_type=MESH).wait_recv()
            cp.wait_send()

    return pl.pallas_call(
        body, name="grad_half_exchange", in_specs=[ANY] * n, out_specs=[ANY] * n,
        out_shape=[jax.ShapeDtypeStruct(sv.shape, F32) for sv in shards],
        input_output_aliases={a: a for a in range(n)},
        scratch_shapes=[pltpu.SemaphoreType.DMA((n,))] * 2,
    )(*shards)


def _allreduce_small_call(arrs):
    n = len(arrs)
    offs, rows = [], 0
    for a in arrs:
        offs.append(rows)
        rows += a.shape[0]
    rows = -(-rows // 8) * 8
    width = -(-max(a.shape[1] for a in arrs) // BLK) * BLK

    def body(*refs):
        ins, outs = refs[:n], refs[n:2 * n]
        gath, send_sems, recv_sems = refs[2 * n:]
        x, y, c, chips = _place()
        me, sibling = (x, y, c), (x, y, 1 - c)

        def slot(px, py, pc):
            return gath.at[4 * px + 2 * py + pc]

        def copy(k, block, to):
            return pltpu.make_async_remote_copy(
                src_ref=slot(*block), dst_ref=slot(*block),
                send_sem=send_sems.at[k], recv_sem=recv_sems.at[k], device_id=to, device_id_type=MESH)

        mine = slot(*me)
        mine[...] = jnp.zeros((rows, width), F32)
        for k in range(n):
            r, w = arrs[k].shape
            mine[offs[k]:offs[k] + r, 0:w] = ins[k][...]
        first = [copy(0, me, sibling)]
        first += [copy(1 + j, me, (*chip, c)) for j, chip in enumerate(chips)]
        for cp in first:
            cp.start()
        passed = [copy(4 + j, (*chip, c), sibling) for j, chip in enumerate(chips)]
        for j, chip in enumerate(chips):
            copy(1 + j, (*chip, c), me).wait_recv()
            passed[j].start()
        copy(0, sibling, me).wait_recv()
        for j, chip in enumerate(chips):
            copy(4 + j, (*chip, 1 - c), me).wait_recv()
        for cp in first + passed:
            cp.wait_send()
        acc = gath[0]
        for d in range(1, 8):
            acc = acc + gath[d]
        for k in range(n):
            r, w = arrs[k].shape
            outs[k][...] = acc[offs[k]:offs[k] + r, 0:w]

    vm = pl.BlockSpec(memory_space=pltpu.VMEM)
    return pl.pallas_call(
        body, name="allreduce_small", in_specs=[vm] * n, out_specs=[vm] * n,
        out_shape=[jax.ShapeDtypeStruct(a.shape, F32) for a in arrs],
        scratch_shapes=[pltpu.VMEM((8, rows, width), F32), pltpu.SemaphoreType.DMA((7,)),
                        pltpu.SemaphoreType.DMA((7,))],
        compiler_params=pltpu.CompilerParams(vmem_limit_bytes=VMEM_LIMIT),
    )(*arrs)


def _adamw_small_call(ws, gs, ms, vs):
    n = len(ws)
    c1 = 1.0 - ADAM_B1 ** ADAM_STEP
    c2 = 1.0 - ADAM_B2 ** ADAM_STEP

    def body(*refs):
        for k in range(n):
            w_ref, g_ref, m_ref, v_ref = (refs[j * n + k] for j in range(4))
            d_ref, mo_ref, vo_ref = (refs[(4 + j) * n + k] for j in range(3))
            gv = g_ref[...]
            m2 = ADAM_B1 * m_ref[...] + (1.0 - ADAM_B1) * gv
            v2 = ADAM_B2 * v_ref[...] + (1.0 - ADAM_B2) * (gv * gv)
            d_ref[...] = -ADAM_LR * ((m2 / c1) / (jnp.sqrt(v2 / c2) + ADAM_EPS) + ADAM_WD * w_ref[...])
            mo_ref[...] = m2
            vo_ref[...] = v2

    vm = pl.BlockSpec(memory_space=pltpu.VMEM)
    res = pl.pallas_call(
        body, name="adamw_small", in_specs=[vm] * (4 * n), out_specs=[vm] * (3 * n),
        out_shape=[jax.ShapeDtypeStruct(a.shape, F32) for a in ws] * 3,
        compiler_params=pltpu.CompilerParams(vmem_limit_bytes=VMEM_LIMIT),
    )(*ws, *gs, *ms, *vs)
    return res[:n], res[n:2 * n], res[2 * n:]


def _pack(arrs, min_rows=8):
    parts = []
    for a in arrs:
        flat = a.reshape(-1).astype(F32)
        parts.append(jnp.pad(flat, (0, (-flat.shape[0]) % BLK)))
    buf = jnp.concatenate(parts).reshape(-1, BLK)
    return jnp.pad(buf, ((0, (-buf.shape[0]) % min_rows), (0, 0)))


def _unpack(buf, shapes):
    out, r = [], 0
    for shp in shapes:
        n = math.prod(shp)
        nr = -(-n // BLK)
        out.append(buf[r:r + nr].reshape(-1)[:n].reshape(shp))
        r += nr
    return out


SMALL = ["meta_tokens", "mix_pre_g", "ssd_conv_w", "ssd_conv_b", "ssd_dt_bias", "ssd_a_log", "ssd_d", "ssd_norm_g",
         "sb_norm_g", "mix_post_g", "ffn_pre_g", "ffn_conv_w", "ffn_conv_b", "ffn_post_g"]
BIG = ["w_in", "w_out", "w_up", "w_down"]
WEIGHTS = ["meta_tokens", "mix_pre_g", "w_in", "ssd_conv_w", "ssd_conv_b", "ssd_dt_bias", "ssd_a_log", "ssd_d",
           "ssd_norm_g", "sb_norm_g", "w_out", "mix_post_g", "ffn_pre_g", "w_up", "ffn_conv_w", "ffn_conv_b",
           "w_down", "ffn_post_g"]
W_IN_SHARD = IN_COLS // N_CHIPS
W_IN_PAD = 1536


def kernel(x, meta_tokens, mix_pre_g, w_in, ssd_conv_w, ssd_conv_b, ssd_dt_bias, ssd_a_log, ssd_d, ssd_norm_g, sb_norm_g, w_out, mix_post_g, ffn_pre_g, w_up, ffn_conv_w, ffn_conv_b, w_down, ffn_post_g, loss_target, m_meta_tokens, m_mix_pre_g, m_w_in, m_ssd_conv_w, m_ssd_conv_b, m_ssd_dt_bias, m_ssd_a_log, m_ssd_d, m_ssd_norm_g, m_sb_norm_g, m_w_out, m_mix_post_g, m_ffn_pre_g, m_w_up, m_ffn_conv_w, m_ffn_conv_b, m_w_down, m_ffn_post_g, v_meta_tokens, v_mix_pre_g, v_w_in, v_ssd_conv_w, v_ssd_conv_b, v_ssd_dt_bias, v_ssd_a_log, v_ssd_d, v_ssd_norm_g, v_sb_norm_g, v_w_out, v_mix_post_g, v_ffn_pre_g, v_w_up, v_ffn_conv_w, v_ffn_conv_b, v_w_down, v_ffn_post_g):
    w = dict(meta_tokens=meta_tokens, mix_pre_g=mix_pre_g, w_in=w_in, ssd_conv_w=ssd_conv_w, ssd_conv_b=ssd_conv_b, ssd_dt_bias=ssd_dt_bias, ssd_a_log=ssd_a_log, ssd_d=ssd_d, ssd_norm_g=ssd_norm_g, sb_norm_g=sb_norm_g, w_out=w_out, mix_post_g=mix_post_g, ffn_pre_g=ffn_pre_g, w_up=w_up, ffn_conv_w=ffn_conv_w, ffn_conv_b=ffn_conv_b, w_down=w_down, ffn_post_g=ffn_post_g)
    m = dict(meta_tokens=m_meta_tokens, mix_pre_g=m_mix_pre_g, w_in=m_w_in, ssd_conv_w=m_ssd_conv_w, ssd_conv_b=m_ssd_conv_b, ssd_dt_bias=m_ssd_dt_bias, ssd_a_log=m_ssd_a_log, ssd_d=m_ssd_d, ssd_norm_g=m_ssd_norm_g, sb_norm_g=m_sb_norm_g, w_out=m_w_out, mix_post_g=m_mix_post_g, ffn_pre_g=m_ffn_pre_g, w_up=m_w_up, ffn_conv_w=m_ffn_conv_w, ffn_conv_b=m_ffn_conv_b, w_down=m_w_down, ffn_post_g=m_ffn_post_g)
    v = dict(meta_tokens=v_meta_tokens, mix_pre_g=v_mix_pre_g, w_in=v_w_in, ssd_conv_w=v_ssd_conv_w, ssd_conv_b=v_ssd_conv_b, ssd_dt_bias=v_ssd_dt_bias, ssd_a_log=v_ssd_a_log, ssd_d=v_ssd_d, ssd_norm_g=v_ssd_norm_g, sb_norm_g=v_sb_norm_g, w_out=v_w_out, mix_post_g=v_mix_post_g, ffn_pre_g=v_ffn_pre_g, w_up=v_w_up, ffn_conv_w=v_ffn_conv_w, ffn_conv_b=v_ffn_conv_b, w_down=v_w_down, ffn_post_g=v_ffn_post_g)
    chip = 2 * lax.axis_index("x") + lax.axis_index("y")
    place = jnp.stack([lax.axis_index("c"), chip, chip ^ 2, chip ^ 1, chip ^ 3]).astype(jnp.int32)

    shard_small = [w["meta_tokens"], w["ssd_conv_w"][0], w["ffn_conv_w"][0]]
    shards = [
        jnp.pad(w["w_in"][0], ((0, 0), (0, W_IN_PAD - W_IN_SHARD))).astype(BF16),
        w["w_out"][0].astype(BF16), w["w_up"][0].astype(BF16), w["w_down"][0].astype(BF16),
        _pack(shard_small, 16),
    ]
    gathered = _allgather_call(shards)

    def blocks(a):
        return [jnp.where(chip == i, shards[a], gathered[a][i]) for i in range(N_CHIPS)]

    cut = DT_REAL_OFF + N_HEADS - W_IN_SHARD
    s_in = blocks(0)
    w_in_c = jnp.concatenate(
        [s_in[0][:, :W_IN_SHARD], s_in[1][:, :cut], jnp.zeros((D_MODEL, BLK - N_HEADS), BF16),
         s_in[1][:, cut:W_IN_SHARD], s_in[2][:, :W_IN_SHARD], s_in[3][:, :W_IN_SHARD]], axis=1)
    parts = [_unpack(b, [s.shape for s in shard_small]) for b in blocks(4)]
    wt = {k: w[k][0][None] if w[k].ndim == 3 else w[k] for k in
          ["mix_pre_g", "ssd_conv_b", "ssd_dt_bias", "ssd_a_log", "ssd_d", "ssd_norm_g", "sb_norm_g", "mix_post_g",
           "ffn_pre_g", "ffn_conv_b", "ffn_post_g"]}
    wt.update(
        meta=jnp.concatenate([p[0] for p in parts], axis=1),
        ssd_conv_w=jnp.concatenate([p[1] for p in parts], axis=1),
        ffn_conv_w=jnp.concatenate([p[2] for p in parts], axis=1),
        w_in=w_in_c, w_out=jnp.concatenate(blocks(1), axis=0), w_up=jnp.concatenate(blocks(2), axis=1),
        w_down=jnp.concatenate(blocks(3), axis=0))

    loss_row, dx, small, big = _local_step(x[0], loss_target[0], wt)

    dw_in = big["w_in"]
    skip = BLK - N_HEADS
    by_chip_cols = [dw_in[:, :W_IN_SHARD],
                    jnp.concatenate([dw_in[:, W_IN_SHARD:W_IN_SHARD + cut], dw_in[:, C_Q:2 * W_IN_SHARD + skip]], axis=1),
                    dw_in[:, 2 * W_IN_SHARD + skip:3 * W_IN_SHARD + skip], dw_in[:, 3 * W_IN_SHARD + skip:]]
    dw_in_s = jnp.stack([jnp.pad(b, ((0, 0), (0, W_IN_PAD - W_IN_SHARD))) for b in by_chip_cols])
    own = [dw_in_s, big["w_out"][0].reshape(N_CHIPS, -1, D_MODEL), big["w_up"][0],
           big["w_down"][0].reshape(N_CHIPS, -1, D_MODEL)]
    own_b = [dw_in_s.astype(BF16), big["w_out"][1].reshape(N_CHIPS, -1, D_MODEL), big["w_up"][1],
             big["w_down"][1].reshape(N_CHIPS, -1, D_MODEL)]
    got = _pair_exchange_call(own_b)
    pairs = [_pair_sum_call(own[a], got[a], place, "pair_sum_" + BIG[a]) for a in range(4)]
    by_chip = _chip_exchange_call(pairs)
    full = _half_exchange_call(
        [_chip_sum_call(pairs[a], by_chip[a], place, "chip_sum_" + BIG[a]) for a in range(4)])
    grads = {"w_in": full[0][:, :W_IN_SHARD], "w_out": full[1], "w_up": full[2], "w_down": full[3]}

    red_list = _allreduce_small_call([small[k] for k in SMALL] + [loss_row])
    loss = jnp.sum(red_list[-1])
    for k, g in zip(SMALL, red_list[:-1]):
        grads[k] = g
    for k in ["meta_tokens", "ssd_conv_w", "ffn_conv_w"]:
        wk = w[k].shape[-1]
        grads[k] = lax.dynamic_slice_in_dim(grads[k], chip * wk, wk, axis=1)

    delta, new_m, new_v = {}, {}, {}
    for k in BIG:
        delta[k], new_m[k], new_v[k] = _adamw_call(w[k][0], grads[k], m[k][0], v[k][0], "adamw_" + k)
    flat = lambda d: [d[k].reshape(grads[k].shape) for k in SMALL]
    res = _adamw_small_call(flat(w), [grads[k] for k in SMALL], flat(m), flat(v))
    for out, arrs in zip((delta, new_m, new_v), res):
        for k, a in zip(SMALL, arrs):
            out[k] = a

    def shaped(d, k):
        return d[k].reshape(w[k].shape)

    return (loss, dx[None], *[shaped(grads, k) for k in WEIGHTS], *[shaped(delta, k) for k in WEIGHTS],
            *[shaped(new_m, k) for k in WEIGHTS], *[shaped(new_v, k) for k in WEIGHTS])
```

```python
import functools
import math

import jax
import jax.numpy as jnp
from jax import lax
from jax.experimental import pallas as pl
from jax.experimental.pallas import tpu as pltpu

F32 = jnp.float32
BF16 = jnp.bfloat16

D_MODEL = 1024
N_META = 16
BLK = 128
PAD = BLK - N_META
HEAD_DIM = 64
N_HEADS = 16
SSD_GROUPS = 2
SSD_STATE = 128
SSD_INNER = 1024
XBC = SSD_INNER + 2 * SSD_GROUPS * SSD_STATE
D_FF = 2816
EPS = 1e-6
IN_COLS = 5648
C_Z, C_XBC, C_DT, C_Q, C_K, C_V, C_END = 0, 1024, 2560, 2688, 3712, 4736, 5760
DT_REAL_OFF = 2560
N_CHIPS = 4
ADAM_LR, ADAM_B1, ADAM_B2, ADAM_EPS, ADAM_WD, ADAM_STEP = 0.001, 0.9, 0.999, 1e-08, 0.01, 10
VMEM_LIMIT = 56 * 1024 * 1024
MESH = pl.DeviceIdType.MESH


def _cparams(sem=None, **kw):
    if sem is not None:
        kw["dimension_semantics"] = sem
    return pltpu.CompilerParams(vmem_limit_bytes=VMEM_LIMIT, **kw)


def _pick(n, cands):
    for c in cands:
        if n % c == 0:
            return c
    raise ValueError((n, cands))


def _iota(shape, dim):
    return lax.broadcasted_iota(jnp.int32, shape, dim)


def _sigmoid(x):
    return 1.0 / (1.0 + jnp.exp(-x))


def _split3(v):
    h1 = v.astype(BF16)
    r1 = v - h1.astype(F32)
    h2 = r1.astype(BF16)
    h3 = (r1 - h2.astype(F32)).astype(BF16)
    return h1, h2, h3


def _dot(a, b, ca=1, cb=0):
    return lax.dot_general(a, b, (((ca,), (cb,)), ((), ())), preferred_element_type=F32)


def _dot_sel_r(v, sel, cb=0):
    h1, h2, h3 = _split3(v)
    return _dot(h1, sel, 1, cb) + _dot(h2, sel, 1, cb) + _dot(h3, sel, 1, cb)


def _dot_sel_l(sel, v, ca=1):
    h1, h2, h3 = _split3(v)
    return _dot(sel, h1, ca, 0) + _dot(sel, h2, ca, 0) + _dot(sel, h3, ca, 0)


def _mm(a, b, *, ta=False, tb=False, tm, tn, tk, out_dtype=F32, nsplit=1, extra_bf16=False, name):
    K, M = (a.shape if ta else a.shape[::-1])
    N = b.shape[0] if tb else b.shape[1]
    assert M % tm == 0 and N % tn == 0 and K % tk == 0, (name, M, N, K, tm, tn, tk)
    nm, nn, nk = M // tm, N // tn, K // tk
    assert nn % nsplit == 0
    per = nn // nsplit
    a_spec = (pl.BlockSpec((tk, tm), lambda i, j, k: (k, i)) if ta
              else pl.BlockSpec((tm, tk), lambda i, j, k: (i, k)))
    b_spec = (pl.BlockSpec((tn, tk), lambda i, j, k: (j, k)) if tb
              else pl.BlockSpec((tk, tn), lambda i, j, k: (k, j)))
    o_spec = pl.BlockSpec((None, tm, tn), lambda i, j, k: (j // per, i, j % per))
    n_out = 2 if extra_bf16 else 1
    ca, cb = (0 if ta else 1), (1 if tb else 0)

    def body(a_ref, b_ref, *rest):
        outs = rest[:n_out]
        p = _dot(a_ref[...].astype(BF16), b_ref[...].astype(BF16), ca, cb)

        def emit(val):
            outs[0][...] = val.astype(out_dtype)
            if extra_bf16:
                outs[1][...] = val.astype(BF16)

        if nk == 1:
            emit(p)
        else:
            acc = rest[n_out]
            k = pl.program_id(2)

            @pl.when(k == 0)
            def _():
                acc[...] = p

            @pl.when(k > 0)
            def _():
                acc[...] += p

            @pl.when(k == nk - 1)
            def _():
                emit(acc[...])

    shp = (nsplit, M, N // nsplit)
    out_shape = [jax.ShapeDtypeStruct(shp, out_dtype)]
    out_specs = [o_spec]
    if extra_bf16:
        out_shape.append(jax.ShapeDtypeStruct(shp, BF16))
        out_specs.append(o_spec)
    res = pl.pallas_call(
        body, name=name, grid=(nm, nn, nk), in_specs=[a_spec, b_spec], out_specs=out_specs,
        out_shape=out_shape, scratch_shapes=([pltpu.VMEM((tm, tn), F32)] if nk > 1 else []),
        compiler_params=_cparams(("parallel", "parallel", "arbitrary")),
    )(a, b)
    return res if extra_bf16 else res[0]


def _rms_stats(x):
    r = lax.rsqrt(jnp.mean(x * x, axis=-1, keepdims=True) + EPS)
    return r, x * r


def _rms_bwd(x, g, dy):
    r, xh = _rms_stats(x)
    dxh = dy * g
    dx = r * (dxh - xh * jnp.mean(dxh * xh, axis=-1, keepdims=True))
    return dx, jnp.sum(dy * xh, axis=0, keepdims=True)


def _row_spec(tr, w, col=0):
    return pl.BlockSpec((tr, w), lambda i: (i, col))


def _vec_spec(w):
    return pl.BlockSpec((1, w), lambda i: (0, 0))


def _acc_rows(ref, val, i):
    @pl.when(i == 0)
    def _():
        ref[...] = val

    @pl.when(i > 0)
    def _():
        ref[...] += val


def _rms_fwd_call(x, g, name):
    lp, w = x.shape
    tr = _pick(lp, [384, 128])

    def body(x_ref, g_ref, o_ref):
        _, xh = _rms_stats(x_ref[...])
        o_ref[...] = (xh * g_ref[...]).astype(BF16)

    return pl.pallas_call(
        body, name=name, grid=(lp // tr,), in_specs=[_row_spec(tr, w), _vec_spec(w)],
        out_specs=_row_spec(tr, w), out_shape=jax.ShapeDtypeStruct((lp, w), BF16),
        compiler_params=_cparams(("parallel",)))(x, g)


def _mid_fwd_call(h0, mix, g_post, g_pre2):
    lp, w = h0.shape
    tr = _pick(lp, [384, 128])

    def body(h0_ref, mix_ref, gp_ref, g2_ref, h1_ref, xn_ref):
        _, mh = _rms_stats(mix_ref[...])
        h1 = h0_ref[...] + mh * gp_ref[...]
        h1_ref[...] = h1
        _, hh = _rms_stats(h1)
        xn_ref[...] = (hh * g2_ref[...]).astype(BF16)

    return pl.pallas_call(
        body, name="mid_fwd", grid=(lp // tr,),
        in_specs=[_row_spec(tr, w), _row_spec(tr, w), _vec_spec(w), _vec_spec(w)],
        out_specs=[_row_spec(tr, w), _row_spec(tr, w)],
        out_shape=[jax.ShapeDtypeStruct((lp, w), F32), jax.ShapeDtypeStruct((lp, w), BF16)],
        compiler_params=_cparams(("parallel",)))(h0, mix, g_post, g_pre2)


def _final_call(h1, f, g_post, target):
    lp, w = h1.shape
    tr = BLK
    nb = lp // tr

    def body(h1_ref, f_ref, g_ref, t_ref, loss_ref, df_ref, dh_ref, dg_ref):
        i = pl.program_id(0)
        fv = f_ref[...]
        g = g_ref[...]
        _, fh = _rms_stats(fv)
        h2 = h1_ref[...] + fh * g
        diff = jnp.where(i > 0, h2 - t_ref[...], 0.0)
        part = 0.5 * jnp.sum(diff * diff, axis=0, keepdims=True) * (1.0 / w)
        _acc_rows(loss_ref, part, i)
        dh = diff * (1.0 / w)
        dh_ref[...] = dh
        df, dg = _rms_bwd(fv, g, dh)
        df_ref[...] = df.astype(BF16)
        _acc_rows(dg_ref, dg, i)

    t_spec = pl.BlockSpec((tr, w), lambda i: (jnp.maximum(i - 1, 0), 0))
    return pl.pallas_call(
        body, name="final_fwd_bwd", grid=(nb,),
        in_specs=[_row_spec(tr, w), _row_spec(tr, w), _vec_spec(w), t_spec],
        out_specs=[_vec_spec(w), _row_spec(tr, w), _row_spec(tr, w), _vec_spec(w)],
        out_shape=[jax.ShapeDtypeStruct((1, w), F32), jax.ShapeDtypeStruct((lp, w), BF16),
                   jax.ShapeDtypeStruct((lp, w), F32), jax.ShapeDtypeStruct((1, w), F32)],
        compiler_params=_cparams(("arbitrary",)))(h1, f, g_post, target)


def _mid_bwd_call(dh2, h1, dxn2, mix, g_pre2, g_post):
    lp, w = h1.shape
    tr = _pick(lp, [384, 128])

    def body(dh2_ref, h1_ref, dxn_ref, mix_ref, g2_ref, gp_ref, dh1_ref, dmix_ref, dg2_ref, dgp_ref):
        i = pl.program_id(0)
        live = (i * tr + _iota((tr, 1), 0)) >= PAD
        dx, dg2 = _rms_bwd(h1_ref[...], g2_ref[...], dxn_ref[...])
        dh1 = jnp.where(live, dh2_ref[...] + dx, 0.0)
        dh1_ref[...] = dh1
        dmix, dgp = _rms_bwd(mix_ref[...], gp_ref[...], dh1)
        dmix_ref[...] = jnp.where(live, dmix, 0.0).astype(BF16)
        _acc_rows(dg2_ref, dg2, i)
        _acc_rows(dgp_ref, dgp, i)

    rs = _row_spec(tr, w)
    return pl.pallas_call(
        body, name="mid_bwd", grid=(lp // tr,),
        in_specs=[rs, rs, rs, rs, _vec_spec(w), _vec_spec(w)],
        out_specs=[rs, rs, _vec_spec(w), _vec_spec(w)],
        out_shape=[jax.ShapeDtypeStruct((lp, w), F32), jax.ShapeDtypeStruct((lp, w), BF16),
                   jax.ShapeDtypeStruct((1, w), F32), jax.ShapeDtypeStruct((1, w), F32)],
        compiler_params=_cparams(("arbitrary",)))(dh2, h1, dxn2, mix, g_pre2, g_post)


def _norm_bwd_call(x, g, dy_arr, dy_col, name, res=None):
    lp, w = x.shape
    tr = _pick(lp, [384, 128])
    has_res = res is not None

    def body(x_ref, g_ref, dy_ref, *rest):
        i = pl.program_id(0)
        live = (i * tr + _iota((tr, 1), 0)) >= PAD
        dx, dg = _rms_bwd(x_ref[...], g_ref[...], dy_ref[...])
        if has_res:
            dx = dx + rest[0][...]
        out_ref, dg_ref = rest[-2], rest[-1]
        out_ref[...] = jnp.where(live, dx, 0.0)
        _acc_rows(dg_ref, dg, i)

    rs = _row_spec(tr, w)
    ins = [rs, _vec_spec(w), _row_spec(tr, w, dy_col)] + ([rs] if has_res else [])
    args = [x, g, dy_arr] + ([res] if has_res else [])
    return pl.pallas_call(
        body, name=name, grid=(lp // tr,), in_specs=ins, out_specs=[rs, _vec_spec(w)],
        out_shape=[jax.ShapeDtypeStruct((lp, w), F32), jax.ShapeDtypeStruct((1, w), F32)],
        compiler_params=_cparams(("arbitrary",)))(*args)


def _shift_down(cur, prev_tail, s, rows):
    if s == 0:
        return cur
    prev = jnp.tile(prev_tail, (BLK // 8, 1))
    return jnp.where(rows >= s, pltpu.roll(cur, s, 0), pltpu.roll(prev, s, 0))


def _shift_up(cur, next_head, s, rows):
    if s == 0:
        return cur
    nxt = jnp.tile(next_head, (BLK // 8, 1))
    return jnp.where(rows < BLK - s, pltpu.roll(cur, BLK - s, 0), pltpu.roll(nxt, BLK - s, 0))


def _gelu_tanh(x):
    c = math.sqrt(2.0 / math.pi)
    t = jnp.tanh(c * (x + 0.044715 * x * x * x))
    return 0.5 * x * (1.0 + t), t


def _conv_fwd_call(src, col0, width, cw, w8, b, taps, *, gate_src=None, gate_col0=0, name):
    lp = src.shape[0]
    nb, nc = lp // BLK, width // cw
    cb0 = col0 // cw
    ffn = gate_src is not None

    def body(x_ref, w_ref, b_ref, *rest):
        if ffn:
            u_ref, y_ref, a_ref, tail = rest
        else:
            y_ref, a_ref, tail = rest
        i = pl.program_id(1)

        @pl.when(i == 0)
        def _():
            tail[...] = jnp.zeros_like(tail)

        cur = x_ref[...]
        rows = _iota((BLK, cw), 0)
        y = b_ref[...] + w_ref[taps - 1:taps, :] * cur
        pt = tail[...]
        for s in range(1, taps):
            y = y + w_ref[taps - 1 - s:taps - s, :] * _shift_down(cur, pt, s, rows)
        tail[...] = cur[BLK - 8:, :]
        y_ref[...] = y
        if ffn:
            ge, _ = _gelu_tanh(y)
            a_ref[...] = (ge * u_ref[...]).astype(BF16)
        else:
            live = (i * BLK + rows) >= PAD
            a_ref[...] = jnp.where(live, y * _sigmoid(y), 0.0)

    blk = lambda c0: pl.BlockSpec((BLK, cw), lambda j, i: (i, c0 + j))
    ins = [blk(cb0), pl.BlockSpec((8, cw), lambda j, i: (0, j)), pl.BlockSpec((1, cw), lambda j, i: (0, j))]
    args = [src, w8, b]
    if ffn:
        ins.append(blk(gate_col0 // cw))
        args.append(gate_src)
    return pl.pallas_call(
        body, name=name, grid=(nc, nb), in_specs=ins, out_specs=[blk(0), blk(0)],
        out_shape=[jax.ShapeDtypeStruct((lp, width), F32),
                   jax.ShapeDtypeStruct((lp, width), BF16 if ffn else F32)],
        scratch_shapes=[pltpu.VMEM((8, cw), F32)],
        compiler_params=_cparams(("parallel", "arbitrary")))(*args)


def _conv_bwd_call(src, col0, width, cw, w8, taps, ypre, dact, *, gate_src=None, gate_col0=0, name):
    lp = src.shape[0]
    nb, nc = lp // BLK, width // cw
    cb0 = col0 // cw
    ffn = gate_src is not None

    def body(x_ref, w_ref, y_ref, d_ref, *rest):
        if ffn:
            u_ref, dx_ref, du_ref, dw_ref, db_ref, head = rest
        else:
            dx_ref, dw_ref, db_ref, head = rest
        step = pl.program_id(1)
        i = nb - 1 - step

        @pl.when(step == 0)
        def _():
            head[...] = jnp.zeros_like(head)

        rows = _iota((BLK, cw), 0)
        live = (i * BLK + rows) >= PAD
        y = y_ref[...]
        d = d_ref[...]
        if ffn:
            ge, t = _gelu_tanh(y)
            c = math.sqrt(2.0 / math.pi)
            dge = 0.5 * (1.0 + t) + 0.5 * y * (1.0 - t * t) * c * (1.0 + 3.0 * 0.044715 * y * y)
            u = u_ref[...]
            du_ref[...] = jnp.where(live, d * ge, 0.0).astype(BF16)
            dy = jnp.where(live, d * u * dge, 0.0)
        else:
            sg = _sigmoid(y)
            dy = jnp.where(live, d * sg * (1.0 + y * (1.0 - sg)), 0.0)
        x = x_ref[...]
        nh = head[...]
        dx = jnp.zeros_like(dy)
        dws = []
        for s in range(taps):
            sh = _shift_up(dy, nh, s, rows)
            dx = dx + w_ref[taps - 1 - s:taps - s, :] * sh
            dws.append(jnp.sum(x * sh, axis=0, keepdims=True))
        head[...] = dy[:8, :]
        dx_ref[...] = jnp.where(live, dx, 0.0).astype(BF16)
        dw = jnp.concatenate([dws[taps - 1 - k] for k in range(taps)]
                             + [jnp.zeros((8 - taps, cw), F32)], axis=0)
        _acc_rows(dw_ref, dw, step)
        _acc_rows(db_ref, jnp.sum(dy, axis=0, keepdims=True), step)

    blk = lambda c0: pl.BlockSpec((BLK, cw), lambda j, s: (nb - 1 - s, c0 + j))
    ins = [blk(cb0), pl.BlockSpec((8, cw), lambda j, s: (0, j)), blk(0), blk(0)]
    args = [src, w8, ypre, dact]
    outs = [blk(0)]
    oshape = [jax.ShapeDtypeStruct((lp, width), BF16)]
    if ffn:
        ins.append(blk(gate_col0 // cw))
        args.append(gate_src)
        outs.append(blk(0))
        oshape.append(jax.ShapeDtypeStruct((lp, width), BF16))
    outs += [pl.BlockSpec((8, cw), lambda j, s: (0, j)), pl.BlockSpec((1, cw), lambda j, s: (0, j))]
    oshape += [jax.ShapeDtypeStruct((8, width), F32), jax.ShapeDtypeStruct((1, width), F32)]
    return pl.pallas_call(
        body, name=name, grid=(nc, nb), in_specs=ins, out_specs=outs, out_shape=oshape,
        scratch_shapes=[pltpu.VMEM((8, cw), F32)],
        compiler_params=_cparams(("parallel", "arbitrary")))(*args)


SB_GROUP = 4


def _sb_scores(qm_h, kb):
    z = _dot(qm_h, kb, 1, 1)
    sp = jnp.maximum(z, 0.0) + jnp.log(1.0 + jnp.exp(-jnp.abs(z)))
    return z - sp, -sp


def _sb_valid(i, off, width):
    kpos = off + _iota((BLK, width), 1)
    qpos = i * BLK + _iota((BLK, width), 0)
    return (kpos < qpos) & (kpos >= PAD)


def _dot_tri1(v, tri2):
    r = _dot(v.astype(BF16), tri2[:BLK])
    return r[:, :BLK], r[:, BLK:]


def _sb_groups(i):
    edge = i // SB_GROUP
    return edge, pl.multiple_of(edge * (SB_GROUP * BLK), BLK)


def _tri2(cond):
    t = jnp.concatenate([cond.astype(BF16), jnp.ones((BLK, BLK), BF16)], axis=1)
    return jnp.concatenate([t, t], axis=0)


def _dot_tri(v, tri2):
    hi = v.astype(BF16)
    lo = (v - hi.astype(F32)).astype(BF16)
    r = _dot(jnp.concatenate([hi, lo], axis=1), tri2)
    return r[:, :BLK], r[:, BLK:]


def _sb_fwd_call(proj):
    lp = proj.shape[0]
    nb = lp // BLK
    assert (nb - 1) % SB_GROUP == 0
    scale = 1.0 / math.sqrt(HEAD_DIM)

    def body(q_ref, k_ref, v_ref, o_ref, tl_ref):
        i = pl.program_id(1)
        lane = _iota((2 * BLK, BLK), 1)
        row = _iota((2 * BLK, BLK), 0)
        first = row < BLK
        qrow = row & (BLK - 1)
        q = q_ref[...] * scale
        q2 = jnp.concatenate([q, q], axis=0)
        qm = jnp.where(first == (lane < HEAD_DIM), q2, 0.0).astype(BF16)
        tri = _tri2(_iota((BLK, BLK), 0) > _iota((BLK, BLK), 1))

        def chunk(off, nsub, last_valid, carry):
            width = nsub * BLK
            sls = [slice(b * BLK, (b + 1) * BLK) for b in range(nsub)]
            kb = k_ref[pl.ds(off, width), :].astype(BF16)
            vb = v_ref[pl.ds(off, width), :].astype(BF16)
            lb, lk = _sb_scores(qm, kb)
            lks = [lk[:, sl] for sl in sls]
            first_valid = (off + lane) >= PAD
            lks[0] = jnp.where(first_valid, lks[0], 0.0)
            if last_valid is not None:
                lks[-1] = jnp.where(last_valid, lks[-1], 0.0)
            afters = [_dot_tri(lks[b], tri) for b in range(nsub)]
            run, acc = carry
            ws = [None] * nsub
            for b in reversed(range(nsub)):
                wb = jnp.exp(lb[:, sls[b]] + afters[b][0] + run)
                if b == 0:
                    wb = jnp.where(first_valid, wb, 0.0)
                if last_valid is not None and b == nsub - 1:
                    wb = jnp.where(last_valid, wb, 0.0)
                ws[b] = wb.astype(BF16)
                run = run + afters[b][1]
            w = ws[0] if nsub == 1 else jnp.concatenate(ws, axis=1)
            return run, acc + _dot(w, vb)

        edge, edge_off = _sb_groups(i)
        diag = lane < qrow
        zero = jnp.zeros((2 * BLK, BLK), F32)
        upto = [functools.partial(chunk, edge_off, r, diag) for r in range(1, SB_GROUP + 1)]
        carry = lax.switch(i - edge * SB_GROUP, upto, (zero, zero))

        def interior(t, carry):
            off = pl.multiple_of((edge - 2 - 2 * t) * (SB_GROUP * BLK), BLK)
            return chunk(off, 2 * SB_GROUP, None, carry)

        carry = lax.fori_loop(0, edge // 2, interior, carry)
        run, acc = lax.cond(edge % 2 == 1, lambda cr: chunk(0, SB_GROUP, None, cr), lambda cr: cr, carry)
        low = lane[:BLK] < HEAD_DIM
        o_ref[...] = jnp.where(low, acc[:BLK], acc[BLK:])
        tl_ref[...] = jnp.where(low, run[:BLK], run[BLK:])

    qc, kc, vc = C_Q // BLK, C_K // BLK, C_V // BLK
    blk = pl.BlockSpec((BLK, BLK), lambda p, i: (i, p))
    return pl.pallas_call(
        body, name="sb_fwd", grid=(N_HEADS // 2, nb),
        in_specs=[pl.BlockSpec((BLK, BLK), lambda p, i: (i, qc + p)),
                  pl.BlockSpec((lp, BLK), lambda p, i: (0, kc + p)),
                  pl.BlockSpec((lp, BLK), lambda p, i: (0, vc + p))],
        out_specs=[blk, blk],
        out_shape=[jax.ShapeDtypeStruct((lp, N_HEADS * HEAD_DIM), F32)] * 2,
        compiler_params=_cparams(("parallel", "arbitrary")))(proj, proj, proj)


def _sb_bwd_call(proj, tl, do):
    lp = proj.shape[0]
    nb = lp // BLK
    assert (nb - 1) % SB_GROUP == 0
    scale = 1.0 / math.sqrt(HEAD_DIM)

    def body(q_ref, k_ref, v_ref, tl_ref, do_ref, dq_ref, dk_ref, dv_ref, dk_acc, dv_acc):
        i = pl.program_id(1)

        @pl.when(i == 0)
        def _():
            dk_acc[...] = jnp.zeros_like(dk_acc)
            dv_acc[...] = jnp.zeros_like(dv_acc)

        lane = _iota((2 * BLK, BLK), 1)
        row = _iota((2 * BLK, BLK), 0)
        qrow = row & (BLK - 1)
        mine = (row < BLK) == (lane < HEAD_DIM)
        q = q_ref[...] * scale
        dov = do_ref[...]
        qm = jnp.where(mine, jnp.concatenate([q, q], axis=0), 0.0).astype(BF16)
        dom = jnp.where(mine, jnp.concatenate([dov, dov], axis=0), 0.0).astype(BF16)
        tlv = tl_ref[...]
        tot = jnp.concatenate([jnp.broadcast_to(tlv[:, 0:1], (BLK, BLK)),
                               jnp.broadcast_to(tlv[:, HEAD_DIM:HEAD_DIM + 1], (BLK, BLK))], axis=0)
        r1, l1 = _iota((BLK, BLK), 0), _iota((BLK, BLK), 1)
        tri_in = _tri2(r1 <= l1)
        tri_ex = _tri2(r1 < l1)

        def chunk(off, nsub, last_valid, carry):
            width = nsub * BLK
            sls = [slice(b * BLK, (b + 1) * BLK) for b in range(nsub)]
            cat = lambda parts: parts[0] if nsub == 1 else jnp.concatenate(parts, axis=1)
            mask_last = lambda b: last_valid is not None and b == nsub - 1
            kb = k_ref[pl.ds(off, width), :].astype(BF16)
            vb = v_ref[pl.ds(off, width), :].astype(BF16)
            lb, lk = _sb_scores(qm, kb)
            dw = _dot(dom, vb, 1, 1)
            lks = [lk[:, sl] for sl in sls]
            first_valid = (off + lane) >= PAD
            lks[0] = jnp.where(first_valid, lks[0], 0.0)
            if last_valid is not None:
                lks[-1] = jnp.where(last_valid, lks[-1], 0.0)
            pins = [_dot_tri(lks[b], tri_in) for b in range(nsub)]
            run, gsum, dq = carry
            ws, gs = [], []
            for b in range(nsub):
                wb = jnp.exp(lb[:, sls[b]] + (tot - run - pins[b][0]))
                if b == 0:
                    wb = jnp.where(first_valid, wb, 0.0)
                if mask_last(b):
                    wb = jnp.where(last_valid, wb, 0.0)
                ws.append(wb.astype(BF16))
                gs.append(wb * dw[:, sls[b]])
                run = run + pins[b][1]
            gexs = [_dot_tri1(gs[b], tri_ex) for b in range(nsub)]
            beta = jnp.exp(lb)
            parts = []
            for b in range(nsub):
                bt = beta[:, sls[b]]
                dzb = gs[b] * (1.0 - bt) - (gsum + gexs[b][0]) * bt
                if b == 0:
                    dzb = jnp.where(first_valid, dzb, 0.0)
                if mask_last(b):
                    dzb = jnp.where(last_valid, dzb, 0.0)
                parts.append(dzb.astype(BF16))
                gsum = gsum + gexs[b][1]
            dz, w = cat(parts), cat(ws)
            dk_acc[pl.ds(off, width), :] += _dot(dz, qm, 0, 0)
            dv_acc[pl.ds(off, width), :] += _dot(w, dom, 0, 0)
            return run, gsum, dq + _dot(dz, kb)

        edge, edge_off = _sb_groups(i)
        diag = lane < qrow
        zero = jnp.zeros((2 * BLK, BLK), F32)
        odd = edge % 2
        carry = lax.cond(odd == 1, lambda cr: chunk(0, SB_GROUP, None, cr), lambda cr: cr, (zero, zero, zero))

        def interior(t, carry):
            off = pl.multiple_of((odd + 2 * t) * (SB_GROUP * BLK), BLK)
            return chunk(off, 2 * SB_GROUP, None, carry)

        carry = lax.fori_loop(0, edge // 2, interior, carry)
        upto = [functools.partial(chunk, edge_off, r, diag) for r in range(1, SB_GROUP + 1)]
        dq = lax.switch(i - edge * SB_GROUP, upto, carry)[2]
        dq_ref[...] = (jnp.where(lane[:BLK] < HEAD_DIM, dq[:BLK], dq[BLK:]) * scale).astype(BF16)

        @pl.when(i == nb - 1)
        def _():
            dk_ref[...] = dk_acc[...].astype(BF16)
            dv_ref[...] = dv_acc[...].astype(BF16)

    qc, kc, vc = C_Q // BLK, C_K // BLK, C_V // BLK
    blk = pl.BlockSpec((BLK, BLK), lambda p, i: (i, p))
    full = pl.BlockSpec((lp, BLK), lambda p, i: (0, p))
    w = N_HEADS * HEAD_DIM
    return pl.pallas_call(
        body, name="sb_bwd", grid=(N_HEADS // 2, nb),
        in_specs=[pl.BlockSpec((BLK, BLK), lambda p, i: (i, qc + p)),
                  pl.BlockSpec((lp, BLK), lambda p, i: (0, kc + p)),
                  pl.BlockSpec((lp, BLK), lambda p, i: (0, vc + p)),
                  blk, blk],
        out_specs=[blk, full, full],
        out_shape=[jax.ShapeDtypeStruct((lp, w), BF16)] * 3,
        scratch_shapes=[pltpu.VMEM((lp, BLK), F32), pltpu.VMEM((lp, BLK), F32)],
        compiler_params=_cparams(("parallel", "arbitrary")))(proj, proj, proj, tl, do)


def _log1p(e):
    u = 1.0 + e
    return jnp.where(u == 1.0, e, jnp.log(u) * e / jnp.where(u == 1.0, 1.0, u - 1.0))


def _ssd_common(c, dtr, bias, alog):
    row = _iota((BLK, BLK), 0)
    lane = _iota((BLK, BLK), 1)
    live = ((c * BLK + row) >= PAD) & (lane < N_HEADS)
    pre = dtr + bias
    dt = jnp.where(live, jnp.maximum(pre, 0.0) + _log1p(jnp.exp(-jnp.abs(pre))), 0.0)
    a_neg = -jnp.exp(alog)
    a = dt * a_neg
    t_in = (lane <= row).astype(BF16)
    cs = _dot_sel_l(t_in, a)
    cs_t = cs.T
    cs_end = cs[BLK - 1:BLK, :]
    e = jnp.exp(cs)
    f = jnp.exp(cs_end - cs)
    xp = ((_iota((BLK, SSD_INNER), 1) // HEAD_DIM) == _iota((BLK, SSD_INNER), 0)).astype(BF16)
    xp_t = ((_iota((SSD_INNER, BLK), 0) // HEAD_DIM) == _iota((SSD_INNER, BLK), 1)).astype(BF16)
    decay_col = _dot_sel_l(xp_t, jnp.exp(cs_t))[:, BLK - 1:BLK]
    return dict(live=live, pre=pre, dt=dt, a_neg=a_neg, cs=cs, cs_t=cs_t, e=e, f=f, xp=xp, xp_t=xp_t,
                decay_col=decay_col, row=row, lane=lane,
                dt_x=_dot_sel_r(dt, xp), e_x=_dot_sel_r(e, xp), f_x=_dot_sel_r(f, xp))


def _ssd_ldec(q, h):
    diff = q["cs"][:, h:h + 1] - q["cs_t"][h:h + 1, :]
    causal = q["row"] >= q["lane"]
    return jnp.where(causal, jnp.exp(jnp.where(causal, diff, 0.0)), 0.0)


def _ssd_fwd_call(xbc, proj, bias, alog, d_x, norm_g):
    lp = xbc.shape[0]
    nb = lp // BLK
    gw = SSD_INNER // SSD_GROUPS
    ppg = gw // BLK

    def body(xbc_ref, dtr_ref, z_ref, bias_ref, alog_ref, dx_ref, ng_ref, yb_ref, ypre_ref, sprev_ref, s_ref):
        c = pl.program_id(0)

        @pl.when(c == 0)
        def _():
            s_ref[...] = jnp.zeros_like(s_ref)

        q = _ssd_common(c, dtr_ref[...], bias_ref[...], alog_ref[...])
        x = xbc_ref[:, 0:SSD_INNER]
        xd = x * q["dt_x"]
        low = q["lane"] < HEAD_DIM
        s_old = s_ref[...]
        sprev_ref[...] = s_old
        xdf = (xd * q["f_x"]).astype(BF16)
        for g in range(SSD_GROUPS):
            bg = xbc_ref[:, SSD_INNER + g * SSD_STATE:SSD_INNER + (g + 1) * SSD_STATE].astype(BF16)
            cg = xbc_ref[:, SSD_INNER + (SSD_GROUPS + g) * SSD_STATE:
                         SSD_INNER + (SSD_GROUPS + g + 1) * SSD_STATE].astype(BF16)
            cb = _dot(cg, bg, 1, 1)
            gs = slice(g * gw, (g + 1) * gw)
            y_off = _dot(cg, s_old[gs, :].astype(BF16), 1, 1) * q["e_x"][:, gs]
            s_ref[gs, :] = s_old[gs, :] * q["decay_col"][gs, :] + _dot(xdf[:, gs], bg, 0, 0)
            for pr in range(ppg):
                cols = slice(g * gw + pr * BLK, g * gw + (pr + 1) * BLK)
                xd_p = xd[:, cols]
                acc = y_off[:, pr * BLK:(pr + 1) * BLK]
                for hh in range(2):
                    h = (g * gw + pr * BLK) // HEAD_DIM + hh
                    m = (cb * _ssd_ldec(q, h)).astype(BF16)
                    xm = jnp.where(low, xd_p, 0.0) if hh == 0 else jnp.where(low, 0.0, xd_p)
                    acc = acc + _dot(m, xm.astype(BF16))
                ypre_ref[:, cols] = acc
        ypre = ypre_ref[...] + x * dx_ref[...]
        ypre_ref[...] = ypre
        z = z_ref[...]
        yg = ypre * (z * _sigmoid(z))
        _, yh = _rms_stats(yg)
        yb_ref[...] = (yh * ng_ref[...]).astype(BF16)

    row = lambda w, col: pl.BlockSpec((BLK, w), lambda c: (c, col))
    vec = lambda w: pl.BlockSpec((1, w), lambda c: (0, 0))
    return pl.pallas_call(
        body, name="ssd_fwd", grid=(nb,),
        in_specs=[row(XBC, 0), row(BLK, C_DT // BLK), row(SSD_INNER, 0), vec(BLK), vec(BLK),
                  vec(SSD_INNER), vec(SSD_INNER)],
        out_specs=[row(SSD_INNER, 0), row(SSD_INNER, 0),
                   pl.BlockSpec((None, SSD_INNER, SSD_STATE), lambda c: (c, 0, 0))],
        out_shape=[jax.ShapeDtypeStruct((lp, SSD_INNER), BF16), jax.ShapeDtypeStruct((lp, SSD_INNER), F32),
                   jax.ShapeDtypeStruct((nb, SSD_INNER, SSD_STATE), F32)],
        scratch_shapes=[pltpu.VMEM((SSD_INNER, SSD_STATE), F32)],
        compiler_params=_cparams(("arbitrary",)))(xbc, proj, proj, bias, alog, d_x, norm_g)


def _ssd_bwd_call(dycat, ypre, xbc, proj, sprev, bias, alog, d_x, norm_g):
    lp = xbc.shape[0]
    nb = lp // BLK
    gw = SSD_INNER // SSD_GROUPS
    ppg = gw // BLK

    def body(dy_ref, ypre_ref, xbc_ref, dtr_ref, z_ref, sp_ref, bias_ref, alog_ref, dxp_ref, ng_ref,
             dz_ref, dxbc_ref, ddt_ref, dng_ref, dd_ref, dal_ref, dbi_ref, ds_ref, dxd_ref):
        step = pl.program_id(0)
        c = nb - 1 - step

        @pl.when(step == 0)
        def _():
            ds_ref[...] = jnp.zeros_like(ds_ref)

        q = _ssd_common(c, dtr_ref[...], bias_ref[...], alog_ref[...])
        row, lane = q["row"], q["lane"]
        low = lane < HEAD_DIM
        rowlive = ((c * BLK + _iota((BLK, 1), 0)) >= PAD)
        x = xbc_ref[:, 0:SSD_INNER]
        xd = x * q["dt_x"]
        z = z_ref[...]
        sz = _sigmoid(z)
        silu = z * sz
        ypre = ypre_ref[...]
        dyg, dng = _rms_bwd(ypre * silu, ng_ref[...], dy_ref[...])
        _acc_rows(dng_ref, dng, step)
        dyp = dyg * silu
        dz_ref[...] = jnp.where(rowlive, dyg * ypre * (sz * (1.0 + z * (1.0 - sz))), 0.0).astype(BF16)
        _acc_rows(dd_ref, jnp.sum(dyp * x, axis=0, keepdims=True), step)
        dye = dyp * q["e_x"]
        xdf = xd * q["f_x"]
        s_prev = sp_ref[...]
        ds_old = ds_ref[...]
        qrow = jnp.zeros((BLK, BLK), F32)
        qcol_t = jnp.zeros((BLK, BLK), F32)
        red_e = []
        red_f = []
        for g in range(SSD_GROUPS):
            gs = slice(g * gw, (g + 1) * gw)
            bsl = slice(SSD_INNER + g * SSD_STATE, SSD_INNER + (g + 1) * SSD_STATE)
            csl = slice(SSD_INNER + (SSD_GROUPS + g) * SSD_STATE, SSD_INNER + (SSD_GROUPS + g + 1) * SSD_STATE)
            bg = xbc_ref[:, bsl].astype(BF16)
            cg = xbc_ref[:, csl].astype(BF16)
            sg = s_prev[gs, :].astype(BF16)
            dsg = ds_old[gs, :].astype(BF16)
            cb = _dot(cg, bg, 1, 1)
            bds = _dot(bg, dsg, 1, 1)
            y_off = _dot(cg, sg, 1, 1) * q["e_x"][:, gs]
            red_e.append(dyp[:, gs] * y_off)
            red_f.append(xd[:, gs] * bds * q["f_x"][:, gs])
            dc = _dot(dye[:, gs].astype(BF16), sg)
            db = _dot(xdf[:, gs].astype(BF16), dsg)
            ds_ref[gs, :] = ds_old[gs, :] * q["decay_col"][gs, :] + _dot(dye[:, gs].astype(BF16), cg, 0, 0)
            dcb = jnp.zeros((BLK, BLK), F32)
            for pr in range(ppg):
                cols = slice(g * gw + pr * BLK, g * gw + (pr + 1) * BLK)
                xd_p = xd[:, cols].astype(BF16)
                dy_p = dyp[:, cols]
                acc = q["f_x"][:, cols] * bds[:, pr * BLK:(pr + 1) * BLK]
                for hh in range(2):
                    h = (g * gw + pr * BLK) // HEAD_DIM + hh
                    ld = _ssd_ldec(q, h)
                    m = cb * ld
                    dym = (jnp.where(low, dy_p, 0.0) if hh == 0 else jnp.where(low, 0.0, dy_p)).astype(BF16)
                    dm = jnp.where(row >= lane, _dot(dym, xd_p, 1, 1), 0.0)
                    acc = acc + _dot(m.astype(BF16), dym, 0, 0)
                    dcb = dcb + dm * ld
                    qq = dm * m
                    qrow = qrow + jnp.where(lane == h, jnp.sum(qq, axis=1, keepdims=True), 0.0)
                    qcol_t = qcol_t + jnp.where(row == h, jnp.sum(qq, axis=0, keepdims=True), 0.0)
                dxd_ref[:, cols] = acc
            dcbb = dcb.astype(BF16)
            dxbc_ref[:, bsl] = jnp.where(rowlive, db + _dot(dcbb, cg, 0, 0), 0.0)
            dxbc_ref[:, csl] = jnp.where(rowlive, dc + _dot(dcbb, bg), 0.0)
        dxd = dxd_ref[...]
        dxbc_ref[:, 0:SSD_INNER] = jnp.where(rowlive, dxd * q["dt_x"] + dyp * dxp_ref[...], 0.0)
        xp_t = q["xp_t"]
        fw = _dot_sel_r(jnp.concatenate(red_f, axis=1), xp_t)
        dcs = qrow - qcol_t.T + _dot_sel_r(jnp.concatenate(red_e, axis=1), xp_t) - fw
        end_f = jnp.sum(fw, axis=0, keepdims=True)
        sds = jnp.sum(ds_old * s_prev, axis=1, keepdims=True)
        per_head = _dot_sel_l(q["xp"], jnp.broadcast_to(sds, (SSD_INNER, BLK)))
        end_e = per_head.T[0:1, :] * jnp.exp(q["cs"][BLK - 1:BLK, :])
        dcs = dcs + jnp.where(row == BLK - 1, end_f + end_e, 0.0)
        t_up = (lane >= row).astype(BF16)
        da = _dot_sel_l(t_up, dcs)
        ddt = da * q["a_neg"] + _dot_sel_r(dxd * x, xp_t)
        _acc_rows(dal_ref, jnp.sum(da * q["dt"] * q["a_neg"], axis=0, keepdims=True), step)
        ddtr = jnp.where(q["live"], ddt * _sigmoid(q["pre"]), 0.0)
        ddt_ref[...] = ddtr.astype(BF16)
        _acc_rows(dbi_ref, jnp.sum(ddtr, axis=0, keepdims=True), step)

    row_s = lambda w, col: pl.BlockSpec((BLK, w), lambda s: (nb - 1 - s, col))
    vec = lambda w: pl.BlockSpec((1, w), lambda s: (0, 0))
    return pl.pallas_call(
        body, name="ssd_bwd", grid=(nb,),
        in_specs=[row_s(SSD_INNER, 0), row_s(SSD_INNER, 0), row_s(XBC, 0), row_s(BLK, C_DT // BLK),
                  row_s(SSD_INNER, 0), pl.BlockSpec((None, SSD_INNER, SSD_STATE), lambda s: (nb - 1 - s, 0, 0)),
                  vec(BLK), vec(BLK), vec(SSD_INNER), vec(SSD_INNER)],
        out_specs=[row_s(SSD_INNER, 0), row_s(XBC, 0), row_s(BLK, 0),
                   vec(SSD_INNER), vec(SSD_INNER), vec(BLK), vec(BLK)],
        out_shape=[jax.ShapeDtypeStruct((lp, SSD_INNER), BF16), jax.ShapeDtypeStruct((lp, XBC), F32),
                   jax.ShapeDtypeStruct((lp, BLK), BF16),
                   jax.ShapeDtypeStruct((1, SSD_INNER), F32), jax.ShapeDtypeStruct((1, SSD_INNER), F32),
                   jax.ShapeDtypeStruct((1, BLK), F32), jax.ShapeDtypeStruct((1, BLK), F32)],
        scratch_shapes=[pltpu.VMEM((SSD_INNER, SSD_STATE), F32), pltpu.VMEM((BLK, SSD_INNER), F32)],
        compiler_params=_cparams(("arbitrary",)))(dycat, ypre, xbc, proj, proj, sprev, bias, alog, d_x, norm_g)


def _pad_rows8(w):
    return jnp.pad(w, ((0, 8 - w.shape[0]), (0, 0)))


def _pad_lanes(v, n=BLK):
    return jnp.pad(v, ((0, 0), (0, n - v.shape[1])))


def _local_step(x, target, wt, late_weights, grad_ready):
    seq = x.shape[0]
    lp = seq + BLK
    tm = _pick(lp, [1408, 768, 384, 128])
    tkr = _pick(lp, [1408, 384, 128])
    h0 = jnp.concatenate([jnp.zeros((PAD, D_MODEL), F32), wt["meta"], x], axis=0)
    bias = _pad_lanes(wt["ssd_dt_bias"])
    alog = _pad_lanes(wt["ssd_a_log"])
    d_x = jnp.repeat(wt["ssd_d"], HEAD_DIM, axis=1)
    cw8 = _pad_rows8(wt["ssd_conv_w"])
    fw8 = _pad_rows8(wt["ffn_conv_w"])
    fcw = D_FF // 2

    xn1 = _rms_fwd_call(h0, wt["mix_pre_g"], "norm1")
    proj = _mm(xn1, wt["w_in"], tm=tm, tn=1152, tk=D_MODEL, name="mm_proj")[0]
    conv_pre, xbc = _conv_fwd_call(proj, C_XBC, XBC, 512, cw8, wt["ssd_conv_b"], 4, name="ssd_conv_fwd")
    y_ssd, ypre, sprev = _ssd_fwd_call(xbc, proj, bias, alog, d_x, wt["ssd_norm_g"])
    o, tl = _sb_fwd_call(proj)
    y_sb = _rms_fwd_call(o, wt["sb_norm_g"], "sb_norm")
    ycat = jnp.concatenate([y_ssd, y_sb], axis=1)
    w_out, w_up, w_down = late_weights(y_sb)
    mix = _mm(ycat, w_out, tm=tm, tn=1024, tk=2048, name="mm_mix")[0]
    h1, xn2 = _mid_fwd_call(h0, mix, wt["mix_post_g"], wt["ffn_pre_g"])
    gu = _mm(xn2, w_up, tm=tm, tn=1408, tk=D_MODEL, name="mm_up")[0]
    gpre, act = _conv_fwd_call(gu, 0, D_FF, fcw, fw8, wt["ffn_conv_b"], 3, gate_src=gu, gate_col0=D_FF,
                               name="ffn_conv_fwd")
    f = _mm(act, w_down, tm=tm, tn=1024, tk=1408, name="mm_down")[0]
    loss_row, df, dh2, dg_ffn_post = _final_call(h1, f, wt["ffn_post_g"], target)

    dact = _mm(df, w_down, tb=True, tm=tm, tn=1408, tk=D_MODEL, name="mm_dact")[0]
    dw_down, dw_down_b = _mm(act, df, ta=True, tm=1408, tn=1024, tk=tkr, extra_bf16=True, name="mm_dw_down")
    zero = grad_ready("w_down", dw_down.reshape(N_CHIPS, -1, D_MODEL), dw_down_b.reshape(N_CHIPS, -1, D_MODEL))
    dgate, dup, dfcw, dfcb = _conv_bwd_call(gu, 0, D_FF, fcw, fw8 + zero, 3, gpre, dact, gate_src=gu,
                                            gate_col0=D_FF, name="ffn_conv_bwd")
    dgu = jnp.concatenate([dgate, dup], axis=1)
    dxn2 = _mm(dgu, w_up, tb=True, tm=tm, tn=1024, tk=1408, name="mm_dxn2")[0]
    dw_up, dw_up_b = _mm(xn2, dgu, ta=True, tm=1024, tn=1408, tk=tkr, nsplit=N_CHIPS, extra_bf16=True,
                         name="mm_dw_up")
    zero = grad_ready("w_up", dw_up, dw_up_b)
    dh1, dmix, dg_ffn_pre, dg_mix_post = _mid_bwd_call(dh2, h1, dxn2, mix, wt["ffn_pre_g"] + zero,
                                                       wt["mix_post_g"])
    dycat = _mm(dmix, w_out, tb=True, tm=tm, tn=1024, tk=D_MODEL, name="mm_dycat")[0]
    dw_out, dw_out_b = _mm(ycat, dmix, ta=True, tm=1024, tn=1024, tk=tkr, extra_bf16=True, name="mm_dw_out")
    zero = grad_ready("w_out", dw_out.reshape(N_CHIPS, -1, D_MODEL), dw_out_b.reshape(N_CHIPS, -1, D_MODEL))
    do, dg_sb = _norm_bwd_call(o, wt["sb_norm_g"] + zero, dycat, 1, "sb_norm_bwd")
    dq, dk, dv = _sb_bwd_call(proj, tl, do)
    dz, dxbc_act, ddt, dg_ssd, dd_x, dalog, dbias = _ssd_bwd_call(
        dycat, ypre, xbc, proj, sprev, bias, alog, d_x, wt["ssd_norm_g"])
    dxbc, dcw, dcb = _conv_bwd_call(proj, C_XBC, XBC, 512, cw8, 4, conv_pre, dxbc_act, name="ssd_conv_bwd")
    dproj = jnp.concatenate([dz, dxbc, ddt, dq, dk, dv], axis=1)
    dw_in = _mm(xn1, dproj, ta=True, tm=1024, tn=1152, tk=tkr, name="mm_dw_in")[0]
    zero = grad_ready("w_in", dw_in, None)
    dxn1 = _mm(dproj, wt["w_in"], tb=True, tm=tm, tn=1024, tk=1152, name="mm_dxn1")[0]
    dh0, dg_pre = _norm_bwd_call(h0, wt["mix_pre_g"] + zero, dxn1, 0, "norm1_bwd", res=dh1)

    small = {
        "meta_tokens": dh0[PAD:BLK], "mix_pre_g": dg_pre, "ssd_conv_w": dcw[:4], "ssd_conv_b": dcb,
        "ssd_dt_bias": dbias[:, :N_HEADS], "ssd_a_log": dalog[:, :N_HEADS],
        "ssd_d": jnp.sum(dd_x.reshape(N_HEADS, HEAD_DIM), axis=1)[None],
        "ssd_norm_g": dg_ssd, "sb_norm_g": dg_sb, "mix_post_g": dg_mix_post, "ffn_pre_g": dg_ffn_pre,
        "ffn_conv_w": dfcw[:3], "ffn_conv_b": dfcb, "ffn_post_g": dg_ffn_post,
    }
    return loss_row, dh0[BLK:], small


def _adamw_call(w, g, m, v, name):
    rows, cols = w.shape
    tr = 256 if rows % 256 == 0 else (352 if rows % 352 == 0 else rows)
    c1 = 1.0 - ADAM_B1 ** ADAM_STEP
    c2 = 1.0 - ADAM_B2 ** ADAM_STEP

    def body(w_ref, g_ref, m_ref, v_ref, d_ref, mo_ref, vo_ref):
        gv = g_ref[...]
        m2 = ADAM_B1 * m_ref[...] + (1.0 - ADAM_B1) * gv
        v2 = ADAM_B2 * v_ref[...] + (1.0 - ADAM_B2) * (gv * gv)
        d_ref[...] = -ADAM_LR * ((m2 / c1) / (jnp.sqrt(v2 / c2) + ADAM_EPS) + ADAM_WD * w_ref[...])
        mo_ref[...] = m2
        vo_ref[...] = v2

    spec = pl.BlockSpec((tr, cols), lambda i: (i, 0))
    return pl.pallas_call(
        body, name=name, grid=(rows // tr,), in_specs=[spec] * 4, out_specs=[spec] * 3,
        out_shape=[jax.ShapeDtypeStruct((rows, cols), F32)] * 3,
        compiler_params=_cparams(("parallel",)))(w, g, m, v)


ANY = pl.BlockSpec(memory_space=pl.ANY)


def _place():
    x, y, c = lax.axis_index("x"), lax.axis_index("y"), lax.axis_index("c")
    chips = [(1 - x, y), (x, 1 - y), (1 - x, 1 - y)]
    return x, y, c, chips


def _half(c, h):
    return pl.ds(pl.multiple_of(c * h, 8), h)


def _allgather_call(shards):
    n = len(shards)

    def body(*refs):
        ins, outs = refs[:n], refs[n:2 * n]
        send_i, recv_i, send_d, recv_d = refs[2 * n:]
        x, y, c, chips = _place()
        me = 2 * x + y
        sends = []
        for a in range(n):
            h = shards[a].shape[0] // 2
            for j, chip in enumerate(chips):
                cp = pltpu.make_async_remote_copy(
                    src_ref=ins[a].at[_half(c, h)], dst_ref=outs[a].at[me, _half(c, h)],
                    send_sem=send_i.at[3 * a + j], recv_sem=recv_i.at[3 * a + j],
                    device_id=(*chip, c), device_id_type=MESH)
                cp.start()
                sends.append(cp)
        for a in range(n):
            h = shards[a].shape[0] // 2
            for j, chip in enumerate(chips):
                src = 2 * chip[0] + chip[1]
                landed = outs[a].at[src, _half(c, h)]
                pltpu.make_async_remote_copy(
                    src_ref=landed, dst_ref=landed, send_sem=send_i.at[3 * a + j], recv_sem=recv_i.at[3 * a + j],
                    device_id=(*chip, c), device_id_type=MESH).wait_recv()
                cp = pltpu.make_async_remote_copy(
                    src_ref=landed, dst_ref=landed, send_sem=send_d.at[3 * a + j], recv_sem=recv_d.at[3 * a + j],
                    device_id=(x, y, 1 - c), device_id_type=MESH)
                cp.start()
                sends.append(cp)
        for a in range(n):
            h = shards[a].shape[0] // 2
            for j, chip in enumerate(chips):
                src = 2 * chip[0] + chip[1]
                other = outs[a].at[src, _half(1 - c, h)]
                pltpu.make_async_remote_copy(
                    src_ref=other, dst_ref=other, send_sem=send_d.at[3 * a + j], recv_sem=recv_d.at[3 * a + j],
                    device_id=(x, y, 1 - c), device_id_type=MESH).wait_recv()
        for cp in sends:
            cp.wait_send()

    return pl.pallas_call(
        body, name="allgather_weights", in_specs=[ANY] * n, out_specs=[ANY] * n,
        out_shape=[jax.ShapeDtypeStruct((N_CHIPS,) + s.shape, s.dtype) for s in shards],
        scratch_shapes=[pltpu.SemaphoreType.DMA((3 * n,))] * 4,
    )(*shards)


HBM = pl.BlockSpec(memory_space=pltpu.HBM)
SEM = pl.BlockSpec(memory_space=pltpu.SEMAPHORE)
EFFECT = pltpu.SideEffectType.DATAFLOW_SIDE_EFFECTING


def _hbm(a):
    return pltpu.with_memory_space_constraint(a, pltpu.HBM)


def _exchange_start(name, srcs, lands, plan, ncp, after):
    ns, nl = len(srcs), len(lands)

    def body(*refs):
        send, recv, token = refs[ns + nl + 1], refs[ns + nl + 2], refs[-1]
        for k, (src, dst, _, dev) in enumerate(plan(refs[:ns], refs[ns:ns + nl])):
            pltpu.make_async_remote_copy(src_ref=src, dst_ref=dst, send_sem=send.at[k], recv_sem=recv.at[k],
                                         device_id=dev, device_id_type=MESH).start()
        token[...] = jnp.zeros_like(token)

    bufs = [_hbm(a) for a in srcs] + list(lands)
    outs = pl.pallas_call(
        body, name=name + "_start",
        out_shape=(pltpu.SemaphoreType.DMA((ncp,)), pltpu.SemaphoreType.DMA((ncp,)),
                   *[pltpu.HBM(a.shape, a.dtype) for a in bufs], jax.ShapeDtypeStruct((8, BLK), F32)),
        in_specs=[HBM] * (ns + nl) + [ANY],
        out_specs=(SEM, SEM, *[HBM] * (ns + nl), pl.BlockSpec(memory_space=pltpu.VMEM)),
        input_output_aliases={i: 2 + i for i in range(ns + nl)},
        compiler_params=pltpu.CompilerParams(has_side_effects=EFFECT),
    )(*bufs, after)
    return (outs[0], outs[1], list(outs[2:-1]), ns), outs[-1]


def _exchange_wait(name, state, plan, after):
    send, recv, bufs, ns = state
    nb = len(bufs)

    def body(*refs):
        send_ref, recv_ref = refs[nb], refs[nb + 1]
        for k, (src, _, land, dev) in enumerate(plan(refs[:ns], refs[ns:nb])):
            cp = pltpu.make_async_remote_copy(src_ref=src, dst_ref=land, send_sem=send_ref.at[k],
                                              recv_sem=recv_ref.at[k], device_id=dev, device_id_type=MESH)
            cp.wait_send()
            cp.wait_recv()

    outs = pl.pallas_call(
        body, name=name + "_wait", out_shape=tuple(pltpu.HBM(a.shape, a.dtype) for a in bufs),
        in_specs=[HBM] * nb + [SEM, SEM, ANY], out_specs=tuple([HBM] * nb),
        input_output_aliases={i: i for i in range(nb)},
        compiler_params=pltpu.CompilerParams(has_side_effects=EFFECT),
    )(*bufs, send, recv, after)
    return list(outs[ns:])


def _gather_plan(n):
    def plan(srcs, lands):
        x, y, c, chips = _place()
        me = 2 * x + y
        return [(srcs[a], lands[a].at[me], lands[a].at[2 * chip[0] + chip[1]], (*chip, c))
                for a in range(n) for chip in chips]
    return plan


def _scatter_plan(h):
    def plan(srcs, lands):
        x, y, c, _ = _place()
        me = 4 * x + 2 * y + c
        out = []
        for p in range(1, 8):
            px = 1 - x if p & 4 else x
            py = 1 - y if p & 2 else y
            pc = 1 - c if p & 1 else c
            out.append((srcs[0].at[2 * px + py, _half(pc, h)], lands[0].at[me],
                        lands[0].at[4 * px + 2 * py + pc], (px, py, pc)))
        return out
    return plan


def _grad_sum_call(own, land, place, name):
    _, h, cols = land.shape
    th = _pick(h, [256, 176, 8])
    nt = h // th

    def body(p_ref, own_ref, *refs):
        acc = own_ref[...]
        for r in refs[:7]:
            acc = acc + r[...].astype(F32)
        refs[7][...] = acc

    def peer(k):
        return pl.BlockSpec((None, th, cols), lambda i, p_ref: (p_ref[2 + k], i, 0))

    return pl.pallas_call(
        body, name=name,
        grid_spec=pltpu.PrefetchScalarGridSpec(
            num_scalar_prefetch=1, grid=(nt,),
            in_specs=[pl.BlockSpec((None, th, cols), lambda i, p_ref: (p_ref[1], p_ref[0] * nt + i, 0))]
            + [peer(k) for k in range(7)],
            out_specs=pl.BlockSpec((th, cols), lambda i, p_ref: (p_ref[0] * nt + i, 0))),
        out_shape=jax.ShapeDtypeStruct((2 * h, cols), F32),
        compiler_params=_cparams(("parallel",)))(place, own, *[land] * 7)


def _half_exchange_call(shards):
    n = len(shards)

    def body(*refs):
        outs = refs[n:2 * n]
        send_d, recv_d = refs[2 * n:]
        x, y, c, _ = _place()
        cps = []
        for a in range(n):
            h = shards[a].shape[0] // 2
            mine = outs[a].at[_half(c, h)]
            cp = pltpu.make_async_remote_copy(
                src_ref=mine, dst_ref=mine, send_sem=send_d.at[a], recv_sem=recv_d.at[a],
                device_id=(x, y, 1 - c), device_id_type=MESH)
            cp.start()
            cps.append(cp)
        for a, cp in enumerate(cps):
            h = shards[a].shape[0] // 2
            theirs = outs[a].at[_half(1 - c, h)]
            pltpu.make_async_remote_copy(
                src_ref=theirs, dst_ref=theirs, send_sem=send_d.at[a], recv_sem=recv_d.at[a],
                device_id=(x, y, 1 - c), device_id_type=MESH).wait_recv()
            cp.wait_send()

    return pl.pallas_call(
        body, name="grad_half_exchange", in_specs=[ANY] * n, out_specs=[ANY] * n,
        out_shape=[jax.ShapeDtypeStruct(sv.shape, F32) for sv in shards],
        input_output_aliases={a: a for a in range(n)},
        scratch_shapes=[pltpu.SemaphoreType.DMA((n,))] * 2,
    )(*shards)


def _allreduce_small_call(arrs):
    n = len(arrs)
    offs, rows = [], 0
    for a in arrs:
        offs.append(rows)
        rows += a.shape[0]
    rows = -(-rows // 8) * 8
    width = -(-max(a.shape[1] for a in arrs) // BLK) * BLK

    def body(*refs):
        ins, outs = refs[:n], refs[n:2 * n]
        gath, send_sems, recv_sems = refs[2 * n:]
        x, y, c, chips = _place()
        me, sibling = (x, y, c), (x, y, 1 - c)

        def slot(px, py, pc):
            return gath.at[4 * px + 2 * py + pc]

        def copy(k, block, to):
            return pltpu.make_async_remote_copy(
                src_ref=slot(*block), dst_ref=slot(*block),
                send_sem=send_sems.at[k], recv_sem=recv_sems.at[k], device_id=to, device_id_type=MESH)

        mine = slot(*me)
        mine[...] = jnp.zeros((rows, width), F32)
        for k in range(n):
            r, w = arrs[k].shape
            mine[offs[k]:offs[k] + r, 0:w] = ins[k][...]
        first = [copy(0, me, sibling)]
        first += [copy(1 + j, me, (*chip, c)) for j, chip in enumerate(chips)]
        for cp in first:
            cp.start()
        passed = [copy(4 + j, (*chip, c), sibling) for j, chip in enumerate(chips)]
        for j, chip in enumerate(chips):
            copy(1 + j, (*chip, c), me).wait_recv()
            passed[j].start()
        copy(0, sibling, me).wait_recv()
        for j, chip in enumerate(chips):
            copy(4 + j, (*chip, 1 - c), me).wait_recv()
        for cp in first + passed:
            cp.wait_send()
        acc = gath[0]
        for d in range(1, 8):
            acc = acc + gath[d]
        for k in range(n):
            r, w = arrs[k].shape
            outs[k][...] = acc[offs[k]:offs[k] + r, 0:w]

    vm = pl.BlockSpec(memory_space=pltpu.VMEM)
    return pl.pallas_call(
        body, name="allreduce_small", in_specs=[vm] * n, out_specs=[vm] * n,
        out_shape=[jax.ShapeDtypeStruct(a.shape, F32) for a in arrs],
        scratch_shapes=[pltpu.VMEM((8, rows, width), F32), pltpu.SemaphoreType.DMA((7,)),
                        pltpu.SemaphoreType.DMA((7,))],
        compiler_params=pltpu.CompilerParams(vmem_limit_bytes=VMEM_LIMIT),
    )(*arrs)


def _adamw_small_call(ws, gs, ms, vs):
    n = len(ws)
    c1 = 1.0 - ADAM_B1 ** ADAM_STEP
    c2 = 1.0 - ADAM_B2 ** ADAM_STEP

    def body(*refs):
        for k in range(n):
            w_ref, g_ref, m_ref, v_ref = (refs[j * n + k] for j in range(4))
            d_ref, mo_ref, vo_ref = (refs[(4 + j) * n + k] for j in range(3))
            gv = g_ref[...]
            m2 = ADAM_B1 * m_ref[...] + (1.0 - ADAM_B1) * gv
            v2 = ADAM_B2 * v_ref[...] + (1.0 - ADAM_B2) * (gv * gv)
            d_ref[...] = -ADAM_LR * ((m2 / c1) / (jnp.sqrt(v2 / c2) + ADAM_EPS) + ADAM_WD * w_ref[...])
            mo_ref[...] = m2
            vo_ref[...] = v2

    vm = pl.BlockSpec(memory_space=pltpu.VMEM)
    res = pl.pallas_call(
        body, name="adamw_small", in_specs=[vm] * (4 * n), out_specs=[vm] * (3 * n),
        out_shape=[jax.ShapeDtypeStruct(a.shape, F32) for a in ws] * 3,
        compiler_params=pltpu.CompilerParams(vmem_limit_bytes=VMEM_LIMIT),
    )(*ws, *gs, *ms, *vs)
    return res[:n], res[n:2 * n], res[2 * n:]


def _pack(arrs, min_rows=8):
    parts = []
    for a in arrs:
        flat = a.reshape(-1).astype(F32)
        parts.append(jnp.pad(flat, (0, (-flat.shape[0]) % BLK)))
    buf = jnp.concatenate(parts).reshape(-1, BLK)
    return jnp.pad(buf, ((0, (-buf.shape[0]) % min_rows), (0, 0)))


def _unpack(buf, shapes):
    out, r = [], 0
    for shp in shapes:
        n = math.prod(shp)
        nr = -(-n // BLK)
        out.append(buf[r:r + nr].reshape(-1)[:n].reshape(shp))
        r += nr
    return out


SMALL = ["meta_tokens", "mix_pre_g", "ssd_conv_w", "ssd_conv_b", "ssd_dt_bias", "ssd_a_log", "ssd_d", "ssd_norm_g",
         "sb_norm_g", "mix_post_g", "ffn_pre_g", "ffn_conv_w", "ffn_conv_b", "ffn_post_g"]
BIG = ["w_in", "w_out", "w_up", "w_down"]
WEIGHTS = ["meta_tokens", "mix_pre_g", "w_in", "ssd_conv_w", "ssd_conv_b", "ssd_dt_bias", "ssd_a_log", "ssd_d",
           "ssd_norm_g", "sb_norm_g", "w_out", "mix_post_g", "ffn_pre_g", "w_up", "ffn_conv_w", "ffn_conv_b",
           "w_down", "ffn_post_g"]
W_IN_SHARD = IN_COLS // N_CHIPS
W_IN_PAD = 1536


def kernel(x, meta_tokens, mix_pre_g, w_in, ssd_conv_w, ssd_conv_b, ssd_dt_bias, ssd_a_log, ssd_d, ssd_norm_g, sb_norm_g, w_out, mix_post_g, ffn_pre_g, w_up, ffn_conv_w, ffn_conv_b, w_down, ffn_post_g, loss_target, m_meta_tokens, m_mix_pre_g, m_w_in, m_ssd_conv_w, m_ssd_conv_b, m_ssd_dt_bias, m_ssd_a_log, m_ssd_d, m_ssd_norm_g, m_sb_norm_g, m_w_out, m_mix_post_g, m_ffn_pre_g, m_w_up, m_ffn_conv_w, m_ffn_conv_b, m_w_down, m_ffn_post_g, v_meta_tokens, v_mix_pre_g, v_w_in, v_ssd_conv_w, v_ssd_conv_b, v_ssd_dt_bias, v_ssd_a_log, v_ssd_d, v_ssd_norm_g, v_sb_norm_g, v_w_out, v_mix_post_g, v_ffn_pre_g, v_w_up, v_ffn_conv_w, v_ffn_conv_b, v_w_down, v_ffn_post_g):
    w = dict(meta_tokens=meta_tokens, mix_pre_g=mix_pre_g, w_in=w_in, ssd_conv_w=ssd_conv_w, ssd_conv_b=ssd_conv_b, ssd_dt_bias=ssd_dt_bias, ssd_a_log=ssd_a_log, ssd_d=ssd_d, ssd_norm_g=ssd_norm_g, sb_norm_g=sb_norm_g, w_out=w_out, mix_post_g=mix_post_g, ffn_pre_g=ffn_pre_g, w_up=w_up, ffn_conv_w=ffn_conv_w, ffn_conv_b=ffn_conv_b, w_down=w_down, ffn_post_g=ffn_post_g)
    m = dict(meta_tokens=m_meta_tokens, mix_pre_g=m_mix_pre_g, w_in=m_w_in, ssd_conv_w=m_ssd_conv_w, ssd_conv_b=m_ssd_conv_b, ssd_dt_bias=m_ssd_dt_bias, ssd_a_log=m_ssd_a_log, ssd_d=m_ssd_d, ssd_norm_g=m_ssd_norm_g, sb_norm_g=m_sb_norm_g, w_out=m_w_out, mix_post_g=m_mix_post_g, ffn_pre_g=m_ffn_pre_g, w_up=m_w_up, ffn_conv_w=m_ffn_conv_w, ffn_conv_b=m_ffn_conv_b, w_down=m_w_down, ffn_post_g=m_ffn_post_g)
    v = dict(meta_tokens=v_meta_tokens, mix_pre_g=v_mix_pre_g, w_in=v_w_in, ssd_conv_w=v_ssd_conv_w, ssd_conv_b=v_ssd_conv_b, ssd_dt_bias=v_ssd_dt_bias, ssd_a_log=v_ssd_a_log, ssd_d=v_ssd_d, ssd_norm_g=v_ssd_norm_g, sb_norm_g=v_sb_norm_g, w_out=v_w_out, mix_post_g=v_mix_post_g, ffn_pre_g=v_ffn_pre_g, w_up=v_w_up, ffn_conv_w=v_ffn_conv_w, ffn_conv_b=v_ffn_conv_b, w_down=v_w_down, ffn_post_g=v_ffn_post_g)
    chip = 2 * lax.axis_index("x") + lax.axis_index("y")
    me = 2 * chip + lax.axis_index("c")
    place = jnp.stack([lax.axis_index("c"), chip] + [me ^ p for p in range(1, 8)]).astype(jnp.int32)

    shard_small = [w["meta_tokens"], w["ssd_conv_w"][0], w["ffn_conv_w"][0]]
    shards = [jnp.pad(w["w_in"][0], ((0, 0), (0, W_IN_PAD - W_IN_SHARD))).astype(BF16), _pack(shard_small, 16)]
    gathered = _allgather_call(shards)
    late_shards = [w["w_out"][0].astype(BF16), w["w_up"][0].astype(BF16), w["w_down"][0].astype(BF16)]
    late_state, zero = _exchange_start(
        "gather_late", late_shards, [_hbm(lax.empty((N_CHIPS,) + s.shape, BF16)) for s in late_shards],
        _gather_plan(3), 9, gathered[1])

    def blocks(own, got):
        return [jnp.where(chip == i, own, got[i]) for i in range(N_CHIPS)]

    def late_weights(after):
        got = _exchange_wait("gather_late", late_state, _gather_plan(3), after)
        return (jnp.concatenate(blocks(late_shards[0], got[0]), axis=0),
                jnp.concatenate(blocks(late_shards[1], got[1]), axis=1),
                jnp.concatenate(blocks(late_shards[2], got[2]), axis=0))

    cut = DT_REAL_OFF + N_HEADS - W_IN_SHARD
    s_in = blocks(shards[0], gathered[0])
    w_in_c = jnp.concatenate(
        [s_in[0][:, :W_IN_SHARD], s_in[1][:, :cut], jnp.zeros((D_MODEL, BLK - N_HEADS), BF16),
         s_in[1][:, cut:W_IN_SHARD], s_in[2][:, :W_IN_SHARD], s_in[3][:, :W_IN_SHARD]], axis=1)
    parts = [_unpack(b, [s.shape for s in shard_small]) for b in blocks(shards[1], gathered[1])]
    wt = {k: w[k][0][None] if w[k].ndim == 3 else w[k] for k in
          ["mix_pre_g", "ssd_conv_b", "ssd_dt_bias", "ssd_a_log", "ssd_d", "ssd_norm_g", "sb_norm_g", "mix_post_g",
           "ffn_pre_g", "ffn_conv_b", "ffn_post_g"]}
    wt.update(
        meta=jnp.concatenate([p[0] for p in parts], axis=1) + zero[0:1, 0:1],
        ssd_conv_w=jnp.concatenate([p[1] for p in parts], axis=1),
        ffn_conv_w=jnp.concatenate([p[2] for p in parts], axis=1), w_in=w_in_c)

    pending = {}

    def grad_ready(name, g, g_b):
        if name == "w_in":
            skip = BLK - N_HEADS
            cols = [g[:, :W_IN_SHARD],
                    jnp.concatenate([g[:, W_IN_SHARD:W_IN_SHARD + cut], g[:, C_Q:2 * W_IN_SHARD + skip]], axis=1),
                    g[:, 2 * W_IN_SHARD + skip:3 * W_IN_SHARD + skip], g[:, 3 * W_IN_SHARD + skip:]]
            g = jnp.stack([jnp.pad(b, ((0, 0), (0, W_IN_PAD - W_IN_SHARD))) for b in cols])
            g_b = g.astype(BF16)
        h = g.shape[1] // 2
        land = _hbm(lax.empty((8, h, g.shape[2]), BF16))
        state, tok = _exchange_start("scatter_" + name, [g_b], [land], _scatter_plan(h), 7, g)
        pending[name] = (g, state, h)
        return tok[0:1, 0:1]

    loss_row, dx, small = _local_step(x[0], loss_target[0], wt, late_weights, grad_ready)

    sums = []
    for k in BIG:
        g, state, h = pending[k]
        land = _exchange_wait("scatter_" + k, state, _scatter_plan(h), dx)[0]
        sums.append(_grad_sum_call(g, land, place, "grad_sum_" + k))
    full = _half_exchange_call(sums)
    grads = {"w_in": full[0][:, :W_IN_SHARD], "w_out": full[1], "w_up": full[2], "w_down": full[3]}

    red_list = _allreduce_small_call([small[k] for k in SMALL] + [loss_row])
    loss = jnp.sum(red_list[-1])
    for k, g in zip(SMALL, red_list[:-1]):
        grads[k] = g
    for k in ["meta_tokens", "ssd_conv_w", "ffn_conv_w"]:
        wk = w[k].shape[-1]
        grads[k] = lax.dynamic_slice_in_dim(grads[k], chip * wk, wk, axis=1)

    delta, new_m, new_v = {}, {}, {}
    for k in BIG:
        delta[k], new_m[k], new_v[k] = _adamw_call(w[k][0], grads[k], m[k][0], v[k][0], "adamw_" + k)
    flat = lambda d: [d[k].reshape(grads[k].shape) for k in SMALL]
    res = _adamw_small_call(flat(w), [grads[k] for k in SMALL], flat(m), flat(v))
    for out, arrs in zip((delta, new_m, new_v), res):
        for k, a in zip(SMALL, arrs):
            out[k] = a

    def shaped(d, k):
        return d[k].reshape(w[k].shape)

    return (loss, dx[None], *[shaped(grads, k) for k in WEIGHTS], *[shaped(delta, k) for k in WEIGHTS],
            *[shaped(new_m, k) for k in WEIGHTS], *[shaped(new_v, k) for k in WEIGHTS])
```

```python
import functools
import math

import jax
import jax.numpy as jnp
from jax import lax
from jax.experimental import pallas as pl
from jax.experimental.pallas import tpu as pltpu

F32 = jnp.float32
BF16 = jnp.bfloat16

D_MODEL = 1024
N_META = 16
BLK = 128
PAD = BLK - N_META
HEAD_DIM = 64
N_HEADS = 16
SSD_GROUPS = 2
SSD_STATE = 128
SSD_INNER = 1024
XBC = SSD_INNER + 2 * SSD_GROUPS * SSD_STATE
D_FF = 2816
EPS = 1e-6
IN_COLS = 5648
C_Z, C_XBC, C_DT, C_Q, C_K, C_V, C_END = 0, 1024, 2560, 2688, 3712, 4736, 5760
DT_REAL_OFF = 2560
N_CHIPS = 4
ADAM_LR, ADAM_B1, ADAM_B2, ADAM_EPS, ADAM_WD, ADAM_STEP = 0.001, 0.9, 0.999, 1e-08, 0.01, 10
VMEM_LIMIT = 56 * 1024 * 1024
MESH = pl.DeviceIdType.MESH


def _cparams(sem=None, **kw):
    if sem is not None:
        kw["dimension_semantics"] = sem
    return pltpu.CompilerParams(vmem_limit_bytes=VMEM_LIMIT, **kw)


def _pick(n, cands):
    for c in cands:
        if n % c == 0:
            return c
    raise ValueError((n, cands))


def _iota(shape, dim):
    return lax.broadcasted_iota(jnp.int32, shape, dim)


def _sigmoid(x):
    return 1.0 / (1.0 + jnp.exp(-x))


def _split3(v):
    h1 = v.astype(BF16)
    r1 = v - h1.astype(F32)
    h2 = r1.astype(BF16)
    h3 = (r1 - h2.astype(F32)).astype(BF16)
    return h1, h2, h3


def _dot(a, b, ca=1, cb=0):
    return lax.dot_general(a, b, (((ca,), (cb,)), ((), ())), preferred_element_type=F32)


def _dot_sel_r(v, sel, cb=0):
    h1, h2, h3 = _split3(v)
    return _dot(h1, sel, 1, cb) + _dot(h2, sel, 1, cb) + _dot(h3, sel, 1, cb)


def _dot_sel_l(sel, v, ca=1):
    h1, h2, h3 = _split3(v)
    return _dot(sel, h1, ca, 0) + _dot(sel, h2, ca, 0) + _dot(sel, h3, ca, 0)


def _mm(a, b, *, ta=False, tb=False, tm, tn, tk, out_dtype=F32, nsplit=1, extra_bf16=False, ride=None, name):
    K, M = (a.shape if ta else a.shape[::-1])
    N = b.shape[0] if tb else b.shape[1]
    assert M % tm == 0 and N % tn == 0 and K % tk == 0, (name, M, N, K, tm, tn, tk)
    nm, nn, nk = M // tm, N // tn, K // tk
    assert nn % nsplit == 0
    per = nn // nsplit
    a_spec = (pl.BlockSpec((tk, tm), lambda i, j, k: (k, i)) if ta
              else pl.BlockSpec((tm, tk), lambda i, j, k: (i, k)))
    b_spec = (pl.BlockSpec((tn, tk), lambda i, j, k: (j, k)) if tb
              else pl.BlockSpec((tk, tn), lambda i, j, k: (k, j)))
    o_spec = pl.BlockSpec((None, tm, tn), lambda i, j, k: (j // per, i, j % per))
    n_out = 2 if extra_bf16 else 1
    ca, cb = (0 if ta else 1), (1 if tb else 0)
    ns, nl = (len(ride[0]), len(ride[1])) if ride else (0, 0)

    def body(a_ref, b_ref, *rest):
        outs = rest[ns:ns + n_out]
        if ride:
            step = (pl.program_id(0) * nn + pl.program_id(1)) * nk + pl.program_id(2)
            _ride_run(ride, rest[:ns], rest[ns + n_out:ns + n_out + nl], rest[-2], rest[-1],
                      step == 0, step == nm * nn * nk - 1)
        p = _dot(a_ref[...].astype(BF16), b_ref[...].astype(BF16), ca, cb)

        def emit(val):
            outs[0][...] = val.astype(out_dtype)
            if extra_bf16:
                outs[1][...] = val.astype(BF16)

        if nk == 1:
            emit(p)
        else:
            acc = rest[ns + n_out + nl]
            k = pl.program_id(2)

            @pl.when(k == 0)
            def _():
                acc[...] = p

            @pl.when(k > 0)
            def _():
                acc[...] += p

            @pl.when(k == nk - 1)
            def _():
                emit(acc[...])

    shp = (nsplit, M, N // nsplit)
    out_shape = [jax.ShapeDtypeStruct(shp, out_dtype)]
    out_specs = [o_spec]
    if extra_bf16:
        out_shape.append(jax.ShapeDtypeStruct(shp, BF16))
        out_specs.append(o_spec)
    scratch = [pltpu.VMEM((tm, tn), F32)] if nk > 1 else []
    if ride:
        res = pl.pallas_call(
            body, name=name, grid=(nm, nn, nk), in_specs=[a_spec, b_spec] + [ANY] * ns,
            out_specs=out_specs + [ANY] * nl, out_shape=out_shape + list(ride[1]),
            scratch_shapes=scratch + _ride_scratch(ride),
            compiler_params=_cparams(("arbitrary", "arbitrary", "arbitrary")),
        )(a, b, *ride[0])
        return (res[:n_out] if extra_bf16 else res[0]), list(res[n_out:])
    res = pl.pallas_call(
        body, name=name, grid=(nm, nn, nk), in_specs=[a_spec, b_spec], out_specs=out_specs,
        out_shape=out_shape, scratch_shapes=scratch,
        compiler_params=_cparams(("parallel", "parallel", "arbitrary")),
    )(a, b)
    return res if extra_bf16 else res[0]


def _rms_stats(x):
    r = lax.rsqrt(jnp.mean(x * x, axis=-1, keepdims=True) + EPS)
    return r, x * r


def _rms_bwd(x, g, dy):
    r, xh = _rms_stats(x)
    dxh = dy * g
    dx = r * (dxh - xh * jnp.mean(dxh * xh, axis=-1, keepdims=True))
    return dx, jnp.sum(dy * xh, axis=0, keepdims=True)


def _row_spec(tr, w, col=0):
    return pl.BlockSpec((tr, w), lambda i: (i, col))


def _vec_spec(w):
    return pl.BlockSpec((1, w), lambda i: (0, 0))


def _acc_rows(ref, val, i):
    @pl.when(i == 0)
    def _():
        ref[...] = val

    @pl.when(i > 0)
    def _():
        ref[...] += val


def _rms_fwd_call(x, g, name):
    lp, w = x.shape
    tr = _pick(lp, [384, 128])

    def body(x_ref, g_ref, o_ref):
        _, xh = _rms_stats(x_ref[...])
        o_ref[...] = (xh * g_ref[...]).astype(BF16)

    return pl.pallas_call(
        body, name=name, grid=(lp // tr,), in_specs=[_row_spec(tr, w), _vec_spec(w)],
        out_specs=_row_spec(tr, w), out_shape=jax.ShapeDtypeStruct((lp, w), BF16),
        compiler_params=_cparams(("parallel",)))(x, g)


def _mid_fwd_call(h0, mix, g_post, g_pre2):
    lp, w = h0.shape
    tr = _pick(lp, [384, 128])

    def body(h0_ref, mix_ref, gp_ref, g2_ref, h1_ref, xn_ref):
        _, mh = _rms_stats(mix_ref[...])
        h1 = h0_ref[...] + mh * gp_ref[...]
        h1_ref[...] = h1
        _, hh = _rms_stats(h1)
        xn_ref[...] = (hh * g2_ref[...]).astype(BF16)

    return pl.pallas_call(
        body, name="mid_fwd", grid=(lp // tr,),
        in_specs=[_row_spec(tr, w), _row_spec(tr, w), _vec_spec(w), _vec_spec(w)],
        out_specs=[_row_spec(tr, w), _row_spec(tr, w)],
        out_shape=[jax.ShapeDtypeStruct((lp, w), F32), jax.ShapeDtypeStruct((lp, w), BF16)],
        compiler_params=_cparams(("parallel",)))(h0, mix, g_post, g_pre2)


def _final_call(h1, f, g_post, target):
    lp, w = h1.shape
    tr = BLK
    nb = lp // tr

    def body(h1_ref, f_ref, g_ref, t_ref, loss_ref, df_ref, dh_ref, dg_ref):
        i = pl.program_id(0)
        fv = f_ref[...]
        g = g_ref[...]
        _, fh = _rms_stats(fv)
        h2 = h1_ref[...] + fh * g
        diff = jnp.where(i > 0, h2 - t_ref[...], 0.0)
        part = 0.5 * jnp.sum(diff * diff, axis=0, keepdims=True) * (1.0 / w)
        _acc_rows(loss_ref, part, i)
        dh = diff * (1.0 / w)
        dh_ref[...] = dh
        df, dg = _rms_bwd(fv, g, dh)
        df_ref[...] = df.astype(BF16)
        _acc_rows(dg_ref, dg, i)

    t_spec = pl.BlockSpec((tr, w), lambda i: (jnp.maximum(i - 1, 0), 0))
    return pl.pallas_call(
        body, name="final_fwd_bwd", grid=(nb,),
        in_specs=[_row_spec(tr, w), _row_spec(tr, w), _vec_spec(w), t_spec],
        out_specs=[_vec_spec(w), _row_spec(tr, w), _row_spec(tr, w), _vec_spec(w)],
        out_shape=[jax.ShapeDtypeStruct((1, w), F32), jax.ShapeDtypeStruct((lp, w), BF16),
                   jax.ShapeDtypeStruct((lp, w), F32), jax.ShapeDtypeStruct((1, w), F32)],
        compiler_params=_cparams(("arbitrary",)))(h1, f, g_post, target)


def _mid_bwd_call(dh2, h1, dxn2, mix, g_pre2, g_post):
    lp, w = h1.shape
    tr = _pick(lp, [384, 128])

    def body(dh2_ref, h1_ref, dxn_ref, mix_ref, g2_ref, gp_ref, dh1_ref, dmix_ref, dg2_ref, dgp_ref):
        i = pl.program_id(0)
        live = (i * tr + _iota((tr, 1), 0)) >= PAD
        dx, dg2 = _rms_bwd(h1_ref[...], g2_ref[...], dxn_ref[...])
        dh1 = jnp.where(live, dh2_ref[...] + dx, 0.0)
        dh1_ref[...] = dh1
        dmix, dgp = _rms_bwd(mix_ref[...], gp_ref[...], dh1)
        dmix_ref[...] = jnp.where(live, dmix, 0.0).astype(BF16)
        _acc_rows(dg2_ref, dg2, i)
        _acc_rows(dgp_ref, dgp, i)

    rs = _row_spec(tr, w)
    return pl.pallas_call(
        body, name="mid_bwd", grid=(lp // tr,),
        in_specs=[rs, rs, rs, rs, _vec_spec(w), _vec_spec(w)],
        out_specs=[rs, rs, _vec_spec(w), _vec_spec(w)],
        out_shape=[jax.ShapeDtypeStruct((lp, w), F32), jax.ShapeDtypeStruct((lp, w), BF16),
                   jax.ShapeDtypeStruct((1, w), F32), jax.ShapeDtypeStruct((1, w), F32)],
        compiler_params=_cparams(("arbitrary",)))(dh2, h1, dxn2, mix, g_pre2, g_post)


def _norm_bwd_call(x, g, dy_arr, dy_col, name, res=None):
    lp, w = x.shape
    tr = _pick(lp, [384, 128])
    has_res = res is not None

    def body(x_ref, g_ref, dy_ref, *rest):
        i = pl.program_id(0)
        live = (i * tr + _iota((tr, 1), 0)) >= PAD
        dx, dg = _rms_bwd(x_ref[...], g_ref[...], dy_ref[...])
        if has_res:
            dx = dx + rest[0][...]
        out_ref, dg_ref = rest[-2], rest[-1]
        out_ref[...] = jnp.where(live, dx, 0.0)
        _acc_rows(dg_ref, dg, i)

    rs = _row_spec(tr, w)
    ins = [rs, _vec_spec(w), _row_spec(tr, w, dy_col)] + ([rs] if has_res else [])
    args = [x, g, dy_arr] + ([res] if has_res else [])
    return pl.pallas_call(
        body, name=name, grid=(lp // tr,), in_specs=ins, out_specs=[rs, _vec_spec(w)],
        out_shape=[jax.ShapeDtypeStruct((lp, w), F32), jax.ShapeDtypeStruct((1, w), F32)],
        compiler_params=_cparams(("arbitrary",)))(*args)


def _shift_down(cur, prev_tail, s, rows):
    if s == 0:
        return cur
    prev = jnp.tile(prev_tail, (BLK // 8, 1))
    return jnp.where(rows >= s, pltpu.roll(cur, s, 0), pltpu.roll(prev, s, 0))


def _shift_up(cur, next_head, s, rows):
    if s == 0:
        return cur
    nxt = jnp.tile(next_head, (BLK // 8, 1))
    return jnp.where(rows < BLK - s, pltpu.roll(cur, BLK - s, 0), pltpu.roll(nxt, BLK - s, 0))


def _gelu_tanh(x):
    c = math.sqrt(2.0 / math.pi)
    t = jnp.tanh(c * (x + 0.044715 * x * x * x))
    return 0.5 * x * (1.0 + t), t


def _conv_fwd_call(src, col0, width, cw, w8, b, taps, *, gate_src=None, gate_col0=0, name):
    lp = src.shape[0]
    nb, nc = lp // BLK, width // cw
    cb0 = col0 // cw
    ffn = gate_src is not None

    def body(x_ref, w_ref, b_ref, *rest):
        if ffn:
            u_ref, y_ref, a_ref, tail = rest
        else:
            y_ref, a_ref, tail = rest
        i = pl.program_id(1)

        @pl.when(i == 0)
        def _():
            tail[...] = jnp.zeros_like(tail)

        cur = x_ref[...]
        rows = _iota((BLK, cw), 0)
        y = b_ref[...] + w_ref[taps - 1:taps, :] * cur
        pt = tail[...]
        for s in range(1, taps):
            y = y + w_ref[taps - 1 - s:taps - s, :] * _shift_down(cur, pt, s, rows)
        tail[...] = cur[BLK - 8:, :]
        y_ref[...] = y
        if ffn:
            ge, _ = _gelu_tanh(y)
            a_ref[...] = (ge * u_ref[...]).astype(BF16)
        else:
            live = (i * BLK + rows) >= PAD
            a_ref[...] = jnp.where(live, y * _sigmoid(y), 0.0)

    blk = lambda c0: pl.BlockSpec((BLK, cw), lambda j, i: (i, c0 + j))
    ins = [blk(cb0), pl.BlockSpec((8, cw), lambda j, i: (0, j)), pl.BlockSpec((1, cw), lambda j, i: (0, j))]
    args = [src, w8, b]
    if ffn:
        ins.append(blk(gate_col0 // cw))
        args.append(gate_src)
    return pl.pallas_call(
        body, name=name, grid=(nc, nb), in_specs=ins, out_specs=[blk(0), blk(0)],
        out_shape=[jax.ShapeDtypeStruct((lp, width), F32),
                   jax.ShapeDtypeStruct((lp, width), BF16 if ffn else F32)],
        scratch_shapes=[pltpu.VMEM((8, cw), F32)],
        compiler_params=_cparams(("parallel", "arbitrary")))(*args)


def _conv_bwd_call(src, col0, width, cw, w8, taps, ypre, dact, *, gate_src=None, gate_col0=0, name):
    lp = src.shape[0]
    nb, nc = lp // BLK, width // cw
    cb0 = col0 // cw
    ffn = gate_src is not None

    def body(x_ref, w_ref, y_ref, d_ref, *rest):
        if ffn:
            u_ref, dx_ref, du_ref, dw_ref, db_ref, head = rest
        else:
            dx_ref, dw_ref, db_ref, head = rest
        step = pl.program_id(1)
        i = nb - 1 - step

        @pl.when(step == 0)
        def _():
            head[...] = jnp.zeros_like(head)

        rows = _iota((BLK, cw), 0)
        live = (i * BLK + rows) >= PAD
        y = y_ref[...]
        d = d_ref[...]
        if ffn:
            ge, t = _gelu_tanh(y)
            c = math.sqrt(2.0 / math.pi)
            dge = 0.5 * (1.0 + t) + 0.5 * y * (1.0 - t * t) * c * (1.0 + 3.0 * 0.044715 * y * y)
            u = u_ref[...]
            du_ref[...] = jnp.where(live, d * ge, 0.0).astype(BF16)
            dy = jnp.where(live, d * u * dge, 0.0)
        else:
            sg = _sigmoid(y)
            dy = jnp.where(live, d * sg * (1.0 + y * (1.0 - sg)), 0.0)
        x = x_ref[...]
        nh = head[...]
        dx = jnp.zeros_like(dy)
        dws = []
        for s in range(taps):
            sh = _shift_up(dy, nh, s, rows)
            dx = dx + w_ref[taps - 1 - s:taps - s, :] * sh
            dws.append(jnp.sum(x * sh, axis=0, keepdims=True))
        head[...] = dy[:8, :]
        dx_ref[...] = jnp.where(live, dx, 0.0).astype(BF16)
        dw = jnp.concatenate([dws[taps - 1 - k] for k in range(taps)]
                             + [jnp.zeros((8 - taps, cw), F32)], axis=0)
        _acc_rows(dw_ref, dw, step)
        _acc_rows(db_ref, jnp.sum(dy, axis=0, keepdims=True), step)

    blk = lambda c0: pl.BlockSpec((BLK, cw), lambda j, s: (nb - 1 - s, c0 + j))
    ins = [blk(cb0), pl.BlockSpec((8, cw), lambda j, s: (0, j)), blk(0), blk(0)]
    args = [src, w8, ypre, dact]
    outs = [blk(0)]
    oshape = [jax.ShapeDtypeStruct((lp, width), BF16)]
    if ffn:
        ins.append(blk(gate_col0 // cw))
        args.append(gate_src)
        outs.append(blk(0))
        oshape.append(jax.ShapeDtypeStruct((lp, width), BF16))
    outs += [pl.BlockSpec((8, cw), lambda j, s: (0, j)), pl.BlockSpec((1, cw), lambda j, s: (0, j))]
    oshape += [jax.ShapeDtypeStruct((8, width), F32), jax.ShapeDtypeStruct((1, width), F32)]
    return pl.pallas_call(
        body, name=name, grid=(nc, nb), in_specs=ins, out_specs=outs, out_shape=oshape,
        scratch_shapes=[pltpu.VMEM((8, cw), F32)],
        compiler_params=_cparams(("parallel", "arbitrary")))(*args)


SB_GROUP = 4
SB_DEAD = -110.0


def _sb_scores(qm_h, kb):
    z = _dot(qm_h, kb, 1, 1)
    sp = jnp.maximum(z, 0.0) + jnp.log(1.0 + jnp.exp(-jnp.abs(z)))
    return z - sp, -sp


def _sb_valid(i, off, width):
    kpos = off + _iota((BLK, width), 1)
    qpos = i * BLK + _iota((BLK, width), 0)
    return (kpos < qpos) & (kpos >= PAD)


def _dot_tri1(v, tri2):
    r = _dot(v.astype(BF16), tri2[:BLK])
    return r[:, :BLK], r[:, BLK:]


def _sb_groups(i):
    edge = i // SB_GROUP
    return edge, pl.multiple_of(edge * (SB_GROUP * BLK), BLK)


def _tri2(cond):
    t = jnp.concatenate([cond.astype(BF16), jnp.ones((BLK, BLK), BF16)], axis=1)
    return jnp.concatenate([t, t], axis=0)


def _dot_tri(v, tri2):
    hi = v.astype(BF16)
    lo = (v - hi.astype(F32)).astype(BF16)
    r = _dot(jnp.concatenate([hi, lo], axis=1), tri2)
    return r[:, :BLK], r[:, BLK:]


def _sb_fwd_call(proj, ride):
    lp = proj.shape[0]
    nb = lp // BLK
    assert (nb - 1) % SB_GROUP == 0
    scale = 1.0 / math.sqrt(HEAD_DIM)

    ns, nl = len(ride[0]), len(ride[1])
    npair = N_HEADS // 2

    def body(q_ref, k_ref, v_ref, *rest):
        o_ref, tl_ref = rest[ns], rest[ns + 1]
        i = pl.program_id(1)
        step = pl.program_id(0) * nb + i
        _ride_run(ride, rest[:ns], rest[ns + 2:ns + 2 + nl], rest[-2], rest[-1], step == 0, step == npair * nb - 1)
        lane = _iota((2 * BLK, BLK), 1)
        row = _iota((2 * BLK, BLK), 0)
        first = row < BLK
        qrow = row & (BLK - 1)
        q = q_ref[...] * scale
        q2 = jnp.concatenate([q, q], axis=0)
        qm = jnp.where(first == (lane < HEAD_DIM), q2, 0.0).astype(BF16)
        tri = _tri2(_iota((BLK, BLK), 0) > _iota((BLK, BLK), 1))

        def chunk(off, nsub, last_valid, carry):
            width = nsub * BLK
            sls = [slice(b * BLK, (b + 1) * BLK) for b in range(nsub)]
            kb = k_ref[pl.ds(off, width), :].astype(BF16)
            vb = v_ref[pl.ds(off, width), :].astype(BF16)
            lb, lk = _sb_scores(qm, kb)
            lks = [lk[:, sl] for sl in sls]
            first_valid = (off + lane) >= PAD
            lks[0] = jnp.where(first_valid, lks[0], 0.0)
            if last_valid is not None:
                lks[-1] = jnp.where(last_valid, lks[-1], 0.0)
            afters = [_dot_tri(lks[b], tri) for b in range(nsub)]
            run, acc = carry
            ws = [None] * nsub
            for b in reversed(range(nsub)):
                wb = jnp.exp(lb[:, sls[b]] + afters[b][0] + run)
                if b == 0:
                    wb = jnp.where(first_valid, wb, 0.0)
                if last_valid is not None and b == nsub - 1:
                    wb = jnp.where(last_valid, wb, 0.0)
                ws[b] = wb.astype(BF16)
                run = run + afters[b][1]
            w = ws[0] if nsub == 1 else jnp.concatenate(ws, axis=1)
            return run, acc + _dot(w, vb)

        edge, edge_off = _sb_groups(i)
        diag = lane < qrow
        zero = jnp.zeros((2 * BLK, BLK), F32)
        upto = [functools.partial(chunk, edge_off, r, diag) for r in range(1, SB_GROUP + 1)]
        carry = lax.switch(i - edge * SB_GROUP, upto, (zero, zero))

        def interior(state):
            off = pl.multiple_of(state[0] * (SB_GROUP * BLK), BLK)
            return (state[0] - 1, *chunk(off, SB_GROUP, None, state[1:]))

        def live(state):
            return jnp.logical_and(state[0] >= 0, jnp.max(state[1]) > SB_DEAD)

        below, run, acc = lax.while_loop(live, interior, (edge - 1, *carry))
        low = lane[:BLK] < HEAD_DIM
        o_ref[...] = jnp.where(low, acc[:BLK], acc[BLK:])
        tl = jnp.where(low, run[:BLK], run[BLK:])
        tl_ref[...] = jnp.where(lane[:BLK] == 1, (below + 1).astype(F32), tl)

    qc, kc, vc = C_Q // BLK, C_K // BLK, C_V // BLK
    blk = pl.BlockSpec((BLK, BLK), lambda p, i: (i, p))
    res = pl.pallas_call(
        body, name="sb_fwd", grid=(npair, nb),
        in_specs=[pl.BlockSpec((BLK, BLK), lambda p, i: (i, qc + p)),
                  pl.BlockSpec((lp, BLK), lambda p, i: (0, kc + p)),
                  pl.BlockSpec((lp, BLK), lambda p, i: (0, vc + p))] + [ANY] * ns,
        out_specs=[blk, blk] + [ANY] * nl,
        out_shape=[jax.ShapeDtypeStruct((lp, N_HEADS * HEAD_DIM), F32)] * 2 + list(ride[1]),
        scratch_shapes=_ride_scratch(ride),
        compiler_params=_cparams(("arbitrary", "arbitrary")))(proj, proj, proj, *ride[0])
    return res[0], res[1], list(res[2:])


def _sb_bwd_call(proj, tl, do, ride):
    lp = proj.shape[0]
    nb = lp // BLK
    assert (nb - 1) % SB_GROUP == 0
    scale = 1.0 / math.sqrt(HEAD_DIM)

    ns, nl = len(ride[0]), len(ride[1])
    npair = N_HEADS // 2

    def body(q_ref, k_ref, v_ref, tl_ref, do_ref, *rest):
        dq_ref, dk_ref, dv_ref = rest[ns:ns + 3]
        dk_acc, dv_acc = rest[ns + 3 + nl:ns + 5 + nl]
        i = pl.program_id(1)
        step = pl.program_id(0) * nb + i
        _ride_run(ride, rest[:ns], rest[ns + 3:ns + 3 + nl], rest[-2], rest[-1], step == 0, step == npair * nb - 1)

        @pl.when(i == 0)
        def _():
            dk_acc[...] = jnp.zeros_like(dk_acc)
            dv_acc[...] = jnp.zeros_like(dv_acc)

        lane = _iota((2 * BLK, BLK), 1)
        row = _iota((2 * BLK, BLK), 0)
        qrow = row & (BLK - 1)
        mine = (row < BLK) == (lane < HEAD_DIM)
        q = q_ref[...] * scale
        dov = do_ref[...]
        qm = jnp.where(mine, jnp.concatenate([q, q], axis=0), 0.0).astype(BF16)
        dom = jnp.where(mine, jnp.concatenate([dov, dov], axis=0), 0.0).astype(BF16)
        tlv = tl_ref[...]
        tot = jnp.concatenate([jnp.broadcast_to(tlv[:, 0:1], (BLK, BLK)),
                               jnp.broadcast_to(tlv[:, HEAD_DIM:HEAD_DIM + 1], (BLK, BLK))], axis=0)
        r1, l1 = _iota((BLK, BLK), 0), _iota((BLK, BLK), 1)
        tri_in = _tri2(r1 <= l1)
        tri_ex = _tri2(r1 < l1)

        def chunk(off, nsub, last_valid, carry):
            width = nsub * BLK
            sls = [slice(b * BLK, (b + 1) * BLK) for b in range(nsub)]
            cat = lambda parts: parts[0] if nsub == 1 else jnp.concatenate(parts, axis=1)
            mask_last = lambda b: last_valid is not None and b == nsub - 1
            kb = k_ref[pl.ds(off, width), :].astype(BF16)
            vb = v_ref[pl.ds(off, width), :].astype(BF16)
            lb, lk = _sb_scores(qm, kb)
            dw = _dot(dom, vb, 1, 1)
            lks = [lk[:, sl] for sl in sls]
            first_valid = (off + lane) >= PAD
            lks[0] = jnp.where(first_valid, lks[0], 0.0)
            if last_valid is not None:
                lks[-1] = jnp.where(last_valid, lks[-1], 0.0)
            pins = [_dot_tri(lks[b], tri_in) for b in range(nsub)]
            run, gsum, dq = carry
            ws, gs = [], []
            for b in range(nsub):
                wb = jnp.exp(lb[:, sls[b]] + (tot - run - pins[b][0]))
                if b == 0:
                    wb = jnp.where(first_valid, wb, 0.0)
                if mask_last(b):
                    wb = jnp.where(last_valid, wb, 0.0)
                ws.append(wb.astype(BF16))
                gs.append(wb * dw[:, sls[b]])
                run = run + pins[b][1]
            gexs = [_dot_tri1(gs[b], tri_ex) for b in range(nsub)]
            beta = jnp.exp(lb)
            parts = []
            for b in range(nsub):
                bt = beta[:, sls[b]]
                dzb = gs[b] * (1.0 - bt) - (gsum + gexs[b][0]) * bt
                if b == 0:
                    dzb = jnp.where(first_valid, dzb, 0.0)
                if mask_last(b):
                    dzb = jnp.where(last_valid, dzb, 0.0)
                parts.append(dzb.astype(BF16))
                gsum = gsum + gexs[b][1]
            dz, w = cat(parts), cat(ws)
            dk_acc[pl.ds(off, width), :] += _dot(dz, qm, 0, 0)
            dv_acc[pl.ds(off, width), :] += _dot(w, dom, 0, 0)
            return run, gsum, dq + _dot(dz, kb)

        edge, edge_off = _sb_groups(i)
        diag = lane < qrow
        zero = jnp.zeros((2 * BLK, BLK), F32)
        first = jnp.max(tlv[:, 1:2]).astype(jnp.int32)

        def interior(g, carry):
            return chunk(pl.multiple_of(g * (SB_GROUP * BLK), BLK), SB_GROUP, None, carry)

        carry = lax.fori_loop(first, edge, interior, (zero, zero, zero))
        upto = [functools.partial(chunk, edge_off, r, diag) for r in range(1, SB_GROUP + 1)]
        dq = lax.switch(i - edge * SB_GROUP, upto, carry)[2]
        dq_ref[...] = (jnp.where(lane[:BLK] < HEAD_DIM, dq[:BLK], dq[BLK:]) * scale).astype(BF16)

        @pl.when(i == nb - 1)
        def _():
            dk_ref[...] = dk_acc[...].astype(BF16)
            dv_ref[...] = dv_acc[...].astype(BF16)

    qc, kc, vc = C_Q // BLK, C_K // BLK, C_V // BLK
    blk = pl.BlockSpec((BLK, BLK), lambda p, i: (i, p))
    full = pl.BlockSpec((lp, BLK), lambda p, i: (0, p))
    w = N_HEADS * HEAD_DIM
    res = pl.pallas_call(
        body, name="sb_bwd", grid=(npair, nb),
        in_specs=[pl.BlockSpec((BLK, BLK), lambda p, i: (i, qc + p)),
                  pl.BlockSpec((lp, BLK), lambda p, i: (0, kc + p)),
                  pl.BlockSpec((lp, BLK), lambda p, i: (0, vc + p)),
                  blk, blk] + [ANY] * ns,
        out_specs=[blk, full, full] + [ANY] * nl,
        out_shape=[jax.ShapeDtypeStruct((lp, w), BF16)] * 3 + list(ride[1]),
        scratch_shapes=[pltpu.VMEM((lp, BLK), F32), pltpu.VMEM((lp, BLK), F32)] + _ride_scratch(ride),
        compiler_params=_cparams(("arbitrary", "arbitrary")))(proj, proj, proj, tl, do, *ride[0])
    return res[0], res[1], res[2], list(res[3:])


def _log1p(e):
    u = 1.0 + e
    return jnp.where(u == 1.0, e, jnp.log(u) * e / jnp.where(u == 1.0, 1.0, u - 1.0))


def _ssd_common(c, dtr, bias, alog):
    row = _iota((BLK, BLK), 0)
    lane = _iota((BLK, BLK), 1)
    live = ((c * BLK + row) >= PAD) & (lane < N_HEADS)
    pre = dtr + bias
    dt = jnp.where(live, jnp.maximum(pre, 0.0) + _log1p(jnp.exp(-jnp.abs(pre))), 0.0)
    a_neg = -jnp.exp(alog)
    a = dt * a_neg
    t_in = (lane <= row).astype(BF16)
    cs = _dot_sel_l(t_in, a)
    cs_t = cs.T
    cs_end = cs[BLK - 1:BLK, :]
    e = jnp.exp(cs)
    f = jnp.exp(cs_end - cs)
    xp = ((_iota((BLK, SSD_INNER), 1) // HEAD_DIM) == _iota((BLK, SSD_INNER), 0)).astype(BF16)
    xp_t = ((_iota((SSD_INNER, BLK), 0) // HEAD_DIM) == _iota((SSD_INNER, BLK), 1)).astype(BF16)
    decay_col = _dot_sel_l(xp_t, jnp.exp(cs_t))[:, BLK - 1:BLK]
    return dict(live=live, pre=pre, dt=dt, a_neg=a_neg, cs=cs, cs_t=cs_t, e=e, f=f, xp=xp, xp_t=xp_t,
                decay_col=decay_col, row=row, lane=lane,
                dt_x=_dot_sel_r(dt, xp), e_x=_dot_sel_r(e, xp), f_x=_dot_sel_r(f, xp))


def _ssd_ldec(q, h):
    diff = q["cs"][:, h:h + 1] - q["cs_t"][h:h + 1, :]
    causal = q["row"] >= q["lane"]
    return jnp.where(causal, jnp.exp(jnp.where(causal, diff, 0.0)), 0.0)


def _ssd_fwd_call(xbc, proj, bias, alog, d_x, norm_g):
    lp = xbc.shape[0]
    nb = lp // BLK
    gw = SSD_INNER // SSD_GROUPS
    ppg = gw // BLK

    def body(xbc_ref, dtr_ref, z_ref, bias_ref, alog_ref, dx_ref, ng_ref, yb_ref, ypre_ref, sprev_ref, s_ref):
        c = pl.program_id(0)

        @pl.when(c == 0)
        def _():
            s_ref[...] = jnp.zeros_like(s_ref)

        q = _ssd_common(c, dtr_ref[...], bias_ref[...], alog_ref[...])
        x = xbc_ref[:, 0:SSD_INNER]
        xd = x * q["dt_x"]
        low = q["lane"] < HEAD_DIM
        s_old = s_ref[...]
        sprev_ref[...] = s_old
        xdf = (xd * q["f_x"]).astype(BF16)
        for g in range(SSD_GROUPS):
            bg = xbc_ref[:, SSD_INNER + g * SSD_STATE:SSD_INNER + (g + 1) * SSD_STATE].astype(BF16)
            cg = xbc_ref[:, SSD_INNER + (SSD_GROUPS + g) * SSD_STATE:
                         SSD_INNER + (SSD_GROUPS + g + 1) * SSD_STATE].astype(BF16)
            cb = _dot(cg, bg, 1, 1)
            gs = slice(g * gw, (g + 1) * gw)
            y_off = _dot(cg, s_old[gs, :].astype(BF16), 1, 1) * q["e_x"][:, gs]
            s_ref[gs, :] = s_old[gs, :] * q["decay_col"][gs, :] + _dot(xdf[:, gs], bg, 0, 0)
            for pr in range(ppg):
                cols = slice(g * gw + pr * BLK, g * gw + (pr + 1) * BLK)
                xd_p = xd[:, cols]
                acc = y_off[:, pr * BLK:(pr + 1) * BLK]
                for hh in range(2):
                    h = (g * gw + pr * BLK) // HEAD_DIM + hh
                    m = (cb * _ssd_ldec(q, h)).astype(BF16)
                    xm = jnp.where(low, xd_p, 0.0) if hh == 0 else jnp.where(low, 0.0, xd_p)
                    acc = acc + _dot(m, xm.astype(BF16))
                ypre_ref[:, cols] = acc
        ypre = ypre_ref[...] + x * dx_ref[...]
        ypre_ref[...] = ypre
        z = z_ref[...]
        yg = ypre * (z * _sigmoid(z))
        _, yh = _rms_stats(yg)
        yb_ref[...] = (yh * ng_ref[...]).astype(BF16)

    row = lambda w, col: pl.BlockSpec((BLK, w), lambda c: (c, col))
    vec = lambda w: pl.BlockSpec((1, w), lambda c: (0, 0))
    return pl.pallas_call(
        body, name="ssd_fwd", grid=(nb,),
        in_specs=[row(XBC, 0), row(BLK, C_DT // BLK), row(SSD_INNER, 0), vec(BLK), vec(BLK),
                  vec(SSD_INNER), vec(SSD_INNER)],
        out_specs=[row(SSD_INNER, 0), row(SSD_INNER, 0),
                   pl.BlockSpec((None, SSD_INNER, SSD_STATE), lambda c: (c, 0, 0))],
        out_shape=[jax.ShapeDtypeStruct((lp, SSD_INNER), BF16), jax.ShapeDtypeStruct((lp, SSD_INNER), F32),
                   jax.ShapeDtypeStruct((nb, SSD_INNER, SSD_STATE), F32)],
        scratch_shapes=[pltpu.VMEM((SSD_INNER, SSD_STATE), F32)],
        compiler_params=_cparams(("arbitrary",)))(xbc, proj, proj, bias, alog, d_x, norm_g)


def _ssd_bwd_call(dycat, ypre, xbc, proj, sprev, bias, alog, d_x, norm_g):
    lp = xbc.shape[0]
    nb = lp // BLK
    gw = SSD_INNER // SSD_GROUPS
    ppg = gw // BLK

    def body(dy_ref, ypre_ref, xbc_ref, dtr_ref, z_ref, sp_ref, bias_ref, alog_ref, dxp_ref, ng_ref,
             dz_ref, dxbc_ref, ddt_ref, dng_ref, dd_ref, dal_ref, dbi_ref, ds_ref, dxd_ref):
        step = pl.program_id(0)
        c = nb - 1 - step

        @pl.when(step == 0)
        def _():
            ds_ref[...] = jnp.zeros_like(ds_ref)

        q = _ssd_common(c, dtr_ref[...], bias_ref[...], alog_ref[...])
        row, lane = q["row"], q["lane"]
        low = lane < HEAD_DIM
        rowlive = ((c * BLK + _iota((BLK, 1), 0)) >= PAD)
        x = xbc_ref[:, 0:SSD_INNER]
        xd = x * q["dt_x"]
        z = z_ref[...]
        sz = _sigmoid(z)
        silu = z * sz
        ypre = ypre_ref[...]
        dyg, dng = _rms_bwd(ypre * silu, ng_ref[...], dy_ref[...])
        _acc_rows(dng_ref, dng, step)
        dyp = dyg * silu
        dz_ref[...] = jnp.where(rowlive, dyg * ypre * (sz * (1.0 + z * (1.0 - sz))), 0.0).astype(BF16)
        _acc_rows(dd_ref, jnp.sum(dyp * x, axis=0, keepdims=True), step)
        dye = dyp * q["e_x"]
        xdf = xd * q["f_x"]
        s_prev = sp_ref[...]
        ds_old = ds_ref[...]
        qrow = jnp.zeros((BLK, BLK), F32)
        qcol_t = jnp.zeros((BLK, BLK), F32)
        red_e = []
        red_f = []
        for g in range(SSD_GROUPS):
            gs = slice(g * gw, (g + 1) * gw)
            bsl = slice(SSD_INNER + g * SSD_STATE, SSD_INNER + (g + 1) * SSD_STATE)
            csl = slice(SSD_INNER + (SSD_GROUPS + g) * SSD_STATE, SSD_INNER + (SSD_GROUPS + g + 1) * SSD_STATE)
            bg = xbc_ref[:, bsl].astype(BF16)
            cg = xbc_ref[:, csl].astype(BF16)
            sg = s_prev[gs, :].astype(BF16)
            dsg = ds_old[gs, :].astype(BF16)
            cb = _dot(cg, bg, 1, 1)
            bds = _dot(bg, dsg, 1, 1)
            y_off = _dot(cg, sg, 1, 1) * q["e_x"][:, gs]
            red_e.append(dyp[:, gs] * y_off)
            red_f.append(xd[:, gs] * bds * q["f_x"][:, gs])
            dc = _dot(dye[:, gs].astype(BF16), sg)
            db = _dot(xdf[:, gs].astype(BF16), dsg)
            ds_ref[gs, :] = ds_old[gs, :] * q["decay_col"][gs, :] + _dot(dye[:, gs].astype(BF16), cg, 0, 0)
            dcb = jnp.zeros((BLK, BLK), F32)
            for pr in range(ppg):
                cols = slice(g * gw + pr * BLK, g * gw + (pr + 1) * BLK)
                xd_p = xd[:, cols].astype(BF16)
                dy_p = dyp[:, cols]
                acc = q["f_x"][:, cols] * bds[:, pr * BLK:(pr + 1) * BLK]
                for hh in range(2):
                    h = (g * gw + pr * BLK) // HEAD_DIM + hh
                    ld = _ssd_ldec(q, h)
                    m = cb * ld
                    dym = (jnp.where(low, dy_p, 0.0) if hh == 0 else jnp.where(low, 0.0, dy_p)).astype(BF16)
                    dm = jnp.where(row >= lane, _dot(dym, xd_p, 1, 1), 0.0)
                    acc = acc + _dot(m.astype(BF16), dym, 0, 0)
                    dcb = dcb + dm * ld
                    qq = dm * m
                    qrow = qrow + jnp.where(lane == h, jnp.sum(qq, axis=1, keepdims=True), 0.0)
                    qcol_t = qcol_t + jnp.where(row == h, jnp.sum(qq, axis=0, keepdims=True), 0.0)
                dxd_ref[:, cols] = acc
            dcbb = dcb.astype(BF16)
            dxbc_ref[:, bsl] = jnp.where(rowlive, db + _dot(dcbb, cg, 0, 0), 0.0)
            dxbc_ref[:, csl] = jnp.where(rowlive, dc + _dot(dcbb, bg), 0.0)
        dxd = dxd_ref[...]
        dxbc_ref[:, 0:SSD_INNER] = jnp.where(rowlive, dxd * q["dt_x"] + dyp * dxp_ref[...], 0.0)
        xp_t = q["xp_t"]
        fw = _dot_sel_r(jnp.concatenate(red_f, axis=1), xp_t)
        dcs = qrow - qcol_t.T + _dot_sel_r(jnp.concatenate(red_e, axis=1), xp_t) - fw
        end_f = jnp.sum(fw, axis=0, keepdims=True)
        sds = jnp.sum(ds_old * s_prev, axis=1, keepdims=True)
        per_head = _dot_sel_l(q["xp"], jnp.broadcast_to(sds, (SSD_INNER, BLK)))
        end_e = per_head.T[0:1, :] * jnp.exp(q["cs"][BLK - 1:BLK, :])
        dcs = dcs + jnp.where(row == BLK - 1, end_f + end_e, 0.0)
        t_up = (lane >= row).astype(BF16)
        da = _dot_sel_l(t_up, dcs)
        ddt = da * q["a_neg"] + _dot_sel_r(dxd * x, xp_t)
        _acc_rows(dal_ref, jnp.sum(da * q["dt"] * q["a_neg"], axis=0, keepdims=True), step)
        ddtr = jnp.where(q["live"], ddt * _sigmoid(q["pre"]), 0.0)
        ddt_ref[...] = ddtr.astype(BF16)
        _acc_rows(dbi_ref, jnp.sum(ddtr, axis=0, keepdims=True), step)

    row_s = lambda w, col: pl.BlockSpec((BLK, w), lambda s: (nb - 1 - s, col))
    vec = lambda w: pl.BlockSpec((1, w), lambda s: (0, 0))
    return pl.pallas_call(
        body, name="ssd_bwd", grid=(nb,),
        in_specs=[row_s(SSD_INNER, 0), row_s(SSD_INNER, 0), row_s(XBC, 0), row_s(BLK, C_DT // BLK),
                  row_s(SSD_INNER, 0), pl.BlockSpec((None, SSD_INNER, SSD_STATE), lambda s: (nb - 1 - s, 0, 0)),
                  vec(BLK), vec(BLK), vec(SSD_INNER), vec(SSD_INNER)],
        out_specs=[row_s(SSD_INNER, 0), row_s(XBC, 0), row_s(BLK, 0),
                   vec(SSD_INNER), vec(SSD_INNER), vec(BLK), vec(BLK)],
        out_shape=[jax.ShapeDtypeStruct((lp, SSD_INNER), BF16), jax.ShapeDtypeStruct((lp, XBC), F32),
                   jax.ShapeDtypeStruct((lp, BLK), BF16),
                   jax.ShapeDtypeStruct((1, SSD_INNER), F32), jax.ShapeDtypeStruct((1, SSD_INNER), F32),
                   jax.ShapeDtypeStruct((1, BLK), F32), jax.ShapeDtypeStruct((1, BLK), F32)],
        scratch_shapes=[pltpu.VMEM((SSD_INNER, SSD_STATE), F32), pltpu.VMEM((BLK, SSD_INNER), F32)],
        compiler_params=_cparams(("arbitrary",)))(dycat, ypre, xbc, proj, proj, sprev, bias, alog, d_x, norm_g)


def _pad_rows8(w):
    return jnp.pad(w, ((0, 8 - w.shape[0]), (0, 0)))


def _pad_lanes(v, n=BLK):
    return jnp.pad(v, ((0, 0), (0, n - v.shape[1])))


def _local_step(x, target, wt, late_shards, late_weights, w_in_shards):
    seq = x.shape[0]
    lp = seq + BLK
    tm = _pick(lp, [1408, 768, 384, 128])
    tkr = _pick(lp, [1408, 384, 128])
    h0 = jnp.concatenate([jnp.zeros((PAD, D_MODEL), F32), wt["meta"], x], axis=0)
    bias = _pad_lanes(wt["ssd_dt_bias"])
    alog = _pad_lanes(wt["ssd_a_log"])
    d_x = jnp.repeat(wt["ssd_d"], HEAD_DIM, axis=1)
    cw8 = _pad_rows8(wt["ssd_conv_w"])
    fw8 = _pad_rows8(wt["ffn_conv_w"])
    fcw = D_FF // 2

    xn1 = _rms_fwd_call(h0, wt["mix_pre_g"], "norm1")
    proj = _mm(xn1, wt["w_in"], tm=tm, tn=1152, tk=D_MODEL, name="mm_proj")[0]
    conv_pre, xbc = _conv_fwd_call(proj, C_XBC, XBC, 512, cw8, wt["ssd_conv_b"], 4, name="ssd_conv_fwd")
    y_ssd, ypre, sprev = _ssd_fwd_call(xbc, proj, bias, alog, d_x, wt["ssd_norm_g"])
    o, tl, late = _sb_fwd_call(proj, _gather_ride(late_shards))
    y_sb = _rms_fwd_call(o, wt["sb_norm_g"], "sb_norm")
    ycat = jnp.concatenate([y_ssd, y_sb], axis=1)
    w_out, w_up, w_down = late_weights(late)
    mix = _mm(ycat, w_out, tm=tm, tn=1024, tk=2048, name="mm_mix")[0]
    h1, xn2 = _mid_fwd_call(h0, mix, wt["mix_post_g"], wt["ffn_pre_g"])
    gu = _mm(xn2, w_up, tm=tm, tn=1408, tk=D_MODEL, name="mm_up")[0]
    gpre, act = _conv_fwd_call(gu, 0, D_FF, fcw, fw8, wt["ffn_conv_b"], 3, gate_src=gu, gate_col0=D_FF,
                               name="ffn_conv_fwd")
    f = _mm(act, w_down, tm=tm, tn=1024, tk=1408, name="mm_down")[0]
    loss_row, df, dh2, dg_ffn_post = _final_call(h1, f, wt["ffn_post_g"], target)

    dact = _mm(df, w_down, tb=True, tm=tm, tn=1408, tk=D_MODEL, name="mm_dact")[0]
    dw_down, dw_down_b = _mm(act, df, ta=True, tm=1408, tn=1024, tk=tkr, extra_bf16=True, name="mm_dw_down")
    by_chip = lambda g: g.reshape(N_CHIPS, -1, D_MODEL)
    dgate, dup, dfcw, dfcb = _conv_bwd_call(gu, 0, D_FF, fcw, fw8, 3, gpre, dact, gate_src=gu, gate_col0=D_FF,
                                            name="ffn_conv_bwd")
    dgu = jnp.concatenate([dgate, dup], axis=1)
    dxn2, land_down = _mm(dgu, w_up, tb=True, tm=tm, tn=1024, tk=1408, ride=_scatter_ride(by_chip(dw_down_b)),
                          name="mm_dxn2")
    dw_up, dw_up_b = _mm(xn2, dgu, ta=True, tm=1024, tn=1408, tk=tkr, nsplit=N_CHIPS, extra_bf16=True,
                         name="mm_dw_up")
    dh1, dmix, dg_ffn_pre, dg_mix_post = _mid_bwd_call(dh2, h1, dxn2[0], mix, wt["ffn_pre_g"], wt["mix_post_g"])
    dycat = _mm(dmix, w_out, tb=True, tm=tm, tn=1024, tk=D_MODEL, name="mm_dycat")[0]
    dw_out, dw_out_b = _mm(ycat, dmix, ta=True, tm=1024, tn=1024, tk=tkr, extra_bf16=True, name="mm_dw_out")
    do, dg_sb = _norm_bwd_call(o, wt["sb_norm_g"], dycat, 1, "sb_norm_bwd")
    dq, dk, dv, lands = _sb_bwd_call(proj, tl, do, _join_rides(_scatter_ride(dw_up_b),
                                                                  _scatter_ride(by_chip(dw_out_b))))
    dz, dxbc_act, ddt, dg_ssd, dd_x, dalog, dbias = _ssd_bwd_call(
        dycat, ypre, xbc, proj, sprev, bias, alog, d_x, wt["ssd_norm_g"])
    dxbc, dcw, dcb = _conv_bwd_call(proj, C_XBC, XBC, 512, cw8, 4, conv_pre, dxbc_act, name="ssd_conv_bwd")
    dproj = jnp.concatenate([dz, dxbc, ddt, dq, dk, dv], axis=1)
    dw_in, dw_in_b = w_in_shards(_mm(xn1, dproj, ta=True, tm=1024, tn=1152, tk=tkr, name="mm_dw_in")[0])
    dxn1, land_in = _mm(dproj, wt["w_in"], tb=True, tm=tm, tn=1024, tk=1152, ride=_scatter_ride(dw_in_b),
                        name="mm_dxn1")
    dh0, dg_pre = _norm_bwd_call(h0, wt["mix_pre_g"], dxn1[0], 0, "norm1_bwd", res=dh1)

    small = {
        "meta_tokens": dh0[PAD:BLK], "mix_pre_g": dg_pre, "ssd_conv_w": dcw[:4], "ssd_conv_b": dcb,
        "ssd_dt_bias": dbias[:, :N_HEADS], "ssd_a_log": dalog[:, :N_HEADS],
        "ssd_d": jnp.sum(dd_x.reshape(N_HEADS, HEAD_DIM), axis=1)[None],
        "ssd_norm_g": dg_ssd, "sb_norm_g": dg_sb, "mix_post_g": dg_mix_post, "ffn_pre_g": dg_ffn_pre,
        "ffn_conv_w": dfcw[:3], "ffn_conv_b": dfcb, "ffn_post_g": dg_ffn_post,
    }
    pending = {"w_in": (dw_in, land_in[0]), "w_out": (by_chip(dw_out), lands[1]), "w_up": (dw_up, lands[0]),
               "w_down": (by_chip(dw_down), land_down[0])}
    return loss_row, dh0[BLK:], small, pending


def _adamw_call(w, g, m, v, name):
    rows, cols = w.shape
    tr = 256 if rows % 256 == 0 else (352 if rows % 352 == 0 else rows)
    c1 = 1.0 - ADAM_B1 ** ADAM_STEP
    c2 = 1.0 - ADAM_B2 ** ADAM_STEP

    def body(w_ref, g_ref, m_ref, v_ref, d_ref, mo_ref, vo_ref):
        gv = g_ref[...]
        m2 = ADAM_B1 * m_ref[...] + (1.0 - ADAM_B1) * gv
        v2 = ADAM_B2 * v_ref[...] + (1.0 - ADAM_B2) * (gv * gv)
        d_ref[...] = -ADAM_LR * ((m2 / c1) / (jnp.sqrt(v2 / c2) + ADAM_EPS) + ADAM_WD * w_ref[...])
        mo_ref[...] = m2
        vo_ref[...] = v2

    spec = pl.BlockSpec((tr, cols), lambda i: (i, 0))
    return pl.pallas_call(
        body, name=name, grid=(rows // tr,), in_specs=[spec] * 4, out_specs=[spec] * 3,
        out_shape=[jax.ShapeDtypeStruct((rows, cols), F32)] * 3,
        compiler_params=_cparams(("parallel",)))(w, g, m, v)


ANY = pl.BlockSpec(memory_space=pl.ANY)


def _place():
    x, y, c = lax.axis_index("x"), lax.axis_index("y"), lax.axis_index("c")
    chips = [(1 - x, y), (x, 1 - y), (1 - x, 1 - y)]
    return x, y, c, chips


def _half(c, h):
    return pl.ds(pl.multiple_of(c * h, 8), h)


def _allgather_call(shards):
    n = len(shards)

    def body(*refs):
        ins, outs = refs[:n], refs[n:2 * n]
        send_i, recv_i, send_d, recv_d = refs[2 * n:]
        x, y, c, chips = _place()
        me = 2 * x + y
        sends = []
        for a in range(n):
            h = shards[a].shape[0] // 2
            for j, chip in enumerate(chips):
                cp = pltpu.make_async_remote_copy(
                    src_ref=ins[a].at[_half(c, h)], dst_ref=outs[a].at[me, _half(c, h)],
                    send_sem=send_i.at[3 * a + j], recv_sem=recv_i.at[3 * a + j],
                    device_id=(*chip, c), device_id_type=MESH)
                cp.start()
                sends.append(cp)
        for a in range(n):
            h = shards[a].shape[0] // 2
            for j, chip in enumerate(chips):
                src = 2 * chip[0] + chip[1]
                landed = outs[a].at[src, _half(c, h)]
                pltpu.make_async_remote_copy(
                    src_ref=landed, dst_ref=landed, send_sem=send_i.at[3 * a + j], recv_sem=recv_i.at[3 * a + j],
                    device_id=(*chip, c), device_id_type=MESH).wait_recv()
                cp = pltpu.make_async_remote_copy(
                    src_ref=landed, dst_ref=landed, send_sem=send_d.at[3 * a + j], recv_sem=recv_d.at[3 * a + j],
                    device_id=(x, y, 1 - c), device_id_type=MESH)
                cp.start()
                sends.append(cp)
        for a in range(n):
            h = shards[a].shape[0] // 2
            for j, chip in enumerate(chips):
                src = 2 * chip[0] + chip[1]
                other = outs[a].at[src, _half(1 - c, h)]
                pltpu.make_async_remote_copy(
                    src_ref=other, dst_ref=other, send_sem=send_d.at[3 * a + j], recv_sem=recv_d.at[3 * a + j],
                    device_id=(x, y, 1 - c), device_id_type=MESH).wait_recv()
        for cp in sends:
            cp.wait_send()

    return pl.pallas_call(
        body, name="allgather_weights", in_specs=[ANY] * n, out_specs=[ANY] * n,
        out_shape=[jax.ShapeDtypeStruct((N_CHIPS,) + s.shape, s.dtype) for s in shards],
        scratch_shapes=[pltpu.SemaphoreType.DMA((3 * n,))] * 4,
    )(*shards)


def _ride_scratch(ride):
    return [pltpu.SemaphoreType.DMA((ride[3],)), pltpu.SemaphoreType.DMA((ride[3],))]


def _ride_run(ride, src_refs, land_refs, send, recv, first, last):
    plan = ride[2]

    @pl.when(first)
    def _():
        for k, (src, dst, _, dev) in enumerate(plan(src_refs, land_refs)):
            pltpu.make_async_remote_copy(src_ref=src, dst_ref=dst, send_sem=send.at[k], recv_sem=recv.at[k],
                                         device_id=dev, device_id_type=MESH).start()

    @pl.when(last)
    def _():
        for k, (src, _, land, dev) in enumerate(plan(src_refs, land_refs)):
            cp = pltpu.make_async_remote_copy(src_ref=src, dst_ref=land, send_sem=send.at[k], recv_sem=recv.at[k],
                                              device_id=dev, device_id_type=MESH)
            cp.wait_send()
            cp.wait_recv()


def _join_rides(r1, r2):
    n1, l1 = len(r1[0]), len(r1[1])

    def plan(srcs, lands):
        return r1[2](srcs[:n1], lands[:l1]) + r2[2](srcs[n1:], lands[l1:])

    return (r1[0] + r2[0], r1[1] + r2[1], plan, r1[3] + r2[3])


def _gather_ride(shards):
    return (list(shards), [jax.ShapeDtypeStruct((N_CHIPS,) + s.shape, s.dtype) for s in shards],
            _gather_plan(len(shards)), 3 * len(shards))


def _scatter_ride(g_b):
    h = g_b.shape[1] // 2
    return ([g_b], [jax.ShapeDtypeStruct((8, h, g_b.shape[2]), BF16)], _scatter_plan(h), 7)


def _gather_plan(n):
    def plan(srcs, lands):
        x, y, c, chips = _place()
        me = 2 * x + y
        return [(srcs[a], lands[a].at[me], lands[a].at[2 * chip[0] + chip[1]], (*chip, c))
                for a in range(n) for chip in chips]
    return plan


def _scatter_plan(h):
    def plan(srcs, lands):
        x, y, c, _ = _place()
        me = 4 * x + 2 * y + c
        out = []
        for p in range(1, 8):
            px = 1 - x if p & 4 else x
            py = 1 - y if p & 2 else y
            pc = 1 - c if p & 1 else c
            out.append((srcs[0].at[2 * px + py, _half(pc, h)], lands[0].at[me],
                        lands[0].at[4 * px + 2 * py + pc], (px, py, pc)))
        return out
    return plan


def _grad_sum_call(own, land, place, name):
    _, h, cols = land.shape
    th = _pick(h, [256, 176, 8])
    nt = h // th

    def body(p_ref, own_ref, *refs):
        acc = own_ref[...]
        for r in refs[:7]:
            acc = acc + r[...].astype(F32)
        refs[7][...] = acc

    def peer(k):
        return pl.BlockSpec((None, th, cols), lambda i, p_ref: (p_ref[2 + k], i, 0))

    return pl.pallas_call(
        body, name=name,
        grid_spec=pltpu.PrefetchScalarGridSpec(
            num_scalar_prefetch=1, grid=(nt,),
            in_specs=[pl.BlockSpec((None, th, cols), lambda i, p_ref: (p_ref[1], p_ref[0] * nt + i, 0))]
            + [peer(k) for k in range(7)],
            out_specs=pl.BlockSpec((th, cols), lambda i, p_ref: (p_ref[0] * nt + i, 0))),
        out_shape=jax.ShapeDtypeStruct((2 * h, cols), F32),
        compiler_params=_cparams(("parallel",)))(place, own, *[land] * 7)


def _half_exchange_call(shards):
    n = len(shards)

    def body(*refs):
        outs = refs[n:2 * n]
        send_d, recv_d = refs[2 * n:]
        x, y, c, _ = _place()
        cps = []
        for a in range(n):
            h = shards[a].shape[0] // 2
            mine = outs[a].at[_half(c, h)]
            cp = pltpu.make_async_remote_copy(
                src_ref=mine, dst_ref=mine, send_sem=send_d.at[a], recv_sem=recv_d.at[a],
                device_id=(x, y, 1 - c), device_id_type=MESH)
            cp.start()
            cps.append(cp)
        for a, cp in enumerate(cps):
            h = shards[a].shape[0] // 2
            theirs = outs[a].at[_half(1 - c, h)]
            pltpu.make_async_remote_copy(
                src_ref=theirs, dst_ref=theirs, send_sem=send_d.at[a], recv_sem=recv_d.at[a],
                device_id=(x, y, 1 - c), device_id_type=MESH).wait_recv()
            cp.wait_send()

    return pl.pallas_call(
        body, name="grad_half_exchange", in_specs=[ANY] * n, out_specs=[ANY] * n,
        out_shape=[jax.ShapeDtypeStruct(sv.shape, F32) for sv in shards],
        input_output_aliases={a: a for a in range(n)},
        scratch_shapes=[pltpu.SemaphoreType.DMA((n,))] * 2,
    )(*shards)


def _allreduce_small_call(arrs):
    n = len(arrs)
    offs, rows = [], 0
    for a in arrs:
        offs.append(rows)
        rows += a.shape[0]
    rows = -(-rows // 8) * 8
    width = -(-max(a.shape[1] for a in arrs) // BLK) * BLK

    def body(*refs):
        ins, outs = refs[:n], refs[n:2 * n]
        gath, send_sems, recv_sems = refs[2 * n:]
        x, y, c, chips = _place()
        me, sibling = (x, y, c), (x, y, 1 - c)

        def slot(px, py, pc):
            return gath.at[4 * px + 2 * py + pc]

        def copy(k, block, to):
            return pltpu.make_async_remote_copy(
                src_ref=slot(*block), dst_ref=slot(*block),
                send_sem=send_sems.at[k], recv_sem=recv_sems.at[k], device_id=to, device_id_type=MESH)

        mine = slot(*me)
        mine[...] = jnp.zeros((rows, width), F32)
        for k in range(n):
            r, w = arrs[k].shape
            mine[offs[k]:offs[k] + r, 0:w] = ins[k][...]
        first = [copy(0, me, sibling)]
        first += [copy(1 + j, me, (*chip, c)) for j, chip in enumerate(chips)]
        for cp in first:
            cp.start()
        passed = [copy(4 + j, (*chip, c), sibling) for j, chip in enumerate(chips)]
        for j, chip in enumerate(chips):
            copy(1 + j, (*chip, c), me).wait_recv()
            passed[j].start()
        copy(0, sibling, me).wait_recv()
        for j, chip in enumerate(chips):
            copy(4 + j, (*chip, 1 - c), me).wait_recv()
        for cp in first + passed:
            cp.wait_send()
        acc = gath[0]
        for d in range(1, 8):
            acc = acc + gath[d]
        for k in range(n):
            r, w = arrs[k].shape
            outs[k][...] = acc[offs[k]:offs[k] + r, 0:w]

    vm = pl.BlockSpec(memory_space=pltpu.VMEM)
    return pl.pallas_call(
        body, name="allreduce_small", in_specs=[vm] * n, out_specs=[vm] * n,
        out_shape=[jax.ShapeDtypeStruct(a.shape, F32) for a in arrs],
        scratch_shapes=[pltpu.VMEM((8, rows, width), F32), pltpu.SemaphoreType.DMA((7,)),
                        pltpu.SemaphoreType.DMA((7,))],
        compiler_params=pltpu.CompilerParams(vmem_limit_bytes=VMEM_LIMIT),
    )(*arrs)


def _adamw_small_call(ws, gs, ms, vs):
    n = len(ws)
    c1 = 1.0 - ADAM_B1 ** ADAM_STEP
    c2 = 1.0 - ADAM_B2 ** ADAM_STEP

    def body(*refs):
        for k in range(n):
            w_ref, g_ref, m_ref, v_ref = (refs[j * n + k] for j in range(4))
            d_ref, mo_ref, vo_ref = (refs[(4 + j) * n + k] for j in range(3))
            gv = g_ref[...]
            m2 = ADAM_B1 * m_ref[...] + (1.0 - ADAM_B1) * gv
            v2 = ADAM_B2 * v_ref[...] + (1.0 - ADAM_B2) * (gv * gv)
            d_ref[...] = -ADAM_LR * ((m2 / c1) / (jnp.sqrt(v2 / c2) + ADAM_EPS) + ADAM_WD * w_ref[...])
            mo_ref[...] = m2
            vo_ref[...] = v2

    vm = pl.BlockSpec(memory_space=pltpu.VMEM)
    res = pl.pallas_call(
        body, name="adamw_small", in_specs=[vm] * (4 * n), out_specs=[vm] * (3 * n),
        out_shape=[jax.ShapeDtypeStruct(a.shape, F32) for a in ws] * 3,
        compiler_params=pltpu.CompilerParams(vmem_limit_bytes=VMEM_LIMIT),
    )(*ws, *gs, *ms, *vs)
    return res[:n], res[n:2 * n], res[2 * n:]


def _pack(arrs, min_rows=8):
    parts = []
    for a in arrs:
        flat = a.reshape(-1).astype(F32)
        parts.append(jnp.pad(flat, (0, (-flat.shape[0]) % BLK)))
    buf = jnp.concatenate(parts).reshape(-1, BLK)
    return jnp.pad(buf, ((0, (-buf.shape[0]) % min_rows), (0, 0)))


def _unpack(buf, shapes):
    out, r = [], 0
    for shp in shapes:
        n = math.prod(shp)
        nr = -(-n // BLK)
        out.append(buf[r:r + nr].reshape(-1)[:n].reshape(shp))
        r += nr
    return out


SMALL = ["meta_tokens", "mix_pre_g", "ssd_conv_w", "ssd_conv_b", "ssd_dt_bias", "ssd_a_log", "ssd_d", "ssd_norm_g",
         "sb_norm_g", "mix_post_g", "ffn_pre_g", "ffn_conv_w", "ffn_conv_b", "ffn_post_g"]
BIG = ["w_in", "w_out", "w_up", "w_down"]
WEIGHTS = ["meta_tokens", "mix_pre_g", "w_in", "ssd_conv_w", "ssd_conv_b", "ssd_dt_bias", "ssd_a_log", "ssd_d",
           "ssd_norm_g", "sb_norm_g", "w_out", "mix_post_g", "ffn_pre_g", "w_up", "ffn_conv_w", "ffn_conv_b",
           "w_down", "ffn_post_g"]
W_IN_SHARD = IN_COLS // N_CHIPS
W_IN_PAD = 1536


def kernel(x, meta_tokens, mix_pre_g, w_in, ssd_conv_w, ssd_conv_b, ssd_dt_bias, ssd_a_log, ssd_d, ssd_norm_g, sb_norm_g, w_out, mix_post_g, ffn_pre_g, w_up, ffn_conv_w, ffn_conv_b, w_down, ffn_post_g, loss_target, m_meta_tokens, m_mix_pre_g, m_w_in, m_ssd_conv_w, m_ssd_conv_b, m_ssd_dt_bias, m_ssd_a_log, m_ssd_d, m_ssd_norm_g, m_sb_norm_g, m_w_out, m_mix_post_g, m_ffn_pre_g, m_w_up, m_ffn_conv_w, m_ffn_conv_b, m_w_down, m_ffn_post_g, v_meta_tokens, v_mix_pre_g, v_w_in, v_ssd_conv_w, v_ssd_conv_b, v_ssd_dt_bias, v_ssd_a_log, v_ssd_d, v_ssd_norm_g, v_sb_norm_g, v_w_out, v_mix_post_g, v_ffn_pre_g, v_w_up, v_ffn_conv_w, v_ffn_conv_b, v_w_down, v_ffn_post_g):
    w = dict(meta_tokens=meta_tokens, mix_pre_g=mix_pre_g, w_in=w_in, ssd_conv_w=ssd_conv_w, ssd_conv_b=ssd_conv_b, ssd_dt_bias=ssd_dt_bias, ssd_a_log=ssd_a_log, ssd_d=ssd_d, ssd_norm_g=ssd_norm_g, sb_norm_g=sb_norm_g, w_out=w_out, mix_post_g=mix_post_g, ffn_pre_g=ffn_pre_g, w_up=w_up, ffn_conv_w=ffn_conv_w, ffn_conv_b=ffn_conv_b, w_down=w_down, ffn_post_g=ffn_post_g)
    m = dict(meta_tokens=m_meta_tokens, mix_pre_g=m_mix_pre_g, w_in=m_w_in, ssd_conv_w=m_ssd_conv_w, ssd_conv_b=m_ssd_conv_b, ssd_dt_bias=m_ssd_dt_bias, ssd_a_log=m_ssd_a_log, ssd_d=m_ssd_d, ssd_norm_g=m_ssd_norm_g, sb_norm_g=m_sb_norm_g, w_out=m_w_out, mix_post_g=m_mix_post_g, ffn_pre_g=m_ffn_pre_g, w_up=m_w_up, ffn_conv_w=m_ffn_conv_w, ffn_conv_b=m_ffn_conv_b, w_down=m_w_down, ffn_post_g=m_ffn_post_g)
    v = dict(meta_tokens=v_meta_tokens, mix_pre_g=v_mix_pre_g, w_in=v_w_in, ssd_conv_w=v_ssd_conv_w, ssd_conv_b=v_ssd_conv_b, ssd_dt_bias=v_ssd_dt_bias, ssd_a_log=v_ssd_a_log, ssd_d=v_ssd_d, ssd_norm_g=v_ssd_norm_g, sb_norm_g=v_sb_norm_g, w_out=v_w_out, mix_post_g=v_mix_post_g, ffn_pre_g=v_ffn_pre_g, w_up=v_w_up, ffn_conv_w=v_ffn_conv_w, ffn_conv_b=v_ffn_conv_b, w_down=v_w_down, ffn_post_g=v_ffn_post_g)
    chip = 2 * lax.axis_index("x") + lax.axis_index("y")
    me = 2 * chip + lax.axis_index("c")
    place = jnp.stack([lax.axis_index("c"), chip] + [me ^ p for p in range(1, 8)]).astype(jnp.int32)

    shard_small = [w["meta_tokens"], w["ssd_conv_w"][0], w["ffn_conv_w"][0]]
    shards = [jnp.pad(w["w_in"][0], ((0, 0), (0, W_IN_PAD - W_IN_SHARD))).astype(BF16), _pack(shard_small, 16)]
    gathered = _allgather_call(shards)
    late_shards = [w["w_out"][0].astype(BF16), w["w_up"][0].astype(BF16), w["w_down"][0].astype(BF16)]

    def blocks(own, got):
        return [jnp.where(chip == i, own, got[i]) for i in range(N_CHIPS)]

    def late_weights(got):
        return (jnp.concatenate(blocks(late_shards[0], got[0]), axis=0),
                jnp.concatenate(blocks(late_shards[1], got[1]), axis=1),
                jnp.concatenate(blocks(late_shards[2], got[2]), axis=0))

    cut = DT_REAL_OFF + N_HEADS - W_IN_SHARD
    s_in = blocks(shards[0], gathered[0])
    w_in_c = jnp.concatenate(
        [s_in[0][:, :W_IN_SHARD], s_in[1][:, :cut], jnp.zeros((D_MODEL, BLK - N_HEADS), BF16),
         s_in[1][:, cut:W_IN_SHARD], s_in[2][:, :W_IN_SHARD], s_in[3][:, :W_IN_SHARD]], axis=1)
    parts = [_unpack(b, [s.shape for s in shard_small]) for b in blocks(shards[1], gathered[1])]
    wt = {k: w[k][0][None] if w[k].ndim == 3 else w[k] for k in
          ["mix_pre_g", "ssd_conv_b", "ssd_dt_bias", "ssd_a_log", "ssd_d", "ssd_norm_g", "sb_norm_g", "mix_post_g",
           "ffn_pre_g", "ffn_conv_b", "ffn_post_g"]}
    wt.update(
        meta=jnp.concatenate([p[0] for p in parts], axis=1),
        ssd_conv_w=jnp.concatenate([p[1] for p in parts], axis=1),
        ffn_conv_w=jnp.concatenate([p[2] for p in parts], axis=1), w_in=w_in_c)

    def w_in_shards(g):
        skip = BLK - N_HEADS
        cols = [g[:, :W_IN_SHARD],
                jnp.concatenate([g[:, W_IN_SHARD:W_IN_SHARD + cut], g[:, C_Q:2 * W_IN_SHARD + skip]], axis=1),
                g[:, 2 * W_IN_SHARD + skip:3 * W_IN_SHARD + skip], g[:, 3 * W_IN_SHARD + skip:]]
        g = jnp.stack([jnp.pad(b, ((0, 0), (0, W_IN_PAD - W_IN_SHARD))) for b in cols])
        return g, g.astype(BF16)

    loss_row, dx, small, pending = _local_step(x[0], loss_target[0], wt, late_shards, late_weights, w_in_shards)

    full = _half_exchange_call([_grad_sum_call(*pending[k], place, "grad_sum_" + k) for k in BIG])
    grads = {"w_in": full[0][:, :W_IN_SHARD], "w_out": full[1], "w_up": full[2], "w_down": full[3]}

    red_list = _allreduce_small_call([small[k] for k in SMALL] + [loss_row])
    loss = jnp.sum(red_list[-1])
    for k, g in zip(SMALL, red_list[:-1]):
        grads[k] = g
    for k in ["meta_tokens", "ssd_conv_w", "ffn_conv_w"]:
        wk = w[k].shape[-1]
        grads[k] = lax.dynamic_slice_in_dim(grads[k], chip * wk, wk, axis=1)

    delta, new_m, new_v = {}, {}, {}
    for k in BIG:
        delta[k], new_m[k], new_v[k] = _adamw_call(w[k][0], grads[k], m[k][0], v[k][0], "adamw_" + k)
    flat = lambda d: [d[k].reshape(grads[k].shape) for k in SMALL]
    res = _adamw_small_call(flat(w), [grads[k] for k in SMALL], flat(m), flat(v))
    for out, arrs in zip((delta, new_m, new_v), res):
        for k, a in zip(SMALL, arrs):
            out[k] = a

    def shaped(d, k):
        return d[k].reshape(w[k].shape)

    return (loss, dx[None], *[shaped(grads, k) for k in WEIGHTS], *[shaped(delta, k) for k in WEIGHTS],
            *[shaped(new_m, k) for k in WEIGHTS], *[shaped(new_v, k) for k in WEIGHTS])
```

```python
import functools
import math

import jax
import jax.numpy as jnp
from jax import lax
from jax.experimental import pallas as pl
from jax.experimental.pallas import tpu as pltpu

F32 = jnp.float32
BF16 = jnp.bfloat16

D_MODEL = 1024
N_META = 16
BLK = 128
PAD = BLK - N_META
HEAD_DIM = 64
N_HEADS = 16
SSD_GROUPS = 2
SSD_STATE = 128
SSD_INNER = 1024
XBC = SSD_INNER + 2 * SSD_GROUPS * SSD_STATE
D_FF = 2816
EPS = 1e-6
IN_COLS = 5648
C_Z, C_XBC, C_DT, C_Q, C_K, C_V, C_END = 0, 1024, 2560, 2688, 3712, 4736, 5760
DT_REAL_OFF = 2560
N_CHIPS = 4
ADAM_LR, ADAM_B1, ADAM_B2, ADAM_EPS, ADAM_WD, ADAM_STEP = 0.001, 0.9, 0.999, 1e-08, 0.01, 10
VMEM_LIMIT = 56 * 1024 * 1024
MESH = pl.DeviceIdType.MESH


def _cparams(sem=None, **kw):
    if sem is not None:
        kw["dimension_semantics"] = sem
    return pltpu.CompilerParams(vmem_limit_bytes=VMEM_LIMIT, **kw)


def _pick(n, cands):
    for c in cands:
        if n % c == 0:
            return c
    raise ValueError((n, cands))


def _iota(shape, dim):
    return lax.broadcasted_iota(jnp.int32, shape, dim)


def _sigmoid(x):
    return 1.0 / (1.0 + jnp.exp(-x))


def _split3(v):
    h1 = v.astype(BF16)
    r1 = v - h1.astype(F32)
    h2 = r1.astype(BF16)
    h3 = (r1 - h2.astype(F32)).astype(BF16)
    return h1, h2, h3


def _dot(a, b, ca=1, cb=0):
    return lax.dot_general(a, b, (((ca,), (cb,)), ((), ())), preferred_element_type=F32)


def _dot_sel_r(v, sel, cb=0):
    h1, h2, h3 = _split3(v)
    return _dot(h1, sel, 1, cb) + _dot(h2, sel, 1, cb) + _dot(h3, sel, 1, cb)


def _dot_sel_l(sel, v, ca=1):
    h1, h2, h3 = _split3(v)
    return _dot(sel, h1, ca, 0) + _dot(sel, h2, ca, 0) + _dot(sel, h3, ca, 0)


def _mm(a, b, *, ta=False, tb=False, tm, tn, tk, out_dtype=F32, nsplit=1, extra_bf16=False, ride=None, name):
    K, M = (a.shape if ta else a.shape[::-1])
    N = b.shape[0] if tb else b.shape[1]
    assert M % tm == 0 and N % tn == 0 and K % tk == 0, (name, M, N, K, tm, tn, tk)
    nm, nn, nk = M // tm, N // tn, K // tk
    assert nn % nsplit == 0
    per = nn // nsplit
    a_spec = (pl.BlockSpec((tk, tm), lambda i, j, k: (k, i)) if ta
              else pl.BlockSpec((tm, tk), lambda i, j, k: (i, k)))
    b_spec = (pl.BlockSpec((tn, tk), lambda i, j, k: (j, k)) if tb
              else pl.BlockSpec((tk, tn), lambda i, j, k: (k, j)))
    o_spec = pl.BlockSpec((None, tm, tn), lambda i, j, k: (j // per, i, j % per))
    n_out = 2 if extra_bf16 else 1
    ca, cb = (0 if ta else 1), (1 if tb else 0)
    ns, nl = (len(ride[0]), len(ride[1])) if ride else (0, 0)

    def body(a_ref, b_ref, *rest):
        outs = rest[ns:ns + n_out]
        if ride:
            step = (pl.program_id(0) * nn + pl.program_id(1)) * nk + pl.program_id(2)
            _ride_run(ride, rest[:ns], rest[ns + n_out:ns + n_out + nl], rest[-2], rest[-1],
                      step == 0, step == nm * nn * nk - 1)
        p = _dot(a_ref[...].astype(BF16), b_ref[...].astype(BF16), ca, cb)

        def emit(val):
            outs[0][...] = val.astype(out_dtype)
            if extra_bf16:
                outs[1][...] = val.astype(BF16)

        if nk == 1:
            emit(p)
        else:
            acc = rest[ns + n_out + nl]
            k = pl.program_id(2)

            @pl.when(k == 0)
            def _():
                acc[...] = p

            @pl.when(k > 0)
            def _():
                acc[...] += p

            @pl.when(k == nk - 1)
            def _():
                emit(acc[...])

    shp = (nsplit, M, N // nsplit)
    out_shape = [jax.ShapeDtypeStruct(shp, out_dtype)]
    out_specs = [o_spec]
    if extra_bf16:
        out_shape.append(jax.ShapeDtypeStruct(shp, BF16))
        out_specs.append(o_spec)
    scratch = [pltpu.VMEM((tm, tn), F32)] if nk > 1 else []
    if ride:
        res = pl.pallas_call(
            body, name=name, grid=(nm, nn, nk), in_specs=[a_spec, b_spec] + [ANY] * ns,
            out_specs=out_specs + [ANY] * nl, out_shape=out_shape + list(ride[1]),
            scratch_shapes=scratch + _ride_scratch(ride),
            compiler_params=_cparams(("arbitrary", "arbitrary", "arbitrary")),
        )(a, b, *ride[0])
        return (res[:n_out] if extra_bf16 else res[0]), list(res[n_out:])
    res = pl.pallas_call(
        body, name=name, grid=(nm, nn, nk), in_specs=[a_spec, b_spec], out_specs=out_specs,
        out_shape=out_shape, scratch_shapes=scratch,
        compiler_params=_cparams(("parallel", "parallel", "arbitrary")),
    )(a, b)
    return res if extra_bf16 else res[0]


def _rms_stats(x):
    r = lax.rsqrt(jnp.mean(x * x, axis=-1, keepdims=True) + EPS)
    return r, x * r


def _rms_bwd(x, g, dy):
    r, xh = _rms_stats(x)
    dxh = dy * g
    dx = r * (dxh - xh * jnp.mean(dxh * xh, axis=-1, keepdims=True))
    return dx, jnp.sum(dy * xh, axis=0, keepdims=True)


def _row_spec(tr, w, col=0):
    return pl.BlockSpec((tr, w), lambda i: (i, col))


def _vec_spec(w):
    return pl.BlockSpec((1, w), lambda i: (0, 0))


def _acc_rows(ref, val, i):
    @pl.when(i == 0)
    def _():
        ref[...] = val

    @pl.when(i > 0)
    def _():
        ref[...] += val


def _rms_fwd_call(x, g, name):
    lp, w = x.shape
    tr = _pick(lp, [384, 128])

    def body(x_ref, g_ref, o_ref):
        _, xh = _rms_stats(x_ref[...])
        o_ref[...] = (xh * g_ref[...]).astype(BF16)

    return pl.pallas_call(
        body, name=name, grid=(lp // tr,), in_specs=[_row_spec(tr, w), _vec_spec(w)],
        out_specs=_row_spec(tr, w), out_shape=jax.ShapeDtypeStruct((lp, w), BF16),
        compiler_params=_cparams(("parallel",)))(x, g)


def _mid_fwd_call(h0, mix, g_post, g_pre2):
    lp, w = h0.shape
    tr = _pick(lp, [384, 128])

    def body(h0_ref, mix_ref, gp_ref, g2_ref, h1_ref, xn_ref):
        _, mh = _rms_stats(mix_ref[...])
        h1 = h0_ref[...] + mh * gp_ref[...]
        h1_ref[...] = h1
        _, hh = _rms_stats(h1)
        xn_ref[...] = (hh * g2_ref[...]).astype(BF16)

    return pl.pallas_call(
        body, name="mid_fwd", grid=(lp // tr,),
        in_specs=[_row_spec(tr, w), _row_spec(tr, w), _vec_spec(w), _vec_spec(w)],
        out_specs=[_row_spec(tr, w), _row_spec(tr, w)],
        out_shape=[jax.ShapeDtypeStruct((lp, w), F32), jax.ShapeDtypeStruct((lp, w), BF16)],
        compiler_params=_cparams(("parallel",)))(h0, mix, g_post, g_pre2)


def _final_call(h1, f, g_post, target):
    lp, w = h1.shape
    tr = BLK
    nb = lp // tr

    def body(h1_ref, f_ref, g_ref, t_ref, loss_ref, df_ref, dh_ref, dg_ref):
        i = pl.program_id(0)
        fv = f_ref[...]
        g = g_ref[...]
        _, fh = _rms_stats(fv)
        h2 = h1_ref[...] + fh * g
        diff = jnp.where(i > 0, h2 - t_ref[...], 0.0)
        part = 0.5 * jnp.sum(diff * diff, axis=0, keepdims=True) * (1.0 / w)
        _acc_rows(loss_ref, part, i)
        dh = diff * (1.0 / w)
        dh_ref[...] = dh
        df, dg = _rms_bwd(fv, g, dh)
        df_ref[...] = df.astype(BF16)
        _acc_rows(dg_ref, dg, i)

    t_spec = pl.BlockSpec((tr, w), lambda i: (jnp.maximum(i - 1, 0), 0))
    return pl.pallas_call(
        body, name="final_fwd_bwd", grid=(nb,),
        in_specs=[_row_spec(tr, w), _row_spec(tr, w), _vec_spec(w), t_spec],
        out_specs=[_vec_spec(w), _row_spec(tr, w), _row_spec(tr, w), _vec_spec(w)],
        out_shape=[jax.ShapeDtypeStruct((1, w), F32), jax.ShapeDtypeStruct((lp, w), BF16),
                   jax.ShapeDtypeStruct((lp, w), F32), jax.ShapeDtypeStruct((1, w), F32)],
        compiler_params=_cparams(("arbitrary",)))(h1, f, g_post, target)


def _mid_bwd_call(dh2, h1, dxn2, mix, g_pre2, g_post):
    lp, w = h1.shape
    tr = _pick(lp, [384, 128])

    def body(dh2_ref, h1_ref, dxn_ref, mix_ref, g2_ref, gp_ref, dh1_ref, dmix_ref, dg2_ref, dgp_ref):
        i = pl.program_id(0)
        live = (i * tr + _iota((tr, 1), 0)) >= PAD
        dx, dg2 = _rms_bwd(h1_ref[...], g2_ref[...], dxn_ref[...])
        dh1 = jnp.where(live, dh2_ref[...] + dx, 0.0)
        dh1_ref[...] = dh1
        dmix, dgp = _rms_bwd(mix_ref[...], gp_ref[...], dh1)
        dmix_ref[...] = jnp.where(live, dmix, 0.0).astype(BF16)
        _acc_rows(dg2_ref, dg2, i)
        _acc_rows(dgp_ref, dgp, i)

    rs = _row_spec(tr, w)
    return pl.pallas_call(
        body, name="mid_bwd", grid=(lp // tr,),
        in_specs=[rs, rs, rs, rs, _vec_spec(w), _vec_spec(w)],
        out_specs=[rs, rs, _vec_spec(w), _vec_spec(w)],
        out_shape=[jax.ShapeDtypeStruct((lp, w), F32), jax.ShapeDtypeStruct((lp, w), BF16),
                   jax.ShapeDtypeStruct((1, w), F32), jax.ShapeDtypeStruct((1, w), F32)],
        compiler_params=_cparams(("arbitrary",)))(dh2, h1, dxn2, mix, g_pre2, g_post)


def _norm_bwd_call(x, g, dy_arr, dy_col, name, res=None):
    lp, w = x.shape
    tr = _pick(lp, [384, 128])
    has_res = res is not None

    def body(x_ref, g_ref, dy_ref, *rest):
        i = pl.program_id(0)
        live = (i * tr + _iota((tr, 1), 0)) >= PAD
        dx, dg = _rms_bwd(x_ref[...], g_ref[...], dy_ref[...])
        if has_res:
            dx = dx + rest[0][...]
        out_ref, dg_ref = rest[-2], rest[-1]
        out_ref[...] = jnp.where(live, dx, 0.0)
        _acc_rows(dg_ref, dg, i)

    rs = _row_spec(tr, w)
    ins = [rs, _vec_spec(w), _row_spec(tr, w, dy_col)] + ([rs] if has_res else [])
    args = [x, g, dy_arr] + ([res] if has_res else [])
    return pl.pallas_call(
        body, name=name, grid=(lp // tr,), in_specs=ins, out_specs=[rs, _vec_spec(w)],
        out_shape=[jax.ShapeDtypeStruct((lp, w), F32), jax.ShapeDtypeStruct((1, w), F32)],
        compiler_params=_cparams(("arbitrary",)))(*args)


def _shift_down(cur, prev_tail, s, rows):
    if s == 0:
        return cur
    prev = jnp.tile(prev_tail, (BLK // 8, 1))
    return jnp.where(rows >= s, pltpu.roll(cur, s, 0), pltpu.roll(prev, s, 0))


def _shift_up(cur, next_head, s, rows):
    if s == 0:
        return cur
    nxt = jnp.tile(next_head, (BLK // 8, 1))
    return jnp.where(rows < BLK - s, pltpu.roll(cur, BLK - s, 0), pltpu.roll(nxt, BLK - s, 0))


def _gelu_tanh(x):
    c = math.sqrt(2.0 / math.pi)
    t = jnp.tanh(c * (x + 0.044715 * x * x * x))
    return 0.5 * x * (1.0 + t), t


def _conv_fwd_call(src, col0, width, cw, w8, b, taps, *, gate_src=None, gate_col0=0, name):
    lp = src.shape[0]
    nb, nc = lp // BLK, width // cw
    cb0 = col0 // cw
    ffn = gate_src is not None

    def body(x_ref, w_ref, b_ref, *rest):
        if ffn:
            u_ref, y_ref, a_ref, tail = rest
        else:
            y_ref, a_ref, tail = rest
        i = pl.program_id(1)

        @pl.when(i == 0)
        def _():
            tail[...] = jnp.zeros_like(tail)

        cur = x_ref[...]
        rows = _iota((BLK, cw), 0)
        y = b_ref[...] + w_ref[taps - 1:taps, :] * cur
        pt = tail[...]
        for s in range(1, taps):
            y = y + w_ref[taps - 1 - s:taps - s, :] * _shift_down(cur, pt, s, rows)
        tail[...] = cur[BLK - 8:, :]
        y_ref[...] = y
        if ffn:
            ge, _ = _gelu_tanh(y)
            a_ref[...] = (ge * u_ref[...]).astype(BF16)
        else:
            live = (i * BLK + rows) >= PAD
            a_ref[...] = jnp.where(live, y * _sigmoid(y), 0.0)

    blk = lambda c0: pl.BlockSpec((BLK, cw), lambda j, i: (i, c0 + j))
    ins = [blk(cb0), pl.BlockSpec((8, cw), lambda j, i: (0, j)), pl.BlockSpec((1, cw), lambda j, i: (0, j))]
    args = [src, w8, b]
    if ffn:
        ins.append(blk(gate_col0 // cw))
        args.append(gate_src)
    return pl.pallas_call(
        body, name=name, grid=(nc, nb), in_specs=ins, out_specs=[blk(0), blk(0)],
        out_shape=[jax.ShapeDtypeStruct((lp, width), F32),
                   jax.ShapeDtypeStruct((lp, width), BF16 if ffn else F32)],
        scratch_shapes=[pltpu.VMEM((8, cw), F32)],
        compiler_params=_cparams(("parallel", "arbitrary")))(*args)


def _conv_bwd_call(src, col0, width, cw, w8, taps, ypre, dact, *, gate_src=None, gate_col0=0, name):
    lp = src.shape[0]
    nb, nc = lp // BLK, width // cw
    cb0 = col0 // cw
    ffn = gate_src is not None

    def body(x_ref, w_ref, y_ref, d_ref, *rest):
        if ffn:
            u_ref, dx_ref, du_ref, dw_ref, db_ref, head = rest
        else:
            dx_ref, dw_ref, db_ref, head = rest
        step = pl.program_id(1)
        i = nb - 1 - step

        @pl.when(step == 0)
        def _():
            head[...] = jnp.zeros_like(head)

        rows = _iota((BLK, cw), 0)
        live = (i * BLK + rows) >= PAD
        y = y_ref[...]
        d = d_ref[...]
        if ffn:
            ge, t = _gelu_tanh(y)
            c = math.sqrt(2.0 / math.pi)
            dge = 0.5 * (1.0 + t) + 0.5 * y * (1.0 - t * t) * c * (1.0 + 3.0 * 0.044715 * y * y)
            u = u_ref[...]
            du_ref[...] = jnp.where(live, d * ge, 0.0).astype(BF16)
            dy = jnp.where(live, d * u * dge, 0.0)
        else:
            sg = _sigmoid(y)
            dy = jnp.where(live, d * sg * (1.0 + y * (1.0 - sg)), 0.0)
        x = x_ref[...]
        nh = head[...]
        dx = jnp.zeros_like(dy)
        dws = []
        for s in range(taps):
            sh = _shift_up(dy, nh, s, rows)
            dx = dx + w_ref[taps - 1 - s:taps - s, :] * sh
            dws.append(jnp.sum(x * sh, axis=0, keepdims=True))
        head[...] = dy[:8, :]
        dx_ref[...] = jnp.where(live, dx, 0.0).astype(BF16)
        dw = jnp.concatenate([dws[taps - 1 - k] for k in range(taps)]
                             + [jnp.zeros((8 - taps, cw), F32)], axis=0)
        _acc_rows(dw_ref, dw, step)
        _acc_rows(db_ref, jnp.sum(dy, axis=0, keepdims=True), step)

    blk = lambda c0: pl.BlockSpec((BLK, cw), lambda j, s: (nb - 1 - s, c0 + j))
    ins = [blk(cb0), pl.BlockSpec((8, cw), lambda j, s: (0, j)), blk(0), blk(0)]
    args = [src, w8, ypre, dact]
    outs = [blk(0)]
    oshape = [jax.ShapeDtypeStruct((lp, width), BF16)]
    if ffn:
        ins.append(blk(gate_col0 // cw))
        args.append(gate_src)
        outs.append(blk(0))
        oshape.append(jax.ShapeDtypeStruct((lp, width), BF16))
    outs += [pl.BlockSpec((8, cw), lambda j, s: (0, j)), pl.BlockSpec((1, cw), lambda j, s: (0, j))]
    oshape += [jax.ShapeDtypeStruct((8, width), F32), jax.ShapeDtypeStruct((1, width), F32)]
    return pl.pallas_call(
        body, name=name, grid=(nc, nb), in_specs=ins, out_specs=outs, out_shape=oshape,
        scratch_shapes=[pltpu.VMEM((8, cw), F32)],
        compiler_params=_cparams(("parallel", "arbitrary")))(*args)


SB_GROUP = 2
SB_DEAD = -110.0


def _sb_scores(qm_h, kb):
    z = _dot(qm_h, kb, 1, 1)
    sp = jnp.maximum(z, 0.0) + jnp.log(1.0 + jnp.exp(-jnp.abs(z)))
    return z - sp, -sp


def _sb_valid(i, off, width):
    kpos = off + _iota((BLK, width), 1)
    qpos = i * BLK + _iota((BLK, width), 0)
    return (kpos < qpos) & (kpos >= PAD)


def _dot_tri1(v, tri2):
    r = _dot(v.astype(BF16), tri2[:BLK])
    return r[:, :BLK], r[:, BLK:]


def _sb_groups(i):
    edge = i // SB_GROUP
    return edge, pl.multiple_of(edge * (SB_GROUP * BLK), BLK)


def _tri2(cond):
    t = jnp.concatenate([cond.astype(BF16), jnp.ones((BLK, BLK), BF16)], axis=1)
    return jnp.concatenate([t, t], axis=0)


def _dot_tri(v, tri2):
    hi = v.astype(BF16)
    lo = (v - hi.astype(F32)).astype(BF16)
    r = _dot(jnp.concatenate([hi, lo], axis=1), tri2)
    return r[:, :BLK], r[:, BLK:]


def _sb_fwd_call(proj, ride):
    lp = proj.shape[0]
    nb = lp // BLK
    assert (nb - 1) % SB_GROUP == 0
    scale = 1.0 / math.sqrt(HEAD_DIM)

    ns, nl = len(ride[0]), len(ride[1])
    npair = N_HEADS // 2

    def body(q_ref, k_ref, v_ref, *rest):
        o_ref, tl_ref = rest[ns], rest[ns + 1]
        i = pl.program_id(1)
        step = pl.program_id(0) * nb + i
        _ride_run(ride, rest[:ns], rest[ns + 2:ns + 2 + nl], rest[-2], rest[-1], step == 0, step == npair * nb - 1)
        lane = _iota((2 * BLK, BLK), 1)
        row = _iota((2 * BLK, BLK), 0)
        first = row < BLK
        qrow = row & (BLK - 1)
        q = q_ref[...] * scale
        q2 = jnp.concatenate([q, q], axis=0)
        qm = jnp.where(first == (lane < HEAD_DIM), q2, 0.0).astype(BF16)
        tri = _tri2(_iota((BLK, BLK), 0) > _iota((BLK, BLK), 1))

        def chunk(off, nsub, last_valid, carry):
            width = nsub * BLK
            sls = [slice(b * BLK, (b + 1) * BLK) for b in range(nsub)]
            kb = k_ref[pl.ds(off, width), :].astype(BF16)
            vb = v_ref[pl.ds(off, width), :].astype(BF16)
            lb, lk = _sb_scores(qm, kb)
            lks = [lk[:, sl] for sl in sls]
            first_valid = (off + lane) >= PAD
            lks[0] = jnp.where(first_valid, lks[0], 0.0)
            if last_valid is not None:
                lks[-1] = jnp.where(last_valid, lks[-1], 0.0)
            afters = [_dot_tri(lks[b], tri) for b in range(nsub)]
            run, acc = carry
            ws = [None] * nsub
            for b in reversed(range(nsub)):
                wb = jnp.exp(lb[:, sls[b]] + afters[b][0] + run)
                if b == 0:
                    wb = jnp.where(first_valid, wb, 0.0)
                if last_valid is not None and b == nsub - 1:
                    wb = jnp.where(last_valid, wb, 0.0)
                ws[b] = wb.astype(BF16)
                run = run + afters[b][1]
            w = ws[0] if nsub == 1 else jnp.concatenate(ws, axis=1)
            return run, acc + _dot(w, vb)

        edge, edge_off = _sb_groups(i)
        diag = lane < qrow
        zero = jnp.zeros((2 * BLK, BLK), F32)
        upto = [functools.partial(chunk, edge_off, r, diag) for r in range(1, SB_GROUP + 1)]
        carry = lax.switch(i - edge * SB_GROUP, upto, (zero, zero))

        def interior(state):
            off = pl.multiple_of(state[0] * (SB_GROUP * BLK), BLK)
            return (state[0] - 1, *chunk(off, SB_GROUP, None, state[1:]))

        def live(state):
            return jnp.logical_and(state[0] >= 0, jnp.max(state[1]) > SB_DEAD)

        below, run, acc = lax.while_loop(live, interior, (edge - 1, *carry))
        low = lane[:BLK] < HEAD_DIM
        o_ref[...] = jnp.where(low, acc[:BLK], acc[BLK:])
        tl = jnp.where(low, run[:BLK], run[BLK:])
        tl_ref[...] = jnp.where(lane[:BLK] == 1, (below + 1).astype(F32), tl)

    qc, kc, vc = C_Q // BLK, C_K // BLK, C_V // BLK
    blk = pl.BlockSpec((BLK, BLK), lambda p, i: (i, p))
    res = pl.pallas_call(
        body, name="sb_fwd", grid=(npair, nb),
        in_specs=[pl.BlockSpec((BLK, BLK), lambda p, i: (i, qc + p)),
                  pl.BlockSpec((lp, BLK), lambda p, i: (0, kc + p)),
                  pl.BlockSpec((lp, BLK), lambda p, i: (0, vc + p))] + [ANY] * ns,
        out_specs=[blk, blk] + [ANY] * nl,
        out_shape=[jax.ShapeDtypeStruct((lp, N_HEADS * HEAD_DIM), F32)] * 2 + list(ride[1]),
        scratch_shapes=_ride_scratch(ride),
        compiler_params=_cparams(("arbitrary", "arbitrary")))(proj, proj, proj, *ride[0])
    return res[0], res[1], list(res[2:])


def _sb_bwd_call(proj, tl, do, ride):
    lp = proj.shape[0]
    nb = lp // BLK
    assert (nb - 1) % SB_GROUP == 0
    scale = 1.0 / math.sqrt(HEAD_DIM)

    ns, nl = len(ride[0]), len(ride[1])
    npair = N_HEADS // 2

    def body(q_ref, k_ref, v_ref, tl_ref, do_ref, *rest):
        dq_ref, dk_ref, dv_ref = rest[ns:ns + 3]
        dk_acc, dv_acc = rest[ns + 3 + nl:ns + 5 + nl]
        i = pl.program_id(1)
        step = pl.program_id(0) * nb + i
        _ride_run(ride, rest[:ns], rest[ns + 3:ns + 3 + nl], rest[-2], rest[-1], step == 0, step == npair * nb - 1)

        @pl.when(i == 0)
        def _():
            dk_acc[...] = jnp.zeros_like(dk_acc)
            dv_acc[...] = jnp.zeros_like(dv_acc)

        lane = _iota((2 * BLK, BLK), 1)
        row = _iota((2 * BLK, BLK), 0)
        qrow = row & (BLK - 1)
        mine = (row < BLK) == (lane < HEAD_DIM)
        q = q_ref[...] * scale
        dov = do_ref[...]
        qm = jnp.where(mine, jnp.concatenate([q, q], axis=0), 0.0).astype(BF16)
        dom = jnp.where(mine, jnp.concatenate([dov, dov], axis=0), 0.0).astype(BF16)
        tlv = tl_ref[...]
        tot = jnp.concatenate([jnp.broadcast_to(tlv[:, 0:1], (BLK, BLK)),
                               jnp.broadcast_to(tlv[:, HEAD_DIM:HEAD_DIM + 1], (BLK, BLK))], axis=0)
        r1, l1 = _iota((BLK, BLK), 0), _iota((BLK, BLK), 1)
        tri_in = _tri2(r1 <= l1)
        tri_ex = _tri2(r1 < l1)

        def chunk(off, nsub, last_valid, carry):
            width = nsub * BLK
            sls = [slice(b * BLK, (b + 1) * BLK) for b in range(nsub)]
            cat = lambda parts: parts[0] if nsub == 1 else jnp.concatenate(parts, axis=1)
            mask_last = lambda b: last_valid is not None and b == nsub - 1
            kb = k_ref[pl.ds(off, width), :].astype(BF16)
            vb = v_ref[pl.ds(off, width), :].astype(BF16)
            lb, lk = _sb_scores(qm, kb)
            dw = _dot(dom, vb, 1, 1)
            lks = [lk[:, sl] for sl in sls]
            first_valid = (off + lane) >= PAD
            lks[0] = jnp.where(first_valid, lks[0], 0.0)
            if last_valid is not None:
                lks[-1] = jnp.where(last_valid, lks[-1], 0.0)
            pins = [_dot_tri(lks[b], tri_in) for b in range(nsub)]
            run, gsum, dq = carry
            ws, gs = [], []
            for b in range(nsub):
                wb = jnp.exp(lb[:, sls[b]] + (tot - run - pins[b][0]))
                if b == 0:
                    wb = jnp.where(first_valid, wb, 0.0)
                if mask_last(b):
                    wb = jnp.where(last_valid, wb, 0.0)
                ws.append(wb.astype(BF16))
                gs.append(wb * dw[:, sls[b]])
                run = run + pins[b][1]
            gexs = [_dot_tri1(gs[b], tri_ex) for b in range(nsub)]
            beta = jnp.exp(lb)
            parts = []
            for b in range(nsub):
                bt = beta[:, sls[b]]
                dzb = gs[b] * (1.0 - bt) - (gsum + gexs[b][0]) * bt
                if b == 0:
                    dzb = jnp.where(first_valid, dzb, 0.0)
                if mask_last(b):
                    dzb = jnp.where(last_valid, dzb, 0.0)
                parts.append(dzb.astype(BF16))
                gsum = gsum + gexs[b][1]
            dz, w = cat(parts), cat(ws)
            dk_acc[pl.ds(off, width), :] += _dot(dz, qm, 0, 0)
            dv_acc[pl.ds(off, width), :] += _dot(w, dom, 0, 0)
            return run, gsum, dq + _dot(dz, kb)

        edge, edge_off = _sb_groups(i)
        diag = lane < qrow
        zero = jnp.zeros((2 * BLK, BLK), F32)
        first = jnp.max(tlv[:, 1:2]).astype(jnp.int32)

        def interior(g, carry):
            return chunk(pl.multiple_of(g * (SB_GROUP * BLK), BLK), SB_GROUP, None, carry)

        carry = lax.fori_loop(first, edge, interior, (zero, zero, zero))
        upto = [functools.partial(chunk, edge_off, r, diag) for r in range(1, SB_GROUP + 1)]
        dq = lax.switch(i - edge * SB_GROUP, upto, carry)[2]
        dq_ref[...] = (jnp.where(lane[:BLK] < HEAD_DIM, dq[:BLK], dq[BLK:]) * scale).astype(BF16)

        @pl.when(i == nb - 1)
        def _():
            dk_ref[...] = dk_acc[...].astype(BF16)
            dv_ref[...] = dv_acc[...].astype(BF16)

    qc, kc, vc = C_Q // BLK, C_K // BLK, C_V // BLK
    blk = pl.BlockSpec((BLK, BLK), lambda p, i: (i, p))
    full = pl.BlockSpec((lp, BLK), lambda p, i: (0, p))
    w = N_HEADS * HEAD_DIM
    res = pl.pallas_call(
        body, name="sb_bwd", grid=(npair, nb),
        in_specs=[pl.BlockSpec((BLK, BLK), lambda p, i: (i, qc + p)),
                  pl.BlockSpec((lp, BLK), lambda p, i: (0, kc + p)),
                  pl.BlockSpec((lp, BLK), lambda p, i: (0, vc + p)),
                  blk, blk] + [ANY] * ns,
        out_specs=[blk, full, full] + [ANY] * nl,
        out_shape=[jax.ShapeDtypeStruct((lp, w), BF16)] * 3 + list(ride[1]),
        scratch_shapes=[pltpu.VMEM((lp, BLK), F32), pltpu.VMEM((lp, BLK), F32)] + _ride_scratch(ride),
        compiler_params=_cparams(("arbitrary", "arbitrary")))(proj, proj, proj, tl, do, *ride[0])
    return res[0], res[1], res[2], list(res[3:])


def _log1p(e):
    u = 1.0 + e
    return jnp.where(u == 1.0, e, jnp.log(u) * e / jnp.where(u == 1.0, 1.0, u - 1.0))


def _ssd_common(c, dtr, bias, alog):
    row = _iota((BLK, BLK), 0)
    lane = _iota((BLK, BLK), 1)
    live = ((c * BLK + row) >= PAD) & (lane < N_HEADS)
    pre = dtr + bias
    dt = jnp.where(live, jnp.maximum(pre, 0.0) + _log1p(jnp.exp(-jnp.abs(pre))), 0.0)
    a_neg = -jnp.exp(alog)
    a = dt * a_neg
    t_in = (lane <= row).astype(BF16)
    cs = _dot_sel_l(t_in, a)
    cs_t = cs.T
    cs_end = cs[BLK - 1:BLK, :]
    e = jnp.exp(cs)
    f = jnp.exp(cs_end - cs)
    xp = ((_iota((BLK, SSD_INNER), 1) // HEAD_DIM) == _iota((BLK, SSD_INNER), 0)).astype(BF16)
    xp_t = ((_iota((SSD_INNER, BLK), 0) // HEAD_DIM) == _iota((SSD_INNER, BLK), 1)).astype(BF16)
    decay_col = _dot_sel_l(xp_t, jnp.exp(cs_t))[:, BLK - 1:BLK]
    return dict(live=live, pre=pre, dt=dt, a_neg=a_neg, cs=cs, cs_t=cs_t, e=e, f=f, xp=xp, xp_t=xp_t,
                decay_col=decay_col, row=row, lane=lane,
                dt_x=_dot_sel_r(dt, xp), e_x=_dot_sel_r(e, xp), f_x=_dot_sel_r(f, xp))


def _ssd_ldec(q, h):
    diff = q["cs"][:, h:h + 1] - q["cs_t"][h:h + 1, :]
    causal = q["row"] >= q["lane"]
    return jnp.where(causal, jnp.exp(jnp.where(causal, diff, 0.0)), 0.0)


def _ssd_fwd_call(xbc, proj, bias, alog, d_x, norm_g):
    lp = xbc.shape[0]
    nb = lp // BLK
    gw = SSD_INNER // SSD_GROUPS
    ppg = gw // BLK

    def body(xbc_ref, dtr_ref, z_ref, bias_ref, alog_ref, dx_ref, ng_ref, yb_ref, ypre_ref, sprev_ref, s_ref):
        c = pl.program_id(0)

        @pl.when(c == 0)
        def _():
            s_ref[...] = jnp.zeros_like(s_ref)

        q = _ssd_common(c, dtr_ref[...], bias_ref[...], alog_ref[...])
        x = xbc_ref[:, 0:SSD_INNER]
        xd = x * q["dt_x"]
        low = q["lane"] < HEAD_DIM
        s_old = s_ref[...]
        sprev_ref[...] = s_old
        xdf = (xd * q["f_x"]).astype(BF16)
        for g in range(SSD_GROUPS):
            bg = xbc_ref[:, SSD_INNER + g * SSD_STATE:SSD_INNER + (g + 1) * SSD_STATE].astype(BF16)
            cg = xbc_ref[:, SSD_INNER + (SSD_GROUPS + g) * SSD_STATE:
                         SSD_INNER + (SSD_GROUPS + g + 1) * SSD_STATE].astype(BF16)
            cb = _dot(cg, bg, 1, 1)
            gs = slice(g * gw, (g + 1) * gw)
            y_off = _dot(cg, s_old[gs, :].astype(BF16), 1, 1) * q["e_x"][:, gs]
            s_ref[gs, :] = s_old[gs, :] * q["decay_col"][gs, :] + _dot(xdf[:, gs], bg, 0, 0)
            for pr in range(ppg):
                cols = slice(g * gw + pr * BLK, g * gw + (pr + 1) * BLK)
                xd_p = xd[:, cols]
                acc = y_off[:, pr * BLK:(pr + 1) * BLK]
                for hh in range(2):
                    h = (g * gw + pr * BLK) // HEAD_DIM + hh
                    m = (cb * _ssd_ldec(q, h)).astype(BF16)
                    xm = jnp.where(low, xd_p, 0.0) if hh == 0 else jnp.where(low, 0.0, xd_p)
                    acc = acc + _dot(m, xm.astype(BF16))
                ypre_ref[:, cols] = acc
        ypre = ypre_ref[...] + x * dx_ref[...]
        ypre_ref[...] = ypre
        z = z_ref[...]
        yg = ypre * (z * _sigmoid(z))
        _, yh = _rms_stats(yg)
        yb_ref[...] = (yh * ng_ref[...]).astype(BF16)

    row = lambda w, col: pl.BlockSpec((BLK, w), lambda c: (c, col))
    vec = lambda w: pl.BlockSpec((1, w), lambda c: (0, 0))
    return pl.pallas_call(
        body, name="ssd_fwd", grid=(nb,),
        in_specs=[row(XBC, 0), row(BLK, C_DT // BLK), row(SSD_INNER, 0), vec(BLK), vec(BLK),
                  vec(SSD_INNER), vec(SSD_INNER)],
        out_specs=[row(SSD_INNER, 0), row(SSD_INNER, 0),
                   pl.BlockSpec((None, SSD_INNER, SSD_STATE), lambda c: (c, 0, 0))],
        out_shape=[jax.ShapeDtypeStruct((lp, SSD_INNER), BF16), jax.ShapeDtypeStruct((lp, SSD_INNER), F32),
                   jax.ShapeDtypeStruct((nb, SSD_INNER, SSD_STATE), F32)],
        scratch_shapes=[pltpu.VMEM((SSD_INNER, SSD_STATE), F32)],
        compiler_params=_cparams(("arbitrary",)))(xbc, proj, proj, bias, alog, d_x, norm_g)


def _ssd_bwd_call(dycat, ypre, xbc, proj, sprev, bias, alog, d_x, norm_g):
    lp = xbc.shape[0]
    nb = lp // BLK
    gw = SSD_INNER // SSD_GROUPS
    ppg = gw // BLK

    def body(dy_ref, ypre_ref, xbc_ref, dtr_ref, z_ref, sp_ref, bias_ref, alog_ref, dxp_ref, ng_ref,
             dz_ref, dxbc_ref, ddt_ref, dng_ref, dd_ref, dal_ref, dbi_ref, ds_ref, dxd_ref):
        step = pl.program_id(0)
        c = nb - 1 - step

        @pl.when(step == 0)
        def _():
            ds_ref[...] = jnp.zeros_like(ds_ref)

        q = _ssd_common(c, dtr_ref[...], bias_ref[...], alog_ref[...])
        row, lane = q["row"], q["lane"]
        low = lane < HEAD_DIM
        rowlive = ((c * BLK + _iota((BLK, 1), 0)) >= PAD)
        x = xbc_ref[:, 0:SSD_INNER]
        xd = x * q["dt_x"]
        z = z_ref[...]
        sz = _sigmoid(z)
        silu = z * sz
        ypre = ypre_ref[...]
        dyg, dng = _rms_bwd(ypre * silu, ng_ref[...], dy_ref[...])
        _acc_rows(dng_ref, dng, step)
        dyp = dyg * silu
        dz_ref[...] = jnp.where(rowlive, dyg * ypre * (sz * (1.0 + z * (1.0 - sz))), 0.0).astype(BF16)
        _acc_rows(dd_ref, jnp.sum(dyp * x, axis=0, keepdims=True), step)
        dye = dyp * q["e_x"]
        xdf = xd * q["f_x"]
        s_prev = sp_ref[...]
        ds_old = ds_ref[...]
        qrow = jnp.zeros((BLK, BLK), F32)
        qcol_t = jnp.zeros((BLK, BLK), F32)
        red_e = []
        red_f = []
        for g in range(SSD_GROUPS):
            gs = slice(g * gw, (g + 1) * gw)
            bsl = slice(SSD_INNER + g * SSD_STATE, SSD_INNER + (g + 1) * SSD_STATE)
            csl = slice(SSD_INNER + (SSD_GROUPS + g) * SSD_STATE, SSD_INNER + (SSD_GROUPS + g + 1) * SSD_STATE)
            bg = xbc_ref[:, bsl].astype(BF16)
            cg = xbc_ref[:, csl].astype(BF16)
            sg = s_prev[gs, :].astype(BF16)
            dsg = ds_old[gs, :].astype(BF16)
            cb = _dot(cg, bg, 1, 1)
            bds = _dot(bg, dsg, 1, 1)
            y_off = _dot(cg, sg, 1, 1) * q["e_x"][:, gs]
            red_e.append(dyp[:, gs] * y_off)
            red_f.append(xd[:, gs] * bds * q["f_x"][:, gs])
            dc = _dot(dye[:, gs].astype(BF16), sg)
            db = _dot(xdf[:, gs].astype(BF16), dsg)
            ds_ref[gs, :] = ds_old[gs, :] * q["decay_col"][gs, :] + _dot(dye[:, gs].astype(BF16), cg, 0, 0)
            dcb = jnp.zeros((BLK, BLK), F32)
            for pr in range(ppg):
                cols = slice(g * gw + pr * BLK, g * gw + (pr + 1) * BLK)
                xd_p = xd[:, cols].astype(BF16)
                dy_p = dyp[:, cols]
                acc = q["f_x"][:, cols] * bds[:, pr * BLK:(pr + 1) * BLK]
                for hh in range(2):
                    h = (g * gw + pr * BLK) // HEAD_DIM + hh
                    ld = _ssd_ldec(q, h)
                    m = cb * ld
                    dym = (jnp.where(low, dy_p, 0.0) if hh == 0 else jnp.where(low, 0.0, dy_p)).astype(BF16)
                    dm = jnp.where(row >= lane, _dot(dym, xd_p, 1, 1), 0.0)
                    acc = acc + _dot(m.astype(BF16), dym, 0, 0)
                    dcb = dcb + dm * ld
                    qq = dm * m
                    qrow = qrow + jnp.where(lane == h, jnp.sum(qq, axis=1, keepdims=True), 0.0)
                    qcol_t = qcol_t + jnp.where(row == h, jnp.sum(qq, axis=0, keepdims=True), 0.0)
                dxd_ref[:, cols] = acc
            dcbb = dcb.astype(BF16)
            dxbc_ref[:, bsl] = jnp.where(rowlive, db + _dot(dcbb, cg, 0, 0), 0.0)
            dxbc_ref[:, csl] = jnp.where(rowlive, dc + _dot(dcbb, bg), 0.0)
        dxd = dxd_ref[...]
        dxbc_ref[:, 0:SSD_INNER] = jnp.where(rowlive, dxd * q["dt_x"] + dyp * dxp_ref[...], 0.0)
        xp_t = q["xp_t"]
        fw = _dot_sel_r(jnp.concatenate(red_f, axis=1), xp_t)
        dcs = qrow - qcol_t.T + _dot_sel_r(jnp.concatenate(red_e, axis=1), xp_t) - fw
        end_f = jnp.sum(fw, axis=0, keepdims=True)
        sds = jnp.sum(ds_old * s_prev, axis=1, keepdims=True)
        per_head = _dot_sel_l(q["xp"], jnp.broadcast_to(sds, (SSD_INNER, BLK)))
        end_e = per_head.T[0:1, :] * jnp.exp(q["cs"][BLK - 1:BLK, :])
        dcs = dcs + jnp.where(row == BLK - 1, end_f + end_e, 0.0)
        t_up = (lane >= row).astype(BF16)
        da = _dot_sel_l(t_up, dcs)
        ddt = da * q["a_neg"] + _dot_sel_r(dxd * x, xp_t)
        _acc_rows(dal_ref, jnp.sum(da * q["dt"] * q["a_neg"], axis=0, keepdims=True), step)
        ddtr = jnp.where(q["live"], ddt * _sigmoid(q["pre"]), 0.0)
        ddt_ref[...] = ddtr.astype(BF16)
        _acc_rows(dbi_ref, jnp.sum(ddtr, axis=0, keepdims=True), step)

    row_s = lambda w, col: pl.BlockSpec((BLK, w), lambda s: (nb - 1 - s, col))
    vec = lambda w: pl.BlockSpec((1, w), lambda s: (0, 0))
    return pl.pallas_call(
        body, name="ssd_bwd", grid=(nb,),
        in_specs=[row_s(SSD_INNER, 0), row_s(SSD_INNER, 0), row_s(XBC, 0), row_s(BLK, C_DT // BLK),
                  row_s(SSD_INNER, 0), pl.BlockSpec((None, SSD_INNER, SSD_STATE), lambda s: (nb - 1 - s, 0, 0)),
                  vec(BLK), vec(BLK), vec(SSD_INNER), vec(SSD_INNER)],
        out_specs=[row_s(SSD_INNER, 0), row_s(XBC, 0), row_s(BLK, 0),
                   vec(SSD_INNER), vec(SSD_INNER), vec(BLK), vec(BLK)],
        out_shape=[jax.ShapeDtypeStruct((lp, SSD_INNER), BF16), jax.ShapeDtypeStruct((lp, XBC), F32),
                   jax.ShapeDtypeStruct((lp, BLK), BF16),
                   jax.ShapeDtypeStruct((1, SSD_INNER), F32), jax.ShapeDtypeStruct((1, SSD_INNER), F32),
                   jax.ShapeDtypeStruct((1, BLK), F32), jax.ShapeDtypeStruct((1, BLK), F32)],
        scratch_shapes=[pltpu.VMEM((SSD_INNER, SSD_STATE), F32), pltpu.VMEM((BLK, SSD_INNER), F32)],
        compiler_params=_cparams(("arbitrary",)))(dycat, ypre, xbc, proj, proj, sprev, bias, alog, d_x, norm_g)


def _pad_rows8(w):
    return jnp.pad(w, ((0, 8 - w.shape[0]), (0, 0)))


def _pad_lanes(v, n=BLK):
    return jnp.pad(v, ((0, 0), (0, n - v.shape[1])))


def _local_step(x, target, wt, late_shards, late_weights, w_in_shards):
    seq = x.shape[0]
    lp = seq + BLK
    tm = _pick(lp, [1408, 768, 384, 128])
    tkr = _pick(lp, [1408, 384, 128])
    h0 = jnp.concatenate([jnp.zeros((PAD, D_MODEL), F32), wt["meta"], x], axis=0)
    bias = _pad_lanes(wt["ssd_dt_bias"])
    alog = _pad_lanes(wt["ssd_a_log"])
    d_x = jnp.repeat(wt["ssd_d"], HEAD_DIM, axis=1)
    cw8 = _pad_rows8(wt["ssd_conv_w"])
    fw8 = _pad_rows8(wt["ffn_conv_w"])
    fcw = D_FF // 2

    xn1 = _rms_fwd_call(h0, wt["mix_pre_g"], "norm1")
    proj, late_a = _mm(xn1, wt["w_in"], tm=tm, tn=1152, tk=D_MODEL, ride=_gather_ride(late_shards[0::2]),
                       name="mm_proj")
    proj = proj[0]
    conv_pre, xbc = _conv_fwd_call(proj, C_XBC, XBC, 512, cw8, wt["ssd_conv_b"], 4, name="ssd_conv_fwd")
    y_ssd, ypre, sprev = _ssd_fwd_call(xbc, proj, bias, alog, d_x, wt["ssd_norm_g"])
    o, tl, late_b = _sb_fwd_call(proj, _gather_ride(late_shards[1:2]))
    y_sb = _rms_fwd_call(o, wt["sb_norm_g"], "sb_norm")
    ycat = jnp.concatenate([y_ssd, y_sb], axis=1)
    w_out, w_up, w_down = late_weights([late_a[0], late_b[0], late_a[1]])
    mix = _mm(ycat, w_out, tm=tm, tn=1024, tk=2048, name="mm_mix")[0]
    h1, xn2 = _mid_fwd_call(h0, mix, wt["mix_post_g"], wt["ffn_pre_g"])
    gu = _mm(xn2, w_up, tm=tm, tn=1408, tk=D_MODEL, name="mm_up")[0]
    gpre, act = _conv_fwd_call(gu, 0, D_FF, fcw, fw8, wt["ffn_conv_b"], 3, gate_src=gu, gate_col0=D_FF,
                               name="ffn_conv_fwd")
    f = _mm(act, w_down, tm=tm, tn=1024, tk=1408, name="mm_down")[0]
    loss_row, df, dh2, dg_ffn_post = _final_call(h1, f, wt["ffn_post_g"], target)

    dact = _mm(df, w_down, tb=True, tm=tm, tn=1408, tk=D_MODEL, name="mm_dact")[0]
    dw_down, dw_down_b = _mm(act, df, ta=True, tm=1408, tn=1024, tk=tkr, extra_bf16=True, name="mm_dw_down")
    by_chip = lambda g: g.reshape(N_CHIPS, -1, D_MODEL)
    dgate, dup, dfcw, dfcb = _conv_bwd_call(gu, 0, D_FF, fcw, fw8, 3, gpre, dact, gate_src=gu, gate_col0=D_FF,
                                            name="ffn_conv_bwd")
    dgu = jnp.concatenate([dgate, dup], axis=1)
    dxn2, land_down = _mm(dgu, w_up, tb=True, tm=tm, tn=1024, tk=1408, ride=_scatter_ride(by_chip(dw_down_b)),
                          name="mm_dxn2")
    dw_up, dw_up_b = _mm(xn2, dgu, ta=True, tm=1024, tn=1408, tk=tkr, nsplit=N_CHIPS, extra_bf16=True,
                         name="mm_dw_up")
    dh1, dmix, dg_ffn_pre, dg_mix_post = _mid_bwd_call(dh2, h1, dxn2[0], mix, wt["ffn_pre_g"], wt["mix_post_g"])
    dycat = _mm(dmix, w_out, tb=True, tm=tm, tn=1024, tk=D_MODEL, name="mm_dycat")[0]
    dw_out, dw_out_b = _mm(ycat, dmix, ta=True, tm=1024, tn=1024, tk=tkr, extra_bf16=True, name="mm_dw_out")
    do, dg_sb = _norm_bwd_call(o, wt["sb_norm_g"], dycat, 1, "sb_norm_bwd")
    dq, dk, dv, lands = _sb_bwd_call(proj, tl, do, _join_rides(_scatter_ride(dw_up_b),
                                                                  _scatter_ride(by_chip(dw_out_b))))
    dz, dxbc_act, ddt, dg_ssd, dd_x, dalog, dbias = _ssd_bwd_call(
        dycat, ypre, xbc, proj, sprev, bias, alog, d_x, wt["ssd_norm_g"])
    dxbc, dcw, dcb = _conv_bwd_call(proj, C_XBC, XBC, 512, cw8, 4, conv_pre, dxbc_act, name="ssd_conv_bwd")
    dproj = jnp.concatenate([dz, dxbc, ddt, dq, dk, dv], axis=1)
    dw_in, dw_in_b = w_in_shards(_mm(xn1, dproj, ta=True, tm=1024, tn=1152, tk=tkr, name="mm_dw_in")[0])
    dxn1, land_in = _mm(dproj, wt["w_in"], tb=True, tm=tm, tn=1024, tk=1152, ride=_scatter_ride(dw_in_b),
                        name="mm_dxn1")
    dh0, dg_pre = _norm_bwd_call(h0, wt["mix_pre_g"], dxn1[0], 0, "norm1_bwd", res=dh1)

    small = {
        "meta_tokens": dh0[PAD:BLK], "mix_pre_g": dg_pre, "ssd_conv_w": dcw[:4], "ssd_conv_b": dcb,
        "ssd_dt_bias": dbias[:, :N_HEADS], "ssd_a_log": dalog[:, :N_HEADS],
        "ssd_d": jnp.sum(dd_x.reshape(N_HEADS, HEAD_DIM), axis=1)[None],
        "ssd_norm_g": dg_ssd, "sb_norm_g": dg_sb, "mix_post_g": dg_mix_post, "ffn_pre_g": dg_ffn_pre,
        "ffn_conv_w": dfcw[:3], "ffn_conv_b": dfcb, "ffn_post_g": dg_ffn_post,
    }
    pending = {"w_in": (dw_in, land_in[0]), "w_out": (by_chip(dw_out), lands[1]), "w_up": (dw_up, lands[0]),
               "w_down": (by_chip(dw_down), land_down[0])}
    return loss_row, dh0[BLK:], small, pending


def _adamw_call(w, g, m, v, name):
    rows, cols = w.shape
    tr = 256 if rows % 256 == 0 else (352 if rows % 352 == 0 else rows)
    c1 = 1.0 - ADAM_B1 ** ADAM_STEP
    c2 = 1.0 - ADAM_B2 ** ADAM_STEP

    def body(w_ref, g_ref, m_ref, v_ref, d_ref, mo_ref, vo_ref):
        gv = g_ref[...]
        m2 = ADAM_B1 * m_ref[...] + (1.0 - ADAM_B1) * gv
        v2 = ADAM_B2 * v_ref[...] + (1.0 - ADAM_B2) * (gv * gv)
        d_ref[...] = -ADAM_LR * ((m2 / c1) / (jnp.sqrt(v2 / c2) + ADAM_EPS) + ADAM_WD * w_ref[...])
        mo_ref[...] = m2
        vo_ref[...] = v2

    spec = pl.BlockSpec((tr, cols), lambda i: (i, 0))
    return pl.pallas_call(
        body, name=name, grid=(rows // tr,), in_specs=[spec] * 4, out_specs=[spec] * 3,
        out_shape=[jax.ShapeDtypeStruct((rows, cols), F32)] * 3,
        compiler_params=_cparams(("parallel",)))(w, g, m, v)


ANY = pl.BlockSpec(memory_space=pl.ANY)


def _place():
    x, y, c = lax.axis_index("x"), lax.axis_index("y"), lax.axis_index("c")
    chips = [(1 - x, y), (x, 1 - y), (1 - x, 1 - y)]
    return x, y, c, chips


def _half(c, h):
    return pl.ds(pl.multiple_of(c * h, 8), h)


def _allgather_call(shards):
    n = len(shards)

    def body(*refs):
        ins, outs = refs[:n], refs[n:2 * n]
        send_i, recv_i, send_d, recv_d = refs[2 * n:]
        x, y, c, chips = _place()
        me = 2 * x + y
        sends = []
        for a in range(n):
            h = shards[a].shape[0] // 2
            for j, chip in enumerate(chips):
                cp = pltpu.make_async_remote_copy(
                    src_ref=ins[a].at[_half(c, h)], dst_ref=outs[a].at[me, _half(c, h)],
                    send_sem=send_i.at[3 * a + j], recv_sem=recv_i.at[3 * a + j],
                    device_id=(*chip, c), device_id_type=MESH)
                cp.start()
                sends.append(cp)
        for a in range(n):
            h = shards[a].shape[0] // 2
            for j, chip in enumerate(chips):
                src = 2 * chip[0] + chip[1]
                landed = outs[a].at[src, _half(c, h)]
                pltpu.make_async_remote_copy(
                    src_ref=landed, dst_ref=landed, send_sem=send_i.at[3 * a + j], recv_sem=recv_i.at[3 * a + j],
                    device_id=(*chip, c), device_id_type=MESH).wait_recv()
                cp = pltpu.make_async_remote_copy(
                    src_ref=landed, dst_ref=landed, send_sem=send_d.at[3 * a + j], recv_sem=recv_d.at[3 * a + j],
                    device_id=(x, y, 1 - c), device_id_type=MESH)
                cp.start()
                sends.append(cp)
        for a in range(n):
            h = shards[a].shape[0] // 2
            for j, chip in enumerate(chips):
                src = 2 * chip[0] + chip[1]
                other = outs[a].at[src, _half(1 - c, h)]
                pltpu.make_async_remote_copy(
                    src_ref=other, dst_ref=other, send_sem=send_d.at[3 * a + j], recv_sem=recv_d.at[3 * a + j],
                    device_id=(x, y, 1 - c), device_id_type=MESH).wait_recv()
        for cp in sends:
            cp.wait_send()

    return pl.pallas_call(
        body, name="allgather_weights", in_specs=[ANY] * n, out_specs=[ANY] * n,
        out_shape=[jax.ShapeDtypeStruct((N_CHIPS,) + s.shape, s.dtype) for s in shards],
        scratch_shapes=[pltpu.SemaphoreType.DMA((3 * n,))] * 4,
    )(*shards)


def _ride_scratch(ride):
    return [pltpu.SemaphoreType.DMA((ride[3],)), pltpu.SemaphoreType.DMA((ride[3],))]


def _ride_run(ride, src_refs, land_refs, send, recv, first, last):
    plan = ride[2]

    @pl.when(first)
    def _():
        for k, (src, dst, _, dev) in enumerate(plan(src_refs, land_refs)):
            pltpu.make_async_remote_copy(src_ref=src, dst_ref=dst, send_sem=send.at[k], recv_sem=recv.at[k],
                                         device_id=dev, device_id_type=MESH).start()

    @pl.when(last)
    def _():
        for k, (src, _, land, dev) in enumerate(plan(src_refs, land_refs)):
            cp = pltpu.make_async_remote_copy(src_ref=src, dst_ref=land, send_sem=send.at[k], recv_sem=recv.at[k],
                                              device_id=dev, device_id_type=MESH)
            cp.wait_send()
            cp.wait_recv()


def _join_rides(r1, r2):
    n1, l1 = len(r1[0]), len(r1[1])

    def plan(srcs, lands):
        return r1[2](srcs[:n1], lands[:l1]) + r2[2](srcs[n1:], lands[l1:])

    return (r1[0] + r2[0], r1[1] + r2[1], plan, r1[3] + r2[3])


def _gather_ride(shards):
    return (list(shards), [jax.ShapeDtypeStruct((N_CHIPS,) + s.shape, s.dtype) for s in shards],
            _gather_plan(len(shards)), 3 * len(shards))


def _scatter_ride(g_b):
    h = g_b.shape[1] // 2
    return ([g_b], [jax.ShapeDtypeStruct((8, h, g_b.shape[2]), BF16)], _scatter_plan(h), 7)


def _gather_plan(n):
    def plan(srcs, lands):
        x, y, c, chips = _place()
        me = 2 * x + y
        return [(srcs[a], lands[a].at[me], lands[a].at[2 * chip[0] + chip[1]], (*chip, c))
                for a in range(n) for chip in chips]
    return plan


def _scatter_plan(h):
    def plan(srcs, lands):
        x, y, c, _ = _place()
        me = 4 * x + 2 * y + c
        out = []
        for p in range(1, 8):
            px = 1 - x if p & 4 else x
            py = 1 - y if p & 2 else y
            pc = 1 - c if p & 1 else c
            out.append((srcs[0].at[2 * px + py, _half(pc, h)], lands[0].at[me],
                        lands[0].at[4 * px + 2 * py + pc], (px, py, pc)))
        return out
    return plan


def _grad_sum_call(own, land, place, name):
    _, h, cols = land.shape
    th = _pick(h, [256, 176, 8])
    nt = h // th

    def body(p_ref, own_ref, *refs):
        acc = own_ref[...]
        for r in refs[:7]:
            acc = acc + r[...].astype(F32)
        refs[7][...] = acc

    def peer(k):
        return pl.BlockSpec((None, th, cols), lambda i, p_ref: (p_ref[2 + k], i, 0))

    return pl.pallas_call(
        body, name=name,
        grid_spec=pltpu.PrefetchScalarGridSpec(
            num_scalar_prefetch=1, grid=(nt,),
            in_specs=[pl.BlockSpec((None, th, cols), lambda i, p_ref: (p_ref[1], p_ref[0] * nt + i, 0))]
            + [peer(k) for k in range(7)],
            out_specs=pl.BlockSpec((th, cols), lambda i, p_ref: (p_ref[0] * nt + i, 0))),
        out_shape=jax.ShapeDtypeStruct((2 * h, cols), F32),
        compiler_params=_cparams(("parallel",)))(place, own, *[land] * 7)


def _half_exchange_call(shards):
    n = len(shards)

    def body(*refs):
        outs = refs[n:2 * n]
        send_d, recv_d = refs[2 * n:]
        x, y, c, _ = _place()
        cps = []
        for a in range(n):
            h = shards[a].shape[0] // 2
            mine = outs[a].at[_half(c, h)]
            cp = pltpu.make_async_remote_copy(
                src_ref=mine, dst_ref=mine, send_sem=send_d.at[a], recv_sem=recv_d.at[a],
                device_id=(x, y, 1 - c), device_id_type=MESH)
            cp.start()
            cps.append(cp)
        for a, cp in enumerate(cps):
            h = shards[a].shape[0] // 2
            theirs = outs[a].at[_half(1 - c, h)]
            pltpu.make_async_remote_copy(
                src_ref=theirs, dst_ref=theirs, send_sem=send_d.at[a], recv_sem=recv_d.at[a],
                device_id=(x, y, 1 - c), device_id_type=MESH).wait_recv()
            cp.wait_send()

    return pl.pallas_call(
        body, name="grad_half_exchange", in_specs=[ANY] * n, out_specs=[ANY] * n,
        out_shape=[jax.ShapeDtypeStruct(sv.shape, F32) for sv in shards],
        input_output_aliases={a: a for a in range(n)},
        scratch_shapes=[pltpu.SemaphoreType.DMA((n,))] * 2,
    )(*shards)


def _allreduce_small_call(arrs):
    n = len(arrs)
    offs, rows = [], 0
    for a in arrs:
        offs.append(rows)
        rows += a.shape[0]
    rows = -(-rows // 8) * 8
    width = -(-max(a.shape[1] for a in arrs) // BLK) * BLK

    def body(*refs):
        ins, outs = refs[:n], refs[n:2 * n]
        gath, send_sems, recv_sems = refs[2 * n:]
        x, y, c, chips = _place()
        me, sibling = (x, y, c), (x, y, 1 - c)

        def slot(px, py, pc):
            return gath.at[4 * px + 2 * py + pc]

        def copy(k, block, to):
            return pltpu.make_async_remote_copy(
                src_ref=slot(*block), dst_ref=slot(*block),
                send_sem=send_sems.at[k], recv_sem=recv_sems.at[k], device_id=to, device_id_type=MESH)

        mine = slot(*me)
        mine[...] = jnp.zeros((rows, width), F32)
        for k in range(n):
            r, w = arrs[k].shape
            mine[offs[k]:offs[k] + r, 0:w] = ins[k][...]
        first = [copy(0, me, sibling)]
        first += [copy(1 + j, me, (*chip, c)) for j, chip in enumerate(chips)]
        for cp in first:
            cp.start()
        passed = [copy(4 + j, (*chip, c), sibling) for j, chip in enumerate(chips)]
        for j, chip in enumerate(chips):
            copy(1 + j, (*chip, c), me).wait_recv()
            passed[j].start()
        copy(0, sibling, me).wait_recv()
        for j, chip in enumerate(chips):
            copy(4 + j, (*chip, 1 - c), me).wait_recv()
        for cp in first + passed:
            cp.wait_send()
        acc = gath[0]
        for d in range(1, 8):
            acc = acc + gath[d]
        for k in range(n):
            r, w = arrs[k].shape
            outs[k][...] = acc[offs[k]:offs[k] + r, 0:w]

    vm = pl.BlockSpec(memory_space=pltpu.VMEM)
    return pl.pallas_call(
        body, name="allreduce_small", in_specs=[vm] * n, out_specs=[vm] * n,
        out_shape=[jax.ShapeDtypeStruct(a.shape, F32) for a in arrs],
        scratch_shapes=[pltpu.VMEM((8, rows, width), F32), pltpu.SemaphoreType.DMA((7,)),
                        pltpu.SemaphoreType.DMA((7,))],
        compiler_params=pltpu.CompilerParams(vmem_limit_bytes=VMEM_LIMIT),
    )(*arrs)


def _adamw_small_call(ws, gs, ms, vs):
    n = len(ws)
    c1 = 1.0 - ADAM_B1 ** ADAM_STEP
    c2 = 1.0 - ADAM_B2 ** ADAM_STEP

    def body(*refs):
        for k in range(n):
            w_ref, g_ref, m_ref, v_ref = (refs[j * n + k] for j in range(4))
            d_ref, mo_ref, vo_ref = (refs[(4 + j) * n + k] for j in range(3))
            gv = g_ref[...]
            m2 = ADAM_B1 * m_ref[...] + (1.0 - ADAM_B1) * gv
            v2 = ADAM_B2 * v_ref[...] + (1.0 - ADAM_B2) * (gv * gv)
            d_ref[...] = -ADAM_LR * ((m2 / c1) / (jnp.sqrt(v2 / c2) + ADAM_EPS) + ADAM_WD * w_ref[...])
            mo_ref[...] = m2
            vo_ref[...] = v2

    vm = pl.BlockSpec(memory_space=pltpu.VMEM)
    res = pl.pallas_call(
        body, name="adamw_small", in_specs=[vm] * (4 * n), out_specs=[vm] * (3 * n),
        out_shape=[jax.ShapeDtypeStruct(a.shape, F32) for a in ws] * 3,
        compiler_params=pltpu.CompilerParams(vmem_limit_bytes=VMEM_LIMIT),
    )(*ws, *gs, *ms, *vs)
    return res[:n], res[n:2 * n], res[2 * n:]


def _pack(arrs, min_rows=8):
    parts = []
    for a in arrs:
        flat = a.reshape(-1).astype(F32)
        parts.append(jnp.pad(flat, (0, (-flat.shape[0]) % BLK)))
    buf = jnp.concatenate(parts).reshape(-1, BLK)
    return jnp.pad(buf, ((0, (-buf.shape[0]) % min_rows), (0, 0)))


def _unpack(buf, shapes):
    out, r = [], 0
    for shp in shapes:
        n = math.prod(shp)
        nr = -(-n // BLK)
        out.append(buf[r:r + nr].reshape(-1)[:n].reshape(shp))
        r += nr
    return out


SMALL = ["meta_tokens", "mix_pre_g", "ssd_conv_w", "ssd_conv_b", "ssd_dt_bias", "ssd_a_log", "ssd_d", "ssd_norm_g",
         "sb_norm_g", "mix_post_g", "ffn_pre_g", "ffn_conv_w", "ffn_conv_b", "ffn_post_g"]
BIG = ["w_in", "w_out", "w_up", "w_down"]
WEIGHTS = ["meta_tokens", "mix_pre_g", "w_in", "ssd_conv_w", "ssd_conv_b", "ssd_dt_bias", "ssd_a_log", "ssd_d",
           "ssd_norm_g", "sb_norm_g", "w_out", "mix_post_g", "ffn_pre_g", "w_up", "ffn_conv_w", "ffn_conv_b",
           "w_down", "ffn_post_g"]
W_IN_SHARD = IN_COLS // N_CHIPS
W_IN_PAD = 1536


def kernel(x, meta_tokens, mix_pre_g, w_in, ssd_conv_w, ssd_conv_b, ssd_dt_bias, ssd_a_log, ssd_d, ssd_norm_g, sb_norm_g, w_out, mix_post_g, ffn_pre_g, w_up, ffn_conv_w, ffn_conv_b, w_down, ffn_post_g, loss_target, m_meta_tokens, m_mix_pre_g, m_w_in, m_ssd_conv_w, m_ssd_conv_b, m_ssd_dt_bias, m_ssd_a_log, m_ssd_d, m_ssd_norm_g, m_sb_norm_g, m_w_out, m_mix_post_g, m_ffn_pre_g, m_w_up, m_ffn_conv_w, m_ffn_conv_b, m_w_down, m_ffn_post_g, v_meta_tokens, v_mix_pre_g, v_w_in, v_ssd_conv_w, v_ssd_conv_b, v_ssd_dt_bias, v_ssd_a_log, v_ssd_d, v_ssd_norm_g, v_sb_norm_g, v_w_out, v_mix_post_g, v_ffn_pre_g, v_w_up, v_ffn_conv_w, v_ffn_conv_b, v_w_down, v_ffn_post_g):
    w = dict(meta_tokens=meta_tokens, mix_pre_g=mix_pre_g, w_in=w_in, ssd_conv_w=ssd_conv_w, ssd_conv_b=ssd_conv_b, ssd_dt_bias=ssd_dt_bias, ssd_a_log=ssd_a_log, ssd_d=ssd_d, ssd_norm_g=ssd_norm_g, sb_norm_g=sb_norm_g, w_out=w_out, mix_post_g=mix_post_g, ffn_pre_g=ffn_pre_g, w_up=w_up, ffn_conv_w=ffn_conv_w, ffn_conv_b=ffn_conv_b, w_down=w_down, ffn_post_g=ffn_post_g)
    m = dict(meta_tokens=m_meta_tokens, mix_pre_g=m_mix_pre_g, w_in=m_w_in, ssd_conv_w=m_ssd_conv_w, ssd_conv_b=m_ssd_conv_b, ssd_dt_bias=m_ssd_dt_bias, ssd_a_log=m_ssd_a_log, ssd_d=m_ssd_d, ssd_norm_g=m_ssd_norm_g, sb_norm_g=m_sb_norm_g, w_out=m_w_out, mix_post_g=m_mix_post_g, ffn_pre_g=m_ffn_pre_g, w_up=m_w_up, ffn_conv_w=m_ffn_conv_w, ffn_conv_b=m_ffn_conv_b, w_down=m_w_down, ffn_post_g=m_ffn_post_g)
    v = dict(meta_tokens=v_meta_tokens, mix_pre_g=v_mix_pre_g, w_in=v_w_in, ssd_conv_w=v_ssd_conv_w, ssd_conv_b=v_ssd_conv_b, ssd_dt_bias=v_ssd_dt_bias, ssd_a_log=v_ssd_a_log, ssd_d=v_ssd_d, ssd_norm_g=v_ssd_norm_g, sb_norm_g=v_sb_norm_g, w_out=v_w_out, mix_post_g=v_mix_post_g, ffn_pre_g=v_ffn_pre_g, w_up=v_w_up, ffn_conv_w=v_ffn_conv_w, ffn_conv_b=v_ffn_conv_b, w_down=v_w_down, ffn_post_g=v_ffn_post_g)
    chip = 2 * lax.axis_index("x") + lax.axis_index("y")
    me = 2 * chip + lax.axis_index("c")
    place = jnp.stack([lax.axis_index("c"), chip] + [me ^ p for p in range(1, 8)]).astype(jnp.int32)

    shard_small = [w["meta_tokens"], w["ssd_conv_w"][0], w["ffn_conv_w"][0]]
    shards = [jnp.pad(w["w_in"][0], ((0, 0), (0, W_IN_PAD - W_IN_SHARD))).astype(BF16), _pack(shard_small, 16)]
    gathered = _allgather_call(shards)
    late_shards = [w["w_out"][0].astype(BF16), w["w_up"][0].astype(BF16), w["w_down"][0].astype(BF16)]

    def blocks(own, got):
        return [jnp.where(chip == i, own, got[i]) for i in range(N_CHIPS)]

    def late_weights(got):
        return (jnp.concatenate(blocks(late_shards[0], got[0]), axis=0),
                jnp.concatenate(blocks(late_shards[1], got[1]), axis=1),
                jnp.concatenate(blocks(late_shards[2], got[2]), axis=0))

    cut = DT_REAL_OFF + N_HEADS - W_IN_SHARD
    s_in = blocks(shards[0], gathered[0])
    w_in_c = jnp.concatenate(
        [s_in[0][:, :W_IN_SHARD], s_in[1][:, :cut], jnp.zeros((D_MODEL, BLK - N_HEADS), BF16),
         s_in[1][:, cut:W_IN_SHARD], s_in[2][:, :W_IN_SHARD], s_in[3][:, :W_IN_SHARD]], axis=1)
    parts = [_unpack(b, [s.shape for s in shard_small]) for b in blocks(shards[1], gathered[1])]
    wt = {k: w[k][0][None] if w[k].ndim == 3 else w[k] for k in
          ["mix_pre_g", "ssd_conv_b", "ssd_dt_bias", "ssd_a_log", "ssd_d", "ssd_norm_g", "sb_norm_g", "mix_post_g",
           "ffn_pre_g", "ffn_conv_b", "ffn_post_g"]}
    wt.update(
        meta=jnp.concatenate([p[0] for p in parts], axis=1),
        ssd_conv_w=jnp.concatenate([p[1] for p in parts], axis=1),
        ffn_conv_w=jnp.concatenate([p[2] for p in parts], axis=1), w_in=w_in_c)

    def w_in_shards(g):
        skip = BLK - N_HEADS
        cols = [g[:, :W_IN_SHARD],
                jnp.concatenate([g[:, W_IN_SHARD:W_IN_SHARD + cut], g[:, C_Q:2 * W_IN_SHARD + skip]], axis=1),
                g[:, 2 * W_IN_SHARD + skip:3 * W_IN_SHARD + skip], g[:, 3 * W_IN_SHARD + skip:]]
        g = jnp.stack([jnp.pad(b, ((0, 0), (0, W_IN_PAD - W_IN_SHARD))) for b in cols])
        return g, g.astype(BF16)

    loss_row, dx, small, pending = _local_step(x[0], loss_target[0], wt, late_shards, late_weights, w_in_shards)

    full = _half_exchange_call([_grad_sum_call(*pending[k], place, "grad_sum_" + k) for k in BIG])
    grads = {"w_in": full[0][:, :W_IN_SHARD], "w_out": full[1], "w_up": full[2], "w_down": full[3]}

    red_list = _allreduce_small_call([small[k] for k in SMALL] + [loss_row])
    loss = jnp.sum(red_list[-1])
    for k, g in zip(SMALL, red_list[:-1]):
        grads[k] = g
    for k in ["meta_tokens", "ssd_conv_w", "ffn_conv_w"]:
        wk = w[k].shape[-1]
        grads[k] = lax.dynamic_slice_in_dim(grads[k], chip * wk, wk, axis=1)

    delta, new_m, new_v = {}, {}, {}
    for k in BIG:
        delta[k], new_m[k], new_v[k] = _adamw_call(w[k][0], grads[k], m[k][0], v[k][0], "adamw_" + k)
    flat = lambda d: [d[k].reshape(grads[k].shape) for k in SMALL]
    res = _adamw_small_call(flat(w), [grads[k] for k in SMALL], flat(m), flat(v))
    for out, arrs in zip((delta, new_m, new_v), res):
        for k, a in zip(SMALL, arrs):
            out[k] = a

    def shaped(d, k):
        return d[k].reshape(w[k].shape)

    return (loss, dx[None], *[shaped(grads, k) for k in WEIGHTS], *[shaped(delta, k) for k in WEIGHTS],
            *[shaped(new_m, k) for k in WEIGHTS], *[shaped(new_v, k) for k in WEIGHTS])
```

```python
import functools
import math

import jax
import jax.numpy as jnp
from jax import lax
from jax.experimental import pallas as pl
from jax.experimental.pallas import tpu as pltpu

F32 = jnp.float32
BF16 = jnp.bfloat16

D_MODEL = 1024
N_META = 16
BLK = 128
PAD = BLK - N_META
HEAD_DIM = 64
N_HEADS = 16
SSD_GROUPS = 2
SSD_STATE = 128
SSD_INNER = 1024
XBC = SSD_INNER + 2 * SSD_GROUPS * SSD_STATE
D_FF = 2816
EPS = 1e-6
IN_COLS = 5648
C_Z, C_XBC, C_DT, C_Q, C_K, C_V, C_END = 0, 1024, 2560, 2688, 3712, 4736, 5760
DT_REAL_OFF = 2560
N_CHIPS = 4
ADAM_LR, ADAM_B1, ADAM_B2, ADAM_EPS, ADAM_WD, ADAM_STEP = 0.001, 0.9, 0.999, 1e-08, 0.01, 10
VMEM_LIMIT = 56 * 1024 * 1024
MESH = pl.DeviceIdType.MESH


def _cparams(sem=None, **kw):
    if sem is not None:
        kw["dimension_semantics"] = sem
    return pltpu.CompilerParams(vmem_limit_bytes=VMEM_LIMIT, **kw)


def _pick(n, cands):
    for c in cands:
        if n % c == 0:
            return c
    raise ValueError((n, cands))


def _iota(shape, dim):
    return lax.broadcasted_iota(jnp.int32, shape, dim)


def _sigmoid(x):
    return 1.0 / (1.0 + jnp.exp(-x))


def _split3(v):
    h1 = v.astype(BF16)
    r1 = v - h1.astype(F32)
    h2 = r1.astype(BF16)
    h3 = (r1 - h2.astype(F32)).astype(BF16)
    return h1, h2, h3


def _dot(a, b, ca=1, cb=0):
    return lax.dot_general(a, b, (((ca,), (cb,)), ((), ())), preferred_element_type=F32)


def _dot_sel_r(v, sel, cb=0):
    h1, h2, h3 = _split3(v)
    return _dot(h1, sel, 1, cb) + _dot(h2, sel, 1, cb) + _dot(h3, sel, 1, cb)


def _dot_sel_l(sel, v, ca=1):
    h1, h2, h3 = _split3(v)
    return _dot(sel, h1, ca, 0) + _dot(sel, h2, ca, 0) + _dot(sel, h3, ca, 0)


def _mm(a, b, *, ta=False, tb=False, tm, tn, tk, out_dtype=F32, nsplit=1, extra_bf16=False, ride=None, name):
    K, M = (a.shape if ta else a.shape[::-1])
    N = b.shape[0] if tb else b.shape[1]
    assert M % tm == 0 and N % tn == 0 and K % tk == 0, (name, M, N, K, tm, tn, tk)
    nm, nn, nk = M // tm, N // tn, K // tk
    assert nn % nsplit == 0
    per = nn // nsplit
    a_spec = (pl.BlockSpec((tk, tm), lambda i, j, k: (k, i)) if ta
              else pl.BlockSpec((tm, tk), lambda i, j, k: (i, k)))
    b_spec = (pl.BlockSpec((tn, tk), lambda i, j, k: (j, k)) if tb
              else pl.BlockSpec((tk, tn), lambda i, j, k: (k, j)))
    o_spec = pl.BlockSpec((None, tm, tn), lambda i, j, k: (j // per, i, j % per))
    n_out = 2 if extra_bf16 else 1
    ca, cb = (0 if ta else 1), (1 if tb else 0)
    ns, nl = (len(ride[0]), len(ride[1])) if ride else (0, 0)

    def body(a_ref, b_ref, *rest):
        outs = rest[ns:ns + n_out]
        if ride:
            step = (pl.program_id(0) * nn + pl.program_id(1)) * nk + pl.program_id(2)
            _ride_run(ride, rest[:ns], rest[ns + n_out:ns + n_out + nl], rest[-2], rest[-1],
                      step == 0, step == nm * nn * nk - 1)
        p = _dot(a_ref[...].astype(BF16), b_ref[...].astype(BF16), ca, cb)

        def emit(val):
            outs[0][...] = val.astype(out_dtype)
            if extra_bf16:
                outs[1][...] = val.astype(BF16)

        if nk == 1:
            emit(p)
        else:
            acc = rest[ns + n_out + nl]
            k = pl.program_id(2)

            @pl.when(k == 0)
            def _():
                acc[...] = p

            @pl.when(k > 0)
            def _():
                acc[...] += p

            @pl.when(k == nk - 1)
            def _():
                emit(acc[...])

    shp = (nsplit, M, N // nsplit)
    out_shape = [jax.ShapeDtypeStruct(shp, out_dtype)]
    out_specs = [o_spec]
    if extra_bf16:
        out_shape.append(jax.ShapeDtypeStruct(shp, BF16))
        out_specs.append(o_spec)
    scratch = [pltpu.VMEM((tm, tn), F32)] if nk > 1 else []
    if ride:
        res = pl.pallas_call(
            body, name=name, grid=(nm, nn, nk), in_specs=[a_spec, b_spec] + [ANY] * ns,
            out_specs=out_specs + [ANY] * nl, out_shape=out_shape + list(ride[1]),
            scratch_shapes=scratch + _ride_scratch(ride),
            compiler_params=_cparams(("arbitrary", "arbitrary", "arbitrary")),
        )(a, b, *ride[0])
        return (res[:n_out] if extra_bf16 else res[0]), list(res[n_out:])
    res = pl.pallas_call(
        body, name=name, grid=(nm, nn, nk), in_specs=[a_spec, b_spec], out_specs=out_specs,
        out_shape=out_shape, scratch_shapes=scratch,
        compiler_params=_cparams(("parallel", "parallel", "arbitrary")),
    )(a, b)
    return res if extra_bf16 else res[0]


def _rms_stats(x):
    r = lax.rsqrt(jnp.mean(x * x, axis=-1, keepdims=True) + EPS)
    return r, x * r


def _rms_bwd(x, g, dy):
    r, xh = _rms_stats(x)
    dxh = dy * g
    dx = r * (dxh - xh * jnp.mean(dxh * xh, axis=-1, keepdims=True))
    return dx, jnp.sum(dy * xh, axis=0, keepdims=True)


def _row_spec(tr, w, col=0):
    return pl.BlockSpec((tr, w), lambda i: (i, col))


def _vec_spec(w):
    return pl.BlockSpec((1, w), lambda i: (0, 0))


def _acc_rows(ref, val, i):
    @pl.when(i == 0)
    def _():
        ref[...] = val

    @pl.when(i > 0)
    def _():
        ref[...] += val


def _rms_fwd_call(x, g, name):
    lp, w = x.shape
    tr = _pick(lp, [384, 128])

    def body(x_ref, g_ref, o_ref):
        _, xh = _rms_stats(x_ref[...])
        o_ref[...] = (xh * g_ref[...]).astype(BF16)

    return pl.pallas_call(
        body, name=name, grid=(lp // tr,), in_specs=[_row_spec(tr, w), _vec_spec(w)],
        out_specs=_row_spec(tr, w), out_shape=jax.ShapeDtypeStruct((lp, w), BF16),
        compiler_params=_cparams(("parallel",)))(x, g)


def _mid_fwd_call(h0, mix, g_post, g_pre2):
    lp, w = h0.shape
    tr = _pick(lp, [384, 128])

    def body(h0_ref, mix_ref, gp_ref, g2_ref, h1_ref, xn_ref):
        _, mh = _rms_stats(mix_ref[...])
        h1 = h0_ref[...] + mh * gp_ref[...]
        h1_ref[...] = h1
        _, hh = _rms_stats(h1)
        xn_ref[...] = (hh * g2_ref[...]).astype(BF16)

    return pl.pallas_call(
        body, name="mid_fwd", grid=(lp // tr,),
        in_specs=[_row_spec(tr, w), _row_spec(tr, w), _vec_spec(w), _vec_spec(w)],
        out_specs=[_row_spec(tr, w), _row_spec(tr, w)],
        out_shape=[jax.ShapeDtypeStruct((lp, w), F32), jax.ShapeDtypeStruct((lp, w), BF16)],
        compiler_params=_cparams(("parallel",)))(h0, mix, g_post, g_pre2)


def _final_call(h1, f, g_post, target):
    lp, w = h1.shape
    tr = BLK
    nb = lp // tr

    def body(h1_ref, f_ref, g_ref, t_ref, loss_ref, df_ref, dh_ref, dg_ref):
        i = pl.program_id(0)
        fv = f_ref[...]
        g = g_ref[...]
        _, fh = _rms_stats(fv)
        h2 = h1_ref[...] + fh * g
        diff = jnp.where(i > 0, h2 - t_ref[...], 0.0)
        part = 0.5 * jnp.sum(diff * diff, axis=0, keepdims=True) * (1.0 / w)
        _acc_rows(loss_ref, part, i)
        dh = diff * (1.0 / w)
        dh_ref[...] = dh
        df, dg = _rms_bwd(fv, g, dh)
        df_ref[...] = df.astype(BF16)
        _acc_rows(dg_ref, dg, i)

    t_spec = pl.BlockSpec((tr, w), lambda i: (jnp.maximum(i - 1, 0), 0))
    return pl.pallas_call(
        body, name="final_fwd_bwd", grid=(nb,),
        in_specs=[_row_spec(tr, w), _row_spec(tr, w), _vec_spec(w), t_spec],
        out_specs=[_vec_spec(w), _row_spec(tr, w), _row_spec(tr, w), _vec_spec(w)],
        out_shape=[jax.ShapeDtypeStruct((1, w), F32), jax.ShapeDtypeStruct((lp, w), BF16),
                   jax.ShapeDtypeStruct((lp, w), F32), jax.ShapeDtypeStruct((1, w), F32)],
        compiler_params=_cparams(("arbitrary",)))(h1, f, g_post, target)


def _mid_bwd_call(dh2, h1, dxn2, mix, g_pre2, g_post):
    lp, w = h1.shape
    tr = _pick(lp, [384, 128])

    def body(dh2_ref, h1_ref, dxn_ref, mix_ref, g2_ref, gp_ref, dh1_ref, dmix_ref, dg2_ref, dgp_ref):
        i = pl.program_id(0)
        live = (i * tr + _iota((tr, 1), 0)) >= PAD
        dx, dg2 = _rms_bwd(h1_ref[...], g2_ref[...], dxn_ref[...])
        dh1 = jnp.where(live, dh2_ref[...] + dx, 0.0)
        dh1_ref[...] = dh1
        dmix, dgp = _rms_bwd(mix_ref[...], gp_ref[...], dh1)
        dmix_ref[...] = jnp.where(live, dmix, 0.0).astype(BF16)
        _acc_rows(dg2_ref, dg2, i)
        _acc_rows(dgp_ref, dgp, i)

    rs = _row_spec(tr, w)
    return pl.pallas_call(
        body, name="mid_bwd", grid=(lp // tr,),
        in_specs=[rs, rs, rs, rs, _vec_spec(w), _vec_spec(w)],
        out_specs=[rs, rs, _vec_spec(w), _vec_spec(w)],
        out_shape=[jax.ShapeDtypeStruct((lp, w), F32), jax.ShapeDtypeStruct((lp, w), BF16),
                   jax.ShapeDtypeStruct((1, w), F32), jax.ShapeDtypeStruct((1, w), F32)],
        compiler_params=_cparams(("arbitrary",)))(dh2, h1, dxn2, mix, g_pre2, g_post)


def _norm_bwd_call(x, g, dy_arr, dy_col, name, res=None):
    lp, w = x.shape
    tr = _pick(lp, [384, 128])
    has_res = res is not None

    def body(x_ref, g_ref, dy_ref, *rest):
        i = pl.program_id(0)
        live = (i * tr + _iota((tr, 1), 0)) >= PAD
        dx, dg = _rms_bwd(x_ref[...], g_ref[...], dy_ref[...])
        if has_res:
            dx = dx + rest[0][...]
        out_ref, dg_ref = rest[-2], rest[-1]
        out_ref[...] = jnp.where(live, dx, 0.0)
        _acc_rows(dg_ref, dg, i)

    rs = _row_spec(tr, w)
    ins = [rs, _vec_spec(w), _row_spec(tr, w, dy_col)] + ([rs] if has_res else [])
    args = [x, g, dy_arr] + ([res] if has_res else [])
    return pl.pallas_call(
        body, name=name, grid=(lp // tr,), in_specs=ins, out_specs=[rs, _vec_spec(w)],
        out_shape=[jax.ShapeDtypeStruct((lp, w), F32), jax.ShapeDtypeStruct((1, w), F32)],
        compiler_params=_cparams(("arbitrary",)))(*args)


def _shift_down(cur, prev_tail, s, rows):
    if s == 0:
        return cur
    prev = jnp.tile(prev_tail, (BLK // 8, 1))
    return jnp.where(rows >= s, pltpu.roll(cur, s, 0), pltpu.roll(prev, s, 0))


def _shift_up(cur, next_head, s, rows):
    if s == 0:
        return cur
    nxt = jnp.tile(next_head, (BLK // 8, 1))
    return jnp.where(rows < BLK - s, pltpu.roll(cur, BLK - s, 0), pltpu.roll(nxt, BLK - s, 0))


def _gelu_tanh(x):
    c = math.sqrt(2.0 / math.pi)
    t = jnp.tanh(c * (x + 0.044715 * x * x * x))
    return 0.5 * x * (1.0 + t), t


def _conv_fwd_call(src, col0, width, cw, w8, b, taps, *, gate_src=None, gate_col0=0, name):
    lp = src.shape[0]
    nb, nc = lp // BLK, width // cw
    cb0 = col0 // cw
    ffn = gate_src is not None

    def body(x_ref, w_ref, b_ref, *rest):
        if ffn:
            u_ref, y_ref, a_ref, tail = rest
        else:
            y_ref, a_ref, tail = rest
        i = pl.program_id(1)

        @pl.when(i == 0)
        def _():
            tail[...] = jnp.zeros_like(tail)

        cur = x_ref[...]
        rows = _iota((BLK, cw), 0)
        y = b_ref[...] + w_ref[taps - 1:taps, :] * cur
        pt = tail[...]
        for s in range(1, taps):
            y = y + w_ref[taps - 1 - s:taps - s, :] * _shift_down(cur, pt, s, rows)
        tail[...] = cur[BLK - 8:, :]
        y_ref[...] = y
        if ffn:
            ge, _ = _gelu_tanh(y)
            a_ref[...] = (ge * u_ref[...]).astype(BF16)
        else:
            live = (i * BLK + rows) >= PAD
            a_ref[...] = jnp.where(live, y * _sigmoid(y), 0.0)

    blk = lambda c0: pl.BlockSpec((BLK, cw), lambda j, i: (i, c0 + j))
    ins = [blk(cb0), pl.BlockSpec((8, cw), lambda j, i: (0, j)), pl.BlockSpec((1, cw), lambda j, i: (0, j))]
    args = [src, w8, b]
    if ffn:
        ins.append(blk(gate_col0 // cw))
        args.append(gate_src)
    return pl.pallas_call(
        body, name=name, grid=(nc, nb), in_specs=ins, out_specs=[blk(0), blk(0)],
        out_shape=[jax.ShapeDtypeStruct((lp, width), F32),
                   jax.ShapeDtypeStruct((lp, width), BF16 if ffn else F32)],
        scratch_shapes=[pltpu.VMEM((8, cw), F32)],
        compiler_params=_cparams(("parallel", "arbitrary")))(*args)


def _conv_bwd_call(src, col0, width, cw, w8, taps, ypre, dact, *, gate_src=None, gate_col0=0, name):
    lp = src.shape[0]
    nb, nc = lp // BLK, width // cw
    cb0 = col0 // cw
    ffn = gate_src is not None

    def body(x_ref, w_ref, y_ref, d_ref, *rest):
        if ffn:
            u_ref, dx_ref, du_ref, dw_ref, db_ref, head = rest
        else:
            dx_ref, dw_ref, db_ref, head = rest
        step = pl.program_id(1)
        i = nb - 1 - step

        @pl.when(step == 0)
        def _():
            head[...] = jnp.zeros_like(head)

        rows = _iota((BLK, cw), 0)
        live = (i * BLK + rows) >= PAD
        y = y_ref[...]
        d = d_ref[...]
        if ffn:
            ge, t = _gelu_tanh(y)
            c = math.sqrt(2.0 / math.pi)
            dge = 0.5 * (1.0 + t) + 0.5 * y * (1.0 - t * t) * c * (1.0 + 3.0 * 0.044715 * y * y)
            u = u_ref[...]
            du_ref[...] = jnp.where(live, d * ge, 0.0).astype(BF16)
            dy = jnp.where(live, d * u * dge, 0.0)
        else:
            sg = _sigmoid(y)
            dy = jnp.where(live, d * sg * (1.0 + y * (1.0 - sg)), 0.0)
        x = x_ref[...]
        nh = head[...]
        dx = jnp.zeros_like(dy)
        dws = []
        for s in range(taps):
            sh = _shift_up(dy, nh, s, rows)
            dx = dx + w_ref[taps - 1 - s:taps - s, :] * sh
            dws.append(jnp.sum(x * sh, axis=0, keepdims=True))
        head[...] = dy[:8, :]
        dx_ref[...] = jnp.where(live, dx, 0.0).astype(BF16)
        dw = jnp.concatenate([dws[taps - 1 - k] for k in range(taps)]
                             + [jnp.zeros((8 - taps, cw), F32)], axis=0)
        _acc_rows(dw_ref, dw, step)
        _acc_rows(db_ref, jnp.sum(dy, axis=0, keepdims=True), step)

    blk = lambda c0: pl.BlockSpec((BLK, cw), lambda j, s: (nb - 1 - s, c0 + j))
    ins = [blk(cb0), pl.BlockSpec((8, cw), lambda j, s: (0, j)), blk(0), blk(0)]
    args = [src, w8, ypre, dact]
    outs = [blk(0)]
    oshape = [jax.ShapeDtypeStruct((lp, width), BF16)]
    if ffn:
        ins.append(blk(gate_col0 // cw))
        args.append(gate_src)
        outs.append(blk(0))
        oshape.append(jax.ShapeDtypeStruct((lp, width), BF16))
    outs += [pl.BlockSpec((8, cw), lambda j, s: (0, j)), pl.BlockSpec((1, cw), lambda j, s: (0, j))]
    oshape += [jax.ShapeDtypeStruct((8, width), F32), jax.ShapeDtypeStruct((1, width), F32)]
    return pl.pallas_call(
        body, name=name, grid=(nc, nb), in_specs=ins, out_specs=outs, out_shape=oshape,
        scratch_shapes=[pltpu.VMEM((8, cw), F32)],
        compiler_params=_cparams(("parallel", "arbitrary")))(*args)


SB_GROUP = 4
SB_DEAD = -110.0


def _sb_scores(qm_h, kb):
    z = _dot(qm_h, kb, 1, 1)
    sp = jnp.maximum(z, 0.0) + jnp.log(1.0 + jnp.exp(-jnp.abs(z)))
    return z - sp, -sp


def _sb_valid(i, off, width):
    kpos = off + _iota((BLK, width), 1)
    qpos = i * BLK + _iota((BLK, width), 0)
    return (kpos < qpos) & (kpos >= PAD)


def _dot_tri1(v, tri2):
    r = _dot(v.astype(BF16), tri2[:BLK])
    return r[:, :BLK], r[:, BLK:]


def _sb_groups(i):
    edge = i // SB_GROUP
    return edge, pl.multiple_of(edge * (SB_GROUP * BLK), BLK)


def _tri2(cond):
    t = jnp.concatenate([cond.astype(BF16), jnp.ones((BLK, BLK), BF16)], axis=1)
    return jnp.concatenate([t, t], axis=0)


def _dot_tri(v, tri2):
    hi = v.astype(BF16)
    lo = (v - hi.astype(F32)).astype(BF16)
    r = _dot(jnp.concatenate([hi, lo], axis=1), tri2)
    return r[:, :BLK], r[:, BLK:]


def _sb_edge(i, edge, edge_off, chunk):
    live = i - edge * SB_GROUP + 1
    merged = jnp.logical_and(live <= SB_GROUP // 2, edge >= 1).astype(jnp.int32)
    before = pl.multiple_of(jnp.maximum(edge_off - SB_GROUP * BLK, 0), BLK)
    upto = [functools.partial(chunk, edge_off, n) for n in range(1, SB_GROUP + 1)]
    upto += [functools.partial(chunk, before, SB_GROUP + n) for n in range(1, SB_GROUP // 2 + 1)]
    return merged, upto, live - 1 + merged * SB_GROUP


def _sb_fwd_call(proj, ride):
    lp = proj.shape[0]
    nb = lp // BLK
    assert (nb - 1) % SB_GROUP == 0
    scale = 1.0 / math.sqrt(HEAD_DIM)

    ns, nl = len(ride[0]), len(ride[1])
    npair = N_HEADS // 2

    def body(q_ref, k_ref, v_ref, *rest):
        o_ref, tl_ref = rest[ns], rest[ns + 1]
        i = pl.program_id(1)
        step = pl.program_id(0) * nb + i
        _ride_run(ride, rest[:ns], rest[ns + 2:ns + 2 + nl], rest[-2], rest[-1], step == 0, step == npair * nb - 1)
        lane = _iota((2 * BLK, BLK), 1)
        row = _iota((2 * BLK, BLK), 0)
        first = row < BLK
        qrow = row & (BLK - 1)
        q = q_ref[...] * scale
        q2 = jnp.concatenate([q, q], axis=0)
        qm = jnp.where(first == (lane < HEAD_DIM), q2, 0.0).astype(BF16)
        tri = _tri2(_iota((BLK, BLK), 0) > _iota((BLK, BLK), 1))

        def chunk(off, nsub, last_valid, carry):
            width = nsub * BLK
            sls = [slice(b * BLK, (b + 1) * BLK) for b in range(nsub)]
            kb = k_ref[pl.ds(off, width), :].astype(BF16)
            vb = v_ref[pl.ds(off, width), :].astype(BF16)
            lb, lk = _sb_scores(qm, kb)
            lks = [lk[:, sl] for sl in sls]
            first_valid = (off + lane) >= PAD
            lks[0] = jnp.where(first_valid, lks[0], 0.0)
            if last_valid is not None:
                lks[-1] = jnp.where(last_valid, lks[-1], 0.0)
            afters = [_dot_tri(lks[b], tri) for b in range(nsub)]
            run, acc = carry
            ws = [None] * nsub
            for b in reversed(range(nsub)):
                wb = jnp.exp(lb[:, sls[b]] + afters[b][0] + run)
                if b == 0:
                    wb = jnp.where(first_valid, wb, 0.0)
                if last_valid is not None and b == nsub - 1:
                    wb = jnp.where(last_valid, wb, 0.0)
                ws[b] = wb.astype(BF16)
                run = run + afters[b][1]
            w = ws[0] if nsub == 1 else jnp.concatenate(ws, axis=1)
            return run, acc + _dot(w, vb)

        edge, edge_off = _sb_groups(i)
        diag = lane < qrow
        zero = jnp.zeros((2 * BLK, BLK), F32)
        merged, upto, which = _sb_edge(i, edge, edge_off, lambda off, n, cr: chunk(off, n, diag, cr))
        carry = lax.switch(which, upto, (zero, zero))

        def interior(state):
            off = pl.multiple_of(state[0] * (SB_GROUP * BLK), BLK)
            return (state[0] - 1, *chunk(off, SB_GROUP, None, state[1:]))

        def live(state):
            return jnp.logical_and(state[0] >= 0, jnp.max(state[1]) > SB_DEAD)

        below, run, acc = lax.while_loop(live, interior, (edge - 1 - merged, *carry))
        low = lane[:BLK] < HEAD_DIM
        o_ref[...] = jnp.where(low, acc[:BLK], acc[BLK:])
        tl = jnp.where(low, run[:BLK], run[BLK:])
        tl_ref[...] = jnp.where(lane[:BLK] == 1, (below + 1).astype(F32), tl)

    qc, kc, vc = C_Q // BLK, C_K // BLK, C_V // BLK
    blk = pl.BlockSpec((BLK, BLK), lambda p, i: (i, p))
    res = pl.pallas_call(
        body, name="sb_fwd", grid=(npair, nb),
        in_specs=[pl.BlockSpec((BLK, BLK), lambda p, i: (i, qc + p)),
                  pl.BlockSpec((lp, BLK), lambda p, i: (0, kc + p)),
                  pl.BlockSpec((lp, BLK), lambda p, i: (0, vc + p))] + [ANY] * ns,
        out_specs=[blk, blk] + [ANY] * nl,
        out_shape=[jax.ShapeDtypeStruct((lp, N_HEADS * HEAD_DIM), F32)] * 2 + list(ride[1]),
        scratch_shapes=_ride_scratch(ride),
        compiler_params=_cparams(("arbitrary", "arbitrary")))(proj, proj, proj, *ride[0])
    return res[0], res[1], list(res[2:])


def _sb_bwd_call(proj, tl, do, ride):
    lp = proj.shape[0]
    nb = lp // BLK
    assert (nb - 1) % SB_GROUP == 0
    scale = 1.0 / math.sqrt(HEAD_DIM)

    ns, nl = len(ride[0]), len(ride[1])
    npair = N_HEADS // 2

    def body(q_ref, k_ref, v_ref, tl_ref, do_ref, *rest):
        dq_ref, dk_ref, dv_ref = rest[ns:ns + 3]
        dk_acc, dv_acc = rest[ns + 3 + nl:ns + 5 + nl]
        i = pl.program_id(1)
        step = pl.program_id(0) * nb + i
        _ride_run(ride, rest[:ns], rest[ns + 3:ns + 3 + nl], rest[-2], rest[-1], step == 0, step == npair * nb - 1)

        @pl.when(i == 0)
        def _():
            dk_acc[...] = jnp.zeros_like(dk_acc)
            dv_acc[...] = jnp.zeros_like(dv_acc)

        lane = _iota((2 * BLK, BLK), 1)
        row = _iota((2 * BLK, BLK), 0)
        qrow = row & (BLK - 1)
        mine = (row < BLK) == (lane < HEAD_DIM)
        q = q_ref[...] * scale
        dov = do_ref[...]
        qm = jnp.where(mine, jnp.concatenate([q, q], axis=0), 0.0).astype(BF16)
        dom = jnp.where(mine, jnp.concatenate([dov, dov], axis=0), 0.0).astype(BF16)
        tlv = tl_ref[...]
        tot = jnp.concatenate([jnp.broadcast_to(tlv[:, 0:1], (BLK, BLK)),
                               jnp.broadcast_to(tlv[:, HEAD_DIM:HEAD_DIM + 1], (BLK, BLK))], axis=0)
        r1, l1 = _iota((BLK, BLK), 0), _iota((BLK, BLK), 1)
        tri_in = _tri2(r1 <= l1)
        tri_ex = _tri2(r1 < l1)

        def chunk(off, nsub, last_valid, carry):
            width = nsub * BLK
            sls = [slice(b * BLK, (b + 1) * BLK) for b in range(nsub)]
            cat = lambda parts: parts[0] if nsub == 1 else jnp.concatenate(parts, axis=1)
            mask_last = lambda b: last_valid is not None and b == nsub - 1
            kb = k_ref[pl.ds(off, width), :].astype(BF16)
            vb = v_ref[pl.ds(off, width), :].astype(BF16)
            lb, lk = _sb_scores(qm, kb)
            dw = _dot(dom, vb, 1, 1)
            lks = [lk[:, sl] for sl in sls]
            first_valid = (off + lane) >= PAD
            lks[0] = jnp.where(first_valid, lks[0], 0.0)
            if last_valid is not None:
                lks[-1] = jnp.where(last_valid, lks[-1], 0.0)
            pins = [_dot_tri(lks[b], tri_in) for b in range(nsub)]
            run, gsum, dq = carry
            ws, gs = [], []
            for b in range(nsub):
                wb = jnp.exp(lb[:, sls[b]] + (tot - run - pins[b][0]))
                if b == 0:
                    wb = jnp.where(first_valid, wb, 0.0)
                if mask_last(b):
                    wb = jnp.where(last_valid, wb, 0.0)
                ws.append(wb.astype(BF16))
                gs.append(wb * dw[:, sls[b]])
                run = run + pins[b][1]
            gexs = [_dot_tri1(gs[b], tri_ex) for b in range(nsub)]
            beta = jnp.exp(lb)
            parts = []
            for b in range(nsub):
                bt = beta[:, sls[b]]
                dzb = gs[b] * (1.0 - bt) - (gsum + gexs[b][0]) * bt
                if b == 0:
                    dzb = jnp.where(first_valid, dzb, 0.0)
                if mask_last(b):
                    dzb = jnp.where(last_valid, dzb, 0.0)
                parts.append(dzb.astype(BF16))
                gsum = gsum + gexs[b][1]
            dz, w = cat(parts), cat(ws)
            dk_acc[pl.ds(off, width), :] += _dot(dz, qm, 0, 0)
            dv_acc[pl.ds(off, width), :] += _dot(w, dom, 0, 0)
            return run, gsum, dq + _dot(dz, kb)

        edge, edge_off = _sb_groups(i)
        diag = lane < qrow
        zero = jnp.zeros((2 * BLK, BLK), F32)
        first = jnp.max(tlv[:, 1:2]).astype(jnp.int32)

        def interior(g, carry):
            return chunk(pl.multiple_of(g * (SB_GROUP * BLK), BLK), SB_GROUP, None, carry)

        merged, upto, which = _sb_edge(i, edge, edge_off, lambda off, n, cr: chunk(off, n, diag, cr))
        carry = lax.fori_loop(first, edge - merged, interior, (zero, zero, zero))
        dq = lax.switch(which, upto, carry)[2]
        dq_ref[...] = (jnp.where(lane[:BLK] < HEAD_DIM, dq[:BLK], dq[BLK:]) * scale).astype(BF16)

        @pl.when(i == nb - 1)
        def _():
            dk_ref[...] = dk_acc[...].astype(BF16)
            dv_ref[...] = dv_acc[...].astype(BF16)

    qc, kc, vc = C_Q // BLK, C_K // BLK, C_V // BLK
    blk = pl.BlockSpec((BLK, BLK), lambda p, i: (i, p))
    full = pl.BlockSpec((lp, BLK), lambda p, i: (0, p))
    w = N_HEADS * HEAD_DIM
    res = pl.pallas_call(
        body, name="sb_bwd", grid=(npair, nb),
        in_specs=[pl.BlockSpec((BLK, BLK), lambda p, i: (i, qc + p)),
                  pl.BlockSpec((lp, BLK), lambda p, i: (0, kc + p)),
                  pl.BlockSpec((lp, BLK), lambda p, i: (0, vc + p)),
                  blk, blk] + [ANY] * ns,
        out_specs=[blk, full, full] + [ANY] * nl,
        out_shape=[jax.ShapeDtypeStruct((lp, w), BF16)] * 3 + list(ride[1]),
        scratch_shapes=[pltpu.VMEM((lp, BLK), F32), pltpu.VMEM((lp, BLK), F32)] + _ride_scratch(ride),
        compiler_params=_cparams(("arbitrary", "arbitrary")))(proj, proj, proj, tl, do, *ride[0])
    return res[0], res[1], res[2], list(res[3:])


def _log1p(e):
    u = 1.0 + e
    return jnp.where(u == 1.0, e, jnp.log(u) * e / jnp.where(u == 1.0, 1.0, u - 1.0))


def _ssd_common(c, dtr, bias, alog):
    row = _iota((BLK, BLK), 0)
    lane = _iota((BLK, BLK), 1)
    live = ((c * BLK + row) >= PAD) & (lane < N_HEADS)
    pre = dtr + bias
    dt = jnp.where(live, jnp.maximum(pre, 0.0) + _log1p(jnp.exp(-jnp.abs(pre))), 0.0)
    a_neg = -jnp.exp(alog)
    a = dt * a_neg
    t_in = (lane <= row).astype(BF16)
    cs = _dot_sel_l(t_in, a)
    cs_t = cs.T
    cs_end = cs[BLK - 1:BLK, :]
    e = jnp.exp(cs)
    f = jnp.exp(cs_end - cs)
    xp = ((_iota((BLK, SSD_INNER), 1) // HEAD_DIM) == _iota((BLK, SSD_INNER), 0)).astype(BF16)
    xp_t = ((_iota((SSD_INNER, BLK), 0) // HEAD_DIM) == _iota((SSD_INNER, BLK), 1)).astype(BF16)
    decay_col = _dot_sel_l(xp_t, jnp.exp(cs_t))[:, BLK - 1:BLK]
    return dict(live=live, pre=pre, dt=dt, a_neg=a_neg, cs=cs, cs_t=cs_t, e=e, f=f, xp=xp, xp_t=xp_t,
                decay_col=decay_col, row=row, lane=lane,
                dt_x=_dot_sel_r(dt, xp), e_x=_dot_sel_r(e, xp), f_x=_dot_sel_r(f, xp))


def _ssd_ldec(q, h):
    diff = q["cs"][:, h:h + 1] - q["cs_t"][h:h + 1, :]
    causal = q["row"] >= q["lane"]
    return jnp.where(causal, jnp.exp(jnp.where(causal, diff, 0.0)), 0.0)


def _ssd_fwd_call(xbc, proj, bias, alog, d_x, norm_g):
    lp = xbc.shape[0]
    nb = lp // BLK
    gw = SSD_INNER // SSD_GROUPS
    ppg = gw // BLK

    def body(xbc_ref, dtr_ref, z_ref, bias_ref, alog_ref, dx_ref, ng_ref, yb_ref, ypre_ref, sprev_ref, s_ref):
        c = pl.program_id(0)

        @pl.when(c == 0)
        def _():
            s_ref[...] = jnp.zeros_like(s_ref)

        q = _ssd_common(c, dtr_ref[...], bias_ref[...], alog_ref[...])
        x = xbc_ref[:, 0:SSD_INNER]
        xd = x * q["dt_x"]
        low = q["lane"] < HEAD_DIM
        s_old = s_ref[...]
        sprev_ref[...] = s_old
        xdf = (xd * q["f_x"]).astype(BF16)
        for g in range(SSD_GROUPS):
            bg = xbc_ref[:, SSD_INNER + g * SSD_STATE:SSD_INNER + (g + 1) * SSD_STATE].astype(BF16)
            cg = xbc_ref[:, SSD_INNER + (SSD_GROUPS + g) * SSD_STATE:
                         SSD_INNER + (SSD_GROUPS + g + 1) * SSD_STATE].astype(BF16)
            cb = _dot(cg, bg, 1, 1)
            gs = slice(g * gw, (g + 1) * gw)
            y_off = _dot(cg, s_old[gs, :].astype(BF16), 1, 1) * q["e_x"][:, gs]
            s_ref[gs, :] = s_old[gs, :] * q["decay_col"][gs, :] + _dot(xdf[:, gs], bg, 0, 0)
            for pr in range(ppg):
                cols = slice(g * gw + pr * BLK, g * gw + (pr + 1) * BLK)
                xd_p = xd[:, cols]
                acc = y_off[:, pr * BLK:(pr + 1) * BLK]
                for hh in range(2):
                    h = (g * gw + pr * BLK) // HEAD_DIM + hh
                    m = (cb * _ssd_ldec(q, h)).astype(BF16)
                    xm = jnp.where(low, xd_p, 0.0) if hh == 0 else jnp.where(low, 0.0, xd_p)
                    acc = acc + _dot(m, xm.astype(BF16))
                ypre_ref[:, cols] = acc
        ypre = ypre_ref[...] + x * dx_ref[...]
        ypre_ref[...] = ypre
        z = z_ref[...]
        yg = ypre * (z * _sigmoid(z))
        _, yh = _rms_stats(yg)
        yb_ref[...] = (yh * ng_ref[...]).astype(BF16)

    row = lambda w, col: pl.BlockSpec((BLK, w), lambda c: (c, col))
    vec = lambda w: pl.BlockSpec((1, w), lambda c: (0, 0))
    return pl.pallas_call(
        body, name="ssd_fwd", grid=(nb,),
        in_specs=[row(XBC, 0), row(BLK, C_DT // BLK), row(SSD_INNER, 0), vec(BLK), vec(BLK),
                  vec(SSD_INNER), vec(SSD_INNER)],
        out_specs=[row(SSD_INNER, 0), row(SSD_INNER, 0),
                   pl.BlockSpec((None, SSD_INNER, SSD_STATE), lambda c: (c, 0, 0))],
        out_shape=[jax.ShapeDtypeStruct((lp, SSD_INNER), BF16), jax.ShapeDtypeStruct((lp, SSD_INNER), F32),
                   jax.ShapeDtypeStruct((nb, SSD_INNER, SSD_STATE), F32)],
        scratch_shapes=[pltpu.VMEM((SSD_INNER, SSD_STATE), F32)],
        compiler_params=_cparams(("arbitrary",)))(xbc, proj, proj, bias, alog, d_x, norm_g)


def _ssd_bwd_call(dycat, ypre, xbc, proj, sprev, bias, alog, d_x, norm_g):
    lp = xbc.shape[0]
    nb = lp // BLK
    gw = SSD_INNER // SSD_GROUPS
    ppg = gw // BLK

    def body(dy_ref, ypre_ref, xbc_ref, dtr_ref, z_ref, sp_ref, bias_ref, alog_ref, dxp_ref, ng_ref,
             dz_ref, dxbc_ref, ddt_ref, dng_ref, dd_ref, dal_ref, dbi_ref, ds_ref, dxd_ref):
        step = pl.program_id(0)
        c = nb - 1 - step

        @pl.when(step == 0)
        def _():
            ds_ref[...] = jnp.zeros_like(ds_ref)

        q = _ssd_common(c, dtr_ref[...], bias_ref[...], alog_ref[...])
        row, lane = q["row"], q["lane"]
        low = lane < HEAD_DIM
        rowlive = ((c * BLK + _iota((BLK, 1), 0)) >= PAD)
        x = xbc_ref[:, 0:SSD_INNER]
        xd = x * q["dt_x"]
        z = z_ref[...]
        sz = _sigmoid(z)
        silu = z * sz
        ypre = ypre_ref[...]
        dyg, dng = _rms_bwd(ypre * silu, ng_ref[...], dy_ref[...])
        _acc_rows(dng_ref, dng, step)
        dyp = dyg * silu
        dz_ref[...] = jnp.where(rowlive, dyg * ypre * (sz * (1.0 + z * (1.0 - sz))), 0.0).astype(BF16)
        _acc_rows(dd_ref, jnp.sum(dyp * x, axis=0, keepdims=True), step)
        dye = dyp * q["e_x"]
        xdf = xd * q["f_x"]
        s_prev = sp_ref[...]
        ds_old = ds_ref[...]
        qrow = jnp.zeros((BLK, BLK), F32)
        qcol_t = jnp.zeros((BLK, BLK), F32)
        red_e = []
        red_f = []
        for g in range(SSD_GROUPS):
            gs = slice(g * gw, (g + 1) * gw)
            bsl = slice(SSD_INNER + g * SSD_STATE, SSD_INNER + (g + 1) * SSD_STATE)
            csl = slice(SSD_INNER + (SSD_GROUPS + g) * SSD_STATE, SSD_INNER + (SSD_GROUPS + g + 1) * SSD_STATE)
            bg = xbc_ref[:, bsl].astype(BF16)
            cg = xbc_ref[:, csl].astype(BF16)
            sg = s_prev[gs, :].astype(BF16)
            dsg = ds_old[gs, :].astype(BF16)
            cb = _dot(cg, bg, 1, 1)
            bds = _dot(bg, dsg, 1, 1)
            y_off = _dot(cg, sg, 1, 1) * q["e_x"][:, gs]
            red_e.append(dyp[:, gs] * y_off)
            red_f.append(xd[:, gs] * bds * q["f_x"][:, gs])
            dc = _dot(dye[:, gs].astype(BF16), sg)
            db = _dot(xdf[:, gs].astype(BF16), dsg)
            ds_ref[gs, :] = ds_old[gs, :] * q["decay_col"][gs, :] + _dot(dye[:, gs].astype(BF16), cg, 0, 0)
            dcb = jnp.zeros((BLK, BLK), F32)
            for pr in range(ppg):
                cols = slice(g * gw + pr * BLK, g * gw + (pr + 1) * BLK)
                xd_p = xd[:, cols].astype(BF16)
                dy_p = dyp[:, cols]
                acc = q["f_x"][:, cols] * bds[:, pr * BLK:(pr + 1) * BLK]
                for hh in range(2):
                    h = (g * gw + pr * BLK) // HEAD_DIM + hh
                    ld = _ssd_ldec(q, h)
                    m = cb * ld
                    dym = (jnp.where(low, dy_p, 0.0) if hh == 0 else jnp.where(low, 0.0, dy_p)).astype(BF16)
                    dm = jnp.where(row >= lane, _dot(dym, xd_p, 1, 1), 0.0)
                    acc = acc + _dot(m.astype(BF16), dym, 0, 0)
                    dcb = dcb + dm * ld
                    qq = dm * m
                    qrow = qrow + jnp.where(lane == h, jnp.sum(qq, axis=1, keepdims=True), 0.0)
                    qcol_t = qcol_t + jnp.where(row == h, jnp.sum(qq, axis=0, keepdims=True), 0.0)
                dxd_ref[:, cols] = acc
            dcbb = dcb.astype(BF16)
            dxbc_ref[:, bsl] = jnp.where(rowlive, db + _dot(dcbb, cg, 0, 0), 0.0)
            dxbc_ref[:, csl] = jnp.where(rowlive, dc + _dot(dcbb, bg), 0.0)
        dxd = dxd_ref[...]
        dxbc_ref[:, 0:SSD_INNER] = jnp.where(rowlive, dxd * q["dt_x"] + dyp * dxp_ref[...], 0.0)
        xp_t = q["xp_t"]
        fw = _dot_sel_r(jnp.concatenate(red_f, axis=1), xp_t)
        dcs = qrow - qcol_t.T + _dot_sel_r(jnp.concatenate(red_e, axis=1), xp_t) - fw
        end_f = jnp.sum(fw, axis=0, keepdims=True)
        sds = jnp.sum(ds_old * s_prev, axis=1, keepdims=True)
        per_head = _dot_sel_l(q["xp"], jnp.broadcast_to(sds, (SSD_INNER, BLK)))
        end_e = per_head.T[0:1, :] * jnp.exp(q["cs"][BLK - 1:BLK, :])
        dcs = dcs + jnp.where(row == BLK - 1, end_f + end_e, 0.0)
        t_up = (lane >= row).astype(BF16)
        da = _dot_sel_l(t_up, dcs)
        ddt = da * q["a_neg"] + _dot_sel_r(dxd * x, xp_t)
        _acc_rows(dal_ref, jnp.sum(da * q["dt"] * q["a_neg"], axis=0, keepdims=True), step)
        ddtr = jnp.where(q["live"], ddt * _sigmoid(q["pre"]), 0.0)
        ddt_ref[...] = ddtr.astype(BF16)
        _acc_rows(dbi_ref, jnp.sum(ddtr, axis=0, keepdims=True), step)

    row_s = lambda w, col: pl.BlockSpec((BLK, w), lambda s: (nb - 1 - s, col))
    vec = lambda w: pl.BlockSpec((1, w), lambda s: (0, 0))
    return pl.pallas_call(
        body, name="ssd_bwd", grid=(nb,),
        in_specs=[row_s(SSD_INNER, 0), row_s(SSD_INNER, 0), row_s(XBC, 0), row_s(BLK, C_DT // BLK),
                  row_s(SSD_INNER, 0), pl.BlockSpec((None, SSD_INNER, SSD_STATE), lambda s: (nb - 1 - s, 0, 0)),
                  vec(BLK), vec(BLK), vec(SSD_INNER), vec(SSD_INNER)],
        out_specs=[row_s(SSD_INNER, 0), row_s(XBC, 0), row_s(BLK, 0),
                   vec(SSD_INNER), vec(SSD_INNER), vec(BLK), vec(BLK)],
        out_shape=[jax.ShapeDtypeStruct((lp, SSD_INNER), BF16), jax.ShapeDtypeStruct((lp, XBC), F32),
                   jax.ShapeDtypeStruct((lp, BLK), BF16),
                   jax.ShapeDtypeStruct((1, SSD_INNER), F32), jax.ShapeDtypeStruct((1, SSD_INNER), F32),
                   jax.ShapeDtypeStruct((1, BLK), F32), jax.ShapeDtypeStruct((1, BLK), F32)],
        scratch_shapes=[pltpu.VMEM((SSD_INNER, SSD_STATE), F32), pltpu.VMEM((BLK, SSD_INNER), F32)],
        compiler_params=_cparams(("arbitrary",)))(dycat, ypre, xbc, proj, proj, sprev, bias, alog, d_x, norm_g)


def _pad_rows8(w):
    return jnp.pad(w, ((0, 8 - w.shape[0]), (0, 0)))


def _pad_lanes(v, n=BLK):
    return jnp.pad(v, ((0, 0), (0, n - v.shape[1])))


def _local_step(x, target, wt, late_shards, late_weights, w_in_shards):
    seq = x.shape[0]
    lp = seq + BLK
    tm = _pick(lp, [1408, 768, 384, 128])
    tkr = _pick(lp, [1408, 384, 128])
    h0 = jnp.concatenate([jnp.zeros((PAD, D_MODEL), F32), wt["meta"], x], axis=0)
    bias = _pad_lanes(wt["ssd_dt_bias"])
    alog = _pad_lanes(wt["ssd_a_log"])
    d_x = jnp.repeat(wt["ssd_d"], HEAD_DIM, axis=1)
    cw8 = _pad_rows8(wt["ssd_conv_w"])
    fw8 = _pad_rows8(wt["ffn_conv_w"])
    fcw = D_FF // 2

    xn1 = _rms_fwd_call(h0, wt["mix_pre_g"], "norm1")
    proj, late_a = _mm(xn1, wt["w_in"], tm=tm, tn=1152, tk=D_MODEL, ride=_gather_ride(late_shards[0:1]),
                       name="mm_proj")
    proj = proj[0]
    conv_pre, xbc = _conv_fwd_call(proj, C_XBC, XBC, 512, cw8, wt["ssd_conv_b"], 4, name="ssd_conv_fwd")
    y_ssd, ypre, sprev = _ssd_fwd_call(xbc, proj, bias, alog, d_x, wt["ssd_norm_g"])
    o, tl, late_b = _sb_fwd_call(proj, _gather_ride(late_shards[1:3]))
    y_sb = _rms_fwd_call(o, wt["sb_norm_g"], "sb_norm")
    ycat = jnp.concatenate([y_ssd, y_sb], axis=1)
    w_out, w_up, w_down = late_weights([late_a[0], late_b[0], late_b[1]])
    mix = _mm(ycat, w_out, tm=tm, tn=1024, tk=2048, name="mm_mix")[0]
    h1, xn2 = _mid_fwd_call(h0, mix, wt["mix_post_g"], wt["ffn_pre_g"])
    gu = _mm(xn2, w_up, tm=tm, tn=1408, tk=D_MODEL, name="mm_up")[0]
    gpre, act = _conv_fwd_call(gu, 0, D_FF, fcw, fw8, wt["ffn_conv_b"], 3, gate_src=gu, gate_col0=D_FF,
                               name="ffn_conv_fwd")
    f = _mm(act, w_down, tm=tm, tn=1024, tk=1408, name="mm_down")[0]
    loss_row, df, dh2, dg_ffn_post = _final_call(h1, f, wt["ffn_post_g"], target)

    dact = _mm(df, w_down, tb=True, tm=tm, tn=1408, tk=D_MODEL, name="mm_dact")[0]
    dw_down, dw_down_b = _mm(act, df, ta=True, tm=1408, tn=1024, tk=tkr, extra_bf16=True, name="mm_dw_down")
    by_chip = lambda g: g.reshape(N_CHIPS, -1, D_MODEL)
    dgate, dup, dfcw, dfcb = _conv_bwd_call(gu, 0, D_FF, fcw, fw8, 3, gpre, dact, gate_src=gu, gate_col0=D_FF,
                                            name="ffn_conv_bwd")
    dgu = jnp.concatenate([dgate, dup], axis=1)
    dxn2, land_down = _mm(dgu, w_up, tb=True, tm=tm, tn=1024, tk=1408, ride=_scatter_ride(by_chip(dw_down_b)),
                          name="mm_dxn2")
    dw_up, dw_up_b = _mm(xn2, dgu, ta=True, tm=1024, tn=1408, tk=tkr, nsplit=N_CHIPS, extra_bf16=True,
                         name="mm_dw_up")
    dh1, dmix, dg_ffn_pre, dg_mix_post = _mid_bwd_call(dh2, h1, dxn2[0], mix, wt["ffn_pre_g"], wt["mix_post_g"])
    dycat = _mm(dmix, w_out, tb=True, tm=tm, tn=1024, tk=D_MODEL, name="mm_dycat")[0]
    dw_out, dw_out_b = _mm(ycat, dmix, ta=True, tm=1024, tn=1024, tk=tkr, extra_bf16=True, name="mm_dw_out")
    do, dg_sb = _norm_bwd_call(o, wt["sb_norm_g"], dycat, 1, "sb_norm_bwd")
    dq, dk, dv, lands = _sb_bwd_call(proj, tl, do, _join_rides(_scatter_ride(dw_up_b),
                                                                  _scatter_ride(by_chip(dw_out_b))))
    dz, dxbc_act, ddt, dg_ssd, dd_x, dalog, dbias = _ssd_bwd_call(
        dycat, ypre, xbc, proj, sprev, bias, alog, d_x, wt["ssd_norm_g"])
    dxbc, dcw, dcb = _conv_bwd_call(proj, C_XBC, XBC, 512, cw8, 4, conv_pre, dxbc_act, name="ssd_conv_bwd")
    dproj = jnp.concatenate([dz, dxbc, ddt, dq, dk, dv], axis=1)
    dw_in, dw_in_b = w_in_shards(_mm(xn1, dproj, ta=True, tm=1024, tn=1152, tk=tkr, name="mm_dw_in")[0])
    dxn1, land_in = _mm(dproj, wt["w_in"], tb=True, tm=tm, tn=1024, tk=1152, ride=_scatter_ride(dw_in_b),
                        name="mm_dxn1")
    dh0, dg_pre = _norm_bwd_call(h0, wt["mix_pre_g"], dxn1[0], 0, "norm1_bwd", res=dh1)

    small = {
        "meta_tokens": dh0[PAD:BLK], "mix_pre_g": dg_pre, "ssd_conv_w": dcw[:4], "ssd_conv_b": dcb,
        "ssd_dt_bias": dbias[:, :N_HEADS], "ssd_a_log": dalog[:, :N_HEADS],
        "ssd_d": jnp.sum(dd_x.reshape(N_HEADS, HEAD_DIM), axis=1)[None],
        "ssd_norm_g": dg_ssd, "sb_norm_g": dg_sb, "mix_post_g": dg_mix_post, "ffn_pre_g": dg_ffn_pre,
        "ffn_conv_w": dfcw[:3], "ffn_conv_b": dfcb, "ffn_post_g": dg_ffn_post,
    }
    pending = {"w_in": (dw_in, land_in[0]), "w_out": (by_chip(dw_out), lands[1]), "w_up": (dw_up, lands[0]),
               "w_down": (by_chip(dw_down), land_down[0])}
    return loss_row, dh0[BLK:], small, pending


def _adamw_call(w, g, m, v, name):
    rows, cols = w.shape
    tr = 256 if rows % 256 == 0 else (352 if rows % 352 == 0 else rows)
    c1 = 1.0 - ADAM_B1 ** ADAM_STEP
    c2 = 1.0 - ADAM_B2 ** ADAM_STEP

    def body(w_ref, g_ref, m_ref, v_ref, d_ref, mo_ref, vo_ref):
        gv = g_ref[...]
        m2 = ADAM_B1 * m_ref[...] + (1.0 - ADAM_B1) * gv
        v2 = ADAM_B2 * v_ref[...] + (1.0 - ADAM_B2) * (gv * gv)
        d_ref[...] = -ADAM_LR * ((m2 / c1) / (jnp.sqrt(v2 / c2) + ADAM_EPS) + ADAM_WD * w_ref[...])
        mo_ref[...] = m2
        vo_ref[...] = v2

    spec = pl.BlockSpec((tr, cols), lambda i: (i, 0))
    return pl.pallas_call(
        body, name=name, grid=(rows // tr,), in_specs=[spec] * 4, out_specs=[spec] * 3,
        out_shape=[jax.ShapeDtypeStruct((rows, cols), F32)] * 3,
        compiler_params=_cparams(("parallel",)))(w, g, m, v)


ANY = pl.BlockSpec(memory_space=pl.ANY)


def _place():
    x, y, c = lax.axis_index("x"), lax.axis_index("y"), lax.axis_index("c")
    chips = [(1 - x, y), (x, 1 - y), (1 - x, 1 - y)]
    return x, y, c, chips


def _half(c, h):
    return pl.ds(pl.multiple_of(c * h, 8), h)


def _allgather_call(shards):
    n = len(shards)

    def body(*refs):
        ins, outs = refs[:n], refs[n:2 * n]
        send_i, recv_i, send_d, recv_d = refs[2 * n:]
        x, y, c, chips = _place()
        me = 2 * x + y
        sends = []
        for a in range(n):
            h = shards[a].shape[0] // 2
            for j, chip in enumerate(chips):
                cp = pltpu.make_async_remote_copy(
                    src_ref=ins[a].at[_half(c, h)], dst_ref=outs[a].at[me, _half(c, h)],
                    send_sem=send_i.at[3 * a + j], recv_sem=recv_i.at[3 * a + j],
                    device_id=(*chip, c), device_id_type=MESH)
                cp.start()
                sends.append(cp)
        for a in range(n):
            h = shards[a].shape[0] // 2
            for j, chip in enumerate(chips):
                src = 2 * chip[0] + chip[1]
                landed = outs[a].at[src, _half(c, h)]
                pltpu.make_async_remote_copy(
                    src_ref=landed, dst_ref=landed, send_sem=send_i.at[3 * a + j], recv_sem=recv_i.at[3 * a + j],
                    device_id=(*chip, c), device_id_type=MESH).wait_recv()
                cp = pltpu.make_async_remote_copy(
                    src_ref=landed, dst_ref=landed, send_sem=send_d.at[3 * a + j], recv_sem=recv_d.at[3 * a + j],
                    device_id=(x, y, 1 - c), device_id_type=MESH)
                cp.start()
                sends.append(cp)
        for a in range(n):
            h = shards[a].shape[0] // 2
            for j, chip in enumerate(chips):
                src = 2 * chip[0] + chip[1]
                other = outs[a].at[src, _half(1 - c, h)]
                pltpu.make_async_remote_copy(
                    src_ref=other, dst_ref=other, send_sem=send_d.at[3 * a + j], recv_sem=recv_d.at[3 * a + j],
                    device_id=(x, y, 1 - c), device_id_type=MESH).wait_recv()
        for cp in sends:
            cp.wait_send()

    return pl.pallas_call(
        body, name="allgather_weights", in_specs=[ANY] * n, out_specs=[ANY] * n,
        out_shape=[jax.ShapeDtypeStruct((N_CHIPS,) + s.shape, s.dtype) for s in shards],
        scratch_shapes=[pltpu.SemaphoreType.DMA((3 * n,))] * 4,
    )(*shards)


def _ride_scratch(ride):
    return [pltpu.SemaphoreType.DMA((ride[3],)), pltpu.SemaphoreType.DMA((ride[3],))]


def _ride_run(ride, src_refs, land_refs, send, recv, first, last):
    plan = ride[2]

    @pl.when(first)
    def _():
        for k, (src, dst, _, dev) in enumerate(plan(src_refs, land_refs)):
            pltpu.make_async_remote_copy(src_ref=src, dst_ref=dst, send_sem=send.at[k], recv_sem=recv.at[k],
                                         device_id=dev, device_id_type=MESH).start()

    @pl.when(last)
    def _():
        for k, (src, _, land, dev) in enumerate(plan(src_refs, land_refs)):
            cp = pltpu.make_async_remote_copy(src_ref=src, dst_ref=land, send_sem=send.at[k], recv_sem=recv.at[k],
                                              device_id=dev, device_id_type=MESH)
            cp.wait_send()
            cp.wait_recv()


def _join_rides(r1, r2):
    n1, l1 = len(r1[0]), len(r1[1])

    def plan(srcs, lands):
        return r1[2](srcs[:n1], lands[:l1]) + r2[2](srcs[n1:], lands[l1:])

    return (r1[0] + r2[0], r1[1] + r2[1], plan, r1[3] + r2[3])


def _gather_ride(shards):
    return (list(shards), [jax.ShapeDtypeStruct((N_CHIPS,) + s.shape, s.dtype) for s in shards],
            _gather_plan(len(shards)), 3 * len(shards))


def _scatter_ride(g_b):
    h = g_b.shape[1] // 2
    return ([g_b], [jax.ShapeDtypeStruct((8, h, g_b.shape[2]), BF16)], _scatter_plan(h), 7)


def _gather_plan(n):
    def plan(srcs, lands):
        x, y, c, chips = _place()
        me = 2 * x + y
        return [(srcs[a], lands[a].at[me], lands[a].at[2 * chip[0] + chip[1]], (*chip, c))
                for a in range(n) for chip in chips]
    return plan


def _scatter_plan(h):
    def plan(srcs, lands):
        x, y, c, _ = _place()
        me = 4 * x + 2 * y + c
        out = []
        for p in range(1, 8):
            px = 1 - x if p & 4 else x
            py = 1 - y if p & 2 else y
            pc = 1 - c if p & 1 else c
            out.append((srcs[0].at[2 * px + py, _half(pc, h)], lands[0].at[me],
                        lands[0].at[4 * px + 2 * py + pc], (px, py, pc)))
        return out
    return plan


def _grad_sum_call(own, land, place, name):
    _, h, cols = land.shape
    th = _pick(h, [256, 176, 8])
    nt = h // th

    def body(p_ref, own_ref, *refs):
        acc = own_ref[...]
        for r in refs[:7]:
            acc = acc + r[...].astype(F32)
        refs[7][...] = acc

    def peer(k):
        return pl.BlockSpec((None, th, cols), lambda i, p_ref: (p_ref[2 + k], i, 0))

    return pl.pallas_call(
        body, name=name,
        grid_spec=pltpu.PrefetchScalarGridSpec(
            num_scalar_prefetch=1, grid=(nt,),
            in_specs=[pl.BlockSpec((None, th, cols), lambda i, p_ref: (p_ref[1], p_ref[0] * nt + i, 0))]
            + [peer(k) for k in range(7)],
            out_specs=pl.BlockSpec((th, cols), lambda i, p_ref: (p_ref[0] * nt + i, 0))),
        out_shape=jax.ShapeDtypeStruct((2 * h, cols), F32),
        compiler_params=_cparams(("parallel",)))(place, own, *[land] * 7)


def _half_exchange_call(shards):
    n = len(shards)

    def body(*refs):
        outs = refs[n:2 * n]
        send_d, recv_d = refs[2 * n:]
        x, y, c, _ = _place()
        cps = []
        for a in range(n):
            h = shards[a].shape[0] // 2
            mine = outs[a].at[_half(c, h)]
            cp = pltpu.make_async_remote_copy(
                src_ref=mine, dst_ref=mine, send_sem=send_d.at[a], recv_sem=recv_d.at[a],
                device_id=(x, y, 1 - c), device_id_type=MESH)
            cp.start()
            cps.append(cp)
        for a, cp in enumerate(cps):
            h = shards[a].shape[0] // 2
            theirs = outs[a].at[_half(1 - c, h)]
            pltpu.make_async_remote_copy(
                src_ref=theirs, dst_ref=theirs, send_sem=send_d.at[a], recv_sem=recv_d.at[a],
                device_id=(x, y, 1 - c), device_id_type=MESH).wait_recv()
            cp.wait_send()

    return pl.pallas_call(
        body, name="grad_half_exchange", in_specs=[ANY] * n, out_specs=[ANY] * n,
        out_shape=[jax.ShapeDtypeStruct(sv.shape, F32) for sv in shards],
        input_output_aliases={a: a for a in range(n)},
        scratch_shapes=[pltpu.SemaphoreType.DMA((n,))] * 2,
    )(*shards)


def _allreduce_small_call(arrs):
    n = len(arrs)
    offs, rows = [], 0
    for a in arrs:
        offs.append(rows)
        rows += a.shape[0]
    rows = -(-rows // 8) * 8
    width = -(-max(a.shape[1] for a in arrs) // BLK) * BLK

    def body(*refs):
        ins, outs = refs[:n], refs[n:2 * n]
        gath, send_sems, recv_sems = refs[2 * n:]
        x, y, c, chips = _place()
        me, sibling = (x, y, c), (x, y, 1 - c)

        def slot(px, py, pc):
            return gath.at[4 * px + 2 * py + pc]

        def copy(k, block, to):
            return pltpu.make_async_remote_copy(
                src_ref=slot(*block), dst_ref=slot(*block),
                send_sem=send_sems.at[k], recv_sem=recv_sems.at[k], device_id=to, device_id_type=MESH)

        mine = slot(*me)
        mine[...] = jnp.zeros((rows, width), F32)
        for k in range(n):
            r, w = arrs[k].shape
            mine[offs[k]:offs[k] + r, 0:w] = ins[k][...]
        first = [copy(0, me, sibling)]
        first += [copy(1 + j, me, (*chip, c)) for j, chip in enumerate(chips)]
        for cp in first:
            cp.start()
        passed = [copy(4 + j, (*chip, c), sibling) for j, chip in enumerate(chips)]
        for j, chip in enumerate(chips):
            copy(1 + j, (*chip, c), me).wait_recv()
            passed[j].start()
        copy(0, sibling, me).wait_recv()
        for j, chip in enumerate(chips):
            copy(4 + j, (*chip, 1 - c), me).wait_recv()
        for cp in first + passed:
            cp.wait_send()
        acc = gath[0]
        for d in range(1, 8):
            acc = acc + gath[d]
        for k in range(n):
            r, w = arrs[k].shape
            outs[k][...] = acc[offs[k]:offs[k] + r, 0:w]

    vm = pl.BlockSpec(memory_space=pltpu.VMEM)
    return pl.pallas_call(
        body, name="allreduce_small", in_specs=[vm] * n, out_specs=[vm] * n,
        out_shape=[jax.ShapeDtypeStruct(a.shape, F32) for a in arrs],
        scratch_shapes=[pltpu.VMEM((8, rows, width), F32), pltpu.SemaphoreType.DMA((7,)),
                        pltpu.SemaphoreType.DMA((7,))],
        compiler_params=pltpu.CompilerParams(vmem_limit_bytes=VMEM_LIMIT),
    )(*arrs)


def _adamw_small_call(ws, gs, ms, vs):
    n = len(ws)
    c1 = 1.0 - ADAM_B1 ** ADAM_STEP
    c2 = 1.0 - ADAM_B2 ** ADAM_STEP

    def body(*refs):
        for k in range(n):
            w_ref, g_ref, m_ref, v_ref = (refs[j * n + k] for j in range(4))
            d_ref, mo_ref, vo_ref = (refs[(4 + j) * n + k] for j in range(3))
            gv = g_ref[...]
            m2 = ADAM_B1 * m_ref[...] + (1.0 - ADAM_B1) * gv
            v2 = ADAM_B2 * v_ref[...] + (1.0 - ADAM_B2) * (gv * gv)
            d_ref[...] = -ADAM_LR * ((m2 / c1) / (jnp.sqrt(v2 / c2) + ADAM_EPS) + ADAM_WD * w_ref[...])
            mo_ref[...] = m2
            vo_ref[...] = v2

    vm = pl.BlockSpec(memory_space=pltpu.VMEM)
    res = pl.pallas_call(
        body, name="adamw_small", in_specs=[vm] * (4 * n), out_specs=[vm] * (3 * n),
        out_shape=[jax.ShapeDtypeStruct(a.shape, F32) for a in ws] * 3,
        compiler_params=pltpu.CompilerParams(vmem_limit_bytes=VMEM_LIMIT),
    )(*ws, *gs, *ms, *vs)
    return res[:n], res[n:2 * n], res[2 * n:]


def _pack(arrs, min_rows=8):
    parts = []
    for a in arrs:
        flat = a.reshape(-1).astype(F32)
        parts.append(jnp.pad(flat, (0, (-flat.shape[0]) % BLK)))
    buf = jnp.concatenate(parts).reshape(-1, BLK)
    return jnp.pad(buf, ((0, (-buf.shape[0]) % min_rows), (0, 0)))


def _unpack(buf, shapes):
    out, r = [], 0
    for shp in shapes:
        n = math.prod(shp)
        nr = -(-n // BLK)
        out.append(buf[r:r + nr].reshape(-1)[:n].reshape(shp))
        r += nr
    return out


SMALL = ["meta_tokens", "mix_pre_g", "ssd_conv_w", "ssd_conv_b", "ssd_dt_bias", "ssd_a_log", "ssd_d", "ssd_norm_g",
         "sb_norm_g", "mix_post_g", "ffn_pre_g", "ffn_conv_w", "ffn_conv_b", "ffn_post_g"]
BIG = ["w_in", "w_out", "w_up", "w_down"]
WEIGHTS = ["meta_tokens", "mix_pre_g", "w_in", "ssd_conv_w", "ssd_conv_b", "ssd_dt_bias", "ssd_a_log", "ssd_d",
           "ssd_norm_g", "sb_norm_g", "w_out", "mix_post_g", "ffn_pre_g", "w_up", "ffn_conv_w", "ffn_conv_b",
           "w_down", "ffn_post_g"]
W_IN_SHARD = IN_COLS // N_CHIPS
W_IN_PAD = 1536


def kernel(x, meta_tokens, mix_pre_g, w_in, ssd_conv_w, ssd_conv_b, ssd_dt_bias, ssd_a_log, ssd_d, ssd_norm_g, sb_norm_g, w_out, mix_post_g, ffn_pre_g, w_up, ffn_conv_w, ffn_conv_b, w_down, ffn_post_g, loss_target, m_meta_tokens, m_mix_pre_g, m_w_in, m_ssd_conv_w, m_ssd_conv_b, m_ssd_dt_bias, m_ssd_a_log, m_ssd_d, m_ssd_norm_g, m_sb_norm_g, m_w_out, m_mix_post_g, m_ffn_pre_g, m_w_up, m_ffn_conv_w, m_ffn_conv_b, m_w_down, m_ffn_post_g, v_meta_tokens, v_mix_pre_g, v_w_in, v_ssd_conv_w, v_ssd_conv_b, v_ssd_dt_bias, v_ssd_a_log, v_ssd_d, v_ssd_norm_g, v_sb_norm_g, v_w_out, v_mix_post_g, v_ffn_pre_g, v_w_up, v_ffn_conv_w, v_ffn_conv_b, v_w_down, v_ffn_post_g):
    w = dict(meta_tokens=meta_tokens, mix_pre_g=mix_pre_g, w_in=w_in, ssd_conv_w=ssd_conv_w, ssd_conv_b=ssd_conv_b, ssd_dt_bias=ssd_dt_bias, ssd_a_log=ssd_a_log, ssd_d=ssd_d, ssd_norm_g=ssd_norm_g, sb_norm_g=sb_norm_g, w_out=w_out, mix_post_g=mix_post_g, ffn_pre_g=ffn_pre_g, w_up=w_up, ffn_conv_w=ffn_conv_w, ffn_conv_b=ffn_conv_b, w_down=w_down, ffn_post_g=ffn_post_g)
    m = dict(meta_tokens=m_meta_tokens, mix_pre_g=m_mix_pre_g, w_in=m_w_in, ssd_conv_w=m_ssd_conv_w, ssd_conv_b=m_ssd_conv_b, ssd_dt_bias=m_ssd_dt_bias, ssd_a_log=m_ssd_a_log, ssd_d=m_ssd_d, ssd_norm_g=m_ssd_norm_g, sb_norm_g=m_sb_norm_g, w_out=m_w_out, mix_post_g=m_mix_post_g, ffn_pre_g=m_ffn_pre_g, w_up=m_w_up, ffn_conv_w=m_ffn_conv_w, ffn_conv_b=m_ffn_conv_b, w_down=m_w_down, ffn_post_g=m_ffn_post_g)
    v = dict(meta_tokens=v_meta_tokens, mix_pre_g=v_mix_pre_g, w_in=v_w_in, ssd_conv_w=v_ssd_conv_w, ssd_conv_b=v_ssd_conv_b, ssd_dt_bias=v_ssd_dt_bias, ssd_a_log=v_ssd_a_log, ssd_d=v_ssd_d, ssd_norm_g=v_ssd_norm_g, sb_norm_g=v_sb_norm_g, w_out=v_w_out, mix_post_g=v_mix_post_g, ffn_pre_g=v_ffn_pre_g, w_up=v_w_up, ffn_conv_w=v_ffn_conv_w, ffn_conv_b=v_ffn_conv_b, w_down=v_w_down, ffn_post_g=v_ffn_post_g)
    chip = 2 * lax.axis_index("x") + lax.axis_index("y")
    me = 2 * chip + lax.axis_index("c")
    place = jnp.stack([lax.axis_index("c"), chip] + [me ^ p for p in range(1, 8)]).astype(jnp.int32)

    shard_small = [w["meta_tokens"], w["ssd_conv_w"][0], w["ffn_conv_w"][0]]
    shards = [jnp.pad(w["w_in"][0], ((0, 0), (0, W_IN_PAD - W_IN_SHARD))).astype(BF16), _pack(shard_small, 16)]
    gathered = _allgather_call(shards)
    late_shards = [w["w_out"][0].astype(BF16), w["w_up"][0].astype(BF16), w["w_down"][0].astype(BF16)]

    def blocks(own, got):
        return [jnp.where(chip == i, own, got[i]) for i in range(N_CHIPS)]

    def late_weights(got):
        return (jnp.concatenate(blocks(late_shards[0], got[0]), axis=0),
                jnp.concatenate(blocks(late_shards[1], got[1]), axis=1),
                jnp.concatenate(blocks(late_shards[2], got[2]), axis=0))

    cut = DT_REAL_OFF + N_HEADS - W_IN_SHARD
    s_in = blocks(shards[0], gathered[0])
    w_in_c = jnp.concatenate(
        [s_in[0][:, :W_IN_SHARD], s_in[1][:, :cut], jnp.zeros((D_MODEL, BLK - N_HEADS), BF16),
         s_in[1][:, cut:W_IN_SHARD], s_in[2][:, :W_IN_SHARD], s_in[3][:, :W_IN_SHARD]], axis=1)
    parts = [_unpack(b, [s.shape for s in shard_small]) for b in blocks(shards[1], gathered[1])]
    wt = {k: w[k][0][None] if w[k].ndim == 3 else w[k] for k in
          ["mix_pre_g", "ssd_conv_b", "ssd_dt_bias", "ssd_a_log", "ssd_d", "ssd_norm_g", "sb_norm_g", "mix_post_g",
           "ffn_pre_g", "ffn_conv_b", "ffn_post_g"]}
    wt.update(
        meta=jnp.concatenate([p[0] for p in parts], axis=1),
        ssd_conv_w=jnp.concatenate([p[1] for p in parts], axis=1),
        ffn_conv_w=jnp.concatenate([p[2] for p in parts], axis=1), w_in=w_in_c)

    def w_in_shards(g):
        skip = BLK - N_HEADS
        cols = [g[:, :W_IN_SHARD],
                jnp.concatenate([g[:, W_IN_SHARD:W_IN_SHARD + cut], g[:, C_Q:2 * W_IN_SHARD + skip]], axis=1),
                g[:, 2 * W_IN_SHARD + skip:3 * W_IN_SHARD + skip], g[:, 3 * W_IN_SHARD + skip:]]
        g = jnp.stack([jnp.pad(b, ((0, 0), (0, W_IN_PAD - W_IN_SHARD))) for b in cols])
        return g, g.astype(BF16)

    loss_row, dx, small, pending = _local_step(x[0], loss_target[0], wt, late_shards, late_weights, w_in_shards)

    full = _half_exchange_call([_grad_sum_call(*pending[k], place, "grad_sum_" + k) for k in BIG])
    grads = {"w_in": full[0][:, :W_IN_SHARD], "w_out": full[1], "w_up": full[2], "w_down": full[3]}

    red_list = _allreduce_small_call([small[k] for k in SMALL] + [loss_row])
    loss = jnp.sum(red_list[-1])
    for k, g in zip(SMALL, red_list[:-1]):
        grads[k] = g
    for k in ["meta_tokens", "ssd_conv_w", "ffn_conv_w"]:
        wk = w[k].shape[-1]
        grads[k] = lax.dynamic_slice_in_dim(grads[k], chip * wk, wk, axis=1)

    delta, new_m, new_v = {}, {}, {}
    for k in BIG:
        delta[k], new_m[k], new_v[k] = _adamw_call(w[k][0], grads[k], m[k][0], v[k][0], "adamw_" + k)
    flat = lambda d: [d[k].reshape(grads[k].shape) for k in SMALL]
    res = _adamw_small_call(flat(w), [grads[k] for k in SMALL], flat(m), flat(v))
    for out, arrs in zip((delta, new_m, new_v), res):
        for k, a in zip(SMALL, arrs):
            out[k] = a

    def shaped(d, k):
        return d[k].reshape(w[k].shape)

    return (loss, dx[None], *[shaped(grads, k) for k in WEIGHTS], *[shaped(delta, k) for k in WEIGHTS],
            *[shaped(new_m, k) for k in WEIGHTS], *[shaped(new_v, k) for k in WEIGHTS])
```

```python
import functools
import math

import jax
import jax.numpy as jnp
from jax import lax
from jax.experimental import pallas as pl
from jax.experimental.pallas import tpu as pltpu

F32 = jnp.float32
BF16 = jnp.bfloat16

D_MODEL = 1024
N_META = 16
BLK = 128
PAD = BLK - N_META
HEAD_DIM = 64
N_HEADS = 16
SSD_GROUPS = 2
SSD_STATE = 128
SSD_INNER = 1024
XBC = SSD_INNER + 2 * SSD_GROUPS * SSD_STATE
D_FF = 2816
EPS = 1e-6
IN_COLS = 5648
C_Z, C_XBC, C_DT, C_Q, C_K, C_V, C_END = 0, 1024, 2560, 2688, 3712, 4736, 5760
DT_REAL_OFF = 2560
N_CHIPS = 4
ADAM_LR, ADAM_B1, ADAM_B2, ADAM_EPS, ADAM_WD, ADAM_STEP = 0.001, 0.9, 0.999, 1e-08, 0.01, 10
VMEM_LIMIT = 56 * 1024 * 1024
MESH = pl.DeviceIdType.MESH


def _cparams(sem=None, **kw):
    if sem is not None:
        kw["dimension_semantics"] = sem
    return pltpu.CompilerParams(vmem_limit_bytes=VMEM_LIMIT, **kw)


def _pick(n, cands):
    for c in cands:
        if n % c == 0:
            return c
    raise ValueError((n, cands))


def _iota(shape, dim):
    return lax.broadcasted_iota(jnp.int32, shape, dim)


def _sigmoid(x):
    return 1.0 / (1.0 + jnp.exp(-x))


def _split2(v):
    h1 = v.astype(BF16)
    return h1, (v - h1.astype(F32)).astype(BF16)


def _dot(a, b, ca=1, cb=0):
    return lax.dot_general(a, b, (((ca,), (cb,)), ((), ())), preferred_element_type=F32)


def _dot_sel_r(v, sel, cb=0):
    h1, h2 = _split2(v)
    return _dot(h1, sel, 1, cb) + _dot(h2, sel, 1, cb)


def _dot_sel_l(sel, v, ca=1):
    h1, h2 = _split2(v)
    return _dot(sel, h1, ca, 0) + _dot(sel, h2, ca, 0)


def _mm(a, b, *, ta=False, tb=False, tm, tn, tk, out_dtype=F32, nsplit=1, extra_bf16=False, ride=None, name):
    K, M = (a.shape if ta else a.shape[::-1])
    N = b.shape[0] if tb else b.shape[1]
    assert M % tm == 0 and N % tn == 0 and K % tk == 0, (name, M, N, K, tm, tn, tk)
    nm, nn, nk = M // tm, N // tn, K // tk
    assert nn % nsplit == 0
    per = nn // nsplit
    a_spec = (pl.BlockSpec((tk, tm), lambda i, j, k: (k, i)) if ta
              else pl.BlockSpec((tm, tk), lambda i, j, k: (i, k)))
    b_spec = (pl.BlockSpec((tn, tk), lambda i, j, k: (j, k)) if tb
              else pl.BlockSpec((tk, tn), lambda i, j, k: (k, j)))
    o_spec = pl.BlockSpec((None, tm, tn), lambda i, j, k: (j // per, i, j % per))
    n_out = 2 if extra_bf16 else 1
    ca, cb = (0 if ta else 1), (1 if tb else 0)
    ns, nl = (len(ride[0]), len(ride[1])) if ride else (0, 0)

    def body(a_ref, b_ref, *rest):
        outs = rest[ns:ns + n_out]
        if ride:
            step = (pl.program_id(0) * nn + pl.program_id(1)) * nk + pl.program_id(2)
            _ride_run(ride, rest[:ns], rest[ns + n_out:ns + n_out + nl], rest[-2], rest[-1],
                      step == 0, step == nm * nn * nk - 1)
        p = _dot(a_ref[...].astype(BF16), b_ref[...].astype(BF16), ca, cb)

        def emit(val):
            outs[0][...] = val.astype(out_dtype)
            if extra_bf16:
                outs[1][...] = val.astype(BF16)

        if nk == 1:
            emit(p)
        else:
            acc = rest[ns + n_out + nl]
            k = pl.program_id(2)

            @pl.when(k == 0)
            def _():
                acc[...] = p

            @pl.when(k > 0)
            def _():
                acc[...] += p

            @pl.when(k == nk - 1)
            def _():
                emit(acc[...])

    shp = (nsplit, M, N // nsplit)
    out_shape = [jax.ShapeDtypeStruct(shp, out_dtype)]
    out_specs = [o_spec]
    if extra_bf16:
        out_shape.append(jax.ShapeDtypeStruct(shp, BF16))
        out_specs.append(o_spec)
    scratch = [pltpu.VMEM((tm, tn), F32)] if nk > 1 else []
    if ride:
        res = pl.pallas_call(
            body, name=name, grid=(nm, nn, nk), in_specs=[a_spec, b_spec] + [ANY] * ns,
            out_specs=out_specs + [ANY] * nl, out_shape=out_shape + list(ride[1]),
            scratch_shapes=scratch + _ride_scratch(ride),
            compiler_params=_cparams(("arbitrary", "arbitrary", "arbitrary")),
        )(a, b, *ride[0])
        return (res[:n_out] if extra_bf16 else res[0]), list(res[n_out:])
    res = pl.pallas_call(
        body, name=name, grid=(nm, nn, nk), in_specs=[a_spec, b_spec], out_specs=out_specs,
        out_shape=out_shape, scratch_shapes=scratch,
        compiler_params=_cparams(("parallel", "parallel", "arbitrary")),
    )(a, b)
    return res if extra_bf16 else res[0]


def _rms_stats(x):
    r = lax.rsqrt(jnp.mean(x * x, axis=-1, keepdims=True) + EPS)
    return r, x * r


def _rms_bwd(x, g, dy):
    r, xh = _rms_stats(x)
    dxh = dy * g
    dx = r * (dxh - xh * jnp.mean(dxh * xh, axis=-1, keepdims=True))
    return dx, jnp.sum(dy * xh, axis=0, keepdims=True)


def _row_spec(tr, w, col=0):
    return pl.BlockSpec((tr, w), lambda i: (i, col))


def _vec_spec(w):
    return pl.BlockSpec((1, w), lambda i: (0, 0))


def _acc_rows(ref, val, i):
    @pl.when(i == 0)
    def _():
        ref[...] = val

    @pl.when(i > 0)
    def _():
        ref[...] += val


def _rms_fwd_call(x, g, name, beside=None):
    lp, w = x.shape
    tr = _pick(lp, [384, 128])

    def body(x_ref, g_ref, *rest):
        _, xh = _rms_stats(x_ref[...])
        rest[-1][...] = (xh * g_ref[...]).astype(BF16)

    if beside is None:
        return pl.pallas_call(
            body, name=name, grid=(lp // tr,), in_specs=[_row_spec(tr, w), _vec_spec(w)],
            out_specs=_row_spec(tr, w), out_shape=jax.ShapeDtypeStruct((lp, w), BF16),
            compiler_params=_cparams(("parallel",)))(x, g)
    return pl.pallas_call(
        body, name=name, grid=(lp // tr,), in_specs=[_row_spec(tr, w), _vec_spec(w), ANY],
        out_specs=_row_spec(tr, w, 1), out_shape=jax.ShapeDtypeStruct((lp, 2 * w), BF16),
        input_output_aliases={2: 0}, compiler_params=_cparams(("parallel",)))(x, g, beside)


def _mid_fwd_call(h0, mix, g_post, g_pre2):
    lp, w = h0.shape
    tr = _pick(lp, [384, 128])

    def body(h0_ref, mix_ref, gp_ref, g2_ref, h1_ref, xn_ref):
        _, mh = _rms_stats(mix_ref[...])
        h1 = h0_ref[...] + mh * gp_ref[...]
        h1_ref[...] = h1
        _, hh = _rms_stats(h1)
        xn_ref[...] = (hh * g2_ref[...]).astype(BF16)

    return pl.pallas_call(
        body, name="mid_fwd", grid=(lp // tr,),
        in_specs=[_row_spec(tr, w), _row_spec(tr, w), _vec_spec(w), _vec_spec(w)],
        out_specs=[_row_spec(tr, w), _row_spec(tr, w)],
        out_shape=[jax.ShapeDtypeStruct((lp, w), F32), jax.ShapeDtypeStruct((lp, w), BF16)],
        compiler_params=_cparams(("parallel",)))(h0, mix, g_post, g_pre2)


def _final_call(h1, f, g_post, target):
    lp, w = h1.shape
    tr = BLK
    nb = lp // tr

    def body(h1_ref, f_ref, g_ref, t_ref, loss_ref, df_ref, dh_ref, dg_ref):
        i = pl.program_id(0)
        fv = f_ref[...]
        g = g_ref[...]
        _, fh = _rms_stats(fv)
        h2 = h1_ref[...] + fh * g
        diff = jnp.where(i > 0, h2 - t_ref[...], 0.0)
        part = 0.5 * jnp.sum(diff * diff, axis=0, keepdims=True) * (1.0 / w)
        _acc_rows(loss_ref, part, i)
        dh = diff * (1.0 / w)
        dh_ref[...] = dh
        df, dg = _rms_bwd(fv, g, dh)
        df_ref[...] = df.astype(BF16)
        _acc_rows(dg_ref, dg, i)

    t_spec = pl.BlockSpec((tr, w), lambda i: (jnp.maximum(i - 1, 0), 0))
    return pl.pallas_call(
        body, name="final_fwd_bwd", grid=(nb,),
        in_specs=[_row_spec(tr, w), _row_spec(tr, w), _vec_spec(w), t_spec],
        out_specs=[_vec_spec(w), _row_spec(tr, w), _row_spec(tr, w), _vec_spec(w)],
        out_shape=[jax.ShapeDtypeStruct((1, w), F32), jax.ShapeDtypeStruct((lp, w), BF16),
                   jax.ShapeDtypeStruct((lp, w), F32), jax.ShapeDtypeStruct((1, w), F32)],
        compiler_params=_cparams(("arbitrary",)))(h1, f, g_post, target)


def _mid_bwd_call(dh2, h1, dxn2, mix, g_pre2, g_post):
    lp, w = h1.shape
    tr = _pick(lp, [384, 128])

    def body(dh2_ref, h1_ref, dxn_ref, mix_ref, g2_ref, gp_ref, dh1_ref, dmix_ref, dg2_ref, dgp_ref):
        i = pl.program_id(0)
        live = (i * tr + _iota((tr, 1), 0)) >= PAD
        dx, dg2 = _rms_bwd(h1_ref[...], g2_ref[...], dxn_ref[...])
        dh1 = jnp.where(live, dh2_ref[...] + dx, 0.0)
        dh1_ref[...] = dh1
        dmix, dgp = _rms_bwd(mix_ref[...], gp_ref[...], dh1)
        dmix_ref[...] = jnp.where(live, dmix, 0.0).astype(BF16)
        _acc_rows(dg2_ref, dg2, i)
        _acc_rows(dgp_ref, dgp, i)

    rs = _row_spec(tr, w)
    return pl.pallas_call(
        body, name="mid_bwd", grid=(lp // tr,),
        in_specs=[rs, rs, rs, rs, _vec_spec(w), _vec_spec(w)],
        out_specs=[rs, rs, _vec_spec(w), _vec_spec(w)],
        out_shape=[jax.ShapeDtypeStruct((lp, w), F32), jax.ShapeDtypeStruct((lp, w), BF16),
                   jax.ShapeDtypeStruct((1, w), F32), jax.ShapeDtypeStruct((1, w), F32)],
        compiler_params=_cparams(("arbitrary",)))(dh2, h1, dxn2, mix, g_pre2, g_post)


def _norm_bwd_call(x, g, dy_arr, dy_col, name, res=None):
    lp, w = x.shape
    tr = _pick(lp, [384, 128])
    has_res = res is not None

    def body(x_ref, g_ref, dy_ref, *rest):
        i = pl.program_id(0)
        live = (i * tr + _iota((tr, 1), 0)) >= PAD
        dx, dg = _rms_bwd(x_ref[...], g_ref[...], dy_ref[...])
        if has_res:
            dx = dx + rest[0][...]
        out_ref, dg_ref = rest[-2], rest[-1]
        out_ref[...] = jnp.where(live, dx, 0.0)
        _acc_rows(dg_ref, dg, i)

    rs = _row_spec(tr, w)
    ins = [rs, _vec_spec(w), _row_spec(tr, w, dy_col)] + ([rs] if has_res else [])
    args = [x, g, dy_arr] + ([res] if has_res else [])
    return pl.pallas_call(
        body, name=name, grid=(lp // tr,), in_specs=ins, out_specs=[rs, _vec_spec(w)],
        out_shape=[jax.ShapeDtypeStruct((lp, w), F32), jax.ShapeDtypeStruct((1, w), F32)],
        compiler_params=_cparams(("arbitrary",)))(*args)


def _shift_down(cur, prev_tail, s, rows):
    if s == 0:
        return cur
    prev = jnp.tile(prev_tail, (cur.shape[0] // 8, 1))
    return jnp.where(rows >= s, pltpu.roll(cur, s, 0), pltpu.roll(prev, s, 0))


def _shift_up(cur, next_head, s, rows):
    if s == 0:
        return cur
    n = cur.shape[0]
    nxt = jnp.tile(next_head, (n // 8, 1))
    return jnp.where(rows < n - s, pltpu.roll(cur, n - s, 0), pltpu.roll(nxt, n - s, 0))


def _gelu_tanh(x):
    c = math.sqrt(2.0 / math.pi)
    t = jnp.tanh(c * (x + 0.044715 * x * x * x))
    return 0.5 * x * (1.0 + t), t


def _conv_fwd_call(src, col0, width, cw, w8, b, taps, *, gate_src=None, gate_col0=0, rb=BLK, name):
    lp = src.shape[0]
    nb, nc = lp // rb, width // cw
    cb0 = col0 // cw
    ffn = gate_src is not None

    def body(x_ref, w_ref, b_ref, *rest):
        if ffn:
            u_ref, y_ref, a_ref, tail = rest
        else:
            y_ref, a_ref, tail = rest
        i = pl.program_id(1)

        @pl.when(i == 0)
        def _():
            tail[...] = jnp.zeros_like(tail)

        cur = x_ref[...]
        rows = _iota((rb, cw), 0)
        y = b_ref[...] + w_ref[taps - 1:taps, :] * cur
        pt = tail[...]
        for s in range(1, taps):
            y = y + w_ref[taps - 1 - s:taps - s, :] * _shift_down(cur, pt, s, rows)
        tail[...] = cur[rb - 8:, :]
        y_ref[...] = y
        if ffn:
            ge, _ = _gelu_tanh(y)
            a_ref[...] = (ge * u_ref[...]).astype(BF16)
        else:
            live = (i * rb + rows) >= PAD
            a_ref[...] = jnp.where(live, y * _sigmoid(y), 0.0)

    blk = lambda c0: pl.BlockSpec((rb, cw), lambda j, i: (i, c0 + j))
    ins = [blk(cb0), pl.BlockSpec((8, cw), lambda j, i: (0, j)), pl.BlockSpec((1, cw), lambda j, i: (0, j))]
    args = [src, w8, b]
    if ffn:
        ins.append(blk(gate_col0 // cw))
        args.append(gate_src)
    return pl.pallas_call(
        body, name=name, grid=(nc, nb), in_specs=ins, out_specs=[blk(0), blk(0)],
        out_shape=[jax.ShapeDtypeStruct((lp, width), F32),
                   jax.ShapeDtypeStruct((lp, width), BF16 if ffn else F32)],
        scratch_shapes=[pltpu.VMEM((8, cw), F32)],
        compiler_params=_cparams(("parallel", "arbitrary")))(*args)


def _conv_bwd_call(src, col0, width, cw, w8, taps, ypre, dact, *, gate_src=None, gate_col0=0, rb=BLK, name):
    lp = src.shape[0]
    nb, nc = lp // rb, width // cw
    cb0 = col0 // cw
    ffn = gate_src is not None

    def body(x_ref, w_ref, y_ref, d_ref, *rest):
        if ffn:
            u_ref, dx_ref, du_ref, dw_ref, db_ref, head = rest
        else:
            dx_ref, dw_ref, db_ref, head = rest
        step = pl.program_id(1)
        i = nb - 1 - step

        @pl.when(step == 0)
        def _():
            head[...] = jnp.zeros_like(head)

        rows = _iota((rb, cw), 0)
        live = (i * rb + rows) >= PAD
        y = y_ref[...]
        d = d_ref[...]
        if ffn:
            ge, t = _gelu_tanh(y)
            c = math.sqrt(2.0 / math.pi)
            dge = 0.5 * (1.0 + t) + 0.5 * y * (1.0 - t * t) * c * (1.0 + 3.0 * 0.044715 * y * y)
            u = u_ref[...]
            du_ref[...] = jnp.where(live, d * ge, 0.0).astype(BF16)
            dy = jnp.where(live, d * u * dge, 0.0)
        else:
            sg = _sigmoid(y)
            dy = jnp.where(live, d * sg * (1.0 + y * (1.0 - sg)), 0.0)
        x = x_ref[...]
        nh = head[...]
        dx = jnp.zeros_like(dy)
        dws = []
        for s in range(taps):
            sh = _shift_up(dy, nh, s, rows)
            dx = dx + w_ref[taps - 1 - s:taps - s, :] * sh
            dws.append(jnp.sum(x * sh, axis=0, keepdims=True))
        head[...] = dy[:8, :]
        dx_ref[...] = jnp.where(live, dx, 0.0).astype(BF16)
        dw = jnp.concatenate([dws[taps - 1 - k] for k in range(taps)]
                             + [jnp.zeros((8 - taps, cw), F32)], axis=0)
        _acc_rows(dw_ref, dw, step)
        _acc_rows(db_ref, jnp.sum(dy, axis=0, keepdims=True), step)

    blk = lambda c0: pl.BlockSpec((rb, cw), lambda j, s: (nb - 1 - s, c0 + j))
    ins = [blk(cb0), pl.BlockSpec((8, cw), lambda j, s: (0, j)), blk(0), blk(0)]
    args = [src, w8, ypre, dact]
    outs = [blk(0)]
    oshape = [jax.ShapeDtypeStruct((lp, width), BF16)]
    if ffn:
        ins.append(blk(gate_col0 // cw))
        args.append(gate_src)
        outs.append(blk(0))
        oshape.append(jax.ShapeDtypeStruct((lp, width), BF16))
    outs += [pl.BlockSpec((8, cw), lambda j, s: (0, j)), pl.BlockSpec((1, cw), lambda j, s: (0, j))]
    oshape += [jax.ShapeDtypeStruct((8, width), F32), jax.ShapeDtypeStruct((1, width), F32)]
    return pl.pallas_call(
        body, name=name, grid=(nc, nb), in_specs=ins, out_specs=outs, out_shape=oshape,
        scratch_shapes=[pltpu.VMEM((8, cw), F32)],
        compiler_params=_cparams(("parallel", "arbitrary")))(*args)


SB_GROUP = 4
SB_DEAD = -110.0


def _sb_scores(qm_h, kb):
    z = _dot(qm_h, kb, 1, 1)
    sp = jnp.maximum(z, 0.0) + jnp.log(1.0 + jnp.exp(-jnp.abs(z)))
    return z - sp, -sp


def _sb_valid(i, off, width):
    kpos = off + _iota((BLK, width), 1)
    qpos = i * BLK + _iota((BLK, width), 0)
    return (kpos < qpos) & (kpos >= PAD)


def _dot_tri1(v, tri2):
    r = _dot(v.astype(BF16), tri2[:BLK])
    return r[:, :BLK], r[:, BLK:]


def _sb_groups(i):
    edge = i // SB_GROUP
    return edge, pl.multiple_of(edge * (SB_GROUP * BLK), BLK)


def _tri2(cond):
    t = jnp.concatenate([cond.astype(BF16), jnp.ones((BLK, BLK), BF16)], axis=1)
    return jnp.concatenate([t, t], axis=0)


def _dot_tri(v, tri2):
    hi = v.astype(BF16)
    lo = (v - hi.astype(F32)).astype(BF16)
    r = _dot(jnp.concatenate([hi, lo], axis=1), tri2)
    return r[:, :BLK], r[:, BLK:]


def _sb_edge(i, edge, edge_off, chunk):
    live = i - edge * SB_GROUP + 1
    merged = jnp.logical_and(live <= SB_GROUP // 2, edge >= 1).astype(jnp.int32)
    before = pl.multiple_of(jnp.maximum(edge_off - SB_GROUP * BLK, 0), BLK)
    upto = [functools.partial(chunk, edge_off, n) for n in range(1, SB_GROUP + 1)]
    upto += [functools.partial(chunk, before, SB_GROUP + n) for n in range(1, SB_GROUP // 2 + 1)]
    return merged, upto, live - 1 + merged * SB_GROUP


def _sb_fwd_call(proj, ride):
    lp = proj.shape[0]
    nb = lp // BLK
    assert (nb - 1) % SB_GROUP == 0
    scale = 1.0 / math.sqrt(HEAD_DIM)

    ns, nl = len(ride[0]), len(ride[1])
    npair = N_HEADS // 2

    def body(q_ref, k_ref, v_ref, *rest):
        o_ref, tl_ref = rest[ns], rest[ns + 1]
        i = pl.program_id(1)
        step = pl.program_id(0) * nb + i
        _ride_run(ride, rest[:ns], rest[ns + 2:ns + 2 + nl], rest[-2], rest[-1], step == 0, step == npair * nb - 1)
        lane = _iota((2 * BLK, BLK), 1)
        row = _iota((2 * BLK, BLK), 0)
        first = row < BLK
        qrow = row & (BLK - 1)
        q = q_ref[...] * scale
        q2 = jnp.concatenate([q, q], axis=0)
        qm = jnp.where(first == (lane < HEAD_DIM), q2, 0.0).astype(BF16)
        tri = _tri2(_iota((BLK, BLK), 0) > _iota((BLK, BLK), 1))

        def chunk(off, nsub, last_valid, carry):
            width = nsub * BLK
            sls = [slice(b * BLK, (b + 1) * BLK) for b in range(nsub)]
            kb = k_ref[pl.ds(off, width), :].astype(BF16)
            vb = v_ref[pl.ds(off, width), :].astype(BF16)
            lb, lk = _sb_scores(qm, kb)
            lks = [lk[:, sl] for sl in sls]
            first_valid = (off + lane) >= PAD
            lks[0] = jnp.where(first_valid, lks[0], 0.0)
            if last_valid is not None:
                lks[-1] = jnp.where(last_valid, lks[-1], 0.0)
            afters = [_dot_tri(lks[b], tri) for b in range(nsub)]
            run, acc = carry
            ws = [None] * nsub
            for b in reversed(range(nsub)):
                wb = jnp.exp(lb[:, sls[b]] + afters[b][0] + run)
                if b == 0:
                    wb = jnp.where(first_valid, wb, 0.0)
                if last_valid is not None and b == nsub - 1:
                    wb = jnp.where(last_valid, wb, 0.0)
                ws[b] = wb.astype(BF16)
                run = run + afters[b][1]
            w = ws[0] if nsub == 1 else jnp.concatenate(ws, axis=1)
            return run, acc + _dot(w, vb)

        edge, edge_off = _sb_groups(i)
        diag = lane < qrow
        zero = jnp.zeros((2 * BLK, BLK), F32)
        merged, upto, which = _sb_edge(i, edge, edge_off, lambda off, n, cr: chunk(off, n, diag, cr))
        carry = lax.switch(which, upto, (zero, zero))

        def interior(state):
            off = pl.multiple_of(state[0] * (SB_GROUP * BLK), BLK)
            return (state[0] - 1, *chunk(off, SB_GROUP, None, state[1:]))

        def live(state):
            return jnp.logical_and(state[0] >= 0, jnp.max(state[1]) > SB_DEAD)

        below, run, acc = lax.while_loop(live, interior, (edge - 1 - merged, *carry))
        low = lane[:BLK] < HEAD_DIM
        o_ref[...] = jnp.where(low, acc[:BLK], acc[BLK:])
        tl = jnp.where(low, run[:BLK], run[BLK:])
        tl_ref[...] = jnp.where(lane[:BLK] == 1, (below + 1).astype(F32), tl)

    qc, kc, vc = C_Q // BLK, C_K // BLK, C_V // BLK
    blk = pl.BlockSpec((BLK, BLK), lambda p, i: (i, p))
    res = pl.pallas_call(
        body, name="sb_fwd", grid=(npair, nb),
        in_specs=[pl.BlockSpec((BLK, BLK), lambda p, i: (i, qc + p)),
                  pl.BlockSpec((lp, BLK), lambda p, i: (0, kc + p)),
                  pl.BlockSpec((lp, BLK), lambda p, i: (0, vc + p))] + [ANY] * ns,
        out_specs=[blk, blk] + [ANY] * nl,
        out_shape=[jax.ShapeDtypeStruct((lp, N_HEADS * HEAD_DIM), F32)] * 2 + list(ride[1]),
        scratch_shapes=_ride_scratch(ride),
        compiler_params=_cparams(("arbitrary", "arbitrary")))(proj, proj, proj, *ride[0])
    return res[0], res[1], list(res[2:])


def _sb_bwd_call(proj, tl, do, ride):
    lp = proj.shape[0]
    nb = lp // BLK
    assert (nb - 1) % SB_GROUP == 0
    scale = 1.0 / math.sqrt(HEAD_DIM)

    ns, nl = len(ride[0]), len(ride[1])
    npair = N_HEADS // 2

    def body(q_ref, k_ref, v_ref, tl_ref, do_ref, *rest):
        dq_ref, dk_ref, dv_ref = rest[ns:ns + 3]
        dk_acc, dv_acc = rest[ns + 3 + nl:ns + 5 + nl]
        i = pl.program_id(1)
        step = pl.program_id(0) * nb + i
        _ride_run(ride, rest[:ns], rest[ns + 3:ns + 3 + nl], rest[-2], rest[-1], step == 0, step == npair * nb - 1)

        @pl.when(i == 0)
        def _():
            dk_acc[...] = jnp.zeros_like(dk_acc)
            dv_acc[...] = jnp.zeros_like(dv_acc)

        lane = _iota((2 * BLK, BLK), 1)
        row = _iota((2 * BLK, BLK), 0)
        qrow = row & (BLK - 1)
        mine = (row < BLK) == (lane < HEAD_DIM)
        q = q_ref[...] * scale
        dov = do_ref[...]
        qm = jnp.where(mine, jnp.concatenate([q, q], axis=0), 0.0).astype(BF16)
        dom = jnp.where(mine, jnp.concatenate([dov, dov], axis=0), 0.0).astype(BF16)
        tlv = tl_ref[...]
        tot = jnp.concatenate([jnp.broadcast_to(tlv[:, 0:1], (BLK, BLK)),
                               jnp.broadcast_to(tlv[:, HEAD_DIM:HEAD_DIM + 1], (BLK, BLK))], axis=0)
        r1, l1 = _iota((BLK, BLK), 0), _iota((BLK, BLK), 1)
        tri_in = _tri2(r1 <= l1)
        tri_ex = _tri2(r1 < l1)

        def chunk(off, nsub, last_valid, carry):
            width = nsub * BLK
            sls = [slice(b * BLK, (b + 1) * BLK) for b in range(nsub)]
            cat = lambda parts: parts[0] if nsub == 1 else jnp.concatenate(parts, axis=1)
            mask_last = lambda b: last_valid is not None and b == nsub - 1
            kb = k_ref[pl.ds(off, width), :].astype(BF16)
            vb = v_ref[pl.ds(off, width), :].astype(BF16)
            lb, lk = _sb_scores(qm, kb)
            dw = _dot(dom, vb, 1, 1)
            lks = [lk[:, sl] for sl in sls]
            first_valid = (off + lane) >= PAD
            lks[0] = jnp.where(first_valid, lks[0], 0.0)
            if last_valid is not None:
                lks[-1] = jnp.where(last_valid, lks[-1], 0.0)
            pins = [_dot_tri(lks[b], tri_in) for b in range(nsub)]
            run, gsum, dq = carry
            ws, gs = [], []
            for b in range(nsub):
                wb = jnp.exp(lb[:, sls[b]] + (tot - run - pins[b][0]))
                if b == 0:
                    wb = jnp.where(first_valid, wb, 0.0)
                if mask_last(b):
                    wb = jnp.where(last_valid, wb, 0.0)
                ws.append(wb.astype(BF16))
                gs.append(wb * dw[:, sls[b]])
                run = run + pins[b][1]
            gexs = [_dot_tri1(gs[b], tri_ex) for b in range(nsub)]
            beta = jnp.exp(lb)
            parts = []
            for b in range(nsub):
                bt = beta[:, sls[b]]
                dzb = gs[b] * (1.0 - bt) - (gsum + gexs[b][0]) * bt
                if b == 0:
                    dzb = jnp.where(first_valid, dzb, 0.0)
                if mask_last(b):
                    dzb = jnp.where(last_valid, dzb, 0.0)
                parts.append(dzb.astype(BF16))
                gsum = gsum + gexs[b][1]
            dz, w = cat(parts), cat(ws)
            dk_acc[pl.ds(off, width), :] += _dot(dz, qm, 0, 0)
            dv_acc[pl.ds(off, width), :] += _dot(w, dom, 0, 0)
            return run, gsum, dq + _dot(dz, kb)

        edge, edge_off = _sb_groups(i)
        diag = lane < qrow
        zero = jnp.zeros((2 * BLK, BLK), F32)
        first = jnp.max(tlv[:, 1:2]).astype(jnp.int32)

        def interior(g, carry):
            return chunk(pl.multiple_of(g * (SB_GROUP * BLK), BLK), SB_GROUP, None, carry)

        merged, upto, which = _sb_edge(i, edge, edge_off, lambda off, n, cr: chunk(off, n, diag, cr))
        carry = lax.fori_loop(first, edge - merged, interior, (zero, zero, zero))
        dq = lax.switch(which, upto, carry)[2]
        dq_ref[...] = (jnp.where(lane[:BLK] < HEAD_DIM, dq[:BLK], dq[BLK:]) * scale).astype(BF16)

        @pl.when(i == nb - 1)
        def _():
            dk_ref[...] = dk_acc[...].astype(BF16)
            dv_ref[...] = dv_acc[...].astype(BF16)

    qc, kc, vc = C_Q // BLK, C_K // BLK, C_V // BLK
    blk = pl.BlockSpec((BLK, BLK), lambda p, i: (i, p))
    full = pl.BlockSpec((lp, BLK), lambda p, i: (0, p))
    w = N_HEADS * HEAD_DIM
    res = pl.pallas_call(
        body, name="sb_bwd", grid=(npair, nb),
        in_specs=[pl.BlockSpec((BLK, BLK), lambda p, i: (i, qc + p)),
                  pl.BlockSpec((lp, BLK), lambda p, i: (0, kc + p)),
                  pl.BlockSpec((lp, BLK), lambda p, i: (0, vc + p)),
                  blk, blk] + [ANY] * ns,
        out_specs=[blk, full, full] + [ANY] * nl,
        out_shape=[jax.ShapeDtypeStruct((lp, w), BF16)] * 3 + list(ride[1]),
        scratch_shapes=[pltpu.VMEM((lp, BLK), F32), pltpu.VMEM((lp, BLK), F32)] + _ride_scratch(ride),
        compiler_params=_cparams(("arbitrary", "arbitrary")))(proj, proj, proj, tl, do, *ride[0])
    return res[0], res[1], res[2], list(res[3:])


def _log1p(e):
    u = 1.0 + e
    return jnp.where(u == 1.0, e, jnp.log(u) * e / jnp.where(u == 1.0, 1.0, u - 1.0))


def _ssd_common(c, dtr, bias, alog):
    row = _iota((BLK, BLK), 0)
    lane = _iota((BLK, BLK), 1)
    live = ((c * BLK + row) >= PAD) & (lane < N_HEADS)
    pre = dtr + bias
    dt = jnp.where(live, jnp.maximum(pre, 0.0) + _log1p(jnp.exp(-jnp.abs(pre))), 0.0)
    a_neg = -jnp.exp(alog)
    a = dt * a_neg
    t_in = (lane <= row).astype(BF16)
    cs = _dot_sel_l(t_in, a)
    cs_t = cs.T
    cs_end = cs[BLK - 1:BLK, :]
    e = jnp.exp(cs)
    f = jnp.exp(cs_end - cs)
    xp = ((_iota((BLK, SSD_INNER), 1) // HEAD_DIM) == _iota((BLK, SSD_INNER), 0)).astype(BF16)
    xp_t = ((_iota((SSD_INNER, BLK), 0) // HEAD_DIM) == _iota((SSD_INNER, BLK), 1)).astype(BF16)
    decay_col = _dot_sel_l(xp_t, jnp.exp(cs_t))[:, BLK - 1:BLK]
    return dict(live=live, pre=pre, dt=dt, a_neg=a_neg, cs=cs, cs_t=cs_t, e=e, f=f, xp=xp, xp_t=xp_t,
                decay_col=decay_col, row=row, lane=lane,
                dt_x=_dot_sel_r(dt, xp), e_x=_dot_sel_r(e, xp), f_x=_dot_sel_r(f, xp))


def _ssd_ldec(q, h):
    diff = q["cs"][:, h:h + 1] - q["cs_t"][h:h + 1, :]
    causal = q["row"] >= q["lane"]
    return jnp.where(causal, jnp.exp(jnp.where(causal, diff, 0.0)), 0.0)


def _ssd_fwd_call(xbc, proj, bias, alog, d_x, norm_g):
    lp = xbc.shape[0]
    nb = lp // BLK
    gw = SSD_INNER // SSD_GROUPS
    ppg = gw // BLK

    def body(xbc_ref, dtr_ref, z_ref, bias_ref, alog_ref, dx_ref, ng_ref, yb_ref, ypre_ref, sprev_ref, s_ref):
        c = pl.program_id(0)

        @pl.when(c == 0)
        def _():
            s_ref[...] = jnp.zeros_like(s_ref)

        q = _ssd_common(c, dtr_ref[...], bias_ref[...], alog_ref[...])
        x = xbc_ref[:, 0:SSD_INNER]
        xd = x * q["dt_x"]
        low = q["lane"] < HEAD_DIM
        s_old = s_ref[...]
        sprev_ref[...] = s_old
        xdf = (xd * q["f_x"]).astype(BF16)
        for g in range(SSD_GROUPS):
            bg = xbc_ref[:, SSD_INNER + g * SSD_STATE:SSD_INNER + (g + 1) * SSD_STATE].astype(BF16)
            cg = xbc_ref[:, SSD_INNER + (SSD_GROUPS + g) * SSD_STATE:
                         SSD_INNER + (SSD_GROUPS + g + 1) * SSD_STATE].astype(BF16)
            cb = _dot(cg, bg, 1, 1)
            gs = slice(g * gw, (g + 1) * gw)
            y_off = _dot(cg, s_old[gs, :].astype(BF16), 1, 1) * q["e_x"][:, gs]
            s_ref[gs, :] = s_old[gs, :] * q["decay_col"][gs, :] + _dot(xdf[:, gs], bg, 0, 0)
            for pr in range(ppg):
                cols = slice(g * gw + pr * BLK, g * gw + (pr + 1) * BLK)
                xd_p = xd[:, cols]
                acc = y_off[:, pr * BLK:(pr + 1) * BLK]
                for hh in range(2):
                    h = (g * gw + pr * BLK) // HEAD_DIM + hh
                    m = (cb * _ssd_ldec(q, h)).astype(BF16)
                    xm = jnp.where(low, xd_p, 0.0) if hh == 0 else jnp.where(low, 0.0, xd_p)
                    acc = acc + _dot(m, xm.astype(BF16))
                ypre_ref[:, cols] = acc
        ypre = ypre_ref[...] + x * dx_ref[...]
        ypre_ref[...] = ypre
        z = z_ref[...]
        yg = ypre * (z * _sigmoid(z))
        _, yh = _rms_stats(yg)
        yb_ref[...] = (yh * ng_ref[...]).astype(BF16)

    row = lambda w, col: pl.BlockSpec((BLK, w), lambda c: (c, col))
    vec = lambda w: pl.BlockSpec((1, w), lambda c: (0, 0))
    return pl.pallas_call(
        body, name="ssd_fwd", grid=(nb,),
        in_specs=[row(XBC, 0), row(BLK, C_DT // BLK), row(SSD_INNER, 0), vec(BLK), vec(BLK),
                  vec(SSD_INNER), vec(SSD_INNER)],
        out_specs=[row(SSD_INNER, 0), row(SSD_INNER, 0),
                   pl.BlockSpec((None, SSD_INNER, SSD_STATE), lambda c: (c, 0, 0))],
        out_shape=[jax.ShapeDtypeStruct((lp, 2 * SSD_INNER), BF16), jax.ShapeDtypeStruct((lp, SSD_INNER), F32),
                   jax.ShapeDtypeStruct((nb, SSD_INNER, SSD_STATE), F32)],
        scratch_shapes=[pltpu.VMEM((SSD_INNER, SSD_STATE), F32)],
        compiler_params=_cparams(("arbitrary",)))(xbc, proj, proj, bias, alog, d_x, norm_g)


def _ssd_bwd_call(dycat, ypre, xbc, proj, sprev, bias, alog, d_x, norm_g):
    lp = xbc.shape[0]
    nb = lp // BLK
    gw = SSD_INNER // SSD_GROUPS
    ppg = gw // BLK

    def body(dy_ref, ypre_ref, xbc_ref, dtr_ref, z_ref, sp_ref, bias_ref, alog_ref, dxp_ref, ng_ref,
             dz_ref, dxbc_ref, ddt_ref, dng_ref, dd_ref, dal_ref, dbi_ref, ds_ref, dxd_ref):
        step = pl.program_id(0)
        c = nb - 1 - step

        @pl.when(step == 0)
        def _():
            ds_ref[...] = jnp.zeros_like(ds_ref)

        q = _ssd_common(c, dtr_ref[...], bias_ref[...], alog_ref[...])
        row, lane = q["row"], q["lane"]
        low = lane < HEAD_DIM
        rowlive = ((c * BLK + _iota((BLK, 1), 0)) >= PAD)
        x = xbc_ref[:, 0:SSD_INNER]
        xd = x * q["dt_x"]
        z = z_ref[...]
        sz = _sigmoid(z)
        silu = z * sz
        ypre = ypre_ref[...]
        dyg, dng = _rms_bwd(ypre * silu, ng_ref[...], dy_ref[...])
        _acc_rows(dng_ref, dng, step)
        dyp = dyg * silu
        dz_ref[...] = jnp.where(rowlive, dyg * ypre * (sz * (1.0 + z * (1.0 - sz))), 0.0).astype(BF16)
        _acc_rows(dd_ref, jnp.sum(dyp * x, axis=0, keepdims=True), step)
        dye = dyp * q["e_x"]
        xdf = xd * q["f_x"]
        s_prev = sp_ref[...]
        ds_old = ds_ref[...]
        qrow = jnp.zeros((BLK, BLK), F32)
        qcol_t = jnp.zeros((BLK, BLK), F32)
        red_e = []
        red_f = []
        for g in range(SSD_GROUPS):
            gs = slice(g * gw, (g + 1) * gw)
            bsl = slice(SSD_INNER + g * SSD_STATE, SSD_INNER + (g + 1) * SSD_STATE)
            csl = slice(SSD_INNER + (SSD_GROUPS + g) * SSD_STATE, SSD_INNER + (SSD_GROUPS + g + 1) * SSD_STATE)
            bg = xbc_ref[:, bsl].astype(BF16)
            cg = xbc_ref[:, csl].astype(BF16)
            sg = s_prev[gs, :].astype(BF16)
            dsg = ds_old[gs, :].astype(BF16)
            cb = _dot(cg, bg, 1, 1)
            bds = _dot(bg, dsg, 1, 1)
            y_off = _dot(cg, sg, 1, 1) * q["e_x"][:, gs]
            red_e.append(dyp[:, gs] * y_off)
            red_f.append(xd[:, gs] * bds * q["f_x"][:, gs])
            dc = _dot(dye[:, gs].astype(BF16), sg)
            db = _dot(xdf[:, gs].astype(BF16), dsg)
            ds_ref[gs, :] = ds_old[gs, :] * q["decay_col"][gs, :] + _dot(dye[:, gs].astype(BF16), cg, 0, 0)
            dcb = jnp.zeros((BLK, BLK), F32)
            for pr in range(ppg):
                cols = slice(g * gw + pr * BLK, g * gw + (pr + 1) * BLK)
                xd_p = xd[:, cols].astype(BF16)
                dy_p = dyp[:, cols]
                acc = q["f_x"][:, cols] * bds[:, pr * BLK:(pr + 1) * BLK]
                for hh in range(2):
                    h = (g * gw + pr * BLK) // HEAD_DIM + hh
                    ld = _ssd_ldec(q, h)
                    m = cb * ld
                    dym = (jnp.where(low, dy_p, 0.0) if hh == 0 else jnp.where(low, 0.0, dy_p)).astype(BF16)
                    dm = jnp.where(row >= lane, _dot(dym, xd_p, 1, 1), 0.0)
                    acc = acc + _dot(m.astype(BF16), dym, 0, 0)
                    dcb = dcb + dm * ld
                    qq = dm * m
                    qrow = qrow + jnp.where(lane == h, jnp.sum(qq, axis=1, keepdims=True), 0.0)
                    qcol_t = qcol_t + jnp.where(row == h, jnp.sum(qq, axis=0, keepdims=True), 0.0)
                dxd_ref[:, cols] = acc
            dcbb = dcb.astype(BF16)
            dxbc_ref[:, bsl] = jnp.where(rowlive, db + _dot(dcbb, cg, 0, 0), 0.0)
            dxbc_ref[:, csl] = jnp.where(rowlive, dc + _dot(dcbb, bg), 0.0)
        dxd = dxd_ref[...]
        dxbc_ref[:, 0:SSD_INNER] = jnp.where(rowlive, dxd * q["dt_x"] + dyp * dxp_ref[...], 0.0)
        xp_t = q["xp_t"]
        fw = _dot_sel_r(jnp.concatenate(red_f, axis=1), xp_t)
        dcs = qrow - qcol_t.T + _dot_sel_r(jnp.concatenate(red_e, axis=1), xp_t) - fw
        end_f = jnp.sum(fw, axis=0, keepdims=True)
        sds = jnp.sum(ds_old * s_prev, axis=1, keepdims=True)
        per_head = _dot_sel_l(q["xp"], jnp.broadcast_to(sds, (SSD_INNER, BLK)))
        end_e = per_head.T[0:1, :] * jnp.exp(q["cs"][BLK - 1:BLK, :])
        dcs = dcs + jnp.where(row == BLK - 1, end_f + end_e, 0.0)
        t_up = (lane >= row).astype(BF16)
        da = _dot_sel_l(t_up, dcs)
        ddt = da * q["a_neg"] + _dot_sel_r(dxd * x, xp_t)
        _acc_rows(dal_ref, jnp.sum(da * q["dt"] * q["a_neg"], axis=0, keepdims=True), step)
        ddtr = jnp.where(q["live"], ddt * _sigmoid(q["pre"]), 0.0)
        ddt_ref[...] = ddtr.astype(BF16)
        _acc_rows(dbi_ref, jnp.sum(ddtr, axis=0, keepdims=True), step)

    row_s = lambda w, col: pl.BlockSpec((BLK, w), lambda s: (nb - 1 - s, col))
    vec = lambda w: pl.BlockSpec((1, w), lambda s: (0, 0))
    return pl.pallas_call(
        body, name="ssd_bwd", grid=(nb,),
        in_specs=[row_s(SSD_INNER, 0), row_s(SSD_INNER, 0), row_s(XBC, 0), row_s(BLK, C_DT // BLK),
                  row_s(SSD_INNER, 0), pl.BlockSpec((None, SSD_INNER, SSD_STATE), lambda s: (nb - 1 - s, 0, 0)),
                  vec(BLK), vec(BLK), vec(SSD_INNER), vec(SSD_INNER)],
        out_specs=[row_s(SSD_INNER, 0), row_s(XBC, 0), row_s(BLK, 0),
                   vec(SSD_INNER), vec(SSD_INNER), vec(BLK), vec(BLK)],
        out_shape=[jax.ShapeDtypeStruct((lp, SSD_INNER), BF16), jax.ShapeDtypeStruct((lp, XBC), F32),
                   jax.ShapeDtypeStruct((lp, BLK), BF16),
                   jax.ShapeDtypeStruct((1, SSD_INNER), F32), jax.ShapeDtypeStruct((1, SSD_INNER), F32),
                   jax.ShapeDtypeStruct((1, BLK), F32), jax.ShapeDtypeStruct((1, BLK), F32)],
        scratch_shapes=[pltpu.VMEM((SSD_INNER, SSD_STATE), F32), pltpu.VMEM((BLK, SSD_INNER), F32)],
        compiler_params=_cparams(("arbitrary",)))(dycat, ypre, xbc, proj, proj, sprev, bias, alog, d_x, norm_g)


def _pad_rows8(w):
    return jnp.pad(w, ((0, 8 - w.shape[0]), (0, 0)))


def _pad_lanes(v, n=BLK):
    return jnp.pad(v, ((0, 0), (0, n - v.shape[1])))


def _local_step(x, target, wt, late_shards, late_weights, w_in_shards):
    seq = x.shape[0]
    lp = seq + BLK
    tm = _pick(lp, [1408, 768, 384, 128])
    tkr = _pick(lp, [1408, 384, 128])
    rbc = _pick(lp, [384, 128])
    h0 = jnp.concatenate([jnp.zeros((PAD, D_MODEL), F32), wt["meta"], x], axis=0)
    bias = _pad_lanes(wt["ssd_dt_bias"])
    alog = _pad_lanes(wt["ssd_a_log"])
    d_x = jnp.repeat(wt["ssd_d"], HEAD_DIM, axis=1)
    cw8 = _pad_rows8(wt["ssd_conv_w"])
    fw8 = _pad_rows8(wt["ffn_conv_w"])
    fcw = D_FF // 2

    xn1 = _rms_fwd_call(h0, wt["mix_pre_g"], "norm1")
    proj, late_a = _mm(xn1, wt["w_in"], tm=tm, tn=1152, tk=D_MODEL, ride=_gather_ride(late_shards[0:1]),
                       name="mm_proj")
    proj = proj[0]
    conv_pre, xbc = _conv_fwd_call(proj, C_XBC, XBC, 512, cw8, wt["ssd_conv_b"], 4, rb=rbc, name="ssd_conv_fwd")
    y_ssd, ypre, sprev = _ssd_fwd_call(xbc, proj, bias, alog, d_x, wt["ssd_norm_g"])
    o, tl, late_b = _sb_fwd_call(proj, _gather_ride(late_shards[1:3]))
    ycat = _rms_fwd_call(o, wt["sb_norm_g"], "sb_norm", beside=y_ssd)
    w_out, w_up, w_down = late_weights([late_a[0], late_b[0], late_b[1]])
    mix = _mm(ycat, w_out, tm=tm, tn=1024, tk=2048, name="mm_mix")[0]
    h1, xn2 = _mid_fwd_call(h0, mix, wt["mix_post_g"], wt["ffn_pre_g"])
    gu = _mm(xn2, w_up, tm=tm, tn=1408, tk=D_MODEL, name="mm_up")[0]
    gpre, act = _conv_fwd_call(gu, 0, D_FF, fcw, fw8, wt["ffn_conv_b"], 3, gate_src=gu, gate_col0=D_FF,
                               name="ffn_conv_fwd")
    f = _mm(act, w_down, tm=tm, tn=1024, tk=1408, name="mm_down")[0]
    loss_row, df, dh2, dg_ffn_post = _final_call(h1, f, wt["ffn_post_g"], target)

    dact = _mm(df, w_down, tb=True, tm=tm, tn=1408, tk=D_MODEL, name="mm_dact")[0]
    dw_down, dw_down_b = _mm(act, df, ta=True, tm=1408, tn=1024, tk=tkr, extra_bf16=True, name="mm_dw_down")
    by_chip = lambda g: g.reshape(N_CHIPS, -1, D_MODEL)
    dgate, dup, dfcw, dfcb = _conv_bwd_call(gu, 0, D_FF, fcw, fw8, 3, gpre, dact, gate_src=gu, gate_col0=D_FF,
                                            name="ffn_conv_bwd")
    dgu = jnp.concatenate([dgate, dup], axis=1)
    dxn2, land_down = _mm(dgu, w_up, tb=True, tm=tm, tn=1024, tk=1408, ride=_scatter_ride(by_chip(dw_down_b)),
                          name="mm_dxn2")
    dw_up, dw_up_b = _mm(xn2, dgu, ta=True, tm=1024, tn=1408, tk=tkr, nsplit=N_CHIPS, extra_bf16=True,
                         name="mm_dw_up")
    dh1, dmix, dg_ffn_pre, dg_mix_post = _mid_bwd_call(dh2, h1, dxn2[0], mix, wt["ffn_pre_g"], wt["mix_post_g"])
    dycat = _mm(dmix, w_out, tb=True, tm=tm, tn=1024, tk=D_MODEL, name="mm_dycat")[0]
    dw_out, dw_out_b = _mm(ycat, dmix, ta=True, tm=1024, tn=1024, tk=tkr, extra_bf16=True, name="mm_dw_out")
    do, dg_sb = _norm_bwd_call(o, wt["sb_norm_g"], dycat, 1, "sb_norm_bwd")
    dq, dk, dv, lands = _sb_bwd_call(proj, tl, do, _join_rides(_scatter_ride(dw_up_b),
                                                                  _scatter_ride(by_chip(dw_out_b))))
    dz, dxbc_act, ddt, dg_ssd, dd_x, dalog, dbias = _ssd_bwd_call(
        dycat, ypre, xbc, proj, sprev, bias, alog, d_x, wt["ssd_norm_g"])
    dxbc, dcw, dcb = _conv_bwd_call(proj, C_XBC, XBC, 512, cw8, 4, conv_pre, dxbc_act, rb=rbc,
                                    name="ssd_conv_bwd")
    dproj = jnp.concatenate([dz, dxbc, ddt, dq, dk, dv], axis=1)
    dw_in, dw_in_b = w_in_shards(_mm(xn1, dproj, ta=True, tm=1024, tn=1152, tk=tkr, name="mm_dw_in")[0])
    dxn1, land_in = _mm(dproj, wt["w_in"], tb=True, tm=tm, tn=1024, tk=1152, ride=_scatter_ride(dw_in_b),
                        name="mm_dxn1")
    dh0, dg_pre = _norm_bwd_call(h0, wt["mix_pre_g"], dxn1[0], 0, "norm1_bwd", res=dh1)

    small = {
        "meta_tokens": dh0[PAD:BLK], "mix_pre_g": dg_pre, "ssd_conv_w": dcw[:4], "ssd_conv_b": dcb,
        "ssd_dt_bias": dbias[:, :N_HEADS], "ssd_a_log": dalog[:, :N_HEADS],
        "ssd_d": jnp.sum(dd_x.reshape(N_HEADS, HEAD_DIM), axis=1)[None],
        "ssd_norm_g": dg_ssd, "sb_norm_g": dg_sb, "mix_post_g": dg_mix_post, "ffn_pre_g": dg_ffn_pre,
        "ffn_conv_w": dfcw[:3], "ffn_conv_b": dfcb, "ffn_post_g": dg_ffn_post,
    }
    pending = {"w_in": (dw_in, land_in[0]), "w_out": (by_chip(dw_out), lands[1]), "w_up": (dw_up, lands[0]),
               "w_down": (by_chip(dw_down), land_down[0])}
    return loss_row, dh0[BLK:], small, pending


def _adamw_call(w, g, m, v, name):
    rows, cols = w.shape
    tr = 256 if rows % 256 == 0 else (352 if rows % 352 == 0 else rows)
    c1 = 1.0 - ADAM_B1 ** ADAM_STEP
    c2 = 1.0 - ADAM_B2 ** ADAM_STEP

    def body(w_ref, g_ref, m_ref, v_ref, d_ref, mo_ref, vo_ref):
        gv = g_ref[...]
        m2 = ADAM_B1 * m_ref[...] + (1.0 - ADAM_B1) * gv
        v2 = ADAM_B2 * v_ref[...] + (1.0 - ADAM_B2) * (gv * gv)
        d_ref[...] = -ADAM_LR * ((m2 / c1) / (jnp.sqrt(v2 / c2) + ADAM_EPS) + ADAM_WD * w_ref[...])
        mo_ref[...] = m2
        vo_ref[...] = v2

    spec = pl.BlockSpec((tr, cols), lambda i: (i, 0))
    return pl.pallas_call(
        body, name=name, grid=(rows // tr,), in_specs=[spec] * 4, out_specs=[spec] * 3,
        out_shape=[jax.ShapeDtypeStruct((rows, cols), F32)] * 3,
        compiler_params=_cparams(("parallel",)))(w, g, m, v)


ANY = pl.BlockSpec(memory_space=pl.ANY)


def _place():
    x, y, c = lax.axis_index("x"), lax.axis_index("y"), lax.axis_index("c")
    chips = [(1 - x, y), (x, 1 - y), (1 - x, 1 - y)]
    return x, y, c, chips


def _half(c, h):
    return pl.ds(pl.multiple_of(c * h, 8), h)


def _allgather_call(shards):
    n = len(shards)

    def body(*refs):
        ins, outs = refs[:n], refs[n:2 * n]
        send_i, recv_i, send_d, recv_d = refs[2 * n:]
        x, y, c, chips = _place()
        me = 2 * x + y
        sends = []
        for a in range(n):
            h = shards[a].shape[0] // 2
            for j, chip in enumerate(chips):
                cp = pltpu.make_async_remote_copy(
                    src_ref=ins[a].at[_half(c, h)], dst_ref=outs[a].at[me, _half(c, h)],
                    send_sem=send_i.at[3 * a + j], recv_sem=recv_i.at[3 * a + j],
                    device_id=(*chip, c), device_id_type=MESH)
                cp.start()
                sends.append(cp)
        for a in range(n):
            h = shards[a].shape[0] // 2
            for j, chip in enumerate(chips):
                src = 2 * chip[0] + chip[1]
                landed = outs[a].at[src, _half(c, h)]
                pltpu.make_async_remote_copy(
                    src_ref=landed, dst_ref=landed, send_sem=send_i.at[3 * a + j], recv_sem=recv_i.at[3 * a + j],
                    device_id=(*chip, c), device_id_type=MESH).wait_recv()
                cp = pltpu.make_async_remote_copy(
                    src_ref=landed, dst_ref=landed, send_sem=send_d.at[3 * a + j], recv_sem=recv_d.at[3 * a + j],
                    device_id=(x, y, 1 - c), device_id_type=MESH)
                cp.start()
                sends.append(cp)
        for a in range(n):
            h = shards[a].shape[0] // 2
            for j, chip in enumerate(chips):
                src = 2 * chip[0] + chip[1]
                other = outs[a].at[src, _half(1 - c, h)]
                pltpu.make_async_remote_copy(
                    src_ref=other, dst_ref=other, send_sem=send_d.at[3 * a + j], recv_sem=recv_d.at[3 * a + j],
                    device_id=(x, y, 1 - c), device_id_type=MESH).wait_recv()
        for cp in sends:
            cp.wait_send()

    return pl.pallas_call(
        body, name="allgather_weights", in_specs=[ANY] * n, out_specs=[ANY] * n,
        out_shape=[jax.ShapeDtypeStruct((N_CHIPS,) + s.shape, s.dtype) for s in shards],
        scratch_shapes=[pltpu.SemaphoreType.DMA((3 * n,))] * 4,
    )(*shards)


def _ride_scratch(ride):
    return [pltpu.SemaphoreType.DMA((ride[3],)), pltpu.SemaphoreType.DMA((ride[3],))]


def _ride_run(ride, src_refs, land_refs, send, recv, first, last):
    plan = ride[2]

    @pl.when(first)
    def _():
        for k, (src, dst, _, dev) in enumerate(plan(src_refs, land_refs)):
            pltpu.make_async_remote_copy(src_ref=src, dst_ref=dst, send_sem=send.at[k], recv_sem=recv.at[k],
                                         device_id=dev, device_id_type=MESH).start()

    @pl.when(last)
    def _():
        for k, (src, _, land, dev) in enumerate(plan(src_refs, land_refs)):
            cp = pltpu.make_async_remote_copy(src_ref=src, dst_ref=land, send_sem=send.at[k], recv_sem=recv.at[k],
                                              device_id=dev, device_id_type=MESH)
            cp.wait_send()
            cp.wait_recv()


def _join_rides(r1, r2):
    n1, l1 = len(r1[0]), len(r1[1])

    def plan(srcs, lands):
        return r1[2](srcs[:n1], lands[:l1]) + r2[2](srcs[n1:], lands[l1:])

    return (r1[0] + r2[0], r1[1] + r2[1], plan, r1[3] + r2[3])


def _gather_ride(shards):
    return (list(shards), [jax.ShapeDtypeStruct((N_CHIPS,) + s.shape, s.dtype) for s in shards],
            _gather_plan(len(shards)), 3 * len(shards))


def _scatter_ride(g_b):
    h = g_b.shape[1] // 2
    return ([g_b], [jax.ShapeDtypeStruct((8, h, g_b.shape[2]), BF16)], _scatter_plan(h), 7)


def _gather_plan(n):
    def plan(srcs, lands):
        x, y, c, chips = _place()
        me = 2 * x + y
        return [(srcs[a], lands[a].at[me], lands[a].at[2 * chip[0] + chip[1]], (*chip, c))
                for a in range(n) for chip in chips]
    return plan


def _scatter_plan(h):
    def plan(srcs, lands):
        x, y, c, _ = _place()
        me = 4 * x + 2 * y + c
        out = []
        for p in range(1, 8):
            px = 1 - x if p & 4 else x
            py = 1 - y if p & 2 else y
            pc = 1 - c if p & 1 else c
            out.append((srcs[0].at[2 * px + py, _half(pc, h)], lands[0].at[me],
                        lands[0].at[4 * px + 2 * py + pc], (px, py, pc)))
        return out
    return plan


def _grad_sum_call(own, land, place, name):
    _, h, cols = land.shape
    th = _pick(h, [256, 176, 8])
    nt = h // th

    def body(p_ref, own_ref, *refs):
        acc = own_ref[...]
        for r in refs[:7]:
            acc = acc + r[...].astype(F32)
        refs[7][...] = acc

    def peer(k):
        return pl.BlockSpec((None, th, cols), lambda i, p_ref: (p_ref[2 + k], i, 0))

    return pl.pallas_call(
        body, name=name,
        grid_spec=pltpu.PrefetchScalarGridSpec(
            num_scalar_prefetch=1, grid=(nt,),
            in_specs=[pl.BlockSpec((None, th, cols), lambda i, p_ref: (p_ref[1], p_ref[0] * nt + i, 0))]
            + [peer(k) for k in range(7)],
            out_specs=pl.BlockSpec((th, cols), lambda i, p_ref: (p_ref[0] * nt + i, 0))),
        out_shape=jax.ShapeDtypeStruct((2 * h, cols), F32),
        compiler_params=_cparams(("parallel",)))(place, own, *[land] * 7)


def _half_exchange_call(shards):
    n = len(shards)

    def body(*refs):
        outs = refs[n:2 * n]
        send_d, recv_d = refs[2 * n:]
        x, y, c, _ = _place()
        cps = []
        for a in range(n):
            h = shards[a].shape[0] // 2
            mine = outs[a].at[_half(c, h)]
            cp = pltpu.make_async_remote_copy(
                src_ref=mine, dst_ref=mine, send_sem=send_d.at[a], recv_sem=recv_d.at[a],
                device_id=(x, y, 1 - c), device_id_type=MESH)
            cp.start()
            cps.append(cp)
        for a, cp in enumerate(cps):
            h = shards[a].shape[0] // 2
            theirs = outs[a].at[_half(1 - c, h)]
            pltpu.make_async_remote_copy(
                src_ref=theirs, dst_ref=theirs, send_sem=send_d.at[a], recv_sem=recv_d.at[a],
                device_id=(x, y, 1 - c), device_id_type=MESH).wait_recv()
            cp.wait_send()

    return pl.pallas_call(
        body, name="grad_half_exchange", in_specs=[ANY] * n, out_specs=[ANY] * n,
        out_shape=[jax.ShapeDtypeStruct(sv.shape, F32) for sv in shards],
        input_output_aliases={a: a for a in range(n)},
        scratch_shapes=[pltpu.SemaphoreType.DMA((n,))] * 2,
    )(*shards)


def _allreduce_small_call(arrs):
    n = len(arrs)
    offs, rows = [], 0
    for a in arrs:
        offs.append(rows)
        rows += a.shape[0]
    rows = -(-rows // 8) * 8
    width = -(-max(a.shape[1] for a in arrs) // BLK) * BLK

    def body(*refs):
        ins, outs = refs[:n], refs[n:2 * n]
        gath, send_sems, recv_sems = refs[2 * n:]
        x, y, c, chips = _place()
        me, sibling = (x, y, c), (x, y, 1 - c)

        def slot(px, py, pc):
            return gath.at[4 * px + 2 * py + pc]

        def copy(k, block, to):
            return pltpu.make_async_remote_copy(
                src_ref=slot(*block), dst_ref=slot(*block),
                send_sem=send_sems.at[k], recv_sem=recv_sems.at[k], device_id=to, device_id_type=MESH)

        mine = slot(*me)
        mine[...] = jnp.zeros((rows, width), F32)
        for k in range(n):
            r, w = arrs[k].shape
            mine[offs[k]:offs[k] + r, 0:w] = ins[k][...]
        first = [copy(0, me, sibling)]
        first += [copy(1 + j, me, (*chip, c)) for j, chip in enumerate(chips)]
        for cp in first:
            cp.start()
        passed = [copy(4 + j, (*chip, c), sibling) for j, chip in enumerate(chips)]
        for j, chip in enumerate(chips):
            copy(1 + j, (*chip, c), me).wait_recv()
            passed[j].start()
        copy(0, sibling, me).wait_recv()
        for j, chip in enumerate(chips):
            copy(4 + j, (*chip, 1 - c), me).wait_recv()
        for cp in first + passed:
            cp.wait_send()
        acc = gath[0]
        for d in range(1, 8):
            acc = acc + gath[d]
        for k in range(n):
            r, w = arrs[k].shape
            outs[k][...] = acc[offs[k]:offs[k] + r, 0:w]

    vm = pl.BlockSpec(memory_space=pltpu.VMEM)
    return pl.pallas_call(
        body, name="allreduce_small", in_specs=[vm] * n, out_specs=[vm] * n,
        out_shape=[jax.ShapeDtypeStruct(a.shape, F32) for a in arrs],
        scratch_shapes=[pltpu.VMEM((8, rows, width), F32), pltpu.SemaphoreType.DMA((7,)),
                        pltpu.SemaphoreType.DMA((7,))],
        compiler_params=pltpu.CompilerParams(vmem_limit_bytes=VMEM_LIMIT),
    )(*arrs)


def _adamw_small_call(ws, gs, ms, vs):
    n = len(ws)
    c1 = 1.0 - ADAM_B1 ** ADAM_STEP
    c2 = 1.0 - ADAM_B2 ** ADAM_STEP

    def body(*refs):
        for k in range(n):
            w_ref, g_ref, m_ref, v_ref = (refs[j * n + k] for j in range(4))
            d_ref, mo_ref, vo_ref = (refs[(4 + j) * n + k] for j in range(3))
            gv = g_ref[...]
            m2 = ADAM_B1 * m_ref[...] + (1.0 - ADAM_B1) * gv
            v2 = ADAM_B2 * v_ref[...] + (1.0 - ADAM_B2) * (gv * gv)
            d_ref[...] = -ADAM_LR * ((m2 / c1) / (jnp.sqrt(v2 / c2) + ADAM_EPS) + ADAM_WD * w_ref[...])
            mo_ref[...] = m2
            vo_ref[...] = v2

    vm = pl.BlockSpec(memory_space=pltpu.VMEM)
    res = pl.pallas_call(
        body, name="adamw_small", in_specs=[vm] * (4 * n), out_specs=[vm] * (3 * n),
        out_shape=[jax.ShapeDtypeStruct(a.shape, F32) for a in ws] * 3,
        compiler_params=pltpu.CompilerParams(vmem_limit_bytes=VMEM_LIMIT),
    )(*ws, *gs, *ms, *vs)
    return res[:n], res[n:2 * n], res[2 * n:]


def _pack(arrs, min_rows=8):
    parts = []
    for a in arrs:
        flat = a.reshape(-1).astype(F32)
        parts.append(jnp.pad(flat, (0, (-flat.shape[0]) % BLK)))
    buf = jnp.concatenate(parts).reshape(-1, BLK)
    return jnp.pad(buf, ((0, (-buf.shape[0]) % min_rows), (0, 0)))


def _unpack(buf, shapes):
    out, r = [], 0
    for shp in shapes:
        n = math.prod(shp)
        nr = -(-n // BLK)
        out.append(buf[r:r + nr].reshape(-1)[:n].reshape(shp))
        r += nr
    return out


SMALL = ["meta_tokens", "mix_pre_g", "ssd_conv_w", "ssd_conv_b", "ssd_dt_bias", "ssd_a_log", "ssd_d", "ssd_norm_g",
         "sb_norm_g", "mix_post_g", "ffn_pre_g", "ffn_conv_w", "ffn_conv_b", "ffn_post_g"]
BIG = ["w_in", "w_out", "w_up", "w_down"]
WEIGHTS = ["meta_tokens", "mix_pre_g", "w_in", "ssd_conv_w", "ssd_conv_b", "ssd_dt_bias", "ssd_a_log", "ssd_d",
           "ssd_norm_g", "sb_norm_g", "w_out", "mix_post_g", "ffn_pre_g", "w_up", "ffn_conv_w", "ffn_conv_b",
           "w_down", "ffn_post_g"]
W_IN_SHARD = IN_COLS // N_CHIPS
W_IN_PAD = 1536


def kernel(x, meta_tokens, mix_pre_g, w_in, ssd_conv_w, ssd_conv_b, ssd_dt_bias, ssd_a_log, ssd_d, ssd_norm_g, sb_norm_g, w_out, mix_post_g, ffn_pre_g, w_up, ffn_conv_w, ffn_conv_b, w_down, ffn_post_g, loss_target, m_meta_tokens, m_mix_pre_g, m_w_in, m_ssd_conv_w, m_ssd_conv_b, m_ssd_dt_bias, m_ssd_a_log, m_ssd_d, m_ssd_norm_g, m_sb_norm_g, m_w_out, m_mix_post_g, m_ffn_pre_g, m_w_up, m_ffn_conv_w, m_ffn_conv_b, m_w_down, m_ffn_post_g, v_meta_tokens, v_mix_pre_g, v_w_in, v_ssd_conv_w, v_ssd_conv_b, v_ssd_dt_bias, v_ssd_a_log, v_ssd_d, v_ssd_norm_g, v_sb_norm_g, v_w_out, v_mix_post_g, v_ffn_pre_g, v_w_up, v_ffn_conv_w, v_ffn_conv_b, v_w_down, v_ffn_post_g):
    w = dict(meta_tokens=meta_tokens, mix_pre_g=mix_pre_g, w_in=w_in, ssd_conv_w=ssd_conv_w, ssd_conv_b=ssd_conv_b, ssd_dt_bias=ssd_dt_bias, ssd_a_log=ssd_a_log, ssd_d=ssd_d, ssd_norm_g=ssd_norm_g, sb_norm_g=sb_norm_g, w_out=w_out, mix_post_g=mix_post_g, ffn_pre_g=ffn_pre_g, w_up=w_up, ffn_conv_w=ffn_conv_w, ffn_conv_b=ffn_conv_b, w_down=w_down, ffn_post_g=ffn_post_g)
    m = dict(meta_tokens=m_meta_tokens, mix_pre_g=m_mix_pre_g, w_in=m_w_in, ssd_conv_w=m_ssd_conv_w, ssd_conv_b=m_ssd_conv_b, ssd_dt_bias=m_ssd_dt_bias, ssd_a_log=m_ssd_a_log, ssd_d=m_ssd_d, ssd_norm_g=m_ssd_norm_g, sb_norm_g=m_sb_norm_g, w_out=m_w_out, mix_post_g=m_mix_post_g, ffn_pre_g=m_ffn_pre_g, w_up=m_w_up, ffn_conv_w=m_ffn_conv_w, ffn_conv_b=m_ffn_conv_b, w_down=m_w_down, ffn_post_g=m_ffn_post_g)
    v = dict(meta_tokens=v_meta_tokens, mix_pre_g=v_mix_pre_g, w_in=v_w_in, ssd_conv_w=v_ssd_conv_w, ssd_conv_b=v_ssd_conv_b, ssd_dt_bias=v_ssd_dt_bias, ssd_a_log=v_ssd_a_log, ssd_d=v_ssd_d, ssd_norm_g=v_ssd_norm_g, sb_norm_g=v_sb_norm_g, w_out=v_w_out, mix_post_g=v_mix_post_g, ffn_pre_g=v_ffn_pre_g, w_up=v_w_up, ffn_conv_w=v_ffn_conv_w, ffn_conv_b=v_ffn_conv_b, w_down=v_w_down, ffn_post_g=v_ffn_post_g)
    chip = 2 * lax.axis_index("x") + lax.axis_index("y")
    me = 2 * chip + lax.axis_index("c")
    place = jnp.stack([lax.axis_index("c"), chip] + [me ^ p for p in range(1, 8)]).astype(jnp.int32)

    shard_small = [w["meta_tokens"], w["ssd_conv_w"][0], w["ffn_conv_w"][0]]
    shards = [jnp.pad(w["w_in"][0], ((0, 0), (0, W_IN_PAD - W_IN_SHARD))).astype(BF16), _pack(shard_small, 16)]
    gathered = _allgather_call(shards)
    late_shards = [w["w_out"][0].astype(BF16), w["w_up"][0].astype(BF16), w["w_down"][0].astype(BF16)]

    def blocks(own, got):
        return [jnp.where(chip == i, own, got[i]) for i in range(N_CHIPS)]

    def late_weights(got):
        return (jnp.concatenate(blocks(late_shards[0], got[0]), axis=0),
                jnp.concatenate(blocks(late_shards[1], got[1]), axis=1),
                jnp.concatenate(blocks(late_shards[2], got[2]), axis=0))

    cut = DT_REAL_OFF + N_HEADS - W_IN_SHARD
    s_in = blocks(shards[0], gathered[0])
    w_in_c = jnp.concatenate(
        [s_in[0][:, :W_IN_SHARD], s_in[1][:, :cut], jnp.zeros((D_MODEL, BLK - N_HEADS), BF16),
         s_in[1][:, cut:W_IN_SHARD], s_in[2][:, :W_IN_SHARD], s_in[3][:, :W_IN_SHARD]], axis=1)
    parts = [_unpack(b, [s.shape for s in shard_small]) for b in blocks(shards[1], gathered[1])]
    wt = {k: w[k][0][None] if w[k].ndim == 3 else w[k] for k in
          ["mix_pre_g", "ssd_conv_b", "ssd_dt_bias", "ssd_a_log", "ssd_d", "ssd_norm_g", "sb_norm_g", "mix_post_g",
           "ffn_pre_g", "ffn_conv_b", "ffn_post_g"]}
    wt.update(
        meta=jnp.concatenate([p[0] for p in parts], axis=1),
        ssd_conv_w=jnp.concatenate([p[1] for p in parts], axis=1),
        ffn_conv_w=jnp.concatenate([p[2] for p in parts], axis=1), w_in=w_in_c)

    def w_in_shards(g):
        skip = BLK - N_HEADS
        cols = [g[:, :W_IN_SHARD],
                jnp.concatenate([g[:, W_IN_SHARD:W_IN_SHARD + cut], g[:, C_Q:2 * W_IN_SHARD + skip]], axis=1),
                g[:, 2 * W_IN_SHARD + skip:3 * W_IN_SHARD + skip], g[:, 3 * W_IN_SHARD + skip:]]
        g = jnp.stack([jnp.pad(b, ((0, 0), (0, W_IN_PAD - W_IN_SHARD))) for b in cols])
        return g, g.astype(BF16)

    loss_row, dx, small, pending = _local_step(x[0], loss_target[0], wt, late_shards, late_weights, w_in_shards)

    full = _half_exchange_call([_grad_sum_call(*pending[k], place, "grad_sum_" + k) for k in BIG])
    grads = {"w_in": full[0][:, :W_IN_SHARD], "w_out": full[1], "w_up": full[2], "w_down": full[3]}

    red_list = _allreduce_small_call([small[k] for k in SMALL] + [loss_row])
    loss = jnp.sum(red_list[-1])
    for k, g in zip(SMALL, red_list[:-1]):
        grads[k] = g
    for k in ["meta_tokens", "ssd_conv_w", "ffn_conv_w"]:
        wk = w[k].shape[-1]
        grads[k] = lax.dynamic_slice_in_dim(grads[k], chip * wk, wk, axis=1)

    delta, new_m, new_v = {}, {}, {}
    for k in BIG:
        delta[k], new_m[k], new_v[k] = _adamw_call(w[k][0], grads[k], m[k][0], v[k][0], "adamw_" + k)
    flat = lambda d: [d[k].reshape(grads[k].shape) for k in SMALL]
    res = _adamw_small_call(flat(w), [grads[k] for k in SMALL], flat(m), flat(v))
    for out, arrs in zip((delta, new_m, new_v), res):
        for k, a in zip(SMALL, arrs):
            out[k] = a

    def shaped(d, k):
        return d[k].reshape(w[k].shape)

    return (loss, dx[None], *[shaped(grads, k) for k in WEIGHTS], *[shaped(delta, k) for k in WEIGHTS],
            *[shaped(new_m, k) for k in WEIGHTS], *[shaped(new_v, k) for k in WEIGHTS])
```

```python
import functools
import math

import jax
import jax.numpy as jnp
from jax import lax
from jax.experimental import pallas as pl
from jax.experimental.pallas import tpu as pltpu

F32 = jnp.float32
BF16 = jnp.bfloat16

D_MODEL = 1024
N_META = 16
BLK = 128
PAD = BLK - N_META
HEAD_DIM = 64
N_HEADS = 16
SSD_GROUPS = 2
SSD_STATE = 128
SSD_INNER = 1024
XBC = SSD_INNER + 2 * SSD_GROUPS * SSD_STATE
D_FF = 2816
EPS = 1e-6
IN_COLS = 5648
C_Z, C_XBC, C_DT, C_Q, C_K, C_V, C_END = 0, 1024, 2560, 2688, 3712, 4736, 5760
DT_REAL_OFF = 2560
N_CHIPS = 4
ADAM_LR, ADAM_B1, ADAM_B2, ADAM_EPS, ADAM_WD, ADAM_STEP = 0.001, 0.9, 0.999, 1e-08, 0.01, 10
VMEM_LIMIT = 56 * 1024 * 1024
MESH = pl.DeviceIdType.MESH


def _cparams(sem=None, **kw):
    if sem is not None:
        kw["dimension_semantics"] = sem
    return pltpu.CompilerParams(vmem_limit_bytes=VMEM_LIMIT, **kw)


def _pick(n, cands):
    for c in cands:
        if n % c == 0:
            return c
    raise ValueError((n, cands))


def _iota(shape, dim):
    return lax.broadcasted_iota(jnp.int32, shape, dim)


def _sigmoid(x):
    return 1.0 / (1.0 + jnp.exp(-x))


def _split2(v):
    h1 = v.astype(BF16)
    return h1, (v - h1.astype(F32)).astype(BF16)


def _dot(a, b, ca=1, cb=0):
    return lax.dot_general(a, b, (((ca,), (cb,)), ((), ())), preferred_element_type=F32)


def _dot_sel_r(v, sel, cb=0):
    h1, h2 = _split2(v)
    return _dot(h1, sel, 1, cb) + _dot(h2, sel, 1, cb)


def _dot_sel_l(sel, v, ca=1):
    h1, h2 = _split2(v)
    return _dot(sel, h1, ca, 0) + _dot(sel, h2, ca, 0)


def _mm(a, b, *, ta=False, tb=False, tm, tn, tk, out_dtype=F32, nsplit=1, extra_bf16=False, ride=None, name):
    K, M = (a.shape if ta else a.shape[::-1])
    N = b.shape[0] if tb else b.shape[1]
    assert M % tm == 0 and N % tn == 0 and K % tk == 0, (name, M, N, K, tm, tn, tk)
    nm, nn, nk = M // tm, N // tn, K // tk
    assert nn % nsplit == 0
    per = nn // nsplit
    a_spec = (pl.BlockSpec((tk, tm), lambda i, j, k: (k, i)) if ta
              else pl.BlockSpec((tm, tk), lambda i, j, k: (i, k)))
    b_spec = (pl.BlockSpec((tn, tk), lambda i, j, k: (j, k)) if tb
              else pl.BlockSpec((tk, tn), lambda i, j, k: (k, j)))
    o_spec = pl.BlockSpec((None, tm, tn), lambda i, j, k: (j // per, i, j % per))
    n_out = 2 if extra_bf16 else 1
    ca, cb = (0 if ta else 1), (1 if tb else 0)
    ns, nl = (len(ride[0]), len(ride[1])) if ride else (0, 0)

    def body(a_ref, b_ref, *rest):
        outs = rest[ns:ns + n_out]
        if ride:
            step = (pl.program_id(0) * nn + pl.program_id(1)) * nk + pl.program_id(2)
            _ride_run(ride, rest[:ns], rest[ns + n_out:ns + n_out + nl], rest[-2], rest[-1],
                      step == 0, step == nm * nn * nk - 1)
        p = _dot(a_ref[...].astype(BF16), b_ref[...].astype(BF16), ca, cb)

        def emit(val):
            outs[0][...] = val.astype(out_dtype)
            if extra_bf16:
                outs[1][...] = val.astype(BF16)

        if nk == 1:
            emit(p)
        else:
            acc = rest[ns + n_out + nl]
            k = pl.program_id(2)

            @pl.when(k == 0)
            def _():
                acc[...] = p

            @pl.when(k > 0)
            def _():
                acc[...] += p

            @pl.when(k == nk - 1)
            def _():
                emit(acc[...])

    shp = (nsplit, M, N // nsplit)
    out_shape = [jax.ShapeDtypeStruct(shp, out_dtype)]
    out_specs = [o_spec]
    if extra_bf16:
        out_shape.append(jax.ShapeDtypeStruct(shp, BF16))
        out_specs.append(o_spec)
    scratch = [pltpu.VMEM((tm, tn), F32)] if nk > 1 else []
    if ride:
        res = pl.pallas_call(
            body, name=name, grid=(nm, nn, nk), in_specs=[a_spec, b_spec] + [ANY] * ns,
            out_specs=out_specs + [ANY] * nl, out_shape=out_shape + list(ride[1]),
            scratch_shapes=scratch + _ride_scratch(ride),
            compiler_params=_cparams(("arbitrary", "arbitrary", "arbitrary")),
        )(a, b, *ride[0])
        return (res[:n_out] if extra_bf16 else res[0]), list(res[n_out:])
    res = pl.pallas_call(
        body, name=name, grid=(nm, nn, nk), in_specs=[a_spec, b_spec], out_specs=out_specs,
        out_shape=out_shape, scratch_shapes=scratch,
        compiler_params=_cparams(("parallel", "parallel", "arbitrary")),
    )(a, b)
    return res if extra_bf16 else res[0]


def _rms_stats(x):
    r = lax.rsqrt(jnp.mean(x * x, axis=-1, keepdims=True) + EPS)
    return r, x * r


def _rms_bwd(x, g, dy):
    r, xh = _rms_stats(x)
    dxh = dy * g
    dx = r * (dxh - xh * jnp.mean(dxh * xh, axis=-1, keepdims=True))
    return dx, jnp.sum(dy * xh, axis=0, keepdims=True)


def _row_spec(tr, w, col=0):
    return pl.BlockSpec((tr, w), lambda i: (i, col))


def _vec_spec(w):
    return pl.BlockSpec((1, w), lambda i: (0, 0))


def _acc_rows(ref, val, i):
    @pl.when(i == 0)
    def _():
        ref[...] = val

    @pl.when(i > 0)
    def _():
        ref[...] += val


def _rms_fwd_call(x, g, name, beside=None):
    lp, w = x.shape
    tr = _pick(lp, [384, 128])

    def body(x_ref, g_ref, *rest):
        _, xh = _rms_stats(x_ref[...])
        rest[-1][...] = (xh * g_ref[...]).astype(BF16)

    if beside is None:
        return pl.pallas_call(
            body, name=name, grid=(lp // tr,), in_specs=[_row_spec(tr, w), _vec_spec(w)],
            out_specs=_row_spec(tr, w), out_shape=jax.ShapeDtypeStruct((lp, w), BF16),
            compiler_params=_cparams(("parallel",)))(x, g)
    return pl.pallas_call(
        body, name=name, grid=(lp // tr,), in_specs=[_row_spec(tr, w), _vec_spec(w), ANY],
        out_specs=_row_spec(tr, w, 1), out_shape=jax.ShapeDtypeStruct((lp, 2 * w), BF16),
        input_output_aliases={2: 0}, compiler_params=_cparams(("parallel",)))(x, g, beside)


def _mid_fwd_call(h0, mix, g_post, g_pre2):
    lp, w = h0.shape
    tr = _pick(lp, [384, 128])

    def body(h0_ref, mix_ref, gp_ref, g2_ref, h1_ref, xn_ref):
        _, mh = _rms_stats(mix_ref[...])
        h1 = h0_ref[...] + mh * gp_ref[...]
        h1_ref[...] = h1
        _, hh = _rms_stats(h1)
        xn_ref[...] = (hh * g2_ref[...]).astype(BF16)

    return pl.pallas_call(
        body, name="mid_fwd", grid=(lp // tr,),
        in_specs=[_row_spec(tr, w), _row_spec(tr, w), _vec_spec(w), _vec_spec(w)],
        out_specs=[_row_spec(tr, w), _row_spec(tr, w)],
        out_shape=[jax.ShapeDtypeStruct((lp, w), F32), jax.ShapeDtypeStruct((lp, w), BF16)],
        compiler_params=_cparams(("parallel",)))(h0, mix, g_post, g_pre2)


def _final_call(h1, f, g_post, target):
    lp, w = h1.shape
    tr = BLK
    nb = lp // tr

    def body(h1_ref, f_ref, g_ref, t_ref, loss_ref, df_ref, dh_ref, dg_ref):
        i = pl.program_id(0)
        fv = f_ref[...]
        g = g_ref[...]
        _, fh = _rms_stats(fv)
        h2 = h1_ref[...] + fh * g
        diff = jnp.where(i > 0, h2 - t_ref[...], 0.0)
        part = 0.5 * jnp.sum(diff * diff, axis=0, keepdims=True) * (1.0 / w)
        _acc_rows(loss_ref, part, i)
        dh = diff * (1.0 / w)
        dh_ref[...] = dh
        df, dg = _rms_bwd(fv, g, dh)
        df_ref[...] = df.astype(BF16)
        _acc_rows(dg_ref, dg, i)

    t_spec = pl.BlockSpec((tr, w), lambda i: (jnp.maximum(i - 1, 0), 0))
    return pl.pallas_call(
        body, name="final_fwd_bwd", grid=(nb,),
        in_specs=[_row_spec(tr, w), _row_spec(tr, w), _vec_spec(w), t_spec],
        out_specs=[_vec_spec(w), _row_spec(tr, w), _row_spec(tr, w), _vec_spec(w)],
        out_shape=[jax.ShapeDtypeStruct((1, w), F32), jax.ShapeDtypeStruct((lp, w), BF16),
                   jax.ShapeDtypeStruct((lp, w), F32), jax.ShapeDtypeStruct((1, w), F32)],
        compiler_params=_cparams(("arbitrary",)))(h1, f, g_post, target)


def _mid_bwd_call(dh2, h1, dxn2, mix, g_pre2, g_post):
    lp, w = h1.shape
    tr = _pick(lp, [384, 128])

    def body(dh2_ref, h1_ref, dxn_ref, mix_ref, g2_ref, gp_ref, dh1_ref, dmix_ref, dg2_ref, dgp_ref):
        i = pl.program_id(0)
        live = (i * tr + _iota((tr, 1), 0)) >= PAD
        dx, dg2 = _rms_bwd(h1_ref[...], g2_ref[...], dxn_ref[...])
        dh1 = jnp.where(live, dh2_ref[...] + dx, 0.0)
        dh1_ref[...] = dh1
        dmix, dgp = _rms_bwd(mix_ref[...], gp_ref[...], dh1)
        dmix_ref[...] = jnp.where(live, dmix, 0.0).astype(BF16)
        _acc_rows(dg2_ref, dg2, i)
        _acc_rows(dgp_ref, dgp, i)

    rs = _row_spec(tr, w)
    return pl.pallas_call(
        body, name="mid_bwd", grid=(lp // tr,),
        in_specs=[rs, rs, rs, rs, _vec_spec(w), _vec_spec(w)],
        out_specs=[rs, rs, _vec_spec(w), _vec_spec(w)],
        out_shape=[jax.ShapeDtypeStruct((lp, w), F32), jax.ShapeDtypeStruct((lp, w), BF16),
                   jax.ShapeDtypeStruct((1, w), F32), jax.ShapeDtypeStruct((1, w), F32)],
        compiler_params=_cparams(("arbitrary",)))(dh2, h1, dxn2, mix, g_pre2, g_post)


def _norm_bwd_call(x, g, dy_arr, dy_col, name, res=None):
    lp, w = x.shape
    tr = _pick(lp, [384, 128])
    has_res = res is not None

    def body(x_ref, g_ref, dy_ref, *rest):
        i = pl.program_id(0)
        live = (i * tr + _iota((tr, 1), 0)) >= PAD
        dx, dg = _rms_bwd(x_ref[...], g_ref[...], dy_ref[...])
        if has_res:
            dx = dx + rest[0][...]
        out_ref, dg_ref = rest[-2], rest[-1]
        out_ref[...] = jnp.where(live, dx, 0.0)
        _acc_rows(dg_ref, dg, i)

    rs = _row_spec(tr, w)
    ins = [rs, _vec_spec(w), _row_spec(tr, w, dy_col)] + ([rs] if has_res else [])
    args = [x, g, dy_arr] + ([res] if has_res else [])
    return pl.pallas_call(
        body, name=name, grid=(lp // tr,), in_specs=ins, out_specs=[rs, _vec_spec(w)],
        out_shape=[jax.ShapeDtypeStruct((lp, w), F32), jax.ShapeDtypeStruct((1, w), F32)],
        compiler_params=_cparams(("arbitrary",)))(*args)


def _shift_down(cur, prev_tail, s, rows):
    if s == 0:
        return cur
    prev = jnp.tile(prev_tail, (cur.shape[0] // 8, 1))
    return jnp.where(rows >= s, pltpu.roll(cur, s, 0), pltpu.roll(prev, s, 0))


def _shift_up(cur, next_head, s, rows):
    if s == 0:
        return cur
    n = cur.shape[0]
    nxt = jnp.tile(next_head, (n // 8, 1))
    return jnp.where(rows < n - s, pltpu.roll(cur, n - s, 0), pltpu.roll(nxt, n - s, 0))


def _gelu_tanh(x):
    c = math.sqrt(2.0 / math.pi)
    t = jnp.tanh(c * (x + 0.044715 * x * x * x))
    return 0.5 * x * (1.0 + t), t


def _conv_fwd_call(src, col0, width, cw, w8, b, taps, *, gate_src=None, gate_col0=0, rb=BLK, name):
    lp = src.shape[0]
    nb, nc = lp // rb, width // cw
    cb0 = col0 // cw
    ffn = gate_src is not None

    def body(x_ref, w_ref, b_ref, *rest):
        if ffn:
            u_ref, y_ref, a_ref, tail = rest
        else:
            y_ref, a_ref, tail = rest
        i = pl.program_id(1)

        @pl.when(i == 0)
        def _():
            tail[...] = jnp.zeros_like(tail)

        cur = x_ref[...]
        rows = _iota((rb, cw), 0)
        y = b_ref[...] + w_ref[taps - 1:taps, :] * cur
        pt = tail[...]
        for s in range(1, taps):
            y = y + w_ref[taps - 1 - s:taps - s, :] * _shift_down(cur, pt, s, rows)
        tail[...] = cur[rb - 8:, :]
        y_ref[...] = y
        if ffn:
            ge, _ = _gelu_tanh(y)
            a_ref[...] = (ge * u_ref[...]).astype(BF16)
        else:
            live = (i * rb + rows) >= PAD
            a_ref[...] = jnp.where(live, y * _sigmoid(y), 0.0)

    blk = lambda c0: pl.BlockSpec((rb, cw), lambda j, i: (i, c0 + j))
    ins = [blk(cb0), pl.BlockSpec((8, cw), lambda j, i: (0, j)), pl.BlockSpec((1, cw), lambda j, i: (0, j))]
    args = [src, w8, b]
    if ffn:
        ins.append(blk(gate_col0 // cw))
        args.append(gate_src)
    return pl.pallas_call(
        body, name=name, grid=(nc, nb), in_specs=ins, out_specs=[blk(0), blk(0)],
        out_shape=[jax.ShapeDtypeStruct((lp, width), F32),
                   jax.ShapeDtypeStruct((lp, width), BF16 if ffn else F32)],
        scratch_shapes=[pltpu.VMEM((8, cw), F32)],
        compiler_params=_cparams(("parallel", "arbitrary")))(*args)


def _conv_bwd_call(src, col0, width, cw, w8, taps, ypre, dact, *, gate_src=None, gate_col0=0, rb=BLK, name):
    lp = src.shape[0]
    nb, nc = lp // rb, width // cw
    cb0 = col0 // cw
    ffn = gate_src is not None

    def body(x_ref, w_ref, y_ref, d_ref, *rest):
        if ffn:
            u_ref, dx_ref, du_ref, dw_ref, db_ref, head = rest
        else:
            dx_ref, dw_ref, db_ref, head = rest
        step = pl.program_id(1)
        i = nb - 1 - step

        @pl.when(step == 0)
        def _():
            head[...] = jnp.zeros_like(head)

        rows = _iota((rb, cw), 0)
        live = (i * rb + rows) >= PAD
        y = y_ref[...]
        d = d_ref[...]
        if ffn:
            ge, t = _gelu_tanh(y)
            c = math.sqrt(2.0 / math.pi)
            dge = 0.5 * (1.0 + t) + 0.5 * y * (1.0 - t * t) * c * (1.0 + 3.0 * 0.044715 * y * y)
            u = u_ref[...]
            du_ref[...] = jnp.where(live, d * ge, 0.0).astype(BF16)
            dy = jnp.where(live, d * u * dge, 0.0)
        else:
            sg = _sigmoid(y)
            dy = jnp.where(live, d * sg * (1.0 + y * (1.0 - sg)), 0.0)
        x = x_ref[...]
        nh = head[...]
        dx = jnp.zeros_like(dy)
        dws = []
        for s in range(taps):
            sh = _shift_up(dy, nh, s, rows)
            dx = dx + w_ref[taps - 1 - s:taps - s, :] * sh
            dws.append(jnp.sum(x * sh, axis=0, keepdims=True))
        head[...] = dy[:8, :]
        dx_ref[...] = jnp.where(live, dx, 0.0).astype(BF16)
        dw = jnp.concatenate([dws[taps - 1 - k] for k in range(taps)]
                             + [jnp.zeros((8 - taps, cw), F32)], axis=0)
        _acc_rows(dw_ref, dw, step)
        _acc_rows(db_ref, jnp.sum(dy, axis=0, keepdims=True), step)

    blk = lambda c0: pl.BlockSpec((rb, cw), lambda j, s: (nb - 1 - s, c0 + j))
    ins = [blk(cb0), pl.BlockSpec((8, cw), lambda j, s: (0, j)), blk(0), blk(0)]
    args = [src, w8, ypre, dact]
    outs = [blk(0)]
    oshape = [jax.ShapeDtypeStruct((lp, width), BF16)]
    if ffn:
        ins.append(blk(gate_col0 // cw))
        args.append(gate_src)
        outs.append(blk(0))
        oshape.append(jax.ShapeDtypeStruct((lp, width), BF16))
    outs += [pl.BlockSpec((8, cw), lambda j, s: (0, j)), pl.BlockSpec((1, cw), lambda j, s: (0, j))]
    oshape += [jax.ShapeDtypeStruct((8, width), F32), jax.ShapeDtypeStruct((1, width), F32)]
    return pl.pallas_call(
        body, name=name, grid=(nc, nb), in_specs=ins, out_specs=outs, out_shape=oshape,
        scratch_shapes=[pltpu.VMEM((8, cw), F32)],
        compiler_params=_cparams(("parallel", "arbitrary")))(*args)


SB_FIRST = 3
SB_GROUP = 4
SB_DEAD = -110.0


def _sb_scores(qm_h, kb):
    z = _dot(qm_h, kb, 1, 1)
    sp = jnp.maximum(z, 0.0) + jnp.log(1.0 + jnp.exp(-jnp.abs(z)))
    return z - sp, -sp


def _dot_tri1(v, tri2):
    r = _dot(v.astype(BF16), tri2[:BLK])
    return r[:, :BLK], r[:, BLK:]


def _tri2(cond):
    t = jnp.concatenate([cond.astype(BF16), jnp.ones((BLK, BLK), BF16)], axis=1)
    return jnp.concatenate([t, t], axis=0)


def _dot_tri(v, tri2):
    hi = v.astype(BF16)
    lo = (v - hi.astype(F32)).astype(BF16)
    r = _dot(jnp.concatenate([hi, lo], axis=1), tri2)
    return r[:, :BLK], r[:, BLK:]


def _sb_fwd_call(proj, ride):
    lp = proj.shape[0]
    nb = lp // BLK
    scale = 1.0 / math.sqrt(HEAD_DIM)

    ns, nl = len(ride[0]), len(ride[1])
    npair = N_HEADS // 2

    def body(q_ref, k_ref, v_ref, *rest):
        o_ref, tl_ref = rest[ns], rest[ns + 1]
        i = pl.program_id(1)
        step = pl.program_id(0) * nb + i
        _ride_run(ride, rest[:ns], rest[ns + 2:ns + 2 + nl], rest[-2], rest[-1], step == 0, step == npair * nb - 1)
        lane = _iota((2 * BLK, BLK), 1)
        row = _iota((2 * BLK, BLK), 0)
        first = row < BLK
        qrow = row & (BLK - 1)
        q = q_ref[...] * scale
        q2 = jnp.concatenate([q, q], axis=0)
        qm = jnp.where(first == (lane < HEAD_DIM), q2, 0.0).astype(BF16)
        tri = _tri2(_iota((BLK, BLK), 0) > _iota((BLK, BLK), 1))

        def chunk(off, nsub, last_valid, carry):
            width = nsub * BLK
            sls = [slice(b * BLK, (b + 1) * BLK) for b in range(nsub)]
            kb = k_ref[pl.ds(off, width), :].astype(BF16)
            vb = v_ref[pl.ds(off, width), :].astype(BF16)
            lb, lk = _sb_scores(qm, kb)
            lks = [lk[:, sl] for sl in sls]
            first_valid = (off + lane) >= PAD
            lks[0] = jnp.where(first_valid, lks[0], 0.0)
            if last_valid is not None:
                lks[-1] = jnp.where(last_valid, lks[-1], 0.0)
            afters = [_dot_tri(lks[b], tri) for b in range(nsub)]
            run, acc = carry
            ws = [None] * nsub
            for b in reversed(range(nsub)):
                wb = jnp.exp(lb[:, sls[b]] + afters[b][0] + run)
                if b == 0:
                    wb = jnp.where(first_valid, wb, 0.0)
                if last_valid is not None and b == nsub - 1:
                    wb = jnp.where(last_valid, wb, 0.0)
                ws[b] = wb.astype(BF16)
                run = run + afters[b][1]
            w = ws[0] if nsub == 1 else jnp.concatenate(ws, axis=1)
            return run, acc + _dot(w, vb)

        before = jnp.minimum(i, SB_FIRST - 1)
        first_off = pl.multiple_of((i - before) * BLK, BLK)
        diag = lane < qrow
        zero = jnp.zeros((2 * BLK, BLK), F32)
        carry = lax.switch(before, [functools.partial(chunk, first_off, n, diag) for n in range(1, SB_FIRST + 1)],
                           (zero, zero))

        def walk(n):
            def body(state):
                off = pl.multiple_of((state[0] - (n - 1)) * BLK, BLK)
                return (state[0] - n, *chunk(off, n, None, state[1:]))

            def cond(state):
                return jnp.logical_and(state[0] >= n - 1, jnp.max(state[1]) > SB_DEAD)

            return cond, body

        state = lax.while_loop(*walk(SB_GROUP), (i - SB_FIRST, *carry))
        pos, run, acc = lax.while_loop(*walk(1), state)
        low = lane[:BLK] < HEAD_DIM
        o_ref[...] = jnp.where(low, acc[:BLK], acc[BLK:])
        tl = jnp.where(low, run[:BLK], run[BLK:])
        tl_ref[...] = jnp.where(lane[:BLK] == 1, jnp.maximum(pos + 1, 0).astype(F32), tl)

    qc, kc, vc = C_Q // BLK, C_K // BLK, C_V // BLK
    blk = pl.BlockSpec((BLK, BLK), lambda p, i: (i, p))
    res = pl.pallas_call(
        body, name="sb_fwd", grid=(npair, nb),
        in_specs=[pl.BlockSpec((BLK, BLK), lambda p, i: (i, qc + p)),
                  pl.BlockSpec((lp, BLK), lambda p, i: (0, kc + p)),
                  pl.BlockSpec((lp, BLK), lambda p, i: (0, vc + p))] + [ANY] * ns,
        out_specs=[blk, blk] + [ANY] * nl,
        out_shape=[jax.ShapeDtypeStruct((lp, N_HEADS * HEAD_DIM), F32)] * 2 + list(ride[1]),
        scratch_shapes=_ride_scratch(ride),
        compiler_params=_cparams(("arbitrary", "arbitrary")))(proj, proj, proj, *ride[0])
    return res[0], res[1], list(res[2:])


def _sb_bwd_call(proj, tl, do, ride):
    lp = proj.shape[0]
    nb = lp // BLK
    scale = 1.0 / math.sqrt(HEAD_DIM)

    ns, nl = len(ride[0]), len(ride[1])
    npair = N_HEADS // 2

    def body(q_ref, k_ref, v_ref, tl_ref, do_ref, *rest):
        dq_ref, dk_ref, dv_ref = rest[ns:ns + 3]
        dk_acc, dv_acc = rest[ns + 3 + nl:ns + 5 + nl]
        i = pl.program_id(1)
        step = pl.program_id(0) * nb + i
        _ride_run(ride, rest[:ns], rest[ns + 3:ns + 3 + nl], rest[-2], rest[-1], step == 0, step == npair * nb - 1)

        @pl.when(i == 0)
        def _():
            dk_acc[...] = jnp.zeros_like(dk_acc)
            dv_acc[...] = jnp.zeros_like(dv_acc)

        lane = _iota((2 * BLK, BLK), 1)
        row = _iota((2 * BLK, BLK), 0)
        qrow = row & (BLK - 1)
        mine = (row < BLK) == (lane < HEAD_DIM)
        q = q_ref[...] * scale
        dov = do_ref[...]
        qm = jnp.where(mine, jnp.concatenate([q, q], axis=0), 0.0).astype(BF16)
        dom = jnp.where(mine, jnp.concatenate([dov, dov], axis=0), 0.0).astype(BF16)
        tlv = tl_ref[...]
        tot = jnp.concatenate([jnp.broadcast_to(tlv[:, 0:1], (BLK, BLK)),
                               jnp.broadcast_to(tlv[:, HEAD_DIM:HEAD_DIM + 1], (BLK, BLK))], axis=0)
        r1, l1 = _iota((BLK, BLK), 0), _iota((BLK, BLK), 1)
        tri_in = _tri2(r1 <= l1)
        tri_ex = _tri2(r1 < l1)

        def chunk(off, nsub, last_valid, carry):
            width = nsub * BLK
            sls = [slice(b * BLK, (b + 1) * BLK) for b in range(nsub)]
            cat = lambda parts: parts[0] if nsub == 1 else jnp.concatenate(parts, axis=1)
            mask_last = lambda b: last_valid is not None and b == nsub - 1
            kb = k_ref[pl.ds(off, width), :].astype(BF16)
            vb = v_ref[pl.ds(off, width), :].astype(BF16)
            lb, lk = _sb_scores(qm, kb)
            dw = _dot(dom, vb, 1, 1)
            lks = [lk[:, sl] for sl in sls]
            first_valid = (off + lane) >= PAD
            lks[0] = jnp.where(first_valid, lks[0], 0.0)
            if last_valid is not None:
                lks[-1] = jnp.where(last_valid, lks[-1], 0.0)
            pins = [_dot_tri(lks[b], tri_in) for b in range(nsub)]
            run, gsum, dq = carry
            ws, gs = [], []
            for b in range(nsub):
                wb = jnp.exp(lb[:, sls[b]] + (tot - run - pins[b][0]))
                if b == 0:
                    wb = jnp.where(first_valid, wb, 0.0)
                if mask_last(b):
                    wb = jnp.where(last_valid, wb, 0.0)
                ws.append(wb.astype(BF16))
                gs.append(wb * dw[:, sls[b]])
                run = run + pins[b][1]
            gexs = [_dot_tri1(gs[b], tri_ex) for b in range(nsub)]
            beta = jnp.exp(lb)
            parts = []
            for b in range(nsub):
                bt = beta[:, sls[b]]
                dzb = gs[b] * (1.0 - bt) - (gsum + gexs[b][0]) * bt
                if b == 0:
                    dzb = jnp.where(first_valid, dzb, 0.0)
                if mask_last(b):
                    dzb = jnp.where(last_valid, dzb, 0.0)
                parts.append(dzb.astype(BF16))
                gsum = gsum + gexs[b][1]
            dz, w = cat(parts), cat(ws)
            dk_acc[pl.ds(off, width), :] += _dot(dz, qm, 0, 0)
            dv_acc[pl.ds(off, width), :] += _dot(w, dom, 0, 0)
            return run, gsum, dq + _dot(dz, kb)

        diag = lane < qrow
        zero = jnp.zeros((2 * BLK, BLK), F32)
        top = i - SB_FIRST

        def walk(n):
            def body(state):
                off = pl.multiple_of(state[0] * BLK, BLK)
                return (state[0] + n, *chunk(off, n, None, state[1:]))

            return (lambda state: state[0] + (n - 1) <= top), body

        state = (jnp.max(tlv[:, 1:2]).astype(jnp.int32), zero, zero, zero)
        state = lax.while_loop(*walk(SB_GROUP), state)
        carry = lax.while_loop(*walk(1), state)[1:]
        before = jnp.minimum(i, SB_FIRST - 1)
        first_off = pl.multiple_of((i - before) * BLK, BLK)
        dq = lax.switch(before, [functools.partial(chunk, first_off, n, diag) for n in range(1, SB_FIRST + 1)],
                        carry)[2]
        dq_ref[...] = (jnp.where(lane[:BLK] < HEAD_DIM, dq[:BLK], dq[BLK:]) * scale).astype(BF16)

        @pl.when(i == nb - 1)
        def _():
            dk_ref[...] = dk_acc[...].astype(BF16)
            dv_ref[...] = dv_acc[...].astype(BF16)

    qc, kc, vc = C_Q // BLK, C_K // BLK, C_V // BLK
    blk = pl.BlockSpec((BLK, BLK), lambda p, i: (i, p))
    full = pl.BlockSpec((lp, BLK), lambda p, i: (0, p))
    w = N_HEADS * HEAD_DIM
    res = pl.pallas_call(
        body, name="sb_bwd", grid=(npair, nb),
        in_specs=[pl.BlockSpec((BLK, BLK), lambda p, i: (i, qc + p)),
                  pl.BlockSpec((lp, BLK), lambda p, i: (0, kc + p)),
                  pl.BlockSpec((lp, BLK), lambda p, i: (0, vc + p)),
                  blk, blk] + [ANY] * ns,
        out_specs=[blk, full, full] + [ANY] * nl,
        out_shape=[jax.ShapeDtypeStruct((lp, w), BF16)] * 3 + list(ride[1]),
        scratch_shapes=[pltpu.VMEM((lp, BLK), F32), pltpu.VMEM((lp, BLK), F32)] + _ride_scratch(ride),
        compiler_params=_cparams(("arbitrary", "arbitrary")))(proj, proj, proj, tl, do, *ride[0])
    return res[0], res[1], res[2], list(res[3:])


def _log1p(e):
    u = 1.0 + e
    return jnp.where(u == 1.0, e, jnp.log(u) * e / jnp.where(u == 1.0, 1.0, u - 1.0))


def _ssd_common(c, dtr, bias, alog):
    row = _iota((BLK, BLK), 0)
    lane = _iota((BLK, BLK), 1)
    live = ((c * BLK + row) >= PAD) & (lane < N_HEADS)
    pre = dtr + bias
    dt = jnp.where(live, jnp.maximum(pre, 0.0) + _log1p(jnp.exp(-jnp.abs(pre))), 0.0)
    a_neg = -jnp.exp(alog)
    a = dt * a_neg
    t_in = (lane <= row).astype(BF16)
    cs = _dot_sel_l(t_in, a)
    cs_t = cs.T
    cs_end = cs[BLK - 1:BLK, :]
    e = jnp.exp(cs)
    f = jnp.exp(cs_end - cs)
    xp = ((_iota((BLK, SSD_INNER), 1) // HEAD_DIM) == _iota((BLK, SSD_INNER), 0)).astype(BF16)
    xp_t = ((_iota((SSD_INNER, BLK), 0) // HEAD_DIM) == _iota((SSD_INNER, BLK), 1)).astype(BF16)
    decay_col = _dot_sel_l(xp_t, jnp.exp(cs_t))[:, BLK - 1:BLK]
    return dict(live=live, pre=pre, dt=dt, a_neg=a_neg, cs=cs, cs_t=cs_t, e=e, f=f, xp=xp, xp_t=xp_t,
                decay_col=decay_col, row=row, lane=lane,
                dt_x=_dot_sel_r(dt, xp), e_x=_dot_sel_r(e, xp), f_x=_dot_sel_r(f, xp))


def _ssd_ldec(q, h):
    diff = q["cs"][:, h:h + 1] - q["cs_t"][h:h + 1, :]
    causal = q["row"] >= q["lane"]
    return jnp.where(causal, jnp.exp(jnp.where(causal, diff, 0.0)), 0.0)


def _ssd_fwd_call(xbc, proj, bias, alog, d_x, norm_g):
    lp = xbc.shape[0]
    nb = lp // BLK
    gw = SSD_INNER // SSD_GROUPS
    ppg = gw // BLK

    def body(xbc_ref, dtr_ref, z_ref, bias_ref, alog_ref, dx_ref, ng_ref, yb_ref, ypre_ref, sprev_ref, s_ref):
        c = pl.program_id(0)

        @pl.when(c == 0)
        def _():
            s_ref[...] = jnp.zeros_like(s_ref)

        q = _ssd_common(c, dtr_ref[...], bias_ref[...], alog_ref[...])
        x = xbc_ref[:, 0:SSD_INNER]
        xd = x * q["dt_x"]
        low = q["lane"] < HEAD_DIM
        s_old = s_ref[...]
        sprev_ref[...] = s_old
        xdf = (xd * q["f_x"]).astype(BF16)
        for g in range(SSD_GROUPS):
            bg = xbc_ref[:, SSD_INNER + g * SSD_STATE:SSD_INNER + (g + 1) * SSD_STATE].astype(BF16)
            cg = xbc_ref[:, SSD_INNER + (SSD_GROUPS + g) * SSD_STATE:
                         SSD_INNER + (SSD_GROUPS + g + 1) * SSD_STATE].astype(BF16)
            cb = _dot(cg, bg, 1, 1)
            gs = slice(g * gw, (g + 1) * gw)
            y_off = _dot(cg, s_old[gs, :].astype(BF16), 1, 1) * q["e_x"][:, gs]
            s_ref[gs, :] = s_old[gs, :] * q["decay_col"][gs, :] + _dot(xdf[:, gs], bg, 0, 0)
            for pr in range(ppg):
                cols = slice(g * gw + pr * BLK, g * gw + (pr + 1) * BLK)
                xd_p = xd[:, cols]
                acc = y_off[:, pr * BLK:(pr + 1) * BLK]
                for hh in range(2):
                    h = (g * gw + pr * BLK) // HEAD_DIM + hh
                    m = (cb * _ssd_ldec(q, h)).astype(BF16)
                    xm = jnp.where(low, xd_p, 0.0) if hh == 0 else jnp.where(low, 0.0, xd_p)
                    acc = acc + _dot(m, xm.astype(BF16))
                ypre_ref[:, cols] = acc
        ypre = ypre_ref[...] + x * dx_ref[...]
        ypre_ref[...] = ypre
        z = z_ref[...]
        yg = ypre * (z * _sigmoid(z))
        _, yh = _rms_stats(yg)
        yb_ref[...] = (yh * ng_ref[...]).astype(BF16)

    row = lambda w, col: pl.BlockSpec((BLK, w), lambda c: (c, col))
    vec = lambda w: pl.BlockSpec((1, w), lambda c: (0, 0))
    return pl.pallas_call(
        body, name="ssd_fwd", grid=(nb,),
        in_specs=[row(XBC, 0), row(BLK, C_DT // BLK), row(SSD_INNER, 0), vec(BLK), vec(BLK),
                  vec(SSD_INNER), vec(SSD_INNER)],
        out_specs=[row(SSD_INNER, 0), row(SSD_INNER, 0),
                   pl.BlockSpec((None, SSD_INNER, SSD_STATE), lambda c: (c, 0, 0))],
        out_shape=[jax.ShapeDtypeStruct((lp, 2 * SSD_INNER), BF16), jax.ShapeDtypeStruct((lp, SSD_INNER), F32),
                   jax.ShapeDtypeStruct((nb, SSD_INNER, SSD_STATE), F32)],
        scratch_shapes=[pltpu.VMEM((SSD_INNER, SSD_STATE), F32)],
        compiler_params=_cparams(("arbitrary",)))(xbc, proj, proj, bias, alog, d_x, norm_g)


def _ssd_bwd_call(dycat, ypre, xbc, proj, sprev, bias, alog, d_x, norm_g):
    lp = xbc.shape[0]
    nb = lp // BLK
    gw = SSD_INNER // SSD_GROUPS
    ppg = gw // BLK

    def body(dy_ref, ypre_ref, xbc_ref, dtr_ref, z_ref, sp_ref, bias_ref, alog_ref, dxp_ref, ng_ref,
             dz_ref, dxbc_ref, ddt_ref, dng_ref, dd_ref, dal_ref, dbi_ref, ds_ref, dxd_ref):
        step = pl.program_id(0)
        c = nb - 1 - step

        @pl.when(step == 0)
        def _():
            ds_ref[...] = jnp.zeros_like(ds_ref)

        q = _ssd_common(c, dtr_ref[...], bias_ref[...], alog_ref[...])
        row, lane = q["row"], q["lane"]
        low = lane < HEAD_DIM
        rowlive = ((c * BLK + _iota((BLK, 1), 0)) >= PAD)
        x = xbc_ref[:, 0:SSD_INNER]
        xd = x * q["dt_x"]
        z = z_ref[...]
        sz = _sigmoid(z)
        silu = z * sz
        ypre = ypre_ref[...]
        dyg, dng = _rms_bwd(ypre * silu, ng_ref[...], dy_ref[...])
        _acc_rows(dng_ref, dng, step)
        dyp = dyg * silu
        dz_ref[...] = jnp.where(rowlive, dyg * ypre * (sz * (1.0 + z * (1.0 - sz))), 0.0).astype(BF16)
        _acc_rows(dd_ref, jnp.sum(dyp * x, axis=0, keepdims=True), step)
        dye = dyp * q["e_x"]
        xdf = xd * q["f_x"]
        s_prev = sp_ref[...]
        ds_old = ds_ref[...]
        qrow = jnp.zeros((BLK, BLK), F32)
        qcol_t = jnp.zeros((BLK, BLK), F32)
        red_e = []
        red_f = []
        for g in range(SSD_GROUPS):
            gs = slice(g * gw, (g + 1) * gw)
            bsl = slice(SSD_INNER + g * SSD_STATE, SSD_INNER + (g + 1) * SSD_STATE)
            csl = slice(SSD_INNER + (SSD_GROUPS + g) * SSD_STATE, SSD_INNER + (SSD_GROUPS + g + 1) * SSD_STATE)
            bg = xbc_ref[:, bsl].astype(BF16)
            cg = xbc_ref[:, csl].astype(BF16)
            sg = s_prev[gs, :].astype(BF16)
            dsg = ds_old[gs, :].astype(BF16)
            cb = _dot(cg, bg, 1, 1)
            bds = _dot(bg, dsg, 1, 1)
            y_off = _dot(cg, sg, 1, 1) * q["e_x"][:, gs]
            red_e.append(dyp[:, gs] * y_off)
            red_f.append(xd[:, gs] * bds * q["f_x"][:, gs])
            dc = _dot(dye[:, gs].astype(BF16), sg)
            db = _dot(xdf[:, gs].astype(BF16), dsg)
            ds_ref[gs, :] = ds_old[gs, :] * q["decay_col"][gs, :] + _dot(dye[:, gs].astype(BF16), cg, 0, 0)
            dcb = jnp.zeros((BLK, BLK), F32)
            for pr in range(ppg):
                cols = slice(g * gw + pr * BLK, g * gw + (pr + 1) * BLK)
                xd_p = xd[:, cols].astype(BF16)
                dy_p = dyp[:, cols]
                acc = q["f_x"][:, cols] * bds[:, pr * BLK:(pr + 1) * BLK]
                for hh in range(2):
                    h = (g * gw + pr * BLK) // HEAD_DIM + hh
                    ld = _ssd_ldec(q, h)
                    m = cb * ld
                    dym = (jnp.where(low, dy_p, 0.0) if hh == 0 else jnp.where(low, 0.0, dy_p)).astype(BF16)
                    dm = jnp.where(row >= lane, _dot(dym, xd_p, 1, 1), 0.0)
                    acc = acc + _dot(m.astype(BF16), dym, 0, 0)
                    dcb = dcb + dm * ld
                    qq = dm * m
                    qrow = qrow + jnp.where(lane == h, jnp.sum(qq, axis=1, keepdims=True), 0.0)
                    qcol_t = qcol_t + jnp.where(row == h, jnp.sum(qq, axis=0, keepdims=True), 0.0)
                dxd_ref[:, cols] = acc
            dcbb = dcb.astype(BF16)
            dxbc_ref[:, bsl] = jnp.where(rowlive, db + _dot(dcbb, cg, 0, 0), 0.0)
            dxbc_ref[:, csl] = jnp.where(rowlive, dc + _dot(dcbb, bg), 0.0)
        dxd = dxd_ref[...]
        dxbc_ref[:, 0:SSD_INNER] = jnp.where(rowlive, dxd * q["dt_x"] + dyp * dxp_ref[...], 0.0)
        xp_t = q["xp_t"]
        fw = _dot_sel_r(jnp.concatenate(red_f, axis=1), xp_t)
        dcs = qrow - qcol_t.T + _dot_sel_r(jnp.concatenate(red_e, axis=1), xp_t) - fw
        end_f = jnp.sum(fw, axis=0, keepdims=True)
        sds = jnp.sum(ds_old * s_prev, axis=1, keepdims=True)
        per_head = _dot_sel_l(q["xp"], jnp.broadcast_to(sds, (SSD_INNER, BLK)))
        end_e = per_head.T[0:1, :] * jnp.exp(q["cs"][BLK - 1:BLK, :])
        dcs = dcs + jnp.where(row == BLK - 1, end_f + end_e, 0.0)
        t_up = (lane >= row).astype(BF16)
        da = _dot_sel_l(t_up, dcs)
        ddt = da * q["a_neg"] + _dot_sel_r(dxd * x, xp_t)
        _acc_rows(dal_ref, jnp.sum(da * q["dt"] * q["a_neg"], axis=0, keepdims=True), step)
        ddtr = jnp.where(q["live"], ddt * _sigmoid(q["pre"]), 0.0)
        ddt_ref[...] = ddtr.astype(BF16)
        _acc_rows(dbi_ref, jnp.sum(ddtr, axis=0, keepdims=True), step)

    row_s = lambda w, col: pl.BlockSpec((BLK, w), lambda s: (nb - 1 - s, col))
    vec = lambda w: pl.BlockSpec((1, w), lambda s: (0, 0))
    return pl.pallas_call(
        body, name="ssd_bwd", grid=(nb,),
        in_specs=[row_s(SSD_INNER, 0), row_s(SSD_INNER, 0), row_s(XBC, 0), row_s(BLK, C_DT // BLK),
                  row_s(SSD_INNER, 0), pl.BlockSpec((None, SSD_INNER, SSD_STATE), lambda s: (nb - 1 - s, 0, 0)),
                  vec(BLK), vec(BLK), vec(SSD_INNER), vec(SSD_INNER)],
        out_specs=[row_s(SSD_INNER, 0), row_s(XBC, 0), row_s(BLK, 0),
                   vec(SSD_INNER), vec(SSD_INNER), vec(BLK), vec(BLK)],
        out_shape=[jax.ShapeDtypeStruct((lp, SSD_INNER), BF16), jax.ShapeDtypeStruct((lp, XBC), F32),
                   jax.ShapeDtypeStruct((lp, BLK), BF16),
                   jax.ShapeDtypeStruct((1, SSD_INNER), F32), jax.ShapeDtypeStruct((1, SSD_INNER), F32),
                   jax.ShapeDtypeStruct((1, BLK), F32), jax.ShapeDtypeStruct((1, BLK), F32)],
        scratch_shapes=[pltpu.VMEM((SSD_INNER, SSD_STATE), F32), pltpu.VMEM((BLK, SSD_INNER), F32)],
        compiler_params=_cparams(("arbitrary",)))(dycat, ypre, xbc, proj, proj, sprev, bias, alog, d_x, norm_g)


def _pad_rows8(w):
    return jnp.pad(w, ((0, 8 - w.shape[0]), (0, 0)))


def _pad_lanes(v, n=BLK):
    return jnp.pad(v, ((0, 0), (0, n - v.shape[1])))


def _local_step(x, target, wt, late_shards, late_weights, w_in_shards):
    seq = x.shape[0]
    lp = seq + BLK
    tm = _pick(lp, [1408, 768, 384, 128])
    tkr = _pick(lp, [1408, 384, 128])
    rbc = _pick(lp, [384, 128])
    h0 = jnp.concatenate([jnp.zeros((PAD, D_MODEL), F32), wt["meta"], x], axis=0)
    bias = _pad_lanes(wt["ssd_dt_bias"])
    alog = _pad_lanes(wt["ssd_a_log"])
    d_x = jnp.repeat(wt["ssd_d"], HEAD_DIM, axis=1)
    cw8 = _pad_rows8(wt["ssd_conv_w"])
    fw8 = _pad_rows8(wt["ffn_conv_w"])
    fcw = D_FF // 2

    xn1 = _rms_fwd_call(h0, wt["mix_pre_g"], "norm1")
    proj, late_a = _mm(xn1, wt["w_in"], tm=tm, tn=1152, tk=D_MODEL, ride=_gather_ride(late_shards[0:1]),
                       name="mm_proj")
    proj = proj[0]
    conv_pre, xbc = _conv_fwd_call(proj, C_XBC, XBC, 512, cw8, wt["ssd_conv_b"], 4, rb=rbc, name="ssd_conv_fwd")
    y_ssd, ypre, sprev = _ssd_fwd_call(xbc, proj, bias, alog, d_x, wt["ssd_norm_g"])
    o, tl, late_b = _sb_fwd_call(proj, _gather_ride(late_shards[1:3]))
    ycat = _rms_fwd_call(o, wt["sb_norm_g"], "sb_norm", beside=y_ssd)
    w_out, w_up, w_down = late_weights([late_a[0], late_b[0], late_b[1]])
    mix = _mm(ycat, w_out, tm=tm, tn=1024, tk=2048, name="mm_mix")[0]
    h1, xn2 = _mid_fwd_call(h0, mix, wt["mix_post_g"], wt["ffn_pre_g"])
    gu = _mm(xn2, w_up, tm=tm, tn=1408, tk=D_MODEL, name="mm_up")[0]
    gpre, act = _conv_fwd_call(gu, 0, D_FF, fcw, fw8, wt["ffn_conv_b"], 3, gate_src=gu, gate_col0=D_FF,
                               rb=rbc, name="ffn_conv_fwd")
    f = _mm(act, w_down, tm=tm, tn=1024, tk=1408, name="mm_down")[0]
    loss_row, df, dh2, dg_ffn_post = _final_call(h1, f, wt["ffn_post_g"], target)

    dact = _mm(df, w_down, tb=True, tm=tm, tn=1408, tk=D_MODEL, name="mm_dact")[0]
    dw_down, dw_down_b = _mm(act, df, ta=True, tm=1408, tn=1024, tk=tkr, extra_bf16=True, name="mm_dw_down")
    by_chip = lambda g: g.reshape(N_CHIPS, -1, D_MODEL)
    dgate, dup, dfcw, dfcb = _conv_bwd_call(gu, 0, D_FF, fcw, fw8, 3, gpre, dact, gate_src=gu, gate_col0=D_FF,
                                            rb=rbc, name="ffn_conv_bwd")
    dgu = jnp.concatenate([dgate, dup], axis=1)
    dxn2, land_down = _mm(dgu, w_up, tb=True, tm=tm, tn=1024, tk=1408, ride=_scatter_ride(by_chip(dw_down_b)),
                          name="mm_dxn2")
    dw_up, dw_up_b = _mm(xn2, dgu, ta=True, tm=1024, tn=1408, tk=tkr, nsplit=N_CHIPS, extra_bf16=True,
                         name="mm_dw_up")
    dh1, dmix, dg_ffn_pre, dg_mix_post = _mid_bwd_call(dh2, h1, dxn2[0], mix, wt["ffn_pre_g"], wt["mix_post_g"])
    dycat = _mm(dmix, w_out, tb=True, tm=tm, tn=1024, tk=D_MODEL, name="mm_dycat")[0]
    dw_out, dw_out_b = _mm(ycat, dmix, ta=True, tm=1024, tn=1024, tk=tkr, extra_bf16=True, name="mm_dw_out")
    do, dg_sb = _norm_bwd_call(o, wt["sb_norm_g"], dycat, 1, "sb_norm_bwd")
    dq, dk, dv, lands = _sb_bwd_call(proj, tl, do, _join_rides(_scatter_ride(dw_up_b),
                                                                  _scatter_ride(by_chip(dw_out_b))))
    dz, dxbc_act, ddt, dg_ssd, dd_x, dalog, dbias = _ssd_bwd_call(
        dycat, ypre, xbc, proj, sprev, bias, alog, d_x, wt["ssd_norm_g"])
    dxbc, dcw, dcb = _conv_bwd_call(proj, C_XBC, XBC, 512, cw8, 4, conv_pre, dxbc_act, rb=rbc,
                                    name="ssd_conv_bwd")
    dproj = jnp.concatenate([dz, dxbc, ddt, dq, dk, dv], axis=1)
    dw_in, dw_in_b = w_in_shards(_mm(xn1, dproj, ta=True, tm=1024, tn=1152, tk=tkr, name="mm_dw_in")[0])
    dxn1, land_in = _mm(dproj, wt["w_in"], tb=True, tm=tm, tn=1024, tk=1152, ride=_scatter_ride(dw_in_b),
                        name="mm_dxn1")
    dh0, dg_pre = _norm_bwd_call(h0, wt["mix_pre_g"], dxn1[0], 0, "norm1_bwd", res=dh1)

    small = {
        "meta_tokens": dh0[PAD:BLK], "mix_pre_g": dg_pre, "ssd_conv_w": dcw[:4], "ssd_conv_b": dcb,
        "ssd_dt_bias": dbias[:, :N_HEADS], "ssd_a_log": dalog[:, :N_HEADS],
        "ssd_d": jnp.sum(dd_x.reshape(N_HEADS, HEAD_DIM), axis=1)[None],
        "ssd_norm_g": dg_ssd, "sb_norm_g": dg_sb, "mix_post_g": dg_mix_post, "ffn_pre_g": dg_ffn_pre,
        "ffn_conv_w": dfcw[:3], "ffn_conv_b": dfcb, "ffn_post_g": dg_ffn_post,
    }
    pending = {"w_in": (dw_in, land_in[0]), "w_out": (by_chip(dw_out), lands[1]), "w_up": (dw_up, lands[0]),
               "w_down": (by_chip(dw_down), land_down[0])}
    return loss_row, dh0[BLK:], small, pending


def _adamw_call(w, g, m, v, name):
    rows, cols = w.shape
    tr = 256 if rows % 256 == 0 else (352 if rows % 352 == 0 else rows)
    c1 = 1.0 - ADAM_B1 ** ADAM_STEP
    c2 = 1.0 - ADAM_B2 ** ADAM_STEP

    def body(w_ref, g_ref, m_ref, v_ref, d_ref, mo_ref, vo_ref):
        gv = g_ref[...]
        m2 = ADAM_B1 * m_ref[...] + (1.0 - ADAM_B1) * gv
        v2 = ADAM_B2 * v_ref[...] + (1.0 - ADAM_B2) * (gv * gv)
        d_ref[...] = -ADAM_LR * ((m2 / c1) / (jnp.sqrt(v2 / c2) + ADAM_EPS) + ADAM_WD * w_ref[...])
        mo_ref[...] = m2
        vo_ref[...] = v2

    spec = pl.BlockSpec((tr, cols), lambda i: (i, 0))
    return pl.pallas_call(
        body, name=name, grid=(rows // tr,), in_specs=[spec] * 4, out_specs=[spec] * 3,
        out_shape=[jax.ShapeDtypeStruct((rows, cols), F32)] * 3,
        compiler_params=_cparams(("parallel",)))(w, g, m, v)


ANY = pl.BlockSpec(memory_space=pl.ANY)


def _place():
    x, y, c = lax.axis_index("x"), lax.axis_index("y"), lax.axis_index("c")
    chips = [(1 - x, y), (x, 1 - y), (1 - x, 1 - y)]
    return x, y, c, chips


def _half(c, h):
    return pl.ds(pl.multiple_of(c * h, 8), h)


def _allgather_call(shards):
    n = len(shards)

    def body(*refs):
        ins, outs = refs[:n], refs[n:2 * n]
        send_i, recv_i, send_d, recv_d = refs[2 * n:]
        x, y, c, chips = _place()
        me = 2 * x + y
        sends = []
        for a in range(n):
            h = shards[a].shape[0] // 2
            for j, chip in enumerate(chips):
                cp = pltpu.make_async_remote_copy(
                    src_ref=ins[a].at[_half(c, h)], dst_ref=outs[a].at[me, _half(c, h)],
                    send_sem=send_i.at[3 * a + j], recv_sem=recv_i.at[3 * a + j],
                    device_id=(*chip, c), device_id_type=MESH)
                cp.start()
                sends.append(cp)
        for a in range(n):
            h = shards[a].shape[0] // 2
            for j, chip in enumerate(chips):
                src = 2 * chip[0] + chip[1]
                landed = outs[a].at[src, _half(c, h)]
                pltpu.make_async_remote_copy(
                    src_ref=landed, dst_ref=landed, send_sem=send_i.at[3 * a + j], recv_sem=recv_i.at[3 * a + j],
                    device_id=(*chip, c), device_id_type=MESH).wait_recv()
                cp = pltpu.make_async_remote_copy(
                    src_ref=landed, dst_ref=landed, send_sem=send_d.at[3 * a + j], recv_sem=recv_d.at[3 * a + j],
                    device_id=(x, y, 1 - c), device_id_type=MESH)
                cp.start()
                sends.append(cp)
        for a in range(n):
            h = shards[a].shape[0] // 2
            for j, chip in enumerate(chips):
                src = 2 * chip[0] + chip[1]
                other = outs[a].at[src, _half(1 - c, h)]
                pltpu.make_async_remote_copy(
                    src_ref=other, dst_ref=other, send_sem=send_d.at[3 * a + j], recv_sem=recv_d.at[3 * a + j],
                    device_id=(x, y, 1 - c), device_id_type=MESH).wait_recv()
        for cp in sends:
            cp.wait_send()

    return pl.pallas_call(
        body, name="allgather_weights", in_specs=[ANY] * n, out_specs=[ANY] * n,
        out_shape=[jax.ShapeDtypeStruct((N_CHIPS,) + s.shape, s.dtype) for s in shards],
        scratch_shapes=[pltpu.SemaphoreType.DMA((3 * n,))] * 4,
    )(*shards)


def _ride_scratch(ride):
    return [pltpu.SemaphoreType.DMA((ride[3],)), pltpu.SemaphoreType.DMA((ride[3],))]


def _ride_run(ride, src_refs, land_refs, send, recv, first, last):
    plan = ride[2]

    @pl.when(first)
    def _():
        for k, (src, dst, _, dev) in enumerate(plan(src_refs, land_refs)):
            pltpu.make_async_remote_copy(src_ref=src, dst_ref=dst, send_sem=send.at[k], recv_sem=recv.at[k],
                                         device_id=dev, device_id_type=MESH).start()

    @pl.when(last)
    def _():
        for k, (src, _, land, dev) in enumerate(plan(src_refs, land_refs)):
            cp = pltpu.make_async_remote_copy(src_ref=src, dst_ref=land, send_sem=send.at[k], recv_sem=recv.at[k],
                                              device_id=dev, device_id_type=MESH)
            cp.wait_send()
            cp.wait_recv()


def _join_rides(r1, r2):
    n1, l1 = len(r1[0]), len(r1[1])

    def plan(srcs, lands):
        return r1[2](srcs[:n1], lands[:l1]) + r2[2](srcs[n1:], lands[l1:])

    return (r1[0] + r2[0], r1[1] + r2[1], plan, r1[3] + r2[3])


def _gather_ride(shards):
    return (list(shards), [jax.ShapeDtypeStruct((N_CHIPS,) + s.shape, s.dtype) for s in shards],
            _gather_plan(len(shards)), 3 * len(shards))


def _scatter_ride(g_b):
    h = g_b.shape[1] // 2
    return ([g_b], [jax.ShapeDtypeStruct((8, h, g_b.shape[2]), BF16)], _scatter_plan(h), 7)


def _gather_plan(n):
    def plan(srcs, lands):
        x, y, c, chips = _place()
        me = 2 * x + y
        return [(srcs[a], lands[a].at[me], lands[a].at[2 * chip[0] + chip[1]], (*chip, c))
                for a in range(n) for chip in chips]
    return plan


def _scatter_plan(h):
    def plan(srcs, lands):
        x, y, c, _ = _place()
        me = 4 * x + 2 * y + c
        out = []
        for p in range(1, 8):
            px = 1 - x if p & 4 else x
            py = 1 - y if p & 2 else y
            pc = 1 - c if p & 1 else c
            out.append((srcs[0].at[2 * px + py, _half(pc, h)], lands[0].at[me],
                        lands[0].at[4 * px + 2 * py + pc], (px, py, pc)))
        return out
    return plan


def _grad_sum_call(own, land, place, name):
    _, h, cols = land.shape
    th = _pick(h, [256, 176, 8])
    nt = h // th

    def body(p_ref, own_ref, *refs):
        acc = own_ref[...]
        for r in refs[:7]:
            acc = acc + r[...].astype(F32)
        refs[7][...] = acc

    def peer(k):
        return pl.BlockSpec((None, th, cols), lambda i, p_ref: (p_ref[2 + k], i, 0))

    return pl.pallas_call(
        body, name=name,
        grid_spec=pltpu.PrefetchScalarGridSpec(
            num_scalar_prefetch=1, grid=(nt,),
            in_specs=[pl.BlockSpec((None, th, cols), lambda i, p_ref: (p_ref[1], p_ref[0] * nt + i, 0))]
            + [peer(k) for k in range(7)],
            out_specs=pl.BlockSpec((th, cols), lambda i, p_ref: (p_ref[0] * nt + i, 0))),
        out_shape=jax.ShapeDtypeStruct((2 * h, cols), F32),
        compiler_params=_cparams(("parallel",)))(place, own, *[land] * 7)


def _half_exchange_call(shards):
    n = len(shards)

    def body(*refs):
        outs = refs[n:2 * n]
        send_d, recv_d = refs[2 * n:]
        x, y, c, _ = _place()
        cps = []
        for a in range(n):
            h = shards[a].shape[0] // 2
            mine = outs[a].at[_half(c, h)]
            cp = pltpu.make_async_remote_copy(
                src_ref=mine, dst_ref=mine, send_sem=send_d.at[a], recv_sem=recv_d.at[a],
                device_id=(x, y, 1 - c), device_id_type=MESH)
            cp.start()
            cps.append(cp)
        for a, cp in enumerate(cps):
            h = shards[a].shape[0] // 2
            theirs = outs[a].at[_half(1 - c, h)]
            pltpu.make_async_remote_copy(
                src_ref=theirs, dst_ref=theirs, send_sem=send_d.at[a], recv_sem=recv_d.at[a],
                device_id=(x, y, 1 - c), device_id_type=MESH).wait_recv()
            cp.wait_send()

    return pl.pallas_call(
        body, name="grad_half_exchange", in_specs=[ANY] * n, out_specs=[ANY] * n,
        out_shape=[jax.ShapeDtypeStruct(sv.shape, F32) for sv in shards],
        input_output_aliases={a: a for a in range(n)},
        scratch_shapes=[pltpu.SemaphoreType.DMA((n,))] * 2,
    )(*shards)


def _allreduce_small_call(arrs):
    n = len(arrs)
    offs, rows = [], 0
    for a in arrs:
        offs.append(rows)
        rows += a.shape[0]
    rows = -(-rows // 8) * 8
    width = -(-max(a.shape[1] for a in arrs) // BLK) * BLK

    def body(*refs):
        ins, outs = refs[:n], refs[n:2 * n]
        gath, send_sems, recv_sems = refs[2 * n:]
        x, y, c, chips = _place()
        me, sibling = (x, y, c), (x, y, 1 - c)

        def slot(px, py, pc):
            return gath.at[4 * px + 2 * py + pc]

        def copy(k, block, to):
            return pltpu.make_async_remote_copy(
                src_ref=slot(*block), dst_ref=slot(*block),
                send_sem=send_sems.at[k], recv_sem=recv_sems.at[k], device_id=to, device_id_type=MESH)

        mine = slot(*me)
        mine[...] = jnp.zeros((rows, width), F32)
        for k in range(n):
            r, w = arrs[k].shape
            mine[offs[k]:offs[k] + r, 0:w] = ins[k][...]
        first = [copy(0, me, sibling)]
        first += [copy(1 + j, me, (*chip, c)) for j, chip in enumerate(chips)]
        for cp in first:
            cp.start()
        passed = [copy(4 + j, (*chip, c), sibling) for j, chip in enumerate(chips)]
        for j, chip in enumerate(chips):
            copy(1 + j, (*chip, c), me).wait_recv()
            passed[j].start()
        copy(0, sibling, me).wait_recv()
        for j, chip in enumerate(chips):
            copy(4 + j, (*chip, 1 - c), me).wait_recv()
        for cp in first + passed:
            cp.wait_send()
        acc = gath[0]
        for d in range(1, 8):
            acc = acc + gath[d]
        for k in range(n):
            r, w = arrs[k].shape
            outs[k][...] = acc[offs[k]:offs[k] + r, 0:w]

    vm = pl.BlockSpec(memory_space=pltpu.VMEM)
    return pl.pallas_call(
        body, name="allreduce_small", in_specs=[vm] * n, out_specs=[vm] * n,
        out_shape=[jax.ShapeDtypeStruct(a.shape, F32) for a in arrs],
        scratch_shapes=[pltpu.VMEM((8, rows, width), F32), pltpu.SemaphoreType.DMA((7,)),
                        pltpu.SemaphoreType.DMA((7,))],
        compiler_params=pltpu.CompilerParams(vmem_limit_bytes=VMEM_LIMIT),
    )(*arrs)


def _adamw_small_call(ws, gs, ms, vs):
    n = len(ws)
    c1 = 1.0 - ADAM_B1 ** ADAM_STEP
    c2 = 1.0 - ADAM_B2 ** ADAM_STEP

    def body(*refs):
        for k in range(n):
            w_ref, g_ref, m_ref, v_ref = (refs[j * n + k] for j in range(4))
            d_ref, mo_ref, vo_ref = (refs[(4 + j) * n + k] for j in range(3))
            gv = g_ref[...]
            m2 = ADAM_B1 * m_ref[...] + (1.0 - ADAM_B1) * gv
            v2 = ADAM_B2 * v_ref[...] + (1.0 - ADAM_B2) * (gv * gv)
            d_ref[...] = -ADAM_LR * ((m2 / c1) / (jnp.sqrt(v2 / c2) + ADAM_EPS) + ADAM_WD * w_ref[...])
            mo_ref[...] = m2
            vo_ref[...] = v2

    vm = pl.BlockSpec(memory_space=pltpu.VMEM)
    res = pl.pallas_call(
        body, name="adamw_small", in_specs=[vm] * (4 * n), out_specs=[vm] * (3 * n),
        out_shape=[jax.ShapeDtypeStruct(a.shape, F32) for a in ws] * 3,
        compiler_params=pltpu.CompilerParams(vmem_limit_bytes=VMEM_LIMIT),
    )(*ws, *gs, *ms, *vs)
    return res[:n], res[n:2 * n], res[2 * n:]


def _pack(arrs, min_rows=8):
    parts = []
    for a in arrs:
        flat = a.reshape(-1).astype(F32)
        parts.append(jnp.pad(flat, (0, (-flat.shape[0]) % BLK)))
    buf = jnp.concatenate(parts).reshape(-1, BLK)
    return jnp.pad(buf, ((0, (-buf.shape[0]) % min_rows), (0, 0)))


def _unpack(buf, shapes):
    out, r = [], 0
    for shp in shapes:
        n = math.prod(shp)
        nr = -(-n // BLK)
        out.append(buf[r:r + nr].reshape(-1)[:n].reshape(shp))
        r += nr
    return out


SMALL = ["meta_tokens", "mix_pre_g", "ssd_conv_w", "ssd_conv_b", "ssd_dt_bias", "ssd_a_log", "ssd_d", "ssd_norm_g",
         "sb_norm_g", "mix_post_g", "ffn_pre_g", "ffn_conv_w", "ffn_conv_b", "ffn_post_g"]
BIG = ["w_in", "w_out", "w_up", "w_down"]
WEIGHTS = ["meta_tokens", "mix_pre_g", "w_in", "ssd_conv_w", "ssd_conv_b", "ssd_dt_bias", "ssd_a_log", "ssd_d",
           "ssd_norm_g", "sb_norm_g", "w_out", "mix_post_g", "ffn_pre_g", "w_up", "ffn_conv_w", "ffn_conv_b",
           "w_down", "ffn_post_g"]
W_IN_SHARD = IN_COLS // N_CHIPS
W_IN_PAD = 1536


def kernel(x, meta_tokens, mix_pre_g, w_in, ssd_conv_w, ssd_conv_b, ssd_dt_bias, ssd_a_log, ssd_d, ssd_norm_g, sb_norm_g, w_out, mix_post_g, ffn_pre_g, w_up, ffn_conv_w, ffn_conv_b, w_down, ffn_post_g, loss_target, m_meta_tokens, m_mix_pre_g, m_w_in, m_ssd_conv_w, m_ssd_conv_b, m_ssd_dt_bias, m_ssd_a_log, m_ssd_d, m_ssd_norm_g, m_sb_norm_g, m_w_out, m_mix_post_g, m_ffn_pre_g, m_w_up, m_ffn_conv_w, m_ffn_conv_b, m_w_down, m_ffn_post_g, v_meta_tokens, v_mix_pre_g, v_w_in, v_ssd_conv_w, v_ssd_conv_b, v_ssd_dt_bias, v_ssd_a_log, v_ssd_d, v_ssd_norm_g, v_sb_norm_g, v_w_out, v_mix_post_g, v_ffn_pre_g, v_w_up, v_ffn_conv_w, v_ffn_conv_b, v_w_down, v_ffn_post_g):
    w = dict(meta_tokens=meta_tokens, mix_pre_g=mix_pre_g, w_in=w_in, ssd_conv_w=ssd_conv_w, ssd_conv_b=ssd_conv_b, ssd_dt_bias=ssd_dt_bias, ssd_a_log=ssd_a_log, ssd_d=ssd_d, ssd_norm_g=ssd_norm_g, sb_norm_g=sb_norm_g, w_out=w_out, mix_post_g=mix_post_g, ffn_pre_g=ffn_pre_g, w_up=w_up, ffn_conv_w=ffn_conv_w, ffn_conv_b=ffn_conv_b, w_down=w_down, ffn_post_g=ffn_post_g)
    m = dict(meta_tokens=m_meta_tokens, mix_pre_g=m_mix_pre_g, w_in=m_w_in, ssd_conv_w=m_ssd_conv_w, ssd_conv_b=m_ssd_conv_b, ssd_dt_bias=m_ssd_dt_bias, ssd_a_log=m_ssd_a_log, ssd_d=m_ssd_d, ssd_norm_g=m_ssd_norm_g, sb_norm_g=m_sb_norm_g, w_out=m_w_out, mix_post_g=m_mix_post_g, ffn_pre_g=m_ffn_pre_g, w_up=m_w_up, ffn_conv_w=m_ffn_conv_w, ffn_conv_b=m_ffn_conv_b, w_down=m_w_down, ffn_post_g=m_ffn_post_g)
    v = dict(meta_tokens=v_meta_tokens, mix_pre_g=v_mix_pre_g, w_in=v_w_in, ssd_conv_w=v_ssd_conv_w, ssd_conv_b=v_ssd_conv_b, ssd_dt_bias=v_ssd_dt_bias, ssd_a_log=v_ssd_a_log, ssd_d=v_ssd_d, ssd_norm_g=v_ssd_norm_g, sb_norm_g=v_sb_norm_g, w_out=v_w_out, mix_post_g=v_mix_post_g, ffn_pre_g=v_ffn_pre_g, w_up=v_w_up, ffn_conv_w=v_ffn_conv_w, ffn_conv_b=v_ffn_conv_b, w_down=v_w_down, ffn_post_g=v_ffn_post_g)
    chip = 2 * lax.axis_index("x") + lax.axis_index("y")
    me = 2 * chip + lax.axis_index("c")
    place = jnp.stack([lax.axis_index("c"), chip] + [me ^ p for p in range(1, 8)]).astype(jnp.int32)

    shard_small = [w["meta_tokens"], w["ssd_conv_w"][0], w["ffn_conv_w"][0]]
    shards = [jnp.pad(w["w_in"][0], ((0, 0), (0, W_IN_PAD - W_IN_SHARD))).astype(BF16), _pack(shard_small, 16)]
    gathered = _allgather_call(shards)
    late_shards = [w["w_out"][0].astype(BF16), w["w_up"][0].astype(BF16), w["w_down"][0].astype(BF16)]

    def blocks(own, got):
        return [jnp.where(chip == i, own, got[i]) for i in range(N_CHIPS)]

    def late_weights(got):
        return (jnp.concatenate(blocks(late_shards[0], got[0]), axis=0),
                jnp.concatenate(blocks(late_shards[1], got[1]), axis=1),
                jnp.concatenate(blocks(late_shards[2], got[2]), axis=0))

    cut = DT_REAL_OFF + N_HEADS - W_IN_SHARD
    s_in = blocks(shards[0], gathered[0])
    w_in_c = jnp.concatenate(
        [s_in[0][:, :W_IN_SHARD], s_in[1][:, :cut], jnp.zeros((D_MODEL, BLK - N_HEADS), BF16),
         s_in[1][:, cut:W_IN_SHARD], s_in[2][:, :W_IN_SHARD], s_in[3][:, :W_IN_SHARD]], axis=1)
    parts = [_unpack(b, [s.shape for s in shard_small]) for b in blocks(shards[1], gathered[1])]
    wt = {k: w[k][0][None] if w[k].ndim == 3 else w[k] for k in
          ["mix_pre_g", "ssd_conv_b", "ssd_dt_bias", "ssd_a_log", "ssd_d", "ssd_norm_g", "sb_norm_g", "mix_post_g",
           "ffn_pre_g", "ffn_conv_b", "ffn_post_g"]}
    wt.update(
        meta=jnp.concatenate([p[0] for p in parts], axis=1),
        ssd_conv_w=jnp.concatenate([p[1] for p in parts], axis=1),
        ffn_conv_w=jnp.concatenate([p[2] for p in parts], axis=1), w_in=w_in_c)

    def w_in_shards(g):
        skip = BLK - N_HEADS
        cols = [g[:, :W_IN_SHARD],
                jnp.concatenate([g[:, W_IN_SHARD:W_IN_SHARD + cut], g[:, C_Q:2 * W_IN_SHARD + skip]], axis=1),
                g[:, 2 * W_IN_SHARD + skip:3 * W_IN_SHARD + skip], g[:, 3 * W_IN_SHARD + skip:]]
        g = jnp.stack([jnp.pad(b, ((0, 0), (0, W_IN_PAD - W_IN_SHARD))) for b in cols])
        return g, g.astype(BF16)

    loss_row, dx, small, pending = _local_step(x[0], loss_target[0], wt, late_shards, late_weights, w_in_shards)

    full = _half_exchange_call([_grad_sum_call(*pending[k], place, "grad_sum_" + k) for k in BIG])
    grads = {"w_in": full[0][:, :W_IN_SHARD], "w_out": full[1], "w_up": full[2], "w_down": full[3]}

    red_list = _allreduce_small_call([small[k] for k in SMALL] + [loss_row])
    loss = jnp.sum(red_list[-1])
    for k, g in zip(SMALL, red_list[:-1]):
        grads[k] = g
    for k in ["meta_tokens", "ssd_conv_w", "ffn_conv_w"]:
        wk = w[k].shape[-1]
        grads[k] = lax.dynamic_slice_in_dim(grads[k], chip * wk, wk, axis=1)

    delta, new_m, new_v = {}, {}, {}
    for k in BIG:
        delta[k], new_m[k], new_v[k] = _adamw_call(w[k][0], grads[k], m[k][0], v[k][0], "adamw_" + k)
    flat = lambda d: [d[k].reshape(grads[k].shape) for k in SMALL]
    res = _adamw_small_call(flat(w), [grads[k] for k in SMALL], flat(m), flat(v))
    for out, arrs in zip((delta, new_m, new_v), res):
        for k, a in zip(SMALL, arrs):
            out[k] = a

    def shaped(d, k):
        return d[k].reshape(w[k].shape)

    return (loss, dx[None], *[shaped(grads, k) for k in WEIGHTS], *[shaped(delta, k) for k in WEIGHTS],
            *[shaped(new_m, k) for k in WEIGHTS], *[shaped(new_v, k) for k in WEIGHTS])
```

```python
import functools
import math

import jax
import jax.numpy as jnp
from jax import lax
from jax.experimental import pallas as pl
from jax.experimental.pallas import tpu as pltpu

F32 = jnp.float32
BF16 = jnp.bfloat16

D_MODEL = 1024
N_META = 16
BLK = 128
PAD = BLK - N_META
HEAD_DIM = 64
N_HEADS = 16
SSD_GROUPS = 2
SSD_STATE = 128
SSD_INNER = 1024
XBC = SSD_INNER + 2 * SSD_GROUPS * SSD_STATE
D_FF = 2816
EPS = 1e-6
IN_COLS = 5648
C_Z, C_XBC, C_DT, C_Q, C_K, C_V, C_END = 0, 1024, 2560, 2688, 3712, 4736, 5760
DT_REAL_OFF = 2560
N_CHIPS = 4
ADAM_LR, ADAM_B1, ADAM_B2, ADAM_EPS, ADAM_WD, ADAM_STEP = 0.001, 0.9, 0.999, 1e-08, 0.01, 10
VMEM_LIMIT = 56 * 1024 * 1024
MESH = pl.DeviceIdType.MESH


def _cparams(sem=None, **kw):
    if sem is not None:
        kw["dimension_semantics"] = sem
    return pltpu.CompilerParams(vmem_limit_bytes=VMEM_LIMIT, **kw)


def _pick(n, cands):
    for c in cands:
        if n % c == 0:
            return c
    raise ValueError((n, cands))


def _iota(shape, dim):
    return lax.broadcasted_iota(jnp.int32, shape, dim)


def _sigmoid(x):
    return 1.0 / (1.0 + jnp.exp(-x))


def _split2(v):
    h1 = v.astype(BF16)
    return h1, (v - h1.astype(F32)).astype(BF16)


def _dot(a, b, ca=1, cb=0):
    return lax.dot_general(a, b, (((ca,), (cb,)), ((), ())), preferred_element_type=F32)


def _dot_sel_r(v, sel, cb=0):
    h1, h2 = _split2(v)
    return _dot(h1, sel, 1, cb) + _dot(h2, sel, 1, cb)


def _dot_sel_l(sel, v, ca=1):
    h1, h2 = _split2(v)
    return _dot(sel, h1, ca, 0) + _dot(sel, h2, ca, 0)


def _mm(a, b, *, ta=False, tb=False, tm, tn, tk, out_dtype=F32, nsplit=1, extra_bf16=False, ride=None, name):
    K, M = (a.shape if ta else a.shape[::-1])
    N = b.shape[0] if tb else b.shape[1]
    assert M % tm == 0 and N % tn == 0 and K % tk == 0, (name, M, N, K, tm, tn, tk)
    nm, nn, nk = M // tm, N // tn, K // tk
    assert nn % nsplit == 0
    per = nn // nsplit
    a_spec = (pl.BlockSpec((tk, tm), lambda i, j, k: (k, i)) if ta
              else pl.BlockSpec((tm, tk), lambda i, j, k: (i, k)))
    b_spec = (pl.BlockSpec((tn, tk), lambda i, j, k: (j, k)) if tb
              else pl.BlockSpec((tk, tn), lambda i, j, k: (k, j)))
    o_spec = pl.BlockSpec((None, tm, tn), lambda i, j, k: (j // per, i, j % per))
    n_out = 2 if extra_bf16 else 1
    ca, cb = (0 if ta else 1), (1 if tb else 0)
    ns, nl = (len(ride[0]), len(ride[1])) if ride else (0, 0)

    def body(a_ref, b_ref, *rest):
        outs = rest[ns:ns + n_out]
        if ride:
            step = (pl.program_id(0) * nn + pl.program_id(1)) * nk + pl.program_id(2)
            _ride_run(ride, rest[:ns], rest[ns + n_out:ns + n_out + nl], rest[-2], rest[-1],
                      step == 0, step == nm * nn * nk - 1)
        p = _dot(a_ref[...].astype(BF16), b_ref[...].astype(BF16), ca, cb)

        def emit(val):
            outs[0][...] = val.astype(out_dtype)
            if extra_bf16:
                outs[1][...] = val.astype(BF16)

        if nk == 1:
            emit(p)
        else:
            acc = rest[ns + n_out + nl]
            k = pl.program_id(2)

            @pl.when(k == 0)
            def _():
                acc[...] = p

            @pl.when(k > 0)
            def _():
                acc[...] += p

            @pl.when(k == nk - 1)
            def _():
                emit(acc[...])

    shp = (nsplit, M, N // nsplit)
    out_shape = [jax.ShapeDtypeStruct(shp, out_dtype)]
    out_specs = [o_spec]
    if extra_bf16:
        out_shape.append(jax.ShapeDtypeStruct(shp, BF16))
        out_specs.append(o_spec)
    scratch = [pltpu.VMEM((tm, tn), F32)] if nk > 1 else []
    if ride:
        res = pl.pallas_call(
            body, name=name, grid=(nm, nn, nk), in_specs=[a_spec, b_spec] + [ANY] * ns,
            out_specs=out_specs + [ANY] * nl, out_shape=out_shape + list(ride[1]),
            scratch_shapes=scratch + _ride_scratch(ride),
            compiler_params=_cparams(("arbitrary", "arbitrary", "arbitrary")),
        )(a, b, *ride[0])
        return (res[:n_out] if extra_bf16 else res[0]), list(res[n_out:])
    res = pl.pallas_call(
        body, name=name, grid=(nm, nn, nk), in_specs=[a_spec, b_spec], out_specs=out_specs,
        out_shape=out_shape, scratch_shapes=scratch,
        compiler_params=_cparams(("parallel", "parallel", "arbitrary")),
    )(a, b)
    return res if extra_bf16 else res[0]


def _rms_stats(x):
    r = lax.rsqrt(jnp.mean(x * x, axis=-1, keepdims=True) + EPS)
    return r, x * r


def _rms_bwd(x, g, dy):
    r, xh = _rms_stats(x)
    dxh = dy * g
    dx = r * (dxh - xh * jnp.mean(dxh * xh, axis=-1, keepdims=True))
    return dx, jnp.sum(dy * xh, axis=0, keepdims=True)


def _row_spec(tr, w, col=0):
    return pl.BlockSpec((tr, w), lambda i: (i, col))


def _vec_spec(w):
    return pl.BlockSpec((1, w), lambda i: (0, 0))


def _acc_rows(ref, val, i):
    @pl.when(i == 0)
    def _():
        ref[...] = val

    @pl.when(i > 0)
    def _():
        ref[...] += val


def _rms_fwd_call(x, g, name, beside=None):
    lp, w = x.shape
    tr = _pick(lp, [384, 128])

    def body(x_ref, g_ref, *rest):
        _, xh = _rms_stats(x_ref[...])
        rest[-1][...] = (xh * g_ref[...]).astype(BF16)

    if beside is None:
        return pl.pallas_call(
            body, name=name, grid=(lp // tr,), in_specs=[_row_spec(tr, w), _vec_spec(w)],
            out_specs=_row_spec(tr, w), out_shape=jax.ShapeDtypeStruct((lp, w), BF16),
            compiler_params=_cparams(("parallel",)))(x, g)
    return pl.pallas_call(
        body, name=name, grid=(lp // tr,), in_specs=[_row_spec(tr, w), _vec_spec(w), ANY],
        out_specs=_row_spec(tr, w, 1), out_shape=jax.ShapeDtypeStruct((lp, 2 * w), BF16),
        input_output_aliases={2: 0}, compiler_params=_cparams(("parallel",)))(x, g, beside)


def _mid_fwd_call(h0, mix, g_post, g_pre2):
    lp, w = h0.shape
    tr = _pick(lp, [384, 128])

    def body(h0_ref, mix_ref, gp_ref, g2_ref, h1_ref, xn_ref):
        _, mh = _rms_stats(mix_ref[...])
        h1 = h0_ref[...] + mh * gp_ref[...]
        h1_ref[...] = h1
        _, hh = _rms_stats(h1)
        xn_ref[...] = (hh * g2_ref[...]).astype(BF16)

    return pl.pallas_call(
        body, name="mid_fwd", grid=(lp // tr,),
        in_specs=[_row_spec(tr, w), _row_spec(tr, w), _vec_spec(w), _vec_spec(w)],
        out_specs=[_row_spec(tr, w), _row_spec(tr, w)],
        out_shape=[jax.ShapeDtypeStruct((lp, w), F32), jax.ShapeDtypeStruct((lp, w), BF16)],
        compiler_params=_cparams(("parallel",)))(h0, mix, g_post, g_pre2)


def _final_call(h1, f, g_post, target):
    lp, w = h1.shape
    tr = BLK
    nb = lp // tr

    def body(h1_ref, f_ref, g_ref, t_ref, loss_ref, df_ref, dh_ref, dg_ref):
        i = pl.program_id(0)
        fv = f_ref[...]
        g = g_ref[...]
        _, fh = _rms_stats(fv)
        h2 = h1_ref[...] + fh * g
        diff = jnp.where(i > 0, h2 - t_ref[...], 0.0)
        part = 0.5 * jnp.sum(diff * diff, axis=0, keepdims=True) * (1.0 / w)
        _acc_rows(loss_ref, part, i)
        dh = diff * (1.0 / w)
        dh_ref[...] = dh
        df, dg = _rms_bwd(fv, g, dh)
        df_ref[...] = df.astype(BF16)
        _acc_rows(dg_ref, dg, i)

    t_spec = pl.BlockSpec((tr, w), lambda i: (jnp.maximum(i - 1, 0), 0))
    return pl.pallas_call(
        body, name="final_fwd_bwd", grid=(nb,),
        in_specs=[_row_spec(tr, w), _row_spec(tr, w), _vec_spec(w), t_spec],
        out_specs=[_vec_spec(w), _row_spec(tr, w), _row_spec(tr, w), _vec_spec(w)],
        out_shape=[jax.ShapeDtypeStruct((1, w), F32), jax.ShapeDtypeStruct((lp, w), BF16),
                   jax.ShapeDtypeStruct((lp, w), F32), jax.ShapeDtypeStruct((1, w), F32)],
        compiler_params=_cparams(("arbitrary",)))(h1, f, g_post, target)


def _mid_bwd_call(dh2, h1, dxn2, mix, g_pre2, g_post):
    lp, w = h1.shape
    tr = _pick(lp, [384, 128])

    def body(dh2_ref, h1_ref, dxn_ref, mix_ref, g2_ref, gp_ref, dh1_ref, dmix_ref, dg2_ref, dgp_ref):
        i = pl.program_id(0)
        live = (i * tr + _iota((tr, 1), 0)) >= PAD
        dx, dg2 = _rms_bwd(h1_ref[...], g2_ref[...], dxn_ref[...])
        dh1 = jnp.where(live, dh2_ref[...] + dx, 0.0)
        dh1_ref[...] = dh1
        dmix, dgp = _rms_bwd(mix_ref[...], gp_ref[...], dh1)
        dmix_ref[...] = jnp.where(live, dmix, 0.0).astype(BF16)
        _acc_rows(dg2_ref, dg2, i)
        _acc_rows(dgp_ref, dgp, i)

    rs = _row_spec(tr, w)
    return pl.pallas_call(
        body, name="mid_bwd", grid=(lp // tr,),
        in_specs=[rs, rs, rs, rs, _vec_spec(w), _vec_spec(w)],
        out_specs=[rs, rs, _vec_spec(w), _vec_spec(w)],
        out_shape=[jax.ShapeDtypeStruct((lp, w), F32), jax.ShapeDtypeStruct((lp, w), BF16),
                   jax.ShapeDtypeStruct((1, w), F32), jax.ShapeDtypeStruct((1, w), F32)],
        compiler_params=_cparams(("arbitrary",)))(dh2, h1, dxn2, mix, g_pre2, g_post)


def _norm_bwd_call(x, g, dy_arr, dy_col, name, res=None):
    lp, w = x.shape
    tr = _pick(lp, [384, 128])
    has_res = res is not None

    def body(x_ref, g_ref, dy_ref, *rest):
        i = pl.program_id(0)
        live = (i * tr + _iota((tr, 1), 0)) >= PAD
        dx, dg = _rms_bwd(x_ref[...], g_ref[...], dy_ref[...])
        if has_res:
            dx = dx + rest[0][...]
        out_ref, dg_ref = rest[-2], rest[-1]
        out_ref[...] = jnp.where(live, dx, 0.0)
        _acc_rows(dg_ref, dg, i)

    rs = _row_spec(tr, w)
    ins = [rs, _vec_spec(w), _row_spec(tr, w, dy_col)] + ([rs] if has_res else [])
    args = [x, g, dy_arr] + ([res] if has_res else [])
    return pl.pallas_call(
        body, name=name, grid=(lp // tr,), in_specs=ins, out_specs=[rs, _vec_spec(w)],
        out_shape=[jax.ShapeDtypeStruct((lp, w), F32), jax.ShapeDtypeStruct((1, w), F32)],
        compiler_params=_cparams(("arbitrary",)))(*args)


def _shift_down(cur, prev_tail, s, rows):
    if s == 0:
        return cur
    prev = jnp.tile(prev_tail, (cur.shape[0] // 8, 1))
    return jnp.where(rows >= s, pltpu.roll(cur, s, 0), pltpu.roll(prev, s, 0))


def _shift_up(cur, next_head, s, rows):
    if s == 0:
        return cur
    n = cur.shape[0]
    nxt = jnp.tile(next_head, (n // 8, 1))
    return jnp.where(rows < n - s, pltpu.roll(cur, n - s, 0), pltpu.roll(nxt, n - s, 0))


def _gelu_tanh(x):
    c = math.sqrt(2.0 / math.pi)
    t = jnp.tanh(c * (x + 0.044715 * x * x * x))
    return 0.5 * x * (1.0 + t), t


def _conv_fwd_call(src, col0, width, cw, w8, b, taps, *, gate_src=None, gate_col0=0, rb=BLK, name):
    lp = src.shape[0]
    nb, nc = lp // rb, width // cw
    cb0 = col0 // cw
    ffn = gate_src is not None

    def body(x_ref, w_ref, b_ref, *rest):
        if ffn:
            u_ref, y_ref, a_ref, tail = rest
        else:
            y_ref, a_ref, tail = rest
        i = pl.program_id(1)

        @pl.when(i == 0)
        def _():
            tail[...] = jnp.zeros_like(tail)

        cur = x_ref[...]
        rows = _iota((rb, cw), 0)
        y = b_ref[...] + w_ref[taps - 1:taps, :] * cur
        pt = tail[...]
        for s in range(1, taps):
            y = y + w_ref[taps - 1 - s:taps - s, :] * _shift_down(cur, pt, s, rows)
        tail[...] = cur[rb - 8:, :]
        y_ref[...] = y
        if ffn:
            ge, _ = _gelu_tanh(y)
            a_ref[...] = (ge * u_ref[...]).astype(BF16)
        else:
            live = (i * rb + rows) >= PAD
            a_ref[...] = jnp.where(live, y * _sigmoid(y), 0.0)

    blk = lambda c0: pl.BlockSpec((rb, cw), lambda j, i: (i, c0 + j))
    ins = [blk(cb0), pl.BlockSpec((8, cw), lambda j, i: (0, j)), pl.BlockSpec((1, cw), lambda j, i: (0, j))]
    args = [src, w8, b]
    if ffn:
        ins.append(blk(gate_col0 // cw))
        args.append(gate_src)
    return pl.pallas_call(
        body, name=name, grid=(nc, nb), in_specs=ins, out_specs=[blk(0), blk(0)],
        out_shape=[jax.ShapeDtypeStruct((lp, width), F32),
                   jax.ShapeDtypeStruct((lp, width), BF16 if ffn else F32)],
        scratch_shapes=[pltpu.VMEM((8, cw), F32)],
        compiler_params=_cparams(("parallel", "arbitrary")))(*args)


def _conv_bwd_call(src, col0, width, cw, w8, taps, ypre, dact, *, gate_src=None, gate_col0=0, rb=BLK, name):
    lp = src.shape[0]
    nb, nc = lp // rb, width // cw
    cb0 = col0 // cw
    ffn = gate_src is not None

    def body(x_ref, w_ref, y_ref, d_ref, *rest):
        if ffn:
            u_ref, dx_ref, du_ref, dw_ref, db_ref, head = rest
        else:
            dx_ref, dw_ref, db_ref, head = rest
        step = pl.program_id(1)
        i = nb - 1 - step

        @pl.when(step == 0)
        def _():
            head[...] = jnp.zeros_like(head)

        rows = _iota((rb, cw), 0)
        live = (i * rb + rows) >= PAD
        y = y_ref[...]
        d = d_ref[...]
        if ffn:
            ge, t = _gelu_tanh(y)
            c = math.sqrt(2.0 / math.pi)
            dge = 0.5 * (1.0 + t) + 0.5 * y * (1.0 - t * t) * c * (1.0 + 3.0 * 0.044715 * y * y)
            u = u_ref[...]
            du_ref[...] = jnp.where(live, d * ge, 0.0).astype(BF16)
            dy = jnp.where(live, d * u * dge, 0.0)
        else:
            sg = _sigmoid(y)
            dy = jnp.where(live, d * sg * (1.0 + y * (1.0 - sg)), 0.0)
        x = x_ref[...]
        nh = head[...]
        dx = jnp.zeros_like(dy)
        dws = []
        for s in range(taps):
            sh = _shift_up(dy, nh, s, rows)
            dx = dx + w_ref[taps - 1 - s:taps - s, :] * sh
            dws.append(jnp.sum(x * sh, axis=0, keepdims=True))
        head[...] = dy[:8, :]
        dx_ref[...] = jnp.where(live, dx, 0.0).astype(BF16)
        dw = jnp.concatenate([dws[taps - 1 - k] for k in range(taps)]
                             + [jnp.zeros((8 - taps, cw), F32)], axis=0)
        _acc_rows(dw_ref, dw, step)
        _acc_rows(db_ref, jnp.sum(dy, axis=0, keepdims=True), step)

    blk = lambda c0: pl.BlockSpec((rb, cw), lambda j, s: (nb - 1 - s, c0 + j))
    ins = [blk(cb0), pl.BlockSpec((8, cw), lambda j, s: (0, j)), blk(0), blk(0)]
    args = [src, w8, ypre, dact]
    outs = [blk(0)]
    oshape = [jax.ShapeDtypeStruct((lp, width), BF16)]
    if ffn:
        ins.append(blk(gate_col0 // cw))
        args.append(gate_src)
        outs.append(blk(0))
        oshape.append(jax.ShapeDtypeStruct((lp, width), BF16))
    outs += [pl.BlockSpec((8, cw), lambda j, s: (0, j)), pl.BlockSpec((1, cw), lambda j, s: (0, j))]
    oshape += [jax.ShapeDtypeStruct((8, width), F32), jax.ShapeDtypeStruct((1, width), F32)]
    return pl.pallas_call(
        body, name=name, grid=(nc, nb), in_specs=ins, out_specs=outs, out_shape=oshape,
        scratch_shapes=[pltpu.VMEM((8, cw), F32)],
        compiler_params=_cparams(("parallel", "arbitrary")))(*args)


SB_FIRST = 3
SB_GROUP = 4
SB_DEAD = -110.0


def _sb_scores(qm_h, kb):
    z = _dot(qm_h, kb, 1, 1)
    sp = jnp.maximum(z, 0.0) + jnp.log(1.0 + jnp.exp(-jnp.abs(z)))
    return z - sp, -sp


def _dot_tri1(v, tri2):
    r = _dot(v.astype(BF16), tri2[:BLK])
    return r[:, :BLK], r[:, BLK:]


def _tri2(cond):
    t = jnp.concatenate([cond.astype(BF16), jnp.ones((BLK, BLK), BF16)], axis=1)
    return jnp.concatenate([t, t], axis=0)


def _dot_tri(v, tri2):
    hi = v.astype(BF16)
    lo = (v - hi.astype(F32)).astype(BF16)
    r = _dot(jnp.concatenate([hi, lo], axis=1), tri2)
    return r[:, :BLK], r[:, BLK:]


def _sb_fwd_call(proj, ride):
    lp = proj.shape[0]
    nb = lp // BLK
    scale = 1.0 / math.sqrt(HEAD_DIM)

    ns, nl = len(ride[0]), len(ride[1])
    npair = N_HEADS // 2

    def body(q_ref, k_ref, v_ref, *rest):
        o_ref, tl_ref = rest[ns], rest[ns + 1]
        i = pl.program_id(1)
        step = pl.program_id(0) * nb + i
        _ride_run(ride, rest[:ns], rest[ns + 2:ns + 2 + nl], rest[-2], rest[-1], step == 0, step == npair * nb - 1)
        lane = _iota((2 * BLK, BLK), 1)
        row = _iota((2 * BLK, BLK), 0)
        first = row < BLK
        qrow = row & (BLK - 1)
        q = q_ref[...] * scale
        q2 = jnp.concatenate([q, q], axis=0)
        qm = jnp.where(first == (lane < HEAD_DIM), q2, 0.0).astype(BF16)
        tri = _tri2(_iota((BLK, BLK), 0) > _iota((BLK, BLK), 1))

        def chunk(off, nsub, last_valid, carry):
            width = nsub * BLK
            sls = [slice(b * BLK, (b + 1) * BLK) for b in range(nsub)]
            kb = k_ref[pl.ds(off, width), :].astype(BF16)
            vb = v_ref[pl.ds(off, width), :].astype(BF16)
            lb, lk = _sb_scores(qm, kb)
            lks = [lk[:, sl] for sl in sls]
            first_valid = (off + lane) >= PAD
            lks[0] = jnp.where(first_valid, lks[0], 0.0)
            if last_valid is not None:
                lks[-1] = jnp.where(last_valid, lks[-1], 0.0)
            afters = [_dot_tri(lks[b], tri) for b in range(nsub)]
            run, acc = carry
            ws = [None] * nsub
            for b in reversed(range(nsub)):
                wb = jnp.exp(lb[:, sls[b]] + afters[b][0] + run)
                if b == 0:
                    wb = jnp.where(first_valid, wb, 0.0)
                if last_valid is not None and b == nsub - 1:
                    wb = jnp.where(last_valid, wb, 0.0)
                ws[b] = wb.astype(BF16)
                run = run + afters[b][1]
            w = ws[0] if nsub == 1 else jnp.concatenate(ws, axis=1)
            return run, acc + _dot(w, vb)

        before = jnp.minimum(i, SB_FIRST - 1)
        first_off = pl.multiple_of((i - before) * BLK, BLK)
        diag = lane < qrow
        zero = jnp.zeros((2 * BLK, BLK), F32)
        carry = lax.switch(before, [functools.partial(chunk, first_off, n, diag) for n in range(1, SB_FIRST + 1)],
                           (zero, zero))

        def walk(n):
            def body(state):
                off = pl.multiple_of((state[0] - (n - 1)) * BLK, BLK)
                return (state[0] - n, *chunk(off, n, None, state[1:]))

            def cond(state):
                return jnp.logical_and(state[0] >= n - 1, jnp.max(state[1]) > SB_DEAD)

            return cond, body

        state = lax.while_loop(*walk(SB_GROUP), (i - SB_FIRST, *carry))
        pos, run, acc = lax.while_loop(*walk(1), state)
        low = lane[:BLK] < HEAD_DIM
        o_ref[...] = jnp.where(low, acc[:BLK], acc[BLK:])
        tl = jnp.where(low, run[:BLK], run[BLK:])
        tl_ref[...] = jnp.where(lane[:BLK] == 1, jnp.maximum(pos + 1, 0).astype(F32), tl)

    qc, kc, vc = C_Q // BLK, C_K // BLK, C_V // BLK
    blk = pl.BlockSpec((BLK, BLK), lambda p, i: (i, p))
    res = pl.pallas_call(
        body, name="sb_fwd", grid=(npair, nb),
        in_specs=[pl.BlockSpec((BLK, BLK), lambda p, i: (i, qc + p)),
                  pl.BlockSpec((lp, BLK), lambda p, i: (0, kc + p)),
                  pl.BlockSpec((lp, BLK), lambda p, i: (0, vc + p))] + [ANY] * ns,
        out_specs=[blk, blk] + [ANY] * nl,
        out_shape=[jax.ShapeDtypeStruct((lp, N_HEADS * HEAD_DIM), F32)] * 2 + list(ride[1]),
        scratch_shapes=_ride_scratch(ride),
        compiler_params=_cparams(("arbitrary", "arbitrary")))(proj, proj, proj, *ride[0])
    return res[0], res[1], list(res[2:])


def _sb_bwd_call(proj, tl, do, ride):
    lp = proj.shape[0]
    nb = lp // BLK
    scale = 1.0 / math.sqrt(HEAD_DIM)

    ns, nl = len(ride[0]), len(ride[1])
    npair = N_HEADS // 2

    def body(q_ref, k_ref, v_ref, tl_ref, do_ref, *rest):
        dq_ref, dk_ref, dv_ref = rest[ns:ns + 3]
        dk_acc, dv_acc = rest[ns + 3 + nl:ns + 5 + nl]
        i = pl.program_id(1)
        step = pl.program_id(0) * nb + i
        _ride_run(ride, rest[:ns], rest[ns + 3:ns + 3 + nl], rest[-2], rest[-1], step == 0, step == npair * nb - 1)

        @pl.when(i == 0)
        def _():
            dk_acc[...] = jnp.zeros_like(dk_acc)
            dv_acc[...] = jnp.zeros_like(dv_acc)

        lane = _iota((2 * BLK, BLK), 1)
        row = _iota((2 * BLK, BLK), 0)
        qrow = row & (BLK - 1)
        mine = (row < BLK) == (lane < HEAD_DIM)
        q = q_ref[...] * scale
        dov = do_ref[...]
        qm = jnp.where(mine, jnp.concatenate([q, q], axis=0), 0.0).astype(BF16)
        dom = jnp.where(mine, jnp.concatenate([dov, dov], axis=0), 0.0).astype(BF16)
        tlv = tl_ref[...]
        tot = jnp.concatenate([jnp.broadcast_to(tlv[:, 0:1], (BLK, BLK)),
                               jnp.broadcast_to(tlv[:, HEAD_DIM:HEAD_DIM + 1], (BLK, BLK))], axis=0)
        r1, l1 = _iota((BLK, BLK), 0), _iota((BLK, BLK), 1)
        tri_in = _tri2(r1 <= l1)
        tri_ex = _tri2(r1 < l1)

        def chunk(off, nsub, last_valid, carry):
            width = nsub * BLK
            sls = [slice(b * BLK, (b + 1) * BLK) for b in range(nsub)]
            cat = lambda parts: parts[0] if nsub == 1 else jnp.concatenate(parts, axis=1)
            mask_last = lambda b: last_valid is not None and b == nsub - 1
            kb = k_ref[pl.ds(off, width), :].astype(BF16)
            vb = v_ref[pl.ds(off, width), :].astype(BF16)
            lb, lk = _sb_scores(qm, kb)
            dw = _dot(dom, vb, 1, 1)
            lks = [lk[:, sl] for sl in sls]
            first_valid = (off + lane) >= PAD
            lks[0] = jnp.where(first_valid, lks[0], 0.0)
            if last_valid is not None:
                lks[-1] = jnp.where(last_valid, lks[-1], 0.0)
            pins = [_dot_tri(lks[b], tri_in) for b in range(nsub)]
            run, gsum, dq = carry
            ws, gs = [], []
            for b in range(nsub):
                wb = jnp.exp(lb[:, sls[b]] + (tot - run - pins[b][0]))
                if b == 0:
                    wb = jnp.where(first_valid, wb, 0.0)
                if mask_last(b):
                    wb = jnp.where(last_valid, wb, 0.0)
                ws.append(wb.astype(BF16))
                gs.append(wb * dw[:, sls[b]])
                run = run + pins[b][1]
            gexs = [_dot_tri1(gs[b], tri_ex) for b in range(nsub)]
            beta = jnp.exp(lb)
            parts = []
            for b in range(nsub):
                bt = beta[:, sls[b]]
                dzb = gs[b] * (1.0 - bt) - (gsum + gexs[b][0]) * bt
                if b == 0:
                    dzb = jnp.where(first_valid, dzb, 0.0)
                if mask_last(b):
                    dzb = jnp.where(last_valid, dzb, 0.0)
                parts.append(dzb.astype(BF16))
                gsum = gsum + gexs[b][1]
            dz, w = cat(parts), cat(ws)
            dk_acc[pl.ds(off, width), :] += _dot(dz, qm, 0, 0)
            dv_acc[pl.ds(off, width), :] += _dot(w, dom, 0, 0)
            return run, gsum, dq + _dot(dz, kb)

        diag = lane < qrow
        zero = jnp.zeros((2 * BLK, BLK), F32)
        top = i - SB_FIRST

        def walk(n):
            def body(state):
                off = pl.multiple_of(state[0] * BLK, BLK)
                return (state[0] + n, *chunk(off, n, None, state[1:]))

            return (lambda state: state[0] + (n - 1) <= top), body

        state = (jnp.max(tlv[:, 1:2]).astype(jnp.int32), zero, zero, zero)
        state = lax.while_loop(*walk(SB_GROUP), state)
        carry = lax.while_loop(*walk(1), state)[1:]
        before = jnp.minimum(i, SB_FIRST - 1)
        first_off = pl.multiple_of((i - before) * BLK, BLK)
        dq = lax.switch(before, [functools.partial(chunk, first_off, n, diag) for n in range(1, SB_FIRST + 1)],
                        carry)[2]
        dq_ref[...] = (jnp.where(lane[:BLK] < HEAD_DIM, dq[:BLK], dq[BLK:]) * scale).astype(BF16)

        @pl.when(i == nb - 1)
        def _():
            dk_ref[...] = dk_acc[...].astype(BF16)
            dv_ref[...] = dv_acc[...].astype(BF16)

    qc, kc, vc = C_Q // BLK, C_K // BLK, C_V // BLK
    blk = pl.BlockSpec((BLK, BLK), lambda p, i: (i, p))
    full = pl.BlockSpec((lp, BLK), lambda p, i: (0, p))
    w = N_HEADS * HEAD_DIM
    res = pl.pallas_call(
        body, name="sb_bwd", grid=(npair, nb),
        in_specs=[pl.BlockSpec((BLK, BLK), lambda p, i: (i, qc + p)),
                  pl.BlockSpec((lp, BLK), lambda p, i: (0, kc + p)),
                  pl.BlockSpec((lp, BLK), lambda p, i: (0, vc + p)),
                  blk, blk] + [ANY] * ns,
        out_specs=[blk, full, full] + [ANY] * nl,
        out_shape=[jax.ShapeDtypeStruct((lp, w), BF16)] * 3 + list(ride[1]),
        scratch_shapes=[pltpu.VMEM((lp, BLK), F32), pltpu.VMEM((lp, BLK), F32)] + _ride_scratch(ride),
        compiler_params=_cparams(("arbitrary", "arbitrary")))(proj, proj, proj, tl, do, *ride[0])
    return res[0], res[1], res[2], list(res[3:])


def _log1p(e):
    u = 1.0 + e
    return jnp.where(u == 1.0, e, jnp.log(u) * e / jnp.where(u == 1.0, 1.0, u - 1.0))


def _ssd_common(c, dtr, bias, alog):
    row = _iota((BLK, BLK), 0)
    lane = _iota((BLK, BLK), 1)
    live = ((c * BLK + row) >= PAD) & (lane < N_HEADS)
    pre = dtr + bias
    dt = jnp.where(live, jnp.maximum(pre, 0.0) + _log1p(jnp.exp(-jnp.abs(pre))), 0.0)
    a_neg = -jnp.exp(alog)
    a = dt * a_neg
    t_in = (lane <= row).astype(BF16)
    cs = _dot_sel_l(t_in, a)
    cs_t = cs.T
    cs_end = cs[BLK - 1:BLK, :]
    e = jnp.exp(cs)
    f = jnp.exp(cs_end - cs)
    xp = ((_iota((BLK, SSD_INNER), 1) // HEAD_DIM) == _iota((BLK, SSD_INNER), 0)).astype(BF16)
    xp_t = ((_iota((SSD_INNER, BLK), 0) // HEAD_DIM) == _iota((SSD_INNER, BLK), 1)).astype(BF16)
    decay_col = _dot_sel_l(xp_t, jnp.exp(cs_t))[:, BLK - 1:BLK]
    return dict(live=live, pre=pre, dt=dt, a_neg=a_neg, cs=cs, cs_t=cs_t, e=e, f=f, xp=xp, xp_t=xp_t,
                decay_col=decay_col, row=row, lane=lane,
                dt_x=_dot_sel_r(dt, xp), e_x=_dot_sel_r(e, xp), f_x=_dot_sel_r(f, xp))


def _ssd_ldec(q, h):
    diff = q["cs"][:, h:h + 1] - q["cs_t"][h:h + 1, :]
    causal = q["row"] >= q["lane"]
    return jnp.where(causal, jnp.exp(jnp.where(causal, diff, 0.0)), 0.0)


def _ssd_fwd_call(xbc, proj, bias, alog, d_x, norm_g):
    lp = xbc.shape[0]
    nb = lp // BLK
    gw = SSD_INNER // SSD_GROUPS
    ppg = gw // BLK

    def body(xbc_ref, dtr_ref, z_ref, bias_ref, alog_ref, dx_ref, ng_ref, yb_ref, ypre_ref, sprev_ref, s_ref):
        c = pl.program_id(0)

        @pl.when(c == 0)
        def _():
            s_ref[...] = jnp.zeros_like(s_ref)

        q = _ssd_common(c, dtr_ref[...], bias_ref[...], alog_ref[...])
        x = xbc_ref[:, 0:SSD_INNER]
        xd = x * q["dt_x"]
        low = q["lane"] < HEAD_DIM
        s_old = s_ref[...]
        sprev_ref[...] = s_old
        xdf = (xd * q["f_x"]).astype(BF16)
        for g in range(SSD_GROUPS):
            bg = xbc_ref[:, SSD_INNER + g * SSD_STATE:SSD_INNER + (g + 1) * SSD_STATE].astype(BF16)
            cg = xbc_ref[:, SSD_INNER + (SSD_GROUPS + g) * SSD_STATE:
                         SSD_INNER + (SSD_GROUPS + g + 1) * SSD_STATE].astype(BF16)
            cb = _dot(cg, bg, 1, 1)
            gs = slice(g * gw, (g + 1) * gw)
            y_off = _dot(cg, s_old[gs, :].astype(BF16), 1, 1) * q["e_x"][:, gs]
            s_ref[gs, :] = s_old[gs, :] * q["decay_col"][gs, :] + _dot(xdf[:, gs], bg, 0, 0)
            for pr in range(ppg):
                cols = slice(g * gw + pr * BLK, g * gw + (pr + 1) * BLK)
                xd_p = xd[:, cols]
                acc = y_off[:, pr * BLK:(pr + 1) * BLK]
                for hh in range(2):
                    h = (g * gw + pr * BLK) // HEAD_DIM + hh
                    m = (cb * _ssd_ldec(q, h)).astype(BF16)
                    xm = jnp.where(low, xd_p, 0.0) if hh == 0 else jnp.where(low, 0.0, xd_p)
                    acc = acc + _dot(m, xm.astype(BF16))
                ypre_ref[:, cols] = acc
        ypre = ypre_ref[...] + x * dx_ref[...]
        ypre_ref[...] = ypre
        z = z_ref[...]
        yg = ypre * (z * _sigmoid(z))
        _, yh = _rms_stats(yg)
        yb_ref[...] = (yh * ng_ref[...]).astype(BF16)

    row = lambda w, col: pl.BlockSpec((BLK, w), lambda c: (c, col))
    vec = lambda w: pl.BlockSpec((1, w), lambda c: (0, 0))
    return pl.pallas_call(
        body, name="ssd_fwd", grid=(nb,),
        in_specs=[row(XBC, 0), row(BLK, C_DT // BLK), row(SSD_INNER, 0), vec(BLK), vec(BLK),
                  vec(SSD_INNER), vec(SSD_INNER)],
        out_specs=[row(SSD_INNER, 0), row(SSD_INNER, 0),
                   pl.BlockSpec((None, SSD_INNER, SSD_STATE), lambda c: (c, 0, 0))],
        out_shape=[jax.ShapeDtypeStruct((lp, 2 * SSD_INNER), BF16), jax.ShapeDtypeStruct((lp, SSD_INNER), F32),
                   jax.ShapeDtypeStruct((nb, SSD_INNER, SSD_STATE), F32)],
        scratch_shapes=[pltpu.VMEM((SSD_INNER, SSD_STATE), F32)],
        compiler_params=_cparams(("arbitrary",)))(xbc, proj, proj, bias, alog, d_x, norm_g)


def _ssd_bwd_call(dycat, ypre, xbc, proj, sprev, bias, alog, d_x, norm_g):
    lp = xbc.shape[0]
    nb = lp // BLK
    gw = SSD_INNER // SSD_GROUPS
    ppg = gw // BLK

    def body(dy_ref, ypre_ref, xbc_ref, dtr_ref, z_ref, sp_ref, bias_ref, alog_ref, dxp_ref, ng_ref,
             dz_ref, dxbc_ref, ddt_ref, dng_ref, dd_ref, dal_ref, dbi_ref, ds_ref, dxd_ref):
        step = pl.program_id(0)
        c = nb - 1 - step

        @pl.when(step == 0)
        def _():
            ds_ref[...] = jnp.zeros_like(ds_ref)

        q = _ssd_common(c, dtr_ref[...], bias_ref[...], alog_ref[...])
        row, lane = q["row"], q["lane"]
        low = lane < HEAD_DIM
        rowlive = ((c * BLK + _iota((BLK, 1), 0)) >= PAD)
        x = xbc_ref[:, 0:SSD_INNER]
        xd = x * q["dt_x"]
        z = z_ref[...]
        sz = _sigmoid(z)
        silu = z * sz
        ypre = ypre_ref[...]
        dyg, dng = _rms_bwd(ypre * silu, ng_ref[...], dy_ref[...])
        _acc_rows(dng_ref, dng, step)
        dyp = dyg * silu
        dz_ref[...] = jnp.where(rowlive, dyg * ypre * (sz * (1.0 + z * (1.0 - sz))), 0.0).astype(BF16)
        _acc_rows(dd_ref, jnp.sum(dyp * x, axis=0, keepdims=True), step)
        dye = dyp * q["e_x"]
        xdf = xd * q["f_x"]
        s_prev = sp_ref[...]
        ds_old = ds_ref[...]
        qrow = jnp.zeros((BLK, BLK), F32)
        qcol_t = jnp.zeros((BLK, BLK), F32)
        red_e = []
        red_f = []
        for g in range(SSD_GROUPS):
            gs = slice(g * gw, (g + 1) * gw)
            bsl = slice(SSD_INNER + g * SSD_STATE, SSD_INNER + (g + 1) * SSD_STATE)
            csl = slice(SSD_INNER + (SSD_GROUPS + g) * SSD_STATE, SSD_INNER + (SSD_GROUPS + g + 1) * SSD_STATE)
            bg = xbc_ref[:, bsl].astype(BF16)
            cg = xbc_ref[:, csl].astype(BF16)
            sg = s_prev[gs, :].astype(BF16)
            dsg = ds_old[gs, :].astype(BF16)
            cb = _dot(cg, bg, 1, 1)
            bds = _dot(bg, dsg, 1, 1)
            y_off = _dot(cg, sg, 1, 1) * q["e_x"][:, gs]
            red_e.append(dyp[:, gs] * y_off)
            red_f.append(xd[:, gs] * bds * q["f_x"][:, gs])
            dc = _dot(dye[:, gs].astype(BF16), sg)
            db = _dot(xdf[:, gs].astype(BF16), dsg)
            ds_ref[gs, :] = ds_old[gs, :] * q["decay_col"][gs, :] + _dot(dye[:, gs].astype(BF16), cg, 0, 0)
            dcb = jnp.zeros((BLK, BLK), F32)
            for pr in range(ppg):
                cols = slice(g * gw + pr * BLK, g * gw + (pr + 1) * BLK)
                xd_p = xd[:, cols].astype(BF16)
                dy_p = dyp[:, cols]
                acc = q["f_x"][:, cols] * bds[:, pr * BLK:(pr + 1) * BLK]
                for hh in range(2):
                    h = (g * gw + pr * BLK) // HEAD_DIM + hh
                    ld = _ssd_ldec(q, h)
                    m = cb * ld
                    dym = (jnp.where(low, dy_p, 0.0) if hh == 0 else jnp.where(low, 0.0, dy_p)).astype(BF16)
                    dm = jnp.where(row >= lane, _dot(dym, xd_p, 1, 1), 0.0)
                    acc = acc + _dot(m.astype(BF16), dym, 0, 0)
                    dcb = dcb + dm * ld
                    qq = dm * m
                    qrow = qrow + jnp.where(lane == h, jnp.sum(qq, axis=1, keepdims=True), 0.0)
                    qcol_t = qcol_t + jnp.where(row == h, jnp.sum(qq, axis=0, keepdims=True), 0.0)
                dxd_ref[:, cols] = acc
            dcbb = dcb.astype(BF16)
            dxbc_ref[:, bsl] = jnp.where(rowlive, db + _dot(dcbb, cg, 0, 0), 0.0)
            dxbc_ref[:, csl] = jnp.where(rowlive, dc + _dot(dcbb, bg), 0.0)
        dxd = dxd_ref[...]
        dxbc_ref[:, 0:SSD_INNER] = jnp.where(rowlive, dxd * q["dt_x"] + dyp * dxp_ref[...], 0.0)
        xp_t = q["xp_t"]
        fw = _dot_sel_r(jnp.concatenate(red_f, axis=1), xp_t)
        dcs = qrow - qcol_t.T + _dot_sel_r(jnp.concatenate(red_e, axis=1), xp_t) - fw
        end_f = jnp.sum(fw, axis=0, keepdims=True)
        sds = jnp.sum(ds_old * s_prev, axis=1, keepdims=True)
        per_head = _dot_sel_l(q["xp"], jnp.broadcast_to(sds, (SSD_INNER, BLK)))
        end_e = per_head.T[0:1, :] * jnp.exp(q["cs"][BLK - 1:BLK, :])
        dcs = dcs + jnp.where(row == BLK - 1, end_f + end_e, 0.0)
        t_up = (lane >= row).astype(BF16)
        da = _dot_sel_l(t_up, dcs)
        ddt = da * q["a_neg"] + _dot_sel_r(dxd * x, xp_t)
        _acc_rows(dal_ref, jnp.sum(da * q["dt"] * q["a_neg"], axis=0, keepdims=True), step)
        ddtr = jnp.where(q["live"], ddt * _sigmoid(q["pre"]), 0.0)
        ddt_ref[...] = ddtr.astype(BF16)
        _acc_rows(dbi_ref, jnp.sum(ddtr, axis=0, keepdims=True), step)

    row_s = lambda w, col: pl.BlockSpec((BLK, w), lambda s: (nb - 1 - s, col))
    vec = lambda w: pl.BlockSpec((1, w), lambda s: (0, 0))
    return pl.pallas_call(
        body, name="ssd_bwd", grid=(nb,),
        in_specs=[row_s(SSD_INNER, 0), row_s(SSD_INNER, 0), row_s(XBC, 0), row_s(BLK, C_DT // BLK),
                  row_s(SSD_INNER, 0), pl.BlockSpec((None, SSD_INNER, SSD_STATE), lambda s: (nb - 1 - s, 0, 0)),
                  vec(BLK), vec(BLK), vec(SSD_INNER), vec(SSD_INNER)],
        out_specs=[row_s(SSD_INNER, 0), row_s(XBC, 0), row_s(BLK, 0),
                   vec(SSD_INNER), vec(SSD_INNER), vec(BLK), vec(BLK)],
        out_shape=[jax.ShapeDtypeStruct((lp, SSD_INNER), BF16), jax.ShapeDtypeStruct((lp, XBC), F32),
                   jax.ShapeDtypeStruct((lp, BLK), BF16),
                   jax.ShapeDtypeStruct((1, SSD_INNER), F32), jax.ShapeDtypeStruct((1, SSD_INNER), F32),
                   jax.ShapeDtypeStruct((1, BLK), F32), jax.ShapeDtypeStruct((1, BLK), F32)],
        scratch_shapes=[pltpu.VMEM((SSD_INNER, SSD_STATE), F32), pltpu.VMEM((BLK, SSD_INNER), F32)],
        compiler_params=_cparams(("arbitrary",)))(dycat, ypre, xbc, proj, proj, sprev, bias, alog, d_x, norm_g)


def _pad_rows8(w):
    return jnp.pad(w, ((0, 8 - w.shape[0]), (0, 0)))


def _pad_lanes(v, n=BLK):
    return jnp.pad(v, ((0, 0), (0, n - v.shape[1])))


def _local_step(x, target, wt, late_shards, late_weights, w_in_shards):
    seq = x.shape[0]
    lp = seq + BLK
    tm = _pick(lp, [1408, 768, 384, 128])
    tkr = _pick(lp, [1408, 384, 128])
    rbc = _pick(lp, [384, 128])
    h0 = jnp.concatenate([jnp.zeros((PAD, D_MODEL), F32), wt["meta"], x], axis=0)
    bias = _pad_lanes(wt["ssd_dt_bias"])
    alog = _pad_lanes(wt["ssd_a_log"])
    d_x = jnp.repeat(wt["ssd_d"], HEAD_DIM, axis=1)
    cw8 = _pad_rows8(wt["ssd_conv_w"])
    fw8 = _pad_rows8(wt["ffn_conv_w"])
    fcw = D_FF // 2

    xn1 = _rms_fwd_call(h0, wt["mix_pre_g"], "norm1")
    proj, late_a = _mm(xn1, wt["w_in"], tm=tm, tn=1152, tk=D_MODEL, ride=_gather_ride(late_shards[0::2]),
                       name="mm_proj")
    proj = proj[0]
    conv_pre, xbc = _conv_fwd_call(proj, C_XBC, XBC, 512, cw8, wt["ssd_conv_b"], 4, rb=rbc, name="ssd_conv_fwd")
    y_ssd, ypre, sprev = _ssd_fwd_call(xbc, proj, bias, alog, d_x, wt["ssd_norm_g"])
    o, tl, late_b = _sb_fwd_call(proj, _gather_ride(late_shards[1:2]))
    ycat = _rms_fwd_call(o, wt["sb_norm_g"], "sb_norm", beside=y_ssd)
    w_out, w_up, w_down = late_weights([late_a[0], late_b[0], late_a[1]])
    mix = _mm(ycat, w_out, tm=tm, tn=1024, tk=2048, name="mm_mix")[0]
    h1, xn2 = _mid_fwd_call(h0, mix, wt["mix_post_g"], wt["ffn_pre_g"])
    gu = _mm(xn2, w_up, tm=tm, tn=1408, tk=D_MODEL, name="mm_up")[0]
    gpre, act = _conv_fwd_call(gu, 0, D_FF, fcw, fw8, wt["ffn_conv_b"], 3, gate_src=gu, gate_col0=D_FF,
                               rb=rbc, name="ffn_conv_fwd")
    f = _mm(act, w_down, tm=tm, tn=1024, tk=1408, name="mm_down")[0]
    loss_row, df, dh2, dg_ffn_post = _final_call(h1, f, wt["ffn_post_g"], target)

    dact = _mm(df, w_down, tb=True, tm=tm, tn=1408, tk=D_MODEL, name="mm_dact")[0]
    dw_down, dw_down_b = _mm(act, df, ta=True, tm=1408, tn=1024, tk=tkr, extra_bf16=True, name="mm_dw_down")
    by_chip = lambda g: g.reshape(N_CHIPS, -1, D_MODEL)
    dgate, dup, dfcw, dfcb = _conv_bwd_call(gu, 0, D_FF, fcw, fw8, 3, gpre, dact, gate_src=gu, gate_col0=D_FF,
                                            rb=rbc, name="ffn_conv_bwd")
    dgu = jnp.concatenate([dgate, dup], axis=1)
    dxn2, land_down = _mm(dgu, w_up, tb=True, tm=tm, tn=1024, tk=1408, ride=_scatter_ride(by_chip(dw_down_b)),
                          name="mm_dxn2")
    dw_up, dw_up_b = _mm(xn2, dgu, ta=True, tm=1024, tn=1408, tk=tkr, nsplit=N_CHIPS, extra_bf16=True,
                         name="mm_dw_up")
    dh1, dmix, dg_ffn_pre, dg_mix_post = _mid_bwd_call(dh2, h1, dxn2[0], mix, wt["ffn_pre_g"], wt["mix_post_g"])
    dycat = _mm(dmix, w_out, tb=True, tm=tm, tn=1024, tk=D_MODEL, name="mm_dycat")[0]
    dw_out, dw_out_b = _mm(ycat, dmix, ta=True, tm=1024, tn=1024, tk=tkr, extra_bf16=True, name="mm_dw_out")
    do, dg_sb = _norm_bwd_call(o, wt["sb_norm_g"], dycat, 1, "sb_norm_bwd")
    dq, dk, dv, lands = _sb_bwd_call(proj, tl, do, _join_rides(_scatter_ride(dw_up_b),
                                                                  _scatter_ride(by_chip(dw_out_b))))
    dz, dxbc_act, ddt, dg_ssd, dd_x, dalog, dbias = _ssd_bwd_call(
        dycat, ypre, xbc, proj, sprev, bias, alog, d_x, wt["ssd_norm_g"])
    dxbc, dcw, dcb = _conv_bwd_call(proj, C_XBC, XBC, 512, cw8, 4, conv_pre, dxbc_act, rb=rbc,
                                    name="ssd_conv_bwd")
    dproj = jnp.concatenate([dz, dxbc, ddt, dq, dk, dv], axis=1)
    dw_in, dw_in_b = w_in_shards(_mm(xn1, dproj, ta=True, tm=1024, tn=1152, tk=tkr, name="mm_dw_in")[0])
    dxn1, land_in = _mm(dproj, wt["w_in"], tb=True, tm=tm, tn=1024, tk=1152, ride=_scatter_ride(dw_in_b),
                        name="mm_dxn1")
    dh0, dg_pre = _norm_bwd_call(h0, wt["mix_pre_g"], dxn1[0], 0, "norm1_bwd", res=dh1)

    small = {
        "meta_tokens": dh0[PAD:BLK], "mix_pre_g": dg_pre, "ssd_conv_w": dcw[:4], "ssd_conv_b": dcb,
        "ssd_dt_bias": dbias[:, :N_HEADS], "ssd_a_log": dalog[:, :N_HEADS],
        "ssd_d": jnp.sum(dd_x.reshape(N_HEADS, HEAD_DIM), axis=1)[None],
        "ssd_norm_g": dg_ssd, "sb_norm_g": dg_sb, "mix_post_g": dg_mix_post, "ffn_pre_g": dg_ffn_pre,
        "ffn_conv_w": dfcw[:3], "ffn_conv_b": dfcb, "ffn_post_g": dg_ffn_post,
    }
    pending = {"w_in": (dw_in, land_in[0]), "w_out": (by_chip(dw_out), lands[1]), "w_up": (dw_up, lands[0]),
               "w_down": (by_chip(dw_down), land_down[0])}
    return loss_row, dh0[BLK:], small, pending


def _adamw_call(w, g, m, v, name):
    rows, cols = w.shape
    tr = 256 if rows % 256 == 0 else (352 if rows % 352 == 0 else rows)
    c1 = 1.0 - ADAM_B1 ** ADAM_STEP
    c2 = 1.0 - ADAM_B2 ** ADAM_STEP

    def body(w_ref, g_ref, m_ref, v_ref, d_ref, mo_ref, vo_ref):
        gv = g_ref[...]
        m2 = ADAM_B1 * m_ref[...] + (1.0 - ADAM_B1) * gv
        v2 = ADAM_B2 * v_ref[...] + (1.0 - ADAM_B2) * (gv * gv)
        d_ref[...] = -ADAM_LR * ((m2 / c1) / (jnp.sqrt(v2 / c2) + ADAM_EPS) + ADAM_WD * w_ref[...])
        mo_ref[...] = m2
        vo_ref[...] = v2

    spec = pl.BlockSpec((tr, cols), lambda i: (i, 0))
    return pl.pallas_call(
        body, name=name, grid=(rows // tr,), in_specs=[spec] * 4, out_specs=[spec] * 3,
        out_shape=[jax.ShapeDtypeStruct((rows, cols), F32)] * 3,
        compiler_params=_cparams(("parallel",)))(w, g, m, v)


ANY = pl.BlockSpec(memory_space=pl.ANY)


def _place():
    x, y, c = lax.axis_index("x"), lax.axis_index("y"), lax.axis_index("c")
    chips = [(1 - x, y), (x, 1 - y), (1 - x, 1 - y)]
    return x, y, c, chips


def _half(c, h):
    return pl.ds(pl.multiple_of(c * h, 8), h)


def _allgather_call(shards):
    n = len(shards)

    def body(*refs):
        ins, outs = refs[:n], refs[n:2 * n]
        send_i, recv_i, send_d, recv_d = refs[2 * n:]
        x, y, c, chips = _place()
        me = 2 * x + y
        sends = []
        for a in range(n):
            h = shards[a].shape[0] // 2
            for j, chip in enumerate(chips):
                cp = pltpu.make_async_remote_copy(
                    src_ref=ins[a].at[_half(c, h)], dst_ref=outs[a].at[me, _half(c, h)],
                    send_sem=send_i.at[3 * a + j], recv_sem=recv_i.at[3 * a + j],
                    device_id=(*chip, c), device_id_type=MESH)
                cp.start()
                sends.append(cp)
        for a in range(n):
            h = shards[a].shape[0] // 2
            for j, chip in enumerate(chips):
                src = 2 * chip[0] + chip[1]
                landed = outs[a].at[src, _half(c, h)]
                pltpu.make_async_remote_copy(
                    src_ref=landed, dst_ref=landed, send_sem=send_i.at[3 * a + j], recv_sem=recv_i.at[3 * a + j],
                    device_id=(*chip, c), device_id_type=MESH).wait_recv()
                cp = pltpu.make_async_remote_copy(
                    src_ref=landed, dst_ref=landed, send_sem=send_d.at[3 * a + j], recv_sem=recv_d.at[3 * a + j],
                    device_id=(x, y, 1 - c), device_id_type=MESH)
                cp.start()
                sends.append(cp)
        for a in range(n):
            h = shards[a].shape[0] // 2
            for j, chip in enumerate(chips):
                src = 2 * chip[0] + chip[1]
                other = outs[a].at[src, _half(1 - c, h)]
                pltpu.make_async_remote_copy(
                    src_ref=other, dst_ref=other, send_sem=send_d.at[3 * a + j], recv_sem=recv_d.at[3 * a + j],
                    device_id=(x, y, 1 - c), device_id_type=MESH).wait_recv()
        for cp in sends:
            cp.wait_send()

    return pl.pallas_call(
        body, name="allgather_weights", in_specs=[ANY] * n, out_specs=[ANY] * n,
        out_shape=[jax.ShapeDtypeStruct((N_CHIPS,) + s.shape, s.dtype) for s in shards],
        scratch_shapes=[pltpu.SemaphoreType.DMA((3 * n,))] * 4,
    )(*shards)


def _ride_scratch(ride):
    n = ride[3] + ride[5]
    return [pltpu.SemaphoreType.DMA((n,)), pltpu.SemaphoreType.DMA((n,))]


def _ride_run(ride, src_refs, land_refs, send, recv, first, last):
    plan, ncp, plan2 = ride[2], ride[3], ride[4]

    def copies(plan, k0, landing):
        return [pltpu.make_async_remote_copy(src_ref=src, dst_ref=(land if landing else dst), send_sem=send.at[k0 + k],
                                             recv_sem=recv.at[k0 + k], device_id=dev, device_id_type=MESH)
                for k, (src, dst, land, dev) in enumerate(plan(src_refs, land_refs))]

    @pl.when(first)
    def _():
        for cp in copies(plan, 0, False):
            cp.start()

    @pl.when(last)
    def _():
        for cp in copies(plan, 0, True):
            cp.wait_send()
            cp.wait_recv()
        if plan2 is not None:
            for cp in copies(plan2, ncp, False):
                cp.start()
            for cp in copies(plan2, ncp, True):
                cp.wait_send()
                cp.wait_recv()


def _join_rides(r1, r2):
    assert r1[4] is None and r2[4] is None
    n1, l1 = len(r1[0]), len(r1[1])

    def plan(srcs, lands):
        return r1[2](srcs[:n1], lands[:l1]) + r2[2](srcs[n1:], lands[l1:])

    return (r1[0] + r2[0], r1[1] + r2[1], plan, r1[3] + r2[3], None, 0)


def _gather_ride(shards):
    n = len(shards)
    halves = [s.shape[0] // 2 for s in shards]

    def over_ici(srcs, lands):
        x, y, c, chips = _place()
        me = 2 * x + y
        return [(srcs[a].at[_half(c, halves[a])], lands[a].at[me, _half(c, halves[a])],
                 lands[a].at[2 * chip[0] + chip[1], _half(c, halves[a])], (*chip, c))
                for a in range(n) for chip in chips]

    def to_sibling(srcs, lands):
        x, y, c, chips = _place()
        return [(lands[a].at[2 * chip[0] + chip[1], _half(c, halves[a])],
                 lands[a].at[2 * chip[0] + chip[1], _half(c, halves[a])],
                 lands[a].at[2 * chip[0] + chip[1], _half(1 - c, halves[a])], (x, y, 1 - c))
                for a in range(n) for chip in chips]

    return (list(shards), [jax.ShapeDtypeStruct((N_CHIPS,) + s.shape, s.dtype) for s in shards],
            over_ici, 3 * n, to_sibling, 3 * n)


def _scatter_ride(g_b):
    h = g_b.shape[1] // 2
    return ([g_b], [jax.ShapeDtypeStruct((8, h, g_b.shape[2]), BF16)], _scatter_plan(h), 7, None, 0)


def _scatter_plan(h):
    def plan(srcs, lands):
        x, y, c, _ = _place()
        me = 4 * x + 2 * y + c
        out = []
        for p in range(1, 8):
            px = 1 - x if p & 4 else x
            py = 1 - y if p & 2 else y
            pc = 1 - c if p & 1 else c
            out.append((srcs[0].at[2 * px + py, _half(pc, h)], lands[0].at[me],
                        lands[0].at[4 * px + 2 * py + pc], (px, py, pc)))
        return out
    return plan


def _grad_sum_call(own, land, place, name):
    _, h, cols = land.shape
    th = _pick(h, [256, 176, 8])
    nt = h // th

    def body(p_ref, own_ref, *refs):
        acc = own_ref[...]
        for r in refs[:7]:
            acc = acc + r[...].astype(F32)
        refs[7][...] = acc

    def peer(k):
        return pl.BlockSpec((None, th, cols), lambda i, p_ref: (p_ref[2 + k], i, 0))

    return pl.pallas_call(
        body, name=name,
        grid_spec=pltpu.PrefetchScalarGridSpec(
            num_scalar_prefetch=1, grid=(nt,),
            in_specs=[pl.BlockSpec((None, th, cols), lambda i, p_ref: (p_ref[1], p_ref[0] * nt + i, 0))]
            + [peer(k) for k in range(7)],
            out_specs=pl.BlockSpec((th, cols), lambda i, p_ref: (p_ref[0] * nt + i, 0))),
        out_shape=jax.ShapeDtypeStruct((2 * h, cols), F32),
        compiler_params=_cparams(("parallel",)))(place, own, *[land] * 7)


def _half_exchange_call(shards):
    n = len(shards)

    def body(*refs):
        outs = refs[n:2 * n]
        send_d, recv_d = refs[2 * n:]
        x, y, c, _ = _place()
        cps = []
        for a in range(n):
            h = shards[a].shape[0] // 2
            mine = outs[a].at[_half(c, h)]
            cp = pltpu.make_async_remote_copy(
                src_ref=mine, dst_ref=mine, send_sem=send_d.at[a], recv_sem=recv_d.at[a],
                device_id=(x, y, 1 - c), device_id_type=MESH)
            cp.start()
            cps.append(cp)
        for a, cp in enumerate(cps):
            h = shards[a].shape[0] // 2
            theirs = outs[a].at[_half(1 - c, h)]
            pltpu.make_async_remote_copy(
                src_ref=theirs, dst_ref=theirs, send_sem=send_d.at[a], recv_sem=recv_d.at[a],
                device_id=(x, y, 1 - c), device_id_type=MESH).wait_recv()
            cp.wait_send()

    return pl.pallas_call(
        body, name="grad_half_exchange", in_specs=[ANY] * n, out_specs=[ANY] * n,
        out_shape=[jax.ShapeDtypeStruct(sv.shape, F32) for sv in shards],
        input_output_aliases={a: a for a in range(n)},
        scratch_shapes=[pltpu.SemaphoreType.DMA((n,))] * 2,
    )(*shards)


def _allreduce_small_call(arrs):
    n = len(arrs)
    offs, rows = [], 0
    for a in arrs:
        offs.append(rows)
        rows += a.shape[0]
    rows = -(-rows // 8) * 8
    width = -(-max(a.shape[1] for a in arrs) // BLK) * BLK

    def body(*refs):
        ins, outs = refs[:n], refs[n:2 * n]
        gath, send_sems, recv_sems = refs[2 * n:]
        x, y, c, chips = _place()
        me, sibling = (x, y, c), (x, y, 1 - c)

        def slot(px, py, pc):
            return gath.at[4 * px + 2 * py + pc]

        def copy(k, block, to):
            return pltpu.make_async_remote_copy(
                src_ref=slot(*block), dst_ref=slot(*block),
                send_sem=send_sems.at[k], recv_sem=recv_sems.at[k], device_id=to, device_id_type=MESH)

        mine = slot(*me)
        mine[...] = jnp.zeros((rows, width), F32)
        for k in range(n):
            r, w = arrs[k].shape
            mine[offs[k]:offs[k] + r, 0:w] = ins[k][...]
        first = [copy(0, me, sibling)]
        first += [copy(1 + j, me, (*chip, c)) for j, chip in enumerate(chips)]
        for cp in first:
            cp.start()
        passed = [copy(4 + j, (*chip, c), sibling) for j, chip in enumerate(chips)]
        for j, chip in enumerate(chips):
            copy(1 + j, (*chip, c), me).wait_recv()
            passed[j].start()
        copy(0, sibling, me).wait_recv()
        for j, chip in enumerate(chips):
            copy(4 + j, (*chip, 1 - c), me).wait_recv()
        for cp in first + passed:
            cp.wait_send()
        acc = gath[0]
        for d in range(1, 8):
            acc = acc + gath[d]
        for k in range(n):
            r, w = arrs[k].shape
            outs[k][...] = acc[offs[k]:offs[k] + r, 0:w]

    vm = pl.BlockSpec(memory_space=pltpu.VMEM)
    return pl.pallas_call(
        body, name="allreduce_small", in_specs=[vm] * n, out_specs=[vm] * n,
        out_shape=[jax.ShapeDtypeStruct(a.shape, F32) for a in arrs],
        scratch_shapes=[pltpu.VMEM((8, rows, width), F32), pltpu.SemaphoreType.DMA((7,)),
                        pltpu.SemaphoreType.DMA((7,))],
        compiler_params=pltpu.CompilerParams(vmem_limit_bytes=VMEM_LIMIT),
    )(*arrs)


def _adamw_small_call(ws, gs, ms, vs):
    n = len(ws)
    c1 = 1.0 - ADAM_B1 ** ADAM_STEP
    c2 = 1.0 - ADAM_B2 ** ADAM_STEP

    def body(*refs):
        for k in range(n):
            w_ref, g_ref, m_ref, v_ref = (refs[j * n + k] for j in range(4))
            d_ref, mo_ref, vo_ref = (refs[(4 + j) * n + k] for j in range(3))
            gv = g_ref[...]
            m2 = ADAM_B1 * m_ref[...] + (1.0 - ADAM_B1) * gv
            v2 = ADAM_B2 * v_ref[...] + (1.0 - ADAM_B2) * (gv * gv)
            d_ref[...] = -ADAM_LR * ((m2 / c1) / (jnp.sqrt(v2 / c2) + ADAM_EPS) + ADAM_WD * w_ref[...])
            mo_ref[...] = m2
            vo_ref[...] = v2

    vm = pl.BlockSpec(memory_space=pltpu.VMEM)
    res = pl.pallas_call(
        body, name="adamw_small", in_specs=[vm] * (4 * n), out_specs=[vm] * (3 * n),
        out_shape=[jax.ShapeDtypeStruct(a.shape, F32) for a in ws] * 3,
        compiler_params=pltpu.CompilerParams(vmem_limit_bytes=VMEM_LIMIT),
    )(*ws, *gs, *ms, *vs)
    return res[:n], res[n:2 * n], res[2 * n:]


def _pack(arrs, min_rows=8):
    parts = []
    for a in arrs:
        flat = a.reshape(-1).astype(F32)
        parts.append(jnp.pad(flat, (0, (-flat.shape[0]) % BLK)))
    buf = jnp.concatenate(parts).reshape(-1, BLK)
    return jnp.pad(buf, ((0, (-buf.shape[0]) % min_rows), (0, 0)))


def _unpack(buf, shapes):
    out, r = [], 0
    for shp in shapes:
        n = math.prod(shp)
        nr = -(-n // BLK)
        out.append(buf[r:r + nr].reshape(-1)[:n].reshape(shp))
        r += nr
    return out


SMALL = ["meta_tokens", "mix_pre_g", "ssd_conv_w", "ssd_conv_b", "ssd_dt_bias", "ssd_a_log", "ssd_d", "ssd_norm_g",
         "sb_norm_g", "mix_post_g", "ffn_pre_g", "ffn_conv_w", "ffn_conv_b", "ffn_post_g"]
BIG = ["w_in", "w_out", "w_up", "w_down"]
WEIGHTS = ["meta_tokens", "mix_pre_g", "w_in", "ssd_conv_w", "ssd_conv_b", "ssd_dt_bias", "ssd_a_log", "ssd_d",
           "ssd_norm_g", "sb_norm_g", "w_out", "mix_post_g", "ffn_pre_g", "w_up", "ffn_conv_w", "ffn_conv_b",
           "w_down", "ffn_post_g"]
W_IN_SHARD = IN_COLS // N_CHIPS
W_IN_PAD = 1536


def kernel(x, meta_tokens, mix_pre_g, w_in, ssd_conv_w, ssd_conv_b, ssd_dt_bias, ssd_a_log, ssd_d, ssd_norm_g, sb_norm_g, w_out, mix_post_g, ffn_pre_g, w_up, ffn_conv_w, ffn_conv_b, w_down, ffn_post_g, loss_target, m_meta_tokens, m_mix_pre_g, m_w_in, m_ssd_conv_w, m_ssd_conv_b, m_ssd_dt_bias, m_ssd_a_log, m_ssd_d, m_ssd_norm_g, m_sb_norm_g, m_w_out, m_mix_post_g, m_ffn_pre_g, m_w_up, m_ffn_conv_w, m_ffn_conv_b, m_w_down, m_ffn_post_g, v_meta_tokens, v_mix_pre_g, v_w_in, v_ssd_conv_w, v_ssd_conv_b, v_ssd_dt_bias, v_ssd_a_log, v_ssd_d, v_ssd_norm_g, v_sb_norm_g, v_w_out, v_mix_post_g, v_ffn_pre_g, v_w_up, v_ffn_conv_w, v_ffn_conv_b, v_w_down, v_ffn_post_g):
    w = dict(meta_tokens=meta_tokens, mix_pre_g=mix_pre_g, w_in=w_in, ssd_conv_w=ssd_conv_w, ssd_conv_b=ssd_conv_b, ssd_dt_bias=ssd_dt_bias, ssd_a_log=ssd_a_log, ssd_d=ssd_d, ssd_norm_g=ssd_norm_g, sb_norm_g=sb_norm_g, w_out=w_out, mix_post_g=mix_post_g, ffn_pre_g=ffn_pre_g, w_up=w_up, ffn_conv_w=ffn_conv_w, ffn_conv_b=ffn_conv_b, w_down=w_down, ffn_post_g=ffn_post_g)
    m = dict(meta_tokens=m_meta_tokens, mix_pre_g=m_mix_pre_g, w_in=m_w_in, ssd_conv_w=m_ssd_conv_w, ssd_conv_b=m_ssd_conv_b, ssd_dt_bias=m_ssd_dt_bias, ssd_a_log=m_ssd_a_log, ssd_d=m_ssd_d, ssd_norm_g=m_ssd_norm_g, sb_norm_g=m_sb_norm_g, w_out=m_w_out, mix_post_g=m_mix_post_g, ffn_pre_g=m_ffn_pre_g, w_up=m_w_up, ffn_conv_w=m_ffn_conv_w, ffn_conv_b=m_ffn_conv_b, w_down=m_w_down, ffn_post_g=m_ffn_post_g)
    v = dict(meta_tokens=v_meta_tokens, mix_pre_g=v_mix_pre_g, w_in=v_w_in, ssd_conv_w=v_ssd_conv_w, ssd_conv_b=v_ssd_conv_b, ssd_dt_bias=v_ssd_dt_bias, ssd_a_log=v_ssd_a_log, ssd_d=v_ssd_d, ssd_norm_g=v_ssd_norm_g, sb_norm_g=v_sb_norm_g, w_out=v_w_out, mix_post_g=v_mix_post_g, ffn_pre_g=v_ffn_pre_g, w_up=v_w_up, ffn_conv_w=v_ffn_conv_w, ffn_conv_b=v_ffn_conv_b, w_down=v_w_down, ffn_post_g=v_ffn_post_g)
    chip = 2 * lax.axis_index("x") + lax.axis_index("y")
    me = 2 * chip + lax.axis_index("c")
    place = jnp.stack([lax.axis_index("c"), chip] + [me ^ p for p in range(1, 8)]).astype(jnp.int32)

    shard_small = [w["meta_tokens"], w["ssd_conv_w"][0], w["ffn_conv_w"][0]]
    shards = [jnp.pad(w["w_in"][0], ((0, 0), (0, W_IN_PAD - W_IN_SHARD))).astype(BF16), _pack(shard_small, 16)]
    gathered = _allgather_call(shards)
    late_shards = [w["w_out"][0].astype(BF16), w["w_up"][0].astype(BF16), w["w_down"][0].astype(BF16)]

    def blocks(own, got):
        return [jnp.where(chip == i, own, got[i]) for i in range(N_CHIPS)]

    def late_weights(got):
        return (jnp.concatenate(blocks(late_shards[0], got[0]), axis=0),
                jnp.concatenate(blocks(late_shards[1], got[1]), axis=1),
                jnp.concatenate(blocks(late_shards[2], got[2]), axis=0))

    cut = DT_REAL_OFF + N_HEADS - W_IN_SHARD
    s_in = blocks(shards[0], gathered[0])
    w_in_c = jnp.concatenate(
        [s_in[0][:, :W_IN_SHARD], s_in[1][:, :cut], jnp.zeros((D_MODEL, BLK - N_HEADS), BF16),
         s_in[1][:, cut:W_IN_SHARD], s_in[2][:, :W_IN_SHARD], s_in[3][:, :W_IN_SHARD]], axis=1)
    parts = [_unpack(b, [s.shape for s in shard_small]) for b in blocks(shards[1], gathered[1])]
    wt = {k: w[k][0][None] if w[k].ndim == 3 else w[k] for k in
          ["mix_pre_g", "ssd_conv_b", "ssd_dt_bias", "ssd_a_log", "ssd_d", "ssd_norm_g", "sb_norm_g", "mix_post_g",
           "ffn_pre_g", "ffn_conv_b", "ffn_post_g"]}
    wt.update(
        meta=jnp.concatenate([p[0] for p in parts], axis=1),
        ssd_conv_w=jnp.concatenate([p[1] for p in parts], axis=1),
        ffn_conv_w=jnp.concatenate([p[2] for p in parts], axis=1), w_in=w_in_c)

    def w_in_shards(g):
        skip = BLK - N_HEADS
        cols = [g[:, :W_IN_SHARD],
                jnp.concatenate([g[:, W_IN_SHARD:W_IN_SHARD + cut], g[:, C_Q:2 * W_IN_SHARD + skip]], axis=1),
                g[:, 2 * W_IN_SHARD + skip:3 * W_IN_SHARD + skip], g[:, 3 * W_IN_SHARD + skip:]]
        g = jnp.stack([jnp.pad(b, ((0, 0), (0, W_IN_PAD - W_IN_SHARD))) for b in cols])
        return g, g.astype(BF16)

    loss_row, dx, small, pending = _local_step(x[0], loss_target[0], wt, late_shards, late_weights, w_in_shards)

    full = _half_exchange_call([_grad_sum_call(*pending[k], place, "grad_sum_" + k) for k in BIG])
    grads = {"w_in": full[0][:, :W_IN_SHARD], "w_out": full[1], "w_up": full[2], "w_down": full[3]}

    red_list = _allreduce_small_call([small[k] for k in SMALL] + [loss_row])
    loss = jnp.sum(red_list[-1])
    for k, g in zip(SMALL, red_list[:-1]):
        grads[k] = g
    for k in ["meta_tokens", "ssd_conv_w", "ffn_conv_w"]:
        wk = w[k].shape[-1]
        grads[k] = lax.dynamic_slice_in_dim(grads[k], chip * wk, wk, axis=1)

    delta, new_m, new_v = {}, {}, {}
    for k in BIG:
        delta[k], new_m[k], new_v[k] = _adamw_call(w[k][0], grads[k], m[k][0], v[k][0], "adamw_" + k)
    flat = lambda d: [d[k].reshape(grads[k].shape) for k in SMALL]
    res = _adamw_small_call(flat(w), [grads[k] for k in SMALL], flat(m), flat(v))
    for out, arrs in zip((delta, new_m, new_v), res):
        for k, a in zip(SMALL, arrs):
            out[k] = a

    def shaped(d, k):
        return d[k].reshape(w[k].shape)

    return (loss, dx[None], *[shaped(grads, k) for k in WEIGHTS], *[shaped(delta, k) for k in WEIGHTS],
            *[shaped(new_m, k) for k in WEIGHTS], *[shaped(new_v, k) for k in WEIGHTS])
```

```python
import functools
import math

import jax
import jax.numpy as jnp
from jax import lax
from jax.experimental import pallas as pl
from jax.experimental.pallas import tpu as pltpu

F32 = jnp.float32
BF16 = jnp.bfloat16

D_MODEL = 1024
N_META = 16
BLK = 128
PAD = BLK - N_META
HEAD_DIM = 64
N_HEADS = 16
SSD_GROUPS = 2
SSD_STATE = 128
SSD_INNER = 1024
XBC = SSD_INNER + 2 * SSD_GROUPS * SSD_STATE
D_FF = 2816
EPS = 1e-6
IN_COLS = 5648
C_Z, C_XBC, C_DT, C_Q, C_K, C_V, C_END = 0, 1024, 2560, 2688, 3712, 4736, 5760
DT_REAL_OFF = 2560
N_CHIPS = 4
ADAM_LR, ADAM_B1, ADAM_B2, ADAM_EPS, ADAM_WD, ADAM_STEP = 0.001, 0.9, 0.999, 1e-08, 0.01, 10
VMEM_LIMIT = 56 * 1024 * 1024
MESH = pl.DeviceIdType.MESH


def _cparams(sem=None, **kw):
    if sem is not None:
        kw["dimension_semantics"] = sem
    return pltpu.CompilerParams(vmem_limit_bytes=VMEM_LIMIT, **kw)


def _pick(n, cands):
    for c in cands:
        if n % c == 0:
            return c
    raise ValueError((n, cands))


def _iota(shape, dim):
    return lax.broadcasted_iota(jnp.int32, shape, dim)


def _sigmoid(x):
    return 1.0 / (1.0 + jnp.exp(-x))


def _split2(v):
    h1 = v.astype(BF16)
    return h1, (v - h1.astype(F32)).astype(BF16)


def _dot(a, b, ca=1, cb=0):
    return lax.dot_general(a, b, (((ca,), (cb,)), ((), ())), preferred_element_type=F32)


def _dot_sel_r(v, sel, cb=0):
    h1, h2 = _split2(v)
    return _dot(h1, sel, 1, cb) + _dot(h2, sel, 1, cb)


def _dot_sel_l(sel, v, ca=1):
    h1, h2 = _split2(v)
    return _dot(sel, h1, ca, 0) + _dot(sel, h2, ca, 0)


def _mm(a, b, *, ta=False, tb=False, tm, tn, tk, out_dtype=F32, nsplit=1, extra_bf16=False, ride=None, name):
    K, M = (a.shape if ta else a.shape[::-1])
    N = b.shape[0] if tb else b.shape[1]
    assert M % tm == 0 and N % tn == 0 and K % tk == 0, (name, M, N, K, tm, tn, tk)
    nm, nn, nk = M // tm, N // tn, K // tk
    assert nn % nsplit == 0
    per = nn // nsplit
    a_spec = (pl.BlockSpec((tk, tm), lambda i, j, k: (k, i)) if ta
              else pl.BlockSpec((tm, tk), lambda i, j, k: (i, k)))
    b_spec = (pl.BlockSpec((tn, tk), lambda i, j, k: (j, k)) if tb
              else pl.BlockSpec((tk, tn), lambda i, j, k: (k, j)))
    o_spec = pl.BlockSpec((None, tm, tn), lambda i, j, k: (j // per, i, j % per))
    n_out = 2 if extra_bf16 else 1
    ca, cb = (0 if ta else 1), (1 if tb else 0)
    ns, nl = (len(ride[0]), len(ride[1])) if ride else (0, 0)

    def body(a_ref, b_ref, *rest):
        outs = rest[ns:ns + n_out]
        if ride:
            step = (pl.program_id(0) * nn + pl.program_id(1)) * nk + pl.program_id(2)
            _ride_run(ride, rest[:ns], rest[ns + n_out:ns + n_out + nl], rest[-2], rest[-1],
                      step == 0, step == nm * nn * nk - 1)
        p = _dot(a_ref[...].astype(BF16), b_ref[...].astype(BF16), ca, cb)

        def emit(val):
            outs[0][...] = val.astype(out_dtype)
            if extra_bf16:
                outs[1][...] = val.astype(BF16)

        if nk == 1:
            emit(p)
        else:
            acc = rest[ns + n_out + nl]
            k = pl.program_id(2)

            @pl.when(k == 0)
            def _():
                acc[...] = p

            @pl.when(k > 0)
            def _():
                acc[...] += p

            @pl.when(k == nk - 1)
            def _():
                emit(acc[...])

    shp = (nsplit, M, N // nsplit)
    out_shape = [jax.ShapeDtypeStruct(shp, out_dtype)]
    out_specs = [o_spec]
    if extra_bf16:
        out_shape.append(jax.ShapeDtypeStruct(shp, BF16))
        out_specs.append(o_spec)
    scratch = [pltpu.VMEM((tm, tn), F32)] if nk > 1 else []
    if ride:
        res = pl.pallas_call(
            body, name=name, grid=(nm, nn, nk), in_specs=[a_spec, b_spec] + [ANY] * ns,
            out_specs=out_specs + [ANY] * nl, out_shape=out_shape + list(ride[1]),
            scratch_shapes=scratch + _ride_scratch(ride),
            compiler_params=_cparams(("arbitrary", "arbitrary", "arbitrary")),
        )(a, b, *ride[0])
        return (res[:n_out] if extra_bf16 else res[0]), list(res[n_out:])
    res = pl.pallas_call(
        body, name=name, grid=(nm, nn, nk), in_specs=[a_spec, b_spec], out_specs=out_specs,
        out_shape=out_shape, scratch_shapes=scratch,
        compiler_params=_cparams(("parallel", "parallel", "arbitrary")),
    )(a, b)
    return res if extra_bf16 else res[0]


def _rms_stats(x):
    r = lax.rsqrt(jnp.mean(x * x, axis=-1, keepdims=True) + EPS)
    return r, x * r


def _rms_bwd(x, g, dy):
    r, xh = _rms_stats(x)
    dxh = dy * g
    dx = r * (dxh - xh * jnp.mean(dxh * xh, axis=-1, keepdims=True))
    return dx, jnp.sum(dy * xh, axis=0, keepdims=True)


def _row_spec(tr, w, col=0):
    return pl.BlockSpec((tr, w), lambda i: (i, col))


def _vec_spec(w):
    return pl.BlockSpec((1, w), lambda i: (0, 0))


def _acc_rows(ref, val, i):
    @pl.when(i == 0)
    def _():
        ref[...] = val

    @pl.when(i > 0)
    def _():
        ref[...] += val


def _rms_fwd_call(x, g, name, beside=None):
    lp, w = x.shape
    tr = _pick(lp, [384, 128])

    def body(x_ref, g_ref, *rest):
        _, xh = _rms_stats(x_ref[...])
        rest[-1][...] = (xh * g_ref[...]).astype(BF16)

    if beside is None:
        return pl.pallas_call(
            body, name=name, grid=(lp // tr,), in_specs=[_row_spec(tr, w), _vec_spec(w)],
            out_specs=_row_spec(tr, w), out_shape=jax.ShapeDtypeStruct((lp, w), BF16),
            compiler_params=_cparams(("parallel",)))(x, g)
    return pl.pallas_call(
        body, name=name, grid=(lp // tr,), in_specs=[_row_spec(tr, w), _vec_spec(w), ANY],
        out_specs=_row_spec(tr, w, 1), out_shape=jax.ShapeDtypeStruct((lp, 2 * w), BF16),
        input_output_aliases={2: 0}, compiler_params=_cparams(("parallel",)))(x, g, beside)


def _mid_fwd_call(h0, mix, g_post, g_pre2):
    lp, w = h0.shape
    tr = _pick(lp, [384, 128])

    def body(h0_ref, mix_ref, gp_ref, g2_ref, h1_ref, xn_ref):
        _, mh = _rms_stats(mix_ref[...])
        h1 = h0_ref[...] + mh * gp_ref[...]
        h1_ref[...] = h1
        _, hh = _rms_stats(h1)
        xn_ref[...] = (hh * g2_ref[...]).astype(BF16)

    return pl.pallas_call(
        body, name="mid_fwd", grid=(lp // tr,),
        in_specs=[_row_spec(tr, w), _row_spec(tr, w), _vec_spec(w), _vec_spec(w)],
        out_specs=[_row_spec(tr, w), _row_spec(tr, w)],
        out_shape=[jax.ShapeDtypeStruct((lp, w), F32), jax.ShapeDtypeStruct((lp, w), BF16)],
        compiler_params=_cparams(("parallel",)))(h0, mix, g_post, g_pre2)


def _final_call(h1, f, g_post, target):
    lp, w = h1.shape
    tr = BLK
    nb = lp // tr

    def body(h1_ref, f_ref, g_ref, t_ref, loss_ref, df_ref, dh_ref, dg_ref):
        i = pl.program_id(0)
        fv = f_ref[...]
        g = g_ref[...]
        _, fh = _rms_stats(fv)
        h2 = h1_ref[...] + fh * g
        diff = jnp.where(i > 0, h2 - t_ref[...], 0.0)
        part = 0.5 * jnp.sum(diff * diff, axis=0, keepdims=True) * (1.0 / w)
        _acc_rows(loss_ref, part, i)
        dh = diff * (1.0 / w)
        dh_ref[...] = dh
        df, dg = _rms_bwd(fv, g, dh)
        df_ref[...] = df.astype(BF16)
        _acc_rows(dg_ref, dg, i)

    t_spec = pl.BlockSpec((tr, w), lambda i: (jnp.maximum(i - 1, 0), 0))
    return pl.pallas_call(
        body, name="final_fwd_bwd", grid=(nb,),
        in_specs=[_row_spec(tr, w), _row_spec(tr, w), _vec_spec(w), t_spec],
        out_specs=[_vec_spec(w), _row_spec(tr, w), _row_spec(tr, w), _vec_spec(w)],
        out_shape=[jax.ShapeDtypeStruct((1, w), F32), jax.ShapeDtypeStruct((lp, w), BF16),
                   jax.ShapeDtypeStruct((lp, w), F32), jax.ShapeDtypeStruct((1, w), F32)],
        compiler_params=_cparams(("arbitrary",)))(h1, f, g_post, target)


def _mid_bwd_call(dh2, h1, dxn2, mix, g_pre2, g_post):
    lp, w = h1.shape
    tr = _pick(lp, [384, 128])

    def body(dh2_ref, h1_ref, dxn_ref, mix_ref, g2_ref, gp_ref, dh1_ref, dmix_ref, dg2_ref, dgp_ref):
        i = pl.program_id(0)
        live = (i * tr + _iota((tr, 1), 0)) >= PAD
        dx, dg2 = _rms_bwd(h1_ref[...], g2_ref[...], dxn_ref[...])
        dh1 = jnp.where(live, dh2_ref[...] + dx, 0.0)
        dh1_ref[...] = dh1
        dmix, dgp = _rms_bwd(mix_ref[...], gp_ref[...], dh1)
        dmix_ref[...] = jnp.where(live, dmix, 0.0).astype(BF16)
        _acc_rows(dg2_ref, dg2, i)
        _acc_rows(dgp_ref, dgp, i)

    rs = _row_spec(tr, w)
    return pl.pallas_call(
        body, name="mid_bwd", grid=(lp // tr,),
        in_specs=[rs, rs, rs, rs, _vec_spec(w), _vec_spec(w)],
        out_specs=[rs, rs, _vec_spec(w), _vec_spec(w)],
        out_shape=[jax.ShapeDtypeStruct((lp, w), F32), jax.ShapeDtypeStruct((lp, w), BF16),
                   jax.ShapeDtypeStruct((1, w), F32), jax.ShapeDtypeStruct((1, w), F32)],
        compiler_params=_cparams(("arbitrary",)))(dh2, h1, dxn2, mix, g_pre2, g_post)


def _norm_bwd_call(x, g, dy_arr, dy_col, name, res=None, ride=None, land=None):
    lp, w = x.shape
    tr = _pick(lp, [384, 128])
    nsteps = lp // tr
    has_res = res is not None
    ns = len(ride[0]) if ride else 0
    first_out = (1 if has_res else 0) + (ns + 1 if ride else 0)

    def body(x_ref, g_ref, dy_ref, *rest):
        i = pl.program_id(0)
        if ride:
            srcs = rest[first_out - ns - 1:first_out - 1]
            _ride_run(ride, srcs, [rest[first_out + 2]], rest[-2], rest[-1], i == 0, i == nsteps - 1)
        live = (i * tr + _iota((tr, 1), 0)) >= PAD
        dx, dg = _rms_bwd(x_ref[...], g_ref[...], dy_ref[...])
        if has_res:
            dx = dx + rest[0][...]
        out_ref, dg_ref = rest[first_out], rest[first_out + 1]
        out_ref[...] = jnp.where(live, dx, 0.0)
        _acc_rows(dg_ref, dg, i)

    rs = _row_spec(tr, w)
    ins = [rs, _vec_spec(w), _row_spec(tr, w, dy_col)] + ([rs] if has_res else [])
    args = [x, g, dy_arr] + ([res] if has_res else [])
    outs = [rs, _vec_spec(w)]
    out_shape = [jax.ShapeDtypeStruct((lp, w), F32), jax.ShapeDtypeStruct((1, w), F32)]
    if not ride:
        return pl.pallas_call(
            body, name=name, grid=(nsteps,), in_specs=ins, out_specs=outs, out_shape=out_shape,
            compiler_params=_cparams(("arbitrary",)))(*args)
    return pl.pallas_call(
        body, name=name, grid=(nsteps,), in_specs=ins + [ANY] * (ns + 1), out_specs=outs + [ANY],
        out_shape=out_shape + [jax.ShapeDtypeStruct(land.shape, land.dtype)],
        input_output_aliases={len(args) + ns: 2}, scratch_shapes=_ride_scratch(ride),
        compiler_params=_cparams(("arbitrary",)))(*args, *ride[0], land)


def _shift_down(cur, prev_tail, s, rows):
    if s == 0:
        return cur
    prev = jnp.tile(prev_tail, (cur.shape[0] // 8, 1))
    return jnp.where(rows >= s, pltpu.roll(cur, s, 0), pltpu.roll(prev, s, 0))


def _shift_up(cur, next_head, s, rows):
    if s == 0:
        return cur
    n = cur.shape[0]
    nxt = jnp.tile(next_head, (n // 8, 1))
    return jnp.where(rows < n - s, pltpu.roll(cur, n - s, 0), pltpu.roll(nxt, n - s, 0))


def _gelu_tanh(x):
    c = math.sqrt(2.0 / math.pi)
    t = jnp.tanh(c * (x + 0.044715 * x * x * x))
    return 0.5 * x * (1.0 + t), t


def _conv_fwd_call(src, col0, width, cw, w8, b, taps, *, gate_src=None, gate_col0=0, rb=BLK, name):
    lp = src.shape[0]
    nb, nc = lp // rb, width // cw
    cb0 = col0 // cw
    ffn = gate_src is not None

    def body(x_ref, w_ref, b_ref, *rest):
        if ffn:
            u_ref, y_ref, a_ref, tail = rest
        else:
            y_ref, a_ref, tail = rest
        i = pl.program_id(1)

        @pl.when(i == 0)
        def _():
            tail[...] = jnp.zeros_like(tail)

        cur = x_ref[...]
        rows = _iota((rb, cw), 0)
        y = b_ref[...] + w_ref[taps - 1:taps, :] * cur
        pt = tail[...]
        for s in range(1, taps):
            y = y + w_ref[taps - 1 - s:taps - s, :] * _shift_down(cur, pt, s, rows)
        tail[...] = cur[rb - 8:, :]
        y_ref[...] = y
        if ffn:
            ge, _ = _gelu_tanh(y)
            a_ref[...] = (ge * u_ref[...]).astype(BF16)
        else:
            live = (i * rb + rows) >= PAD
            a_ref[...] = jnp.where(live, y * _sigmoid(y), 0.0)

    blk = lambda c0: pl.BlockSpec((rb, cw), lambda j, i: (i, c0 + j))
    ins = [blk(cb0), pl.BlockSpec((8, cw), lambda j, i: (0, j)), pl.BlockSpec((1, cw), lambda j, i: (0, j))]
    args = [src, w8, b]
    if ffn:
        ins.append(blk(gate_col0 // cw))
        args.append(gate_src)
    return pl.pallas_call(
        body, name=name, grid=(nc, nb), in_specs=ins, out_specs=[blk(0), blk(0)],
        out_shape=[jax.ShapeDtypeStruct((lp, width), F32),
                   jax.ShapeDtypeStruct((lp, width), BF16 if ffn else F32)],
        scratch_shapes=[pltpu.VMEM((8, cw), F32)],
        compiler_params=_cparams(("parallel", "arbitrary")))(*args)


def _conv_bwd_call(src, col0, width, cw, w8, taps, ypre, dact, *, gate_src=None, gate_col0=0, rb=BLK, name):
    lp = src.shape[0]
    nb, nc = lp // rb, width // cw
    cb0 = col0 // cw
    ffn = gate_src is not None

    def body(x_ref, w_ref, y_ref, d_ref, *rest):
        if ffn:
            u_ref, dx_ref, du_ref, dw_ref, db_ref, head = rest
        else:
            dx_ref, dw_ref, db_ref, head = rest
        step = pl.program_id(1)
        i = nb - 1 - step

        @pl.when(step == 0)
        def _():
            head[...] = jnp.zeros_like(head)

        rows = _iota((rb, cw), 0)
        live = (i * rb + rows) >= PAD
        y = y_ref[...]
        d = d_ref[...]
        if ffn:
            ge, t = _gelu_tanh(y)
            c = math.sqrt(2.0 / math.pi)
            dge = 0.5 * (1.0 + t) + 0.5 * y * (1.0 - t * t) * c * (1.0 + 3.0 * 0.044715 * y * y)
            u = u_ref[...]
            du_ref[...] = jnp.where(live, d * ge, 0.0).astype(BF16)
            dy = jnp.where(live, d * u * dge, 0.0)
        else:
            sg = _sigmoid(y)
            dy = jnp.where(live, d * sg * (1.0 + y * (1.0 - sg)), 0.0)
        x = x_ref[...]
        nh = head[...]
        dx = jnp.zeros_like(dy)
        dws = []
        for s in range(taps):
            sh = _shift_up(dy, nh, s, rows)
            dx = dx + w_ref[taps - 1 - s:taps - s, :] * sh
            dws.append(jnp.sum(x * sh, axis=0, keepdims=True))
        head[...] = dy[:8, :]
        dx_ref[...] = jnp.where(live, dx, 0.0).astype(BF16)
        dw = jnp.concatenate([dws[taps - 1 - k] for k in range(taps)]
                             + [jnp.zeros((8 - taps, cw), F32)], axis=0)
        _acc_rows(dw_ref, dw, step)
        _acc_rows(db_ref, jnp.sum(dy, axis=0, keepdims=True), step)

    blk = lambda c0: pl.BlockSpec((rb, cw), lambda j, s: (nb - 1 - s, c0 + j))
    ins = [blk(cb0), pl.BlockSpec((8, cw), lambda j, s: (0, j)), blk(0), blk(0)]
    args = [src, w8, ypre, dact]
    outs = [blk(0)]
    oshape = [jax.ShapeDtypeStruct((lp, width), BF16)]
    if ffn:
        ins.append(blk(gate_col0 // cw))
        args.append(gate_src)
        outs.append(blk(0))
        oshape.append(jax.ShapeDtypeStruct((lp, width), BF16))
    outs += [pl.BlockSpec((8, cw), lambda j, s: (0, j)), pl.BlockSpec((1, cw), lambda j, s: (0, j))]
    oshape += [jax.ShapeDtypeStruct((8, width), F32), jax.ShapeDtypeStruct((1, width), F32)]
    return pl.pallas_call(
        body, name=name, grid=(nc, nb), in_specs=ins, out_specs=outs, out_shape=oshape,
        scratch_shapes=[pltpu.VMEM((8, cw), F32)],
        compiler_params=_cparams(("parallel", "arbitrary")))(*args)


SB_FIRST = 3
SB_GROUP = 4
SB_DEAD = -110.0


def _sb_scores(qm_h, kb):
    z = _dot(qm_h, kb, 1, 1)
    sp = jnp.maximum(z, 0.0) + jnp.log(1.0 + jnp.exp(-jnp.abs(z)))
    return z - sp, -sp


def _dot_tri1(v, tri2):
    r = _dot(v.astype(BF16), tri2[:BLK])
    return r[:, :BLK], r[:, BLK:]


def _tri2(cond):
    t = jnp.concatenate([cond.astype(BF16), jnp.ones((BLK, BLK), BF16)], axis=1)
    return jnp.concatenate([t, t], axis=0)


def _dot_tri(v, tri2):
    hi = v.astype(BF16)
    lo = (v - hi.astype(F32)).astype(BF16)
    r = _dot(jnp.concatenate([hi, lo], axis=1), tri2)
    return r[:, :BLK], r[:, BLK:]


def _sb_fwd_call(proj, ride):
    lp = proj.shape[0]
    nb = lp // BLK
    scale = 1.0 / math.sqrt(HEAD_DIM)

    ns, nl = len(ride[0]), len(ride[1])
    npair = N_HEADS // 2

    def body(q_ref, k_ref, v_ref, *rest):
        o_ref, tl_ref = rest[ns], rest[ns + 1]
        i = pl.program_id(1)
        step = pl.program_id(0) * nb + i
        _ride_run(ride, rest[:ns], rest[ns + 2:ns + 2 + nl], rest[-2], rest[-1], step == 0, step == npair * nb - 1)
        lane = _iota((2 * BLK, BLK), 1)
        row = _iota((2 * BLK, BLK), 0)
        first = row < BLK
        qrow = row & (BLK - 1)
        q = q_ref[...] * scale
        q2 = jnp.concatenate([q, q], axis=0)
        qm = jnp.where(first == (lane < HEAD_DIM), q2, 0.0).astype(BF16)
        tri = _tri2(_iota((BLK, BLK), 0) > _iota((BLK, BLK), 1))

        def chunk(off, nsub, last_valid, carry):
            width = nsub * BLK
            sls = [slice(b * BLK, (b + 1) * BLK) for b in range(nsub)]
            kb = k_ref[pl.ds(off, width), :].astype(BF16)
            vb = v_ref[pl.ds(off, width), :].astype(BF16)
            lb, lk = _sb_scores(qm, kb)
            lks = [lk[:, sl] for sl in sls]
            first_valid = (off + lane) >= PAD
            lks[0] = jnp.where(first_valid, lks[0], 0.0)
            if last_valid is not None:
                lks[-1] = jnp.where(last_valid, lks[-1], 0.0)
            afters = [_dot_tri(lks[b], tri) for b in range(nsub)]
            run, acc = carry
            ws = [None] * nsub
            for b in reversed(range(nsub)):
                wb = jnp.exp(lb[:, sls[b]] + afters[b][0] + run)
                if b == 0:
                    wb = jnp.where(first_valid, wb, 0.0)
                if last_valid is not None and b == nsub - 1:
                    wb = jnp.where(last_valid, wb, 0.0)
                ws[b] = wb.astype(BF16)
                run = run + afters[b][1]
            w = ws[0] if nsub == 1 else jnp.concatenate(ws, axis=1)
            return run, acc + _dot(w, vb)

        before = jnp.minimum(i, SB_FIRST - 1)
        first_off = pl.multiple_of((i - before) * BLK, BLK)
        diag = lane < qrow
        zero = jnp.zeros((2 * BLK, BLK), F32)
        carry = lax.switch(before, [functools.partial(chunk, first_off, n, diag) for n in range(1, SB_FIRST + 1)],
                           (zero, zero))

        def walk(n):
            def body(state):
                off = pl.multiple_of((state[0] - (n - 1)) * BLK, BLK)
                return (state[0] - n, *chunk(off, n, None, state[1:]))

            def cond(state):
                return jnp.logical_and(state[0] >= n - 1, jnp.max(state[1]) > SB_DEAD)

            return cond, body

        state = lax.while_loop(*walk(SB_GROUP), (i - SB_FIRST, *carry))
        pos, run, acc = lax.while_loop(*walk(1), state)
        low = lane[:BLK] < HEAD_DIM
        o_ref[...] = jnp.where(low, acc[:BLK], acc[BLK:])
        tl = jnp.where(low, run[:BLK], run[BLK:])
        tl_ref[...] = jnp.where(lane[:BLK] == 1, jnp.maximum(pos + 1, 0).astype(F32), tl)

    qc, kc, vc = C_Q // BLK, C_K // BLK, C_V // BLK
    blk = pl.BlockSpec((BLK, BLK), lambda p, i: (i, p))
    res = pl.pallas_call(
        body, name="sb_fwd", grid=(npair, nb),
        in_specs=[pl.BlockSpec((BLK, BLK), lambda p, i: (i, qc + p)),
                  pl.BlockSpec((lp, BLK), lambda p, i: (0, kc + p)),
                  pl.BlockSpec((lp, BLK), lambda p, i: (0, vc + p))] + [ANY] * ns,
        out_specs=[blk, blk] + [ANY] * nl,
        out_shape=[jax.ShapeDtypeStruct((lp, N_HEADS * HEAD_DIM), F32)] * 2 + list(ride[1]),
        scratch_shapes=_ride_scratch(ride),
        compiler_params=_cparams(("arbitrary", "arbitrary")))(proj, proj, proj, *ride[0])
    return res[0], res[1], list(res[2:])


def _sb_bwd_call(proj, tl, do, ride):
    lp = proj.shape[0]
    nb = lp // BLK
    scale = 1.0 / math.sqrt(HEAD_DIM)

    ns, nl = len(ride[0]), len(ride[1])
    npair = N_HEADS // 2

    def body(q_ref, k_ref, v_ref, tl_ref, do_ref, *rest):
        dq_ref, dk_ref, dv_ref = rest[ns:ns + 3]
        dk_acc, dv_acc = rest[ns + 3 + nl:ns + 5 + nl]
        i = pl.program_id(1)
        step = pl.program_id(0) * nb + i
        _ride_run(ride, rest[:ns], rest[ns + 3:ns + 3 + nl], rest[-2], rest[-1], step == 0, step == npair * nb - 1)

        @pl.when(i == 0)
        def _():
            dk_acc[...] = jnp.zeros_like(dk_acc)
            dv_acc[...] = jnp.zeros_like(dv_acc)

        lane = _iota((2 * BLK, BLK), 1)
        row = _iota((2 * BLK, BLK), 0)
        qrow = row & (BLK - 1)
        mine = (row < BLK) == (lane < HEAD_DIM)
        q = q_ref[...] * scale
        dov = do_ref[...]
        qm = jnp.where(mine, jnp.concatenate([q, q], axis=0), 0.0).astype(BF16)
        dom = jnp.where(mine, jnp.concatenate([dov, dov], axis=0), 0.0).astype(BF16)
        tlv = tl_ref[...]
        tot = jnp.concatenate([jnp.broadcast_to(tlv[:, 0:1], (BLK, BLK)),
                               jnp.broadcast_to(tlv[:, HEAD_DIM:HEAD_DIM + 1], (BLK, BLK))], axis=0)
        r1, l1 = _iota((BLK, BLK), 0), _iota((BLK, BLK), 1)
        tri_in = _tri2(r1 <= l1)
        tri_ex = _tri2(r1 < l1)

        def chunk(off, nsub, last_valid, carry):
            width = nsub * BLK
            sls = [slice(b * BLK, (b + 1) * BLK) for b in range(nsub)]
            cat = lambda parts: parts[0] if nsub == 1 else jnp.concatenate(parts, axis=1)
            mask_last = lambda b: last_valid is not None and b == nsub - 1
            kb = k_ref[pl.ds(off, width), :].astype(BF16)
            vb = v_ref[pl.ds(off, width), :].astype(BF16)
            lb, lk = _sb_scores(qm, kb)
            dw = _dot(dom, vb, 1, 1)
            lks = [lk[:, sl] for sl in sls]
            first_valid = (off + lane) >= PAD
            lks[0] = jnp.where(first_valid, lks[0], 0.0)
            if last_valid is not None:
                lks[-1] = jnp.where(last_valid, lks[-1], 0.0)
            pins = [_dot_tri(lks[b], tri_in) for b in range(nsub)]
            run, gsum, dq = carry
            ws, gs = [], []
            for b in range(nsub):
                wb = jnp.exp(lb[:, sls[b]] + (tot - run - pins[b][0]))
                if b == 0:
                    wb = jnp.where(first_valid, wb, 0.0)
                if mask_last(b):
                    wb = jnp.where(last_valid, wb, 0.0)
                ws.append(wb.astype(BF16))
                gs.append(wb * dw[:, sls[b]])
                run = run + pins[b][1]
            gexs = [_dot_tri1(gs[b], tri_ex) for b in range(nsub)]
            beta = jnp.exp(lb)
            parts = []
            for b in range(nsub):
                bt = beta[:, sls[b]]
                dzb = gs[b] * (1.0 - bt) - (gsum + gexs[b][0]) * bt
                if b == 0:
                    dzb = jnp.where(first_valid, dzb, 0.0)
                if mask_last(b):
                    dzb = jnp.where(last_valid, dzb, 0.0)
                parts.append(dzb.astype(BF16))
                gsum = gsum + gexs[b][1]
            dz, w = cat(parts), cat(ws)
            dk_acc[pl.ds(off, width), :] += _dot(dz, qm, 0, 0)
            dv_acc[pl.ds(off, width), :] += _dot(w, dom, 0, 0)
            return run, gsum, dq + _dot(dz, kb)

        diag = lane < qrow
        zero = jnp.zeros((2 * BLK, BLK), F32)
        top = i - SB_FIRST

        def walk(n):
            def body(state):
                off = pl.multiple_of(state[0] * BLK, BLK)
                return (state[0] + n, *chunk(off, n, None, state[1:]))

            return (lambda state: state[0] + (n - 1) <= top), body

        state = (jnp.max(tlv[:, 1:2]).astype(jnp.int32), zero, zero, zero)
        state = lax.while_loop(*walk(SB_GROUP), state)
        carry = lax.while_loop(*walk(1), state)[1:]
        before = jnp.minimum(i, SB_FIRST - 1)
        first_off = pl.multiple_of((i - before) * BLK, BLK)
        dq = lax.switch(before, [functools.partial(chunk, first_off, n, diag) for n in range(1, SB_FIRST + 1)],
                        carry)[2]
        dq_ref[...] = (jnp.where(lane[:BLK] < HEAD_DIM, dq[:BLK], dq[BLK:]) * scale).astype(BF16)

        @pl.when(i == nb - 1)
        def _():
            dk_ref[...] = dk_acc[...].astype(BF16)
            dv_ref[...] = dv_acc[...].astype(BF16)

    qc, kc, vc = C_Q // BLK, C_K // BLK, C_V // BLK
    blk = pl.BlockSpec((BLK, BLK), lambda p, i: (i, p))
    full = pl.BlockSpec((lp, BLK), lambda p, i: (0, p))
    w = N_HEADS * HEAD_DIM
    res = pl.pallas_call(
        body, name="sb_bwd", grid=(npair, nb),
        in_specs=[pl.BlockSpec((BLK, BLK), lambda p, i: (i, qc + p)),
                  pl.BlockSpec((lp, BLK), lambda p, i: (0, kc + p)),
                  pl.BlockSpec((lp, BLK), lambda p, i: (0, vc + p)),
                  blk, blk] + [ANY] * ns,
        out_specs=[blk, full, full] + [ANY] * nl,
        out_shape=[jax.ShapeDtypeStruct((lp, w), BF16)] * 3 + list(ride[1]),
        scratch_shapes=[pltpu.VMEM((lp, BLK), F32), pltpu.VMEM((lp, BLK), F32)] + _ride_scratch(ride),
        compiler_params=_cparams(("arbitrary", "arbitrary")))(proj, proj, proj, tl, do, *ride[0])
    return res[0], res[1], res[2], list(res[3:])


def _log1p(e):
    u = 1.0 + e
    return jnp.where(u == 1.0, e, jnp.log(u) * e / jnp.where(u == 1.0, 1.0, u - 1.0))


def _ssd_common(c, dtr, bias, alog):
    row = _iota((BLK, BLK), 0)
    lane = _iota((BLK, BLK), 1)
    live = ((c * BLK + row) >= PAD) & (lane < N_HEADS)
    pre = dtr + bias
    dt = jnp.where(live, jnp.maximum(pre, 0.0) + _log1p(jnp.exp(-jnp.abs(pre))), 0.0)
    a_neg = -jnp.exp(alog)
    a = dt * a_neg
    t_in = (lane <= row).astype(BF16)
    cs = _dot_sel_l(t_in, a)
    cs_t = cs.T
    cs_end = cs[BLK - 1:BLK, :]
    e = jnp.exp(cs)
    f = jnp.exp(cs_end - cs)
    xp = ((_iota((BLK, SSD_INNER), 1) // HEAD_DIM) == _iota((BLK, SSD_INNER), 0)).astype(BF16)
    xp_t = ((_iota((SSD_INNER, BLK), 0) // HEAD_DIM) == _iota((SSD_INNER, BLK), 1)).astype(BF16)
    decay_col = _dot_sel_l(xp_t, jnp.exp(cs_t))[:, BLK - 1:BLK]
    return dict(live=live, pre=pre, dt=dt, a_neg=a_neg, cs=cs, cs_t=cs_t, e=e, f=f, xp=xp, xp_t=xp_t,
                decay_col=decay_col, row=row, lane=lane,
                dt_x=_dot_sel_r(dt, xp), e_x=_dot_sel_r(e, xp), f_x=_dot_sel_r(f, xp))


def _ssd_ldec(q, h):
    diff = q["cs"][:, h:h + 1] - q["cs_t"][h:h + 1, :]
    causal = q["row"] >= q["lane"]
    return jnp.where(causal, jnp.exp(jnp.where(causal, diff, 0.0)), 0.0)


def _ssd_fwd_call(xbc, proj, bias, alog, d_x, norm_g):
    lp = xbc.shape[0]
    nb = lp // BLK
    gw = SSD_INNER // SSD_GROUPS
    ppg = gw // BLK

    def body(xbc_ref, dtr_ref, z_ref, bias_ref, alog_ref, dx_ref, ng_ref, yb_ref, ypre_ref, sprev_ref, s_ref):
        c = pl.program_id(0)

        @pl.when(c == 0)
        def _():
            s_ref[...] = jnp.zeros_like(s_ref)

        q = _ssd_common(c, dtr_ref[...], bias_ref[...], alog_ref[...])
        x = xbc_ref[:, 0:SSD_INNER]
        xd = x * q["dt_x"]
        low = q["lane"] < HEAD_DIM
        s_old = s_ref[...]
        sprev_ref[...] = s_old
        xdf = (xd * q["f_x"]).astype(BF16)
        for g in range(SSD_GROUPS):
            bg = xbc_ref[:, SSD_INNER + g * SSD_STATE:SSD_INNER + (g + 1) * SSD_STATE].astype(BF16)
            cg = xbc_ref[:, SSD_INNER + (SSD_GROUPS + g) * SSD_STATE:
                         SSD_INNER + (SSD_GROUPS + g + 1) * SSD_STATE].astype(BF16)
            cb = _dot(cg, bg, 1, 1)
            gs = slice(g * gw, (g + 1) * gw)
            y_off = _dot(cg, s_old[gs, :].astype(BF16), 1, 1) * q["e_x"][:, gs]
            s_ref[gs, :] = s_old[gs, :] * q["decay_col"][gs, :] + _dot(xdf[:, gs], bg, 0, 0)
            for pr in range(ppg):
                cols = slice(g * gw + pr * BLK, g * gw + (pr + 1) * BLK)
                xd_p = xd[:, cols]
                acc = y_off[:, pr * BLK:(pr + 1) * BLK]
                for hh in range(2):
                    h = (g * gw + pr * BLK) // HEAD_DIM + hh
                    m = (cb * _ssd_ldec(q, h)).astype(BF16)
                    xm = jnp.where(low, xd_p, 0.0) if hh == 0 else jnp.where(low, 0.0, xd_p)
                    acc = acc + _dot(m, xm.astype(BF16))
                ypre_ref[:, cols] = acc
        ypre = ypre_ref[...] + x * dx_ref[...]
        ypre_ref[...] = ypre
        z = z_ref[...]
        yg = ypre * (z * _sigmoid(z))
        _, yh = _rms_stats(yg)
        yb_ref[...] = (yh * ng_ref[...]).astype(BF16)

    row = lambda w, col: pl.BlockSpec((BLK, w), lambda c: (c, col))
    vec = lambda w: pl.BlockSpec((1, w), lambda c: (0, 0))
    return pl.pallas_call(
        body, name="ssd_fwd", grid=(nb,),
        in_specs=[row(XBC, 0), row(BLK, C_DT // BLK), row(SSD_INNER, 0), vec(BLK), vec(BLK),
                  vec(SSD_INNER), vec(SSD_INNER)],
        out_specs=[row(SSD_INNER, 0), row(SSD_INNER, 0),
                   pl.BlockSpec((None, SSD_INNER, SSD_STATE), lambda c: (c, 0, 0))],
        out_shape=[jax.ShapeDtypeStruct((lp, 2 * SSD_INNER), BF16), jax.ShapeDtypeStruct((lp, SSD_INNER), F32),
                   jax.ShapeDtypeStruct((nb, SSD_INNER, SSD_STATE), F32)],
        scratch_shapes=[pltpu.VMEM((SSD_INNER, SSD_STATE), F32)],
        compiler_params=_cparams(("arbitrary",)))(xbc, proj, proj, bias, alog, d_x, norm_g)


def _ssd_bwd_call(dycat, ypre, xbc, proj, sprev, bias, alog, d_x, norm_g):
    lp = xbc.shape[0]
    nb = lp // BLK
    gw = SSD_INNER // SSD_GROUPS
    ppg = gw // BLK

    def body(dy_ref, ypre_ref, xbc_ref, dtr_ref, z_ref, sp_ref, bias_ref, alog_ref, dxp_ref, ng_ref,
             dz_ref, dxbc_ref, ddt_ref, dng_ref, dd_ref, dal_ref, dbi_ref, ds_ref, dxd_ref):
        step = pl.program_id(0)
        c = nb - 1 - step

        @pl.when(step == 0)
        def _():
            ds_ref[...] = jnp.zeros_like(ds_ref)

        q = _ssd_common(c, dtr_ref[...], bias_ref[...], alog_ref[...])
        row, lane = q["row"], q["lane"]
        low = lane < HEAD_DIM
        rowlive = ((c * BLK + _iota((BLK, 1), 0)) >= PAD)
        x = xbc_ref[:, 0:SSD_INNER]
        xd = x * q["dt_x"]
        z = z_ref[...]
        sz = _sigmoid(z)
        silu = z * sz
        ypre = ypre_ref[...]
        dyg, dng = _rms_bwd(ypre * silu, ng_ref[...], dy_ref[...])
        _acc_rows(dng_ref, dng, step)
        dyp = dyg * silu
        dz_ref[...] = jnp.where(rowlive, dyg * ypre * (sz * (1.0 + z * (1.0 - sz))), 0.0).astype(BF16)
        _acc_rows(dd_ref, jnp.sum(dyp * x, axis=0, keepdims=True), step)
        dye = dyp * q["e_x"]
        xdf = xd * q["f_x"]
        s_prev = sp_ref[...]
        ds_old = ds_ref[...]
        qrow = jnp.zeros((BLK, BLK), F32)
        qcol_t = jnp.zeros((BLK, BLK), F32)
        red_e = []
        red_f = []
        for g in range(SSD_GROUPS):
            gs = slice(g * gw, (g + 1) * gw)
            bsl = slice(SSD_INNER + g * SSD_STATE, SSD_INNER + (g + 1) * SSD_STATE)
            csl = slice(SSD_INNER + (SSD_GROUPS + g) * SSD_STATE, SSD_INNER + (SSD_GROUPS + g + 1) * SSD_STATE)
            bg = xbc_ref[:, bsl].astype(BF16)
            cg = xbc_ref[:, csl].astype(BF16)
            sg = s_prev[gs, :].astype(BF16)
            dsg = ds_old[gs, :].astype(BF16)
            cb = _dot(cg, bg, 1, 1)
            bds = _dot(bg, dsg, 1, 1)
            y_off = _dot(cg, sg, 1, 1) * q["e_x"][:, gs]
            red_e.append(dyp[:, gs] * y_off)
            red_f.append(xd[:, gs] * bds * q["f_x"][:, gs])
            dc = _dot(dye[:, gs].astype(BF16), sg)
            db = _dot(xdf[:, gs].astype(BF16), dsg)
            ds_ref[gs, :] = ds_old[gs, :] * q["decay_col"][gs, :] + _dot(dye[:, gs].astype(BF16), cg, 0, 0)
            dcb = jnp.zeros((BLK, BLK), F32)
            for pr in range(ppg):
                cols = slice(g * gw + pr * BLK, g * gw + (pr + 1) * BLK)
                xd_p = xd[:, cols].astype(BF16)
                dy_p = dyp[:, cols]
                acc = q["f_x"][:, cols] * bds[:, pr * BLK:(pr + 1) * BLK]
                for hh in range(2):
                    h = (g * gw + pr * BLK) // HEAD_DIM + hh
                    ld = _ssd_ldec(q, h)
                    m = cb * ld
                    dym = (jnp.where(low, dy_p, 0.0) if hh == 0 else jnp.where(low, 0.0, dy_p)).astype(BF16)
                    dm = jnp.where(row >= lane, _dot(dym, xd_p, 1, 1), 0.0)
                    acc = acc + _dot(m.astype(BF16), dym, 0, 0)
                    dcb = dcb + dm * ld
                    qq = dm * m
                    qrow = qrow + jnp.where(lane == h, jnp.sum(qq, axis=1, keepdims=True), 0.0)
                    qcol_t = qcol_t + jnp.where(row == h, jnp.sum(qq, axis=0, keepdims=True), 0.0)
                dxd_ref[:, cols] = acc
            dcbb = dcb.astype(BF16)
            dxbc_ref[:, bsl] = jnp.where(rowlive, db + _dot(dcbb, cg, 0, 0), 0.0)
            dxbc_ref[:, csl] = jnp.where(rowlive, dc + _dot(dcbb, bg), 0.0)
        dxd = dxd_ref[...]
        dxbc_ref[:, 0:SSD_INNER] = jnp.where(rowlive, dxd * q["dt_x"] + dyp * dxp_ref[...], 0.0)
        xp_t = q["xp_t"]
        fw = _dot_sel_r(jnp.concatenate(red_f, axis=1), xp_t)
        dcs = qrow - qcol_t.T + _dot_sel_r(jnp.concatenate(red_e, axis=1), xp_t) - fw
        end_f = jnp.sum(fw, axis=0, keepdims=True)
        sds = jnp.sum(ds_old * s_prev, axis=1, keepdims=True)
        per_head = _dot_sel_l(q["xp"], jnp.broadcast_to(sds, (SSD_INNER, BLK)))
        end_e = per_head.T[0:1, :] * jnp.exp(q["cs"][BLK - 1:BLK, :])
        dcs = dcs + jnp.where(row == BLK - 1, end_f + end_e, 0.0)
        t_up = (lane >= row).astype(BF16)
        da = _dot_sel_l(t_up, dcs)
        ddt = da * q["a_neg"] + _dot_sel_r(dxd * x, xp_t)
        _acc_rows(dal_ref, jnp.sum(da * q["dt"] * q["a_neg"], axis=0, keepdims=True), step)
        ddtr = jnp.where(q["live"], ddt * _sigmoid(q["pre"]), 0.0)
        ddt_ref[...] = ddtr.astype(BF16)
        _acc_rows(dbi_ref, jnp.sum(ddtr, axis=0, keepdims=True), step)

    row_s = lambda w, col: pl.BlockSpec((BLK, w), lambda s: (nb - 1 - s, col))
    vec = lambda w: pl.BlockSpec((1, w), lambda s: (0, 0))
    return pl.pallas_call(
        body, name="ssd_bwd", grid=(nb,),
        in_specs=[row_s(SSD_INNER, 0), row_s(SSD_INNER, 0), row_s(XBC, 0), row_s(BLK, C_DT // BLK),
                  row_s(SSD_INNER, 0), pl.BlockSpec((None, SSD_INNER, SSD_STATE), lambda s: (nb - 1 - s, 0, 0)),
                  vec(BLK), vec(BLK), vec(SSD_INNER), vec(SSD_INNER)],
        out_specs=[row_s(SSD_INNER, 0), row_s(XBC, 0), row_s(BLK, 0),
                   vec(SSD_INNER), vec(SSD_INNER), vec(BLK), vec(BLK)],
        out_shape=[jax.ShapeDtypeStruct((lp, SSD_INNER), BF16), jax.ShapeDtypeStruct((lp, XBC), F32),
                   jax.ShapeDtypeStruct((lp, BLK), BF16),
                   jax.ShapeDtypeStruct((1, SSD_INNER), F32), jax.ShapeDtypeStruct((1, SSD_INNER), F32),
                   jax.ShapeDtypeStruct((1, BLK), F32), jax.ShapeDtypeStruct((1, BLK), F32)],
        scratch_shapes=[pltpu.VMEM((SSD_INNER, SSD_STATE), F32), pltpu.VMEM((BLK, SSD_INNER), F32)],
        compiler_params=_cparams(("arbitrary",)))(dycat, ypre, xbc, proj, proj, sprev, bias, alog, d_x, norm_g)


def _pad_rows8(w):
    return jnp.pad(w, ((0, 8 - w.shape[0]), (0, 0)))


def _pad_lanes(v, n=BLK):
    return jnp.pad(v, ((0, 0), (0, n - v.shape[1])))


def _local_step(x, target, wt, late_shards, late_weights, w_in_shards):
    seq = x.shape[0]
    lp = seq + BLK
    tm = _pick(lp, [1408, 768, 384, 128])
    tkr = _pick(lp, [1408, 384, 128])
    rbc = _pick(lp, [384, 128])
    h0 = jnp.concatenate([jnp.zeros((PAD, D_MODEL), F32), wt["meta"], x], axis=0)
    bias = _pad_lanes(wt["ssd_dt_bias"])
    alog = _pad_lanes(wt["ssd_a_log"])
    d_x = jnp.repeat(wt["ssd_d"], HEAD_DIM, axis=1)
    cw8 = _pad_rows8(wt["ssd_conv_w"])
    fw8 = _pad_rows8(wt["ffn_conv_w"])
    fcw = D_FF // 2

    xn1 = _rms_fwd_call(h0, wt["mix_pre_g"], "norm1")
    proj, late_a = _mm(xn1, wt["w_in"], tm=tm, tn=1152, tk=D_MODEL, ride=_gather_ride(late_shards[0:1]),
                       name="mm_proj")
    proj = proj[0]
    conv_pre, xbc = _conv_fwd_call(proj, C_XBC, XBC, 512, cw8, wt["ssd_conv_b"], 4, rb=rbc, name="ssd_conv_fwd")
    y_ssd, ypre, sprev = _ssd_fwd_call(xbc, proj, bias, alog, d_x, wt["ssd_norm_g"])
    o, tl, late_b = _sb_fwd_call(proj, _gather_ride(late_shards[1:3]))
    ycat = _rms_fwd_call(o, wt["sb_norm_g"], "sb_norm", beside=y_ssd)
    w_out, w_up, w_down = late_weights([late_a[0], late_b[0], late_b[1]])
    mix = _mm(ycat, w_out, tm=tm, tn=1024, tk=2048, name="mm_mix")[0]
    h1, xn2 = _mid_fwd_call(h0, mix, wt["mix_post_g"], wt["ffn_pre_g"])
    gu = _mm(xn2, w_up, tm=tm, tn=1408, tk=D_MODEL, name="mm_up")[0]
    gpre, act = _conv_fwd_call(gu, 0, D_FF, fcw, fw8, wt["ffn_conv_b"], 3, gate_src=gu, gate_col0=D_FF,
                               rb=rbc, name="ffn_conv_fwd")
    f = _mm(act, w_down, tm=tm, tn=1024, tk=1408, name="mm_down")[0]
    loss_row, df, dh2, dg_ffn_post = _final_call(h1, f, wt["ffn_post_g"], target)

    dact = _mm(df, w_down, tb=True, tm=tm, tn=1408, tk=D_MODEL, name="mm_dact")[0]
    dw_down, dw_down_b = _mm(act, df, ta=True, tm=1408, tn=1024, tk=tkr, extra_bf16=True, name="mm_dw_down")
    by_chip = lambda g: g.reshape(N_CHIPS, -1, D_MODEL)
    dgate, dup, dfcw, dfcb = _conv_bwd_call(gu, 0, D_FF, fcw, fw8, 3, gpre, dact, gate_src=gu, gate_col0=D_FF,
                                            rb=rbc, name="ffn_conv_bwd")
    dgu = jnp.concatenate([dgate, dup], axis=1)
    dxn2, land_down = _mm(dgu, w_up, tb=True, tm=tm, tn=1024, tk=1408, ride=_scatter_ride(by_chip(dw_down_b)),
                          name="mm_dxn2")
    dw_up, dw_up_b = _mm(xn2, dgu, ta=True, tm=1024, tn=1408, tk=tkr, nsplit=N_CHIPS, extra_bf16=True,
                         name="mm_dw_up")
    dh1, dmix, dg_ffn_pre, dg_mix_post = _mid_bwd_call(dh2, h1, dxn2[0], mix, wt["ffn_pre_g"], wt["mix_post_g"])
    dycat = _mm(dmix, w_out, tb=True, tm=tm, tn=1024, tk=D_MODEL, name="mm_dycat")[0]
    dw_out, dw_out_b = _mm(ycat, dmix, ta=True, tm=1024, tn=1024, tk=tkr, extra_bf16=True, name="mm_dw_out")
    do, dg_sb = _norm_bwd_call(o, wt["sb_norm_g"], dycat, 1, "sb_norm_bwd")
    dq, dk, dv, lands = _sb_bwd_call(proj, tl, do, _join_rides(_scatter_ride(dw_up_b),
                                                                  _scatter_ride(by_chip(dw_out_b))))
    dz, dxbc_act, ddt, dg_ssd, dd_x, dalog, dbias = _ssd_bwd_call(
        dycat, ypre, xbc, proj, sprev, bias, alog, d_x, wt["ssd_norm_g"])
    dxbc, dcw, dcb = _conv_bwd_call(proj, C_XBC, XBC, 512, cw8, 4, conv_pre, dxbc_act, rb=rbc,
                                    name="ssd_conv_bwd")
    dproj = jnp.concatenate([dz, dxbc, ddt, dq, dk, dv], axis=1)
    dw_in, dw_in_b = w_in_shards(_mm(xn1, dproj, ta=True, tm=1024, tn=1152, tk=tkr, name="mm_dw_in")[0])
    part = dw_in_b.shape[1] // 4
    dxn1, land_in = _mm(dproj, wt["w_in"], tb=True, tm=tm, tn=1024, tk=1152, ride=_scatter_ride(dw_in_b, 0, part),
                        name="mm_dxn1")
    dh0, dg_pre, land_in = _norm_bwd_call(h0, wt["mix_pre_g"], dxn1[0], 0, "norm1_bwd", res=dh1,
                                          ride=_scatter_ride(dw_in_b, part, part), land=land_in[0])

    small = {
        "meta_tokens": dh0[PAD:BLK], "mix_pre_g": dg_pre, "ssd_conv_w": dcw[:4], "ssd_conv_b": dcb,
        "ssd_dt_bias": dbias[:, :N_HEADS], "ssd_a_log": dalog[:, :N_HEADS],
        "ssd_d": jnp.sum(dd_x.reshape(N_HEADS, HEAD_DIM), axis=1)[None],
        "ssd_norm_g": dg_ssd, "sb_norm_g": dg_sb, "mix_post_g": dg_mix_post, "ffn_pre_g": dg_ffn_pre,
        "ffn_conv_w": dfcw[:3], "ffn_conv_b": dfcb, "ffn_post_g": dg_ffn_post,
    }
    pending = {"w_in": (dw_in, land_in), "w_out": (by_chip(dw_out), lands[1]), "w_up": (dw_up, lands[0]),
               "w_down": (by_chip(dw_down), land_down[0])}
    return loss_row, dh0[BLK:], small, pending


def _adamw_call(w, g, m, v, name):
    rows, cols = w.shape
    tr = 256 if rows % 256 == 0 else (352 if rows % 352 == 0 else rows)
    c1 = 1.0 - ADAM_B1 ** ADAM_STEP
    c2 = 1.0 - ADAM_B2 ** ADAM_STEP

    def body(w_ref, g_ref, m_ref, v_ref, d_ref, mo_ref, vo_ref):
        gv = g_ref[...]
        m2 = ADAM_B1 * m_ref[...] + (1.0 - ADAM_B1) * gv
        v2 = ADAM_B2 * v_ref[...] + (1.0 - ADAM_B2) * (gv * gv)
        d_ref[...] = -ADAM_LR * ((m2 / c1) / (jnp.sqrt(v2 / c2) + ADAM_EPS) + ADAM_WD * w_ref[...])
        mo_ref[...] = m2
        vo_ref[...] = v2

    spec = pl.BlockSpec((tr, cols), lambda i: (i, 0))
    return pl.pallas_call(
        body, name=name, grid=(rows // tr,), in_specs=[spec] * 4, out_specs=[spec] * 3,
        out_shape=[jax.ShapeDtypeStruct((rows, cols), F32)] * 3,
        compiler_params=_cparams(("parallel",)))(w, g, m, v)


ANY = pl.BlockSpec(memory_space=pl.ANY)


def _place():
    x, y, c = lax.axis_index("x"), lax.axis_index("y"), lax.axis_index("c")
    chips = [(1 - x, y), (x, 1 - y), (1 - x, 1 - y)]
    return x, y, c, chips


def _half(c, h):
    return pl.ds(pl.multiple_of(c * h, 8), h)


def _allgather_call(shards):
    n = len(shards)

    def body(*refs):
        ins, outs = refs[:n], refs[n:2 * n]
        send_i, recv_i, send_d, recv_d = refs[2 * n:]
        x, y, c, chips = _place()
        me = 2 * x + y
        sends = []
        for a in range(n):
            h = shards[a].shape[0] // 2
            for j, chip in enumerate(chips):
                cp = pltpu.make_async_remote_copy(
                    src_ref=ins[a].at[_half(c, h)], dst_ref=outs[a].at[me, _half(c, h)],
                    send_sem=send_i.at[3 * a + j], recv_sem=recv_i.at[3 * a + j],
                    device_id=(*chip, c), device_id_type=MESH)
                cp.start()
                sends.append(cp)
        for a in range(n):
            h = shards[a].shape[0] // 2
            for j, chip in enumerate(chips):
                src = 2 * chip[0] + chip[1]
                landed = outs[a].at[src, _half(c, h)]
                pltpu.make_async_remote_copy(
                    src_ref=landed, dst_ref=landed, send_sem=send_i.at[3 * a + j], recv_sem=recv_i.at[3 * a + j],
                    device_id=(*chip, c), device_id_type=MESH).wait_recv()
                cp = pltpu.make_async_remote_copy(
                    src_ref=landed, dst_ref=landed, send_sem=send_d.at[3 * a + j], recv_sem=recv_d.at[3 * a + j],
                    device_id=(x, y, 1 - c), device_id_type=MESH)
                cp.start()
                sends.append(cp)
        for a in range(n):
            h = shards[a].shape[0] // 2
            for j, chip in enumerate(chips):
                src = 2 * chip[0] + chip[1]
                other = outs[a].at[src, _half(1 - c, h)]
                pltpu.make_async_remote_copy(
                    src_ref=other, dst_ref=other, send_sem=send_d.at[3 * a + j], recv_sem=recv_d.at[3 * a + j],
                    device_id=(x, y, 1 - c), device_id_type=MESH).wait_recv()
        for cp in sends:
            cp.wait_send()

    return pl.pallas_call(
        body, name="allgather_weights", in_specs=[ANY] * n, out_specs=[ANY] * n,
        out_shape=[jax.ShapeDtypeStruct((N_CHIPS,) + s.shape, s.dtype) for s in shards],
        scratch_shapes=[pltpu.SemaphoreType.DMA((3 * n,))] * 4,
    )(*shards)


def _ride_scratch(ride):
    return [pltpu.SemaphoreType.DMA((ride[3],)), pltpu.SemaphoreType.DMA((ride[3],))]


def _ride_run(ride, src_refs, land_refs, send, recv, first, last):
    plan = ride[2]

    @pl.when(first)
    def _():
        for k, (src, dst, _, dev) in enumerate(plan(src_refs, land_refs)):
            pltpu.make_async_remote_copy(src_ref=src, dst_ref=dst, send_sem=send.at[k], recv_sem=recv.at[k],
                                         device_id=dev, device_id_type=MESH).start()

    @pl.when(last)
    def _():
        for k, (src, _, land, dev) in enumerate(plan(src_refs, land_refs)):
            cp = pltpu.make_async_remote_copy(src_ref=src, dst_ref=land, send_sem=send.at[k], recv_sem=recv.at[k],
                                              device_id=dev, device_id_type=MESH)
            cp.wait_send()
            cp.wait_recv()


def _join_rides(r1, r2):
    n1, l1 = len(r1[0]), len(r1[1])

    def plan(srcs, lands):
        return r1[2](srcs[:n1], lands[:l1]) + r2[2](srcs[n1:], lands[l1:])

    return (r1[0] + r2[0], r1[1] + r2[1], plan, r1[3] + r2[3])


def _gather_ride(shards):
    return (list(shards), [jax.ShapeDtypeStruct((N_CHIPS,) + s.shape, s.dtype) for s in shards],
            _gather_plan(len(shards)), 3 * len(shards))


def _scatter_ride(g_b, r0=0, nr=None):
    h = g_b.shape[1] // 2
    return ([g_b], [jax.ShapeDtypeStruct((8, h, g_b.shape[2]), BF16)], _scatter_plan(h, r0, h if nr is None else nr), 7)


def _gather_plan(n):
    def plan(srcs, lands):
        x, y, c, chips = _place()
        me = 2 * x + y
        return [(srcs[a], lands[a].at[me], lands[a].at[2 * chip[0] + chip[1]], (*chip, c))
                for a in range(n) for chip in chips]
    return plan


def _scatter_plan(h, r0, nr):
    def plan(srcs, lands):
        x, y, c, _ = _place()
        me = 4 * x + 2 * y + c
        out = []
        for p in range(1, 8):
            px = 1 - x if p & 4 else x
            py = 1 - y if p & 2 else y
            pc = 1 - c if p & 1 else c
            rows = pl.ds(pl.multiple_of(pc * h + r0, 8), nr)
            out.append((srcs[0].at[2 * px + py, rows], lands[0].at[me, pl.ds(r0, nr)],
                        lands[0].at[4 * px + 2 * py + pc, pl.ds(r0, nr)], (px, py, pc)))
        return out
    return plan


def _grad_sum_call(own, land, place, name):
    _, h, cols = land.shape
    th = _pick(h, [256, 176, 8])
    nt = h // th

    def body(p_ref, own_ref, *refs):
        acc = own_ref[...]
        for r in refs[:7]:
            acc = acc + r[...].astype(F32)
        refs[7][...] = acc

    def peer(k):
        return pl.BlockSpec((None, th, cols), lambda i, p_ref: (p_ref[2 + k], i, 0))

    return pl.pallas_call(
        body, name=name,
        grid_spec=pltpu.PrefetchScalarGridSpec(
            num_scalar_prefetch=1, grid=(nt,),
            in_specs=[pl.BlockSpec((None, th, cols), lambda i, p_ref: (p_ref[1], p_ref[0] * nt + i, 0))]
            + [peer(k) for k in range(7)],
            out_specs=pl.BlockSpec((th, cols), lambda i, p_ref: (p_ref[0] * nt + i, 0))),
        out_shape=jax.ShapeDtypeStruct((2 * h, cols), F32),
        compiler_params=_cparams(("parallel",)))(place, own, *[land] * 7)


def _half_exchange_call(shards):
    n = len(shards)

    def body(*refs):
        outs = refs[n:2 * n]
        send_d, recv_d = refs[2 * n:]
        x, y, c, _ = _place()
        cps = []
        for a in range(n):
            h = shards[a].shape[0] // 2
            mine = outs[a].at[_half(c, h)]
            cp = pltpu.make_async_remote_copy(
                src_ref=mine, dst_ref=mine, send_sem=send_d.at[a], recv_sem=recv_d.at[a],
                device_id=(x, y, 1 - c), device_id_type=MESH)
            cp.start()
            cps.append(cp)
        for a, cp in enumerate(cps):
            h = shards[a].shape[0] // 2
            theirs = outs[a].at[_half(1 - c, h)]
            pltpu.make_async_remote_copy(
                src_ref=theirs, dst_ref=theirs, send_sem=send_d.at[a], recv_sem=recv_d.at[a],
                device_id=(x, y, 1 - c), device_id_type=MESH).wait_recv()
            cp.wait_send()

    return pl.pallas_call(
        body, name="grad_half_exchange", in_specs=[ANY] * n, out_specs=[ANY] * n,
        out_shape=[jax.ShapeDtypeStruct(sv.shape, F32) for sv in shards],
        input_output_aliases={a: a for a in range(n)},
        scratch_shapes=[pltpu.SemaphoreType.DMA((n,))] * 2,
    )(*shards)


def _allreduce_small_call(arrs):
    n = len(arrs)
    offs, rows = [], 0
    for a in arrs:
        offs.append(rows)
        rows += a.shape[0]
    rows = -(-rows // 8) * 8
    width = -(-max(a.shape[1] for a in arrs) // BLK) * BLK

    def body(*refs):
        ins, outs = refs[:n], refs[n:2 * n]
        gath, send_sems, recv_sems = refs[2 * n:]
        x, y, c, chips = _place()
        me, sibling = (x, y, c), (x, y, 1 - c)

        def slot(px, py, pc):
            return gath.at[4 * px + 2 * py + pc]

        def copy(k, block, to):
            return pltpu.make_async_remote_copy(
                src_ref=slot(*block), dst_ref=slot(*block),
                send_sem=send_sems.at[k], recv_sem=recv_sems.at[k], device_id=to, device_id_type=MESH)

        mine = slot(*me)
        mine[...] = jnp.zeros((rows, width), F32)
        for k in range(n):
            r, w = arrs[k].shape
            mine[offs[k]:offs[k] + r, 0:w] = ins[k][...]
        first = [copy(0, me, sibling)]
        first += [copy(1 + j, me, (*chip, c)) for j, chip in enumerate(chips)]
        for cp in first:
            cp.start()
        passed = [copy(4 + j, (*chip, c), sibling) for j, chip in enumerate(chips)]
        for j, chip in enumerate(chips):
            copy(1 + j, (*chip, c), me).wait_recv()
            passed[j].start()
        copy(0, sibling, me).wait_recv()
        for j, chip in enumerate(chips):
            copy(4 + j, (*chip, 1 - c), me).wait_recv()
        for cp in first + passed:
            cp.wait_send()
        acc = gath[0]
        for d in range(1, 8):
            acc = acc + gath[d]
        for k in range(n):
            r, w = arrs[k].shape
            outs[k][...] = acc[offs[k]:offs[k] + r, 0:w]

    vm = pl.BlockSpec(memory_space=pltpu.VMEM)
    return pl.pallas_call(
        body, name="allreduce_small", in_specs=[vm] * n, out_specs=[vm] * n,
        out_shape=[jax.ShapeDtypeStruct(a.shape, F32) for a in arrs],
        scratch_shapes=[pltpu.VMEM((8, rows, width), F32), pltpu.SemaphoreType.DMA((7,)),
                        pltpu.SemaphoreType.DMA((7,))],
        compiler_params=pltpu.CompilerParams(vmem_limit_bytes=VMEM_LIMIT),
    )(*arrs)


def _adamw_small_call(ws, gs, ms, vs):
    n = len(ws)
    c1 = 1.0 - ADAM_B1 ** ADAM_STEP
    c2 = 1.0 - ADAM_B2 ** ADAM_STEP

    def body(*refs):
        for k in range(n):
            w_ref, g_ref, m_ref, v_ref = (refs[j * n + k] for j in range(4))
            d_ref, mo_ref, vo_ref = (refs[(4 + j) * n + k] for j in range(3))
            gv = g_ref[...]
            m2 = ADAM_B1 * m_ref[...] + (1.0 - ADAM_B1) * gv
            v2 = ADAM_B2 * v_ref[...] + (1.0 - ADAM_B2) * (gv * gv)
            d_ref[...] = -ADAM_LR * ((m2 / c1) / (jnp.sqrt(v2 / c2) + ADAM_EPS) + ADAM_WD * w_ref[...])
            mo_ref[...] = m2
            vo_ref[...] = v2

    vm = pl.BlockSpec(memory_space=pltpu.VMEM)
    res = pl.pallas_call(
        body, name="adamw_small", in_specs=[vm] * (4 * n), out_specs=[vm] * (3 * n),
        out_shape=[jax.ShapeDtypeStruct(a.shape, F32) for a in ws] * 3,
        compiler_params=pltpu.CompilerParams(vmem_limit_bytes=VMEM_LIMIT),
    )(*ws, *gs, *ms, *vs)
    return res[:n], res[n:2 * n], res[2 * n:]


def _pack(arrs, min_rows=8):
    parts = []
    for a in arrs:
        flat = a.reshape(-1).astype(F32)
        parts.append(jnp.pad(flat, (0, (-flat.shape[0]) % BLK)))
    buf = jnp.concatenate(parts).reshape(-1, BLK)
    return jnp.pad(buf, ((0, (-buf.shape[0]) % min_rows), (0, 0)))


def _unpack(buf, shapes):
    out, r = [], 0
    for shp in shapes:
        n = math.prod(shp)
        nr = -(-n // BLK)
        out.append(buf[r:r + nr].reshape(-1)[:n].reshape(shp))
        r += nr
    return out


SMALL = ["meta_tokens", "mix_pre_g", "ssd_conv_w", "ssd_conv_b", "ssd_dt_bias", "ssd_a_log", "ssd_d", "ssd_norm_g",
         "sb_norm_g", "mix_post_g", "ffn_pre_g", "ffn_conv_w", "ffn_conv_b", "ffn_post_g"]
BIG = ["w_in", "w_out", "w_up", "w_down"]
WEIGHTS = ["meta_tokens", "mix_pre_g", "w_in", "ssd_conv_w", "ssd_conv_b", "ssd_dt_bias", "ssd_a_log", "ssd_d",
           "ssd_norm_g", "sb_norm_g", "w_out", "mix_post_g", "ffn_pre_g", "w_up", "ffn_conv_w", "ffn_conv_b",
           "w_down", "ffn_post_g"]
W_IN_SHARD = IN_COLS // N_CHIPS
W_IN_PAD = 1536


def kernel(x, meta_tokens, mix_pre_g, w_in, ssd_conv_w, ssd_conv_b, ssd_dt_bias, ssd_a_log, ssd_d, ssd_norm_g, sb_norm_g, w_out, mix_post_g, ffn_pre_g, w_up, ffn_conv_w, ffn_conv_b, w_down, ffn_post_g, loss_target, m_meta_tokens, m_mix_pre_g, m_w_in, m_ssd_conv_w, m_ssd_conv_b, m_ssd_dt_bias, m_ssd_a_log, m_ssd_d, m_ssd_norm_g, m_sb_norm_g, m_w_out, m_mix_post_g, m_ffn_pre_g, m_w_up, m_ffn_conv_w, m_ffn_conv_b, m_w_down, m_ffn_post_g, v_meta_tokens, v_mix_pre_g, v_w_in, v_ssd_conv_w, v_ssd_conv_b, v_ssd_dt_bias, v_ssd_a_log, v_ssd_d, v_ssd_norm_g, v_sb_norm_g, v_w_out, v_mix_post_g, v_ffn_pre_g, v_w_up, v_ffn_conv_w, v_ffn_conv_b, v_w_down, v_ffn_post_g):
    w = dict(meta_tokens=meta_tokens, mix_pre_g=mix_pre_g, w_in=w_in, ssd_conv_w=ssd_conv_w, ssd_conv_b=ssd_conv_b, ssd_dt_bias=ssd_dt_bias, ssd_a_log=ssd_a_log, ssd_d=ssd_d, ssd_norm_g=ssd_norm_g, sb_norm_g=sb_norm_g, w_out=w_out, mix_post_g=mix_post_g, ffn_pre_g=ffn_pre_g, w_up=w_up, ffn_conv_w=ffn_conv_w, ffn_conv_b=ffn_conv_b, w_down=w_down, ffn_post_g=ffn_post_g)
    m = dict(meta_tokens=m_meta_tokens, mix_pre_g=m_mix_pre_g, w_in=m_w_in, ssd_conv_w=m_ssd_conv_w, ssd_conv_b=m_ssd_conv_b, ssd_dt_bias=m_ssd_dt_bias, ssd_a_log=m_ssd_a_log, ssd_d=m_ssd_d, ssd_norm_g=m_ssd_norm_g, sb_norm_g=m_sb_norm_g, w_out=m_w_out, mix_post_g=m_mix_post_g, ffn_pre_g=m_ffn_pre_g, w_up=m_w_up, ffn_conv_w=m_ffn_conv_w, ffn_conv_b=m_ffn_conv_b, w_down=m_w_down, ffn_post_g=m_ffn_post_g)
    v = dict(meta_tokens=v_meta_tokens, mix_pre_g=v_mix_pre_g, w_in=v_w_in, ssd_conv_w=v_ssd_conv_w, ssd_conv_b=v_ssd_conv_b, ssd_dt_bias=v_ssd_dt_bias, ssd_a_log=v_ssd_a_log, ssd_d=v_ssd_d, ssd_norm_g=v_ssd_norm_g, sb_norm_g=v_sb_norm_g, w_out=v_w_out, mix_post_g=v_mix_post_g, ffn_pre_g=v_ffn_pre_g, w_up=v_w_up, ffn_conv_w=v_ffn_conv_w, ffn_conv_b=v_ffn_conv_b, w_down=v_w_down, ffn_post_g=v_ffn_post_g)
    chip = 2 * lax.axis_index("x") + lax.axis_index("y")
    me = 2 * chip + lax.axis_index("c")
    place = jnp.stack([lax.axis_index("c"), chip] + [me ^ p for p in range(1, 8)]).astype(jnp.int32)

    shard_small = [w["meta_tokens"], w["ssd_conv_w"][0], w["ffn_conv_w"][0]]
    shards = [jnp.pad(w["w_in"][0], ((0, 0), (0, W_IN_PAD - W_IN_SHARD))).astype(BF16), _pack(shard_small, 16)]
    gathered = _allgather_call(shards)
    late_shards = [w["w_out"][0].astype(BF16), w["w_up"][0].astype(BF16), w["w_down"][0].astype(BF16)]

    def blocks(own, got):
        return [jnp.where(chip == i, own, got[i]) for i in range(N_CHIPS)]

    def late_weights(got):
        return (jnp.concatenate(blocks(late_shards[0], got[0]), axis=0),
                jnp.concatenate(blocks(late_shards[1], got[1]), axis=1),
                jnp.concatenate(blocks(late_shards[2], got[2]), axis=0))

    cut = DT_REAL_OFF + N_HEADS - W_IN_SHARD
    s_in = blocks(shards[0], gathered[0])
    w_in_c = jnp.concatenate(
        [s_in[0][:, :W_IN_SHARD], s_in[1][:, :cut], jnp.zeros((D_MODEL, BLK - N_HEADS), BF16),
         s_in[1][:, cut:W_IN_SHARD], s_in[2][:, :W_IN_SHARD], s_in[3][:, :W_IN_SHARD]], axis=1)
    parts = [_unpack(b, [s.shape for s in shard_small]) for b in blocks(shards[1], gathered[1])]
    wt = {k: w[k][0][None] if w[k].ndim == 3 else w[k] for k in
          ["mix_pre_g", "ssd_conv_b", "ssd_dt_bias", "ssd_a_log", "ssd_d", "ssd_norm_g", "sb_norm_g", "mix_post_g",
           "ffn_pre_g", "ffn_conv_b", "ffn_post_g"]}
    wt.update(
        meta=jnp.concatenate([p[0] for p in parts], axis=1),
        ssd_conv_w=jnp.concatenate([p[1] for p in parts], axis=1),
        ffn_conv_w=jnp.concatenate([p[2] for p in parts], axis=1), w_in=w_in_c)

    def w_in_shards(g):
        skip = BLK - N_HEADS
        cols = [g[:, :W_IN_SHARD],
                jnp.concatenate([g[:, W_IN_SHARD:W_IN_SHARD + cut], g[:, C_Q:2 * W_IN_SHARD + skip]], axis=1),
                g[:, 2 * W_IN_SHARD + skip:3 * W_IN_SHARD + skip], g[:, 3 * W_IN_SHARD + skip:]]
        g = jnp.stack([jnp.pad(b, ((0, 0), (0, W_IN_PAD - W_IN_SHARD))) for b in cols])
        return g, g.astype(BF16)

    loss_row, dx, small, pending = _local_step(x[0], loss_target[0], wt, late_shards, late_weights, w_in_shards)

    full = _half_exchange_call([_grad_sum_call(*pending[k], place, "grad_sum_" + k) for k in BIG])
    grads = {"w_in": full[0][:, :W_IN_SHARD], "w_out": full[1], "w_up": full[2], "w_down": full[3]}

    red_list = _allreduce_small_call([small[k] for k in SMALL] + [loss_row])
    loss = jnp.sum(red_list[-1])
    for k, g in zip(SMALL, red_list[:-1]):
        grads[k] = g
    for k in ["meta_tokens", "ssd_conv_w", "ffn_conv_w"]:
        wk = w[k].shape[-1]
        grads[k] = lax.dynamic_slice_in_dim(grads[k], chip * wk, wk, axis=1)

    delta, new_m, new_v = {}, {}, {}
    for k in BIG:
        delta[k], new_m[k], new_v[k] = _adamw_call(w[k][0], grads[k], m[k][0], v[k][0], "adamw_" + k)
    flat = lambda d: [d[k].reshape(grads[k].shape) for k in SMALL]
    res = _adamw_small_call(flat(w), [grads[k] for k in SMALL], flat(m), flat(v))
    for out, arrs in zip((delta, new_m, new_v), res):
        for k, a in zip(SMALL, arrs):
            out[k] = a

    def shaped(d, k):
        return d[k].reshape(w[k].shape)

    return (loss, dx[None], *[shaped(grads, k) for k in WEIGHTS], *[shaped(delta, k) for k in WEIGHTS],
            *[shaped(new_m, k) for k in WEIGHTS], *[shaped(new_v, k) for k in WEIGHTS])
```

```python
import functools
import math

import jax
import jax.numpy as jnp
from jax import lax
from jax.experimental import pallas as pl
from jax.experimental.pallas import tpu as pltpu

F32 = jnp.float32
BF16 = jnp.bfloat16

D_MODEL = 1024
N_META = 16
BLK = 128
PAD = BLK - N_META
HEAD_DIM = 64
N_HEADS = 16
SSD_GROUPS = 2
SSD_STATE = 128
SSD_INNER = 1024
XBC = SSD_INNER + 2 * SSD_GROUPS * SSD_STATE
D_FF = 2816
EPS = 1e-6
IN_COLS = 5648
C_Z, C_XBC, C_DT, C_Q, C_K, C_V, C_END = 0, 1024, 2560, 2688, 3712, 4736, 5760
DT_REAL_OFF = 2560
N_CHIPS = 4
ADAM_LR, ADAM_B1, ADAM_B2, ADAM_EPS, ADAM_WD, ADAM_STEP = 0.001, 0.9, 0.999, 1e-08, 0.01, 10
VMEM_LIMIT = 56 * 1024 * 1024
MESH = pl.DeviceIdType.MESH


def _cparams(sem=None, **kw):
    if sem is not None:
        kw["dimension_semantics"] = sem
    return pltpu.CompilerParams(vmem_limit_bytes=VMEM_LIMIT, **kw)


def _pick(n, cands):
    for c in cands:
        if n % c == 0:
            return c
    raise ValueError((n, cands))


def _iota(shape, dim):
    return lax.broadcasted_iota(jnp.int32, shape, dim)


def _sigmoid(x):
    return 1.0 / (1.0 + jnp.exp(-x))


def _split2(v):
    h1 = v.astype(BF16)
    return h1, (v - h1.astype(F32)).astype(BF16)


def _dot(a, b, ca=1, cb=0):
    return lax.dot_general(a, b, (((ca,), (cb,)), ((), ())), preferred_element_type=F32)


def _dot_sel_r(v, sel, cb=0):
    h1, h2 = _split2(v)
    return _dot(h1, sel, 1, cb) + _dot(h2, sel, 1, cb)


def _dot_sel_l(sel, v, ca=1):
    h1, h2 = _split2(v)
    return _dot(sel, h1, ca, 0) + _dot(sel, h2, ca, 0)


def _mm(a, b, *, ta=False, tb=False, tm, tn, tk, out_dtype=F32, nsplit=1, extra_bf16=False, ride=None, name):
    K, M = (a.shape if ta else a.shape[::-1])
    N = b.shape[0] if tb else b.shape[1]
    assert M % tm == 0 and N % tn == 0 and K % tk == 0, (name, M, N, K, tm, tn, tk)
    nm, nn, nk = M // tm, N // tn, K // tk
    assert nn % nsplit == 0
    per = nn // nsplit
    a_spec = (pl.BlockSpec((tk, tm), lambda i, j, k: (k, i)) if ta
              else pl.BlockSpec((tm, tk), lambda i, j, k: (i, k)))
    b_spec = (pl.BlockSpec((tn, tk), lambda i, j, k: (j, k)) if tb
              else pl.BlockSpec((tk, tn), lambda i, j, k: (k, j)))
    o_spec = pl.BlockSpec((None, tm, tn), lambda i, j, k: (j // per, i, j % per))
    n_out = 2 if extra_bf16 else 1
    ca, cb = (0 if ta else 1), (1 if tb else 0)
    ns, nl = (len(ride[0]), len(ride[1])) if ride else (0, 0)

    def body(a_ref, b_ref, *rest):
        outs = rest[ns:ns + n_out]
        if ride:
            step = (pl.program_id(0) * nn + pl.program_id(1)) * nk + pl.program_id(2)
            _ride_run(ride, rest[:ns], rest[ns + n_out:ns + n_out + nl], rest[-2], rest[-1],
                      step == 0, step == nm * nn * nk - 1)
        p = _dot(a_ref[...].astype(BF16), b_ref[...].astype(BF16), ca, cb)

        def emit(val):
            outs[0][...] = val.astype(out_dtype)
            if extra_bf16:
                outs[1][...] = val.astype(BF16)

        if nk == 1:
            emit(p)
        else:
            acc = rest[ns + n_out + nl]
            k = pl.program_id(2)

            @pl.when(k == 0)
            def _():
                acc[...] = p

            @pl.when(k > 0)
            def _():
                acc[...] += p

            @pl.when(k == nk - 1)
            def _():
                emit(acc[...])

    shp = (nsplit, M, N // nsplit)
    out_shape = [jax.ShapeDtypeStruct(shp, out_dtype)]
    out_specs = [o_spec]
    if extra_bf16:
        out_shape.append(jax.ShapeDtypeStruct(shp, BF16))
        out_specs.append(o_spec)
    scratch = [pltpu.VMEM((tm, tn), F32)] if nk > 1 else []
    if ride:
        res = pl.pallas_call(
            body, name=name, grid=(nm, nn, nk), in_specs=[a_spec, b_spec] + [ANY] * ns,
            out_specs=out_specs + [ANY] * nl, out_shape=out_shape + list(ride[1]),
            scratch_shapes=scratch + _ride_scratch(ride),
            compiler_params=_cparams(("arbitrary", "arbitrary", "arbitrary")),
        )(a, b, *ride[0])
        return (res[:n_out] if extra_bf16 else res[0]), list(res[n_out:])
    res = pl.pallas_call(
        body, name=name, grid=(nm, nn, nk), in_specs=[a_spec, b_spec], out_specs=out_specs,
        out_shape=out_shape, scratch_shapes=scratch,
        compiler_params=_cparams(("parallel", "parallel", "arbitrary")),
    )(a, b)
    return res if extra_bf16 else res[0]


def _rms_stats(x):
    r = lax.rsqrt(jnp.mean(x * x, axis=-1, keepdims=True) + EPS)
    return r, x * r


def _rms_bwd(x, g, dy):
    r, xh = _rms_stats(x)
    dxh = dy * g
    dx = r * (dxh - xh * jnp.mean(dxh * xh, axis=-1, keepdims=True))
    return dx, jnp.sum(dy * xh, axis=0, keepdims=True)


def _row_spec(tr, w, col=0):
    return pl.BlockSpec((tr, w), lambda i: (i, col))


def _vec_spec(w):
    return pl.BlockSpec((1, w), lambda i: (0, 0))


def _acc_rows(ref, val, i):
    @pl.when(i == 0)
    def _():
        ref[...] = val

    @pl.when(i > 0)
    def _():
        ref[...] += val


def _rms_fwd_call(x, g, name, beside=None):
    lp, w = x.shape
    tr = _pick(lp, [384, 128])

    def body(x_ref, g_ref, *rest):
        _, xh = _rms_stats(x_ref[...])
        rest[-1][...] = (xh * g_ref[...]).astype(BF16)

    if beside is None:
        return pl.pallas_call(
            body, name=name, grid=(lp // tr,), in_specs=[_row_spec(tr, w), _vec_spec(w)],
            out_specs=_row_spec(tr, w), out_shape=jax.ShapeDtypeStruct((lp, w), BF16),
            compiler_params=_cparams(("parallel",)))(x, g)
    return pl.pallas_call(
        body, name=name, grid=(lp // tr,), in_specs=[_row_spec(tr, w), _vec_spec(w), ANY],
        out_specs=_row_spec(tr, w, 1), out_shape=jax.ShapeDtypeStruct((lp, 2 * w), BF16),
        input_output_aliases={2: 0}, compiler_params=_cparams(("parallel",)))(x, g, beside)


def _mid_fwd_call(h0, mix, g_post, g_pre2):
    lp, w = h0.shape
    tr = _pick(lp, [384, 128])

    def body(h0_ref, mix_ref, gp_ref, g2_ref, h1_ref, xn_ref):
        _, mh = _rms_stats(mix_ref[...])
        h1 = h0_ref[...] + mh * gp_ref[...]
        h1_ref[...] = h1
        _, hh = _rms_stats(h1)
        xn_ref[...] = (hh * g2_ref[...]).astype(BF16)

    return pl.pallas_call(
        body, name="mid_fwd", grid=(lp // tr,),
        in_specs=[_row_spec(tr, w), _row_spec(tr, w), _vec_spec(w), _vec_spec(w)],
        out_specs=[_row_spec(tr, w), _row_spec(tr, w)],
        out_shape=[jax.ShapeDtypeStruct((lp, w), F32), jax.ShapeDtypeStruct((lp, w), BF16)],
        compiler_params=_cparams(("parallel",)))(h0, mix, g_post, g_pre2)


def _final_call(h1, f, g_post, target):
    lp, w = h1.shape
    tr = BLK
    nb = lp // tr

    def body(h1_ref, f_ref, g_ref, t_ref, loss_ref, df_ref, dh_ref, dg_ref):
        i = pl.program_id(0)
        fv = f_ref[...]
        g = g_ref[...]
        _, fh = _rms_stats(fv)
        h2 = h1_ref[...] + fh * g
        diff = jnp.where(i > 0, h2 - t_ref[...], 0.0)
        part = 0.5 * jnp.sum(diff * diff, axis=0, keepdims=True) * (1.0 / w)
        _acc_rows(loss_ref, part, i)
        dh = diff * (1.0 / w)
        dh_ref[...] = dh
        df, dg = _rms_bwd(fv, g, dh)
        df_ref[...] = df.astype(BF16)
        _acc_rows(dg_ref, dg, i)

    t_spec = pl.BlockSpec((tr, w), lambda i: (jnp.maximum(i - 1, 0), 0))
    return pl.pallas_call(
        body, name="final_fwd_bwd", grid=(nb,),
        in_specs=[_row_spec(tr, w), _row_spec(tr, w), _vec_spec(w), t_spec],
        out_specs=[_vec_spec(w), _row_spec(tr, w), _row_spec(tr, w), _vec_spec(w)],
        out_shape=[jax.ShapeDtypeStruct((1, w), F32), jax.ShapeDtypeStruct((lp, w), BF16),
                   jax.ShapeDtypeStruct((lp, w), F32), jax.ShapeDtypeStruct((1, w), F32)],
        compiler_params=_cparams(("arbitrary",)))(h1, f, g_post, target)


def _mid_bwd_call(dh2, h1, dxn2, mix, g_pre2, g_post):
    lp, w = h1.shape
    tr = _pick(lp, [384, 128])

    def body(dh2_ref, h1_ref, dxn_ref, mix_ref, g2_ref, gp_ref, dh1_ref, dmix_ref, dg2_ref, dgp_ref):
        i = pl.program_id(0)
        live = (i * tr + _iota((tr, 1), 0)) >= PAD
        dx, dg2 = _rms_bwd(h1_ref[...], g2_ref[...], dxn_ref[...])
        dh1 = jnp.where(live, dh2_ref[...] + dx, 0.0)
        dh1_ref[...] = dh1
        dmix, dgp = _rms_bwd(mix_ref[...], gp_ref[...], dh1)
        dmix_ref[...] = jnp.where(live, dmix, 0.0).astype(BF16)
        _acc_rows(dg2_ref, dg2, i)
        _acc_rows(dgp_ref, dgp, i)

    rs = _row_spec(tr, w)
    return pl.pallas_call(
        body, name="mid_bwd", grid=(lp // tr,),
        in_specs=[rs, rs, rs, rs, _vec_spec(w), _vec_spec(w)],
        out_specs=[rs, rs, _vec_spec(w), _vec_spec(w)],
        out_shape=[jax.ShapeDtypeStruct((lp, w), F32), jax.ShapeDtypeStruct((lp, w), BF16),
                   jax.ShapeDtypeStruct((1, w), F32), jax.ShapeDtypeStruct((1, w), F32)],
        compiler_params=_cparams(("arbitrary",)))(dh2, h1, dxn2, mix, g_pre2, g_post)


def _norm_bwd_call(x, g, dy_arr, dy_col, name, res=None, ride=None, land=None):
    lp, w = x.shape
    tr = _pick(lp, [384, 128])
    nsteps = lp // tr
    has_res = res is not None
    ns = len(ride[0]) if ride else 0
    first_out = (1 if has_res else 0) + (ns + 1 if ride else 0)

    def body(x_ref, g_ref, dy_ref, *rest):
        i = pl.program_id(0)
        if ride:
            srcs = rest[first_out - ns - 1:first_out - 1]
            _ride_run(ride, srcs, [rest[first_out + 2]], rest[-2], rest[-1], i == 0, i == nsteps - 1)
        live = (i * tr + _iota((tr, 1), 0)) >= PAD
        dx, dg = _rms_bwd(x_ref[...], g_ref[...], dy_ref[...])
        if has_res:
            dx = dx + rest[0][...]
        out_ref, dg_ref = rest[first_out], rest[first_out + 1]
        out_ref[...] = jnp.where(live, dx, 0.0)
        _acc_rows(dg_ref, dg, i)

    rs = _row_spec(tr, w)
    ins = [rs, _vec_spec(w), _row_spec(tr, w, dy_col)] + ([rs] if has_res else [])
    args = [x, g, dy_arr] + ([res] if has_res else [])
    outs = [rs, _vec_spec(w)]
    out_shape = [jax.ShapeDtypeStruct((lp, w), F32), jax.ShapeDtypeStruct((1, w), F32)]
    if not ride:
        return pl.pallas_call(
            body, name=name, grid=(nsteps,), in_specs=ins, out_specs=outs, out_shape=out_shape,
            compiler_params=_cparams(("arbitrary",)))(*args)
    return pl.pallas_call(
        body, name=name, grid=(nsteps,), in_specs=ins + [ANY] * (ns + 1), out_specs=outs + [ANY],
        out_shape=out_shape + [jax.ShapeDtypeStruct(land.shape, land.dtype)],
        input_output_aliases={len(args) + ns: 2}, scratch_shapes=_ride_scratch(ride),
        compiler_params=_cparams(("arbitrary",)))(*args, *ride[0], land)


def _shift_down(cur, prev_tail, s, rows):
    if s == 0:
        return cur
    prev = jnp.tile(prev_tail, (cur.shape[0] // 8, 1))
    return jnp.where(rows >= s, pltpu.roll(cur, s, 0), pltpu.roll(prev, s, 0))


def _shift_up(cur, next_head, s, rows):
    if s == 0:
        return cur
    n = cur.shape[0]
    nxt = jnp.tile(next_head, (n // 8, 1))
    return jnp.where(rows < n - s, pltpu.roll(cur, n - s, 0), pltpu.roll(nxt, n - s, 0))


def _gelu_tanh(x):
    c = math.sqrt(2.0 / math.pi)
    t = jnp.tanh(c * (x + 0.044715 * x * x * x))
    return 0.5 * x * (1.0 + t), t


def _conv_fwd_call(src, col0, width, cw, w8, b, taps, *, gate_src=None, gate_col0=0, rb=BLK, name):
    lp = src.shape[0]
    nb, nc = lp // rb, width // cw
    cb0 = col0 // cw
    ffn = gate_src is not None

    def body(x_ref, w_ref, b_ref, *rest):
        if ffn:
            u_ref, y_ref, a_ref, tail = rest
        else:
            y_ref, a_ref, tail = rest
        i = pl.program_id(1)

        @pl.when(i == 0)
        def _():
            tail[...] = jnp.zeros_like(tail)

        cur = x_ref[...]
        rows = _iota((rb, cw), 0)
        y = b_ref[...] + w_ref[taps - 1:taps, :] * cur
        pt = tail[...]
        for s in range(1, taps):
            y = y + w_ref[taps - 1 - s:taps - s, :] * _shift_down(cur, pt, s, rows)
        tail[...] = cur[rb - 8:, :]
        y_ref[...] = y
        if ffn:
            ge, _ = _gelu_tanh(y)
            a_ref[...] = (ge * u_ref[...]).astype(BF16)
        else:
            live = (i * rb + rows) >= PAD
            a_ref[...] = jnp.where(live, y * _sigmoid(y), 0.0)

    blk = lambda c0: pl.BlockSpec((rb, cw), lambda j, i: (i, c0 + j))
    ins = [blk(cb0), pl.BlockSpec((8, cw), lambda j, i: (0, j)), pl.BlockSpec((1, cw), lambda j, i: (0, j))]
    args = [src, w8, b]
    if ffn:
        ins.append(blk(gate_col0 // cw))
        args.append(gate_src)
    return pl.pallas_call(
        body, name=name, grid=(nc, nb), in_specs=ins, out_specs=[blk(0), blk(0)],
        out_shape=[jax.ShapeDtypeStruct((lp, width), F32),
                   jax.ShapeDtypeStruct((lp, width), BF16 if ffn else F32)],
        scratch_shapes=[pltpu.VMEM((8, cw), F32)],
        compiler_params=_cparams(("parallel", "arbitrary")))(*args)


def _conv_bwd_call(src, col0, width, cw, w8, taps, ypre, dact, *, gate_src=None, gate_col0=0, rb=BLK, name):
    lp = src.shape[0]
    nb, nc = lp // rb, width // cw
    cb0 = col0 // cw
    ffn = gate_src is not None

    def body(x_ref, w_ref, y_ref, d_ref, *rest):
        if ffn:
            u_ref, dx_ref, du_ref, dw_ref, db_ref, head = rest
        else:
            dx_ref, dw_ref, db_ref, head = rest
        step = pl.program_id(1)
        i = nb - 1 - step

        @pl.when(step == 0)
        def _():
            head[...] = jnp.zeros_like(head)

        rows = _iota((rb, cw), 0)
        live = (i * rb + rows) >= PAD
        y = y_ref[...]
        d = d_ref[...]
        if ffn:
            ge, t = _gelu_tanh(y)
            c = math.sqrt(2.0 / math.pi)
            dge = 0.5 * (1.0 + t) + 0.5 * y * (1.0 - t * t) * c * (1.0 + 3.0 * 0.044715 * y * y)
            u = u_ref[...]
            du_ref[...] = jnp.where(live, d * ge, 0.0).astype(BF16)
            dy = jnp.where(live, d * u * dge, 0.0)
        else:
            sg = _sigmoid(y)
            dy = jnp.where(live, d * sg * (1.0 + y * (1.0 - sg)), 0.0)
        x = x_ref[...]
        nh = head[...]
        dx = jnp.zeros_like(dy)
        dws = []
        for s in range(taps):
            sh = _shift_up(dy, nh, s, rows)
            dx = dx + w_ref[taps - 1 - s:taps - s, :] * sh
            dws.append(jnp.sum(x * sh, axis=0, keepdims=True))
        head[...] = dy[:8, :]
        dx_ref[...] = jnp.where(live, dx, 0.0).astype(BF16)
        dw = jnp.concatenate([dws[taps - 1 - k] for k in range(taps)]
                             + [jnp.zeros((8 - taps, cw), F32)], axis=0)
        _acc_rows(dw_ref, dw, step)
        _acc_rows(db_ref, jnp.sum(dy, axis=0, keepdims=True), step)

    blk = lambda c0: pl.BlockSpec((rb, cw), lambda j, s: (nb - 1 - s, c0 + j))
    ins = [blk(cb0), pl.BlockSpec((8, cw), lambda j, s: (0, j)), blk(0), blk(0)]
    args = [src, w8, ypre, dact]
    outs = [blk(0)]
    oshape = [jax.ShapeDtypeStruct((lp, width), BF16)]
    if ffn:
        ins.append(blk(gate_col0 // cw))
        args.append(gate_src)
        outs.append(blk(0))
        oshape.append(jax.ShapeDtypeStruct((lp, width), BF16))
    outs += [pl.BlockSpec((8, cw), lambda j, s: (0, j)), pl.BlockSpec((1, cw), lambda j, s: (0, j))]
    oshape += [jax.ShapeDtypeStruct((8, width), F32), jax.ShapeDtypeStruct((1, width), F32)]
    return pl.pallas_call(
        body, name=name, grid=(nc, nb), in_specs=ins, out_specs=outs, out_shape=oshape,
        scratch_shapes=[pltpu.VMEM((8, cw), F32)],
        compiler_params=_cparams(("parallel", "arbitrary")))(*args)


SB_FIRST = 3
SB_GROUP = 4
SB_DEAD = -110.0


def _sb_scores(qm_h, kb):
    z = _dot(qm_h, kb, 1, 1)
    sp = jnp.maximum(z, 0.0) + jnp.log(1.0 + jnp.exp(-jnp.abs(z)))
    return z - sp, -sp


def _dot_tri1(v, tri2):
    r = _dot(v.astype(BF16), tri2[:BLK])
    return r[:, :BLK], r[:, BLK:]


def _tri2(cond):
    t = jnp.concatenate([cond.astype(BF16), jnp.ones((BLK, BLK), BF16)], axis=1)
    return jnp.concatenate([t, t], axis=0)


def _dot_tri(v, tri2):
    hi = v.astype(BF16)
    lo = (v - hi.astype(F32)).astype(BF16)
    r = _dot(jnp.concatenate([hi, lo], axis=1), tri2)
    return r[:, :BLK], r[:, BLK:]


def _sb_fwd_call(proj, ride):
    lp = proj.shape[0]
    nb = lp // BLK
    scale = 1.0 / math.sqrt(HEAD_DIM)

    ns, nl = len(ride[0]), len(ride[1])
    npair = N_HEADS // 2

    def body(q_ref, k_ref, v_ref, *rest):
        o_ref, tl_ref = rest[ns], rest[ns + 1]
        i = pl.program_id(1)
        step = pl.program_id(0) * nb + i
        _ride_run(ride, rest[:ns], rest[ns + 2:ns + 2 + nl], rest[-2], rest[-1], step == 0, step == npair * nb - 1)
        lane = _iota((2 * BLK, BLK), 1)
        row = _iota((2 * BLK, BLK), 0)
        first = row < BLK
        qrow = row & (BLK - 1)
        q = q_ref[...] * scale
        q2 = jnp.concatenate([q, q], axis=0)
        qm = jnp.where(first == (lane < HEAD_DIM), q2, 0.0).astype(BF16)
        tri = _tri2(_iota((BLK, BLK), 0) > _iota((BLK, BLK), 1))

        def chunk(off, nsub, last_valid, carry):
            width = nsub * BLK
            sls = [slice(b * BLK, (b + 1) * BLK) for b in range(nsub)]
            kb = k_ref[pl.ds(off, width), :].astype(BF16)
            vb = v_ref[pl.ds(off, width), :].astype(BF16)
            lb, lk = _sb_scores(qm, kb)
            lks = [lk[:, sl] for sl in sls]
            first_valid = (off + lane) >= PAD
            lks[0] = jnp.where(first_valid, lks[0], 0.0)
            if last_valid is not None:
                lks[-1] = jnp.where(last_valid, lks[-1], 0.0)
            afters = [_dot_tri(lks[b], tri) for b in range(nsub)]
            run, acc = carry
            ws = [None] * nsub
            for b in reversed(range(nsub)):
                wb = jnp.exp(lb[:, sls[b]] + afters[b][0] + run)
                if b == 0:
                    wb = jnp.where(first_valid, wb, 0.0)
                if last_valid is not None and b == nsub - 1:
                    wb = jnp.where(last_valid, wb, 0.0)
                ws[b] = wb.astype(BF16)
                run = run + afters[b][1]
            w = ws[0] if nsub == 1 else jnp.concatenate(ws, axis=1)
            return run, acc + _dot(w, vb)

        before = jnp.minimum(i, SB_FIRST - 1)
        first_off = pl.multiple_of((i - before) * BLK, BLK)
        diag = lane < qrow
        zero = jnp.zeros((2 * BLK, BLK), F32)
        carry = lax.switch(before, [functools.partial(chunk, first_off, n, diag) for n in range(1, SB_FIRST + 1)],
                           (zero, zero))

        def walk(n):
            def body(state):
                off = pl.multiple_of((state[0] - (n - 1)) * BLK, BLK)
                return (state[0] - n, *chunk(off, n, None, state[1:]))

            def cond(state):
                return jnp.logical_and(state[0] >= n - 1, jnp.max(state[1]) > SB_DEAD)

            return cond, body

        state = lax.while_loop(*walk(SB_GROUP), (i - SB_FIRST, *carry))
        pos, run, acc = lax.while_loop(*walk(1), state)
        low = lane[:BLK] < HEAD_DIM
        o_ref[...] = jnp.where(low, acc[:BLK], acc[BLK:])
        tl = jnp.where(low, run[:BLK], run[BLK:])
        tl_ref[...] = jnp.where(lane[:BLK] == 1, jnp.maximum(pos + 1, 0).astype(F32), tl)

    qc, kc, vc = C_Q // BLK, C_K // BLK, C_V // BLK
    blk = pl.BlockSpec((BLK, BLK), lambda p, i: (i, p))
    res = pl.pallas_call(
        body, name="sb_fwd", grid=(npair, nb),
        in_specs=[pl.BlockSpec((BLK, BLK), lambda p, i: (i, qc + p)),
                  pl.BlockSpec((lp, BLK), lambda p, i: (0, kc + p)),
                  pl.BlockSpec((lp, BLK), lambda p, i: (0, vc + p))] + [ANY] * ns,
        out_specs=[blk, blk] + [ANY] * nl,
        out_shape=[jax.ShapeDtypeStruct((lp, N_HEADS * HEAD_DIM), F32)] * 2 + list(ride[1]),
        scratch_shapes=_ride_scratch(ride),
        compiler_params=_cparams(("arbitrary", "arbitrary")))(proj, proj, proj, *ride[0])
    return res[0], res[1], list(res[2:])


def _sb_bwd_call(proj, tl, do, ride):
    lp = proj.shape[0]
    nb = lp // BLK
    scale = 1.0 / math.sqrt(HEAD_DIM)

    ns, nl = len(ride[0]), len(ride[1])
    npair = N_HEADS // 2

    def body(q_ref, k_ref, v_ref, tl_ref, do_ref, *rest):
        dq_ref, dk_ref, dv_ref = rest[ns:ns + 3]
        dk_acc, dv_acc = rest[ns + 3 + nl:ns + 5 + nl]
        i = pl.program_id(1)
        step = pl.program_id(0) * nb + i
        _ride_run(ride, rest[:ns], rest[ns + 3:ns + 3 + nl], rest[-2], rest[-1], step == 0, step == npair * nb - 1)

        @pl.when(i == 0)
        def _():
            dk_acc[...] = jnp.zeros_like(dk_acc)
            dv_acc[...] = jnp.zeros_like(dv_acc)

        lane = _iota((2 * BLK, BLK), 1)
        row = _iota((2 * BLK, BLK), 0)
        qrow = row & (BLK - 1)
        mine = (row < BLK) == (lane < HEAD_DIM)
        q = q_ref[...] * scale
        dov = do_ref[...]
        qm = jnp.where(mine, jnp.concatenate([q, q], axis=0), 0.0).astype(BF16)
        dom = jnp.where(mine, jnp.concatenate([dov, dov], axis=0), 0.0).astype(BF16)
        tlv = tl_ref[...]
        tot = jnp.concatenate([jnp.broadcast_to(tlv[:, 0:1], (BLK, BLK)),
                               jnp.broadcast_to(tlv[:, HEAD_DIM:HEAD_DIM + 1], (BLK, BLK))], axis=0)
        r1, l1 = _iota((BLK, BLK), 0), _iota((BLK, BLK), 1)
        tri_in = _tri2(r1 <= l1)
        tri_ex = _tri2(r1 < l1)

        def chunk(off, nsub, last_valid, carry):
            width = nsub * BLK
            sls = [slice(b * BLK, (b + 1) * BLK) for b in range(nsub)]
            cat = lambda parts: parts[0] if nsub == 1 else jnp.concatenate(parts, axis=1)
            mask_last = lambda b: last_valid is not None and b == nsub - 1
            kb = k_ref[pl.ds(off, width), :].astype(BF16)
            vb = v_ref[pl.ds(off, width), :].astype(BF16)
            lb, lk = _sb_scores(qm, kb)
            dw = _dot(dom, vb, 1, 1)
            lks = [lk[:, sl] for sl in sls]
            first_valid = (off + lane) >= PAD
            lks[0] = jnp.where(first_valid, lks[0], 0.0)
            if last_valid is not None:
                lks[-1] = jnp.where(last_valid, lks[-1], 0.0)
            pins = [_dot_tri(lks[b], tri_in) for b in range(nsub)]
            run, gsum, dq = carry
            ws, gs = [], []
            for b in range(nsub):
                wb = jnp.exp(lb[:, sls[b]] + (tot - run - pins[b][0]))
                if b == 0:
                    wb = jnp.where(first_valid, wb, 0.0)
                if mask_last(b):
                    wb = jnp.where(last_valid, wb, 0.0)
                ws.append(wb.astype(BF16))
                gs.append(wb * dw[:, sls[b]])
                run = run + pins[b][1]
            gexs = [_dot_tri1(gs[b], tri_ex) for b in range(nsub)]
            beta = jnp.exp(lb)
            parts = []
            for b in range(nsub):
                bt = beta[:, sls[b]]
                dzb = gs[b] * (1.0 - bt) - (gsum + gexs[b][0]) * bt
                if b == 0:
                    dzb = jnp.where(first_valid, dzb, 0.0)
                if mask_last(b):
                    dzb = jnp.where(last_valid, dzb, 0.0)
                parts.append(dzb.astype(BF16))
                gsum = gsum + gexs[b][1]
            dz, w = cat(parts), cat(ws)
            dk_acc[pl.ds(off, width), :] += _dot(dz, qm, 0, 0)
            dv_acc[pl.ds(off, width), :] += _dot(w, dom, 0, 0)
            return run, gsum, dq + _dot(dz, kb)

        diag = lane < qrow
        zero = jnp.zeros((2 * BLK, BLK), F32)
        top = i - SB_FIRST

        def walk(n):
            def body(state):
                off = pl.multiple_of(state[0] * BLK, BLK)
                return (state[0] + n, *chunk(off, n, None, state[1:]))

            return (lambda state: state[0] + (n - 1) <= top), body

        state = (jnp.max(tlv[:, 1:2]).astype(jnp.int32), zero, zero, zero)
        state = lax.while_loop(*walk(SB_GROUP), state)
        carry = lax.while_loop(*walk(1), state)[1:]
        before = jnp.minimum(i, SB_FIRST - 1)
        first_off = pl.multiple_of((i - before) * BLK, BLK)
        dq = lax.switch(before, [functools.partial(chunk, first_off, n, diag) for n in range(1, SB_FIRST + 1)],
                        carry)[2]
        dq_ref[...] = (jnp.where(lane[:BLK] < HEAD_DIM, dq[:BLK], dq[BLK:]) * scale).astype(BF16)

        @pl.when(i == nb - 1)
        def _():
            dk_ref[...] = dk_acc[...].astype(BF16)
            dv_ref[...] = dv_acc[...].astype(BF16)

    qc, kc, vc = C_Q // BLK, C_K // BLK, C_V // BLK
    blk = pl.BlockSpec((BLK, BLK), lambda p, i: (i, p))
    full = pl.BlockSpec((lp, BLK), lambda p, i: (0, p))
    w = N_HEADS * HEAD_DIM
    res = pl.pallas_call(
        body, name="sb_bwd", grid=(npair, nb),
        in_specs=[pl.BlockSpec((BLK, BLK), lambda p, i: (i, qc + p)),
                  pl.BlockSpec((lp, BLK), lambda p, i: (0, kc + p)),
                  pl.BlockSpec((lp, BLK), lambda p, i: (0, vc + p)),
                  blk, blk] + [ANY] * ns,
        out_specs=[blk, full, full] + [ANY] * nl,
        out_shape=[jax.ShapeDtypeStruct((lp, w), BF16)] * 3 + list(ride[1]),
        scratch_shapes=[pltpu.VMEM((lp, BLK), F32), pltpu.VMEM((lp, BLK), F32)] + _ride_scratch(ride),
        compiler_params=_cparams(("arbitrary", "arbitrary")))(proj, proj, proj, tl, do, *ride[0])
    return res[0], res[1], res[2], list(res[3:])


def _log1p(e):
    u = 1.0 + e
    return jnp.where(u == 1.0, e, jnp.log(u) * e / jnp.where(u == 1.0, 1.0, u - 1.0))


def _ssd_common(c, dtr, bias, alog):
    row = _iota((BLK, BLK), 0)
    lane = _iota((BLK, BLK), 1)
    live = ((c * BLK + row) >= PAD) & (lane < N_HEADS)
    pre = dtr + bias
    dt = jnp.where(live, jnp.maximum(pre, 0.0) + _log1p(jnp.exp(-jnp.abs(pre))), 0.0)
    a_neg = -jnp.exp(alog)
    a = dt * a_neg
    t_in = (lane <= row).astype(BF16)
    cs = _dot_sel_l(t_in, a)
    cs_t = cs.T
    cs_end = cs[BLK - 1:BLK, :]
    e = jnp.exp(cs)
    f = jnp.exp(cs_end - cs)
    xp = ((_iota((BLK, SSD_INNER), 1) // HEAD_DIM) == _iota((BLK, SSD_INNER), 0)).astype(BF16)
    xp_t = ((_iota((SSD_INNER, BLK), 0) // HEAD_DIM) == _iota((SSD_INNER, BLK), 1)).astype(BF16)
    decay_col = _dot_sel_l(xp_t, jnp.exp(cs_t))[:, BLK - 1:BLK]
    return dict(live=live, pre=pre, dt=dt, a_neg=a_neg, cs=cs, cs_t=cs_t, e=e, f=f, xp=xp, xp_t=xp_t,
                decay_col=decay_col, row=row, lane=lane,
                dt_x=_dot_sel_r(dt, xp), e_x=_dot_sel_r(e, xp), f_x=_dot_sel_r(f, xp))


def _ssd_ldec(q, h):
    diff = q["cs"][:, h:h + 1] - q["cs_t"][h:h + 1, :]
    causal = q["row"] >= q["lane"]
    return jnp.where(causal, jnp.exp(jnp.where(causal, diff, 0.0)), 0.0)


def _ssd_fwd_call(xbc, proj, bias, alog, d_x, norm_g):
    lp = xbc.shape[0]
    nb = lp // BLK
    gw = SSD_INNER // SSD_GROUPS
    ppg = gw // BLK

    def body(xbc_ref, dtr_ref, z_ref, bias_ref, alog_ref, dx_ref, ng_ref, yb_ref, ypre_ref, sprev_ref, s_ref):
        c = pl.program_id(0)

        @pl.when(c == 0)
        def _():
            s_ref[...] = jnp.zeros_like(s_ref)

        q = _ssd_common(c, dtr_ref[...], bias_ref[...], alog_ref[...])
        x = xbc_ref[:, 0:SSD_INNER]
        xd = x * q["dt_x"]
        low = q["lane"] < HEAD_DIM
        s_old = s_ref[...]
        sprev_ref[...] = s_old
        xdf = (xd * q["f_x"]).astype(BF16)
        for g in range(SSD_GROUPS):
            bg = xbc_ref[:, SSD_INNER + g * SSD_STATE:SSD_INNER + (g + 1) * SSD_STATE].astype(BF16)
            cg = xbc_ref[:, SSD_INNER + (SSD_GROUPS + g) * SSD_STATE:
                         SSD_INNER + (SSD_GROUPS + g + 1) * SSD_STATE].astype(BF16)
            cb = _dot(cg, bg, 1, 1)
            gs = slice(g * gw, (g + 1) * gw)
            y_off = _dot(cg, s_old[gs, :].astype(BF16), 1, 1) * q["e_x"][:, gs]
            s_ref[gs, :] = s_old[gs, :] * q["decay_col"][gs, :] + _dot(xdf[:, gs], bg, 0, 0)
            for pr in range(ppg):
                cols = slice(g * gw + pr * BLK, g * gw + (pr + 1) * BLK)
                xd_p = xd[:, cols]
                acc = y_off[:, pr * BLK:(pr + 1) * BLK]
                for hh in range(2):
                    h = (g * gw + pr * BLK) // HEAD_DIM + hh
                    m = (cb * _ssd_ldec(q, h)).astype(BF16)
                    xm = jnp.where(low, xd_p, 0.0) if hh == 0 else jnp.where(low, 0.0, xd_p)
                    acc = acc + _dot(m, xm.astype(BF16))
                ypre_ref[:, cols] = acc
        ypre = ypre_ref[...] + x * dx_ref[...]
        ypre_ref[...] = ypre
        z = z_ref[...]
        yg = ypre * (z * _sigmoid(z))
        _, yh = _rms_stats(yg)
        yb_ref[...] = (yh * ng_ref[...]).astype(BF16)

    row = lambda w, col: pl.BlockSpec((BLK, w), lambda c: (c, col))
    vec = lambda w: pl.BlockSpec((1, w), lambda c: (0, 0))
    return pl.pallas_call(
        body, name="ssd_fwd", grid=(nb,),
        in_specs=[row(XBC, 0), row(BLK, C_DT // BLK), row(SSD_INNER, 0), vec(BLK), vec(BLK),
                  vec(SSD_INNER), vec(SSD_INNER)],
        out_specs=[row(SSD_INNER, 0), row(SSD_INNER, 0),
                   pl.BlockSpec((None, SSD_INNER, SSD_STATE), lambda c: (c, 0, 0))],
        out_shape=[jax.ShapeDtypeStruct((lp, 2 * SSD_INNER), BF16), jax.ShapeDtypeStruct((lp, SSD_INNER), F32),
                   jax.ShapeDtypeStruct((nb, SSD_INNER, SSD_STATE), F32)],
        scratch_shapes=[pltpu.VMEM((SSD_INNER, SSD_STATE), F32)],
        compiler_params=_cparams(("arbitrary",)))(xbc, proj, proj, bias, alog, d_x, norm_g)


def _ssd_bwd_call(dycat, ypre, xbc, proj, sprev, bias, alog, d_x, norm_g):
    lp = xbc.shape[0]
    nb = lp // BLK
    gw = SSD_INNER // SSD_GROUPS
    ppg = gw // BLK

    def body(dy_ref, ypre_ref, xbc_ref, dtr_ref, z_ref, sp_ref, bias_ref, alog_ref, dxp_ref, ng_ref,
             dz_ref, dxbc_ref, ddt_ref, dng_ref, dd_ref, dal_ref, dbi_ref, ds_ref, dxd_ref):
        step = pl.program_id(0)
        c = nb - 1 - step

        @pl.when(step == 0)
        def _():
            ds_ref[...] = jnp.zeros_like(ds_ref)

        q = _ssd_common(c, dtr_ref[...], bias_ref[...], alog_ref[...])
        row, lane = q["row"], q["lane"]
        low = lane < HEAD_DIM
        rowlive = ((c * BLK + _iota((BLK, 1), 0)) >= PAD)
        x = xbc_ref[:, 0:SSD_INNER]
        xd = x * q["dt_x"]
        z = z_ref[...]
        sz = _sigmoid(z)
        silu = z * sz
        ypre = ypre_ref[...]
        dyg, dng = _rms_bwd(ypre * silu, ng_ref[...], dy_ref[...])
        _acc_rows(dng_ref, dng, step)
        dyp = dyg * silu
        dz_ref[...] = jnp.where(rowlive, dyg * ypre * (sz * (1.0 + z * (1.0 - sz))), 0.0).astype(BF16)
        _acc_rows(dd_ref, jnp.sum(dyp * x, axis=0, keepdims=True), step)
        dye = dyp * q["e_x"]
        xdf = xd * q["f_x"]
        s_prev = sp_ref[...]
        ds_old = ds_ref[...]
        qrow = jnp.zeros((BLK, BLK), F32)
        qcol_t = jnp.zeros((BLK, BLK), F32)
        red_e = []
        red_f = []
        for g in range(SSD_GROUPS):
            gs = slice(g * gw, (g + 1) * gw)
            bsl = slice(SSD_INNER + g * SSD_STATE, SSD_INNER + (g + 1) * SSD_STATE)
            csl = slice(SSD_INNER + (SSD_GROUPS + g) * SSD_STATE, SSD_INNER + (SSD_GROUPS + g + 1) * SSD_STATE)
            bg = xbc_ref[:, bsl].astype(BF16)
            cg = xbc_ref[:, csl].astype(BF16)
            sg = s_prev[gs, :].astype(BF16)
            dsg = ds_old[gs, :].astype(BF16)
            cb = _dot(cg, bg, 1, 1)
            bds = _dot(bg, dsg, 1, 1)
            y_off = _dot(cg, sg, 1, 1) * q["e_x"][:, gs]
            red_e.append(dyp[:, gs] * y_off)
            red_f.append(xd[:, gs] * bds * q["f_x"][:, gs])
            dc = _dot(dye[:, gs].astype(BF16), sg)
            db = _dot(xdf[:, gs].astype(BF16), dsg)
            ds_ref[gs, :] = ds_old[gs, :] * q["decay_col"][gs, :] + _dot(dye[:, gs].astype(BF16), cg, 0, 0)
            dcb = jnp.zeros((BLK, BLK), F32)
            for pr in range(ppg):
                cols = slice(g * gw + pr * BLK, g * gw + (pr + 1) * BLK)
                xd_p = xd[:, cols].astype(BF16)
                dy_p = dyp[:, cols]
                acc = q["f_x"][:, cols] * bds[:, pr * BLK:(pr + 1) * BLK]
                for hh in range(2):
                    h = (g * gw + pr * BLK) // HEAD_DIM + hh
                    ld = _ssd_ldec(q, h)
                    m = cb * ld
                    dym = (jnp.where(low, dy_p, 0.0) if hh == 0 else jnp.where(low, 0.0, dy_p)).astype(BF16)
                    dm = jnp.where(row >= lane, _dot(dym, xd_p, 1, 1), 0.0)
                    acc = acc + _dot(m.astype(BF16), dym, 0, 0)
                    dcb = dcb + dm * ld
                    qq = dm * m
                    qrow = qrow + jnp.where(lane == h, jnp.sum(qq, axis=1, keepdims=True), 0.0)
                    qcol_t = qcol_t + jnp.where(row == h, jnp.sum(qq, axis=0, keepdims=True), 0.0)
                dxd_ref[:, cols] = acc
            dcbb = dcb.astype(BF16)
            dxbc_ref[:, bsl] = jnp.where(rowlive, db + _dot(dcbb, cg, 0, 0), 0.0)
            dxbc_ref[:, csl] = jnp.where(rowlive, dc + _dot(dcbb, bg), 0.0)
        dxd = dxd_ref[...]
        dxbc_ref[:, 0:SSD_INNER] = jnp.where(rowlive, dxd * q["dt_x"] + dyp * dxp_ref[...], 0.0)
        xp_t = q["xp_t"]
        fw = _dot_sel_r(jnp.concatenate(red_f, axis=1), xp_t)
        dcs = qrow - qcol_t.T + _dot_sel_r(jnp.concatenate(red_e, axis=1), xp_t) - fw
        end_f = jnp.sum(fw, axis=0, keepdims=True)
        sds = jnp.sum(ds_old * s_prev, axis=1, keepdims=True)
        per_head = _dot_sel_l(q["xp"], jnp.broadcast_to(sds, (SSD_INNER, BLK)))
        end_e = per_head.T[0:1, :] * jnp.exp(q["cs"][BLK - 1:BLK, :])
        dcs = dcs + jnp.where(row == BLK - 1, end_f + end_e, 0.0)
        t_up = (lane >= row).astype(BF16)
        da = _dot_sel_l(t_up, dcs)
        ddt = da * q["a_neg"] + _dot_sel_r(dxd * x, xp_t)
        _acc_rows(dal_ref, jnp.sum(da * q["dt"] * q["a_neg"], axis=0, keepdims=True), step)
        ddtr = jnp.where(q["live"], ddt * _sigmoid(q["pre"]), 0.0)
        ddt_ref[...] = ddtr.astype(BF16)
        _acc_rows(dbi_ref, jnp.sum(ddtr, axis=0, keepdims=True), step)

    row_s = lambda w, col: pl.BlockSpec((BLK, w), lambda s: (nb - 1 - s, col))
    vec = lambda w: pl.BlockSpec((1, w), lambda s: (0, 0))
    return pl.pallas_call(
        body, name="ssd_bwd", grid=(nb,),
        in_specs=[row_s(SSD_INNER, 0), row_s(SSD_INNER, 0), row_s(XBC, 0), row_s(BLK, C_DT // BLK),
                  row_s(SSD_INNER, 0), pl.BlockSpec((None, SSD_INNER, SSD_STATE), lambda s: (nb - 1 - s, 0, 0)),
                  vec(BLK), vec(BLK), vec(SSD_INNER), vec(SSD_INNER)],
        out_specs=[row_s(SSD_INNER, 0), row_s(XBC, 0), row_s(BLK, 0),
                   vec(SSD_INNER), vec(SSD_INNER), vec(BLK), vec(BLK)],
        out_shape=[jax.ShapeDtypeStruct((lp, SSD_INNER), BF16), jax.ShapeDtypeStruct((lp, XBC), F32),
                   jax.ShapeDtypeStruct((lp, BLK), BF16),
                   jax.ShapeDtypeStruct((1, SSD_INNER), F32), jax.ShapeDtypeStruct((1, SSD_INNER), F32),
                   jax.ShapeDtypeStruct((1, BLK), F32), jax.ShapeDtypeStruct((1, BLK), F32)],
        scratch_shapes=[pltpu.VMEM((SSD_INNER, SSD_STATE), F32), pltpu.VMEM((BLK, SSD_INNER), F32)],
        compiler_params=_cparams(("arbitrary",)))(dycat, ypre, xbc, proj, proj, sprev, bias, alog, d_x, norm_g)


def _pad_rows8(w):
    return jnp.pad(w, ((0, 8 - w.shape[0]), (0, 0)))


def _pad_lanes(v, n=BLK):
    return jnp.pad(v, ((0, 0), (0, n - v.shape[1])))


def _local_step(x, target, wt, late_shards, late_weights, w_in_shards):
    seq = x.shape[0]
    lp = seq + BLK
    tm = _pick(lp, [1408, 768, 384, 128])
    tkr = _pick(lp, [1408, 384, 128])
    rbc = _pick(lp, [384, 128])
    h0 = jnp.concatenate([jnp.zeros((PAD, D_MODEL), F32), wt["meta"], x], axis=0)
    bias = _pad_lanes(wt["ssd_dt_bias"])
    alog = _pad_lanes(wt["ssd_a_log"])
    d_x = jnp.repeat(wt["ssd_d"], HEAD_DIM, axis=1)
    cw8 = _pad_rows8(wt["ssd_conv_w"])
    fw8 = _pad_rows8(wt["ffn_conv_w"])
    fcw = D_FF // 2

    xn1 = _rms_fwd_call(h0, wt["mix_pre_g"], "norm1")
    proj, late_a = _mm(xn1, wt["w_in"], tm=tm, tn=1152, tk=D_MODEL, ride=_gather_ride(late_shards[0:1]),
                       name="mm_proj")
    proj = proj[0]
    conv_pre, xbc = _conv_fwd_call(proj, C_XBC, XBC, 512, cw8, wt["ssd_conv_b"], 4, rb=rbc, name="ssd_conv_fwd")
    y_ssd, ypre, sprev = _ssd_fwd_call(xbc, proj, bias, alog, d_x, wt["ssd_norm_g"])
    o, tl, late_b = _sb_fwd_call(proj, _gather_ride(late_shards[1:3]))
    ycat = _rms_fwd_call(o, wt["sb_norm_g"], "sb_norm", beside=y_ssd)
    w_out, w_up, w_down = late_weights([late_a[0], late_b[0], late_b[1]])
    mix = _mm(ycat, w_out, tm=tm, tn=1024, tk=2048, name="mm_mix")[0]
    h1, xn2 = _mid_fwd_call(h0, mix, wt["mix_post_g"], wt["ffn_pre_g"])
    gu = _mm(xn2, w_up, tm=tm, tn=1408, tk=D_MODEL, name="mm_up")[0]
    gpre, act = _conv_fwd_call(gu, 0, D_FF, fcw, fw8, wt["ffn_conv_b"], 3, gate_src=gu, gate_col0=D_FF,
                               rb=rbc, name="ffn_conv_fwd")
    f = _mm(act, w_down, tm=tm, tn=1024, tk=1408, name="mm_down")[0]
    loss_row, df, dh2, dg_ffn_post = _final_call(h1, f, wt["ffn_post_g"], target)

    dact = _mm(df, w_down, tb=True, tm=tm, tn=1408, tk=D_MODEL, name="mm_dact")[0]
    dw_down, dw_down_b = _mm(act, df, ta=True, tm=1408, tn=1024, tk=tkr, extra_bf16=True, name="mm_dw_down")
    by_chip = lambda g: g.reshape(N_CHIPS, -1, D_MODEL)
    dgate, dup, dfcw, dfcb = _conv_bwd_call(gu, 0, D_FF, fcw, fw8, 3, gpre, dact, gate_src=gu, gate_col0=D_FF,
                                            rb=rbc, name="ffn_conv_bwd")
    dgu = jnp.concatenate([dgate, dup], axis=1)
    dxn2, land_down = _mm(dgu, w_up, tb=True, tm=tm, tn=1024, tk=1408, ride=_scatter_ride(by_chip(dw_down_b)),
                          name="mm_dxn2")
    dw_up, dw_up_b = _mm(xn2, dgu, ta=True, tm=1024, tn=1408, tk=tkr, nsplit=N_CHIPS, extra_bf16=True,
                         name="mm_dw_up")
    dh1, dmix, dg_ffn_pre, dg_mix_post = _mid_bwd_call(dh2, h1, dxn2[0], mix, wt["ffn_pre_g"], wt["mix_post_g"])
    dycat = _mm(dmix, w_out, tb=True, tm=tm, tn=1024, tk=D_MODEL, name="mm_dycat")[0]
    dw_out, dw_out_b = _mm(ycat, dmix, ta=True, tm=1024, tn=1024, tk=tkr, extra_bf16=True, name="mm_dw_out")
    do, dg_sb = _norm_bwd_call(o, wt["sb_norm_g"], dycat, 1, "sb_norm_bwd")
    dq, dk, dv, lands = _sb_bwd_call(proj, tl, do, _join_rides(_scatter_ride(dw_up_b),
                                                                  _scatter_ride(by_chip(dw_out_b))))
    dz, dxbc_act, ddt, dg_ssd, dd_x, dalog, dbias = _ssd_bwd_call(
        dycat, ypre, xbc, proj, sprev, bias, alog, d_x, wt["ssd_norm_g"])
    dxbc, dcw, dcb = _conv_bwd_call(proj, C_XBC, XBC, 512, cw8, 4, conv_pre, dxbc_act, rb=rbc,
                                    name="ssd_conv_bwd")
    dproj = jnp.concatenate([dz, dxbc, ddt, dq, dk, dv], axis=1)
    dw_in, dw_in_b = w_in_shards(_mm(xn1, dproj, ta=True, tm=1024, tn=1152, tk=tkr, name="mm_dw_in")[0])
    half = dw_in_b.shape[1] // 2
    part = half * 5 // 8
    dxn1, land_in = _mm(dproj, wt["w_in"], tb=True, tm=tm, tn=1024, tk=1152, ride=_scatter_ride(dw_in_b, 0, part),
                        name="mm_dxn1")
    dh0, dg_pre, land_in = _norm_bwd_call(h0, wt["mix_pre_g"], dxn1[0], 0, "norm1_bwd", res=dh1,
                                          ride=_scatter_ride(dw_in_b, part, half - part), land=land_in[0])

    small = {
        "meta_tokens": dh0[PAD:BLK], "mix_pre_g": dg_pre, "ssd_conv_w": dcw[:4], "ssd_conv_b": dcb,
        "ssd_dt_bias": dbias[:, :N_HEADS], "ssd_a_log": dalog[:, :N_HEADS],
        "ssd_d": jnp.sum(dd_x.reshape(N_HEADS, HEAD_DIM), axis=1)[None],
        "ssd_norm_g": dg_ssd, "sb_norm_g": dg_sb, "mix_post_g": dg_mix_post, "ffn_pre_g": dg_ffn_pre,
        "ffn_conv_w": dfcw[:3], "ffn_conv_b": dfcb, "ffn_post_g": dg_ffn_post,
    }
    pending = {"w_in": (dw_in, land_in), "w_out": (by_chip(dw_out), lands[1]), "w_up": (dw_up, lands[0]),
               "w_down": (by_chip(dw_down), land_down[0])}
    return loss_row, dh0[BLK:], small, pending


def _adamw_call(w, g, m, v, name):
    rows, cols = w.shape
    tr = 256 if rows % 256 == 0 else (352 if rows % 352 == 0 else rows)
    c1 = 1.0 - ADAM_B1 ** ADAM_STEP
    c2 = 1.0 - ADAM_B2 ** ADAM_STEP

    def body(w_ref, g_ref, m_ref, v_ref, d_ref, mo_ref, vo_ref):
        gv = g_ref[...]
        m2 = ADAM_B1 * m_ref[...] + (1.0 - ADAM_B1) * gv
        v2 = ADAM_B2 * v_ref[...] + (1.0 - ADAM_B2) * (gv * gv)
        d_ref[...] = -ADAM_LR * ((m2 / c1) / (jnp.sqrt(v2 / c2) + ADAM_EPS) + ADAM_WD * w_ref[...])
        mo_ref[...] = m2
        vo_ref[...] = v2

    spec = pl.BlockSpec((tr, cols), lambda i: (i, 0))
    return pl.pallas_call(
        body, name=name, grid=(rows // tr,), in_specs=[spec] * 4, out_specs=[spec] * 3,
        out_shape=[jax.ShapeDtypeStruct((rows, cols), F32)] * 3,
        compiler_params=_cparams(("parallel",)))(w, g, m, v)


ANY = pl.BlockSpec(memory_space=pl.ANY)


def _place():
    x, y, c = lax.axis_index("x"), lax.axis_index("y"), lax.axis_index("c")
    chips = [(1 - x, y), (x, 1 - y), (1 - x, 1 - y)]
    return x, y, c, chips


def _half(c, h):
    return pl.ds(pl.multiple_of(c * h, 8), h)


def _allgather_call(shards):
    n = len(shards)

    def body(*refs):
        ins, outs = refs[:n], refs[n:2 * n]
        send_i, recv_i, send_d, recv_d = refs[2 * n:]
        x, y, c, chips = _place()
        me = 2 * x + y
        sends = []
        for a in range(n):
            h = shards[a].shape[0] // 2
            for j, chip in enumerate(chips):
                cp = pltpu.make_async_remote_copy(
                    src_ref=ins[a].at[_half(c, h)], dst_ref=outs[a].at[me, _half(c, h)],
                    send_sem=send_i.at[3 * a + j], recv_sem=recv_i.at[3 * a + j],
                    device_id=(*chip, c), device_id_type=MESH)
                cp.start()
                sends.append(cp)
        for a in range(n):
            h = shards[a].shape[0] // 2
            for j, chip in enumerate(chips):
                src = 2 * chip[0] + chip[1]
                landed = outs[a].at[src, _half(c, h)]
                pltpu.make_async_remote_copy(
                    src_ref=landed, dst_ref=landed, send_sem=send_i.at[3 * a + j], recv_sem=recv_i.at[3 * a + j],
                    device_id=(*chip, c), device_id_type=MESH).wait_recv()
                cp = pltpu.make_async_remote_copy(
                    src_ref=landed, dst_ref=landed, send_sem=send_d.at[3 * a + j], recv_sem=recv_d.at[3 * a + j],
                    device_id=(x, y, 1 - c), device_id_type=MESH)
                cp.start()
                sends.append(cp)
        for a in range(n):
            h = shards[a].shape[0] // 2
            for j, chip in enumerate(chips):
                src = 2 * chip[0] + chip[1]
                other = outs[a].at[src, _half(1 - c, h)]
                pltpu.make_async_remote_copy(
                    src_ref=other, dst_ref=other, send_sem=send_d.at[3 * a + j], recv_sem=recv_d.at[3 * a + j],
                    device_id=(x, y, 1 - c), device_id_type=MESH).wait_recv()
        for cp in sends:
            cp.wait_send()

    return pl.pallas_call(
        body, name="allgather_weights", in_specs=[ANY] * n, out_specs=[ANY] * n,
        out_shape=[jax.ShapeDtypeStruct((N_CHIPS,) + s.shape, s.dtype) for s in shards],
        scratch_shapes=[pltpu.SemaphoreType.DMA((3 * n,))] * 4,
    )(*shards)


def _ride_scratch(ride):
    return [pltpu.SemaphoreType.DMA((ride[3],)), pltpu.SemaphoreType.DMA((ride[3],))]


def _ride_run(ride, src_refs, land_refs, send, recv, first, last):
    plan = ride[2]

    @pl.when(first)
    def _():
        for k, (src, dst, _, dev) in enumerate(plan(src_refs, land_refs)):
            pltpu.make_async_remote_copy(src_ref=src, dst_ref=dst, send_sem=send.at[k], recv_sem=recv.at[k],
                                         device_id=dev, device_id_type=MESH).start()

    @pl.when(last)
    def _():
        for k, (src, _, land, dev) in enumerate(plan(src_refs, land_refs)):
            cp = pltpu.make_async_remote_copy(src_ref=src, dst_ref=land, send_sem=send.at[k], recv_sem=recv.at[k],
                                              device_id=dev, device_id_type=MESH)
            cp.wait_send()
            cp.wait_recv()


def _join_rides(r1, r2):
    n1, l1 = len(r1[0]), len(r1[1])

    def plan(srcs, lands):
        return r1[2](srcs[:n1], lands[:l1]) + r2[2](srcs[n1:], lands[l1:])

    return (r1[0] + r2[0], r1[1] + r2[1], plan, r1[3] + r2[3])


def _gather_ride(shards):
    return (list(shards), [jax.ShapeDtypeStruct((N_CHIPS,) + s.shape, s.dtype) for s in shards],
            _gather_plan(len(shards)), 3 * len(shards))


def _scatter_ride(g_b, r0=0, nr=None):
    h = g_b.shape[1] // 2
    return ([g_b], [jax.ShapeDtypeStruct((8, h, g_b.shape[2]), BF16)], _scatter_plan(h, r0, h if nr is None else nr), 7)


def _gather_plan(n):
    def plan(srcs, lands):
        x, y, c, chips = _place()
        me = 2 * x + y
        return [(srcs[a], lands[a].at[me], lands[a].at[2 * chip[0] + chip[1]], (*chip, c))
                for a in range(n) for chip in chips]
    return plan


def _scatter_plan(h, r0, nr):
    def plan(srcs, lands):
        x, y, c, _ = _place()
        me = 4 * x + 2 * y + c
        out = []
        for p in range(1, 8):
            px = 1 - x if p & 4 else x
            py = 1 - y if p & 2 else y
            pc = 1 - c if p & 1 else c
            rows = pl.ds(pl.multiple_of(pc * h + r0, 8), nr)
            out.append((srcs[0].at[2 * px + py, rows], lands[0].at[me, pl.ds(r0, nr)],
                        lands[0].at[4 * px + 2 * py + pc, pl.ds(r0, nr)], (px, py, pc)))
        return out
    return plan


def _grad_sum_call(own, land, place, name):
    _, h, cols = land.shape
    th = _pick(h, [256, 176, 8])
    nt = h // th

    def body(p_ref, own_ref, *refs):
        acc = own_ref[...]
        for r in refs[:7]:
            acc = acc + r[...].astype(F32)
        refs[7][...] = acc

    def peer(k):
        return pl.BlockSpec((None, th, cols), lambda i, p_ref: (p_ref[2 + k], i, 0))

    return pl.pallas_call(
        body, name=name,
        grid_spec=pltpu.PrefetchScalarGridSpec(
            num_scalar_prefetch=1, grid=(nt,),
            in_specs=[pl.BlockSpec((None, th, cols), lambda i, p_ref: (p_ref[1], p_ref[0] * nt + i, 0))]
            + [peer(k) for k in range(7)],
            out_specs=pl.BlockSpec((th, cols), lambda i, p_ref: (p_ref[0] * nt + i, 0))),
        out_shape=jax.ShapeDtypeStruct((2 * h, cols), F32),
        compiler_params=_cparams(("parallel",)))(place, own, *[land] * 7)


def _half_exchange_call(shards):
    n = len(shards)

    def body(*refs):
        outs = refs[n:2 * n]
        send_d, recv_d = refs[2 * n:]
        x, y, c, _ = _place()
        cps = []
        for a in range(n):
            h = shards[a].shape[0] // 2
            mine = outs[a].at[_half(c, h)]
            cp = pltpu.make_async_remote_copy(
                src_ref=mine, dst_ref=mine, send_sem=send_d.at[a], recv_sem=recv_d.at[a],
                device_id=(x, y, 1 - c), device_id_type=MESH)
            cp.start()
            cps.append(cp)
        for a, cp in enumerate(cps):
            h = shards[a].shape[0] // 2
            theirs = outs[a].at[_half(1 - c, h)]
            pltpu.make_async_remote_copy(
                src_ref=theirs, dst_ref=theirs, send_sem=send_d.at[a], recv_sem=recv_d.at[a],
                device_id=(x, y, 1 - c), device_id_type=MESH).wait_recv()
            cp.wait_send()

    return pl.pallas_call(
        body, name="grad_half_exchange", in_specs=[ANY] * n, out_specs=[ANY] * n,
        out_shape=[jax.ShapeDtypeStruct(sv.shape, F32) for sv in shards],
        input_output_aliases={a: a for a in range(n)},
        scratch_shapes=[pltpu.SemaphoreType.DMA((n,))] * 2,
    )(*shards)


def _allreduce_small_call(arrs):
    n = len(arrs)
    offs, rows = [], 0
    for a in arrs:
        offs.append(rows)
        rows += a.shape[0]
    rows = -(-rows // 8) * 8
    width = -(-max(a.shape[1] for a in arrs) // BLK) * BLK

    def body(*refs):
        ins, outs = refs[:n], refs[n:2 * n]
        gath, send_sems, recv_sems = refs[2 * n:]
        x, y, c, chips = _place()
        me, sibling = (x, y, c), (x, y, 1 - c)

        def slot(px, py, pc):
            return gath.at[4 * px + 2 * py + pc]

        def copy(k, block, to):
            return pltpu.make_async_remote_copy(
                src_ref=slot(*block), dst_ref=slot(*block),
                send_sem=send_sems.at[k], recv_sem=recv_sems.at[k], device_id=to, device_id_type=MESH)

        mine = slot(*me)
        mine[...] = jnp.zeros((rows, width), F32)
        for k in range(n):
            r, w = arrs[k].shape
            mine[offs[k]:offs[k] + r, 0:w] = ins[k][...]
        first = [copy(0, me, sibling)]
        first += [copy(1 + j, me, (*chip, c)) for j, chip in enumerate(chips)]
        for cp in first:
            cp.start()
        passed = [copy(4 + j, (*chip, c), sibling) for j, chip in enumerate(chips)]
        for j, chip in enumerate(chips):
            copy(1 + j, (*chip, c), me).wait_recv()
            passed[j].start()
        copy(0, sibling, me).wait_recv()
        for j, chip in enumerate(chips):
            copy(4 + j, (*chip, 1 - c), me).wait_recv()
        for cp in first + passed:
            cp.wait_send()
        acc = gath[0]
        for d in range(1, 8):
            acc = acc + gath[d]
        for k in range(n):
            r, w = arrs[k].shape
            outs[k][...] = acc[offs[k]:offs[k] + r, 0:w]

    vm = pl.BlockSpec(memory_space=pltpu.VMEM)
    return pl.pallas_call(
        body, name="allreduce_small", in_specs=[vm] * n, out_specs=[vm] * n,
        out_shape=[jax.ShapeDtypeStruct(a.shape, F32) for a in arrs],
        scratch_shapes=[pltpu.VMEM((8, rows, width), F32), pltpu.SemaphoreType.DMA((7,)),
                        pltpu.SemaphoreType.DMA((7,))],
        compiler_params=pltpu.CompilerParams(vmem_limit_bytes=VMEM_LIMIT),
    )(*arrs)


def _adamw_small_call(ws, gs, ms, vs):
    n = len(ws)
    c1 = 1.0 - ADAM_B1 ** ADAM_STEP
    c2 = 1.0 - ADAM_B2 ** ADAM_STEP

    def body(*refs):
        for k in range(n):
            w_ref, g_ref, m_ref, v_ref = (refs[j * n + k] for j in range(4))
            d_ref, mo_ref, vo_ref = (refs[(4 + j) * n + k] for j in range(3))
            gv = g_ref[...]
            m2 = ADAM_B1 * m_ref[...] + (1.0 - ADAM_B1) * gv
            v2 = ADAM_B2 * v_ref[...] + (1.0 - ADAM_B2) * (gv * gv)
            d_ref[...] = -ADAM_LR * ((m2 / c1) / (jnp.sqrt(v2 / c2) + ADAM_EPS) + ADAM_WD * w_ref[...])
            mo_ref[...] = m2
            vo_ref[...] = v2

    vm = pl.BlockSpec(memory_space=pltpu.VMEM)
    res = pl.pallas_call(
        body, name="adamw_small", in_specs=[vm] * (4 * n), out_specs=[vm] * (3 * n),
        out_shape=[jax.ShapeDtypeStruct(a.shape, F32) for a in ws] * 3,
        compiler_params=pltpu.CompilerParams(vmem_limit_bytes=VMEM_LIMIT),
    )(*ws, *gs, *ms, *vs)
    return res[:n], res[n:2 * n], res[2 * n:]


def _pack(arrs, min_rows=8):
    parts = []
    for a in arrs:
        flat = a.reshape(-1).astype(F32)
        parts.append(jnp.pad(flat, (0, (-flat.shape[0]) % BLK)))
    buf = jnp.concatenate(parts).reshape(-1, BLK)
    return jnp.pad(buf, ((0, (-buf.shape[0]) % min_rows), (0, 0)))


def _unpack(buf, shapes):
    out, r = [], 0
    for shp in shapes:
        n = math.prod(shp)
        nr = -(-n // BLK)
        out.append(buf[r:r + nr].reshape(-1)[:n].reshape(shp))
        r += nr
    return out


SMALL = ["meta_tokens", "mix_pre_g", "ssd_conv_w", "ssd_conv_b", "ssd_dt_bias", "ssd_a_log", "ssd_d", "ssd_norm_g",
         "sb_norm_g", "mix_post_g", "ffn_pre_g", "ffn_conv_w", "ffn_conv_b", "ffn_post_g"]
BIG = ["w_in", "w_out", "w_up", "w_down"]
WEIGHTS = ["meta_tokens", "mix_pre_g", "w_in", "ssd_conv_w", "ssd_conv_b", "ssd_dt_bias", "ssd_a_log", "ssd_d",
           "ssd_norm_g", "sb_norm_g", "w_out", "mix_post_g", "ffn_pre_g", "w_up", "ffn_conv_w", "ffn_conv_b",
           "w_down", "ffn_post_g"]
W_IN_SHARD = IN_COLS // N_CHIPS
W_IN_PAD = 1536


def kernel(x, meta_tokens, mix_pre_g, w_in, ssd_conv_w, ssd_conv_b, ssd_dt_bias, ssd_a_log, ssd_d, ssd_norm_g, sb_norm_g, w_out, mix_post_g, ffn_pre_g, w_up, ffn_conv_w, ffn_conv_b, w_down, ffn_post_g, loss_target, m_meta_tokens, m_mix_pre_g, m_w_in, m_ssd_conv_w, m_ssd_conv_b, m_ssd_dt_bias, m_ssd_a_log, m_ssd_d, m_ssd_norm_g, m_sb_norm_g, m_w_out, m_mix_post_g, m_ffn_pre_g, m_w_up, m_ffn_conv_w, m_ffn_conv_b, m_w_down, m_ffn_post_g, v_meta_tokens, v_mix_pre_g, v_w_in, v_ssd_conv_w, v_ssd_conv_b, v_ssd_dt_bias, v_ssd_a_log, v_ssd_d, v_ssd_norm_g, v_sb_norm_g, v_w_out, v_mix_post_g, v_ffn_pre_g, v_w_up, v_ffn_conv_w, v_ffn_conv_b, v_w_down, v_ffn_post_g):
    w = dict(meta_tokens=meta_tokens, mix_pre_g=mix_pre_g, w_in=w_in, ssd_conv_w=ssd_conv_w, ssd_conv_b=ssd_conv_b, ssd_dt_bias=ssd_dt_bias, ssd_a_log=ssd_a_log, ssd_d=ssd_d, ssd_norm_g=ssd_norm_g, sb_norm_g=sb_norm_g, w_out=w_out, mix_post_g=mix_post_g, ffn_pre_g=ffn_pre_g, w_up=w_up, ffn_conv_w=ffn_conv_w, ffn_conv_b=ffn_conv_b, w_down=w_down, ffn_post_g=ffn_post_g)
    m = dict(meta_tokens=m_meta_tokens, mix_pre_g=m_mix_pre_g, w_in=m_w_in, ssd_conv_w=m_ssd_conv_w, ssd_conv_b=m_ssd_conv_b, ssd_dt_bias=m_ssd_dt_bias, ssd_a_log=m_ssd_a_log, ssd_d=m_ssd_d, ssd_norm_g=m_ssd_norm_g, sb_norm_g=m_sb_norm_g, w_out=m_w_out, mix_post_g=m_mix_post_g, ffn_pre_g=m_ffn_pre_g, w_up=m_w_up, ffn_conv_w=m_ffn_conv_w, ffn_conv_b=m_ffn_conv_b, w_down=m_w_down, ffn_post_g=m_ffn_post_g)
    v = dict(meta_tokens=v_meta_tokens, mix_pre_g=v_mix_pre_g, w_in=v_w_in, ssd_conv_w=v_ssd_conv_w, ssd_conv_b=v_ssd_conv_b, ssd_dt_bias=v_ssd_dt_bias, ssd_a_log=v_ssd_a_log, ssd_d=v_ssd_d, ssd_norm_g=v_ssd_norm_g, sb_norm_g=v_sb_norm_g, w_out=v_w_out, mix_post_g=v_mix_post_g, ffn_pre_g=v_ffn_pre_g, w_up=v_w_up, ffn_conv_w=v_ffn_conv_w, ffn_conv_b=v_ffn_conv_b, w_down=v_w_down, ffn_post_g=v_ffn_post_g)
    chip = 2 * lax.axis_index("x") + lax.axis_index("y")
    me = 2 * chip + lax.axis_index("c")
    place = jnp.stack([lax.axis_index("c"), chip] + [me ^ p for p in range(1, 8)]).astype(jnp.int32)

    shard_small = [w["meta_tokens"], w["ssd_conv_w"][0], w["ffn_conv_w"][0]]
    shards = [jnp.pad(w["w_in"][0], ((0, 0), (0, W_IN_PAD - W_IN_SHARD))).astype(BF16), _pack(shard_small, 16)]
    gathered = _allgather_call(shards)
    late_shards = [w["w_out"][0].astype(BF16), w["w_up"][0].astype(BF16), w["w_down"][0].astype(BF16)]

    def blocks(own, got):
        return [jnp.where(chip == i, own, got[i]) for i in range(N_CHIPS)]

    def late_weights(got):
        return (jnp.concatenate(blocks(late_shards[0], got[0]), axis=0),
                jnp.concatenate(blocks(late_shards[1], got[1]), axis=1),
                jnp.concatenate(blocks(late_shards[2], got[2]), axis=0))

    cut = DT_REAL_OFF + N_HEADS - W_IN_SHARD
    s_in = blocks(shards[0], gathered[0])
    w_in_c = jnp.concatenate(
        [s_in[0][:, :W_IN_SHARD], s_in[1][:, :cut], jnp.zeros((D_MODEL, BLK - N_HEADS), BF16),
         s_in[1][:, cut:W_IN_SHARD], s_in[2][:, :W_IN_SHARD], s_in[3][:, :W_IN_SHARD]], axis=1)
    parts = [_unpack(b, [s.shape for s in shard_small]) for b in blocks(shards[1], gathered[1])]
    wt = {k: w[k][0][None] if w[k].ndim == 3 else w[k] for k in
          ["mix_pre_g", "ssd_conv_b", "ssd_dt_bias", "ssd_a_log", "ssd_d", "ssd_norm_g", "sb_norm_g", "mix_post_g",
           "ffn_pre_g", "ffn_conv_b", "ffn_post_g"]}
    wt.update(
        meta=jnp.concatenate([p[0] for p in parts], axis=1),
        ssd_conv_w=jnp.concatenate([p[1] for p in parts], axis=1),
        ffn_conv_w=jnp.concatenate([p[2] for p in parts], axis=1), w_in=w_in_c)

    def w_in_shards(g):
        skip = BLK - N_HEADS
        cols = [g[:, :W_IN_SHARD],
                jnp.concatenate([g[:, W_IN_SHARD:W_IN_SHARD + cut], g[:, C_Q:2 * W_IN_SHARD + skip]], axis=1),
                g[:, 2 * W_IN_SHARD + skip:3 * W_IN_SHARD + skip], g[:, 3 * W_IN_SHARD + skip:]]
        g = jnp.stack([jnp.pad(b, ((0, 0), (0, W_IN_PAD - W_IN_SHARD))) for b in cols])
        return g, g.astype(BF16)

    loss_row, dx, small, pending = _local_step(x[0], loss_target[0], wt, late_shards, late_weights, w_in_shards)

    full = _half_exchange_call([_grad_sum_call(*pending[k], place, "grad_sum_" + k) for k in BIG])
    grads = {"w_in": full[0][:, :W_IN_SHARD], "w_out": full[1], "w_up": full[2], "w_down": full[3]}

    red_list = _allreduce_small_call([small[k] for k in SMALL] + [loss_row])
    loss = jnp.sum(red_list[-1])
    for k, g in zip(SMALL, red_list[:-1]):
        grads[k] = g
    for k in ["meta_tokens", "ssd_conv_w", "ffn_conv_w"]:
        wk = w[k].shape[-1]
        grads[k] = lax.dynamic_slice_in_dim(grads[k], chip * wk, wk, axis=1)

    delta, new_m, new_v = {}, {}, {}
    for k in BIG:
        delta[k], new_m[k], new_v[k] = _adamw_call(w[k][0], grads[k], m[k][0], v[k][0], "adamw_" + k)
    flat = lambda d: [d[k].reshape(grads[k].shape) for k in SMALL]
    res = _adamw_small_call(flat(w), [grads[k] for k in SMALL], flat(m), flat(v))
    for out, arrs in zip((delta, new_m, new_v), res):
        for k, a in zip(SMALL, arrs):
            out[k] = a

    def shaped(d, k):
        return d[k].reshape(w[k].shape)

    return (loss, dx[None], *[shaped(grads, k) for k in WEIGHTS], *[shaped(delta, k) for k in WEIGHTS],
            *[shaped(new_m, k) for k in WEIGHTS], *[shaped(new_v, k) for k in WEIGHTS])
```

```python
import functools
import math

import jax
import jax.numpy as jnp
from jax import lax
from jax.experimental import pallas as pl
from jax.experimental.pallas import tpu as pltpu

F32 = jnp.float32
BF16 = jnp.bfloat16

D_MODEL = 1024
N_META = 16
BLK = 128
PAD = BLK - N_META
HEAD_DIM = 64
N_HEADS = 16
SSD_GROUPS = 2
SSD_STATE = 128
SSD_INNER = 1024
XBC = SSD_INNER + 2 * SSD_GROUPS * SSD_STATE
D_FF = 2816
EPS = 1e-6
IN_COLS = 5648
C_Z, C_XBC, C_DT, C_Q, C_K, C_V, C_END = 0, 1024, 2560, 2688, 3712, 4736, 5760
DT_REAL_OFF = 2560
N_CHIPS = 4
ADAM_LR, ADAM_B1, ADAM_B2, ADAM_EPS, ADAM_WD, ADAM_STEP = 0.001, 0.9, 0.999, 1e-08, 0.01, 10
VMEM_LIMIT = 56 * 1024 * 1024
MESH = pl.DeviceIdType.MESH


def _cparams(sem=None, **kw):
    if sem is not None:
        kw["dimension_semantics"] = sem
    return pltpu.CompilerParams(vmem_limit_bytes=VMEM_LIMIT, **kw)


def _pick(n, cands):
    for c in cands:
        if n % c == 0:
            return c
    raise ValueError((n, cands))


def _iota(shape, dim):
    return lax.broadcasted_iota(jnp.int32, shape, dim)


def _sigmoid(x):
    return 1.0 / (1.0 + jnp.exp(-x))


def _split2(v):
    h1 = v.astype(BF16)
    return h1, (v - h1.astype(F32)).astype(BF16)


def _dot(a, b, ca=1, cb=0):
    return lax.dot_general(a, b, (((ca,), (cb,)), ((), ())), preferred_element_type=F32)


def _dot_sel_r(v, sel, cb=0):
    h1, h2 = _split2(v)
    return _dot(h1, sel, 1, cb) + _dot(h2, sel, 1, cb)


def _dot_sel_l(sel, v, ca=1):
    h1, h2 = _split2(v)
    return _dot(sel, h1, ca, 0) + _dot(sel, h2, ca, 0)


def _mm(a, b, *, ta=False, tb=False, tm, tn, tk, out_dtype=F32, nsplit=1, extra_bf16=False, ride=None, name):
    K, M = (a.shape if ta else a.shape[::-1])
    N = b.shape[0] if tb else b.shape[1]
    assert M % tm == 0 and N % tn == 0 and K % tk == 0, (name, M, N, K, tm, tn, tk)
    nm, nn, nk = M // tm, N // tn, K // tk
    assert nn % nsplit == 0
    per = nn // nsplit
    a_spec = (pl.BlockSpec((tk, tm), lambda i, j, k: (k, i)) if ta
              else pl.BlockSpec((tm, tk), lambda i, j, k: (i, k)))
    b_spec = (pl.BlockSpec((tn, tk), lambda i, j, k: (j, k)) if tb
              else pl.BlockSpec((tk, tn), lambda i, j, k: (k, j)))
    o_spec = pl.BlockSpec((None, tm, tn), lambda i, j, k: (j // per, i, j % per))
    n_out = 2 if extra_bf16 else 1
    ca, cb = (0 if ta else 1), (1 if tb else 0)
    ns, nl = (len(ride[0]), len(ride[1])) if ride else (0, 0)

    def body(a_ref, b_ref, *rest):
        outs = rest[ns:ns + n_out]
        if ride:
            step = (pl.program_id(0) * nn + pl.program_id(1)) * nk + pl.program_id(2)
            _ride_run(ride, rest[:ns], rest[ns + n_out:ns + n_out + nl], rest[-2], rest[-1],
                      step == 0, step == nm * nn * nk - 1)
        p = _dot(a_ref[...].astype(BF16), b_ref[...].astype(BF16), ca, cb)

        def emit(val):
            outs[0][...] = val.astype(out_dtype)
            if extra_bf16:
                outs[1][...] = val.astype(BF16)

        if nk == 1:
            emit(p)
        else:
            acc = rest[ns + n_out + nl]
            k = pl.program_id(2)

            @pl.when(k == 0)
            def _():
                acc[...] = p

            @pl.when(k > 0)
            def _():
                acc[...] += p

            @pl.when(k == nk - 1)
            def _():
                emit(acc[...])

    shp = (nsplit, M, N // nsplit)
    out_shape = [jax.ShapeDtypeStruct(shp, out_dtype)]
    out_specs = [o_spec]
    if extra_bf16:
        out_shape.append(jax.ShapeDtypeStruct(shp, BF16))
        out_specs.append(o_spec)
    scratch = [pltpu.VMEM((tm, tn), F32)] if nk > 1 else []
    if ride:
        res = pl.pallas_call(
            body, name=name, grid=(nm, nn, nk), in_specs=[a_spec, b_spec] + [ANY] * ns,
            out_specs=out_specs + [ANY] * nl, out_shape=out_shape + list(ride[1]),
            scratch_shapes=scratch + _ride_scratch(ride),
            compiler_params=_cparams(("arbitrary", "arbitrary", "arbitrary")),
        )(a, b, *ride[0])
        return (res[:n_out] if extra_bf16 else res[0]), list(res[n_out:])
    res = pl.pallas_call(
        body, name=name, grid=(nm, nn, nk), in_specs=[a_spec, b_spec], out_specs=out_specs,
        out_shape=out_shape, scratch_shapes=scratch,
        compiler_params=_cparams(("parallel", "parallel", "arbitrary")),
    )(a, b)
    return res if extra_bf16 else res[0]


def _rms_stats(x):
    r = lax.rsqrt(jnp.mean(x * x, axis=-1, keepdims=True) + EPS)
    return r, x * r


def _rms_bwd(x, g, dy):
    r, xh = _rms_stats(x)
    dxh = dy * g
    dx = r * (dxh - xh * jnp.mean(dxh * xh, axis=-1, keepdims=True))
    return dx, jnp.sum(dy * xh, axis=0, keepdims=True)


def _row_spec(tr, w, col=0):
    return pl.BlockSpec((tr, w), lambda i: (i, col))


def _vec_spec(w):
    return pl.BlockSpec((1, w), lambda i: (0, 0))


def _acc_rows(ref, val, i):
    @pl.when(i == 0)
    def _():
        ref[...] = val

    @pl.when(i > 0)
    def _():
        ref[...] += val


def _rms_fwd_call(x, g, name, beside=None):
    lp, w = x.shape
    tr = _pick(lp, [384, 128])

    def body(x_ref, g_ref, *rest):
        _, xh = _rms_stats(x_ref[...])
        rest[-1][...] = (xh * g_ref[...]).astype(BF16)

    if beside is None:
        return pl.pallas_call(
            body, name=name, grid=(lp // tr,), in_specs=[_row_spec(tr, w), _vec_spec(w)],
            out_specs=_row_spec(tr, w), out_shape=jax.ShapeDtypeStruct((lp, w), BF16),
            compiler_params=_cparams(("parallel",)))(x, g)
    return pl.pallas_call(
        body, name=name, grid=(lp // tr,), in_specs=[_row_spec(tr, w), _vec_spec(w), ANY],
        out_specs=_row_spec(tr, w, 1), out_shape=jax.ShapeDtypeStruct((lp, 2 * w), BF16),
        input_output_aliases={2: 0}, compiler_params=_cparams(("parallel",)))(x, g, beside)


def _mid_fwd_call(h0, mix, g_post, g_pre2):
    lp, w = h0.shape
    tr = _pick(lp, [384, 128])

    def body(h0_ref, mix_ref, gp_ref, g2_ref, h1_ref, xn_ref):
        _, mh = _rms_stats(mix_ref[...])
        h1 = h0_ref[...] + mh * gp_ref[...]
        h1_ref[...] = h1
        _, hh = _rms_stats(h1)
        xn_ref[...] = (hh * g2_ref[...]).astype(BF16)

    return pl.pallas_call(
        body, name="mid_fwd", grid=(lp // tr,),
        in_specs=[_row_spec(tr, w), _row_spec(tr, w), _vec_spec(w), _vec_spec(w)],
        out_specs=[_row_spec(tr, w), _row_spec(tr, w)],
        out_shape=[jax.ShapeDtypeStruct((lp, w), F32), jax.ShapeDtypeStruct((lp, w), BF16)],
        compiler_params=_cparams(("parallel",)))(h0, mix, g_post, g_pre2)


def _final_call(h1, f, g_post, target):
    lp, w = h1.shape
    tr = BLK
    nb = lp // tr

    def body(h1_ref, f_ref, g_ref, t_ref, loss_ref, df_ref, dh_ref, dg_ref):
        i = pl.program_id(0)
        fv = f_ref[...]
        g = g_ref[...]
        _, fh = _rms_stats(fv)
        h2 = h1_ref[...] + fh * g
        diff = jnp.where(i > 0, h2 - t_ref[...], 0.0)
        part = 0.5 * jnp.sum(diff * diff, axis=0, keepdims=True) * (1.0 / w)
        _acc_rows(loss_ref, part, i)
        dh = diff * (1.0 / w)
        dh_ref[...] = dh
        df, dg = _rms_bwd(fv, g, dh)
        df_ref[...] = df.astype(BF16)
        _acc_rows(dg_ref, dg, i)

    t_spec = pl.BlockSpec((tr, w), lambda i: (jnp.maximum(i - 1, 0), 0))
    return pl.pallas_call(
        body, name="final_fwd_bwd", grid=(nb,),
        in_specs=[_row_spec(tr, w), _row_spec(tr, w), _vec_spec(w), t_spec],
        out_specs=[_vec_spec(w), _row_spec(tr, w), _row_spec(tr, w), _vec_spec(w)],
        out_shape=[jax.ShapeDtypeStruct((1, w), F32), jax.ShapeDtypeStruct((lp, w), BF16),
                   jax.ShapeDtypeStruct((lp, w), F32), jax.ShapeDtypeStruct((1, w), F32)],
        compiler_params=_cparams(("arbitrary",)))(h1, f, g_post, target)


def _mid_bwd_call(dh2, h1, dxn2, mix, g_pre2, g_post):
    lp, w = h1.shape
    tr = _pick(lp, [384, 128])

    def body(dh2_ref, h1_ref, dxn_ref, mix_ref, g2_ref, gp_ref, dh1_ref, dmix_ref, dg2_ref, dgp_ref):
        i = pl.program_id(0)
        live = (i * tr + _iota((tr, 1), 0)) >= PAD
        dx, dg2 = _rms_bwd(h1_ref[...], g2_ref[...], dxn_ref[...])
        dh1 = jnp.where(live, dh2_ref[...] + dx, 0.0)
        dh1_ref[...] = dh1
        dmix, dgp = _rms_bwd(mix_ref[...], gp_ref[...], dh1)
        dmix_ref[...] = jnp.where(live, dmix, 0.0).astype(BF16)
        _acc_rows(dg2_ref, dg2, i)
        _acc_rows(dgp_ref, dgp, i)

    rs = _row_spec(tr, w)
    return pl.pallas_call(
        body, name="mid_bwd", grid=(lp // tr,),
        in_specs=[rs, rs, rs, rs, _vec_spec(w), _vec_spec(w)],
        out_specs=[rs, rs, _vec_spec(w), _vec_spec(w)],
        out_shape=[jax.ShapeDtypeStruct((lp, w), F32), jax.ShapeDtypeStruct((lp, w), BF16),
                   jax.ShapeDtypeStruct((1, w), F32), jax.ShapeDtypeStruct((1, w), F32)],
        compiler_params=_cparams(("arbitrary",)))(dh2, h1, dxn2, mix, g_pre2, g_post)


def _norm_bwd_call(x, g, dy_arr, dy_col, name, res=None, ride=None, land=None):
    lp, w = x.shape
    tr = _pick(lp, [384, 128])
    nsteps = lp // tr
    has_res = res is not None
    ns = len(ride[0]) if ride else 0
    first_out = (1 if has_res else 0) + (ns + 1 if ride else 0)

    def body(x_ref, g_ref, dy_ref, *rest):
        i = pl.program_id(0)
        if ride:
            srcs = rest[first_out - ns - 1:first_out - 1]
            _ride_run(ride, srcs, [rest[first_out + 2]], rest[-2], rest[-1], i == 0, i == nsteps - 1)
        live = (i * tr + _iota((tr, 1), 0)) >= PAD
        dx, dg = _rms_bwd(x_ref[...], g_ref[...], dy_ref[...])
        if has_res:
            dx = dx + rest[0][...]
        out_ref, dg_ref = rest[first_out], rest[first_out + 1]
        out_ref[...] = jnp.where(live, dx, 0.0)
        _acc_rows(dg_ref, dg, i)

    rs = _row_spec(tr, w)
    ins = [rs, _vec_spec(w), _row_spec(tr, w, dy_col)] + ([rs] if has_res else [])
    args = [x, g, dy_arr] + ([res] if has_res else [])
    outs = [rs, _vec_spec(w)]
    out_shape = [jax.ShapeDtypeStruct((lp, w), F32), jax.ShapeDtypeStruct((1, w), F32)]
    if not ride:
        return pl.pallas_call(
            body, name=name, grid=(nsteps,), in_specs=ins, out_specs=outs, out_shape=out_shape,
            compiler_params=_cparams(("arbitrary",)))(*args)
    return pl.pallas_call(
        body, name=name, grid=(nsteps,), in_specs=ins + [ANY] * (ns + 1), out_specs=outs + [ANY],
        out_shape=out_shape + [jax.ShapeDtypeStruct(land.shape, land.dtype)],
        input_output_aliases={len(args) + ns: 2}, scratch_shapes=_ride_scratch(ride),
        compiler_params=_cparams(("arbitrary",)))(*args, *ride[0], land)


def _shift_down(cur, prev_tail, s, rows):
    if s == 0:
        return cur
    prev = jnp.tile(prev_tail, (cur.shape[0] // 8, 1))
    return jnp.where(rows >= s, pltpu.roll(cur, s, 0), pltpu.roll(prev, s, 0))


def _shift_up(cur, next_head, s, rows):
    if s == 0:
        return cur
    n = cur.shape[0]
    nxt = jnp.tile(next_head, (n // 8, 1))
    return jnp.where(rows < n - s, pltpu.roll(cur, n - s, 0), pltpu.roll(nxt, n - s, 0))


def _gelu_tanh(x):
    c = math.sqrt(2.0 / math.pi)
    t = jnp.tanh(c * (x + 0.044715 * x * x * x))
    return 0.5 * x * (1.0 + t), t


def _conv_fwd_call(src, col0, width, cw, w8, b, taps, *, gate_src=None, gate_col0=0, rb=BLK, name):
    lp = src.shape[0]
    nb, nc = lp // rb, width // cw
    cb0 = col0 // cw
    ffn = gate_src is not None

    def body(x_ref, w_ref, b_ref, *rest):
        if ffn:
            u_ref, y_ref, a_ref, tail = rest
        else:
            y_ref, a_ref, tail = rest
        i = pl.program_id(1)

        @pl.when(i == 0)
        def _():
            tail[...] = jnp.zeros_like(tail)

        cur = x_ref[...]
        rows = _iota((rb, cw), 0)
        y = b_ref[...] + w_ref[taps - 1:taps, :] * cur
        pt = tail[...]
        for s in range(1, taps):
            y = y + w_ref[taps - 1 - s:taps - s, :] * _shift_down(cur, pt, s, rows)
        tail[...] = cur[rb - 8:, :]
        y_ref[...] = y
        if ffn:
            ge, _ = _gelu_tanh(y)
            a_ref[...] = (ge * u_ref[...]).astype(BF16)
        else:
            live = (i * rb + rows) >= PAD
            a_ref[...] = jnp.where(live, y * _sigmoid(y), 0.0)

    blk = lambda c0: pl.BlockSpec((rb, cw), lambda j, i: (i, c0 + j))
    ins = [blk(cb0), pl.BlockSpec((8, cw), lambda j, i: (0, j)), pl.BlockSpec((1, cw), lambda j, i: (0, j))]
    args = [src, w8, b]
    if ffn:
        ins.append(blk(gate_col0 // cw))
        args.append(gate_src)
    return pl.pallas_call(
        body, name=name, grid=(nc, nb), in_specs=ins, out_specs=[blk(0), blk(0)],
        out_shape=[jax.ShapeDtypeStruct((lp, width), F32),
                   jax.ShapeDtypeStruct((lp, width), BF16 if ffn else F32)],
        scratch_shapes=[pltpu.VMEM((8, cw), F32)],
        compiler_params=_cparams(("parallel", "arbitrary")))(*args)


def _conv_bwd_call(src, col0, width, cw, w8, taps, ypre, dact, *, gate_src=None, gate_col0=0, rb=BLK, name):
    lp = src.shape[0]
    nb, nc = lp // rb, width // cw
    cb0 = col0 // cw
    ffn = gate_src is not None

    def body(x_ref, w_ref, y_ref, d_ref, *rest):
        if ffn:
            u_ref, dx_ref, du_ref, dw_ref, db_ref, head = rest
        else:
            dx_ref, dw_ref, db_ref, head = rest
        step = pl.program_id(1)
        i = nb - 1 - step

        @pl.when(step == 0)
        def _():
            head[...] = jnp.zeros_like(head)

        rows = _iota((rb, cw), 0)
        live = (i * rb + rows) >= PAD
        y = y_ref[...]
        d = d_ref[...]
        if ffn:
            ge, t = _gelu_tanh(y)
            c = math.sqrt(2.0 / math.pi)
            dge = 0.5 * (1.0 + t) + 0.5 * y * (1.0 - t * t) * c * (1.0 + 3.0 * 0.044715 * y * y)
            u = u_ref[...]
            du_ref[...] = jnp.where(live, d * ge, 0.0).astype(BF16)
            dy = jnp.where(live, d * u * dge, 0.0)
        else:
            sg = _sigmoid(y)
            dy = jnp.where(live, d * sg * (1.0 + y * (1.0 - sg)), 0.0)
        x = x_ref[...]
        nh = head[...]
        dx = jnp.zeros_like(dy)
        dws = []
        for s in range(taps):
            sh = _shift_up(dy, nh, s, rows)
            dx = dx + w_ref[taps - 1 - s:taps - s, :] * sh
            dws.append(jnp.sum(x * sh, axis=0, keepdims=True))
        head[...] = dy[:8, :]
        dx_ref[...] = jnp.where(live, dx, 0.0).astype(BF16)
        dw = jnp.concatenate([dws[taps - 1 - k] for k in range(taps)]
                             + [jnp.zeros((8 - taps, cw), F32)], axis=0)
        _acc_rows(dw_ref, dw, step)
        _acc_rows(db_ref, jnp.sum(dy, axis=0, keepdims=True), step)

    blk = lambda c0: pl.BlockSpec((rb, cw), lambda j, s: (nb - 1 - s, c0 + j))
    ins = [blk(cb0), pl.BlockSpec((8, cw), lambda j, s: (0, j)), blk(0), blk(0)]
    args = [src, w8, ypre, dact]
    outs = [blk(0)]
    oshape = [jax.ShapeDtypeStruct((lp, width), BF16)]
    if ffn:
        ins.append(blk(gate_col0 // cw))
        args.append(gate_src)
        outs.append(blk(0))
        oshape.append(jax.ShapeDtypeStruct((lp, width), BF16))
    outs += [pl.BlockSpec((8, cw), lambda j, s: (0, j)), pl.BlockSpec((1, cw), lambda j, s: (0, j))]
    oshape += [jax.ShapeDtypeStruct((8, width), F32), jax.ShapeDtypeStruct((1, width), F32)]
    return pl.pallas_call(
        body, name=name, grid=(nc, nb), in_specs=ins, out_specs=outs, out_shape=oshape,
        scratch_shapes=[pltpu.VMEM((8, cw), F32)],
        compiler_params=_cparams(("parallel", "arbitrary")))(*args)


SB_FIRST = 3
SB_GROUP = 4
SB_DEAD = -110.0


def _sb_scores(qm_h, kb):
    z = _dot(qm_h, kb, 1, 1)
    sp = jnp.maximum(z, 0.0) + jnp.log(1.0 + jnp.exp(-jnp.abs(z)))
    return z - sp, -sp


def _dot_tri1(v, tri2):
    r = _dot(v.astype(BF16), tri2[:BLK])
    return r[:, :BLK], r[:, BLK:]


def _tri2(cond):
    t = jnp.concatenate([cond.astype(BF16), jnp.ones((BLK, BLK), BF16)], axis=1)
    return jnp.concatenate([t, t], axis=0)


def _dot_tri(v, tri2):
    hi = v.astype(BF16)
    lo = (v - hi.astype(F32)).astype(BF16)
    r = _dot(jnp.concatenate([hi, lo], axis=1), tri2)
    return r[:, :BLK], r[:, BLK:]


def _sb_fwd_call(proj, ride):
    lp = proj.shape[0]
    nb = lp // BLK
    scale = 1.0 / math.sqrt(HEAD_DIM)

    ns, nl = len(ride[0]), len(ride[1])
    npair = N_HEADS // 2

    def body(q_ref, k_ref, v_ref, *rest):
        o_ref, tl_ref = rest[ns], rest[ns + 1]
        i = pl.program_id(1)
        step = pl.program_id(0) * nb + i
        _ride_run(ride, rest[:ns], rest[ns + 2:ns + 2 + nl], rest[-2], rest[-1], step == 0, step == npair * nb - 1)
        lane = _iota((2 * BLK, BLK), 1)
        row = _iota((2 * BLK, BLK), 0)
        first = row < BLK
        qrow = row & (BLK - 1)
        q = q_ref[...] * scale
        q2 = jnp.concatenate([q, q], axis=0)
        qm = jnp.where(first == (lane < HEAD_DIM), q2, 0.0).astype(BF16)
        tri_ref = rest[ns + 2 + nl]

        @pl.when(step == 0)
        def _():
            tri_ref[...] = _tri2(_iota((BLK, BLK), 0) > _iota((BLK, BLK), 1))

        tri = tri_ref[...]

        def chunk(off, nsub, last_valid, carry):
            width = nsub * BLK
            sls = [slice(b * BLK, (b + 1) * BLK) for b in range(nsub)]
            kb = k_ref[pl.ds(off, width), :].astype(BF16)
            vb = v_ref[pl.ds(off, width), :].astype(BF16)
            lb, lk = _sb_scores(qm, kb)
            lks = [lk[:, sl] for sl in sls]
            first_valid = (off + lane) >= PAD
            lks[0] = jnp.where(first_valid, lks[0], 0.0)
            if last_valid is not None:
                lks[-1] = jnp.where(last_valid, lks[-1], 0.0)
            afters = [_dot_tri(lks[b], tri) for b in range(nsub)]
            run, acc = carry
            ws = [None] * nsub
            for b in reversed(range(nsub)):
                wb = jnp.exp(lb[:, sls[b]] + afters[b][0] + run)
                if b == 0:
                    wb = jnp.where(first_valid, wb, 0.0)
                if last_valid is not None and b == nsub - 1:
                    wb = jnp.where(last_valid, wb, 0.0)
                ws[b] = wb.astype(BF16)
                run = run + afters[b][1]
            w = ws[0] if nsub == 1 else jnp.concatenate(ws, axis=1)
            return run, acc + _dot(w, vb)

        before = jnp.minimum(i, SB_FIRST - 1)
        first_off = pl.multiple_of((i - before) * BLK, BLK)
        diag = lane < qrow
        zero = jnp.zeros((2 * BLK, BLK), F32)
        carry = lax.switch(before, [functools.partial(chunk, first_off, n, diag) for n in range(1, SB_FIRST + 1)],
                           (zero, zero))

        def walk(n):
            def body(state):
                off = pl.multiple_of((state[0] - (n - 1)) * BLK, BLK)
                return (state[0] - n, *chunk(off, n, None, state[1:]))

            def cond(state):
                return jnp.logical_and(state[0] >= n - 1, jnp.max(state[1]) > SB_DEAD)

            return cond, body

        state = lax.while_loop(*walk(SB_GROUP), (i - SB_FIRST, *carry))
        pos, run, acc = lax.while_loop(*walk(1), state)
        low = lane[:BLK] < HEAD_DIM
        o_ref[...] = jnp.where(low, acc[:BLK], acc[BLK:])
        tl = jnp.where(low, run[:BLK], run[BLK:])
        tl_ref[...] = jnp.where(lane[:BLK] == 1, jnp.maximum(pos + 1, 0).astype(F32), tl)

    qc, kc, vc = C_Q // BLK, C_K // BLK, C_V // BLK
    blk = pl.BlockSpec((BLK, BLK), lambda p, i: (i, p))
    res = pl.pallas_call(
        body, name="sb_fwd", grid=(npair, nb),
        in_specs=[pl.BlockSpec((BLK, BLK), lambda p, i: (i, qc + p)),
                  pl.BlockSpec((lp, BLK), lambda p, i: (0, kc + p)),
                  pl.BlockSpec((lp, BLK), lambda p, i: (0, vc + p))] + [ANY] * ns,
        out_specs=[blk, blk] + [ANY] * nl,
        out_shape=[jax.ShapeDtypeStruct((lp, N_HEADS * HEAD_DIM), F32)] * 2 + list(ride[1]),
        scratch_shapes=[pltpu.VMEM((2 * BLK, 2 * BLK), BF16)] + _ride_scratch(ride),
        compiler_params=_cparams(("arbitrary", "arbitrary")))(proj, proj, proj, *ride[0])
    return res[0], res[1], list(res[2:])


def _sb_bwd_call(proj, tl, do, ride):
    lp = proj.shape[0]
    nb = lp // BLK
    scale = 1.0 / math.sqrt(HEAD_DIM)

    ns, nl = len(ride[0]), len(ride[1])
    npair = N_HEADS // 2

    def body(q_ref, k_ref, v_ref, tl_ref, do_ref, *rest):
        dq_ref, dk_ref, dv_ref = rest[ns:ns + 3]
        dk_acc, dv_acc = rest[ns + 3 + nl:ns + 5 + nl]
        i = pl.program_id(1)
        step = pl.program_id(0) * nb + i
        _ride_run(ride, rest[:ns], rest[ns + 3:ns + 3 + nl], rest[-2], rest[-1], step == 0, step == npair * nb - 1)

        @pl.when(i == 0)
        def _():
            dk_acc[...] = jnp.zeros_like(dk_acc)
            dv_acc[...] = jnp.zeros_like(dv_acc)

        lane = _iota((2 * BLK, BLK), 1)
        row = _iota((2 * BLK, BLK), 0)
        qrow = row & (BLK - 1)
        mine = (row < BLK) == (lane < HEAD_DIM)
        q = q_ref[...] * scale
        dov = do_ref[...]
        qm = jnp.where(mine, jnp.concatenate([q, q], axis=0), 0.0).astype(BF16)
        dom = jnp.where(mine, jnp.concatenate([dov, dov], axis=0), 0.0).astype(BF16)
        tlv = tl_ref[...]
        tot = jnp.concatenate([jnp.broadcast_to(tlv[:, 0:1], (BLK, BLK)),
                               jnp.broadcast_to(tlv[:, HEAD_DIM:HEAD_DIM + 1], (BLK, BLK))], axis=0)
        tri_in_ref, tri_ex_ref = rest[ns + 5 + nl:ns + 7 + nl]

        @pl.when(step == 0)
        def _():
            r1, l1 = _iota((BLK, BLK), 0), _iota((BLK, BLK), 1)
            tri_in_ref[...] = _tri2(r1 <= l1)
            tri_ex_ref[...] = _tri2(r1 < l1)

        tri_in, tri_ex = tri_in_ref[...], tri_ex_ref[...]

        def chunk(off, nsub, last_valid, carry):
            width = nsub * BLK
            sls = [slice(b * BLK, (b + 1) * BLK) for b in range(nsub)]
            cat = lambda parts: parts[0] if nsub == 1 else jnp.concatenate(parts, axis=1)
            mask_last = lambda b: last_valid is not None and b == nsub - 1
            kb = k_ref[pl.ds(off, width), :].astype(BF16)
            vb = v_ref[pl.ds(off, width), :].astype(BF16)
            lb, lk = _sb_scores(qm, kb)
            dw = _dot(dom, vb, 1, 1)
            lks = [lk[:, sl] for sl in sls]
            first_valid = (off + lane) >= PAD
            lks[0] = jnp.where(first_valid, lks[0], 0.0)
            if last_valid is not None:
                lks[-1] = jnp.where(last_valid, lks[-1], 0.0)
            pins = [_dot_tri(lks[b], tri_in) for b in range(nsub)]
            run, gsum, dq = carry
            ws, gs = [], []
            for b in range(nsub):
                wb = jnp.exp(lb[:, sls[b]] + (tot - run - pins[b][0]))
                if b == 0:
                    wb = jnp.where(first_valid, wb, 0.0)
                if mask_last(b):
                    wb = jnp.where(last_valid, wb, 0.0)
                ws.append(wb.astype(BF16))
                gs.append(wb * dw[:, sls[b]])
                run = run + pins[b][1]
            gexs = [_dot_tri1(gs[b], tri_ex) for b in range(nsub)]
            beta = jnp.exp(lb)
            parts = []
            for b in range(nsub):
                bt = beta[:, sls[b]]
                dzb = gs[b] * (1.0 - bt) - (gsum + gexs[b][0]) * bt
                if b == 0:
                    dzb = jnp.where(first_valid, dzb, 0.0)
                if mask_last(b):
                    dzb = jnp.where(last_valid, dzb, 0.0)
                parts.append(dzb.astype(BF16))
                gsum = gsum + gexs[b][1]
            dz, w = cat(parts), cat(ws)
            dk_acc[pl.ds(off, width), :] += _dot(dz, qm, 0, 0)
            dv_acc[pl.ds(off, width), :] += _dot(w, dom, 0, 0)
            return run, gsum, dq + _dot(dz, kb)

        diag = lane < qrow
        zero = jnp.zeros((2 * BLK, BLK), F32)
        top = i - SB_FIRST

        def walk(n):
            def body(state):
                off = pl.multiple_of(state[0] * BLK, BLK)
                return (state[0] + n, *chunk(off, n, None, state[1:]))

            return (lambda state: state[0] + (n - 1) <= top), body

        state = (jnp.max(tlv[:, 1:2]).astype(jnp.int32), zero, zero, zero)
        state = lax.while_loop(*walk(SB_GROUP), state)
        carry = lax.while_loop(*walk(1), state)[1:]
        before = jnp.minimum(i, SB_FIRST - 1)
        first_off = pl.multiple_of((i - before) * BLK, BLK)
        dq = lax.switch(before, [functools.partial(chunk, first_off, n, diag) for n in range(1, SB_FIRST + 1)],
                        carry)[2]
        dq_ref[...] = (jnp.where(lane[:BLK] < HEAD_DIM, dq[:BLK], dq[BLK:]) * scale).astype(BF16)

        @pl.when(i == nb - 1)
        def _():
            dk_ref[...] = dk_acc[...].astype(BF16)
            dv_ref[...] = dv_acc[...].astype(BF16)

    qc, kc, vc = C_Q // BLK, C_K // BLK, C_V // BLK
    blk = pl.BlockSpec((BLK, BLK), lambda p, i: (i, p))
    full = pl.BlockSpec((lp, BLK), lambda p, i: (0, p))
    w = N_HEADS * HEAD_DIM
    res = pl.pallas_call(
        body, name="sb_bwd", grid=(npair, nb),
        in_specs=[pl.BlockSpec((BLK, BLK), lambda p, i: (i, qc + p)),
                  pl.BlockSpec((lp, BLK), lambda p, i: (0, kc + p)),
                  pl.BlockSpec((lp, BLK), lambda p, i: (0, vc + p)),
                  blk, blk] + [ANY] * ns,
        out_specs=[blk, full, full] + [ANY] * nl,
        out_shape=[jax.ShapeDtypeStruct((lp, w), BF16)] * 3 + list(ride[1]),
        scratch_shapes=[pltpu.VMEM((lp, BLK), F32), pltpu.VMEM((lp, BLK), F32),
                        pltpu.VMEM((2 * BLK, 2 * BLK), BF16), pltpu.VMEM((2 * BLK, 2 * BLK), BF16)]
        + _ride_scratch(ride),
        compiler_params=_cparams(("arbitrary", "arbitrary")))(proj, proj, proj, tl, do, *ride[0])
    return res[0], res[1], res[2], list(res[3:])


def _log1p(e):
    u = 1.0 + e
    return jnp.where(u == 1.0, e, jnp.log(u) * e / jnp.where(u == 1.0, 1.0, u - 1.0))


def _ssd_common(c, dtr, bias, alog):
    row = _iota((BLK, BLK), 0)
    lane = _iota((BLK, BLK), 1)
    live = ((c * BLK + row) >= PAD) & (lane < N_HEADS)
    pre = dtr + bias
    dt = jnp.where(live, jnp.maximum(pre, 0.0) + _log1p(jnp.exp(-jnp.abs(pre))), 0.0)
    a_neg = -jnp.exp(alog)
    a = dt * a_neg
    t_in = (lane <= row).astype(BF16)
    cs = _dot_sel_l(t_in, a)
    cs_t = cs.T
    cs_end = cs[BLK - 1:BLK, :]
    e = jnp.exp(cs)
    f = jnp.exp(cs_end - cs)
    xp = ((_iota((BLK, SSD_INNER), 1) // HEAD_DIM) == _iota((BLK, SSD_INNER), 0)).astype(BF16)
    xp_t = ((_iota((SSD_INNER, BLK), 0) // HEAD_DIM) == _iota((SSD_INNER, BLK), 1)).astype(BF16)
    decay_col = _dot_sel_l(xp_t, jnp.exp(cs_t))[:, BLK - 1:BLK]
    return dict(live=live, pre=pre, dt=dt, a_neg=a_neg, cs=cs, cs_t=cs_t, e=e, f=f, xp=xp, xp_t=xp_t,
                decay_col=decay_col, row=row, lane=lane,
                dt_x=_dot_sel_r(dt, xp), e_x=_dot_sel_r(e, xp), f_x=_dot_sel_r(f, xp))


def _ssd_ldec(q, h):
    diff = q["cs"][:, h:h + 1] - q["cs_t"][h:h + 1, :]
    causal = q["row"] >= q["lane"]
    return jnp.where(causal, jnp.exp(jnp.where(causal, diff, 0.0)), 0.0)


def _ssd_fwd_call(xbc, proj, bias, alog, d_x, norm_g):
    lp = xbc.shape[0]
    nb = lp // BLK
    gw = SSD_INNER // SSD_GROUPS
    ppg = gw // BLK

    def body(xbc_ref, dtr_ref, z_ref, bias_ref, alog_ref, dx_ref, ng_ref, yb_ref, ypre_ref, sprev_ref, s_ref):
        c = pl.program_id(0)

        @pl.when(c == 0)
        def _():
            s_ref[...] = jnp.zeros_like(s_ref)

        q = _ssd_common(c, dtr_ref[...], bias_ref[...], alog_ref[...])
        x = xbc_ref[:, 0:SSD_INNER]
        xd = x * q["dt_x"]
        low = q["lane"] < HEAD_DIM
        s_old = s_ref[...]
        sprev_ref[...] = s_old
        xdf = (xd * q["f_x"]).astype(BF16)
        for g in range(SSD_GROUPS):
            bg = xbc_ref[:, SSD_INNER + g * SSD_STATE:SSD_INNER + (g + 1) * SSD_STATE].astype(BF16)
            cg = xbc_ref[:, SSD_INNER + (SSD_GROUPS + g) * SSD_STATE:
                         SSD_INNER + (SSD_GROUPS + g + 1) * SSD_STATE].astype(BF16)
            cb = _dot(cg, bg, 1, 1)
            gs = slice(g * gw, (g + 1) * gw)
            y_off = _dot(cg, s_old[gs, :].astype(BF16), 1, 1) * q["e_x"][:, gs]
            s_ref[gs, :] = s_old[gs, :] * q["decay_col"][gs, :] + _dot(xdf[:, gs], bg, 0, 0)
            for pr in range(ppg):
                cols = slice(g * gw + pr * BLK, g * gw + (pr + 1) * BLK)
                xd_p = xd[:, cols]
                acc = y_off[:, pr * BLK:(pr + 1) * BLK]
                for hh in range(2):
                    h = (g * gw + pr * BLK) // HEAD_DIM + hh
                    m = (cb * _ssd_ldec(q, h)).astype(BF16)
                    xm = jnp.where(low, xd_p, 0.0) if hh == 0 else jnp.where(low, 0.0, xd_p)
                    acc = acc + _dot(m, xm.astype(BF16))
                ypre_ref[:, cols] = acc
        ypre = ypre_ref[...] + x * dx_ref[...]
        ypre_ref[...] = ypre
        z = z_ref[...]
        yg = ypre * (z * _sigmoid(z))
        _, yh = _rms_stats(yg)
        yb_ref[...] = (yh * ng_ref[...]).astype(BF16)

    row = lambda w, col: pl.BlockSpec((BLK, w), lambda c: (c, col))
    vec = lambda w: pl.BlockSpec((1, w), lambda c: (0, 0))
    return pl.pallas_call(
        body, name="ssd_fwd", grid=(nb,),
        in_specs=[row(XBC, 0), row(BLK, C_DT // BLK), row(SSD_INNER, 0), vec(BLK), vec(BLK),
                  vec(SSD_INNER), vec(SSD_INNER)],
        out_specs=[row(SSD_INNER, 0), row(SSD_INNER, 0),
                   pl.BlockSpec((None, SSD_INNER, SSD_STATE), lambda c: (c, 0, 0))],
        out_shape=[jax.ShapeDtypeStruct((lp, 2 * SSD_INNER), BF16), jax.ShapeDtypeStruct((lp, SSD_INNER), F32),
                   jax.ShapeDtypeStruct((nb, SSD_INNER, SSD_STATE), F32)],
        scratch_shapes=[pltpu.VMEM((SSD_INNER, SSD_STATE), F32)],
        compiler_params=_cparams(("arbitrary",)))(xbc, proj, proj, bias, alog, d_x, norm_g)


def _ssd_bwd_call(dycat, ypre, xbc, proj, sprev, bias, alog, d_x, norm_g):
    lp = xbc.shape[0]
    nb = lp // BLK
    gw = SSD_INNER // SSD_GROUPS
    ppg = gw // BLK

    def body(dy_ref, ypre_ref, xbc_ref, dtr_ref, z_ref, sp_ref, bias_ref, alog_ref, dxp_ref, ng_ref,
             dz_ref, dxbc_ref, ddt_ref, dng_ref, dd_ref, dal_ref, dbi_ref, ds_ref, dxd_ref):
        step = pl.program_id(0)
        c = nb - 1 - step

        @pl.when(step == 0)
        def _():
            ds_ref[...] = jnp.zeros_like(ds_ref)

        q = _ssd_common(c, dtr_ref[...], bias_ref[...], alog_ref[...])
        row, lane = q["row"], q["lane"]
        low = lane < HEAD_DIM
        rowlive = ((c * BLK + _iota((BLK, 1), 0)) >= PAD)
        x = xbc_ref[:, 0:SSD_INNER]
        xd = x * q["dt_x"]
        z = z_ref[...]
        sz = _sigmoid(z)
        silu = z * sz
        ypre = ypre_ref[...]
        dyg, dng = _rms_bwd(ypre * silu, ng_ref[...], dy_ref[...])
        _acc_rows(dng_ref, dng, step)
        dyp = dyg * silu
        dz_ref[...] = jnp.where(rowlive, dyg * ypre * (sz * (1.0 + z * (1.0 - sz))), 0.0).astype(BF16)
        _acc_rows(dd_ref, jnp.sum(dyp * x, axis=0, keepdims=True), step)
        dye = dyp * q["e_x"]
        xdf = xd * q["f_x"]
        s_prev = sp_ref[...]
        ds_old = ds_ref[...]
        qrow = jnp.zeros((BLK, BLK), F32)
        qcol_t = jnp.zeros((BLK, BLK), F32)
        red_e = []
        red_f = []
        for g in range(SSD_GROUPS):
            gs = slice(g * gw, (g + 1) * gw)
            bsl = slice(SSD_INNER + g * SSD_STATE, SSD_INNER + (g + 1) * SSD_STATE)
            csl = slice(SSD_INNER + (SSD_GROUPS + g) * SSD_STATE, SSD_INNER + (SSD_GROUPS + g + 1) * SSD_STATE)
            bg = xbc_ref[:, bsl].astype(BF16)
            cg = xbc_ref[:, csl].astype(BF16)
            sg = s_prev[gs, :].astype(BF16)
            dsg = ds_old[gs, :].astype(BF16)
            cb = _dot(cg, bg, 1, 1)
            bds = _dot(bg, dsg, 1, 1)
            y_off = _dot(cg, sg, 1, 1) * q["e_x"][:, gs]
            red_e.append(dyp[:, gs] * y_off)
            red_f.append(xd[:, gs] * bds * q["f_x"][:, gs])
            dc = _dot(dye[:, gs].astype(BF16), sg)
            db = _dot(xdf[:, gs].astype(BF16), dsg)
            ds_ref[gs, :] = ds_old[gs, :] * q["decay_col"][gs, :] + _dot(dye[:, gs].astype(BF16), cg, 0, 0)
            dcb = jnp.zeros((BLK, BLK), F32)
            for pr in range(ppg):
                cols = slice(g * gw + pr * BLK, g * gw + (pr + 1) * BLK)
                xd_p = xd[:, cols].astype(BF16)
                dy_p = dyp[:, cols]
                acc = q["f_x"][:, cols] * bds[:, pr * BLK:(pr + 1) * BLK]
                for hh in range(2):
                    h = (g * gw + pr * BLK) // HEAD_DIM + hh
                    ld = _ssd_ldec(q, h)
                    m = cb * ld
                    dym = (jnp.where(low, dy_p, 0.0) if hh == 0 else jnp.where(low, 0.0, dy_p)).astype(BF16)
                    dm = jnp.where(row >= lane, _dot(dym, xd_p, 1, 1), 0.0)
                    acc = acc + _dot(m.astype(BF16), dym, 0, 0)
                    dcb = dcb + dm * ld
                    qq = dm * m
                    qrow = qrow + jnp.where(lane == h, jnp.sum(qq, axis=1, keepdims=True), 0.0)
                    qcol_t = qcol_t + jnp.where(row == h, jnp.sum(qq, axis=0, keepdims=True), 0.0)
                dxd_ref[:, cols] = acc
            dcbb = dcb.astype(BF16)
            dxbc_ref[:, bsl] = jnp.where(rowlive, db + _dot(dcbb, cg, 0, 0), 0.0)
            dxbc_ref[:, csl] = jnp.where(rowlive, dc + _dot(dcbb, bg), 0.0)
        dxd = dxd_ref[...]
        dxbc_ref[:, 0:SSD_INNER] = jnp.where(rowlive, dxd * q["dt_x"] + dyp * dxp_ref[...], 0.0)
        xp_t = q["xp_t"]
        fw = _dot_sel_r(jnp.concatenate(red_f, axis=1), xp_t)
        dcs = qrow - qcol_t.T + _dot_sel_r(jnp.concatenate(red_e, axis=1), xp_t) - fw
        end_f = jnp.sum(fw, axis=0, keepdims=True)
        sds = jnp.sum(ds_old * s_prev, axis=1, keepdims=True)
        per_head = _dot_sel_l(q["xp"], jnp.broadcast_to(sds, (SSD_INNER, BLK)))
        end_e = per_head.T[0:1, :] * jnp.exp(q["cs"][BLK - 1:BLK, :])
        dcs = dcs + jnp.where(row == BLK - 1, end_f + end_e, 0.0)
        t_up = (lane >= row).astype(BF16)
        da = _dot_sel_l(t_up, dcs)
        ddt = da * q["a_neg"] + _dot_sel_r(dxd * x, xp_t)
        _acc_rows(dal_ref, jnp.sum(da * q["dt"] * q["a_neg"], axis=0, keepdims=True), step)
        ddtr = jnp.where(q["live"], ddt * _sigmoid(q["pre"]), 0.0)
        ddt_ref[...] = ddtr.astype(BF16)
        _acc_rows(dbi_ref, jnp.sum(ddtr, axis=0, keepdims=True), step)

    row_s = lambda w, col: pl.BlockSpec((BLK, w), lambda s: (nb - 1 - s, col))
    vec = lambda w: pl.BlockSpec((1, w), lambda s: (0, 0))
    return pl.pallas_call(
        body, name="ssd_bwd", grid=(nb,),
        in_specs=[row_s(SSD_INNER, 0), row_s(SSD_INNER, 0), row_s(XBC, 0), row_s(BLK, C_DT // BLK),
                  row_s(SSD_INNER, 0), pl.BlockSpec((None, SSD_INNER, SSD_STATE), lambda s: (nb - 1 - s, 0, 0)),
                  vec(BLK), vec(BLK), vec(SSD_INNER), vec(SSD_INNER)],
        out_specs=[row_s(SSD_INNER, 0), row_s(XBC, 0), row_s(BLK, 0),
                   vec(SSD_INNER), vec(SSD_INNER), vec(BLK), vec(BLK)],
        out_shape=[jax.ShapeDtypeStruct((lp, SSD_INNER), BF16), jax.ShapeDtypeStruct((lp, XBC), F32),
                   jax.ShapeDtypeStruct((lp, BLK), BF16),
                   jax.ShapeDtypeStruct((1, SSD_INNER), F32), jax.ShapeDtypeStruct((1, SSD_INNER), F32),
                   jax.ShapeDtypeStruct((1, BLK), F32), jax.ShapeDtypeStruct((1, BLK), F32)],
        scratch_shapes=[pltpu.VMEM((SSD_INNER, SSD_STATE), F32), pltpu.VMEM((BLK, SSD_INNER), F32)],
        compiler_params=_cparams(("arbitrary",)))(dycat, ypre, xbc, proj, proj, sprev, bias, alog, d_x, norm_g)


def _pad_rows8(w):
    return jnp.pad(w, ((0, 8 - w.shape[0]), (0, 0)))


def _pad_lanes(v, n=BLK):
    return jnp.pad(v, ((0, 0), (0, n - v.shape[1])))


def _local_step(x, target, wt, late_shards, late_weights, w_in_shards):
    seq = x.shape[0]
    lp = seq + BLK
    tm = _pick(lp, [1408, 768, 384, 128])
    tkr = _pick(lp, [1408, 384, 128])
    rbc = _pick(lp, [384, 128])
    h0 = jnp.concatenate([jnp.zeros((PAD, D_MODEL), F32), wt["meta"], x], axis=0)
    bias = _pad_lanes(wt["ssd_dt_bias"])
    alog = _pad_lanes(wt["ssd_a_log"])
    d_x = jnp.repeat(wt["ssd_d"], HEAD_DIM, axis=1)
    cw8 = _pad_rows8(wt["ssd_conv_w"])
    fw8 = _pad_rows8(wt["ffn_conv_w"])
    fcw = D_FF // 2

    xn1 = _rms_fwd_call(h0, wt["mix_pre_g"], "norm1")
    proj, late_a = _mm(xn1, wt["w_in"], tm=tm, tn=1152, tk=D_MODEL, ride=_gather_ride(late_shards[0:1]),
                       name="mm_proj")
    proj = proj[0]
    conv_pre, xbc = _conv_fwd_call(proj, C_XBC, XBC, 512, cw8, wt["ssd_conv_b"], 4, rb=rbc, name="ssd_conv_fwd")
    y_ssd, ypre, sprev = _ssd_fwd_call(xbc, proj, bias, alog, d_x, wt["ssd_norm_g"])
    o, tl, late_b = _sb_fwd_call(proj, _gather_ride(late_shards[1:3]))
    ycat = _rms_fwd_call(o, wt["sb_norm_g"], "sb_norm", beside=y_ssd)
    w_out, w_up, w_down = late_weights([late_a[0], late_b[0], late_b[1]])
    mix = _mm(ycat, w_out, tm=tm, tn=1024, tk=2048, name="mm_mix")[0]
    h1, xn2 = _mid_fwd_call(h0, mix, wt["mix_post_g"], wt["ffn_pre_g"])
    gu = _mm(xn2, w_up, tm=tm, tn=1408, tk=D_MODEL, name="mm_up")[0]
    gpre, act = _conv_fwd_call(gu, 0, D_FF, fcw, fw8, wt["ffn_conv_b"], 3, gate_src=gu, gate_col0=D_FF,
                               rb=rbc, name="ffn_conv_fwd")
    f = _mm(act, w_down, tm=tm, tn=1024, tk=1408, name="mm_down")[0]
    loss_row, df, dh2, dg_ffn_post = _final_call(h1, f, wt["ffn_post_g"], target)

    dact = _mm(df, w_down, tb=True, tm=tm, tn=1408, tk=D_MODEL, name="mm_dact")[0]
    dw_down, dw_down_b = _mm(act, df, ta=True, tm=1408, tn=1024, tk=tkr, extra_bf16=True, name="mm_dw_down")
    by_chip = lambda g: g.reshape(N_CHIPS, -1, D_MODEL)
    dgate, dup, dfcw, dfcb = _conv_bwd_call(gu, 0, D_FF, fcw, fw8, 3, gpre, dact, gate_src=gu, gate_col0=D_FF,
                                            rb=rbc, name="ffn_conv_bwd")
    dgu = jnp.concatenate([dgate, dup], axis=1)
    dxn2, land_down = _mm(dgu, w_up, tb=True, tm=tm, tn=1024, tk=1408, ride=_scatter_ride(by_chip(dw_down_b)),
                          name="mm_dxn2")
    dw_up, dw_up_b = _mm(xn2, dgu, ta=True, tm=1024, tn=1408, tk=tkr, nsplit=N_CHIPS, extra_bf16=True,
                         name="mm_dw_up")
    dh1, dmix, dg_ffn_pre, dg_mix_post = _mid_bwd_call(dh2, h1, dxn2[0], mix, wt["ffn_pre_g"], wt["mix_post_g"])
    dycat = _mm(dmix, w_out, tb=True, tm=tm, tn=1024, tk=D_MODEL, name="mm_dycat")[0]
    dw_out, dw_out_b = _mm(ycat, dmix, ta=True, tm=1024, tn=1024, tk=tkr, extra_bf16=True, name="mm_dw_out")
    do, dg_sb = _norm_bwd_call(o, wt["sb_norm_g"], dycat, 1, "sb_norm_bwd")
    dq, dk, dv, lands = _sb_bwd_call(proj, tl, do, _join_rides(_scatter_ride(dw_up_b),
                                                                  _scatter_ride(by_chip(dw_out_b))))
    dz, dxbc_act, ddt, dg_ssd, dd_x, dalog, dbias = _ssd_bwd_call(
        dycat, ypre, xbc, proj, sprev, bias, alog, d_x, wt["ssd_norm_g"])
    dxbc, dcw, dcb = _conv_bwd_call(proj, C_XBC, XBC, 512, cw8, 4, conv_pre, dxbc_act, rb=rbc,
                                    name="ssd_conv_bwd")
    dproj = jnp.concatenate([dz, dxbc, ddt, dq, dk, dv], axis=1)
    dw_in, dw_in_b = w_in_shards(_mm(xn1, dproj, ta=True, tm=1024, tn=1152, tk=tkr, name="mm_dw_in")[0])
    half = dw_in_b.shape[1] // 2
    part = half * 5 // 8
    dxn1, land_in = _mm(dproj, wt["w_in"], tb=True, tm=tm, tn=1024, tk=1152, ride=_scatter_ride(dw_in_b, 0, part),
                        name="mm_dxn1")
    dh0, dg_pre, land_in = _norm_bwd_call(h0, wt["mix_pre_g"], dxn1[0], 0, "norm1_bwd", res=dh1,
                                          ride=_scatter_ride(dw_in_b, part, half - part), land=land_in[0])

    small = {
        "meta_tokens": dh0[PAD:BLK], "mix_pre_g": dg_pre, "ssd_conv_w": dcw[:4], "ssd_conv_b": dcb,
        "ssd_dt_bias": dbias[:, :N_HEADS], "ssd_a_log": dalog[:, :N_HEADS],
        "ssd_d": jnp.sum(dd_x.reshape(N_HEADS, HEAD_DIM), axis=1)[None],
        "ssd_norm_g": dg_ssd, "sb_norm_g": dg_sb, "mix_post_g": dg_mix_post, "ffn_pre_g": dg_ffn_pre,
        "ffn_conv_w": dfcw[:3], "ffn_conv_b": dfcb, "ffn_post_g": dg_ffn_post,
    }
    pending = {"w_in": (dw_in, land_in), "w_out": (by_chip(dw_out), lands[1]), "w_up": (dw_up, lands[0]),
               "w_down": (by_chip(dw_down), land_down[0])}
    return loss_row, dh0[BLK:], small, pending


def _adamw_call(w, g, m, v, name):
    rows, cols = w.shape
    tr = 256 if rows % 256 == 0 else (352 if rows % 352 == 0 else rows)
    c1 = 1.0 - ADAM_B1 ** ADAM_STEP
    c2 = 1.0 - ADAM_B2 ** ADAM_STEP

    def body(w_ref, g_ref, m_ref, v_ref, d_ref, mo_ref, vo_ref):
        gv = g_ref[...]
        m2 = ADAM_B1 * m_ref[...] + (1.0 - ADAM_B1) * gv
        v2 = ADAM_B2 * v_ref[...] + (1.0 - ADAM_B2) * (gv * gv)
        d_ref[...] = -ADAM_LR * ((m2 / c1) / (jnp.sqrt(v2 / c2) + ADAM_EPS) + ADAM_WD * w_ref[...])
        mo_ref[...] = m2
        vo_ref[...] = v2

    spec = pl.BlockSpec((tr, cols), lambda i: (i, 0))
    return pl.pallas_call(
        body, name=name, grid=(rows // tr,), in_specs=[spec] * 4, out_specs=[spec] * 3,
        out_shape=[jax.ShapeDtypeStruct((rows, cols), F32)] * 3,
        compiler_params=_cparams(("parallel",)))(w, g, m, v)


ANY = pl.BlockSpec(memory_space=pl.ANY)


def _place():
    x, y, c = lax.axis_index("x"), lax.axis_index("y"), lax.axis_index("c")
    chips = [(1 - x, y), (x, 1 - y), (1 - x, 1 - y)]
    return x, y, c, chips


def _half(c, h):
    return pl.ds(pl.multiple_of(c * h, 8), h)


def _allgather_call(shards):
    n = len(shards)

    def body(*refs):
        ins, outs = refs[:n], refs[n:2 * n]
        send_i, recv_i, send_d, recv_d = refs[2 * n:]
        x, y, c, chips = _place()
        me = 2 * x + y
        sends = []
        for a in range(n):
            h = shards[a].shape[0] // 2
            for j, chip in enumerate(chips):
                cp = pltpu.make_async_remote_copy(
                    src_ref=ins[a].at[_half(c, h)], dst_ref=outs[a].at[me, _half(c, h)],
                    send_sem=send_i.at[3 * a + j], recv_sem=recv_i.at[3 * a + j],
                    device_id=(*chip, c), device_id_type=MESH)
                cp.start()
                sends.append(cp)
        for a in range(n):
            h = shards[a].shape[0] // 2
            for j, chip in enumerate(chips):
                src = 2 * chip[0] + chip[1]
                landed = outs[a].at[src, _half(c, h)]
                pltpu.make_async_remote_copy(
                    src_ref=landed, dst_ref=landed, send_sem=send_i.at[3 * a + j], recv_sem=recv_i.at[3 * a + j],
                    device_id=(*chip, c), device_id_type=MESH).wait_recv()
                cp = pltpu.make_async_remote_copy(
                    src_ref=landed, dst_ref=landed, send_sem=send_d.at[3 * a + j], recv_sem=recv_d.at[3 * a + j],
                    device_id=(x, y, 1 - c), device_id_type=MESH)
                cp.start()
                sends.append(cp)
        for a in range(n):
            h = shards[a].shape[0] // 2
            for j, chip in enumerate(chips):
                src = 2 * chip[0] + chip[1]
                other = outs[a].at[src, _half(1 - c, h)]
                pltpu.make_async_remote_copy(
                    src_ref=other, dst_ref=other, send_sem=send_d.at[3 * a + j], recv_sem=recv_d.at[3 * a + j],
                    device_id=(x, y, 1 - c), device_id_type=MESH).wait_recv()
        for cp in sends:
            cp.wait_send()

    return pl.pallas_call(
        body, name="allgather_weights", in_specs=[ANY] * n, out_specs=[ANY] * n,
        out_shape=[jax.ShapeDtypeStruct((N_CHIPS,) + s.shape, s.dtype) for s in shards],
        scratch_shapes=[pltpu.SemaphoreType.DMA((3 * n,))] * 4,
    )(*shards)


def _ride_scratch(ride):
    return [pltpu.SemaphoreType.DMA((ride[3],)), pltpu.SemaphoreType.DMA((ride[3],))]


def _ride_run(ride, src_refs, land_refs, send, recv, first, last):
    plan = ride[2]

    @pl.when(first)
    def _():
        for k, (src, dst, _, dev) in enumerate(plan(src_refs, land_refs)):
            pltpu.make_async_remote_copy(src_ref=src, dst_ref=dst, send_sem=send.at[k], recv_sem=recv.at[k],
                                         device_id=dev, device_id_type=MESH).start()

    @pl.when(last)
    def _():
        for k, (src, _, land, dev) in enumerate(plan(src_refs, land_refs)):
            cp = pltpu.make_async_remote_copy(src_ref=src, dst_ref=land, send_sem=send.at[k], recv_sem=recv.at[k],
                                              device_id=dev, device_id_type=MESH)
            cp.wait_send()
            cp.wait_recv()


def _join_rides(r1, r2):
    n1, l1 = len(r1[0]), len(r1[1])

    def plan(srcs, lands):
        return r1[2](srcs[:n1], lands[:l1]) + r2[2](srcs[n1:], lands[l1:])

    return (r1[0] + r2[0], r1[1] + r2[1], plan, r1[3] + r2[3])


def _gather_ride(shards):
    return (list(shards), [jax.ShapeDtypeStruct((N_CHIPS,) + s.shape, s.dtype) for s in shards],
            _gather_plan(len(shards)), 3 * len(shards))


def _scatter_ride(g_b, r0=0, nr=None):
    h = g_b.shape[1] // 2
    return ([g_b], [jax.ShapeDtypeStruct((8, h, g_b.shape[2]), BF16)], _scatter_plan(h, r0, h if nr is None else nr), 7)


def _gather_plan(n):
    def plan(srcs, lands):
        x, y, c, chips = _place()
        me = 2 * x + y
        return [(srcs[a], lands[a].at[me], lands[a].at[2 * chip[0] + chip[1]], (*chip, c))
                for a in range(n) for chip in chips]
    return plan


def _scatter_plan(h, r0, nr):
    def plan(srcs, lands):
        x, y, c, _ = _place()
        me = 4 * x + 2 * y + c
        out = []
        for p in range(1, 8):
            px = 1 - x if p & 4 else x
            py = 1 - y if p & 2 else y
            pc = 1 - c if p & 1 else c
            rows = pl.ds(pl.multiple_of(pc * h + r0, 8), nr)
            out.append((srcs[0].at[2 * px + py, rows], lands[0].at[me, pl.ds(r0, nr)],
                        lands[0].at[4 * px + 2 * py + pc, pl.ds(r0, nr)], (px, py, pc)))
        return out
    return plan


def _grad_sum_call(own, land, place, name):
    _, h, cols = land.shape
    th = _pick(h, [256, 176, 8])
    nt = h // th

    def body(p_ref, own_ref, *refs):
        acc = own_ref[...]
        for r in refs[:7]:
            acc = acc + r[...].astype(F32)
        refs[7][...] = acc

    def peer(k):
        return pl.BlockSpec((None, th, cols), lambda i, p_ref: (p_ref[2 + k], i, 0))

    return pl.pallas_call(
        body, name=name,
        grid_spec=pltpu.PrefetchScalarGridSpec(
            num_scalar_prefetch=1, grid=(nt,),
            in_specs=[pl.BlockSpec((None, th, cols), lambda i, p_ref: (p_ref[1], p_ref[0] * nt + i, 0))]
            + [peer(k) for k in range(7)],
            out_specs=pl.BlockSpec((th, cols), lambda i, p_ref: (p_ref[0] * nt + i, 0))),
        out_shape=jax.ShapeDtypeStruct((2 * h, cols), F32),
        compiler_params=_cparams(("parallel",)))(place, own, *[land] * 7)


def _half_exchange_call(shards):
    n = len(shards)

    def body(*refs):
        outs = refs[n:2 * n]
        send_d, recv_d = refs[2 * n:]
        x, y, c, _ = _place()
        cps = []
        for a in range(n):
            h = shards[a].shape[0] // 2
            mine = outs[a].at[_half(c, h)]
            cp = pltpu.make_async_remote_copy(
                src_ref=mine, dst_ref=mine, send_sem=send_d.at[a], recv_sem=recv_d.at[a],
                device_id=(x, y, 1 - c), device_id_type=MESH)
            cp.start()
            cps.append(cp)
        for a, cp in enumerate(cps):
            h = shards[a].shape[0] // 2
            theirs = outs[a].at[_half(1 - c, h)]
            pltpu.make_async_remote_copy(
                src_ref=theirs, dst_ref=theirs, send_sem=send_d.at[a], recv_sem=recv_d.at[a],
                device_id=(x, y, 1 - c), device_id_type=MESH).wait_recv()
            cp.wait_send()

    return pl.pallas_call(
        body, name="grad_half_exchange", in_specs=[ANY] * n, out_specs=[ANY] * n,
        out_shape=[jax.ShapeDtypeStruct(sv.shape, F32) for sv in shards],
        input_output_aliases={a: a for a in range(n)},
        scratch_shapes=[pltpu.SemaphoreType.DMA((n,))] * 2,
    )(*shards)


def _allreduce_small_call(arrs):
    n = len(arrs)
    offs, rows = [], 0
    for a in arrs:
        offs.append(rows)
        rows += a.shape[0]
    rows = -(-rows // 8) * 8
    width = -(-max(a.shape[1] for a in arrs) // BLK) * BLK

    def body(*refs):
        ins, outs = refs[:n], refs[n:2 * n]
        gath, send_sems, recv_sems = refs[2 * n:]
        x, y, c, chips = _place()
        me, sibling = (x, y, c), (x, y, 1 - c)

        def slot(px, py, pc):
            return gath.at[4 * px + 2 * py + pc]

        def copy(k, block, to):
            return pltpu.make_async_remote_copy(
                src_ref=slot(*block), dst_ref=slot(*block),
                send_sem=send_sems.at[k], recv_sem=recv_sems.at[k], device_id=to, device_id_type=MESH)

        mine = slot(*me)
        mine[...] = jnp.zeros((rows, width), F32)
        for k in range(n):
            r, w = arrs[k].shape
            mine[offs[k]:offs[k] + r, 0:w] = ins[k][...]
        first = [copy(0, me, sibling)]
        first += [copy(1 + j, me, (*chip, c)) for j, chip in enumerate(chips)]
        for cp in first:
            cp.start()
        passed = [copy(4 + j, (*chip, c), sibling) for j, chip in enumerate(chips)]
        for j, chip in enumerate(chips):
            copy(1 + j, (*chip, c), me).wait_recv()
            passed[j].start()
        copy(0, sibling, me).wait_recv()
        for j, chip in enumerate(chips):
            copy(4 + j, (*chip, 1 - c), me).wait_recv()
        for cp in first + passed:
            cp.wait_send()
        acc = gath[0]
        for d in range(1, 8):
            acc = acc + gath[d]
        for k in range(n):
            r, w = arrs[k].shape
            outs[k][...] = acc[offs[k]:offs[k] + r, 0:w]

    vm = pl.BlockSpec(memory_space=pltpu.VMEM)
    return pl.pallas_call(
        body, name="allreduce_small", in_specs=[vm] * n, out_specs=[vm] * n,
        out_shape=[jax.ShapeDtypeStruct(a.shape, F32) for a in arrs],
        scratch_shapes=[pltpu.VMEM((8, rows, width), F32), pltpu.SemaphoreType.DMA((7,)),
                        pltpu.SemaphoreType.DMA((7,))],
        compiler_params=pltpu.CompilerParams(vmem_limit_bytes=VMEM_LIMIT),
    )(*arrs)


def _adamw_small_call(ws, gs, ms, vs):
    n = len(ws)
    c1 = 1.0 - ADAM_B1 ** ADAM_STEP
    c2 = 1.0 - ADAM_B2 ** ADAM_STEP

    def body(*refs):
        for k in range(n):
            w_ref, g_ref, m_ref, v_ref = (refs[j * n + k] for j in range(4))
            d_ref, mo_ref, vo_ref = (refs[(4 + j) * n + k] for j in range(3))
            gv = g_ref[...]
            m2 = ADAM_B1 * m_ref[...] + (1.0 - ADAM_B1) * gv
            v2 = ADAM_B2 * v_ref[...] + (1.0 - ADAM_B2) * (gv * gv)
            d_ref[...] = -ADAM_LR * ((m2 / c1) / (jnp.sqrt(v2 / c2) + ADAM_EPS) + ADAM_WD * w_ref[...])
            mo_ref[...] = m2
            vo_ref[...] = v2

    vm = pl.BlockSpec(memory_space=pltpu.VMEM)
    res = pl.pallas_call(
        body, name="adamw_small", in_specs=[vm] * (4 * n), out_specs=[vm] * (3 * n),
        out_shape=[jax.ShapeDtypeStruct(a.shape, F32) for a in ws] * 3,
        compiler_params=pltpu.CompilerParams(vmem_limit_bytes=VMEM_LIMIT),
    )(*ws, *gs, *ms, *vs)
    return res[:n], res[n:2 * n], res[2 * n:]


def _pack(arrs, min_rows=8):
    parts = []
    for a in arrs:
        flat = a.reshape(-1).astype(F32)
        parts.append(jnp.pad(flat, (0, (-flat.shape[0]) % BLK)))
    buf = jnp.concatenate(parts).reshape(-1, BLK)
    return jnp.pad(buf, ((0, (-buf.shape[0]) % min_rows), (0, 0)))


def _unpack(buf, shapes):
    out, r = [], 0
    for shp in shapes:
        n = math.prod(shp)
        nr = -(-n // BLK)
        out.append(buf[r:r + nr].reshape(-1)[:n].reshape(shp))
        r += nr
    return out


SMALL = ["meta_tokens", "mix_pre_g", "ssd_conv_w", "ssd_conv_b", "ssd_dt_bias", "ssd_a_log", "ssd_d", "ssd_norm_g",
         "sb_norm_g", "mix_post_g", "ffn_pre_g", "ffn_conv_w", "ffn_conv_b", "ffn_post_g"]
BIG = ["w_in", "w_out", "w_up", "w_down"]
WEIGHTS = ["meta_tokens", "mix_pre_g", "w_in", "ssd_conv_w", "ssd_conv_b", "ssd_dt_bias", "ssd_a_log", "ssd_d",
           "ssd_norm_g", "sb_norm_g", "w_out", "mix_post_g", "ffn_pre_g", "w_up", "ffn_conv_w", "ffn_conv_b",
           "w_down", "ffn_post_g"]
W_IN_SHARD = IN_COLS // N_CHIPS
W_IN_PAD = 1536


def kernel(x, meta_tokens, mix_pre_g, w_in, ssd_conv_w, ssd_conv_b, ssd_dt_bias, ssd_a_log, ssd_d, ssd_norm_g, sb_norm_g, w_out, mix_post_g, ffn_pre_g, w_up, ffn_conv_w, ffn_conv_b, w_down, ffn_post_g, loss_target, m_meta_tokens, m_mix_pre_g, m_w_in, m_ssd_conv_w, m_ssd_conv_b, m_ssd_dt_bias, m_ssd_a_log, m_ssd_d, m_ssd_norm_g, m_sb_norm_g, m_w_out, m_mix_post_g, m_ffn_pre_g, m_w_up, m_ffn_conv_w, m_ffn_conv_b, m_w_down, m_ffn_post_g, v_meta_tokens, v_mix_pre_g, v_w_in, v_ssd_conv_w, v_ssd_conv_b, v_ssd_dt_bias, v_ssd_a_log, v_ssd_d, v_ssd_norm_g, v_sb_norm_g, v_w_out, v_mix_post_g, v_ffn_pre_g, v_w_up, v_ffn_conv_w, v_ffn_conv_b, v_w_down, v_ffn_post_g):
    w = dict(meta_tokens=meta_tokens, mix_pre_g=mix_pre_g, w_in=w_in, ssd_conv_w=ssd_conv_w, ssd_conv_b=ssd_conv_b, ssd_dt_bias=ssd_dt_bias, ssd_a_log=ssd_a_log, ssd_d=ssd_d, ssd_norm_g=ssd_norm_g, sb_norm_g=sb_norm_g, w_out=w_out, mix_post_g=mix_post_g, ffn_pre_g=ffn_pre_g, w_up=w_up, ffn_conv_w=ffn_conv_w, ffn_conv_b=ffn_conv_b, w_down=w_down, ffn_post_g=ffn_post_g)
    m = dict(meta_tokens=m_meta_tokens, mix_pre_g=m_mix_pre_g, w_in=m_w_in, ssd_conv_w=m_ssd_conv_w, ssd_conv_b=m_ssd_conv_b, ssd_dt_bias=m_ssd_dt_bias, ssd_a_log=m_ssd_a_log, ssd_d=m_ssd_d, ssd_norm_g=m_ssd_norm_g, sb_norm_g=m_sb_norm_g, w_out=m_w_out, mix_post_g=m_mix_post_g, ffn_pre_g=m_ffn_pre_g, w_up=m_w_up, ffn_conv_w=m_ffn_conv_w, ffn_conv_b=m_ffn_conv_b, w_down=m_w_down, ffn_post_g=m_ffn_post_g)
    v = dict(meta_tokens=v_meta_tokens, mix_pre_g=v_mix_pre_g, w_in=v_w_in, ssd_conv_w=v_ssd_conv_w, ssd_conv_b=v_ssd_conv_b, ssd_dt_bias=v_ssd_dt_bias, ssd_a_log=v_ssd_a_log, ssd_d=v_ssd_d, ssd_norm_g=v_ssd_norm_g, sb_norm_g=v_sb_norm_g, w_out=v_w_out, mix_post_g=v_mix_post_g, ffn_pre_g=v_ffn_pre_g, w_up=v_w_up, ffn_conv_w=v_ffn_conv_w, ffn_conv_b=v_ffn_conv_b, w_down=v_w_down, ffn_post_g=v_ffn_post_g)
    chip = 2 * lax.axis_index("x") + lax.axis_index("y")
    me = 2 * chip + lax.axis_index("c")
    place = jnp.stack([lax.axis_index("c"), chip] + [me ^ p for p in range(1, 8)]).astype(jnp.int32)

    shard_small = [w["meta_tokens"], w["ssd_conv_w"][0], w["ffn_conv_w"][0]]
    shards = [jnp.pad(w["w_in"][0], ((0, 0), (0, W_IN_PAD - W_IN_SHARD))).astype(BF16), _pack(shard_small, 16)]
    gathered = _allgather_call(shards)
    late_shards = [w["w_out"][0].astype(BF16), w["w_up"][0].astype(BF16), w["w_down"][0].astype(BF16)]

    def blocks(own, got):
        return [jnp.where(chip == i, own, got[i]) for i in range(N_CHIPS)]

    def late_weights(got):
        return (jnp.concatenate(blocks(late_shards[0], got[0]), axis=0),
                jnp.concatenate(blocks(late_shards[1], got[1]), axis=1),
                jnp.concatenate(blocks(late_shards[2], got[2]), axis=0))

    cut = DT_REAL_OFF + N_HEADS - W_IN_SHARD
    s_in = blocks(shards[0], gathered[0])
    w_in_c = jnp.concatenate(
        [s_in[0][:, :W_IN_SHARD], s_in[1][:, :cut], jnp.zeros((D_MODEL, BLK - N_HEADS), BF16),
         s_in[1][:, cut:W_IN_SHARD], s_in[2][:, :W_IN_SHARD], s_in[3][:, :W_IN_SHARD]], axis=1)
    parts = [_unpack(b, [s.shape for s in shard_small]) for b in blocks(shards[1], gathered[1])]
    wt = {k: w[k][0][None] if w[k].ndim == 3 else w[k] for k in
          ["mix_pre_g", "ssd_conv_b", "ssd_dt_bias", "ssd_a_log", "ssd_d", "ssd_norm_g", "sb_norm_g", "mix_post_g",
           "ffn_pre_g", "ffn_conv_b", "ffn_post_g"]}
    wt.update(
        meta=jnp.concatenate([p[0] for p in parts], axis=1),
        ssd_conv_w=jnp.concatenate([p[1] for p in parts], axis=1),
        ffn_conv_w=jnp.concatenate([p[2] for p in parts], axis=1), w_in=w_in_c)

    def w_in_shards(g):
        skip = BLK - N_HEADS
        cols = [g[:, :W_IN_SHARD],
                jnp.concatenate([g[:, W_IN_SHARD:W_IN_SHARD + cut], g[:, C_Q:2 * W_IN_SHARD + skip]], axis=1),
                g[:, 2 * W_IN_SHARD + skip:3 * W_IN_SHARD + skip], g[:, 3 * W_IN_SHARD + skip:]]
        g = jnp.stack([jnp.pad(b, ((0, 0), (0, W_IN_PAD - W_IN_SHARD))) for b in cols])
        return g, g.astype(BF16)

    loss_row, dx, small, pending = _local_step(x[0], loss_target[0], wt, late_shards, late_weights, w_in_shards)

    full = _half_exchange_call([_grad_sum_call(*pending[k], place, "grad_sum_" + k) for k in BIG])
    grads = {"w_in": full[0][:, :W_IN_SHARD], "w_out": full[1], "w_up": full[2], "w_down": full[3]}

    red_list = _allreduce_small_call([small[k] for k in SMALL] + [loss_row])
    loss = jnp.sum(red_list[-1])
    for k, g in zip(SMALL, red_list[:-1]):
        grads[k] = g
    for k in ["meta_tokens", "ssd_conv_w", "ffn_conv_w"]:
        wk = w[k].shape[-1]
        grads[k] = lax.dynamic_slice_in_dim(grads[k], chip * wk, wk, axis=1)

    delta, new_m, new_v = {}, {}, {}
    for k in BIG:
        delta[k], new_m[k], new_v[k] = _adamw_call(w[k][0], grads[k], m[k][0], v[k][0], "adamw_" + k)
    flat = lambda d: [d[k].reshape(grads[k].shape) for k in SMALL]
    res = _adamw_small_call(flat(w), [grads[k] for k in SMALL], flat(m), flat(v))
    for out, arrs in zip((delta, new_m, new_v), res):
        for k, a in zip(SMALL, arrs):
            out[k] = a

    def shaped(d, k):
        return d[k].reshape(w[k].shape)

    return (loss, dx[None], *[shaped(grads, k) for k in WEIGHTS], *[shaped(delta, k) for k in WEIGHTS],
            *[shaped(new_m, k) for k in WEIGHTS], *[shaped(new_v, k) for k in WEIGHTS])
```

```python
import functools
import math

import jax
import jax.numpy as jnp
from jax import lax
from jax.experimental import pallas as pl
from jax.experimental.pallas import tpu as pltpu

F32 = jnp.float32
BF16 = jnp.bfloat16

D_MODEL = 1024
N_META = 16
BLK = 128
PAD = BLK - N_META
HEAD_DIM = 64
N_HEADS = 16
SSD_GROUPS = 2
SSD_STATE = 128
SSD_INNER = 1024
XBC = SSD_INNER + 2 * SSD_GROUPS * SSD_STATE
D_FF = 2816
EPS = 1e-6
IN_COLS = 5648
C_Z, C_XBC, C_DT, C_Q, C_K, C_V, C_END = 0, 1024, 2560, 2688, 3712, 4736, 5760
DT_REAL_OFF = 2560
N_CHIPS = 4
ADAM_LR, ADAM_B1, ADAM_B2, ADAM_EPS, ADAM_WD, ADAM_STEP = 0.001, 0.9, 0.999, 1e-08, 0.01, 10
VMEM_LIMIT = 56 * 1024 * 1024
MESH = pl.DeviceIdType.MESH


def _cparams(sem=None, **kw):
    if sem is not None:
        kw["dimension_semantics"] = sem
    return pltpu.CompilerParams(vmem_limit_bytes=VMEM_LIMIT, **kw)


def _pick(n, cands):
    for c in cands:
        if n % c == 0:
            return c
    raise ValueError((n, cands))


def _iota(shape, dim):
    return lax.broadcasted_iota(jnp.int32, shape, dim)


def _sigmoid(x):
    return 1.0 / (1.0 + jnp.exp(-x))


def _split2(v):
    h1 = v.astype(BF16)
    return h1, (v - h1.astype(F32)).astype(BF16)


def _dot(a, b, ca=1, cb=0):
    return lax.dot_general(a, b, (((ca,), (cb,)), ((), ())), preferred_element_type=F32)


def _dot_sel_r(v, sel, cb=0):
    h1, h2 = _split2(v)
    return _dot(h1, sel, 1, cb) + _dot(h2, sel, 1, cb)


def _dot_sel_l(sel, v, ca=1):
    h1, h2 = _split2(v)
    return _dot(sel, h1, ca, 0) + _dot(sel, h2, ca, 0)


def _mm(a, b, *, ta=False, tb=False, tm, tn, tk, out_dtype=F32, nsplit=1, extra_bf16=False, ride=None, name):
    K, M = (a.shape if ta else a.shape[::-1])
    N = b.shape[0] if tb else b.shape[1]
    assert M % tm == 0 and N % tn == 0 and K % tk == 0, (name, M, N, K, tm, tn, tk)
    nm, nn, nk = M // tm, N // tn, K // tk
    assert nn % nsplit == 0
    per = nn // nsplit
    a_spec = (pl.BlockSpec((tk, tm), lambda i, j, k: (k, i)) if ta
              else pl.BlockSpec((tm, tk), lambda i, j, k: (i, k)))
    b_spec = (pl.BlockSpec((tn, tk), lambda i, j, k: (j, k)) if tb
              else pl.BlockSpec((tk, tn), lambda i, j, k: (k, j)))
    o_spec = pl.BlockSpec((None, tm, tn), lambda i, j, k: (j // per, i, j % per))
    n_out = 2 if extra_bf16 else 1
    ca, cb = (0 if ta else 1), (1 if tb else 0)
    ns, nl = (len(ride[0]), len(ride[1])) if ride else (0, 0)

    def body(a_ref, b_ref, *rest):
        outs = rest[ns:ns + n_out]
        if ride:
            step = (pl.program_id(0) * nn + pl.program_id(1)) * nk + pl.program_id(2)
            _ride_run(ride, rest[:ns], rest[ns + n_out:ns + n_out + nl], rest[-2], rest[-1],
                      step == 0, step == nm * nn * nk - 1)
        p = _dot(a_ref[...].astype(BF16), b_ref[...].astype(BF16), ca, cb)

        def emit(val):
            outs[0][...] = val.astype(out_dtype)
            if extra_bf16:
                outs[1][...] = val.astype(BF16)

        if nk == 1:
            emit(p)
        else:
            acc = rest[ns + n_out + nl]
            k = pl.program_id(2)

            @pl.when(k == 0)
            def _():
                acc[...] = p

            @pl.when(k > 0)
            def _():
                acc[...] += p

            @pl.when(k == nk - 1)
            def _():
                emit(acc[...])

    shp = (nsplit, M, N // nsplit)
    out_shape = [jax.ShapeDtypeStruct(shp, out_dtype)]
    out_specs = [o_spec]
    if extra_bf16:
        out_shape.append(jax.ShapeDtypeStruct(shp, BF16))
        out_specs.append(o_spec)
    scratch = [pltpu.VMEM((tm, tn), F32)] if nk > 1 else []
    if ride:
        res = pl.pallas_call(
            body, name=name, grid=(nm, nn, nk), in_specs=[a_spec, b_spec] + [ANY] * ns,
            out_specs=out_specs + [ANY] * nl, out_shape=out_shape + list(ride[1]),
            scratch_shapes=scratch + _ride_scratch(ride),
            compiler_params=_cparams(("arbitrary", "arbitrary", "arbitrary")),
        )(a, b, *ride[0])
        return (res[:n_out] if extra_bf16 else res[0]), list(res[n_out:])
    res = pl.pallas_call(
        body, name=name, grid=(nm, nn, nk), in_specs=[a_spec, b_spec], out_specs=out_specs,
        out_shape=out_shape, scratch_shapes=scratch,
        compiler_params=_cparams(("parallel", "parallel", "arbitrary")),
    )(a, b)
    return res if extra_bf16 else res[0]


def _rms_stats(x):
    r = lax.rsqrt(jnp.mean(x * x, axis=-1, keepdims=True) + EPS)
    return r, x * r


def _rms_bwd(x, g, dy):
    r, xh = _rms_stats(x)
    dxh = dy * g
    dx = r * (dxh - xh * jnp.mean(dxh * xh, axis=-1, keepdims=True))
    return dx, jnp.sum(dy * xh, axis=0, keepdims=True)


def _row_spec(tr, w, col=0):
    return pl.BlockSpec((tr, w), lambda i: (i, col))


def _vec_spec(w):
    return pl.BlockSpec((1, w), lambda i: (0, 0))


def _acc_rows(ref, val, i):
    @pl.when(i == 0)
    def _():
        ref[...] = val

    @pl.when(i > 0)
    def _():
        ref[...] += val


def _rms_fwd_call(x, g, name, beside=None):
    lp, w = x.shape
    tr = _pick(lp, [384, 128])

    def body(x_ref, g_ref, *rest):
        _, xh = _rms_stats(x_ref[...])
        rest[-1][...] = (xh * g_ref[...]).astype(BF16)

    if beside is None:
        return pl.pallas_call(
            body, name=name, grid=(lp // tr,), in_specs=[_row_spec(tr, w), _vec_spec(w)],
            out_specs=_row_spec(tr, w), out_shape=jax.ShapeDtypeStruct((lp, w), BF16),
            compiler_params=_cparams(("parallel",)))(x, g)
    return pl.pallas_call(
        body, name=name, grid=(lp // tr,), in_specs=[_row_spec(tr, w), _vec_spec(w), ANY],
        out_specs=_row_spec(tr, w, 1), out_shape=jax.ShapeDtypeStruct((lp, 2 * w), BF16),
        input_output_aliases={2: 0}, compiler_params=_cparams(("parallel",)))(x, g, beside)


def _mid_fwd_call(h0, mix, g_post, g_pre2):
    lp, w = h0.shape
    tr = _pick(lp, [384, 128])

    def body(h0_ref, mix_ref, gp_ref, g2_ref, h1_ref, xn_ref):
        _, mh = _rms_stats(mix_ref[...])
        h1 = h0_ref[...] + mh * gp_ref[...]
        h1_ref[...] = h1
        _, hh = _rms_stats(h1)
        xn_ref[...] = (hh * g2_ref[...]).astype(BF16)

    return pl.pallas_call(
        body, name="mid_fwd", grid=(lp // tr,),
        in_specs=[_row_spec(tr, w), _row_spec(tr, w), _vec_spec(w), _vec_spec(w)],
        out_specs=[_row_spec(tr, w), _row_spec(tr, w)],
        out_shape=[jax.ShapeDtypeStruct((lp, w), F32), jax.ShapeDtypeStruct((lp, w), BF16)],
        compiler_params=_cparams(("parallel",)))(h0, mix, g_post, g_pre2)


def _final_call(h1, f, g_post, target):
    lp, w = h1.shape
    tr = BLK
    nb = lp // tr

    def body(h1_ref, f_ref, g_ref, t_ref, loss_ref, df_ref, dh_ref, dg_ref):
        i = pl.program_id(0)
        fv = f_ref[...]
        g = g_ref[...]
        _, fh = _rms_stats(fv)
        h2 = h1_ref[...] + fh * g
        diff = jnp.where(i > 0, h2 - t_ref[...], 0.0)
        part = 0.5 * jnp.sum(diff * diff, axis=0, keepdims=True) * (1.0 / w)
        _acc_rows(loss_ref, part, i)
        dh = diff * (1.0 / w)
        dh_ref[...] = dh
        df, dg = _rms_bwd(fv, g, dh)
        df_ref[...] = df.astype(BF16)
        _acc_rows(dg_ref, dg, i)

    t_spec = pl.BlockSpec((tr, w), lambda i: (jnp.maximum(i - 1, 0), 0))
    return pl.pallas_call(
        body, name="final_fwd_bwd", grid=(nb,),
        in_specs=[_row_spec(tr, w), _row_spec(tr, w), _vec_spec(w), t_spec],
        out_specs=[_vec_spec(w), _row_spec(tr, w), _row_spec(tr, w), _vec_spec(w)],
        out_shape=[jax.ShapeDtypeStruct((1, w), F32), jax.ShapeDtypeStruct((lp, w), BF16),
                   jax.ShapeDtypeStruct((lp, w), F32), jax.ShapeDtypeStruct((1, w), F32)],
        compiler_params=_cparams(("arbitrary",)))(h1, f, g_post, target)


def _mid_bwd_call(dh2, h1, dxn2, mix, g_pre2, g_post):
    lp, w = h1.shape
    tr = _pick(lp, [384, 128])

    def body(dh2_ref, h1_ref, dxn_ref, mix_ref, g2_ref, gp_ref, dh1_ref, dmix_ref, dg2_ref, dgp_ref):
        i = pl.program_id(0)
        live = (i * tr + _iota((tr, 1), 0)) >= PAD
        dx, dg2 = _rms_bwd(h1_ref[...], g2_ref[...], dxn_ref[...])
        dh1 = jnp.where(live, dh2_ref[...] + dx, 0.0)
        dh1_ref[...] = dh1
        dmix, dgp = _rms_bwd(mix_ref[...], gp_ref[...], dh1)
        dmix_ref[...] = jnp.where(live, dmix, 0.0).astype(BF16)
        _acc_rows(dg2_ref, dg2, i)
        _acc_rows(dgp_ref, dgp, i)

    rs = _row_spec(tr, w)
    return pl.pallas_call(
        body, name="mid_bwd", grid=(lp // tr,),
        in_specs=[rs, rs, rs, rs, _vec_spec(w), _vec_spec(w)],
        out_specs=[rs, rs, _vec_spec(w), _vec_spec(w)],
        out_shape=[jax.ShapeDtypeStruct((lp, w), F32), jax.ShapeDtypeStruct((lp, w), BF16),
                   jax.ShapeDtypeStruct((1, w), F32), jax.ShapeDtypeStruct((1, w), F32)],
        compiler_params=_cparams(("arbitrary",)))(dh2, h1, dxn2, mix, g_pre2, g_post)


def _norm_bwd_call(x, g, dy_arr, dy_col, name, res=None, ride=None, land=None):
    lp, w = x.shape
    tr = _pick(lp, [384, 128])
    nsteps = lp // tr
    has_res = res is not None
    ns = len(ride[0]) if ride else 0
    first_out = (1 if has_res else 0) + (ns + 1 if ride else 0)

    def body(x_ref, g_ref, dy_ref, *rest):
        i = pl.program_id(0)
        if ride:
            srcs = rest[first_out - ns - 1:first_out - 1]
            _ride_run(ride, srcs, [rest[first_out + 2]], rest[-2], rest[-1], i == 0, i == nsteps - 1)
        live = (i * tr + _iota((tr, 1), 0)) >= PAD
        dx, dg = _rms_bwd(x_ref[...], g_ref[...], dy_ref[...])
        if has_res:
            dx = dx + rest[0][...]
        out_ref, dg_ref = rest[first_out], rest[first_out + 1]
        out_ref[...] = jnp.where(live, dx, 0.0)
        _acc_rows(dg_ref, dg, i)

    rs = _row_spec(tr, w)
    ins = [rs, _vec_spec(w), _row_spec(tr, w, dy_col)] + ([rs] if has_res else [])
    args = [x, g, dy_arr] + ([res] if has_res else [])
    outs = [rs, _vec_spec(w)]
    out_shape = [jax.ShapeDtypeStruct((lp, w), F32), jax.ShapeDtypeStruct((1, w), F32)]
    if not ride:
        return pl.pallas_call(
            body, name=name, grid=(nsteps,), in_specs=ins, out_specs=outs, out_shape=out_shape,
            compiler_params=_cparams(("arbitrary",)))(*args)
    return pl.pallas_call(
        body, name=name, grid=(nsteps,), in_specs=ins + [ANY] * (ns + 1), out_specs=outs + [ANY],
        out_shape=out_shape + [jax.ShapeDtypeStruct(land.shape, land.dtype)],
        input_output_aliases={len(args) + ns: 2}, scratch_shapes=_ride_scratch(ride),
        compiler_params=_cparams(("arbitrary",)))(*args, *ride[0], land)


def _shift_down(cur, prev_tail, s, rows):
    if s == 0:
        return cur
    prev = jnp.tile(prev_tail, (cur.shape[0] // 8, 1))
    return jnp.where(rows >= s, pltpu.roll(cur, s, 0), pltpu.roll(prev, s, 0))


def _shift_up(cur, next_head, s, rows):
    if s == 0:
        return cur
    n = cur.shape[0]
    nxt = jnp.tile(next_head, (n // 8, 1))
    return jnp.where(rows < n - s, pltpu.roll(cur, n - s, 0), pltpu.roll(nxt, n - s, 0))


def _gelu_tanh(x):
    c = math.sqrt(2.0 / math.pi)
    t = jnp.tanh(c * (x + 0.044715 * x * x * x))
    return 0.5 * x * (1.0 + t), t


def _conv_fwd_call(src, col0, width, cw, w8, b, taps, *, gate_src=None, gate_col0=0, rb=BLK, name):
    lp = src.shape[0]
    nb, nc = lp // rb, width // cw
    cb0 = col0 // cw
    ffn = gate_src is not None

    def body(x_ref, w_ref, b_ref, *rest):
        if ffn:
            u_ref, y_ref, a_ref, tail = rest
        else:
            y_ref, a_ref, tail = rest
        i = pl.program_id(1)

        @pl.when(i == 0)
        def _():
            tail[...] = jnp.zeros_like(tail)

        cur = x_ref[...]
        rows = _iota((rb, cw), 0)
        y = b_ref[...] + w_ref[taps - 1:taps, :] * cur
        pt = tail[...]
        for s in range(1, taps):
            y = y + w_ref[taps - 1 - s:taps - s, :] * _shift_down(cur, pt, s, rows)
        tail[...] = cur[rb - 8:, :]
        y_ref[...] = y
        if ffn:
            ge, _ = _gelu_tanh(y)
            a_ref[...] = (ge * u_ref[...]).astype(BF16)
        else:
            live = (i * rb + rows) >= PAD
            a_ref[...] = jnp.where(live, y * _sigmoid(y), 0.0)

    blk = lambda c0: pl.BlockSpec((rb, cw), lambda j, i: (i, c0 + j))
    ins = [blk(cb0), pl.BlockSpec((8, cw), lambda j, i: (0, j)), pl.BlockSpec((1, cw), lambda j, i: (0, j))]
    args = [src, w8, b]
    if ffn:
        ins.append(blk(gate_col0 // cw))
        args.append(gate_src)
    return pl.pallas_call(
        body, name=name, grid=(nc, nb), in_specs=ins, out_specs=[blk(0), blk(0)],
        out_shape=[jax.ShapeDtypeStruct((lp, width), F32),
                   jax.ShapeDtypeStruct((lp, width), BF16 if ffn else F32)],
        scratch_shapes=[pltpu.VMEM((8, cw), F32)],
        compiler_params=_cparams(("parallel", "arbitrary")))(*args)


def _conv_bwd_call(src, col0, width, cw, w8, taps, ypre, dact, *, gate_src=None, gate_col0=0, rb=BLK, name):
    lp = src.shape[0]
    nb, nc = lp // rb, width // cw
    cb0 = col0 // cw
    ffn = gate_src is not None

    def body(x_ref, w_ref, y_ref, d_ref, *rest):
        if ffn:
            u_ref, dx_ref, du_ref, dw_ref, db_ref, head = rest
        else:
            dx_ref, dw_ref, db_ref, head = rest
        step = pl.program_id(1)
        i = nb - 1 - step

        @pl.when(step == 0)
        def _():
            head[...] = jnp.zeros_like(head)

        rows = _iota((rb, cw), 0)
        live = (i * rb + rows) >= PAD
        y = y_ref[...]
        d = d_ref[...]
        if ffn:
            ge, t = _gelu_tanh(y)
            c = math.sqrt(2.0 / math.pi)
            dge = 0.5 * (1.0 + t) + 0.5 * y * (1.0 - t * t) * c * (1.0 + 3.0 * 0.044715 * y * y)
            u = u_ref[...]
            du_ref[...] = jnp.where(live, d * ge, 0.0).astype(BF16)
            dy = jnp.where(live, d * u * dge, 0.0)
        else:
            sg = _sigmoid(y)
            dy = jnp.where(live, d * sg * (1.0 + y * (1.0 - sg)), 0.0)
        x = x_ref[...]
        nh = head[...]
        dx = jnp.zeros_like(dy)
        dws = []
        for s in range(taps):
            sh = _shift_up(dy, nh, s, rows)
            dx = dx + w_ref[taps - 1 - s:taps - s, :] * sh
            dws.append(jnp.sum(x * sh, axis=0, keepdims=True))
        head[...] = dy[:8, :]
        dx_ref[...] = jnp.where(live, dx, 0.0).astype(BF16)
        dw = jnp.concatenate([dws[taps - 1 - k] for k in range(taps)]
                             + [jnp.zeros((8 - taps, cw), F32)], axis=0)
        _acc_rows(dw_ref, dw, step)
        _acc_rows(db_ref, jnp.sum(dy, axis=0, keepdims=True), step)

    blk = lambda c0: pl.BlockSpec((rb, cw), lambda j, s: (nb - 1 - s, c0 + j))
    ins = [blk(cb0), pl.BlockSpec((8, cw), lambda j, s: (0, j)), blk(0), blk(0)]
    args = [src, w8, ypre, dact]
    outs = [blk(0)]
    oshape = [jax.ShapeDtypeStruct((lp, width), BF16)]
    if ffn:
        ins.append(blk(gate_col0 // cw))
        args.append(gate_src)
        outs.append(blk(0))
        oshape.append(jax.ShapeDtypeStruct((lp, width), BF16))
    outs += [pl.BlockSpec((8, cw), lambda j, s: (0, j)), pl.BlockSpec((1, cw), lambda j, s: (0, j))]
    oshape += [jax.ShapeDtypeStruct((8, width), F32), jax.ShapeDtypeStruct((1, width), F32)]
    return pl.pallas_call(
        body, name=name, grid=(nc, nb), in_specs=ins, out_specs=outs, out_shape=oshape,
        scratch_shapes=[pltpu.VMEM((8, cw), F32)],
        compiler_params=_cparams(("parallel", "arbitrary")))(*args)


SB_FIRST = 3
SB_GROUP = 4
SB_DEAD = -110.0


def _sb_scores(qm_h, kb):
    z = _dot(qm_h, kb, 1, 1)
    sp = jnp.maximum(z, 0.0) + jnp.log(1.0 + jnp.exp(-jnp.abs(z)))
    return z - sp, -sp


def _dot_tri1(v, tri2):
    r = _dot(v.astype(BF16), tri2[:BLK])
    return r[:, :BLK], r[:, BLK:]


def _tri2(cond):
    t = jnp.concatenate([cond.astype(BF16), jnp.ones((BLK, BLK), BF16)], axis=1)
    return jnp.concatenate([t, t], axis=0)


def _dot_tri(v, tri2):
    hi = v.astype(BF16)
    lo = (v - hi.astype(F32)).astype(BF16)
    r = _dot(jnp.concatenate([hi, lo], axis=1), tri2)
    return r[:, :BLK], r[:, BLK:]


def _sb_fwd_call(proj, ride):
    lp = proj.shape[0]
    nb = lp // BLK
    scale = 1.0 / math.sqrt(HEAD_DIM)

    ns, nl = len(ride[0]), len(ride[1])
    npair = N_HEADS // 2

    def body(q_ref, k_ref, v_ref, *rest):
        o_ref, tl_ref = rest[ns], rest[ns + 1]
        i = pl.program_id(1)
        step = pl.program_id(0) * nb + i
        _ride_run(ride, rest[:ns], rest[ns + 2:ns + 2 + nl], rest[-2], rest[-1], step == 0, step == npair * nb - 1)
        lane = _iota((2 * BLK, BLK), 1)
        row = _iota((2 * BLK, BLK), 0)
        first = row < BLK
        qrow = row & (BLK - 1)
        q = q_ref[...] * scale
        q2 = jnp.concatenate([q, q], axis=0)
        qm = jnp.where(first == (lane < HEAD_DIM), q2, 0.0).astype(BF16)
        tri = _tri2(_iota((BLK, BLK), 0) > _iota((BLK, BLK), 1))

        def chunk(off, nsub, last_valid, carry):
            width = nsub * BLK
            sls = [slice(b * BLK, (b + 1) * BLK) for b in range(nsub)]
            kb = k_ref[pl.ds(off, width), :].astype(BF16)
            vb = v_ref[pl.ds(off, width), :].astype(BF16)
            lb, lk = _sb_scores(qm, kb)
            lks = [lk[:, sl] for sl in sls]
            first_valid = (off + lane) >= PAD
            lks[0] = jnp.where(first_valid, lks[0], 0.0)
            if last_valid is not None:
                lks[-1] = jnp.where(last_valid, lks[-1], 0.0)
            afters = [_dot_tri(lks[b], tri) for b in range(nsub)]
            run, acc = carry
            ws = [None] * nsub
            for b in reversed(range(nsub)):
                wb = jnp.exp(lb[:, sls[b]] + afters[b][0] + run)
                if b == 0:
                    wb = jnp.where(first_valid, wb, 0.0)
                if last_valid is not None and b == nsub - 1:
                    wb = jnp.where(last_valid, wb, 0.0)
                ws[b] = wb.astype(BF16)
                run = run + afters[b][1]
            w = ws[0] if nsub == 1 else jnp.concatenate(ws, axis=1)
            return run, acc + _dot(w, vb)

        before = jnp.minimum(i, SB_FIRST - 1)
        first_off = pl.multiple_of((i - before) * BLK, BLK)
        diag = lane < qrow
        zero = jnp.zeros((2 * BLK, BLK), F32)
        carry = lax.switch(before, [functools.partial(chunk, first_off, n, diag) for n in range(1, SB_FIRST + 1)],
                           (zero, zero))

        def walk(n):
            def body(state):
                off = pl.multiple_of((state[0] - (n - 1)) * BLK, BLK)
                return (state[0] - n, *chunk(off, n, None, state[1:]))

            def cond(state):
                return jnp.logical_and(state[0] >= n - 1, jnp.max(state[1]) > SB_DEAD)

            return cond, body

        state = lax.while_loop(*walk(SB_GROUP), (i - SB_FIRST, *carry))
        pos, run, acc = lax.while_loop(*walk(1), state)
        low = lane[:BLK] < HEAD_DIM
        o_ref[...] = jnp.where(low, acc[:BLK], acc[BLK:])
        tl = jnp.where(low, run[:BLK], run[BLK:])
        tl_ref[...] = jnp.where(lane[:BLK] == 1, jnp.maximum(pos + 1, 0).astype(F32), tl)

    qc, kc, vc = C_Q // BLK, C_K // BLK, C_V // BLK
    blk = pl.BlockSpec((BLK, BLK), lambda p, i: (i, p))
    res = pl.pallas_call(
        body, name="sb_fwd", grid=(npair, nb),
        in_specs=[pl.BlockSpec((BLK, BLK), lambda p, i: (i, qc + p)),
                  pl.BlockSpec((lp, BLK), lambda p, i: (0, kc + p)),
                  pl.BlockSpec((lp, BLK), lambda p, i: (0, vc + p))] + [ANY] * ns,
        out_specs=[blk, blk] + [ANY] * nl,
        out_shape=[jax.ShapeDtypeStruct((lp, N_HEADS * HEAD_DIM), F32)] * 2 + list(ride[1]),
        scratch_shapes=_ride_scratch(ride),
        compiler_params=_cparams(("arbitrary", "arbitrary")))(proj, proj, proj, *ride[0])
    return res[0], res[1], list(res[2:])


def _sb_bwd_call(proj, tl, do, ride):
    lp = proj.shape[0]
    nb = lp // BLK
    scale = 1.0 / math.sqrt(HEAD_DIM)

    ns, nl = len(ride[0]), len(ride[1])
    npair = N_HEADS // 2

    def body(q_ref, k_ref, v_ref, tl_ref, do_ref, *rest):
        dq_ref, dk_ref, dv_ref = rest[ns:ns + 3]
        dk_acc, dv_acc = rest[ns + 3 + nl:ns + 5 + nl]
        i = pl.program_id(1)
        step = pl.program_id(0) * nb + i
        _ride_run(ride, rest[:ns], rest[ns + 3:ns + 3 + nl], rest[-2], rest[-1], step == 0, step == npair * nb - 1)

        @pl.when(i == 0)
        def _():
            dk_acc[...] = jnp.zeros_like(dk_acc)
            dv_acc[...] = jnp.zeros_like(dv_acc)

        lane = _iota((2 * BLK, BLK), 1)
        row = _iota((2 * BLK, BLK), 0)
        qrow = row & (BLK - 1)
        mine = (row < BLK) == (lane < HEAD_DIM)
        q = q_ref[...] * scale
        dov = do_ref[...]
        qm = jnp.where(mine, jnp.concatenate([q, q], axis=0), 0.0).astype(BF16)
        dom = jnp.where(mine, jnp.concatenate([dov, dov], axis=0), 0.0).astype(BF16)
        tlv = tl_ref[...]
        tot = jnp.concatenate([jnp.broadcast_to(tlv[:, 0:1], (BLK, BLK)),
                               jnp.broadcast_to(tlv[:, HEAD_DIM:HEAD_DIM + 1], (BLK, BLK))], axis=0)
        r1, l1 = _iota((BLK, BLK), 0), _iota((BLK, BLK), 1)
        tri_in = _tri2(r1 <= l1)
        tri_ex = _tri2(r1 < l1)

        def chunk(off, nsub, last_valid, carry):
            width = nsub * BLK
            sls = [slice(b * BLK, (b + 1) * BLK) for b in range(nsub)]
            cat = lambda parts: parts[0] if nsub == 1 else jnp.concatenate(parts, axis=1)
            mask_last = lambda b: last_valid is not None and b == nsub - 1
            kb = k_ref[pl.ds(off, width), :].astype(BF16)
            vb = v_ref[pl.ds(off, width), :].astype(BF16)
            lb, lk = _sb_scores(qm, kb)
            dw = _dot(dom, vb, 1, 1)
            lks = [lk[:, sl] for sl in sls]
            first_valid = (off + lane) >= PAD
            lks[0] = jnp.where(first_valid, lks[0], 0.0)
            if last_valid is not None:
                lks[-1] = jnp.where(last_valid, lks[-1], 0.0)
            pins = [_dot_tri(lks[b], tri_in) for b in range(nsub)]
            run, gsum, dq = carry
            ws, gs = [], []
            for b in range(nsub):
                wb = jnp.exp(lb[:, sls[b]] + (tot - run - pins[b][0]))
                if b == 0:
                    wb = jnp.where(first_valid, wb, 0.0)
                if mask_last(b):
                    wb = jnp.where(last_valid, wb, 0.0)
                ws.append(wb.astype(BF16))
                gs.append(wb * dw[:, sls[b]])
                run = run + pins[b][1]
            gexs = [_dot_tri1(gs[b], tri_ex) for b in range(nsub)]
            beta = jnp.exp(lb)
            parts = []
            for b in range(nsub):
                bt = beta[:, sls[b]]
                dzb = gs[b] * (1.0 - bt) - (gsum + gexs[b][0]) * bt
                if b == 0:
                    dzb = jnp.where(first_valid, dzb, 0.0)
                if mask_last(b):
                    dzb = jnp.where(last_valid, dzb, 0.0)
                parts.append(dzb.astype(BF16))
                gsum = gsum + gexs[b][1]
            dz, w = cat(parts), cat(ws)
            dk_acc[pl.ds(off, width), :] += _dot(dz, qm, 0, 0)
            dv_acc[pl.ds(off, width), :] += _dot(w, dom, 0, 0)
            return run, gsum, dq + _dot(dz, kb)

        diag = lane < qrow
        zero = jnp.zeros((2 * BLK, BLK), F32)
        top = i - SB_FIRST

        def walk(n):
            def body(state):
                off = pl.multiple_of(state[0] * BLK, BLK)
                return (state[0] + n, *chunk(off, n, None, state[1:]))

            return (lambda state: state[0] + (n - 1) <= top), body

        state = (jnp.max(tlv[:, 1:2]).astype(jnp.int32), zero, zero, zero)
        state = lax.while_loop(*walk(SB_GROUP), state)
        carry = lax.while_loop(*walk(1), state)[1:]
        before = jnp.minimum(i, SB_FIRST - 1)
        first_off = pl.multiple_of((i - before) * BLK, BLK)
        dq = lax.switch(before, [functools.partial(chunk, first_off, n, diag) for n in range(1, SB_FIRST + 1)],
                        carry)[2]
        dq_ref[...] = (jnp.where(lane[:BLK] < HEAD_DIM, dq[:BLK], dq[BLK:]) * scale).astype(BF16)

        @pl.when(i == nb - 1)
        def _():
            dk_ref[...] = dk_acc[...].astype(BF16)
            dv_ref[...] = dv_acc[...].astype(BF16)

    qc, kc, vc = C_Q // BLK, C_K // BLK, C_V // BLK
    blk = pl.BlockSpec((BLK, BLK), lambda p, i: (i, p))
    full = pl.BlockSpec((lp, BLK), lambda p, i: (0, p))
    w = N_HEADS * HEAD_DIM
    res = pl.pallas_call(
        body, name="sb_bwd", grid=(npair, nb),
        in_specs=[pl.BlockSpec((BLK, BLK), lambda p, i: (i, qc + p)),
                  pl.BlockSpec((lp, BLK), lambda p, i: (0, kc + p)),
                  pl.BlockSpec((lp, BLK), lambda p, i: (0, vc + p)),
                  blk, blk] + [ANY] * ns,
        out_specs=[blk, full, full] + [ANY] * nl,
        out_shape=[jax.ShapeDtypeStruct((lp, w), BF16)] * 3 + list(ride[1]),
        scratch_shapes=[pltpu.VMEM((lp, BLK), F32), pltpu.VMEM((lp, BLK), F32)] + _ride_scratch(ride),
        compiler_params=_cparams(("arbitrary", "arbitrary")))(proj, proj, proj, tl, do, *ride[0])
    return res[0], res[1], res[2], list(res[3:])


def _log1p(e):
    u = 1.0 + e
    return jnp.where(u == 1.0, e, jnp.log(u) * e / jnp.where(u == 1.0, 1.0, u - 1.0))


def _ssd_common(c, dtr, bias, alog):
    row = _iota((BLK, BLK), 0)
    lane = _iota((BLK, BLK), 1)
    live = ((c * BLK + row) >= PAD) & (lane < N_HEADS)
    pre = dtr + bias
    dt = jnp.where(live, jnp.maximum(pre, 0.0) + _log1p(jnp.exp(-jnp.abs(pre))), 0.0)
    a_neg = -jnp.exp(alog)
    a = dt * a_neg
    t_in = (lane <= row).astype(BF16)
    cs = _dot_sel_l(t_in, a)
    cs_t = cs.T
    cs_end = cs[BLK - 1:BLK, :]
    e = jnp.exp(cs)
    f = jnp.exp(cs_end - cs)
    xp = ((_iota((BLK, SSD_INNER), 1) // HEAD_DIM) == _iota((BLK, SSD_INNER), 0)).astype(BF16)
    xp_t = ((_iota((SSD_INNER, BLK), 0) // HEAD_DIM) == _iota((SSD_INNER, BLK), 1)).astype(BF16)
    decay_col = _dot_sel_l(xp_t, jnp.exp(cs_t))[:, BLK - 1:BLK]
    return dict(live=live, pre=pre, dt=dt, a_neg=a_neg, cs=cs, cs_t=cs_t, e=e, f=f, xp=xp, xp_t=xp_t,
                decay_col=decay_col, row=row, lane=lane,
                dt_x=_dot_sel_r(dt, xp), e_x=_dot_sel_r(e, xp), f_x=_dot_sel_r(f, xp))


def _ssd_ldec(q, h):
    diff = q["cs"][:, h:h + 1] - q["cs_t"][h:h + 1, :]
    causal = q["row"] >= q["lane"]
    return jnp.where(causal, jnp.exp(jnp.where(causal, diff, 0.0)), 0.0)


def _ssd_fwd_call(xbc, proj, bias, alog, d_x, norm_g):
    lp = xbc.shape[0]
    nb = lp // BLK
    gw = SSD_INNER // SSD_GROUPS
    ppg = gw // BLK

    def body(xbc_ref, dtr_ref, z_ref, bias_ref, alog_ref, dx_ref, ng_ref, yb_ref, ypre_ref, sprev_ref, s_ref):
        c = pl.program_id(0)

        @pl.when(c == 0)
        def _():
            s_ref[...] = jnp.zeros_like(s_ref)

        q = _ssd_common(c, dtr_ref[...], bias_ref[...], alog_ref[...])
        x = xbc_ref[:, 0:SSD_INNER]
        xd = x * q["dt_x"]
        low = q["lane"] < HEAD_DIM
        s_old = s_ref[...]
        sprev_ref[...] = s_old
        xdf = (xd * q["f_x"]).astype(BF16)
        for g in range(SSD_GROUPS):
            bg = xbc_ref[:, SSD_INNER + g * SSD_STATE:SSD_INNER + (g + 1) * SSD_STATE].astype(BF16)
            cg = xbc_ref[:, SSD_INNER + (SSD_GROUPS + g) * SSD_STATE:
                         SSD_INNER + (SSD_GROUPS + g + 1) * SSD_STATE].astype(BF16)
            cb = _dot(cg, bg, 1, 1)
            gs = slice(g * gw, (g + 1) * gw)
            y_off = _dot(cg, s_old[gs, :].astype(BF16), 1, 1) * q["e_x"][:, gs]
            s_ref[gs, :] = s_old[gs, :] * q["decay_col"][gs, :] + _dot(xdf[:, gs], bg, 0, 0)
            for pr in range(ppg):
                cols = slice(g * gw + pr * BLK, g * gw + (pr + 1) * BLK)
                xd_p = xd[:, cols]
                acc = y_off[:, pr * BLK:(pr + 1) * BLK]
                for hh in range(2):
                    h = (g * gw + pr * BLK) // HEAD_DIM + hh
                    m = (cb * _ssd_ldec(q, h)).astype(BF16)
                    xm = jnp.where(low, xd_p, 0.0) if hh == 0 else jnp.where(low, 0.0, xd_p)
                    acc = acc + _dot(m, xm.astype(BF16))
                ypre_ref[:, cols] = acc
        ypre = ypre_ref[...] + x * dx_ref[...]
        ypre_ref[...] = ypre
        z = z_ref[...]
        yg = ypre * (z * _sigmoid(z))
        _, yh = _rms_stats(yg)
        yb_ref[...] = (yh * ng_ref[...]).astype(BF16)

    row = lambda w, col: pl.BlockSpec((BLK, w), lambda c: (c, col))
    vec = lambda w: pl.BlockSpec((1, w), lambda c: (0, 0))
    return pl.pallas_call(
        body, name="ssd_fwd", grid=(nb,),
        in_specs=[row(XBC, 0), row(BLK, C_DT // BLK), row(SSD_INNER, 0), vec(BLK), vec(BLK),
                  vec(SSD_INNER), vec(SSD_INNER)],
        out_specs=[row(SSD_INNER, 0), row(SSD_INNER, 0),
                   pl.BlockSpec((None, SSD_INNER, SSD_STATE), lambda c: (c, 0, 0))],
        out_shape=[jax.ShapeDtypeStruct((lp, 2 * SSD_INNER), BF16), jax.ShapeDtypeStruct((lp, SSD_INNER), F32),
                   jax.ShapeDtypeStruct((nb, SSD_INNER, SSD_STATE), F32)],
        scratch_shapes=[pltpu.VMEM((SSD_INNER, SSD_STATE), F32)],
        compiler_params=_cparams(("arbitrary",)))(xbc, proj, proj, bias, alog, d_x, norm_g)


def _ssd_bwd_call(dycat, ypre, xbc, proj, sprev, bias, alog, d_x, norm_g):
    lp = xbc.shape[0]
    nb = lp // BLK
    gw = SSD_INNER // SSD_GROUPS
    ppg = gw // BLK

    def body(dy_ref, ypre_ref, xbc_ref, dtr_ref, z_ref, sp_ref, bias_ref, alog_ref, dxp_ref, ng_ref,
             dz_ref, dxbc_ref, ddt_ref, dng_ref, dd_ref, dal_ref, dbi_ref, ds_ref, dxd_ref):
        step = pl.program_id(0)
        c = nb - 1 - step

        @pl.when(step == 0)
        def _():
            ds_ref[...] = jnp.zeros_like(ds_ref)

        q = _ssd_common(c, dtr_ref[...], bias_ref[...], alog_ref[...])
        row, lane = q["row"], q["lane"]
        low = lane < HEAD_DIM
        rowlive = ((c * BLK + _iota((BLK, 1), 0)) >= PAD)
        x = xbc_ref[:, 0:SSD_INNER]
        xd = x * q["dt_x"]
        z = z_ref[...]
        sz = _sigmoid(z)
        silu = z * sz
        ypre = ypre_ref[...]
        dyg, dng = _rms_bwd(ypre * silu, ng_ref[...], dy_ref[...])
        _acc_rows(dng_ref, dng, step)
        dyp = dyg * silu
        dz_ref[...] = jnp.where(rowlive, dyg * ypre * (sz * (1.0 + z * (1.0 - sz))), 0.0).astype(BF16)
        _acc_rows(dd_ref, jnp.sum(dyp * x, axis=0, keepdims=True), step)
        dye = dyp * q["e_x"]
        xdf = xd * q["f_x"]
        s_prev = sp_ref[...]
        ds_old = ds_ref[...]
        qrow = jnp.zeros((BLK, BLK), F32)
        qcol_t = jnp.zeros((BLK, BLK), F32)
        red_e = []
        red_f = []
        for g in range(SSD_GROUPS):
            gs = slice(g * gw, (g + 1) * gw)
            bsl = slice(SSD_INNER + g * SSD_STATE, SSD_INNER + (g + 1) * SSD_STATE)
            csl = slice(SSD_INNER + (SSD_GROUPS + g) * SSD_STATE, SSD_INNER + (SSD_GROUPS + g + 1) * SSD_STATE)
            bg = xbc_ref[:, bsl].astype(BF16)
            cg = xbc_ref[:, csl].astype(BF16)
            sg = s_prev[gs, :].astype(BF16)
            dsg = ds_old[gs, :].astype(BF16)
            cb = _dot(cg, bg, 1, 1)
            bds = _dot(bg, dsg, 1, 1)
            y_off = _dot(cg, sg, 1, 1) * q["e_x"][:, gs]
            red_e.append(dyp[:, gs] * y_off)
            red_f.append(xd[:, gs] * bds * q["f_x"][:, gs])
            dc = _dot(dye[:, gs].astype(BF16), sg)
            db = _dot(xdf[:, gs].astype(BF16), dsg)
            ds_ref[gs, :] = ds_old[gs, :] * q["decay_col"][gs, :] + _dot(dye[:, gs].astype(BF16), cg, 0, 0)
            dcb = jnp.zeros((BLK, BLK), F32)
            for pr in range(ppg):
                cols = slice(g * gw + pr * BLK, g * gw + (pr + 1) * BLK)
                xd_p = xd[:, cols].astype(BF16)
                dy_p = dyp[:, cols]
                acc = q["f_x"][:, cols] * bds[:, pr * BLK:(pr + 1) * BLK]
                for hh in range(2):
                    h = (g * gw + pr * BLK) // HEAD_DIM + hh
                    ld = _ssd_ldec(q, h)
                    m = cb * ld
                    dym = (jnp.where(low, dy_p, 0.0) if hh == 0 else jnp.where(low, 0.0, dy_p)).astype(BF16)
                    dm = jnp.where(row >= lane, _dot(dym, xd_p, 1, 1), 0.0)
                    acc = acc + _dot(m.astype(BF16), dym, 0, 0)
                    dcb = dcb + dm * ld
                    qq = dm * m
                    qrow = qrow + jnp.where(lane == h, jnp.sum(qq, axis=1, keepdims=True), 0.0)
                    qcol_t = qcol_t + jnp.where(row == h, jnp.sum(qq, axis=0, keepdims=True), 0.0)
                dxd_ref[:, cols] = acc
            dcbb = dcb.astype(BF16)
            dxbc_ref[:, bsl] = jnp.where(rowlive, db + _dot(dcbb, cg, 0, 0), 0.0)
            dxbc_ref[:, csl] = jnp.where(rowlive, dc + _dot(dcbb, bg), 0.0)
        dxd = dxd_ref[...]
        dxbc_ref[:, 0:SSD_INNER] = jnp.where(rowlive, dxd * q["dt_x"] + dyp * dxp_ref[...], 0.0)
        xp_t = q["xp_t"]
        fw = _dot_sel_r(jnp.concatenate(red_f, axis=1), xp_t)
        dcs = qrow - qcol_t.T + _dot_sel_r(jnp.concatenate(red_e, axis=1), xp_t) - fw
        end_f = jnp.sum(fw, axis=0, keepdims=True)
        sds = jnp.sum(ds_old * s_prev, axis=1, keepdims=True)
        per_head = _dot_sel_l(q["xp"], jnp.broadcast_to(sds, (SSD_INNER, BLK)))
        end_e = per_head.T[0:1, :] * jnp.exp(q["cs"][BLK - 1:BLK, :])
        dcs = dcs + jnp.where(row == BLK - 1, end_f + end_e, 0.0)
        t_up = (lane >= row).astype(BF16)
        da = _dot_sel_l(t_up, dcs)
        ddt = da * q["a_neg"] + _dot_sel_r(dxd * x, xp_t)
        _acc_rows(dal_ref, jnp.sum(da * q["dt"] * q["a_neg"], axis=0, keepdims=True), step)
        ddtr = jnp.where(q["live"], ddt * _sigmoid(q["pre"]), 0.0)
        ddt_ref[...] = ddtr.astype(BF16)
        _acc_rows(dbi_ref, jnp.sum(ddtr, axis=0, keepdims=True), step)

    row_s = lambda w, col: pl.BlockSpec((BLK, w), lambda s: (nb - 1 - s, col))
    vec = lambda w: pl.BlockSpec((1, w), lambda s: (0, 0))
    return pl.pallas_call(
        body, name="ssd_bwd", grid=(nb,),
        in_specs=[row_s(SSD_INNER, 0), row_s(SSD_INNER, 0), row_s(XBC, 0), row_s(BLK, C_DT // BLK),
                  row_s(SSD_INNER, 0), pl.BlockSpec((None, SSD_INNER, SSD_STATE), lambda s: (nb - 1 - s, 0, 0)),
                  vec(BLK), vec(BLK), vec(SSD_INNER), vec(SSD_INNER)],
        out_specs=[row_s(SSD_INNER, 0), row_s(XBC, 0), row_s(BLK, 0),
                   vec(SSD_INNER), vec(SSD_INNER), vec(BLK), vec(BLK)],
        out_shape=[jax.ShapeDtypeStruct((lp, SSD_INNER), BF16), jax.ShapeDtypeStruct((lp, XBC), F32),
                   jax.ShapeDtypeStruct((lp, BLK), BF16),
                   jax.ShapeDtypeStruct((1, SSD_INNER), F32), jax.ShapeDtypeStruct((1, SSD_INNER), F32),
                   jax.ShapeDtypeStruct((1, BLK), F32), jax.ShapeDtypeStruct((1, BLK), F32)],
        scratch_shapes=[pltpu.VMEM((SSD_INNER, SSD_STATE), F32), pltpu.VMEM((BLK, SSD_INNER), F32)],
        compiler_params=_cparams(("arbitrary",)))(dycat, ypre, xbc, proj, proj, sprev, bias, alog, d_x, norm_g)


def _pad_rows8(w):
    return jnp.pad(w, ((0, 8 - w.shape[0]), (0, 0)))


def _pad_lanes(v, n=BLK):
    return jnp.pad(v, ((0, 0), (0, n - v.shape[1])))


def _local_step(x, target, wt, late_shards, late_weights, w_in_shards):
    seq = x.shape[0]
    lp = seq + BLK
    tm = _pick(lp, [1408, 768, 384, 128])
    tkr = _pick(lp, [1408, 384, 128])
    rbc = _pick(lp, [384, 128])
    h0 = jnp.concatenate([jnp.zeros((PAD, D_MODEL), F32), wt["meta"], x], axis=0)
    bias = _pad_lanes(wt["ssd_dt_bias"])
    alog = _pad_lanes(wt["ssd_a_log"])
    d_x = jnp.repeat(wt["ssd_d"], HEAD_DIM, axis=1)
    cw8 = _pad_rows8(wt["ssd_conv_w"])
    fw8 = _pad_rows8(wt["ffn_conv_w"])
    fcw = D_FF // 2

    xn1 = _rms_fwd_call(h0, wt["mix_pre_g"], "norm1")
    proj, late_a = _mm(xn1, wt["w_in"], tm=tm, tn=1152, tk=D_MODEL, ride=_gather_ride(late_shards[0:1]),
                       name="mm_proj")
    proj = proj[0]
    conv_pre, xbc = _conv_fwd_call(proj, C_XBC, XBC, 512, cw8, wt["ssd_conv_b"], 4, rb=rbc, name="ssd_conv_fwd")
    y_ssd, ypre, sprev = _ssd_fwd_call(xbc, proj, bias, alog, d_x, wt["ssd_norm_g"])
    o, tl, late_b = _sb_fwd_call(proj, _gather_ride(late_shards[1:3]))
    ycat = _rms_fwd_call(o, wt["sb_norm_g"], "sb_norm", beside=y_ssd)
    w_out, w_up, w_down = late_weights([late_a[0], late_b[0], late_b[1]])
    mix = _mm(ycat, w_out, tm=tm, tn=1024, tk=2048, name="mm_mix")[0]
    h1, xn2 = _mid_fwd_call(h0, mix, wt["mix_post_g"], wt["ffn_pre_g"])
    gu = _mm(xn2, w_up, tm=tm, tn=1408, tk=D_MODEL, name="mm_up")[0]
    gpre, act = _conv_fwd_call(gu, 0, D_FF, fcw, fw8, wt["ffn_conv_b"], 3, gate_src=gu, gate_col0=D_FF,
                               rb=rbc, name="ffn_conv_fwd")
    f = _mm(act, w_down, tm=tm, tn=1024, tk=1408, name="mm_down")[0]
    loss_row, df, dh2, dg_ffn_post = _final_call(h1, f, wt["ffn_post_g"], target)

    dact = _mm(df, w_down, tb=True, tm=tm, tn=1408, tk=D_MODEL, name="mm_dact")[0]
    dw_down, dw_down_b = _mm(act, df, ta=True, tm=1408, tn=1024, tk=tkr, extra_bf16=True, name="mm_dw_down")
    by_chip = lambda g: g.reshape(N_CHIPS, -1, D_MODEL)
    dgate, dup, dfcw, dfcb = _conv_bwd_call(gu, 0, D_FF, fcw, fw8, 3, gpre, dact, gate_src=gu, gate_col0=D_FF,
                                            rb=rbc, name="ffn_conv_bwd")
    dgu = jnp.concatenate([dgate, dup], axis=1)
    dxn2, land_down = _mm(dgu, w_up, tb=True, tm=tm, tn=1024, tk=1408, ride=_scatter_ride(by_chip(dw_down_b)),
                          name="mm_dxn2")
    dw_up, dw_up_b = _mm(xn2, dgu, ta=True, tm=1024, tn=1408, tk=tkr, nsplit=N_CHIPS, extra_bf16=True,
                         name="mm_dw_up")
    dh1, dmix, dg_ffn_pre, dg_mix_post = _mid_bwd_call(dh2, h1, dxn2[0], mix, wt["ffn_pre_g"], wt["mix_post_g"])
    dycat = _mm(dmix, w_out, tb=True, tm=tm, tn=1024, tk=D_MODEL, name="mm_dycat")[0]
    dw_out, dw_out_b = _mm(ycat, dmix, ta=True, tm=1024, tn=1024, tk=tkr, extra_bf16=True, name="mm_dw_out")
    do, dg_sb = _norm_bwd_call(o, wt["sb_norm_g"], dycat, 1, "sb_norm_bwd")
    dq, dk, dv, lands = _sb_bwd_call(proj, tl, do, _join_rides(_scatter_ride(dw_up_b),
                                                                  _scatter_ride(by_chip(dw_out_b))))
    dz, dxbc_act, ddt, dg_ssd, dd_x, dalog, dbias = _ssd_bwd_call(
        dycat, ypre, xbc, proj, sprev, bias, alog, d_x, wt["ssd_norm_g"])
    dxbc, dcw, dcb = _conv_bwd_call(proj, C_XBC, XBC, 512, cw8, 4, conv_pre, dxbc_act, rb=rbc,
                                    name="ssd_conv_bwd")
    dproj = jnp.concatenate([dz, dxbc, ddt, dq, dk, dv], axis=1)
    dw_in, dw_in_b = w_in_shards(_mm(xn1, dproj, ta=True, tm=1024, tn=1152, tk=tkr, name="mm_dw_in")[0])
    half = dw_in_b.shape[1] // 2
    part = half * 5 // 8
    dxn1, land_in = _mm(dproj, wt["w_in"], tb=True, tm=tm, tn=1024, tk=1152, ride=_scatter_ride(dw_in_b, 0, part),
                        name="mm_dxn1")
    dh0, dg_pre, land_in = _norm_bwd_call(h0, wt["mix_pre_g"], dxn1[0], 0, "norm1_bwd", res=dh1,
                                          ride=_scatter_ride(dw_in_b, part, half - part), land=land_in[0])

    small = {
        "meta_tokens": dh0[PAD:BLK], "mix_pre_g": dg_pre, "ssd_conv_w": dcw[:4], "ssd_conv_b": dcb,
        "ssd_dt_bias": dbias[:, :N_HEADS], "ssd_a_log": dalog[:, :N_HEADS],
        "ssd_d": jnp.sum(dd_x.reshape(N_HEADS, HEAD_DIM), axis=1)[None],
        "ssd_norm_g": dg_ssd, "sb_norm_g": dg_sb, "mix_post_g": dg_mix_post, "ffn_pre_g": dg_ffn_pre,
        "ffn_conv_w": dfcw[:3], "ffn_conv_b": dfcb, "ffn_post_g": dg_ffn_post,
    }
    pending = {"w_in": (dw_in, land_in), "w_out": (by_chip(dw_out), lands[1]), "w_up": (dw_up, lands[0]),
               "w_down": (by_chip(dw_down), land_down[0])}
    return loss_row, dh0[BLK:], small, pending


def _adamw_call(w, g, m, v, name):
    rows, cols = w.shape
    tr = 256 if rows % 256 == 0 else (352 if rows % 352 == 0 else rows)
    c1 = 1.0 - ADAM_B1 ** ADAM_STEP
    c2 = 1.0 - ADAM_B2 ** ADAM_STEP

    def body(w_ref, g_ref, m_ref, v_ref, d_ref, mo_ref, vo_ref):
        gv = g_ref[...]
        m2 = ADAM_B1 * m_ref[...] + (1.0 - ADAM_B1) * gv
        v2 = ADAM_B2 * v_ref[...] + (1.0 - ADAM_B2) * (gv * gv)
        d_ref[...] = -ADAM_LR * ((m2 / c1) / (jnp.sqrt(v2 / c2) + ADAM_EPS) + ADAM_WD * w_ref[...])
        mo_ref[...] = m2
        vo_ref[...] = v2

    spec = pl.BlockSpec((tr, cols), lambda i: (i, 0))
    return pl.pallas_call(
        body, name=name, grid=(rows // tr,), in_specs=[spec] * 4, out_specs=[spec] * 3,
        out_shape=[jax.ShapeDtypeStruct((rows, cols), F32)] * 3,
        compiler_params=_cparams(("parallel",)))(w, g, m, v)


ANY = pl.BlockSpec(memory_space=pl.ANY)


def _place():
    x, y, c = lax.axis_index("x"), lax.axis_index("y"), lax.axis_index("c")
    chips = [(1 - x, y), (x, 1 - y), (1 - x, 1 - y)]
    return x, y, c, chips


def _half(c, h):
    return pl.ds(pl.multiple_of(c * h, 8), h)


def _allgather_call(shards):
    n = len(shards)

    def body(*refs):
        ins, outs = refs[:n], refs[n:2 * n]
        send_i, recv_i, send_d, recv_d = refs[2 * n:]
        x, y, c, chips = _place()
        me = 2 * x + y
        sends = []
        for a in range(n):
            h = shards[a].shape[0] // 2
            for j, chip in enumerate(chips):
                cp = pltpu.make_async_remote_copy(
                    src_ref=ins[a].at[_half(c, h)], dst_ref=outs[a].at[me, _half(c, h)],
                    send_sem=send_i.at[3 * a + j], recv_sem=recv_i.at[3 * a + j],
                    device_id=(*chip, c), device_id_type=MESH)
                cp.start()
                sends.append(cp)
        for a in range(n):
            h = shards[a].shape[0] // 2
            for j, chip in enumerate(chips):
                src = 2 * chip[0] + chip[1]
                landed = outs[a].at[src, _half(c, h)]
                pltpu.make_async_remote_copy(
                    src_ref=landed, dst_ref=landed, send_sem=send_i.at[3 * a + j], recv_sem=recv_i.at[3 * a + j],
                    device_id=(*chip, c), device_id_type=MESH).wait_recv()
                cp = pltpu.make_async_remote_copy(
                    src_ref=landed, dst_ref=landed, send_sem=send_d.at[3 * a + j], recv_sem=recv_d.at[3 * a + j],
                    device_id=(x, y, 1 - c), device_id_type=MESH)
                cp.start()
                sends.append(cp)
        for a in range(n):
            h = shards[a].shape[0] // 2
            for j, chip in enumerate(chips):
                src = 2 * chip[0] + chip[1]
                other = outs[a].at[src, _half(1 - c, h)]
                pltpu.make_async_remote_copy(
                    src_ref=other, dst_ref=other, send_sem=send_d.at[3 * a + j], recv_sem=recv_d.at[3 * a + j],
                    device_id=(x, y, 1 - c), device_id_type=MESH).wait_recv()
        for cp in sends:
            cp.wait_send()

    return pl.pallas_call(
        body, name="allgather_weights", in_specs=[ANY] * n, out_specs=[ANY] * n,
        out_shape=[jax.ShapeDtypeStruct((N_CHIPS,) + s.shape, s.dtype) for s in shards],
        scratch_shapes=[pltpu.SemaphoreType.DMA((3 * n,))] * 4,
    )(*shards)


def _ride_scratch(ride):
    return [pltpu.SemaphoreType.DMA((ride[3],)), pltpu.SemaphoreType.DMA((ride[3],))]


def _ride_run(ride, src_refs, land_refs, send, recv, first, last):
    plan = ride[2]

    @pl.when(first)
    def _():
        for k, (src, dst, _, dev) in enumerate(plan(src_refs, land_refs)):
            pltpu.make_async_remote_copy(src_ref=src, dst_ref=dst, send_sem=send.at[k], recv_sem=recv.at[k],
                                         device_id=dev, device_id_type=MESH).start()

    @pl.when(last)
    def _():
        for k, (src, _, land, dev) in enumerate(plan(src_refs, land_refs)):
            cp = pltpu.make_async_remote_copy(src_ref=src, dst_ref=land, send_sem=send.at[k], recv_sem=recv.at[k],
                                              device_id=dev, device_id_type=MESH)
            cp.wait_send()
            cp.wait_recv()


def _join_rides(r1, r2):
    n1, l1 = len(r1[0]), len(r1[1])

    def plan(srcs, lands):
        return r1[2](srcs[:n1], lands[:l1]) + r2[2](srcs[n1:], lands[l1:])

    return (r1[0] + r2[0], r1[1] + r2[1], plan, r1[3] + r2[3])


def _gather_ride(shards):
    return (list(shards), [jax.ShapeDtypeStruct((N_CHIPS,) + s.shape, s.dtype) for s in shards],
            _gather_plan(len(shards)), 3 * len(shards))


def _scatter_ride(g_b, r0=0, nr=None):
    h = g_b.shape[1] // 2
    return ([g_b], [jax.ShapeDtypeStruct((8, h, g_b.shape[2]), BF16)], _scatter_plan(h, r0, h if nr is None else nr), 7)


def _gather_plan(n):
    def plan(srcs, lands):
        x, y, c, chips = _place()
        me = 2 * x + y
        return [(srcs[a], lands[a].at[me], lands[a].at[2 * chip[0] + chip[1]], (*chip, c))
                for a in range(n) for chip in chips]
    return plan


def _scatter_plan(h, r0, nr):
    def plan(srcs, lands):
        x, y, c, _ = _place()
        me = 4 * x + 2 * y + c
        out = []
        for p in range(1, 8):
            px = 1 - x if p & 4 else x
            py = 1 - y if p & 2 else y
            pc = 1 - c if p & 1 else c
            rows = pl.ds(pl.multiple_of(pc * h + r0, 8), nr)
            out.append((srcs[0].at[2 * px + py, rows], lands[0].at[me, pl.ds(r0, nr)],
                        lands[0].at[4 * px + 2 * py + pc, pl.ds(r0, nr)], (px, py, pc)))
        return out
    return plan


def _grad_sum_call(own, land, place, name):
    _, h, cols = land.shape
    th = _pick(h, [256, 176, 8])
    nt = h // th

    def body(p_ref, own_ref, *refs):
        acc = own_ref[...]
        for r in refs[:7]:
            acc = acc + r[...].astype(F32)
        refs[7][...] = acc

    def peer(k):
        return pl.BlockSpec((None, th, cols), lambda i, p_ref: (p_ref[2 + k], i, 0))

    return pl.pallas_call(
        body, name=name,
        grid_spec=pltpu.PrefetchScalarGridSpec(
            num_scalar_prefetch=1, grid=(nt,),
            in_specs=[pl.BlockSpec((None, th, cols), lambda i, p_ref: (p_ref[1], p_ref[0] * nt + i, 0))]
            + [peer(k) for k in range(7)],
            out_specs=pl.BlockSpec((th, cols), lambda i, p_ref: (p_ref[0] * nt + i, 0))),
        out_shape=jax.ShapeDtypeStruct((2 * h, cols), F32),
        compiler_params=_cparams(("parallel",)))(place, own, *[land] * 7)


def _half_exchange_call(shards):
    n = len(shards)

    def body(*refs):
        outs = refs[n:2 * n]
        send_d, recv_d = refs[2 * n:]
        x, y, c, _ = _place()
        cps = []
        for a in range(n):
            h = shards[a].shape[0] // 2
            mine = outs[a].at[_half(c, h)]
            cp = pltpu.make_async_remote_copy(
                src_ref=mine, dst_ref=mine, send_sem=send_d.at[a], recv_sem=recv_d.at[a],
                device_id=(x, y, 1 - c), device_id_type=MESH)
            cp.start()
            cps.append(cp)
        for a, cp in enumerate(cps):
            h = shards[a].shape[0] // 2
            theirs = outs[a].at[_half(1 - c, h)]
            pltpu.make_async_remote_copy(
                src_ref=theirs, dst_ref=theirs, send_sem=send_d.at[a], recv_sem=recv_d.at[a],
                device_id=(x, y, 1 - c), device_id_type=MESH).wait_recv()
            cp.wait_send()

    return pl.pallas_call(
        body, name="grad_half_exchange", in_specs=[ANY] * n, out_specs=[ANY] * n,
        out_shape=[jax.ShapeDtypeStruct(sv.shape, F32) for sv in shards],
        input_output_aliases={a: a for a in range(n)},
        scratch_shapes=[pltpu.SemaphoreType.DMA((n,))] * 2,
    )(*shards)


def _allreduce_small_call(arrs):
    n = len(arrs)
    offs, rows = [], 0
    for a in arrs:
        offs.append(rows)
        rows += a.shape[0]
    rows = -(-rows // 8) * 8
    width = -(-max(a.shape[1] for a in arrs) // BLK) * BLK

    def body(*refs):
        ins, outs = refs[:n], refs[n:2 * n]
        gath, send_sems, recv_sems = refs[2 * n:]
        x, y, c, _ = _place()
        me = 4 * x + 2 * y + c
        mine = gath.at[me]
        mine[...] = jnp.zeros((rows, width), F32)
        for k in range(n):
            r, w = arrs[k].shape
            mine[offs[k]:offs[k] + r, 0:w] = ins[k][...]
        peers = []
        for p in range(1, 8):
            px = 1 - x if p & 4 else x
            py = 1 - y if p & 2 else y
            pc = 1 - c if p & 1 else c
            peers.append((4 * px + 2 * py + pc, (px, py, pc)))
        for k, (_, dev) in enumerate(peers):
            pltpu.make_async_remote_copy(src_ref=mine, dst_ref=mine, send_sem=send_sems.at[k],
                                         recv_sem=recv_sems.at[k], device_id=dev, device_id_type=MESH).start()
        for k, (idx, dev) in enumerate(peers):
            cp = pltpu.make_async_remote_copy(src_ref=mine, dst_ref=gath.at[idx], send_sem=send_sems.at[k],
                                              recv_sem=recv_sems.at[k], device_id=dev, device_id_type=MESH)
            cp.wait_send()
            cp.wait_recv()
        acc = gath[0]
        for d in range(1, 8):
            acc = acc + gath[d]
        for k in range(n):
            r, w = arrs[k].shape
            outs[k][...] = acc[offs[k]:offs[k] + r, 0:w]

    vm = pl.BlockSpec(memory_space=pltpu.VMEM)
    return pl.pallas_call(
        body, name="allreduce_small", in_specs=[vm] * n, out_specs=[vm] * n,
        out_shape=[jax.ShapeDtypeStruct(a.shape, F32) for a in arrs],
        scratch_shapes=[pltpu.VMEM((8, rows, width), F32), pltpu.SemaphoreType.DMA((7,)),
                        pltpu.SemaphoreType.DMA((7,))],
        compiler_params=pltpu.CompilerParams(vmem_limit_bytes=VMEM_LIMIT),
    )(*arrs)


def _adamw_small_call(ws, gs, ms, vs):
    n = len(ws)
    c1 = 1.0 - ADAM_B1 ** ADAM_STEP
    c2 = 1.0 - ADAM_B2 ** ADAM_STEP

    def body(*refs):
        for k in range(n):
            w_ref, g_ref, m_ref, v_ref = (refs[j * n + k] for j in range(4))
            d_ref, mo_ref, vo_ref = (refs[(4 + j) * n + k] for j in range(3))
            gv = g_ref[...]
            m2 = ADAM_B1 * m_ref[...] + (1.0 - ADAM_B1) * gv
            v2 = ADAM_B2 * v_ref[...] + (1.0 - ADAM_B2) * (gv * gv)
            d_ref[...] = -ADAM_LR * ((m2 / c1) / (jnp.sqrt(v2 / c2) + ADAM_EPS) + ADAM_WD * w_ref[...])
            mo_ref[...] = m2
            vo_ref[...] = v2

    vm = pl.BlockSpec(memory_space=pltpu.VMEM)
    res = pl.pallas_call(
        body, name="adamw_small", in_specs=[vm] * (4 * n), out_specs=[vm] * (3 * n),
        out_shape=[jax.ShapeDtypeStruct(a.shape, F32) for a in ws] * 3,
        compiler_params=pltpu.CompilerParams(vmem_limit_bytes=VMEM_LIMIT),
    )(*ws, *gs, *ms, *vs)
    return res[:n], res[n:2 * n], res[2 * n:]


def _pack(arrs, min_rows=8):
    parts = []
    for a in arrs:
        flat = a.reshape(-1).astype(F32)
        parts.append(jnp.pad(flat, (0, (-flat.shape[0]) % BLK)))
    buf = jnp.concatenate(parts).reshape(-1, BLK)
    return jnp.pad(buf, ((0, (-buf.shape[0]) % min_rows), (0, 0)))


def _unpack(buf, shapes):
    out, r = [], 0
    for shp in shapes:
        n = math.prod(shp)
        nr = -(-n // BLK)
        out.append(buf[r:r + nr].reshape(-1)[:n].reshape(shp))
        r += nr
    return out


SMALL = ["meta_tokens", "mix_pre_g", "ssd_conv_w", "ssd_conv_b", "ssd_dt_bias", "ssd_a_log", "ssd_d", "ssd_norm_g",
         "sb_norm_g", "mix_post_g", "ffn_pre_g", "ffn_conv_w", "ffn_conv_b", "ffn_post_g"]
BIG = ["w_in", "w_out", "w_up", "w_down"]
WEIGHTS = ["meta_tokens", "mix_pre_g", "w_in", "ssd_conv_w", "ssd_conv_b", "ssd_dt_bias", "ssd_a_log", "ssd_d",
           "ssd_norm_g", "sb_norm_g", "w_out", "mix_post_g", "ffn_pre_g", "w_up", "ffn_conv_w", "ffn_conv_b",
           "w_down", "ffn_post_g"]
W_IN_SHARD = IN_COLS // N_CHIPS
W_IN_PAD = 1536


def kernel(x, meta_tokens, mix_pre_g, w_in, ssd_conv_w, ssd_conv_b, ssd_dt_bias, ssd_a_log, ssd_d, ssd_norm_g, sb_norm_g, w_out, mix_post_g, ffn_pre_g, w_up, ffn_conv_w, ffn_conv_b, w_down, ffn_post_g, loss_target, m_meta_tokens, m_mix_pre_g, m_w_in, m_ssd_conv_w, m_ssd_conv_b, m_ssd_dt_bias, m_ssd_a_log, m_ssd_d, m_ssd_norm_g, m_sb_norm_g, m_w_out, m_mix_post_g, m_ffn_pre_g, m_w_up, m_ffn_conv_w, m_ffn_conv_b, m_w_down, m_ffn_post_g, v_meta_tokens, v_mix_pre_g, v_w_in, v_ssd_conv_w, v_ssd_conv_b, v_ssd_dt_bias, v_ssd_a_log, v_ssd_d, v_ssd_norm_g, v_sb_norm_g, v_w_out, v_mix_post_g, v_ffn_pre_g, v_w_up, v_ffn_conv_w, v_ffn_conv_b, v_w_down, v_ffn_post_g):
    w = dict(meta_tokens=meta_tokens, mix_pre_g=mix_pre_g, w_in=w_in, ssd_conv_w=ssd_conv_w, ssd_conv_b=ssd_conv_b, ssd_dt_bias=ssd_dt_bias, ssd_a_log=ssd_a_log, ssd_d=ssd_d, ssd_norm_g=ssd_norm_g, sb_norm_g=sb_norm_g, w_out=w_out, mix_post_g=mix_post_g, ffn_pre_g=ffn_pre_g, w_up=w_up, ffn_conv_w=ffn_conv_w, ffn_conv_b=ffn_conv_b, w_down=w_down, ffn_post_g=ffn_post_g)
    m = dict(meta_tokens=m_meta_tokens, mix_pre_g=m_mix_pre_g, w_in=m_w_in, ssd_conv_w=m_ssd_conv_w, ssd_conv_b=m_ssd_conv_b, ssd_dt_bias=m_ssd_dt_bias, ssd_a_log=m_ssd_a_log, ssd_d=m_ssd_d, ssd_norm_g=m_ssd_norm_g, sb_norm_g=m_sb_norm_g, w_out=m_w_out, mix_post_g=m_mix_post_g, ffn_pre_g=m_ffn_pre_g, w_up=m_w_up, ffn_conv_w=m_ffn_conv_w, ffn_conv_b=m_ffn_conv_b, w_down=m_w_down, ffn_post_g=m_ffn_post_g)
    v = dict(meta_tokens=v_meta_tokens, mix_pre_g=v_mix_pre_g, w_in=v_w_in, ssd_conv_w=v_ssd_conv_w, ssd_conv_b=v_ssd_conv_b, ssd_dt_bias=v_ssd_dt_bias, ssd_a_log=v_ssd_a_log, ssd_d=v_ssd_d, ssd_norm_g=v_ssd_norm_g, sb_norm_g=v_sb_norm_g, w_out=v_w_out, mix_post_g=v_mix_post_g, ffn_pre_g=v_ffn_pre_g, w_up=v_w_up, ffn_conv_w=v_ffn_conv_w, ffn_conv_b=v_ffn_conv_b, w_down=v_w_down, ffn_post_g=v_ffn_post_g)
    chip = 2 * lax.axis_index("x") + lax.axis_index("y")
    me = 2 * chip + lax.axis_index("c")
    place = jnp.stack([lax.axis_index("c"), chip] + [me ^ p for p in range(1, 8)]).astype(jnp.int32)

    shard_small = [w["meta_tokens"], w["ssd_conv_w"][0], w["ffn_conv_w"][0]]
    shards = [jnp.pad(w["w_in"][0], ((0, 0), (0, W_IN_PAD - W_IN_SHARD))).astype(BF16), _pack(shard_small, 16)]
    gathered = _allgather_call(shards)
    late_shards = [w["w_out"][0].astype(BF16), w["w_up"][0].astype(BF16), w["w_down"][0].astype(BF16)]

    def blocks(own, got):
        return [jnp.where(chip == i, own, got[i]) for i in range(N_CHIPS)]

    def late_weights(got):
        return (jnp.concatenate(blocks(late_shards[0], got[0]), axis=0),
                jnp.concatenate(blocks(late_shards[1], got[1]), axis=1),
                jnp.concatenate(blocks(late_shards[2], got[2]), axis=0))

    cut = DT_REAL_OFF + N_HEADS - W_IN_SHARD
    s_in = blocks(shards[0], gathered[0])
    w_in_c = jnp.concatenate(
        [s_in[0][:, :W_IN_SHARD], s_in[1][:, :cut], jnp.zeros((D_MODEL, BLK - N_HEADS), BF16),
         s_in[1][:, cut:W_IN_SHARD], s_in[2][:, :W_IN_SHARD], s_in[3][:, :W_IN_SHARD]], axis=1)
    parts = [_unpack(b, [s.shape for s in shard_small]) for b in blocks(shards[1], gathered[1])]
    wt = {k: w[k][0][None] if w[k].ndim == 3 else w[k] for k in
          ["mix_pre_g", "ssd_conv_b", "ssd_dt_bias", "ssd_a_log", "ssd_d", "ssd_norm_g", "sb_norm_g", "mix_post_g",
           "ffn_pre_g", "ffn_conv_b", "ffn_post_g"]}
    wt.update(
        meta=jnp.concatenate([p[0] for p in parts], axis=1),
        ssd_conv_w=jnp.concatenate([p[1] for p in parts], axis=1),
        ffn_conv_w=jnp.concatenate([p[2] for p in parts], axis=1), w_in=w_in_c)

    def w_in_shards(g):
        skip = BLK - N_HEADS
        cols = [g[:, :W_IN_SHARD],
                jnp.concatenate([g[:, W_IN_SHARD:W_IN_SHARD + cut], g[:, C_Q:2 * W_IN_SHARD + skip]], axis=1),
                g[:, 2 * W_IN_SHARD + skip:3 * W_IN_SHARD + skip], g[:, 3 * W_IN_SHARD + skip:]]
        g = jnp.stack([jnp.pad(b, ((0, 0), (0, W_IN_PAD - W_IN_SHARD))) for b in cols])
        return g, g.astype(BF16)

    loss_row, dx, small, pending = _local_step(x[0], loss_target[0], wt, late_shards, late_weights, w_in_shards)

    full = _half_exchange_call([_grad_sum_call(*pending[k], place, "grad_sum_" + k) for k in BIG])
    grads = {"w_in": full[0][:, :W_IN_SHARD], "w_out": full[1], "w_up": full[2], "w_down": full[3]}

    red_list = _allreduce_small_call([small[k] for k in SMALL] + [loss_row])
    loss = jnp.sum(red_list[-1])
    for k, g in zip(SMALL, red_list[:-1]):
        grads[k] = g
    for k in ["meta_tokens", "ssd_conv_w", "ffn_conv_w"]:
        wk = w[k].shape[-1]
        grads[k] = lax.dynamic_slice_in_dim(grads[k], chip * wk, wk, axis=1)

    delta, new_m, new_v = {}, {}, {}
    for k in BIG:
        delta[k], new_m[k], new_v[k] = _adamw_call(w[k][0], grads[k], m[k][0], v[k][0], "adamw_" + k)
    flat = lambda d: [d[k].reshape(grads[k].shape) for k in SMALL]
    res = _adamw_small_call(flat(w), [grads[k] for k in SMALL], flat(m), flat(v))
    for out, arrs in zip((delta, new_m, new_v), res):
        for k, a in zip(SMALL, arrs):
            out[k] = a

    def shaped(d, k):
        return d[k].reshape(w[k].shape)

    return (loss, dx[None], *[shaped(grads, k) for k in WEIGHTS], *[shaped(delta, k) for k in WEIGHTS],
            *[shaped(new_m, k) for k in WEIGHTS], *[shaped(new_v, k) for k in WEIGHTS])
```

```python
import functools
import math

import jax
import jax.numpy as jnp
from jax import lax
from jax.experimental import pallas as pl
from jax.experimental.pallas import tpu as pltpu

F32 = jnp.float32
BF16 = jnp.bfloat16

D_MODEL = 1024
N_META = 16
BLK = 128
PAD = BLK - N_META
HEAD_DIM = 64
N_HEADS = 16
SSD_GROUPS = 2
SSD_STATE = 128
SSD_INNER = 1024
XBC = SSD_INNER + 2 * SSD_GROUPS * SSD_STATE
D_FF = 2816
EPS = 1e-6
IN_COLS = 5648
C_Z, C_XBC, C_DT, C_Q, C_K, C_V, C_END = 0, 1024, 2560, 2688, 3712, 4736, 5760
DT_REAL_OFF = 2560
N_CHIPS = 4
ADAM_LR, ADAM_B1, ADAM_B2, ADAM_EPS, ADAM_WD, ADAM_STEP = 0.001, 0.9, 0.999, 1e-08, 0.01, 10
VMEM_LIMIT = 56 * 1024 * 1024
MESH = pl.DeviceIdType.MESH


def _cparams(sem=None, **kw):
    if sem is not None:
        kw["dimension_semantics"] = sem
    return pltpu.CompilerParams(vmem_limit_bytes=VMEM_LIMIT, **kw)


def _pick(n, cands):
    for c in cands:
        if n % c == 0:
            return c
    raise ValueError((n, cands))


def _iota(shape, dim):
    return lax.broadcasted_iota(jnp.int32, shape, dim)


def _sigmoid(x):
    return 1.0 / (1.0 + jnp.exp(-x))


def _split2(v):
    h1 = v.astype(BF16)
    return h1, (v - h1.astype(F32)).astype(BF16)


def _dot(a, b, ca=1, cb=0):
    return lax.dot_general(a, b, (((ca,), (cb,)), ((), ())), preferred_element_type=F32)


def _dot_sel_r(v, sel, cb=0):
    h1, h2 = _split2(v)
    return _dot(h1, sel, 1, cb) + _dot(h2, sel, 1, cb)


def _dot_sel_l(sel, v, ca=1):
    h1, h2 = _split2(v)
    return _dot(sel, h1, ca, 0) + _dot(sel, h2, ca, 0)


def _mm(a, b, *, ta=False, tb=False, tm, tn, tk, out_dtype=F32, nsplit=1, extra_bf16=False, ride=None, name):
    K, M = (a.shape if ta else a.shape[::-1])
    N = b.shape[0] if tb else b.shape[1]
    assert M % tm == 0 and N % tn == 0 and K % tk == 0, (name, M, N, K, tm, tn, tk)
    nm, nn, nk = M // tm, N // tn, K // tk
    assert nn % nsplit == 0
    per = nn // nsplit
    a_spec = (pl.BlockSpec((tk, tm), lambda i, j, k: (k, i)) if ta
              else pl.BlockSpec((tm, tk), lambda i, j, k: (i, k)))
    b_spec = (pl.BlockSpec((tn, tk), lambda i, j, k: (j, k)) if tb
              else pl.BlockSpec((tk, tn), lambda i, j, k: (k, j)))
    o_spec = pl.BlockSpec((None, tm, tn), lambda i, j, k: (j // per, i, j % per))
    n_out = 2 if extra_bf16 else 1
    ca, cb = (0 if ta else 1), (1 if tb else 0)
    ns, nl = (len(ride[0]), len(ride[1])) if ride else (0, 0)

    def body(a_ref, b_ref, *rest):
        outs = rest[ns:ns + n_out]
        if ride:
            step = (pl.program_id(0) * nn + pl.program_id(1)) * nk + pl.program_id(2)
            _ride_run(ride, rest[:ns], rest[ns + n_out:ns + n_out + nl], rest[-2], rest[-1],
                      step == 0, step == nm * nn * nk - 1)
        p = _dot(a_ref[...].astype(BF16), b_ref[...].astype(BF16), ca, cb)

        def emit(val):
            outs[0][...] = val.astype(out_dtype)
            if extra_bf16:
                outs[1][...] = val.astype(BF16)

        if nk == 1:
            emit(p)
        else:
            acc = rest[ns + n_out + nl]
            k = pl.program_id(2)

            @pl.when(k == 0)
            def _():
                acc[...] = p

            @pl.when(k > 0)
            def _():
                acc[...] += p

            @pl.when(k == nk - 1)
            def _():
                emit(acc[...])

    shp = (nsplit, M, N // nsplit)
    out_shape = [jax.ShapeDtypeStruct(shp, out_dtype)]
    out_specs = [o_spec]
    if extra_bf16:
        out_shape.append(jax.ShapeDtypeStruct(shp, BF16))
        out_specs.append(o_spec)
    scratch = [pltpu.VMEM((tm, tn), F32)] if nk > 1 else []
    if ride:
        res = pl.pallas_call(
            body, name=name, grid=(nm, nn, nk), in_specs=[a_spec, b_spec] + [ANY] * ns,
            out_specs=out_specs + [ANY] * nl, out_shape=out_shape + list(ride[1]),
            scratch_shapes=scratch + _ride_scratch(ride),
            compiler_params=_cparams(("arbitrary", "arbitrary", "arbitrary")),
        )(a, b, *ride[0])
        return (res[:n_out] if extra_bf16 else res[0]), list(res[n_out:])
    res = pl.pallas_call(
        body, name=name, grid=(nm, nn, nk), in_specs=[a_spec, b_spec], out_specs=out_specs,
        out_shape=out_shape, scratch_shapes=scratch,
        compiler_params=_cparams(("parallel", "parallel", "arbitrary")),
    )(a, b)
    return res if extra_bf16 else res[0]


def _rms_stats(x):
    r = lax.rsqrt(jnp.mean(x * x, axis=-1, keepdims=True) + EPS)
    return r, x * r


def _rms_bwd(x, g, dy):
    r, xh = _rms_stats(x)
    dxh = dy * g
    dx = r * (dxh - xh * jnp.mean(dxh * xh, axis=-1, keepdims=True))
    return dx, jnp.sum(dy * xh, axis=0, keepdims=True)


def _row_spec(tr, w, col=0):
    return pl.BlockSpec((tr, w), lambda i: (i, col))


def _vec_spec(w):
    return pl.BlockSpec((1, w), lambda i: (0, 0))


def _acc_rows(ref, val, i):
    @pl.when(i == 0)
    def _():
        ref[...] = val

    @pl.when(i > 0)
    def _():
        ref[...] += val


def _rms_fwd_call(x, g, name, beside=None):
    lp, w = x.shape
    tr = _pick(lp, [384, 128])

    def body(x_ref, g_ref, *rest):
        _, xh = _rms_stats(x_ref[...])
        rest[-1][...] = (xh * g_ref[...]).astype(BF16)

    if beside is None:
        return pl.pallas_call(
            body, name=name, grid=(lp // tr,), in_specs=[_row_spec(tr, w), _vec_spec(w)],
            out_specs=_row_spec(tr, w), out_shape=jax.ShapeDtypeStruct((lp, w), BF16),
            compiler_params=_cparams(("parallel",)))(x, g)
    return pl.pallas_call(
        body, name=name, grid=(lp // tr,), in_specs=[_row_spec(tr, w), _vec_spec(w), ANY],
        out_specs=_row_spec(tr, w, 1), out_shape=jax.ShapeDtypeStruct((lp, 2 * w), BF16),
        input_output_aliases={2: 0}, compiler_params=_cparams(("parallel",)))(x, g, beside)


def _mid_fwd_call(h0, mix, g_post, g_pre2):
    lp, w = h0.shape
    tr = _pick(lp, [384, 128])

    def body(h0_ref, mix_ref, gp_ref, g2_ref, h1_ref, xn_ref):
        _, mh = _rms_stats(mix_ref[...])
        h1 = h0_ref[...] + mh * gp_ref[...]
        h1_ref[...] = h1
        _, hh = _rms_stats(h1)
        xn_ref[...] = (hh * g2_ref[...]).astype(BF16)

    return pl.pallas_call(
        body, name="mid_fwd", grid=(lp // tr,),
        in_specs=[_row_spec(tr, w), _row_spec(tr, w), _vec_spec(w), _vec_spec(w)],
        out_specs=[_row_spec(tr, w), _row_spec(tr, w)],
        out_shape=[jax.ShapeDtypeStruct((lp, w), F32), jax.ShapeDtypeStruct((lp, w), BF16)],
        compiler_params=_cparams(("parallel",)))(h0, mix, g_post, g_pre2)


def _final_call(h1, f, g_post, target):
    lp, w = h1.shape
    tr = BLK
    nb = lp // tr

    def body(h1_ref, f_ref, g_ref, t_ref, loss_ref, df_ref, dh_ref, dg_ref):
        i = pl.program_id(0)
        fv = f_ref[...]
        g = g_ref[...]
        _, fh = _rms_stats(fv)
        h2 = h1_ref[...] + fh * g
        diff = jnp.where(i > 0, h2 - t_ref[...], 0.0)
        part = 0.5 * jnp.sum(diff * diff, axis=0, keepdims=True) * (1.0 / w)
        _acc_rows(loss_ref, part, i)
        dh = diff * (1.0 / w)
        dh_ref[...] = dh
        df, dg = _rms_bwd(fv, g, dh)
        df_ref[...] = df.astype(BF16)
        _acc_rows(dg_ref, dg, i)

    t_spec = pl.BlockSpec((tr, w), lambda i: (jnp.maximum(i - 1, 0), 0))
    return pl.pallas_call(
        body, name="final_fwd_bwd", grid=(nb,),
        in_specs=[_row_spec(tr, w), _row_spec(tr, w), _vec_spec(w), t_spec],
        out_specs=[_vec_spec(w), _row_spec(tr, w), _row_spec(tr, w), _vec_spec(w)],
        out_shape=[jax.ShapeDtypeStruct((1, w), F32), jax.ShapeDtypeStruct((lp, w), BF16),
                   jax.ShapeDtypeStruct((lp, w), F32), jax.ShapeDtypeStruct((1, w), F32)],
        compiler_params=_cparams(("arbitrary",)))(h1, f, g_post, target)


def _mid_bwd_call(dh2, h1, dxn2, mix, g_pre2, g_post):
    lp, w = h1.shape
    tr = _pick(lp, [384, 128])

    def body(dh2_ref, h1_ref, dxn_ref, mix_ref, g2_ref, gp_ref, dh1_ref, dmix_ref, dg2_ref, dgp_ref):
        i = pl.program_id(0)
        live = (i * tr + _iota((tr, 1), 0)) >= PAD
        dx, dg2 = _rms_bwd(h1_ref[...], g2_ref[...], dxn_ref[...])
        dh1 = jnp.where(live, dh2_ref[...] + dx, 0.0)
        dh1_ref[...] = dh1
        dmix, dgp = _rms_bwd(mix_ref[...], gp_ref[...], dh1)
        dmix_ref[...] = jnp.where(live, dmix, 0.0).astype(BF16)
        _acc_rows(dg2_ref, dg2, i)
        _acc_rows(dgp_ref, dgp, i)

    rs = _row_spec(tr, w)
    return pl.pallas_call(
        body, name="mid_bwd", grid=(lp // tr,),
        in_specs=[rs, rs, rs, rs, _vec_spec(w), _vec_spec(w)],
        out_specs=[rs, rs, _vec_spec(w), _vec_spec(w)],
        out_shape=[jax.ShapeDtypeStruct((lp, w), F32), jax.ShapeDtypeStruct((lp, w), BF16),
                   jax.ShapeDtypeStruct((1, w), F32), jax.ShapeDtypeStruct((1, w), F32)],
        compiler_params=_cparams(("arbitrary",)))(dh2, h1, dxn2, mix, g_pre2, g_post)


def _norm_bwd_call(x, g, dy_arr, dy_col, name, res=None, ride=None, land=None):
    lp, w = x.shape
    tr = _pick(lp, [384, 128])
    nsteps = lp // tr
    has_res = res is not None
    ns = len(ride[0]) if ride else 0
    first_out = (1 if has_res else 0) + (ns + 1 if ride else 0)

    def body(x_ref, g_ref, dy_ref, *rest):
        i = pl.program_id(0)
        if ride:
            srcs = rest[first_out - ns - 1:first_out - 1]
            _ride_run(ride, srcs, [rest[first_out + 2]], rest[-2], rest[-1], i == 0, i == nsteps - 1)
        live = (i * tr + _iota((tr, 1), 0)) >= PAD
        dx, dg = _rms_bwd(x_ref[...], g_ref[...], dy_ref[...])
        if has_res:
            dx = dx + rest[0][...]
        out_ref, dg_ref = rest[first_out], rest[first_out + 1]
        out_ref[...] = jnp.where(live, dx, 0.0)
        _acc_rows(dg_ref, dg, i)

    rs = _row_spec(tr, w)
    ins = [rs, _vec_spec(w), _row_spec(tr, w, dy_col)] + ([rs] if has_res else [])
    args = [x, g, dy_arr] + ([res] if has_res else [])
    outs = [rs, _vec_spec(w)]
    out_shape = [jax.ShapeDtypeStruct((lp, w), F32), jax.ShapeDtypeStruct((1, w), F32)]
    if not ride:
        return pl.pallas_call(
            body, name=name, grid=(nsteps,), in_specs=ins, out_specs=outs, out_shape=out_shape,
            compiler_params=_cparams(("arbitrary",)))(*args)
    return pl.pallas_call(
        body, name=name, grid=(nsteps,), in_specs=ins + [ANY] * (ns + 1), out_specs=outs + [ANY],
        out_shape=out_shape + [jax.ShapeDtypeStruct(land.shape, land.dtype)],
        input_output_aliases={len(args) + ns: 2}, scratch_shapes=_ride_scratch(ride),
        compiler_params=_cparams(("arbitrary",)))(*args, *ride[0], land)


def _shift_down(cur, prev_tail, s, rows):
    if s == 0:
        return cur
    prev = jnp.tile(prev_tail, (cur.shape[0] // 8, 1))
    return jnp.where(rows >= s, pltpu.roll(cur, s, 0), pltpu.roll(prev, s, 0))


def _shift_up(cur, next_head, s, rows):
    if s == 0:
        return cur
    n = cur.shape[0]
    nxt = jnp.tile(next_head, (n // 8, 1))
    return jnp.where(rows < n - s, pltpu.roll(cur, n - s, 0), pltpu.roll(nxt, n - s, 0))


def _gelu_tanh(x):
    c = math.sqrt(2.0 / math.pi)
    t = jnp.tanh(c * (x + 0.044715 * x * x * x))
    return 0.5 * x * (1.0 + t), t


def _conv_fwd_call(src, col0, width, cw, w8, b, taps, *, gate_src=None, gate_col0=0, rb=BLK, name):
    lp = src.shape[0]
    nb, nc = lp // rb, width // cw
    cb0 = col0 // cw
    ffn = gate_src is not None

    def body(x_ref, w_ref, b_ref, *rest):
        if ffn:
            u_ref, y_ref, a_ref, tail = rest
        else:
            y_ref, a_ref, tail = rest
        i = pl.program_id(1)

        @pl.when(i == 0)
        def _():
            tail[...] = jnp.zeros_like(tail)

        cur = x_ref[...]
        rows = _iota((rb, cw), 0)
        y = b_ref[...] + w_ref[taps - 1:taps, :] * cur
        pt = tail[...]
        for s in range(1, taps):
            y = y + w_ref[taps - 1 - s:taps - s, :] * _shift_down(cur, pt, s, rows)
        tail[...] = cur[rb - 8:, :]
        y_ref[...] = y
        if ffn:
            ge, _ = _gelu_tanh(y)
            a_ref[...] = (ge * u_ref[...]).astype(BF16)
        else:
            live = (i * rb + rows) >= PAD
            a_ref[...] = jnp.where(live, y * _sigmoid(y), 0.0)

    blk = lambda c0: pl.BlockSpec((rb, cw), lambda j, i: (i, c0 + j))
    ins = [blk(cb0), pl.BlockSpec((8, cw), lambda j, i: (0, j)), pl.BlockSpec((1, cw), lambda j, i: (0, j))]
    args = [src, w8, b]
    if ffn:
        ins.append(blk(gate_col0 // cw))
        args.append(gate_src)
    return pl.pallas_call(
        body, name=name, grid=(nc, nb), in_specs=ins, out_specs=[blk(0), blk(0)],
        out_shape=[jax.ShapeDtypeStruct((lp, width), F32),
                   jax.ShapeDtypeStruct((lp, width), BF16 if ffn else F32)],
        scratch_shapes=[pltpu.VMEM((8, cw), F32)],
        compiler_params=_cparams(("parallel", "arbitrary")))(*args)


def _conv_bwd_call(src, col0, width, cw, w8, taps, ypre, dact, *, gate_src=None, gate_col0=0, rb=BLK, name):
    lp = src.shape[0]
    nb, nc = lp // rb, width // cw
    cb0 = col0 // cw
    ffn = gate_src is not None

    def body(x_ref, w_ref, y_ref, d_ref, *rest):
        if ffn:
            u_ref, dx_ref, du_ref, dw_ref, db_ref, head = rest
        else:
            dx_ref, dw_ref, db_ref, head = rest
        step = pl.program_id(1)
        i = nb - 1 - step

        @pl.when(step == 0)
        def _():
            head[...] = jnp.zeros_like(head)

        rows = _iota((rb, cw), 0)
        live = (i * rb + rows) >= PAD
        y = y_ref[...]
        d = d_ref[...]
        if ffn:
            ge, t = _gelu_tanh(y)
            c = math.sqrt(2.0 / math.pi)
            dge = 0.5 * (1.0 + t) + 0.5 * y * (1.0 - t * t) * c * (1.0 + 3.0 * 0.044715 * y * y)
            u = u_ref[...]
            du_ref[...] = jnp.where(live, d * ge, 0.0).astype(BF16)
            dy = jnp.where(live, d * u * dge, 0.0)
        else:
            sg = _sigmoid(y)
            dy = jnp.where(live, d * sg * (1.0 + y * (1.0 - sg)), 0.0)
        x = x_ref[...]
        nh = head[...]
        dx = jnp.zeros_like(dy)
        dws = []
        for s in range(taps):
            sh = _shift_up(dy, nh, s, rows)
            dx = dx + w_ref[taps - 1 - s:taps - s, :] * sh
            dws.append(jnp.sum(x * sh, axis=0, keepdims=True))
        head[...] = dy[:8, :]
        dx_ref[...] = jnp.where(live, dx, 0.0).astype(BF16)
        dw = jnp.concatenate([dws[taps - 1 - k] for k in range(taps)]
                             + [jnp.zeros((8 - taps, cw), F32)], axis=0)
        _acc_rows(dw_ref, dw, step)
        _acc_rows(db_ref, jnp.sum(dy, axis=0, keepdims=True), step)

    blk = lambda c0: pl.BlockSpec((rb, cw), lambda j, s: (nb - 1 - s, c0 + j))
    ins = [blk(cb0), pl.BlockSpec((8, cw), lambda j, s: (0, j)), blk(0), blk(0)]
    args = [src, w8, ypre, dact]
    outs = [blk(0)]
    oshape = [jax.ShapeDtypeStruct((lp, width), BF16)]
    if ffn:
        ins.append(blk(gate_col0 // cw))
        args.append(gate_src)
        outs.append(blk(0))
        oshape.append(jax.ShapeDtypeStruct((lp, width), BF16))
    outs += [pl.BlockSpec((8, cw), lambda j, s: (0, j)), pl.BlockSpec((1, cw), lambda j, s: (0, j))]
    oshape += [jax.ShapeDtypeStruct((8, width), F32), jax.ShapeDtypeStruct((1, width), F32)]
    return pl.pallas_call(
        body, name=name, grid=(nc, nb), in_specs=ins, out_specs=outs, out_shape=oshape,
        scratch_shapes=[pltpu.VMEM((8, cw), F32)],
        compiler_params=_cparams(("parallel", "arbitrary")))(*args)


SB_FIRST = 3
SB_GROUP = 4
SB_DEAD = -110.0


def _sb_scores(qm_h, kb):
    z = _dot(qm_h, kb, 1, 1)
    sp = jnp.maximum(z, 0.0) + jnp.log(1.0 + jnp.exp(-jnp.abs(z)))
    return z - sp, -sp


def _dot_tri1(v, tri2):
    r = _dot(v.astype(BF16), tri2[:BLK])
    return r[:, :BLK], r[:, BLK:]


def _tri2(cond):
    t = jnp.concatenate([cond.astype(BF16), jnp.ones((BLK, BLK), BF16)], axis=1)
    return jnp.concatenate([t, t], axis=0)


def _dot_tri(v, tri2):
    hi = v.astype(BF16)
    lo = (v - hi.astype(F32)).astype(BF16)
    r = _dot(jnp.concatenate([hi, lo], axis=1), tri2)
    return r[:, :BLK], r[:, BLK:]


def _sb_fwd_call(proj, ride):
    lp = proj.shape[0]
    nb = lp // BLK
    scale = 1.0 / math.sqrt(HEAD_DIM)

    ns, nl = len(ride[0]), len(ride[1])
    npair = N_HEADS // 2

    def body(q_ref, k_ref, v_ref, *rest):
        o_ref, tl_ref = rest[ns], rest[ns + 1]
        i = pl.program_id(1)
        step = pl.program_id(0) * nb + i
        _ride_run(ride, rest[:ns], rest[ns + 2:ns + 2 + nl], rest[-2], rest[-1], step == 0, step == npair * nb - 1)
        lane = _iota((2 * BLK, BLK), 1)
        row = _iota((2 * BLK, BLK), 0)
        first = row < BLK
        qrow = row & (BLK - 1)
        q = q_ref[...] * scale
        q2 = jnp.concatenate([q, q], axis=0)
        qm = jnp.where(first == (lane < HEAD_DIM), q2, 0.0).astype(BF16)
        tri = _tri2(_iota((BLK, BLK), 0) > _iota((BLK, BLK), 1))

        def chunk(off, nsub, last_valid, carry):
            width = nsub * BLK
            sls = [slice(b * BLK, (b + 1) * BLK) for b in range(nsub)]
            kb = k_ref[pl.ds(off, width), :].astype(BF16)
            vb = v_ref[pl.ds(off, width), :].astype(BF16)
            lb, lk = _sb_scores(qm, kb)
            lks = [lk[:, sl] for sl in sls]
            first_valid = (off + lane) >= PAD
            lks[0] = jnp.where(first_valid, lks[0], 0.0)
            if last_valid is not None:
                lks[-1] = jnp.where(last_valid, lks[-1], 0.0)
            afters = [_dot_tri(lks[b], tri) for b in range(nsub)]
            run, acc = carry
            ws = [None] * nsub
            for b in reversed(range(nsub)):
                wb = jnp.exp(lb[:, sls[b]] + afters[b][0] + run)
                if b == 0:
                    wb = jnp.where(first_valid, wb, 0.0)
                if last_valid is not None and b == nsub - 1:
                    wb = jnp.where(last_valid, wb, 0.0)
                ws[b] = wb.astype(BF16)
                run = run + afters[b][1]
            w = ws[0] if nsub == 1 else jnp.concatenate(ws, axis=1)
            return run, acc + _dot(w, vb)

        before = jnp.minimum(i, SB_FIRST - 1)
        first_off = pl.multiple_of((i - before) * BLK, BLK)
        diag = lane < qrow
        zero = jnp.zeros((2 * BLK, BLK), F32)
        carry = lax.switch(before, [functools.partial(chunk, first_off, n, diag) for n in range(1, SB_FIRST + 1)],
                           (zero, zero))

        def walk(n):
            def body(state):
                off = pl.multiple_of((state[0] - (n - 1)) * BLK, BLK)
                return (state[0] - n, *chunk(off, n, None, state[1:]))

            def cond(state):
                return jnp.logical_and(state[0] >= n - 1, jnp.max(state[1]) > SB_DEAD)

            return cond, body

        state = lax.while_loop(*walk(SB_GROUP), (i - SB_FIRST, *carry))
        pos, run, acc = lax.while_loop(*walk(1), state)
        low = lane[:BLK] < HEAD_DIM
        o_ref[...] = jnp.where(low, acc[:BLK], acc[BLK:])
        tl = jnp.where(low, run[:BLK], run[BLK:])
        tl_ref[...] = jnp.where(lane[:BLK] == 1, jnp.maximum(pos + 1, 0).astype(F32), tl)

    qc, kc, vc = C_Q // BLK, C_K // BLK, C_V // BLK
    blk = pl.BlockSpec((BLK, BLK), lambda p, i: (i, p))
    res = pl.pallas_call(
        body, name="sb_fwd", grid=(npair, nb),
        in_specs=[pl.BlockSpec((BLK, BLK), lambda p, i: (i, qc + p)),
                  pl.BlockSpec((lp, BLK), lambda p, i: (0, kc + p)),
                  pl.BlockSpec((lp, BLK), lambda p, i: (0, vc + p))] + [ANY] * ns,
        out_specs=[blk, blk] + [ANY] * nl,
        out_shape=[jax.ShapeDtypeStruct((lp, N_HEADS * HEAD_DIM), F32)] * 2 + list(ride[1]),
        scratch_shapes=_ride_scratch(ride),
        compiler_params=_cparams(("arbitrary", "arbitrary")))(proj, proj, proj, *ride[0])
    return res[0], res[1], list(res[2:])


def _sb_bwd_call(proj, tl, do, ride):
    lp = proj.shape[0]
    nb = lp // BLK
    scale = 1.0 / math.sqrt(HEAD_DIM)

    ns, nl = len(ride[0]), len(ride[1])
    npair = N_HEADS // 2

    def body(q_ref, k_ref, v_ref, tl_ref, do_ref, *rest):
        dq_ref, dk_ref, dv_ref = rest[ns:ns + 3]
        dk_acc, dv_acc = rest[ns + 3 + nl:ns + 5 + nl]
        i = pl.program_id(1)
        step = pl.program_id(0) * nb + i
        _ride_run(ride, rest[:ns], rest[ns + 3:ns + 3 + nl], rest[-2], rest[-1], step == 0, step == npair * nb - 1)

        @pl.when(i == 0)
        def _():
            dk_acc[...] = jnp.zeros_like(dk_acc)
            dv_acc[...] = jnp.zeros_like(dv_acc)

        lane = _iota((2 * BLK, BLK), 1)
        row = _iota((2 * BLK, BLK), 0)
        qrow = row & (BLK - 1)
        mine = (row < BLK) == (lane < HEAD_DIM)
        q = q_ref[...] * scale
        dov = do_ref[...]
        qm = jnp.where(mine, jnp.concatenate([q, q], axis=0), 0.0).astype(BF16)
        dom = jnp.where(mine, jnp.concatenate([dov, dov], axis=0), 0.0).astype(BF16)
        tlv = tl_ref[...]
        tot = jnp.concatenate([jnp.broadcast_to(tlv[:, 0:1], (BLK, BLK)),
                               jnp.broadcast_to(tlv[:, HEAD_DIM:HEAD_DIM + 1], (BLK, BLK))], axis=0)
        r1, l1 = _iota((BLK, BLK), 0), _iota((BLK, BLK), 1)
        tri_in = _tri2(r1 <= l1)
        tri_ex = _tri2(r1 < l1)

        def chunk(off, nsub, last_valid, carry):
            width = nsub * BLK
            sls = [slice(b * BLK, (b + 1) * BLK) for b in range(nsub)]
            cat = lambda parts: parts[0] if nsub == 1 else jnp.concatenate(parts, axis=1)
            mask_last = lambda b: last_valid is not None and b == nsub - 1
            kb = k_ref[pl.ds(off, width), :].astype(BF16)
            vb = v_ref[pl.ds(off, width), :].astype(BF16)
            lb, lk = _sb_scores(qm, kb)
            dw = _dot(dom, vb, 1, 1)
            lks = [lk[:, sl] for sl in sls]
            first_valid = (off + lane) >= PAD
            lks[0] = jnp.where(first_valid, lks[0], 0.0)
            if last_valid is not None:
                lks[-1] = jnp.where(last_valid, lks[-1], 0.0)
            pins = [_dot_tri(lks[b], tri_in) for b in range(nsub)]
            run, gsum, dq = carry
            ws, gs = [], []
            for b in range(nsub):
                wb = jnp.exp(lb[:, sls[b]] + (tot - run - pins[b][0]))
                if b == 0:
                    wb = jnp.where(first_valid, wb, 0.0)
                if mask_last(b):
                    wb = jnp.where(last_valid, wb, 0.0)
                ws.append(wb.astype(BF16))
                gs.append(wb * dw[:, sls[b]])
                run = run + pins[b][1]
            gexs = [_dot_tri1(gs[b], tri_ex) for b in range(nsub)]
            beta = jnp.exp(lb)
            parts = []
            for b in range(nsub):
                bt = beta[:, sls[b]]
                dzb = gs[b] * (1.0 - bt) - (gsum + gexs[b][0]) * bt
                if b == 0:
                    dzb = jnp.where(first_valid, dzb, 0.0)
                if mask_last(b):
                    dzb = jnp.where(last_valid, dzb, 0.0)
                parts.append(dzb.astype(BF16))
                gsum = gsum + gexs[b][1]
            dz, w = cat(parts), cat(ws)
            dk_acc[pl.ds(off, width), :] += _dot(dz, qm, 0, 0)
            dv_acc[pl.ds(off, width), :] += _dot(w, dom, 0, 0)
            return run, gsum, dq + _dot(dz, kb)

        diag = lane < qrow
        zero = jnp.zeros((2 * BLK, BLK), F32)
        top = i - SB_FIRST

        def walk(n):
            def body(state):
                off = pl.multiple_of(state[0] * BLK, BLK)
                return (state[0] + n, *chunk(off, n, None, state[1:]))

            return (lambda state: state[0] + (n - 1) <= top), body

        state = (jnp.max(tlv[:, 1:2]).astype(jnp.int32), zero, zero, zero)
        state = lax.while_loop(*walk(SB_GROUP), state)
        carry = lax.while_loop(*walk(1), state)[1:]
        before = jnp.minimum(i, SB_FIRST - 1)
        first_off = pl.multiple_of((i - before) * BLK, BLK)
        dq = lax.switch(before, [functools.partial(chunk, first_off, n, diag) for n in range(1, SB_FIRST + 1)],
                        carry)[2]
        dq_ref[...] = (jnp.where(lane[:BLK] < HEAD_DIM, dq[:BLK], dq[BLK:]) * scale).astype(BF16)

        @pl.when(i == nb - 1)
        def _():
            dk_ref[...] = dk_acc[...].astype(BF16)
            dv_ref[...] = dv_acc[...].astype(BF16)

    qc, kc, vc = C_Q // BLK, C_K // BLK, C_V // BLK
    blk = pl.BlockSpec((BLK, BLK), lambda p, i: (i, p))
    full = pl.BlockSpec((lp, BLK), lambda p, i: (0, p))
    w = N_HEADS * HEAD_DIM
    res = pl.pallas_call(
        body, name="sb_bwd", grid=(npair, nb),
        in_specs=[pl.BlockSpec((BLK, BLK), lambda p, i: (i, qc + p)),
                  pl.BlockSpec((lp, BLK), lambda p, i: (0, kc + p)),
                  pl.BlockSpec((lp, BLK), lambda p, i: (0, vc + p)),
                  blk, blk] + [ANY] * ns,
        out_specs=[blk, full, full] + [ANY] * nl,
        out_shape=[jax.ShapeDtypeStruct((lp, w), BF16)] * 3 + list(ride[1]),
        scratch_shapes=[pltpu.VMEM((lp, BLK), F32), pltpu.VMEM((lp, BLK), F32)] + _ride_scratch(ride),
        compiler_params=_cparams(("arbitrary", "arbitrary")))(proj, proj, proj, tl, do, *ride[0])
    return res[0], res[1], res[2], list(res[3:])


def _log1p(e):
    u = 1.0 + e
    return jnp.where(u == 1.0, e, jnp.log(u) * e / jnp.where(u == 1.0, 1.0, u - 1.0))


def _ssd_common(c, dtr, bias, alog):
    row = _iota((BLK, BLK), 0)
    lane = _iota((BLK, BLK), 1)
    live = ((c * BLK + row) >= PAD) & (lane < N_HEADS)
    pre = dtr + bias
    dt = jnp.where(live, jnp.maximum(pre, 0.0) + _log1p(jnp.exp(-jnp.abs(pre))), 0.0)
    a_neg = -jnp.exp(alog)
    a = dt * a_neg
    t_in = (lane <= row).astype(BF16)
    cs = _dot_sel_l(t_in, a)
    cs_t = cs.T
    cs_end = cs[BLK - 1:BLK, :]
    e = jnp.exp(cs)
    f = jnp.exp(cs_end - cs)
    xp = ((_iota((BLK, SSD_INNER), 1) // HEAD_DIM) == _iota((BLK, SSD_INNER), 0)).astype(BF16)
    xp_t = ((_iota((SSD_INNER, BLK), 0) // HEAD_DIM) == _iota((SSD_INNER, BLK), 1)).astype(BF16)
    decay_col = _dot_sel_l(xp_t, jnp.exp(cs_t))[:, BLK - 1:BLK]
    return dict(live=live, pre=pre, dt=dt, a_neg=a_neg, cs=cs, cs_t=cs_t, e=e, f=f, xp=xp, xp_t=xp_t,
                decay_col=decay_col, row=row, lane=lane,
                dt_x=_dot_sel_r(dt, xp), e_x=_dot_sel_r(e, xp), f_x=_dot_sel_r(f, xp))


def _ssd_ldec(q, h):
    diff = q["cs"][:, h:h + 1] - q["cs_t"][h:h + 1, :]
    causal = q["row"] >= q["lane"]
    return jnp.where(causal, jnp.exp(jnp.where(causal, diff, 0.0)), 0.0)


def _ssd_fwd_call(xbc, proj, bias, alog, d_x, norm_g):
    lp = xbc.shape[0]
    nb = lp // BLK
    gw = SSD_INNER // SSD_GROUPS
    ppg = gw // BLK

    def body(xbc_ref, dtr_ref, z_ref, bias_ref, alog_ref, dx_ref, ng_ref, yb_ref, ypre_ref, sprev_ref, s_ref):
        c = pl.program_id(0)

        @pl.when(c == 0)
        def _():
            s_ref[...] = jnp.zeros_like(s_ref)

        q = _ssd_common(c, dtr_ref[...], bias_ref[...], alog_ref[...])
        x = xbc_ref[:, 0:SSD_INNER]
        xd = x * q["dt_x"]
        low = q["lane"] < HEAD_DIM
        s_old = s_ref[...]
        sprev_ref[...] = s_old
        xdf = (xd * q["f_x"]).astype(BF16)
        for g in range(SSD_GROUPS):
            bg = xbc_ref[:, SSD_INNER + g * SSD_STATE:SSD_INNER + (g + 1) * SSD_STATE].astype(BF16)
            cg = xbc_ref[:, SSD_INNER + (SSD_GROUPS + g) * SSD_STATE:
                         SSD_INNER + (SSD_GROUPS + g + 1) * SSD_STATE].astype(BF16)
            cb = _dot(cg, bg, 1, 1)
            gs = slice(g * gw, (g + 1) * gw)
            y_off = _dot(cg, s_old[gs, :].astype(BF16), 1, 1) * q["e_x"][:, gs]
            s_ref[gs, :] = s_old[gs, :] * q["decay_col"][gs, :] + _dot(xdf[:, gs], bg, 0, 0)
            for pr in range(ppg):
                cols = slice(g * gw + pr * BLK, g * gw + (pr + 1) * BLK)
                xd_p = xd[:, cols]
                acc = y_off[:, pr * BLK:(pr + 1) * BLK]
                for hh in range(2):
                    h = (g * gw + pr * BLK) // HEAD_DIM + hh
                    m = (cb * _ssd_ldec(q, h)).astype(BF16)
                    xm = jnp.where(low, xd_p, 0.0) if hh == 0 else jnp.where(low, 0.0, xd_p)
                    acc = acc + _dot(m, xm.astype(BF16))
                ypre_ref[:, cols] = acc
        ypre = ypre_ref[...] + x * dx_ref[...]
        ypre_ref[...] = ypre
        z = z_ref[...]
        yg = ypre * (z * _sigmoid(z))
        _, yh = _rms_stats(yg)
        yb_ref[...] = (yh * ng_ref[...]).astype(BF16)

    row = lambda w, col: pl.BlockSpec((BLK, w), lambda c: (c, col))
    vec = lambda w: pl.BlockSpec((1, w), lambda c: (0, 0))
    return pl.pallas_call(
        body, name="ssd_fwd", grid=(nb,),
        in_specs=[row(XBC, 0), row(BLK, C_DT // BLK), row(SSD_INNER, 0), vec(BLK), vec(BLK),
                  vec(SSD_INNER), vec(SSD_INNER)],
        out_specs=[row(SSD_INNER, 0), row(SSD_INNER, 0),
                   pl.BlockSpec((None, SSD_INNER, SSD_STATE), lambda c: (c, 0, 0))],
        out_shape=[jax.ShapeDtypeStruct((lp, 2 * SSD_INNER), BF16), jax.ShapeDtypeStruct((lp, SSD_INNER), F32),
                   jax.ShapeDtypeStruct((nb, SSD_INNER, SSD_STATE), F32)],
        scratch_shapes=[pltpu.VMEM((SSD_INNER, SSD_STATE), F32)],
        compiler_params=_cparams(("arbitrary",)))(xbc, proj, proj, bias, alog, d_x, norm_g)


def _ssd_bwd_call(dycat, ypre, xbc, proj, sprev, bias, alog, d_x, norm_g):
    lp = xbc.shape[0]
    nb = lp // BLK
    gw = SSD_INNER // SSD_GROUPS
    ppg = gw // BLK

    def body(dy_ref, ypre_ref, xbc_ref, dtr_ref, z_ref, sp_ref, bias_ref, alog_ref, dxp_ref, ng_ref,
             dz_ref, dxbc_ref, ddt_ref, dng_ref, dd_ref, dal_ref, dbi_ref, ds_ref, dxd_ref):
        step = pl.program_id(0)
        c = nb - 1 - step

        @pl.when(step == 0)
        def _():
            ds_ref[...] = jnp.zeros_like(ds_ref)

        q = _ssd_common(c, dtr_ref[...], bias_ref[...], alog_ref[...])
        row, lane = q["row"], q["lane"]
        low = lane < HEAD_DIM
        rowlive = ((c * BLK + _iota((BLK, 1), 0)) >= PAD)
        x = xbc_ref[:, 0:SSD_INNER]
        xd = x * q["dt_x"]
        z = z_ref[...]
        sz = _sigmoid(z)
        silu = z * sz
        ypre = ypre_ref[...]
        dyg, dng = _rms_bwd(ypre * silu, ng_ref[...], dy_ref[...])
        _acc_rows(dng_ref, dng, step)
        dyp = dyg * silu
        dz_ref[...] = jnp.where(rowlive, dyg * ypre * (sz * (1.0 + z * (1.0 - sz))), 0.0).astype(BF16)
        _acc_rows(dd_ref, jnp.sum(dyp * x, axis=0, keepdims=True), step)
        dye = dyp * q["e_x"]
        xdf = xd * q["f_x"]
        s_prev = sp_ref[...]
        ds_old = ds_ref[...]
        qrow = jnp.zeros((BLK, BLK), F32)
        qcol_t = jnp.zeros((BLK, BLK), F32)
        red_e = []
        red_f = []
        for g in range(SSD_GROUPS):
            gs = slice(g * gw, (g + 1) * gw)
            bsl = slice(SSD_INNER + g * SSD_STATE, SSD_INNER + (g + 1) * SSD_STATE)
            csl = slice(SSD_INNER + (SSD_GROUPS + g) * SSD_STATE, SSD_INNER + (SSD_GROUPS + g + 1) * SSD_STATE)
            bg = xbc_ref[:, bsl].astype(BF16)
            cg = xbc_ref[:, csl].astype(BF16)
            sg = s_prev[gs, :].astype(BF16)
            dsg = ds_old[gs, :].astype(BF16)
            cb = _dot(cg, bg, 1, 1)
            bds = _dot(bg, dsg, 1, 1)
            y_off = _dot(cg, sg, 1, 1) * q["e_x"][:, gs]
            red_e.append(dyp[:, gs] * y_off)
            red_f.append(xd[:, gs] * bds * q["f_x"][:, gs])
            dc = _dot(dye[:, gs].astype(BF16), sg)
            db = _dot(xdf[:, gs].astype(BF16), dsg)
            ds_ref[gs, :] = ds_old[gs, :] * q["decay_col"][gs, :] + _dot(dye[:, gs].astype(BF16), cg, 0, 0)
            dcb = jnp.zeros((BLK, BLK), F32)
            for pr in range(ppg):
                cols = slice(g * gw + pr * BLK, g * gw + (pr + 1) * BLK)
                xd_p = xd[:, cols].astype(BF16)
                dy_p = dyp[:, cols]
                acc = q["f_x"][:, cols] * bds[:, pr * BLK:(pr + 1) * BLK]
                for hh in range(2):
                    h = (g * gw + pr * BLK) // HEAD_DIM + hh
                    ld = _ssd_ldec(q, h)
                    m = cb * ld
                    dym = (jnp.where(low, dy_p, 0.0) if hh == 0 else jnp.where(low, 0.0, dy_p)).astype(BF16)
                    dm = jnp.where(row >= lane, _dot(dym, xd_p, 1, 1), 0.0)
                    acc = acc + _dot(m.astype(BF16), dym, 0, 0)
                    dcb = dcb + dm * ld
                    qq = dm * m
                    qrow = qrow + jnp.where(lane == h, jnp.sum(qq, axis=1, keepdims=True), 0.0)
                    qcol_t = qcol_t + jnp.where(row == h, jnp.sum(qq, axis=0, keepdims=True), 0.0)
                dxd_ref[:, cols] = acc
            dcbb = dcb.astype(BF16)
            dxbc_ref[:, bsl] = jnp.where(rowlive, db + _dot(dcbb, cg, 0, 0), 0.0)
            dxbc_ref[:, csl] = jnp.where(rowlive, dc + _dot(dcbb, bg), 0.0)
        dxd = dxd_ref[...]
        dxbc_ref[:, 0:SSD_INNER] = jnp.where(rowlive, dxd * q["dt_x"] + dyp * dxp_ref[...], 0.0)
        xp_t = q["xp_t"]
        fw = _dot_sel_r(jnp.concatenate(red_f, axis=1), xp_t)
        dcs = qrow - qcol_t.T + _dot_sel_r(jnp.concatenate(red_e, axis=1), xp_t) - fw
        end_f = jnp.sum(fw, axis=0, keepdims=True)
        sds = jnp.sum(ds_old * s_prev, axis=1, keepdims=True)
        per_head = _dot_sel_l(q["xp"], jnp.broadcast_to(sds, (SSD_INNER, BLK)))
        end_e = per_head.T[0:1, :] * jnp.exp(q["cs"][BLK - 1:BLK, :])
        dcs = dcs + jnp.where(row == BLK - 1, end_f + end_e, 0.0)
        t_up = (lane >= row).astype(BF16)
        da = _dot_sel_l(t_up, dcs)
        ddt = da * q["a_neg"] + _dot_sel_r(dxd * x, xp_t)
        _acc_rows(dal_ref, jnp.sum(da * q["dt"] * q["a_neg"], axis=0, keepdims=True), step)
        ddtr = jnp.where(q["live"], ddt * _sigmoid(q["pre"]), 0.0)
        ddt_ref[...] = ddtr.astype(BF16)
        _acc_rows(dbi_ref, jnp.sum(ddtr, axis=0, keepdims=True), step)

    row_s = lambda w, col: pl.BlockSpec((BLK, w), lambda s: (nb - 1 - s, col))
    vec = lambda w: pl.BlockSpec((1, w), lambda s: (0, 0))
    return pl.pallas_call(
        body, name="ssd_bwd", grid=(nb,),
        in_specs=[row_s(SSD_INNER, 0), row_s(SSD_INNER, 0), row_s(XBC, 0), row_s(BLK, C_DT // BLK),
                  row_s(SSD_INNER, 0), pl.BlockSpec((None, SSD_INNER, SSD_STATE), lambda s: (nb - 1 - s, 0, 0)),
                  vec(BLK), vec(BLK), vec(SSD_INNER), vec(SSD_INNER)],
        out_specs=[row_s(SSD_INNER, 0), row_s(XBC, 0), row_s(BLK, 0),
                   vec(SSD_INNER), vec(SSD_INNER), vec(BLK), vec(BLK)],
        out_shape=[jax.ShapeDtypeStruct((lp, SSD_INNER), BF16), jax.ShapeDtypeStruct((lp, XBC), F32),
                   jax.ShapeDtypeStruct((lp, BLK), BF16),
                   jax.ShapeDtypeStruct((1, SSD_INNER), F32), jax.ShapeDtypeStruct((1, SSD_INNER), F32),
                   jax.ShapeDtypeStruct((1, BLK), F32), jax.ShapeDtypeStruct((1, BLK), F32)],
        scratch_shapes=[pltpu.VMEM((SSD_INNER, SSD_STATE), F32), pltpu.VMEM((BLK, SSD_INNER), F32)],
        compiler_params=_cparams(("arbitrary",)))(dycat, ypre, xbc, proj, proj, sprev, bias, alog, d_x, norm_g)


def _pad_rows8(w):
    return jnp.pad(w, ((0, 8 - w.shape[0]), (0, 0)))


def _pad_lanes(v, n=BLK):
    return jnp.pad(v, ((0, 0), (0, n - v.shape[1])))


def _local_step(x, target, wt, late_shards, late_weights, w_in_shards):
    seq = x.shape[0]
    lp = seq + BLK
    tm = _pick(lp, [1408, 768, 384, 128])
    tkr = _pick(lp, [1408, 384, 128])
    rbc = _pick(lp, [384, 128])
    h0 = jnp.concatenate([jnp.zeros((PAD, D_MODEL), F32), wt["meta"], x], axis=0)
    bias = _pad_lanes(wt["ssd_dt_bias"])
    alog = _pad_lanes(wt["ssd_a_log"])
    d_x = jnp.repeat(wt["ssd_d"], HEAD_DIM, axis=1)
    cw8 = _pad_rows8(wt["ssd_conv_w"])
    fw8 = _pad_rows8(wt["ffn_conv_w"])
    fcw = D_FF // 2

    xn1 = _rms_fwd_call(h0, wt["mix_pre_g"], "norm1")
    proj, late_a = _mm(xn1, wt["w_in"], tm=tm, tn=1920, tk=D_MODEL, ride=_gather_ride(late_shards[0:1]),
                       name="mm_proj")
    proj = proj[0]
    conv_pre, xbc = _conv_fwd_call(proj, C_XBC, XBC, 512, cw8, wt["ssd_conv_b"], 4, rb=rbc, name="ssd_conv_fwd")
    y_ssd, ypre, sprev = _ssd_fwd_call(xbc, proj, bias, alog, d_x, wt["ssd_norm_g"])
    o, tl, late_b = _sb_fwd_call(proj, _gather_ride(late_shards[1:3]))
    ycat = _rms_fwd_call(o, wt["sb_norm_g"], "sb_norm", beside=y_ssd)
    w_out, w_up, w_down = late_weights([late_a[0], late_b[0], late_b[1]])
    mix = _mm(ycat, w_out, tm=tm, tn=1024, tk=2048, name="mm_mix")[0]
    h1, xn2 = _mid_fwd_call(h0, mix, wt["mix_post_g"], wt["ffn_pre_g"])
    gu = _mm(xn2, w_up, tm=tm, tn=1408, tk=D_MODEL, name="mm_up")[0]
    gpre, act = _conv_fwd_call(gu, 0, D_FF, fcw, fw8, wt["ffn_conv_b"], 3, gate_src=gu, gate_col0=D_FF,
                               rb=rbc, name="ffn_conv_fwd")
    f = _mm(act, w_down, tm=tm, tn=1024, tk=1408, name="mm_down")[0]
    loss_row, df, dh2, dg_ffn_post = _final_call(h1, f, wt["ffn_post_g"], target)

    dact = _mm(df, w_down, tb=True, tm=tm, tn=1408, tk=D_MODEL, name="mm_dact")[0]
    dw_down, dw_down_b = _mm(act, df, ta=True, tm=1408, tn=1024, tk=tkr, extra_bf16=True, name="mm_dw_down")
    by_chip = lambda g: g.reshape(N_CHIPS, -1, D_MODEL)
    dgate, dup, dfcw, dfcb = _conv_bwd_call(gu, 0, D_FF, fcw, fw8, 3, gpre, dact, gate_src=gu, gate_col0=D_FF,
                                            rb=rbc, name="ffn_conv_bwd")
    dgu = jnp.concatenate([dgate, dup], axis=1)
    dxn2, land_down = _mm(dgu, w_up, tb=True, tm=tm, tn=1024, tk=1408, ride=_scatter_ride(by_chip(dw_down_b)),
                          name="mm_dxn2")
    dw_up, dw_up_b = _mm(xn2, dgu, ta=True, tm=1024, tn=1408, tk=tkr, nsplit=N_CHIPS, extra_bf16=True,
                         name="mm_dw_up")
    dh1, dmix, dg_ffn_pre, dg_mix_post = _mid_bwd_call(dh2, h1, dxn2[0], mix, wt["ffn_pre_g"], wt["mix_post_g"])
    dycat = _mm(dmix, w_out, tb=True, tm=tm, tn=1024, tk=D_MODEL, name="mm_dycat")[0]
    dw_out, dw_out_b = _mm(ycat, dmix, ta=True, tm=1024, tn=1024, tk=tkr, extra_bf16=True, name="mm_dw_out")
    do, dg_sb = _norm_bwd_call(o, wt["sb_norm_g"], dycat, 1, "sb_norm_bwd")
    dq, dk, dv, lands = _sb_bwd_call(proj, tl, do, _join_rides(_scatter_ride(dw_up_b),
                                                                  _scatter_ride(by_chip(dw_out_b))))
    dz, dxbc_act, ddt, dg_ssd, dd_x, dalog, dbias = _ssd_bwd_call(
        dycat, ypre, xbc, proj, sprev, bias, alog, d_x, wt["ssd_norm_g"])
    dxbc, dcw, dcb = _conv_bwd_call(proj, C_XBC, XBC, 512, cw8, 4, conv_pre, dxbc_act, rb=rbc,
                                    name="ssd_conv_bwd")
    dproj = jnp.concatenate([dz, dxbc, ddt, dq, dk, dv], axis=1)
    dw_in, dw_in_b = w_in_shards(_mm(xn1, dproj, ta=True, tm=1024, tn=1920, tk=tkr, name="mm_dw_in")[0])
    half = dw_in_b.shape[1] // 2
    part = half * 5 // 8
    dxn1, land_in = _mm(dproj, wt["w_in"], tb=True, tm=tm, tn=1024, tk=1920, ride=_scatter_ride(dw_in_b, 0, part),
                        name="mm_dxn1")
    dh0, dg_pre, land_in = _norm_bwd_call(h0, wt["mix_pre_g"], dxn1[0], 0, "norm1_bwd", res=dh1,
                                          ride=_scatter_ride(dw_in_b, part, half - part), land=land_in[0])

    small = {
        "meta_tokens": dh0[PAD:BLK], "mix_pre_g": dg_pre, "ssd_conv_w": dcw[:4], "ssd_conv_b": dcb,
        "ssd_dt_bias": dbias[:, :N_HEADS], "ssd_a_log": dalog[:, :N_HEADS],
        "ssd_d": jnp.sum(dd_x.reshape(N_HEADS, HEAD_DIM), axis=1)[None],
        "ssd_norm_g": dg_ssd, "sb_norm_g": dg_sb, "mix_post_g": dg_mix_post, "ffn_pre_g": dg_ffn_pre,
        "ffn_conv_w": dfcw[:3], "ffn_conv_b": dfcb, "ffn_post_g": dg_ffn_post,
    }
    pending = {"w_in": (dw_in, land_in), "w_out": (by_chip(dw_out), lands[1]), "w_up": (dw_up, lands[0]),
               "w_down": (by_chip(dw_down), land_down[0])}
    return loss_row, dh0[BLK:], small, pending


def _adamw_call(w, g, m, v, name):
    rows, cols = w.shape
    tr = 256 if rows % 256 == 0 else (352 if rows % 352 == 0 else rows)
    c1 = 1.0 - ADAM_B1 ** ADAM_STEP
    c2 = 1.0 - ADAM_B2 ** ADAM_STEP

    def body(w_ref, g_ref, m_ref, v_ref, d_ref, mo_ref, vo_ref):
        gv = g_ref[...]
        m2 = ADAM_B1 * m_ref[...] + (1.0 - ADAM_B1) * gv
        v2 = ADAM_B2 * v_ref[...] + (1.0 - ADAM_B2) * (gv * gv)
        d_ref[...] = -ADAM_LR * ((m2 / c1) / (jnp.sqrt(v2 / c2) + ADAM_EPS) + ADAM_WD * w_ref[...])
        mo_ref[...] = m2
        vo_ref[...] = v2

    spec = pl.BlockSpec((tr, cols), lambda i: (i, 0))
    return pl.pallas_call(
        body, name=name, grid=(rows // tr,), in_specs=[spec] * 4, out_specs=[spec] * 3,
        out_shape=[jax.ShapeDtypeStruct((rows, cols), F32)] * 3,
        compiler_params=_cparams(("parallel",)))(w, g, m, v)


ANY = pl.BlockSpec(memory_space=pl.ANY)


def _place():
    x, y, c = lax.axis_index("x"), lax.axis_index("y"), lax.axis_index("c")
    chips = [(1 - x, y), (x, 1 - y), (1 - x, 1 - y)]
    return x, y, c, chips


def _half(c, h):
    return pl.ds(pl.multiple_of(c * h, 8), h)


def _allgather_call(shards):
    n = len(shards)

    def body(*refs):
        ins, outs = refs[:n], refs[n:2 * n]
        send_i, recv_i, send_d, recv_d = refs[2 * n:]
        x, y, c, chips = _place()
        me = 2 * x + y
        sends = []
        for a in range(n):
            h = shards[a].shape[0] // 2
            for j, chip in enumerate(chips):
                cp = pltpu.make_async_remote_copy(
                    src_ref=ins[a].at[_half(c, h)], dst_ref=outs[a].at[me, _half(c, h)],
                    send_sem=send_i.at[3 * a + j], recv_sem=recv_i.at[3 * a + j],
                    device_id=(*chip, c), device_id_type=MESH)
                cp.start()
                sends.append(cp)
        for a in range(n):
            h = shards[a].shape[0] // 2
            for j, chip in enumerate(chips):
                src = 2 * chip[0] + chip[1]
                landed = outs[a].at[src, _half(c, h)]
                pltpu.make_async_remote_copy(
                    src_ref=landed, dst_ref=landed, send_sem=send_i.at[3 * a + j], recv_sem=recv_i.at[3 * a + j],
                    device_id=(*chip, c), device_id_type=MESH).wait_recv()
                cp = pltpu.make_async_remote_copy(
                    src_ref=landed, dst_ref=landed, send_sem=send_d.at[3 * a + j], recv_sem=recv_d.at[3 * a + j],
                    device_id=(x, y, 1 - c), device_id_type=MESH)
                cp.start()
                sends.append(cp)
        for a in range(n):
            h = shards[a].shape[0] // 2
            for j, chip in enumerate(chips):
                src = 2 * chip[0] + chip[1]
                other = outs[a].at[src, _half(1 - c, h)]
                pltpu.make_async_remote_copy(
                    src_ref=other, dst_ref=other, send_sem=send_d.at[3 * a + j], recv_sem=recv_d.at[3 * a + j],
                    device_id=(x, y, 1 - c), device_id_type=MESH).wait_recv()
        for cp in sends:
            cp.wait_send()

    return pl.pallas_call(
        body, name="allgather_weights", in_specs=[ANY] * n, out_specs=[ANY] * n,
        out_shape=[jax.ShapeDtypeStruct((N_CHIPS,) + s.shape, s.dtype) for s in shards],
        scratch_shapes=[pltpu.SemaphoreType.DMA((3 * n,))] * 4,
    )(*shards)


def _ride_scratch(ride):
    return [pltpu.SemaphoreType.DMA((ride[3],)), pltpu.SemaphoreType.DMA((ride[3],))]


def _ride_run(ride, src_refs, land_refs, send, recv, first, last):
    plan = ride[2]

    @pl.when(first)
    def _():
        for k, (src, dst, _, dev) in enumerate(plan(src_refs, land_refs)):
            pltpu.make_async_remote_copy(src_ref=src, dst_ref=dst, send_sem=send.at[k], recv_sem=recv.at[k],
                                         device_id=dev, device_id_type=MESH).start()

    @pl.when(last)
    def _():
        for k, (src, _, land, dev) in enumerate(plan(src_refs, land_refs)):
            cp = pltpu.make_async_remote_copy(src_ref=src, dst_ref=land, send_sem=send.at[k], recv_sem=recv.at[k],
                                              device_id=dev, device_id_type=MESH)
            cp.wait_send()
            cp.wait_recv()


def _join_rides(r1, r2):
    n1, l1 = len(r1[0]), len(r1[1])

    def plan(srcs, lands):
        return r1[2](srcs[:n1], lands[:l1]) + r2[2](srcs[n1:], lands[l1:])

    return (r1[0] + r2[0], r1[1] + r2[1], plan, r1[3] + r2[3])


def _gather_ride(shards):
    return (list(shards), [jax.ShapeDtypeStruct((N_CHIPS,) + s.shape, s.dtype) for s in shards],
            _gather_plan(len(shards)), 3 * len(shards))


def _scatter_ride(g_b, r0=0, nr=None):
    h = g_b.shape[1] // 2
    return ([g_b], [jax.ShapeDtypeStruct((8, h, g_b.shape[2]), BF16)], _scatter_plan(h, r0, h if nr is None else nr), 7)


def _gather_plan(n):
    def plan(srcs, lands):
        x, y, c, chips = _place()
        me = 2 * x + y
        return [(srcs[a], lands[a].at[me], lands[a].at[2 * chip[0] + chip[1]], (*chip, c))
                for a in range(n) for chip in chips]
    return plan


def _scatter_plan(h, r0, nr):
    def plan(srcs, lands):
        x, y, c, _ = _place()
        me = 4 * x + 2 * y + c
        out = []
        for p in range(1, 8):
            px = 1 - x if p & 4 else x
            py = 1 - y if p & 2 else y
            pc = 1 - c if p & 1 else c
            rows = pl.ds(pl.multiple_of(pc * h + r0, 8), nr)
            out.append((srcs[0].at[2 * px + py, rows], lands[0].at[me, pl.ds(r0, nr)],
                        lands[0].at[4 * px + 2 * py + pc, pl.ds(r0, nr)], (px, py, pc)))
        return out
    return plan


def _grad_sum_call(own, land, place, name):
    _, h, cols = land.shape
    th = _pick(h, [256, 176, 8])
    nt = h // th

    def body(p_ref, own_ref, *refs):
        acc = own_ref[...]
        for r in refs[:7]:
            acc = acc + r[...].astype(F32)
        refs[7][...] = acc

    def peer(k):
        return pl.BlockSpec((None, th, cols), lambda i, p_ref: (p_ref[2 + k], i, 0))

    return pl.pallas_call(
        body, name=name,
        grid_spec=pltpu.PrefetchScalarGridSpec(
            num_scalar_prefetch=1, grid=(nt,),
            in_specs=[pl.BlockSpec((None, th, cols), lambda i, p_ref: (p_ref[1], p_ref[0] * nt + i, 0))]
            + [peer(k) for k in range(7)],
            out_specs=pl.BlockSpec((th, cols), lambda i, p_ref: (p_ref[0] * nt + i, 0))),
        out_shape=jax.ShapeDtypeStruct((2 * h, cols), F32),
        compiler_params=_cparams(("parallel",)))(place, own, *[land] * 7)


def _half_exchange_call(shards):
    n = len(shards)

    def body(*refs):
        outs = refs[n:2 * n]
        send_d, recv_d = refs[2 * n:]
        x, y, c, _ = _place()
        cps = []
        for a in range(n):
            h = shards[a].shape[0] // 2
            mine = outs[a].at[_half(c, h)]
            cp = pltpu.make_async_remote_copy(
                src_ref=mine, dst_ref=mine, send_sem=send_d.at[a], recv_sem=recv_d.at[a],
                device_id=(x, y, 1 - c), device_id_type=MESH)
            cp.start()
            cps.append(cp)
        for a, cp in enumerate(cps):
            h = shards[a].shape[0] // 2
            theirs = outs[a].at[_half(1 - c, h)]
            pltpu.make_async_remote_copy(
                src_ref=theirs, dst_ref=theirs, send_sem=send_d.at[a], recv_sem=recv_d.at[a],
                device_id=(x, y, 1 - c), device_id_type=MESH).wait_recv()
            cp.wait_send()

    return pl.pallas_call(
        body, name="grad_half_exchange", in_specs=[ANY] * n, out_specs=[ANY] * n,
        out_shape=[jax.ShapeDtypeStruct(sv.shape, F32) for sv in shards],
        input_output_aliases={a: a for a in range(n)},
        scratch_shapes=[pltpu.SemaphoreType.DMA((n,))] * 2,
    )(*shards)


def _allreduce_small_call(arrs):
    n = len(arrs)
    offs, rows = [], 0
    for a in arrs:
        offs.append(rows)
        rows += a.shape[0]
    rows = -(-rows // 8) * 8
    width = -(-max(a.shape[1] for a in arrs) // BLK) * BLK

    def body(*refs):
        ins, outs = refs[:n], refs[n:2 * n]
        gath, send_sems, recv_sems = refs[2 * n:]
        x, y, c, chips = _place()
        me, sibling = (x, y, c), (x, y, 1 - c)

        def slot(px, py, pc):
            return gath.at[4 * px + 2 * py + pc]

        def copy(k, block, to):
            return pltpu.make_async_remote_copy(
                src_ref=slot(*block), dst_ref=slot(*block),
                send_sem=send_sems.at[k], recv_sem=recv_sems.at[k], device_id=to, device_id_type=MESH)

        mine = slot(*me)
        mine[...] = jnp.zeros((rows, width), F32)
        for k in range(n):
            r, w = arrs[k].shape
            mine[offs[k]:offs[k] + r, 0:w] = ins[k][...]
        first = [copy(0, me, sibling)]
        first += [copy(1 + j, me, (*chip, c)) for j, chip in enumerate(chips)]
        for cp in first:
            cp.start()
        passed = [copy(4 + j, (*chip, c), sibling) for j, chip in enumerate(chips)]
        for j, chip in enumerate(chips):
            copy(1 + j, (*chip, c), me).wait_recv()
            passed[j].start()
        copy(0, sibling, me).wait_recv()
        for j, chip in enumerate(chips):
            copy(4 + j, (*chip, 1 - c), me).wait_recv()
        for cp in first + passed:
            cp.wait_send()
        acc = gath[0]
        for d in range(1, 8):
            acc = acc + gath[d]
        for k in range(n):
            r, w = arrs[k].shape
            outs[k][...] = acc[offs[k]:offs[k] + r, 0:w]

    vm = pl.BlockSpec(memory_space=pltpu.VMEM)
    return pl.pallas_call(
        body, name="allreduce_small", in_specs=[vm] * n, out_specs=[vm] * n,
        out_shape=[jax.ShapeDtypeStruct(a.shape, F32) for a in arrs],
        scratch_shapes=[pltpu.VMEM((8, rows, width), F32), pltpu.SemaphoreType.DMA((7,)),
                        pltpu.SemaphoreType.DMA((7,))],
        compiler_params=pltpu.CompilerParams(vmem_limit_bytes=VMEM_LIMIT),
    )(*arrs)


def _adamw_small_call(ws, gs, ms, vs):
    n = len(ws)
    c1 = 1.0 - ADAM_B1 ** ADAM_STEP
    c2 = 1.0 - ADAM_B2 ** ADAM_STEP

    def body(*refs):
        for k in range(n):
            w_ref, g_ref, m_ref, v_ref = (refs[j * n + k] for j in range(4))
            d_ref, mo_ref, vo_ref = (refs[(4 + j) * n + k] for j in range(3))
            gv = g_ref[...]
            m2 = ADAM_B1 * m_ref[...] + (1.0 - ADAM_B1) * gv
            v2 = ADAM_B2 * v_ref[...] + (1.0 - ADAM_B2) * (gv * gv)
            d_ref[...] = -ADAM_LR * ((m2 / c1) / (jnp.sqrt(v2 / c2) + ADAM_EPS) + ADAM_WD * w_ref[...])
            mo_ref[...] = m2
            vo_ref[...] = v2

    vm = pl.BlockSpec(memory_space=pltpu.VMEM)
    res = pl.pallas_call(
        body, name="adamw_small", in_specs=[vm] * (4 * n), out_specs=[vm] * (3 * n),
        out_shape=[jax.ShapeDtypeStruct(a.shape, F32) for a in ws] * 3,
        compiler_params=pltpu.CompilerParams(vmem_limit_bytes=VMEM_LIMIT),
    )(*ws, *gs, *ms, *vs)
    return res[:n], res[n:2 * n], res[2 * n:]


def _pack(arrs, min_rows=8):
    parts = []
    for a in arrs:
        flat = a.reshape(-1).astype(F32)
        parts.append(jnp.pad(flat, (0, (-flat.shape[0]) % BLK)))
    buf = jnp.concatenate(parts).reshape(-1, BLK)
    return jnp.pad(buf, ((0, (-buf.shape[0]) % min_rows), (0, 0)))


def _unpack(buf, shapes):
    out, r = [], 0
    for shp in shapes:
        n = math.prod(shp)
        nr = -(-n // BLK)
        out.append(buf[r:r + nr].reshape(-1)[:n].reshape(shp))
        r += nr
    return out


SMALL = ["meta_tokens", "mix_pre_g", "ssd_conv_w", "ssd_conv_b", "ssd_dt_bias", "ssd_a_log", "ssd_d", "ssd_norm_g",
         "sb_norm_g", "mix_post_g", "ffn_pre_g", "ffn_conv_w", "ffn_conv_b", "ffn_post_g"]
BIG = ["w_in", "w_out", "w_up", "w_down"]
WEIGHTS = ["meta_tokens", "mix_pre_g", "w_in", "ssd_conv_w", "ssd_conv_b", "ssd_dt_bias", "ssd_a_log", "ssd_d",
           "ssd_norm_g", "sb_norm_g", "w_out", "mix_post_g", "ffn_pre_g", "w_up", "ffn_conv_w", "ffn_conv_b",
           "w_down", "ffn_post_g"]
W_IN_SHARD = IN_COLS // N_CHIPS
W_IN_PAD = 1536


def kernel(x, meta_tokens, mix_pre_g, w_in, ssd_conv_w, ssd_conv_b, ssd_dt_bias, ssd_a_log, ssd_d, ssd_norm_g, sb_norm_g, w_out, mix_post_g, ffn_pre_g, w_up, ffn_conv_w, ffn_conv_b, w_down, ffn_post_g, loss_target, m_meta_tokens, m_mix_pre_g, m_w_in, m_ssd_conv_w, m_ssd_conv_b, m_ssd_dt_bias, m_ssd_a_log, m_ssd_d, m_ssd_norm_g, m_sb_norm_g, m_w_out, m_mix_post_g, m_ffn_pre_g, m_w_up, m_ffn_conv_w, m_ffn_conv_b, m_w_down, m_ffn_post_g, v_meta_tokens, v_mix_pre_g, v_w_in, v_ssd_conv_w, v_ssd_conv_b, v_ssd_dt_bias, v_ssd_a_log, v_ssd_d, v_ssd_norm_g, v_sb_norm_g, v_w_out, v_mix_post_g, v_ffn_pre_g, v_w_up, v_ffn_conv_w, v_ffn_conv_b, v_w_down, v_ffn_post_g):
    w = dict(meta_tokens=meta_tokens, mix_pre_g=mix_pre_g, w_in=w_in, ssd_conv_w=ssd_conv_w, ssd_conv_b=ssd_conv_b, ssd_dt_bias=ssd_dt_bias, ssd_a_log=ssd_a_log, ssd_d=ssd_d, ssd_norm_g=ssd_norm_g, sb_norm_g=sb_norm_g, w_out=w_out, mix_post_g=mix_post_g, ffn_pre_g=ffn_pre_g, w_up=w_up, ffn_conv_w=ffn_conv_w, ffn_conv_b=ffn_conv_b, w_down=w_down, ffn_post_g=ffn_post_g)
    m = dict(meta_tokens=m_meta_tokens, mix_pre_g=m_mix_pre_g, w_in=m_w_in, ssd_conv_w=m_ssd_conv_w, ssd_conv_b=m_ssd_conv_b, ssd_dt_bias=m_ssd_dt_bias, ssd_a_log=m_ssd_a_log, ssd_d=m_ssd_d, ssd_norm_g=m_ssd_norm_g, sb_norm_g=m_sb_norm_g, w_out=m_w_out, mix_post_g=m_mix_post_g, ffn_pre_g=m_ffn_pre_g, w_up=m_w_up, ffn_conv_w=m_ffn_conv_w, ffn_conv_b=m_ffn_conv_b, w_down=m_w_down, ffn_post_g=m_ffn_post_g)
    v = dict(meta_tokens=v_meta_tokens, mix_pre_g=v_mix_pre_g, w_in=v_w_in, ssd_conv_w=v_ssd_conv_w, ssd_conv_b=v_ssd_conv_b, ssd_dt_bias=v_ssd_dt_bias, ssd_a_log=v_ssd_a_log, ssd_d=v_ssd_d, ssd_norm_g=v_ssd_norm_g, sb_norm_g=v_sb_norm_g, w_out=v_w_out, mix_post_g=v_mix_post_g, ffn_pre_g=v_ffn_pre_g, w_up=v_w_up, ffn_conv_w=v_ffn_conv_w, ffn_conv_b=v_ffn_conv_b, w_down=v_w_down, ffn_post_g=v_ffn_post_g)
    chip = 2 * lax.axis_index("x") + lax.axis_index("y")
    me = 2 * chip + lax.axis_index("c")
    place = jnp.stack([lax.axis_index("c"), chip] + [me ^ p for p in range(1, 8)]).astype(jnp.int32)

    shard_small = [w["meta_tokens"], w["ssd_conv_w"][0], w["ffn_conv_w"][0]]
    shards = [jnp.pad(w["w_in"][0], ((0, 0), (0, W_IN_PAD - W_IN_SHARD))).astype(BF16), _pack(shard_small, 16)]
    gathered = _allgather_call(shards)
    late_shards = [w["w_out"][0].astype(BF16), w["w_up"][0].astype(BF16), w["w_down"][0].astype(BF16)]

    def blocks(own, got):
        return [jnp.where(chip == i, own, got[i]) for i in range(N_CHIPS)]

    def late_weights(got):
        return (jnp.concatenate(blocks(late_shards[0], got[0]), axis=0),
                jnp.concatenate(blocks(late_shards[1], got[1]), axis=1),
                jnp.concatenate(blocks(late_shards[2], got[2]), axis=0))

    cut = DT_REAL_OFF + N_HEADS - W_IN_SHARD
    s_in = blocks(shards[0], gathered[0])
    w_in_c = jnp.concatenate(
        [s_in[0][:, :W_IN_SHARD], s_in[1][:, :cut], jnp.zeros((D_MODEL, BLK - N_HEADS), BF16),
         s_in[1][:, cut:W_IN_SHARD], s_in[2][:, :W_IN_SHARD], s_in[3][:, :W_IN_SHARD]], axis=1)
    parts = [_unpack(b, [s.shape for s in shard_small]) for b in blocks(shards[1], gathered[1])]
    wt = {k: w[k][0][None] if w[k].ndim == 3 else w[k] for k in
          ["mix_pre_g", "ssd_conv_b", "ssd_dt_bias", "ssd_a_log", "ssd_d", "ssd_norm_g", "sb_norm_g", "mix_post_g",
           "ffn_pre_g", "ffn_conv_b", "ffn_post_g"]}
    wt.update(
        meta=jnp.concatenate([p[0] for p in parts], axis=1),
        ssd_conv_w=jnp.concatenate([p[1] for p in parts], axis=1),
        ffn_conv_w=jnp.concatenate([p[2] for p in parts], axis=1), w_in=w_in_c)

    def w_in_shards(g):
        skip = BLK - N_HEADS
        cols = [g[:, :W_IN_SHARD],
                jnp.concatenate([g[:, W_IN_SHARD:W_IN_SHARD + cut], g[:, C_Q:2 * W_IN_SHARD + skip]], axis=1),
                g[:, 2 * W_IN_SHARD + skip:3 * W_IN_SHARD + skip], g[:, 3 * W_IN_SHARD + skip:]]
        g = jnp.stack([jnp.pad(b, ((0, 0), (0, W_IN_PAD - W_IN_SHARD))) for b in cols])
        return g, g.astype(BF16)

    loss_row, dx, small, pending = _local_step(x[0], loss_target[0], wt, late_shards, late_weights, w_in_shards)

    full = _half_exchange_call([_grad_sum_call(*pending[k], place, "grad_sum_" + k) for k in BIG])
    grads = {"w_in": full[0][:, :W_IN_SHARD], "w_out": full[1], "w_up": full[2], "w_down": full[3]}

    red_list = _allreduce_small_call([small[k] for k in SMALL] + [loss_row])
    loss = jnp.sum(red_list[-1])
    for k, g in zip(SMALL, red_list[:-1]):
        grads[k] = g
    for k in ["meta_tokens", "ssd_conv_w", "ffn_conv_w"]:
        wk = w[k].shape[-1]
        grads[k] = lax.dynamic_slice_in_dim(grads[k], chip * wk, wk, axis=1)

    delta, new_m, new_v = {}, {}, {}
    for k in BIG:
        delta[k], new_m[k], new_v[k] = _adamw_call(w[k][0], grads[k], m[k][0], v[k][0], "adamw_" + k)
    flat = lambda d: [d[k].reshape(grads[k].shape) for k in SMALL]
    res = _adamw_small_call(flat(w), [grads[k] for k in SMALL], flat(m), flat(v))
    for out, arrs in zip((delta, new_m, new_v), res):
        for k, a in zip(SMALL, arrs):
            out[k] = a

    def shaped(d, k):
        return d[k].reshape(w[k].shape)

    return (loss, dx[None], *[shaped(grads, k) for k in WEIGHTS], *[shaped(delta, k) for k in WEIGHTS],
            *[shaped(new_m, k) for k in WEIGHTS], *[shaped(new_v, k) for k in WEIGHTS])
```
